```python
import math
import jax, jax.numpy as jnp
from jax import lax
import numpy as np

D_MODEL = 2048
BATCH = 8
SEQ = 8192
DEPTH = 2

CHUNK = 64
N_MIXERS = 2
N_RET_LAYERS = (DEPTH + 1) // 2
N_SSD_LAYERS = DEPTH // 2
RMS_EPS = 1e-6

RET_HEAD_DK = 256
RET_HEADS = D_MODEL // RET_HEAD_DK
RET_QK = RET_HEADS * RET_HEAD_DK
RET_HEAD_DV = 2 * RET_HEAD_DK
RET_V = RET_HEADS * RET_HEAD_DV
RET_IN = 2 * RET_QK + 2 * RET_V
ROPE_BASE = 10000.0
GN_EPS = 1e-5

SSD_EXPAND = 2
SSD_D_INNER = SSD_EXPAND * D_MODEL
SSD_HEADDIM = 64
SSD_HEADS = SSD_D_INNER // SSD_HEADDIM
SSD_GROUPS = 8
SSD_HEADS_PER_GROUP = SSD_HEADS // SSD_GROUPS
SSD_STATE = 128
SSD_CONV_W = 4
SSD_CONV_DIM = SSD_D_INNER + 2 * SSD_GROUPS * SSD_STATE
SSD_IN = SSD_D_INNER + SSD_CONV_DIM + SSD_HEADS
SSD_NORM_GROUPS = SSD_GROUPS

D_FF = 4 * D_MODEL

kernel_name = "hybrid_retention_ssd_sandwich_trunk"


def rms_norm(x, w):
    xf = x.astype(jnp.float32)
    y = xf * lax.rsqrt(jnp.mean(xf * xf, axis=-1, keepdims=True) + RMS_EPS)
    return (y * w.astype(jnp.float32)).astype(x.dtype)


def to_chunks(t):
    b, s = t.shape[:2]
    return jnp.moveaxis(t.reshape(b, s // CHUNK, CHUNK, *t.shape[2:]), 1, 0)


def from_chunks(t):
    t = jnp.moveaxis(t, 0, 1)
    return t.reshape(t.shape[0], t.shape[1] * t.shape[2], *t.shape[3:])


def rotary(t, pos):
    half = t.shape[-1] // 2
    inv_freq = ROPE_BASE ** (-jnp.arange(half, dtype=jnp.float32) / half)
    ang = pos.astype(jnp.float32)[:, None] * inv_freq[None, :]
    cos = jnp.cos(ang)[None, :, None, :]
    sin = jnp.sin(ang)[None, :, None, :]
    t1, t2 = t[..., :half], t[..., half:]
    return jnp.concatenate([t1 * cos - t2 * sin, t1 * sin + t2 * cos], axis=-1)


def retention_mixer(u, w_in, gn_w, w_out):
    b, s, _ = u.shape
    f32 = jnp.float32
    proj = u @ w_in
    q, k, v, g = jnp.split(proj, [RET_QK, 2 * RET_QK, 2 * RET_QK + RET_V], axis=-1)
    pos = jnp.arange(s)
    q = rotary(q.astype(f32).reshape(b, s, RET_HEADS, RET_HEAD_DK), pos)
    k = rotary(k.astype(f32).reshape(b, s, RET_HEADS, RET_HEAD_DK), pos) * (RET_HEAD_DK ** -0.5)
    v = v.astype(f32).reshape(b, s, RET_HEADS, RET_HEAD_DV)

    log_gamma = jnp.log1p(-jnp.exp2(-5.0 - jnp.arange(RET_HEADS, dtype=f32)))
    idx = jnp.arange(CHUNK, dtype=f32)
    dist = jnp.abs(idx[:, None] - idx[None, :])
    dmat = jnp.exp(dist[None] * log_gamma[:, None, None])
    xi = jnp.exp((idx[:, None] + 1.0) * log_gamma[None, :])
    zeta = jnp.exp((CHUNK - 1.0 - idx)[:, None] * log_gamma[None, :])
    chunk_decay = jnp.exp(CHUNK * log_gamma)

    def step(state, inp):
        qc, kc, vc = inp
        scores = jnp.einsum('blhd,bmhd->bhlm', qc, kc) * dmat[None]
        inner = jnp.einsum('bhlm,bmhv->blhv', scores, vc)
        cross = jnp.einsum('blhd,bhdv->blhv', qc, state) * xi[None, :, :, None]
        state = state * chunk_decay[None, :, None, None] + jnp.einsum(
            'bmhd,bmhv->bhdv', kc * zeta[None, :, :, None], vc)
        return state, inner + cross

    state0 = jnp.zeros((b, RET_HEADS, RET_HEAD_DK, RET_HEAD_DV), f32)
    _, o = lax.scan(step, state0, (to_chunks(q), to_chunks(k), to_chunks(v)))
    o = from_chunks(o)
    mu = jnp.mean(o, axis=-1, keepdims=True)
    var = jnp.mean(jnp.square(o - mu), axis=-1, keepdims=True)
    o = ((o - mu) * lax.rsqrt(var + GN_EPS)).reshape(b, s, RET_V) * gn_w.astype(f32)
    y = jax.nn.silu(g.astype(f32)) * o
    return y.astype(u.dtype) @ w_out


def causal_depthwise_conv(t, w, bias):
    s = t.shape[1]
    width = w.shape[0]
    tp = jnp.pad(t, ((0, 0), (width - 1, 0), (0, 0)))
    out = bias[None, None, :]
    for tap in range(width):
        out = out + tp[:, tap:tap + s, :] * w[tap][None, None, :]
    return out


def ssd_mixer(u, w_in, conv_w, conv_b, dt_bias, a_log, d_skip, norm_w, w_out):
    b, s, _ = u.shape
    f32 = jnp.float32
    G, R, P, N = SSD_GROUPS, SSD_HEADS_PER_GROUP, SSD_HEADDIM, SSD_STATE
    proj = u @ w_in
    z, xbc, dt = jnp.split(proj, [SSD_D_INNER, SSD_D_INNER + SSD_CONV_DIM], axis=-1)
    xbc = jax.nn.silu(causal_depthwise_conv(xbc, conv_w, conv_b))
    xs, bm, cm = jnp.split(xbc, [SSD_D_INNER, SSD_D_INNER + G * N], axis=-1)
    xs = xs.astype(f32).reshape(b, s, G, R, P)
    bm = bm.astype(f32).reshape(b, s, G, N)
    cm = cm.astype(f32).reshape(b, s, G, N)
    dt = jax.nn.softplus(dt.astype(f32) + dt_bias.astype(f32)).reshape(b, s, G, R)
    a = -jnp.exp(a_log.astype(f32)).reshape(G, R)
    da = dt * a[None, None]
    xdt = xs * dt[..., None]
    tril = jnp.tril(jnp.ones((CHUNK, CHUNK), dtype=bool))

    def step(state, inp):
        xc, ac, bc, cc = inp
        acum = jnp.cumsum(ac, axis=1)
        seg = acum[:, :, None] - acum[:, None, :]
        lmat = jnp.exp(jnp.where(tril[None, :, :, None, None], seg, -jnp.inf))
        cb = jnp.einsum('blgn,bsgn->blsg', cc, bc)
        y_diag = jnp.einsum('blsgr,bsgrp->blgrp', cb[..., None] * lmat, xc)
        y_off = jnp.einsum('blgn,bgrpn->blgrp', cc, state) * jnp.exp(acum)[..., None]
        decay_to_end = jnp.exp(acum[:, -1:] - acum)
        state = state * jnp.exp(acum[:, -1])[..., None, None] + jnp.einsum(
            'bsgn,bsgrp->bgrpn', bc, xc * decay_to_end[..., None])
        return state, y_diag + y_off

    state0 = jnp.zeros((b, G, R, P, N), f32)
    _, ys = lax.scan(step, state0, (to_chunks(xdt), to_chunks(da), to_chunks(bm), to_chunks(cm)))
    y = from_chunks(ys) + d_skip.astype(f32).reshape(G, R)[None, None, :, :, None] * xs
    y = y.reshape(b, s, SSD_D_INNER) * jax.nn.silu(z.astype(f32))
    yg = y.reshape(b, s, SSD_NORM_GROUPS, SSD_D_INNER // SSD_NORM_GROUPS)
    yg = yg * lax.rsqrt(jnp.mean(yg * yg, axis=-1, keepdims=True) + RMS_EPS)
    y = yg.reshape(b, s, SSD_D_INNER) * norm_w.astype(f32)
    return y.astype(u.dtype) @ w_out


def sq_relu_mlp(u, w_up, w_down):
    h = jax.nn.relu(u @ w_up)
    return (h * h) @ w_down


def _fwd_setup_inputs(seed: int = 0) -> dict:
    key = jax.random.key(seed)
    ks = jax.random.split(key, 20)
    f32 = jnp.float32

    def nrm(k, shape, scale):
        return jax.random.normal(k, shape, f32) * scale

    def gain(k, shape):
        return 1.0 + 0.05 * jax.random.normal(k, shape, f32)

    x = jax.random.normal(ks[0], (BATCH, SEQ, D_MODEL), f32)
    norm_mix_pre = gain(ks[1], (DEPTH, D_MODEL))
    norm_mix_post = gain(ks[2], (DEPTH, D_MODEL))
    norm_ffn_pre = gain(ks[3], (DEPTH, D_MODEL))
    norm_ffn_post = gain(ks[4], (DEPTH, D_MODEL))

    ret_w_in = nrm(ks[5], (N_RET_LAYERS, D_MODEL, RET_IN), D_MODEL ** -0.5)
    ret_gn_w = gain(ks[6], (N_RET_LAYERS, RET_V))
    ret_w_out = nrm(ks[7], (N_RET_LAYERS, RET_V, D_MODEL), RET_V ** -0.5)

    ssd_w_in = nrm(ks[8], (N_SSD_LAYERS, D_MODEL, SSD_IN), D_MODEL ** -0.5)
    ssd_conv_w = nrm(ks[9], (N_SSD_LAYERS, SSD_CONV_W, SSD_CONV_DIM), SSD_CONV_W ** -0.5)
    ssd_conv_b = nrm(ks[10], (N_SSD_LAYERS, SSD_CONV_DIM), 0.02)
    dt0 = jnp.exp(jax.random.uniform(ks[11], (N_SSD_LAYERS, SSD_HEADS), f32,
                                     math.log(1e-3), math.log(1e-1)))
    ssd_dt_bias = dt0 + jnp.log(-jnp.expm1(-dt0))
    ssd_a_log = jnp.log(jax.random.uniform(ks[12], (N_SSD_LAYERS, SSD_HEADS), f32, 1.0, 16.0))
    ssd_d = gain(ks[13], (N_SSD_LAYERS, SSD_HEADS))
    ssd_norm_w = gain(ks[14], (N_SSD_LAYERS, SSD_D_INNER))
    ssd_w_out = nrm(ks[15], (N_SSD_LAYERS, SSD_D_INNER, D_MODEL), SSD_D_INNER ** -0.5)

    mlp_w_up = nrm(ks[16], (DEPTH, D_MODEL, D_FF), D_MODEL ** -0.5)
    mlp_w_down = nrm(ks[17], (DEPTH, D_FF, D_MODEL), D_FF ** -0.5)

    return {"x": x,
            "norm_mix_pre": norm_mix_pre, "norm_mix_post": norm_mix_post,
            "norm_ffn_pre": norm_ffn_pre, "norm_ffn_post": norm_ffn_post,
            "ret_w_in": ret_w_in, "ret_gn_w": ret_gn_w, "ret_w_out": ret_w_out,
            "ssd_w_in": ssd_w_in, "ssd_conv_w": ssd_conv_w, "ssd_conv_b": ssd_conv_b,
            "ssd_dt_bias": ssd_dt_bias, "ssd_a_log": ssd_a_log, "ssd_d": ssd_d,
            "ssd_norm_w": ssd_norm_w, "ssd_w_out": ssd_w_out,
            "mlp_w_up": mlp_w_up, "mlp_w_down": mlp_w_down}


def _fwd_reference(x, norm_mix_pre, norm_mix_post, norm_ffn_pre, norm_ffn_post,
              ret_w_in, ret_gn_w, ret_w_out,
              ssd_w_in, ssd_conv_w, ssd_conv_b, ssd_dt_bias, ssd_a_log, ssd_d,
              ssd_norm_w, ssd_w_out, mlp_w_up, mlp_w_down):
    h = x
    for i in range(DEPTH):
        j = i // N_MIXERS
        u = rms_norm(h, norm_mix_pre[i])
        if i % N_MIXERS == 0:
            m = retention_mixer(u, ret_w_in[j], ret_gn_w[j], ret_w_out[j])
        else:
            m = ssd_mixer(u, ssd_w_in[j], ssd_conv_w[j], ssd_conv_b[j], ssd_dt_bias[j],
                          ssd_a_log[j], ssd_d[j], ssd_norm_w[j], ssd_w_out[j])
        h = h + rms_norm(m, norm_mix_post[i])
        u = rms_norm(h, norm_ffn_pre[i])
        h = h + rms_norm(sq_relu_mlp(u, mlp_w_up[i], mlp_w_down[i]), norm_ffn_post[i])
    return h


import jax as _jax
import jax.numpy as _jnp

TWIN_FORMAT = 'train_step'
FWD_PARAMS = ['x', 'norm_mix_pre', 'norm_mix_post', 'norm_ffn_pre', 'norm_ffn_post', 'ret_w_in', 'ret_gn_w', 'ret_w_out', 'ssd_w_in', 'ssd_conv_w', 'ssd_conv_b', 'ssd_dt_bias', 'ssd_a_log', 'ssd_d', 'ssd_norm_w', 'ssd_w_out', 'mlp_w_up', 'mlp_w_down']
TWIN_WEIGHTS = ['norm_mix_pre', 'norm_mix_post', 'norm_ffn_pre', 'norm_ffn_post', 'ret_w_in', 'ret_gn_w', 'ret_w_out', 'ssd_w_in', 'ssd_conv_w', 'ssd_conv_b', 'ssd_dt_bias', 'ssd_a_log', 'ssd_d', 'ssd_norm_w', 'ssd_w_out', 'mlp_w_up', 'mlp_w_down']
TWIN_DIFF_INPUT = 'x'
TWIN_INPUTS = ['x', 'norm_mix_pre', 'norm_mix_post', 'norm_ffn_pre', 'norm_ffn_post', 'ret_w_in', 'ret_gn_w', 'ret_w_out', 'ssd_w_in', 'ssd_conv_w', 'ssd_conv_b', 'ssd_dt_bias', 'ssd_a_log', 'ssd_d', 'ssd_norm_w', 'ssd_w_out', 'mlp_w_up', 'mlp_w_down', 'loss_target', 'm_norm_mix_pre', 'm_norm_mix_post', 'm_norm_ffn_pre', 'm_norm_ffn_post', 'm_ret_w_in', 'm_ret_gn_w', 'm_ret_w_out', 'm_ssd_w_in', 'm_ssd_conv_w', 'm_ssd_conv_b', 'm_ssd_dt_bias', 'm_ssd_a_log', 'm_ssd_d', 'm_ssd_norm_w', 'm_ssd_w_out', 'm_mlp_w_up', 'm_mlp_w_down', 'v_norm_mix_pre', 'v_norm_mix_post', 'v_norm_ffn_pre', 'v_norm_ffn_post', 'v_ret_w_in', 'v_ret_gn_w', 'v_ret_w_out', 'v_ssd_w_in', 'v_ssd_conv_w', 'v_ssd_conv_b', 'v_ssd_dt_bias', 'v_ssd_a_log', 'v_ssd_d', 'v_ssd_norm_w', 'v_ssd_w_out', 'v_mlp_w_up', 'v_mlp_w_down']
TWIN_OUTPUTS = ['loss', 'grad_x', 'grad_norm_mix_pre', 'grad_norm_mix_post', 'grad_norm_ffn_pre', 'grad_norm_ffn_post', 'grad_ret_w_in', 'grad_ret_gn_w', 'grad_ret_w_out', 'grad_ssd_w_in', 'grad_ssd_conv_w', 'grad_ssd_conv_b', 'grad_ssd_dt_bias', 'grad_ssd_a_log', 'grad_ssd_d', 'grad_ssd_norm_w', 'grad_ssd_w_out', 'grad_mlp_w_up', 'grad_mlp_w_down', 'delta_norm_mix_pre', 'delta_norm_mix_post', 'delta_norm_ffn_pre', 'delta_norm_ffn_post', 'delta_ret_w_in', 'delta_ret_gn_w', 'delta_ret_w_out', 'delta_ssd_w_in', 'delta_ssd_conv_w', 'delta_ssd_conv_b', 'delta_ssd_dt_bias', 'delta_ssd_a_log', 'delta_ssd_d', 'delta_ssd_norm_w', 'delta_ssd_w_out', 'delta_mlp_w_up', 'delta_mlp_w_down', 'new_m_norm_mix_pre', 'new_m_norm_mix_post', 'new_m_norm_ffn_pre', 'new_m_norm_ffn_post', 'new_m_ret_w_in', 'new_m_ret_gn_w', 'new_m_ret_w_out', 'new_m_ssd_w_in', 'new_m_ssd_conv_w', 'new_m_ssd_conv_b', 'new_m_ssd_dt_bias', 'new_m_ssd_a_log', 'new_m_ssd_d', 'new_m_ssd_norm_w', 'new_m_ssd_w_out', 'new_m_mlp_w_up', 'new_m_mlp_w_down', 'new_v_norm_mix_pre', 'new_v_norm_mix_post', 'new_v_norm_ffn_pre', 'new_v_norm_ffn_post', 'new_v_ret_w_in', 'new_v_ret_gn_w', 'new_v_ret_w_out', 'new_v_ssd_w_in', 'new_v_ssd_conv_w', 'new_v_ssd_conv_b', 'new_v_ssd_dt_bias', 'new_v_ssd_a_log', 'new_v_ssd_d', 'new_v_ssd_norm_w', 'new_v_ssd_w_out', 'new_v_mlp_w_up', 'new_v_mlp_w_down']
TWIN_LEAF_KINDS = {'loss': 'loss', 'grad_x': 'grad_x', 'grad_norm_mix_pre': 'grad_w', 'grad_norm_mix_post': 'grad_w', 'grad_norm_ffn_pre': 'grad_w', 'grad_norm_ffn_post': 'grad_w', 'grad_ret_w_in': 'grad_w', 'grad_ret_gn_w': 'grad_w', 'grad_ret_w_out': 'grad_w', 'grad_ssd_w_in': 'grad_w', 'grad_ssd_conv_w': 'grad_w', 'grad_ssd_conv_b': 'grad_w', 'grad_ssd_dt_bias': 'grad_w', 'grad_ssd_a_log': 'grad_w', 'grad_ssd_d': 'grad_w', 'grad_ssd_norm_w': 'grad_w', 'grad_ssd_w_out': 'grad_w', 'grad_mlp_w_up': 'grad_w', 'grad_mlp_w_down': 'grad_w', 'delta_norm_mix_pre': 'delta_w', 'delta_norm_mix_post': 'delta_w', 'delta_norm_ffn_pre': 'delta_w', 'delta_norm_ffn_post': 'delta_w', 'delta_ret_w_in': 'delta_w', 'delta_ret_gn_w': 'delta_w', 'delta_ret_w_out': 'delta_w', 'delta_ssd_w_in': 'delta_w', 'delta_ssd_conv_w': 'delta_w', 'delta_ssd_conv_b': 'delta_w', 'delta_ssd_dt_bias': 'delta_w', 'delta_ssd_a_log': 'delta_w', 'delta_ssd_d': 'delta_w', 'delta_ssd_norm_w': 'delta_w', 'delta_ssd_w_out': 'delta_w', 'delta_mlp_w_up': 'delta_w', 'delta_mlp_w_down': 'delta_w', 'new_m_norm_mix_pre': 'new_m', 'new_m_norm_mix_post': 'new_m', 'new_m_norm_ffn_pre': 'new_m', 'new_m_norm_ffn_post': 'new_m', 'new_m_ret_w_in': 'new_m', 'new_m_ret_gn_w': 'new_m', 'new_m_ret_w_out': 'new_m', 'new_m_ssd_w_in': 'new_m', 'new_m_ssd_conv_w': 'new_m', 'new_m_ssd_conv_b': 'new_m', 'new_m_ssd_dt_bias': 'new_m', 'new_m_ssd_a_log': 'new_m', 'new_m_ssd_d': 'new_m', 'new_m_ssd_norm_w': 'new_m', 'new_m_ssd_w_out': 'new_m', 'new_m_mlp_w_up': 'new_m', 'new_m_mlp_w_down': 'new_m', 'new_v_norm_mix_pre': 'new_v', 'new_v_norm_mix_post': 'new_v', 'new_v_norm_ffn_pre': 'new_v', 'new_v_norm_ffn_post': 'new_v', 'new_v_ret_w_in': 'new_v', 'new_v_ret_gn_w': 'new_v', 'new_v_ret_w_out': 'new_v', 'new_v_ssd_w_in': 'new_v', 'new_v_ssd_conv_w': 'new_v', 'new_v_ssd_conv_b': 'new_v', 'new_v_ssd_dt_bias': 'new_v', 'new_v_ssd_a_log': 'new_v', 'new_v_ssd_d': 'new_v', 'new_v_ssd_norm_w': 'new_v', 'new_v_ssd_w_out': 'new_v', 'new_v_mlp_w_up': 'new_v', 'new_v_mlp_w_down': 'new_v'}


def _forward(args):
    return _fwd_reference(*[args[k] for k in FWD_PARAMS])


def _output_shape():
    def fwd():
        inp = _fwd_setup_inputs(0)
        return _fwd_reference(*[inp[k] for k in FWD_PARAMS])
    out = _jax.eval_shape(fwd)
    return out.shape, out.dtype

N_MICROBATCH = 1
ADAM_LR = 0.001
ADAM_B1 = 0.9
ADAM_B2 = 0.999
ADAM_EPS = 1e-08
ADAM_WD = 0.01
ADAM_STEP = 10
PER_EXAMPLE_BATCH_AXIS = {'x': 0, 'loss_target': 0}
SHARED_INPUTS = []
_WEIGHT_DTYPES = {'norm_mix_pre': _jnp.float32, 'norm_mix_post': _jnp.float32, 'norm_ffn_pre': _jnp.float32, 'norm_ffn_post': _jnp.float32, 'ret_w_in': _jnp.float32, 'ret_gn_w': _jnp.float32, 'ret_w_out': _jnp.float32, 'ssd_w_in': _jnp.float32, 'ssd_conv_w': _jnp.float32, 'ssd_conv_b': _jnp.float32, 'ssd_dt_bias': _jnp.float32, 'ssd_a_log': _jnp.float32, 'ssd_d': _jnp.float32, 'ssd_norm_w': _jnp.float32, 'ssd_w_out': _jnp.float32, 'mlp_w_up': _jnp.float32, 'mlp_w_down': _jnp.float32}
MOMENT_SCALE = {'norm_mix_pre': 1.847504e+00, 'norm_mix_post': 3.216691e+01, 'norm_ffn_pre': 1.560305e+00, 'norm_ffn_post': 3.408270e+01, 'ret_w_in': 4.593508e-01, 'ret_gn_w': 4.990358e-01, 'ret_w_out': 6.312967e-01, 'ssd_w_in': 1.346247e+00, 'ssd_conv_w': 2.210550e+00, 'ssd_conv_b': 5.884572e+00, 'ssd_dt_bias': 8.190701e-01, 'ssd_a_log': 1.179114e+01, 'ssd_d': 1.215334e+01, 'ssd_norm_w': 3.491491e+00, 'ssd_w_out': 4.856161e+00, 'mlp_w_up': 7.808098e-01, 'mlp_w_down': 7.113133e+00}


def _to_microbatches(a, axis):
    t = _jnp.moveaxis(a, axis, 0)
    t = t.reshape((N_MICROBATCH, t.shape[0] // N_MICROBATCH) + t.shape[1:])
    return _jnp.moveaxis(t, 1, axis + 1)


def setup_inputs(seed: int = 0) -> dict:
    inp = _fwd_setup_inputs(seed)
    key = _jax.random.fold_in(_jax.random.key(seed), 7919)
    shape, _ = _output_shape()
    out = dict(inp)
    out["loss_target"] = _jax.random.normal(_jax.random.fold_in(key, 0), shape, _jnp.float32)
    for i, name in enumerate(TWIN_WEIGHTS):
        w = inp[name].astype(_jnp.float32)
        if MOMENT_SCALE is None:
            s = _jnp.sqrt(_jnp.mean(_jnp.square(w)) + 1e-30)
        else:
            s = MOMENT_SCALE[name]
        km, kv = _jax.random.split(_jax.random.fold_in(key, i + 1))
        out[name] = w
        out["m_" + name] = s * _jax.random.normal(km, w.shape, _jnp.float32)
        out["v_" + name] = (s * s) * _jax.random.uniform(kv, w.shape, _jnp.float32, 0.5, 1.5)
    if N_MICROBATCH > 1:
        for name, axis in PER_EXAMPLE_BATCH_AXIS.items():
            out[name] = _to_microbatches(out[name], axis)
    return {'x': out['x'], 'norm_mix_pre': out['norm_mix_pre'], 'norm_mix_post': out['norm_mix_post'], 'norm_ffn_pre': out['norm_ffn_pre'], 'norm_ffn_post': out['norm_ffn_post'], 'ret_w_in': out['ret_w_in'], 'ret_gn_w': out['ret_gn_w'], 'ret_w_out': out['ret_w_out'], 'ssd_w_in': out['ssd_w_in'], 'ssd_conv_w': out['ssd_conv_w'], 'ssd_conv_b': out['ssd_conv_b'], 'ssd_dt_bias': out['ssd_dt_bias'], 'ssd_a_log': out['ssd_a_log'], 'ssd_d': out['ssd_d'], 'ssd_norm_w': out['ssd_norm_w'], 'ssd_w_out': out['ssd_w_out'], 'mlp_w_up': out['mlp_w_up'], 'mlp_w_down': out['mlp_w_down'], 'loss_target': out['loss_target'], 'm_norm_mix_pre': out['m_norm_mix_pre'], 'm_norm_mix_post': out['m_norm_mix_post'], 'm_norm_ffn_pre': out['m_norm_ffn_pre'], 'm_norm_ffn_post': out['m_norm_ffn_post'], 'm_ret_w_in': out['m_ret_w_in'], 'm_ret_gn_w': out['m_ret_gn_w'], 'm_ret_w_out': out['m_ret_w_out'], 'm_ssd_w_in': out['m_ssd_w_in'], 'm_ssd_conv_w': out['m_ssd_conv_w'], 'm_ssd_conv_b': out['m_ssd_conv_b'], 'm_ssd_dt_bias': out['m_ssd_dt_bias'], 'm_ssd_a_log': out['m_ssd_a_log'], 'm_ssd_d': out['m_ssd_d'], 'm_ssd_norm_w': out['m_ssd_norm_w'], 'm_ssd_w_out': out['m_ssd_w_out'], 'm_mlp_w_up': out['m_mlp_w_up'], 'm_mlp_w_down': out['m_mlp_w_down'], 'v_norm_mix_pre': out['v_norm_mix_pre'], 'v_norm_mix_post': out['v_norm_mix_post'], 'v_norm_ffn_pre': out['v_norm_ffn_pre'], 'v_norm_ffn_post': out['v_norm_ffn_post'], 'v_ret_w_in': out['v_ret_w_in'], 'v_ret_gn_w': out['v_ret_gn_w'], 'v_ret_w_out': out['v_ret_w_out'], 'v_ssd_w_in': out['v_ssd_w_in'], 'v_ssd_conv_w': out['v_ssd_conv_w'], 'v_ssd_conv_b': out['v_ssd_conv_b'], 'v_ssd_dt_bias': out['v_ssd_dt_bias'], 'v_ssd_a_log': out['v_ssd_a_log'], 'v_ssd_d': out['v_ssd_d'], 'v_ssd_norm_w': out['v_ssd_norm_w'], 'v_ssd_w_out': out['v_ssd_w_out'], 'v_mlp_w_up': out['v_mlp_w_up'], 'v_mlp_w_down': out['v_mlp_w_down']}


def _loss(weights, diff, rest, loss_target):
    with _jax.named_scope("forward"):
        args = {**rest, TWIN_DIFF_INPUT: diff, **{k: w.astype(_WEIGHT_DTYPES[k]) for k, w in weights.items()}}
        y = _forward(args)
    with _jax.named_scope("loss_head"):
        err = _jnp.square(y.astype(_jnp.float32) - loss_target)
        return 0.5 * _jnp.sum(_jnp.mean(err, axis=-1)) if err.ndim else 0.5 * err


def _adamw(w, g, m, v):
    m = ADAM_B1 * m + (1.0 - ADAM_B1) * g
    v = ADAM_B2 * v + (1.0 - ADAM_B2) * _jnp.square(g)
    m_hat = m / (1.0 - ADAM_B1 ** ADAM_STEP)
    v_hat = v / (1.0 - ADAM_B2 ** ADAM_STEP)
    delta = -ADAM_LR * (m_hat / (_jnp.sqrt(v_hat) + ADAM_EPS) + ADAM_WD * w)
    return delta, m, v


def reference(x, norm_mix_pre, norm_mix_post, norm_ffn_pre, norm_ffn_post, ret_w_in, ret_gn_w, ret_w_out, ssd_w_in, ssd_conv_w, ssd_conv_b, ssd_dt_bias, ssd_a_log, ssd_d, ssd_norm_w, ssd_w_out, mlp_w_up, mlp_w_down, loss_target, m_norm_mix_pre, m_norm_mix_post, m_norm_ffn_pre, m_norm_ffn_post, m_ret_w_in, m_ret_gn_w, m_ret_w_out, m_ssd_w_in, m_ssd_conv_w, m_ssd_conv_b, m_ssd_dt_bias, m_ssd_a_log, m_ssd_d, m_ssd_norm_w, m_ssd_w_out, m_mlp_w_up, m_mlp_w_down, v_norm_mix_pre, v_norm_mix_post, v_norm_ffn_pre, v_norm_ffn_post, v_ret_w_in, v_ret_gn_w, v_ret_w_out, v_ssd_w_in, v_ssd_conv_w, v_ssd_conv_b, v_ssd_dt_bias, v_ssd_a_log, v_ssd_d, v_ssd_norm_w, v_ssd_w_out, v_mlp_w_up, v_mlp_w_down):
    given = dict(x=x, norm_mix_pre=norm_mix_pre, norm_mix_post=norm_mix_post, norm_ffn_pre=norm_ffn_pre, norm_ffn_post=norm_ffn_post, ret_w_in=ret_w_in, ret_gn_w=ret_gn_w, ret_w_out=ret_w_out, ssd_w_in=ssd_w_in, ssd_conv_w=ssd_conv_w, ssd_conv_b=ssd_conv_b, ssd_dt_bias=ssd_dt_bias, ssd_a_log=ssd_a_log, ssd_d=ssd_d, ssd_norm_w=ssd_norm_w, ssd_w_out=ssd_w_out, mlp_w_up=mlp_w_up, mlp_w_down=mlp_w_down, loss_target=loss_target, m_norm_mix_pre=m_norm_mix_pre, m_norm_mix_post=m_norm_mix_post, m_norm_ffn_pre=m_norm_ffn_pre, m_norm_ffn_post=m_norm_ffn_post, m_ret_w_in=m_ret_w_in, m_ret_gn_w=m_ret_gn_w, m_ret_w_out=m_ret_w_out, m_ssd_w_in=m_ssd_w_in, m_ssd_conv_w=m_ssd_conv_w, m_ssd_conv_b=m_ssd_conv_b, m_ssd_dt_bias=m_ssd_dt_bias, m_ssd_a_log=m_ssd_a_log, m_ssd_d=m_ssd_d, m_ssd_norm_w=m_ssd_norm_w, m_ssd_w_out=m_ssd_w_out, m_mlp_w_up=m_mlp_w_up, m_mlp_w_down=m_mlp_w_down, v_norm_mix_pre=v_norm_mix_pre, v_norm_mix_post=v_norm_mix_post, v_norm_ffn_pre=v_norm_ffn_pre, v_norm_ffn_post=v_norm_ffn_post, v_ret_w_in=v_ret_w_in, v_ret_gn_w=v_ret_gn_w, v_ret_w_out=v_ret_w_out, v_ssd_w_in=v_ssd_w_in, v_ssd_conv_w=v_ssd_conv_w, v_ssd_conv_b=v_ssd_conv_b, v_ssd_dt_bias=v_ssd_dt_bias, v_ssd_a_log=v_ssd_a_log, v_ssd_d=v_ssd_d, v_ssd_norm_w=v_ssd_norm_w, v_ssd_w_out=v_ssd_w_out, v_mlp_w_up=v_mlp_w_up, v_mlp_w_down=v_mlp_w_down)
    weights = {n: given[n] for n in TWIN_WEIGHTS}
    shared = {n: given[n] for n in SHARED_INPUTS}
    per_example = {n: given[n] for n in ['x']}
    grad_fn = _jax.value_and_grad(_loss, argnums=(0, 1))

    def one_microbatch(ex, loss_target):
        ex = dict(ex)
        diff = ex.pop(TWIN_DIFF_INPUT)
        return grad_fn(weights, diff, {**shared, **ex}, loss_target)

    if N_MICROBATCH == 1:
        loss, (grad_w, grad_x) = one_microbatch(per_example, given["loss_target"])
    else:
        def body(carry, xs):
            loss_sum, grad_sum = carry
            l_k, (gw_k, gx_k) = one_microbatch(xs[0], xs[1])
            with _jax.named_scope("update"):
                return (loss_sum + l_k, _jax.tree.map(_jnp.add, grad_sum, gw_k)), gx_k

        init = (_jnp.zeros((), _jnp.float32), _jax.tree.map(_jnp.zeros_like, weights))
        (loss, grad_w), grad_x = _jax.lax.scan(body, init, (per_example, given["loss_target"]))
    with _jax.named_scope("update"):
        delta_w, new_m, new_v = {}, {}, {}
        for n in TWIN_WEIGHTS:
            delta_w[n], new_m[n], new_v[n] = _adamw(weights[n], grad_w[n], given["m_" + n], given["v_" + n])
    return (loss, grad_x, *[grad_w[n] for n in TWIN_WEIGHTS], *[delta_w[n] for n in TWIN_WEIGHTS],
            *[new_m[n] for n in TWIN_WEIGHTS], *[new_v[n] for n in TWIN_WEIGHTS])
```

```python
import functools
import math

import numpy as np
import jax
import jax.numpy as jnp
from jax import lax
from jax.experimental import pallas as pl
from jax.experimental.pallas import tpu as pltpu

F32 = jnp.float32
BF16 = jnp.bfloat16
HI = lax.Precision.HIGHEST
VMEM_LIMIT_BYTES = 56 * 1024 * 1024
MESH_AXES = ("x", "y", "c")
MESH_ID = pl.DeviceIdType.MESH

RMS_EPS = 1e-6
GN_EPS = 1e-5
RET_DK = 256
RET_DV = 512
ROPE_BASE = 10000.0
REF_CHUNK = 64
SSD_P = 64
SSD_N = 128
SSD_G = 8
SSD_CONV_W = 4
ADAM_LR, ADAM_B1, ADAM_B2, ADAM_EPS, ADAM_WD, ADAM_STEP = 0.001, 0.9, 0.999, 1e-08, 0.01, 10

NN = (((1,), (0,)), ((), ()))
NT = (((1,), (1,)), ((), ()))
TN = (((0,), (0,)), ((), ()))


def _cparams(*sem):
    return pltpu.CompilerParams(dimension_semantics=sem, vmem_limit_bytes=VMEM_LIMIT_BYTES)


def _dot(a, b, dims=NN):
    return lax.dot_general(a.astype(BF16), b.astype(BF16), dims, preferred_element_type=F32)


def _dot_hi(a, b, dims=NN):
    return lax.dot_general(a.astype(F32), b.astype(F32), dims, precision=HI, preferred_element_type=F32)


def _sigmoid(x):
    return 1.0 / (1.0 + jnp.exp(-x))


def _colsum(x):
    return jnp.sum(x, axis=0, keepdims=True)


def _matmul(a, b, mode, *, out_dtype, name, epi=None, extra=None, tm=512, tn=1024, tk=512):
    if mode == "nn":
        (M, K), (K2, N) = a.shape, b.shape
    elif mode == "nt":
        (M, K), (N, K2) = a.shape, b.shape
    else:
        (K, M), (K2, N) = a.shape, b.shape
    assert K == K2, (a.shape, b.shape, mode)
    tm, tn, tk = min(tm, M), min(tn, N), min(tk, K)
    assert M % tm == 0 and N % tn == 0 and K % tk == 0, (M, N, K, tm, tn, tk)
    nk = K // tk
    if mode == "tn":
        a_spec = pl.BlockSpec((tk, tm), lambda i, j, k: (k, i))
    else:
        a_spec = pl.BlockSpec((tm, tk), lambda i, j, k: (i, k))
    if mode == "nt":
        b_spec = pl.BlockSpec((tn, tk), lambda i, j, k: (j, k))
    else:
        b_spec = pl.BlockSpec((tk, tn), lambda i, j, k: (k, j))
    dims = {"nn": NN, "nt": NT, "tn": TN}[mode]
    o_spec = pl.BlockSpec((tm, tn), lambda i, j, k: (i, j))
    has_extra = epi in ("drelu2", "add")
    n_out = 2 if epi == "relu2" else 1

    def body(*refs):
        a_ref, b_ref = refs[0], refs[1]
        e_ref = refs[2] if has_extra else None
        outs = refs[2 + int(has_extra):2 + int(has_extra) + n_out]
        acc = refs[-1]
        k = pl.program_id(2)

        @pl.when(k == 0)
        def _():
            acc[...] = jnp.zeros_like(acc)

        acc[...] += _dot(a_ref[...], b_ref[...], dims)

        @pl.when(k == nk - 1)
        def _():
            r = acc[...]
            if epi is None:
                outs[0][...] = r.astype(outs[0].dtype)
            elif epi == "relu2":
                outs[0][...] = r.astype(outs[0].dtype)
                h = jnp.maximum(r, 0.0)
                outs[1][...] = (h * h).astype(outs[1].dtype)
            elif epi == "drelu2":
                av = jnp.maximum(e_ref[...].astype(F32), 0.0)
                outs[0][...] = (r * (2.0 * av)).astype(outs[0].dtype)
            else:
                outs[0][...] = (r + e_ref[...].astype(F32)).astype(outs[0].dtype)

    in_specs = [a_spec, b_spec] + ([o_spec] if has_extra else [])
    args = [a, b] + ([extra] if has_extra else [])
    out_shape = [jax.ShapeDtypeStruct((M, N), out_dtype)] * n_out
    res = pl.pallas_call(
        body, name=name, grid=(M // tm, N // tn, nk), in_specs=in_specs, out_specs=[o_spec] * n_out,
        out_shape=out_shape, scratch_shapes=[pltpu.VMEM((tm, tn), F32)],
        compiler_params=_cparams("parallel", "parallel", "arbitrary"),
    )(*args)
    return res if n_out == 2 else res[0]


def _rstd(x):
    return lax.rsqrt(jnp.mean(x * x, axis=-1, keepdims=True) + RMS_EPS)


def _row_call(body, ins, outs_shape, *, name, rows, tr, acc_outs=()):
    tr = min(tr, rows)
    assert rows % tr == 0
    in_specs = []
    for arr, blocked in ins:
        if blocked:
            in_specs.append(pl.BlockSpec((tr, arr.shape[1]), lambda i: (i, 0)))
        else:
            in_specs.append(pl.BlockSpec(arr.shape, lambda i: (0, 0)))
    out_specs = []
    for n, s in enumerate(outs_shape):
        if n in acc_outs:
            out_specs.append(pl.BlockSpec(s.shape, lambda i: (0, 0)))
        else:
            out_specs.append(pl.BlockSpec((tr, s.shape[1]), lambda i: (i, 0)))
    return pl.pallas_call(
        body, name=name, grid=(rows // tr,), in_specs=in_specs, out_specs=out_specs, out_shape=outs_shape,
        compiler_params=_cparams("arbitrary" if acc_outs else "parallel"),
    )(*[a for a, _ in ins])


def _rms_pre(h, w, *, name):
    T, D = h.shape

    def body(h_ref, w_ref, u_ref):
        x = h_ref[...]
        u_ref[...] = (x * _rstd(x) * w_ref[...]).astype(BF16)

    return _row_call(body, [(h, True), (w, False)], [jax.ShapeDtypeStruct((T, D), BF16)], name=name, rows=T, tr=256)[0]


def _rms_post_pre(h, m, w_post, w_pre, *, name):
    T, D = h.shape

    def body(h_ref, m_ref, wp_ref, wn_ref, hn_ref, u_ref):
        mm = m_ref[...]
        hn = h_ref[...] + mm * _rstd(mm) * wp_ref[...]
        hn_ref[...] = hn
        u_ref[...] = (hn * _rstd(hn) * wn_ref[...]).astype(BF16)

    return _row_call(body, [(h, True), (m, True), (w_post, False), (w_pre, False)],
                     [jax.ShapeDtypeStruct((T, D), F32), jax.ShapeDtypeStruct((T, D), BF16)], name=name, rows=T, tr=256)


def _rms_post_loss(h, m, w_post, tgt, *, name):
    T, D = h.shape

    def body(h_ref, m_ref, wp_ref, t_ref, dh_ref, loss_ref):
        @pl.when(pl.program_id(0) == 0)
        def _():
            loss_ref[...] = jnp.zeros_like(loss_ref)

        mm = m_ref[...]
        err = h_ref[...] + mm * _rstd(mm) * wp_ref[...] - t_ref[...]
        dh_ref[...] = err * (1.0 / D)
        loss_ref[...] += _colsum(jnp.sum(err * err, axis=1, keepdims=True))

    return _row_call(body, [(h, True), (m, True), (w_post, False), (tgt, True)],
                     [jax.ShapeDtypeStruct((T, D), F32), jax.ShapeDtypeStruct((1, 1), F32)],
                     name=name, rows=T, tr=256, acc_outs=(1,))


def _rms_bwd_vals(x, w, dy):
    r = _rstd(x)
    xh = x * r
    g = dy * w
    dx = r * (g - xh * jnp.mean(g * xh, axis=-1, keepdims=True))
    return dx, _colsum(dy * xh)


def _rms_post_bwd(m, w_post, dh, *, name):
    T, D = m.shape

    def body(m_ref, w_ref, dh_ref, dm_ref, dw_ref):
        @pl.when(pl.program_id(0) == 0)
        def _():
            dw_ref[...] = jnp.zeros_like(dw_ref)

        dx, dw = _rms_bwd_vals(m_ref[...], w_ref[...], dh_ref[...])
        dm_ref[...] = dx.astype(BF16)
        dw_ref[...] += dw

    return _row_call(body, [(m, True), (w_post, False), (dh, True)],
                     [jax.ShapeDtypeStruct((T, D), BF16), jax.ShapeDtypeStruct((1, D), F32)],
                     name=name, rows=T, tr=256, acc_outs=(1,))


def _rms_pre_bwd(h, w_pre, du, dh_out, *, name):
    T, D = h.shape

    def body(h_ref, w_ref, du_ref, dho_ref, dh_ref, dw_ref):
        @pl.when(pl.program_id(0) == 0)
        def _():
            dw_ref[...] = jnp.zeros_like(dw_ref)

        dx, dw = _rms_bwd_vals(h_ref[...], w_ref[...], du_ref[...])
        dh_ref[...] = dho_ref[...] + dx
        dw_ref[...] += dw

    return _row_call(body, [(h, True), (w_pre, False), (du, True), (dh_out, True)],
                     [jax.ShapeDtypeStruct((T, D), F32), jax.ShapeDtypeStruct((1, D), F32)],
                     name=name, rows=T, tr=256, acc_outs=(1,))


def _ret_consts(T, C, H):
    lg = np.log1p(-np.exp2(-5.0 - np.arange(H, dtype=np.float64)))
    idx = np.arange(C, dtype=np.float64)
    dist = np.abs(idx[:, None] - idx[None, :])
    vis = (idx[None, :] // REF_CHUNK) <= (idx[:, None] // REF_CHUNK)
    mask = np.exp(dist[None] * lg[:, None, None]) * vis[None]
    xi = np.exp((idx[None, :] + 1.0) * lg[:, None])[..., None]
    zeta = np.exp((C - 1.0 - idx)[None, :] * lg[:, None])[..., None]
    half = RET_DK // 2
    inv_freq = ROPE_BASE ** (-np.arange(half, dtype=np.float32) / np.float32(half))
    ang = np.arange(T, dtype=np.float32)[:, None] * inv_freq[None, :].astype(np.float32)
    return (jnp.asarray(mask, F32), jnp.asarray(xi, F32), jnp.asarray(zeta, F32),
            jnp.asarray(np.cos(ang), F32), jnp.asarray(np.sin(ang), F32))


def _rot(t, cos, sin):
    half = RET_DK // 2
    t1, t2 = t[:, :half], t[:, half:]
    return jnp.concatenate([t1 * cos - t2 * sin, t1 * sin + t2 * cos], axis=1)


def _unrot(d, cos, sin):
    half = RET_DK // 2
    d1, d2 = d[:, :half], d[:, half:]
    return jnp.concatenate([d1 * cos + d2 * sin, d2 * cos - d1 * sin], axis=1)


def _ret_specs(C, H, rev, NS):
    def ci(i):
        return NS - 1 - i if rev else i

    nq = H
    q_spec = pl.BlockSpec((C, RET_DK), lambda h, i: (ci(i), h))
    k_spec = pl.BlockSpec((C, RET_DK), lambda h, i: (ci(i), nq + h))
    v_spec = pl.BlockSpec((C, RET_DV), lambda h, i: (ci(i), H + h))
    g_spec = pl.BlockSpec((C, RET_DV), lambda h, i: (ci(i), 2 * H + h))
    cs_spec = pl.BlockSpec((C, RET_DK // 2), lambda h, i: (ci(i), 0))
    m_spec = pl.BlockSpec((None, C, C), lambda h, i: (h, 0, 0))
    vec_spec = pl.BlockSpec((None, C, 1), lambda h, i: (h, 0, 0))
    gn_spec = pl.BlockSpec((1, RET_DV), lambda h, i: (0, h))
    st_spec = pl.BlockSpec((None, None, RET_DK, RET_DV), lambda h, i: (h, ci(i), 0, 0))
    return q_spec, k_spec, v_spec, g_spec, cs_spec, m_spec, vec_spec, gn_spec, st_spec


def _ret_fwd_vals(q, k, v, cos, sin, mask, xi, s_in):
    qr = _rot(q, cos, sin)
    kr = _rot(k, cos, sin) * (RET_DK ** -0.5)
    a = _dot(qr, kr, NT) * mask
    o = _dot(a, v) + _dot(qr, s_in) * xi
    mu = jnp.mean(o, axis=1, keepdims=True)
    oc = o - mu
    rstd = lax.rsqrt(jnp.mean(oc * oc, axis=1, keepdims=True) + GN_EPS)
    return qr, kr, a, oc * rstd, rstd


def _ret_fwd(proj, gn_w, consts, *, C, name):
    T = proj.shape[0]
    H = gn_w.shape[1] // RET_DV
    NS = T // C
    mask, xi, zeta, cos, sin = consts
    q_spec, k_spec, v_spec, g_spec, cs_spec, m_spec, vec_spec, gn_spec, st_spec = _ret_specs(C, H, False, NS)
    y_spec = pl.BlockSpec((C, RET_DV), lambda h, i: (i, h))

    def body(q_ref, k_ref, v_ref, g_ref, cos_ref, sin_ref, m_ref, xi_ref, ze_ref, gn_ref, y_ref, st_ref, S):
        @pl.when(pl.program_id(1) == 0)
        def _():
            S[...] = jnp.zeros_like(S)

        s_in = S[...]
        st_ref[...] = s_in
        v = v_ref[...]
        xi_v = xi_ref[...]
        qr, kr, a, on, rstd = _ret_fwd_vals(q_ref[...], k_ref[...], v, cos_ref[...], sin_ref[...], m_ref[...], xi_v, s_in)
        g = g_ref[...]
        y_ref[...] = (g * _sigmoid(g) * on * gn_ref[...]).astype(BF16)
        S[...] = s_in * xi_v[C - 1:C, :] + _dot(kr * ze_ref[...], v, TN)

    return pl.pallas_call(
        body, name=name, grid=(H, NS),
        in_specs=[q_spec, k_spec, v_spec, g_spec, cs_spec, cs_spec, m_spec, vec_spec, vec_spec, gn_spec],
        out_specs=[y_spec, st_spec],
        out_shape=[jax.ShapeDtypeStruct((T, H * RET_DV), BF16), jax.ShapeDtypeStruct((H, NS, RET_DK, RET_DV), F32)],
        scratch_shapes=[pltpu.VMEM((RET_DK, RET_DV), F32)],
        compiler_params=_cparams("parallel", "arbitrary"),
    )(proj, proj, proj, proj, cos, sin, mask, xi, zeta, gn_w)


def _ret_bwd(proj, gn_w, consts, states, dy, *, C, name):
    T = proj.shape[0]
    H = gn_w.shape[1] // RET_DV
    NS = T // C
    mask, xi, zeta, cos, sin = consts
    q_spec, k_spec, v_spec, g_spec, cs_spec, m_spec, vec_spec, gn_spec, st_spec = _ret_specs(C, H, True, NS)
    dy_spec = pl.BlockSpec((C, RET_DV), lambda h, i: (NS - 1 - i, h))
    dqk_spec = pl.BlockSpec((C, RET_DK), lambda h, i: (NS - 1 - i, h))
    scale = RET_DK ** -0.5

    def body(q_ref, k_ref, v_ref, g_ref, cos_ref, sin_ref, m_ref, xi_ref, ze_ref, gn_ref, st_ref, dy_ref,
             dq_ref, dk_ref, dv_ref, dg_ref, dgn_ref, dS):
        @pl.when(pl.program_id(1) == 0)
        def _():
            dS[...] = jnp.zeros_like(dS)
            dgn_ref[...] = jnp.zeros_like(dgn_ref)

        s_in = st_ref[...]
        v = v_ref[...]
        cos, sin, mask, xi_v, ze = cos_ref[...], sin_ref[...], m_ref[...], xi_ref[...], ze_ref[...]
        qr, kr, a, on, rstd = _ret_fwd_vals(q_ref[...], k_ref[...], v, cos, sin, mask, xi_v, s_in)
        g = g_ref[...]
        sg = _sigmoid(g)
        silu = g * sg
        gnw = gn_ref[...]
        dy = dy_ref[...].astype(F32)
        dg_ref[...] = (dy * on * gnw * (sg * (1.0 + g * (1.0 - sg)))).astype(BF16)
        t = dy * silu
        dgn_ref[...] += _colsum(t * on)
        don = t * gnw
        do = rstd * (don - jnp.mean(don, axis=1, keepdims=True) - on * jnp.mean(don * on, axis=1, keepdims=True))
        dox = do * xi_v
        ds_out = dS[...]
        da = _dot(do, v, NT) * mask
        kz = kr * ze
        dv_ref[...] = (_dot(a, do, TN) + _dot(kz, ds_out)).astype(BF16)
        dqr = _dot(da, kr) + _dot(dox, s_in, NT)
        dkr = _dot(da, qr, TN) + _dot(v, ds_out, NT) * ze
        dS[...] = ds_out * xi_v[C - 1:C, :] + _dot(qr, dox, TN)
        dq_ref[...] = _unrot(dqr, cos, sin).astype(BF16)
        dk_ref[...] = _unrot(dkr * scale, cos, sin).astype(BF16)

    return pl.pallas_call(
        body, name=name, grid=(H, NS),
        in_specs=[q_spec, k_spec, v_spec, g_spec, cs_spec, cs_spec, m_spec, vec_spec, vec_spec, gn_spec, st_spec, dy_spec],
        out_specs=[dqk_spec, dqk_spec, dy_spec, dy_spec, gn_spec],
        out_shape=[jax.ShapeDtypeStruct((T, H * RET_DK), BF16), jax.ShapeDtypeStruct((T, H * RET_DK), BF16),
                   jax.ShapeDtypeStruct((T, H * RET_DV), BF16), jax.ShapeDtypeStruct((T, H * RET_DV), BF16),
                   jax.ShapeDtypeStruct((1, H * RET_DV), F32)],
        scratch_shapes=[pltpu.VMEM((RET_DK, RET_DV), F32)],
        compiler_params=_cparams("parallel", "arbitrary"),
    )(proj, proj, proj, proj, cos, sin, mask, xi, zeta, gn_w, states, dy)


def _shift_down(x, prev8, k):
    if k == 0:
        return x
    y = pltpu.roll(x, k, 0)
    row = lax.broadcasted_iota(jnp.int32, prev8.shape, 0)
    top = jnp.where(row < k, pltpu.roll(prev8, k, 0), y[:8])
    return jnp.concatenate([top, y[8:]], axis=0)


def _shift_up(x, next8, k):
    if k == 0:
        return x
    n = x.shape[0]
    y = pltpu.roll(x, n - k, 0)
    row = lax.broadcasted_iota(jnp.int32, next8.shape, 0)
    bot = jnp.where(row >= 8 - k, pltpu.roll(next8, 8 - k, 0), y[n - 8:])
    return jnp.concatenate([y[:n - 8], bot], axis=0)


def _conv_silu(raw, halo, w, b):
    cv = b
    for tap in range(SSD_CONV_W):
        cv = cv + _shift_down(raw, halo, SSD_CONV_W - 1 - tap) * w[tap:tap + 1, :]
    sg = _sigmoid(cv)
    return cv * sg, cv, sg


def _conv_silu_bwd(d_post, cv, sg, raw, halo, w, carry8):
    dcv = d_post * (sg * (1.0 + cv * (1.0 - sg)))
    d_raw = jnp.zeros_like(raw)
    dws = []
    for tap in range(SSD_CONV_W):
        k = SSD_CONV_W - 1 - tap
        d_raw = d_raw + _shift_up(dcv, carry8, k) * w[tap:tap + 1, :]
        dws.append(_colsum(dcv * _shift_down(raw, halo, k)))
    return d_raw, jnp.concatenate(dws, axis=0), _colsum(dcv), dcv[:8]


def _softplus(x):
    return jnp.maximum(x, 0.0) + jnp.log1p(jnp.exp(-jnp.abs(x)))


def _ssd_common(C, R, dt, dtT, bias, biasT, alog, alogT, E):
    p = dt + bias
    dtv = _softplus(p)
    a = -jnp.exp(alog)
    da = dtv * a
    daT = _softplus(dtT + biasT) * (-jnp.exp(alogT))
    row = lax.broadcasted_iota(jnp.int32, (C, C), 0)
    col = lax.broadcasted_iota(jnp.int32, (C, C), 1)
    tril = row >= col
    trilf = jnp.where(tril, 1.0, 0.0).astype(F32)
    acum = _dot_hi(trilf, da)
    acumT = _dot_hi(daT, trilf, NT)
    al = acum[C - 1:C, :]
    ea = jnp.exp(acum)
    dte = jnp.exp(al - acum)
    eal = jnp.exp(al)
    return dict(p=p, dtv=dtv, a=a, da=da, tril=tril, trilf=trilf, acum=acum, acumT=acumT, al=al, ea=ea, dte=dte, eal=eal,
                dtv_e=_dot_hi(dtv, E), ea_e=_dot_hi(ea, E), dte_e=_dot_hi(dte, E), eal_e=_dot_hi(eal, E))


def _head_decay(q, r, C, R):
    sel = jnp.where(lax.broadcasted_iota(jnp.int32, (R, C), 0) == r, 1.0, 0.0).astype(F32)
    colb = _dot_hi(q["acum"], sel)
    seg = colb - q["acumT"][r:r + 1, :]
    return jnp.exp(jnp.where(q["tril"], seg, -1e30))


def _ssd_group_specs(C, R, NS, rev):
    RP = R * SSD_P
    G = SSD_G
    nz = 1
    hb = C // 8

    def ci(i):
        return NS - 1 - i if rev else i

    def halo_row(i):
        return jnp.maximum(ci(i) * hb - 1, 0)

    off_b = G * RP // SSD_N
    z_spec = pl.BlockSpec((C, RP), lambda g, i: (ci(i), g))
    x_spec = pl.BlockSpec((C, RP), lambda g, i: (ci(i), G + g))
    b_spec = pl.BlockSpec((C, SSD_N), lambda g, i: (ci(i), 2 * off_b + g))
    c_spec = pl.BlockSpec((C, SSD_N), lambda g, i: (ci(i), 2 * off_b + G + g))
    xh_spec = pl.BlockSpec((8, RP), lambda g, i: (halo_row(i), G + g))
    bh_spec = pl.BlockSpec((8, SSD_N), lambda g, i: (halo_row(i), 2 * off_b + g))
    ch_spec = pl.BlockSpec((8, SSD_N), lambda g, i: (halo_row(i), 2 * off_b + G + g))
    dt_spec = pl.BlockSpec((None, C, R), lambda g, i: (g, ci(i), 0))
    dtT_spec = pl.BlockSpec((None, R, C), lambda g, i: (g, 0, ci(i)))
    pr_spec = pl.BlockSpec((None, 1, R), lambda g, i: (g, 0, 0))
    prT_spec = pl.BlockSpec((None, R, 1), lambda g, i: (g, 0, 0))
    cwx_spec = pl.BlockSpec((SSD_CONV_W, RP), lambda g, i: (0, g))
    cwn_spec = pl.BlockSpec((SSD_CONV_W, SSD_N), lambda g, i: (0, g))
    cbx_spec = pl.BlockSpec((1, RP), lambda g, i: (0, g))
    cbn_spec = pl.BlockSpec((1, SSD_N), lambda g, i: (0, g))
    e_spec = pl.BlockSpec((R, RP), lambda g, i: (0, 0))
    st_spec = pl.BlockSpec((None, None, SSD_N, RP), lambda g, i: (g, ci(i), 0, 0))
    return dict(z=z_spec, x=x_spec, b=b_spec, c=c_spec, xh=xh_spec, bh=bh_spec, ch=ch_spec, dt=dt_spec, dtT=dtT_spec,
                pr=pr_spec, prT=prT_spec, cwx=cwx_spec, cwn=cwn_spec, cbx=cbx_spec, cbn=cbn_spec, e=e_spec, st=st_spec)


def _ssd_forward_vals(C, R, refs, first, s_in):
    E = refs["E"]
    halo_on = jnp.where(first, 0.0, 1.0)
    xh, bh, ch = refs["xh"] * halo_on, refs["bh"] * halo_on, refs["ch"] * halo_on
    xs, cvx, sgx = _conv_silu(refs["x"], xh, refs["cwx"], refs["cbx"])
    bm, cvb, sgb = _conv_silu(refs["b"], bh, refs["cwb"], refs["cbb"])
    cm, cvc, sgc = _conv_silu(refs["c"], ch, refs["cwc"], refs["cbc"])
    q = _ssd_common(C, R, refs["dt"], refs["dtT"], refs["bias"], refs["biasT"], refs["alog"], refs["alogT"], E)
    xdt = xs * q["dtv_e"]
    cb = _dot(cm, bm, NT)
    yoff_raw = _dot(cm, s_in)
    ydiag = jnp.zeros_like(xs)
    for r in range(R):
        w_r = cb * _head_decay(q, r, C, R)
        ydiag = ydiag + _dot(w_r, xdt * E[r:r + 1, :])
    d_e = _dot_hi(refs["dskip"], E)
    y = ydiag + yoff_raw * q["ea_e"] + d_e * xs
    xd = xdt * q["dte_e"]
    s_out = s_in * q["eal_e"] + _dot(bm, xd, TN)
    z = refs["z"]
    sgz = _sigmoid(z)
    yz = y * (z * sgz)
    rn = lax.rsqrt(jnp.mean(yz * yz, axis=1, keepdims=True) + RMS_EPS)
    return dict(q=q, xh=xh, bh=bh, ch=ch, xs=xs, cvx=cvx, sgx=sgx, bm=bm, cvb=cvb, sgb=sgb, cm=cm, cvc=cvc, sgc=sgc,
                xdt=xdt, cb=cb, yoff_raw=yoff_raw, d_e=d_e, y=y, xd=xd, s_out=s_out, z=z, sgz=sgz, yz=yz, rn=rn)


_SSD_IN_NAMES = ("z", "x", "b", "c", "xh", "bh", "ch", "dt", "dtT", "bias", "biasT", "alog", "alogT", "dskip",
                 "cwx", "cwb", "cwc", "cbx", "cbb", "cbc", "nw", "E")


def _ssd_inputs(pm, dt_g, dtT_g, prm, sp):
    bias, biasT, alog, alogT, dskip, cwx, cwb, cwc, cbx, cbb, cbc, nw, E = prm
    args = [pm, pm, pm, pm, pm, pm, pm, dt_g, dtT_g, bias, biasT, alog, alogT, dskip, cwx, cwb, cwc, cbx, cbb, cbc, nw, E]
    specs = [sp["z"], sp["x"], sp["b"], sp["c"], sp["xh"], sp["bh"], sp["ch"], sp["dt"], sp["dtT"], sp["pr"], sp["prT"],
             sp["pr"], sp["prT"], sp["pr"], sp["cwx"], sp["cwn"], sp["cwn"], sp["cbx"], sp["cbn"], sp["cbn"], sp["cbx"], sp["e"]]
    return args, specs


def _ssd_fwd(pm, dt_g, dtT_g, prm, *, C, R, name):
    T = pm.shape[0]
    NS = T // C
    RP = R * SSD_P
    G = SSD_G
    sp = _ssd_group_specs(C, R, NS, False)
    args, specs = _ssd_inputs(pm, dt_g, dtT_g, prm, sp)
    nin = len(args)

    def body(*refs):
        ins = {n: r[...] for n, r in zip(_SSD_IN_NAMES, refs[:nin])}
        y_ref, st_ref, S = refs[nin:]
        first = pl.program_id(1) == 0

        @pl.when(first)
        def _():
            S[...] = jnp.zeros_like(S)

        s_in = S[...]
        st_ref[...] = s_in
        f = _ssd_forward_vals(C, R, ins, first, s_in)
        y_ref[...] = (f["yz"] * f["rn"] * ins["nw"]).astype(BF16)
        S[...] = f["s_out"]

    return pl.pallas_call(
        body, name=name, grid=(G, NS), in_specs=specs,
        out_specs=[pl.BlockSpec((C, RP), lambda g, i: (i, g)), sp["st"]],
        out_shape=[jax.ShapeDtypeStruct((T, G * RP), BF16), jax.ShapeDtypeStruct((G, NS, SSD_N, RP), F32)],
        scratch_shapes=[pltpu.VMEM((SSD_N, RP), F32)],
        compiler_params=_cparams("parallel", "arbitrary"),
    )(*args)


def _ssd_bwd(pm, dt_g, dtT_g, prm, states, dout, *, C, R, name):
    T = pm.shape[0]
    NS = T // C
    RP = R * SSD_P
    G = SSD_G
    sp = _ssd_group_specs(C, R, NS, True)
    args, specs = _ssd_inputs(pm, dt_g, dtT_g, prm, sp)
    nin = len(args)
    rows_spec = pl.BlockSpec((C, RP), lambda g, i: (NS - 1 - i, g))
    rown_spec = pl.BlockSpec((C, SSD_N), lambda g, i: (NS - 1 - i, g))
    args = args + [states, dout]
    specs = specs + [sp["st"], rows_spec]

    def body(*refs):
        ins = {n: r[...] for n, r in zip(_SSD_IN_NAMES, refs[:nin])}
        st_ref, dout_ref = refs[nin], refs[nin + 1]
        (dz_ref, dx_ref, db_ref, dc_ref, ddt_ref, dbias_ref, dalog_ref, dd_ref, dcwx_ref, dcwb_ref, dcwc_ref,
         dcbx_ref, dcbb_ref, dcbc_ref, dnw_ref) = refs[nin + 2:nin + 17]
        dS, cx8, cb8, cc8 = refs[nin + 17:]
        acc_refs = (dbias_ref, dalog_ref, dd_ref, dcwx_ref, dcwb_ref, dcwc_ref, dcbx_ref, dcbb_ref, dcbc_ref, dnw_ref)
        step = pl.program_id(1)

        @pl.when(step == 0)
        def _():
            for r_ in acc_refs + (dS, cx8, cb8, cc8):
                r_[...] = jnp.zeros_like(r_)

        first = step == NS - 1
        E = ins["E"]
        s_in = st_ref[...]
        f = _ssd_forward_vals(C, R, ins, first, s_in)
        q = f["q"]
        xs, bm, cm, xdt, cb, y, z, sgz, yz, rn = (f[n] for n in ("xs", "bm", "cm", "xdt", "cb", "y", "z", "sgz", "yz", "rn"))
        nw = ins["nw"]
        dout = dout_ref[...].astype(F32)
        yh = yz * rn
        dnw_ref[...] += _colsum(dout * yh)
        g1 = dout * nw
        dyz = rn * (g1 - yh * jnp.mean(g1 * yh, axis=1, keepdims=True))
        dz_ref[...] = (dyz * y * (sgz * (1.0 + z * (1.0 - sgz)))).astype(BF16)
        dy = dyz * (z * sgz)
        dd_ref[...] += _dot_hi(_colsum(dy * xs), E, NT)
        dxs = dy * f["d_e"]
        dyo = dy * q["ea_e"]
        dcm = _dot(dyo, s_in, NT)
        ds_acc = _dot(cm, dyo, TN)
        dacum = _dot_hi(dy * f["yoff_raw"], E, NT) * q["ea"]
        dacumT = jnp.zeros((R, C), F32)
        dxdt = jnp.zeros_like(xs)
        dcb = jnp.zeros((C, C), F32)
        rowR = lax.broadcasted_iota(jnp.int32, (1, R), 1)
        rowRT = lax.broadcasted_iota(jnp.int32, (R, 1), 0)
        for r in range(R):
            mk = E[r:r + 1, :]
            lr = _head_decay(q, r, C, R)
            w_r = cb * lr
            dw = _dot(dy * mk, xdt, NT)
            dxdt = dxdt + _dot(w_r, dy, TN) * mk
            dcb = dcb + dw * lr
            dseg = dw * w_r
            dacum = dacum + jnp.sum(dseg, axis=1, keepdims=True) * jnp.where(rowR == r, 1.0, 0.0)
            dacumT = dacumT - _colsum(dseg) * jnp.where(rowRT == r, 1.0, 0.0)
        dsn = dS[...]
        ds_acc = ds_acc + dsn * q["eal_e"]
        d_eal = _dot_hi(_colsum(dsn * s_in), E, NT)
        dbm = _dot(f["xd"], dsn, NT)
        dxd = _dot(bm, dsn)
        dxdt = dxdt + dxd * q["dte_e"]
        d_dte = _dot_hi(dxd * xdt, E, NT) * q["dte"]
        d_al = _colsum(d_dte) + d_eal * q["eal"]
        dacum = dacum - d_dte
        rowC = lax.broadcasted_iota(jnp.int32, (C, 1), 0)
        dacum = dacum + jnp.where(rowC == C - 1, 1.0, 0.0) * d_al
        dS[...] = ds_acc
        dcm = dcm + _dot(dcb, bm)
        dbm = dbm + _dot(dcb, cm, TN)
        eye = jnp.where(lax.broadcasted_iota(jnp.int32, (C, C), 0) == lax.broadcasted_iota(jnp.int32, (C, C), 1), 1.0, 0.0)
        dacum = dacum + _dot_hi(eye, dacumT, NT)
        dda = _dot_hi(q["trilf"], dacum, TN)
        ddtv = dda * q["a"] + _dot_hi(dxdt * xs, E, NT)
        dalog_ref[...] += _colsum(dda * q["dtv"]) * q["a"]
        dxs = dxs + dxdt * q["dtv_e"]
        dp = ddtv * _sigmoid(q["p"])
        ddt_ref[...] = dp
        dbias_ref[...] += _colsum(dp)
        d_raw, d_w, d_b, c8 = _conv_silu_bwd(dxs, f["cvx"], f["sgx"], ins["x"], f["xh"], ins["cwx"], cx8[...])
        dx_ref[...] = d_raw.astype(BF16)
        dcwx_ref[...] += d_w
        dcbx_ref[...] += d_b
        cx8[...] = c8
        d_raw, d_w, d_b, c8 = _conv_silu_bwd(dbm, f["cvb"], f["sgb"], ins["b"], f["bh"], ins["cwb"], cb8[...])
        db_ref[...] = d_raw.astype(BF16)
        dcwb_ref[...] += d_w
        dcbb_ref[...] += d_b
        cb8[...] = c8
        d_raw, d_w, d_b, c8 = _conv_silu_bwd(dcm, f["cvc"], f["sgc"], ins["c"], f["ch"], ins["cwc"], cc8[...])
        dc_ref[...] = d_raw.astype(BF16)
        dcwc_ref[...] += d_w
        dcbc_ref[...] += d_b
        cc8[...] = c8

    out_specs = [rows_spec, rows_spec, rown_spec, rown_spec,
                 pl.BlockSpec((None, C, R), lambda g, i: (g, NS - 1 - i, 0)),
                 sp["pr"], sp["pr"], sp["pr"], sp["cwx"], sp["cwn"], sp["cwn"], sp["cbx"], sp["cbn"], sp["cbn"], sp["cbx"]]
    out_shape = [jax.ShapeDtypeStruct((T, G * RP), BF16), jax.ShapeDtypeStruct((T, G * RP), BF16),
                 jax.ShapeDtypeStruct((T, G * SSD_N), BF16), jax.ShapeDtypeStruct((T, G * SSD_N), BF16),
                 jax.ShapeDtypeStruct((G, T, R), F32),
                 jax.ShapeDtypeStruct((G, 1, R), F32), jax.ShapeDtypeStruct((G, 1, R), F32), jax.ShapeDtypeStruct((G, 1, R), F32),
                 jax.ShapeDtypeStruct((SSD_CONV_W, G * RP), F32), jax.ShapeDtypeStruct((SSD_CONV_W, G * SSD_N), F32),
                 jax.ShapeDtypeStruct((SSD_CONV_W, G * SSD_N), F32),
                 jax.ShapeDtypeStruct((1, G * RP), F32), jax.ShapeDtypeStruct((1, G * SSD_N), F32),
                 jax.ShapeDtypeStruct((1, G * SSD_N), F32), jax.ShapeDtypeStruct((1, G * RP), F32)]
    return pl.pallas_call(
        body, name=name, grid=(G, NS), in_specs=specs, out_specs=out_specs, out_shape=out_shape,
        scratch_shapes=[pltpu.VMEM((SSD_N, RP), F32), pltpu.VMEM((8, RP), F32), pltpu.VMEM((8, SSD_N), F32),
                        pltpu.VMEM((8, SSD_N), F32)],
        compiler_params=_cparams("parallel", "arbitrary"),
    )(*args)


_ANY = pl.BlockSpec(memory_space=pl.ANY)


def _chip_peer(k):
    x, y, c = lax.axis_index("x"), lax.axis_index("y"), lax.axis_index("c")
    return (x ^ (k >> 1), y ^ (k & 1), c)


def _my_chip():
    return 2 * lax.axis_index("x") + lax.axis_index("y")


def _all_gather_chips(shards, *, name):
    n = len(shards)

    def body(*refs):
        ins, outs = refs[:n], refs[n:2 * n]
        send, recv, loc = refs[2 * n:]
        s = _my_chip()
        copies = []
        for a in range(n):
            cp = pltpu.make_async_copy(ins[a], outs[a].at[s], loc.at[a])
            cp.start()
            copies.append(cp)
        remote = []
        for a in range(n):
            for k in (1, 2, 3):
                cp = pltpu.make_async_remote_copy(
                    src_ref=ins[a], dst_ref=outs[a].at[s], send_sem=send.at[3 * a + k - 1], recv_sem=recv.at[3 * a + k - 1],
                    device_id=_chip_peer(k), device_id_type=MESH_ID)
                cp.start()
                remote.append((a, k, cp))
        for a, k, cp in remote:
            cp.wait_send()
            pltpu.make_async_remote_copy(
                src_ref=ins[a], dst_ref=outs[a].at[s ^ k], send_sem=send.at[3 * a + k - 1], recv_sem=recv.at[3 * a + k - 1],
                device_id=_chip_peer(k), device_id_type=MESH_ID).wait_recv()
        for cp in copies:
            cp.wait()

    return pl.pallas_call(
        body, name=name, in_specs=[_ANY] * n, out_specs=[_ANY] * n,
        out_shape=[jax.ShapeDtypeStruct((4,) + a.shape, a.dtype) for a in shards],
        scratch_shapes=[pltpu.SemaphoreType.DMA((3 * n,)), pltpu.SemaphoreType.DMA((3 * n,)), pltpu.SemaphoreType.DMA((n,))],
        compiler_params=pltpu.CompilerParams(has_side_effects=True),
    )(*shards)


def _scatter_to_chips(parts, *, name):
    n = len(parts)

    def body(*refs):
        ins, outs = refs[:n], refs[n:2 * n]
        send, recv, loc = refs[2 * n:]
        s = _my_chip()
        copies = []
        for a in range(n):
            cp = pltpu.make_async_copy(ins[a].at[s], outs[a].at[s], loc.at[a])
            cp.start()
            copies.append(cp)
        remote = []
        for a in range(n):
            for k in (1, 2, 3):
                cp = pltpu.make_async_remote_copy(
                    src_ref=ins[a].at[s ^ k], dst_ref=outs[a].at[s], send_sem=send.at[3 * a + k - 1],
                    recv_sem=recv.at[3 * a + k - 1], device_id=_chip_peer(k), device_id_type=MESH_ID)
                cp.start()
                remote.append((a, k, cp))
        for a, k, cp in remote:
            cp.wait_send()
            pltpu.make_async_remote_copy(
                src_ref=ins[a].at[s ^ k], dst_ref=outs[a].at[s ^ k], send_sem=send.at[3 * a + k - 1],
                recv_sem=recv.at[3 * a + k - 1], device_id=_chip_peer(k), device_id_type=MESH_ID).wait_recv()
        for cp in copies:
            cp.wait()

    return pl.pallas_call(
        body, name=name, in_specs=[_ANY] * n, out_specs=[_ANY] * n,
        out_shape=[jax.ShapeDtypeStruct(a.shape, a.dtype) for a in parts],
        scratch_shapes=[pltpu.SemaphoreType.DMA((3 * n,)), pltpu.SemaphoreType.DMA((3 * n,)), pltpu.SemaphoreType.DMA((n,))],
        compiler_params=pltpu.CompilerParams(has_side_effects=True),
    )(*parts)


def _swap_with_sibling(arrs, *, name):
    n = len(arrs)

    def body(*refs):
        ins, outs = refs[:n], refs[n:2 * n]
        send, recv = refs[2 * n:]
        sib = (lax.axis_index("x"), lax.axis_index("y"), 1 - lax.axis_index("c"))
        cps = []
        for a in range(n):
            cp = pltpu.make_async_remote_copy(src_ref=ins[a], dst_ref=outs[a], send_sem=send.at[a], recv_sem=recv.at[a],
                                              device_id=sib, device_id_type=MESH_ID)
            cp.start()
            cps.append(cp)
        for cp in cps:
            cp.wait()

    return pl.pallas_call(
        body, name=name, in_specs=[_ANY] * n, out_specs=[_ANY] * n,
        out_shape=[jax.ShapeDtypeStruct(a.shape, a.dtype) for a in arrs],
        scratch_shapes=[pltpu.SemaphoreType.DMA((n,)), pltpu.SemaphoreType.DMA((n,))],
        compiler_params=pltpu.CompilerParams(has_side_effects=True),
    )(*arrs)


def _all_gather_devices(v, *, name):
    r = v.shape[0]

    def body(v_ref, out_ref, send, recv):
        x, y, c = lax.axis_index("x"), lax.axis_index("y"), lax.axis_index("c")
        me = 4 * x + 2 * y + c
        out_ref[me] = v_ref[...]
        cps = []
        for k in range(1, 8):
            peer = (x ^ (k >> 2), y ^ ((k >> 1) & 1), c ^ (k & 1))
            cp = pltpu.make_async_remote_copy(src_ref=v_ref, dst_ref=out_ref.at[me], send_sem=send.at[k - 1],
                                              recv_sem=recv.at[k - 1], device_id=peer, device_id_type=MESH_ID)
            cp.start()
            cps.append(cp)
        for k, cp in enumerate(cps, start=1):
            cp.wait_send()
            pltpu.make_async_remote_copy(src_ref=v_ref, dst_ref=out_ref.at[me ^ k], send_sem=send.at[k - 1],
                                         recv_sem=recv.at[k - 1], device_id=(x, y, c), device_id_type=MESH_ID).wait_recv()

    vm = pl.BlockSpec(memory_space=pltpu.VMEM)
    return pl.pallas_call(
        body, name=name, in_specs=[vm], out_specs=vm, out_shape=jax.ShapeDtypeStruct((8, r, 128), F32),
        scratch_shapes=[pltpu.SemaphoreType.DMA((7,)), pltpu.SemaphoreType.DMA((7,))],
        compiler_params=pltpu.CompilerParams(has_side_effects=True),
    )(v)


def _row_tile(r, target):
    best = None
    for t in range(16, min(target, r) + 1, 16):
        if r % t == 0:
            best = t
    return best or r


def _sum_slots(buf, *, name, tr=384):
    S, r, c = buf.shape
    tr = _row_tile(r, tr)

    def body(b_ref, o_ref):
        acc = b_ref[0].astype(F32)
        for j in range(1, S):
            acc = acc + b_ref[j].astype(F32)
        o_ref[...] = acc

    return pl.pallas_call(
        body, name=name, grid=(r // tr,), in_specs=[pl.BlockSpec((S, tr, c), lambda i: (0, i, 0))],
        out_specs=pl.BlockSpec((tr, c), lambda i: (i, 0)), out_shape=jax.ShapeDtypeStruct((r, c), F32),
        compiler_params=_cparams("parallel"),
    )(buf)


def _add2(a, b, *, name, tr=384):
    r, c = a.shape
    tr = _row_tile(r, tr)

    def body(a_ref, b_ref, o_ref):
        o_ref[...] = a_ref[...] + b_ref[...]

    spec = pl.BlockSpec((tr, c), lambda i: (i, 0))
    return pl.pallas_call(body, name=name, grid=(r // tr,), in_specs=[spec, spec], out_specs=spec,
                          out_shape=jax.ShapeDtypeStruct((r, c), F32), compiler_params=_cparams("parallel"))(a, b)


def _adamw(w, g, m, v, *, name, tr=256):
    r, c = w.shape
    tr = _row_tile(r, tr)
    bc1 = 1.0 - ADAM_B1 ** ADAM_STEP
    bc2 = 1.0 - ADAM_B2 ** ADAM_STEP

    def body(w_ref, g_ref, m_ref, v_ref, d_ref, mo_ref, vo_ref):
        gg = g_ref[...]
        mn = ADAM_B1 * m_ref[...] + (1.0 - ADAM_B1) * gg
        vn = ADAM_B2 * v_ref[...] + (1.0 - ADAM_B2) * (gg * gg)
        mo_ref[...] = mn
        vo_ref[...] = vn
        d_ref[...] = -ADAM_LR * ((mn / bc1) / (jnp.sqrt(vn / bc2) + ADAM_EPS) + ADAM_WD * w_ref[...])

    spec = pl.BlockSpec((tr, c), lambda i: (i, 0))
    return pl.pallas_call(body, name=name, grid=(r // tr,), in_specs=[spec] * 4, out_specs=[spec] * 3,
                          out_shape=[jax.ShapeDtypeStruct((r, c), F32)] * 3, compiler_params=_cparams("parallel"))(w, g, m, v)


def _pack(vecs, rows):
    flat = jnp.concatenate([v.reshape(-1).astype(F32) for v in vecs])
    return jnp.pad(flat, (0, rows * 128 - flat.shape[0])).reshape(rows, 128)


def _unpack(packed, shapes):
    flat = packed.reshape(-1)
    out, off = [], 0
    for s in shapes:
        n = math.prod(s)
        out.append(flat[off:off + n].reshape(s))
        off += n
    return out


def _pack_rows(shapes):
    n = sum(math.prod(s) for s in shapes)
    return -(-n // 1024) * 8


def kernel(x, norm_mix_pre, norm_mix_post, norm_ffn_pre, norm_ffn_post, ret_w_in, ret_gn_w, ret_w_out, ssd_w_in, ssd_conv_w, ssd_conv_b, ssd_dt_bias, ssd_a_log, ssd_d, ssd_norm_w, ssd_w_out, mlp_w_up, mlp_w_down, loss_target, m_norm_mix_pre, m_norm_mix_post, m_norm_ffn_pre, m_norm_ffn_post, m_ret_w_in, m_ret_gn_w, m_ret_w_out, m_ssd_w_in, m_ssd_conv_w, m_ssd_conv_b, m_ssd_dt_bias, m_ssd_a_log, m_ssd_d, m_ssd_norm_w, m_ssd_w_out, m_mlp_w_up, m_mlp_w_down, v_norm_mix_pre, v_norm_mix_post, v_norm_ffn_pre, v_norm_ffn_post, v_ret_w_in, v_ret_gn_w, v_ret_w_out, v_ssd_w_in, v_ssd_conv_w, v_ssd_conv_b, v_ssd_dt_bias, v_ssd_a_log, v_ssd_d, v_ssd_norm_w, v_ssd_w_out, v_mlp_w_up, v_mlp_w_down):
    T, D = x.shape[1], x.shape[2]
    H = D // RET_DK
    d_inner = 2 * D
    R = d_inner // SSD_P // SSD_G
    RP = R * SSD_P
    n_heads = SSD_G * R
    conv_dim = d_inner + 2 * SSD_G * SSD_N
    n_main = d_inner + conv_dim
    C = min(256, T)
    chip = _my_chip()
    xs, tgt = x[0], loss_target[0]

    conv_sh = ssd_conv_w.shape[2]
    small_shapes = [(SSD_CONV_W, conv_sh), (conv_sh,), (ssd_norm_w.shape[1],)]
    small_rows = _pack_rows(small_shapes)
    shards = [ret_w_in[0].T.astype(BF16), ret_w_out[0].astype(BF16), ssd_w_in[0].T.astype(BF16), ssd_w_out[0].astype(BF16),
              mlp_w_up[0].T.astype(BF16), mlp_w_up[1].T.astype(BF16), mlp_w_down[0].astype(BF16), mlp_w_down[1].astype(BF16),
              _pack([ssd_conv_w[0], ssd_conv_b[0], ssd_norm_w[0]], small_rows)]
    gathered = _all_gather_chips(shards, name="gather_weights")
    full = [g.reshape(4 * g.shape[1], g.shape[2]) for g in gathered[:8]]
    ret_in_t, ret_out, ssd_in_t, ssd_out, up_t0, up_t1, down0, down1 = full
    up_t, down = (up_t0, up_t1), (down0, down1)
    ssd_main_t, ssd_dt_t = ssd_in_t[:n_main], ssd_in_t[n_main:]
    sm = [_unpack(gathered[8][j], small_shapes) for j in range(4)]
    conv_w = jnp.concatenate([sm[j][0] for j in range(4)], axis=1)
    conv_b = jnp.concatenate([sm[j][1] for j in range(4)])[None, :]
    norm_w = jnp.concatenate([sm[j][2] for j in range(4)])[None, :]

    gb = SSD_G * SSD_N
    ssd_prm = (ssd_dt_bias.reshape(SSD_G, 1, R), ssd_dt_bias.reshape(SSD_G, R, 1),
               ssd_a_log.reshape(SSD_G, 1, R), ssd_a_log.reshape(SSD_G, R, 1), ssd_d.reshape(SSD_G, 1, R),
               conv_w[:, :d_inner], conv_w[:, d_inner:d_inner + gb], conv_w[:, d_inner + gb:],
               conv_b[:, :d_inner], conv_b[:, d_inner:d_inner + gb], conv_b[:, d_inner + gb:],
               norm_w, jnp.asarray(np.kron(np.eye(R), np.ones((1, SSD_P))), F32))
    ret_consts = _ret_consts(T, C, H)

    u0 = _rms_pre(xs, norm_mix_pre[0:1], name="pre0")
    proj = _matmul(u0, ret_in_t, "nt", out_dtype=F32, name="ret_in")
    y_ret, st_ret = _ret_fwd(proj, ret_gn_w, ret_consts, C=C, name="ret_fwd")
    m0 = _matmul(y_ret, ret_out, "nn", out_dtype=F32, name="ret_out")
    h1, u1 = _rms_post_pre(xs, m0, norm_mix_post[0:1], norm_ffn_pre[0:1], name="post_pre1")
    a0, hh0 = _matmul(u1, up_t[0], "nt", out_dtype=BF16, name="mlp_up0", epi="relu2")
    f0 = _matmul(hh0, down[0], "nn", out_dtype=F32, name="mlp_down0")
    h2, u2 = _rms_post_pre(h1, f0, norm_ffn_post[0:1], norm_mix_pre[1:2], name="post_pre2")
    pm = _matmul(u2, ssd_main_t, "nt", out_dtype=F32, name="ssd_in")
    pdt = _matmul(u2, ssd_dt_t, "nt", out_dtype=F32, name="ssd_in_dt")
    dt_g = pdt.reshape(T, SSD_G, R).transpose(1, 0, 2)
    dtT_g = pdt.reshape(T, SSD_G, R).transpose(1, 2, 0)
    y_ssd, st_ssd = _ssd_fwd(pm, dt_g, dtT_g, ssd_prm, C=C, R=R, name="ssd_fwd")
    m1 = _matmul(y_ssd, ssd_out, "nn", out_dtype=F32, name="ssd_out")
    h3, u3 = _rms_post_pre(h2, m1, norm_mix_post[1:2], norm_ffn_pre[1:2], name="post_pre3")
    a1, hh1 = _matmul(u3, up_t[1], "nt", out_dtype=BF16, name="mlp_up1", epi="relu2")
    f1 = _matmul(hh1, down[1], "nn", out_dtype=F32, name="mlp_down1")
    dh4, sq = _rms_post_loss(h3, f1, norm_ffn_post[1:2], tgt, name="post_loss")
    loss = lax.psum(sq[0, 0], MESH_AXES) * (0.5 / D)

    def mlp_bwd(i, dh_out, h_in, u, a, hh, f):
        df, d_post = _rms_post_bwd(f, norm_ffn_post[i:i + 1], dh_out, name=f"post_bwd_ffn{i}")
        g_down = _matmul(hh, df, "tn", out_dtype=BF16, name=f"mlp_down_wg{i}")
        da = _matmul(df, down[i], "nt", out_dtype=BF16, name=f"mlp_down_dg{i}", epi="drelu2", extra=a)
        g_up_t = _matmul(da, u, "tn", out_dtype=BF16, name=f"mlp_up_wg{i}")
        du = _matmul(da, up_t[i], "nn", out_dtype=F32, name=f"mlp_up_dg{i}")
        dh, d_pre = _rms_pre_bwd(h_in, norm_ffn_pre[i:i + 1], du, dh_out, name=f"pre_bwd_ffn{i}")
        return dh, g_up_t, g_down, d_pre, d_post

    dh3, g_up_t1, g_down1, d_nfp1, d_nfpost1 = mlp_bwd(1, dh4, h3, u3, a1, hh1, f1)
    dm1, d_nmpost1 = _rms_post_bwd(m1, norm_mix_post[1:2], dh3, name="post_bwd_mix1")
    g_ssd_out = _matmul(y_ssd, dm1, "tn", out_dtype=BF16, name="ssd_out_wg")
    dy_ssd = _matmul(dm1, ssd_out, "nt", out_dtype=F32, name="ssd_out_dg")
    (dz, dxr, dbr, dcr, ddt_g, d_bias, d_alog, d_dskip, dcwx, dcwb, dcwc, dcbx, dcbb, dcbc, d_normw) = _ssd_bwd(
        pm, dt_g, dtT_g, ssd_prm, st_ssd, dy_ssd, C=C, R=R, name="ssd_bwd")
    dpm = jnp.concatenate([dz, dxr, dbr, dcr], axis=1)
    dpdt = ddt_g.transpose(1, 0, 2).reshape(T, n_heads).astype(BF16)
    g_ssd_in_t = jnp.concatenate([_matmul(dpm, u2, "tn", out_dtype=BF16, name="ssd_in_wg"),
                                  _matmul(dpdt, u2, "tn", out_dtype=BF16, name="ssd_in_dt_wg")], axis=0)
    du2 = _matmul(dpm, ssd_main_t, "nn", out_dtype=F32, name="ssd_in_dg")
    du2 = _matmul(dpdt, ssd_dt_t, "nn", out_dtype=F32, name="ssd_in_dt_dg", epi="add", extra=du2)
    dh2, d_nmp1 = _rms_pre_bwd(h2, norm_mix_pre[1:2], du2, dh3, name="pre_bwd_mix1")
    dh1, g_up_t0, g_down0, d_nfp0, d_nfpost0 = mlp_bwd(0, dh2, h1, u1, a0, hh0, f0)
    dm0, d_nmpost0 = _rms_post_bwd(m0, norm_mix_post[0:1], dh1, name="post_bwd_mix0")
    g_ret_out = _matmul(y_ret, dm0, "tn", out_dtype=BF16, name="ret_out_wg")
    dy_ret = _matmul(dm0, ret_out, "nt", out_dtype=F32, name="ret_out_dg")
    dq, dk, dv, dg, d_gn = _ret_bwd(proj, ret_gn_w, ret_consts, st_ret, dy_ret, C=C, name="ret_bwd")
    dproj = jnp.concatenate([dq, dk, dv, dg], axis=1)
    g_ret_in_t = _matmul(dproj, u0, "tn", out_dtype=BF16, name="ret_in_wg")
    du0 = _matmul(dproj, ret_in_t, "nn", out_dtype=F32, name="ret_in_dg")
    grad_x, d_nmp0 = _rms_pre_bwd(xs, norm_mix_pre[0:1], du0, dh1, name="pre_bwd_mix0")

    big = [g_ret_in_t, g_ret_out, g_ssd_in_t, g_ssd_out, g_up_t0, g_up_t1, g_down0, g_down1]
    parts = [g.reshape(4, g.shape[0] // 4, g.shape[1]) for g in big]
    landed = _scatter_to_chips(parts, name="scatter_grads")
    partial = [_sum_slots(b, name=f"sum_chips{n}") for n, b in enumerate(landed)]
    other = _swap_with_sibling(partial, name="swap_partials")
    summed = [_add2(p, o, name=f"sum_cores{n}") for n, (p, o) in enumerate(zip(partial, other))]
    transposed = (True, False, True, False, True, True, False, False)
    gshard = [s.T if t else s for s, t in zip(summed, transposed)]
    g_ret_in, g_ret_out_s, g_ssd_in, g_ssd_out_s, g_up0, g_up1, g_dn0, g_dn1 = gshard

    def upd(w, g, m, v, name):
        shp = w.shape
        w2, g2, m2, v2 = (t.reshape(-1, shp[-1]) for t in (w, g, m, v))
        d, mn, vn = _adamw(w2, g2, m2, v2, name=name)
        return g.reshape(shp), d.reshape(shp), mn.reshape(shp), vn.reshape(shp)

    res = {}
    res["ret_w_in"] = upd(ret_w_in, g_ret_in[None], m_ret_w_in, v_ret_w_in, "adamw_ret_in")
    res["ret_w_out"] = upd(ret_w_out, g_ret_out_s[None], m_ret_w_out, v_ret_w_out, "adamw_ret_out")
    res["ssd_w_in"] = upd(ssd_w_in, g_ssd_in[None], m_ssd_w_in, v_ssd_w_in, "adamw_ssd_in")
    res["ssd_w_out"] = upd(ssd_w_out, g_ssd_out_s[None], m_ssd_w_out, v_ssd_w_out, "adamw_ssd_out")
    res["mlp_w_up"] = upd(mlp_w_up, jnp.stack([g_up0, g_up1]), m_mlp_w_up, v_mlp_w_up, "adamw_up")
    res["mlp_w_down"] = upd(mlp_w_down, jnp.stack([g_dn0, g_dn1]), m_mlp_w_down, v_mlp_w_down, "adamw_down")

    d_conv_w = jnp.concatenate([dcwx, dcwb, dcwc], axis=1)
    d_conv_b = jnp.concatenate([dcbx, dcbb, dcbc], axis=1)
    small_grads = [jnp.concatenate([d_nmp0, d_nmp1]), jnp.concatenate([d_nmpost0, d_nmpost1]),
                   jnp.concatenate([d_nfp0, d_nfp1]), jnp.concatenate([d_nfpost0, d_nfpost1]),
                   d_gn, d_bias.reshape(1, n_heads), d_alog.reshape(1, n_heads), d_dskip.reshape(1, n_heads),
                   d_conv_w, d_conv_b, d_normw]
    sg_shapes = [g.shape for g in small_grads]
    sg_rows = _pack_rows(sg_shapes)
    everyone = _all_gather_devices(_pack(small_grads, sg_rows), name="gather_small_grads")
    sg = _unpack(_sum_slots(everyone, name="sum_small_grads", tr=sg_rows), sg_shapes)
    (g_nmp, g_nmpost, g_nfp, g_nfpost, g_gn, g_bias, g_alog, g_dskip, g_cw_full, g_cb_full, g_nw_full) = sg
    g_cw = lax.dynamic_slice_in_dim(g_cw_full, chip * conv_sh, conv_sh, axis=1)[None]
    g_cb = lax.dynamic_slice_in_dim(g_cb_full, chip * conv_sh, conv_sh, axis=1)
    nw_sh = ssd_norm_w.shape[1]
    g_nw = lax.dynamic_slice_in_dim(g_nw_full, chip * nw_sh, nw_sh, axis=1)
    small = [("norm_mix_pre", norm_mix_pre, g_nmp, m_norm_mix_pre, v_norm_mix_pre),
             ("norm_mix_post", norm_mix_post, g_nmpost, m_norm_mix_post, v_norm_mix_post),
             ("norm_ffn_pre", norm_ffn_pre, g_nfp, m_norm_ffn_pre, v_norm_ffn_pre),
             ("norm_ffn_post", norm_ffn_post, g_nfpost, m_norm_ffn_post, v_norm_ffn_post),
             ("ret_gn_w", ret_gn_w, g_gn, m_ret_gn_w, v_ret_gn_w),
             ("ssd_conv_w", ssd_conv_w, g_cw, m_ssd_conv_w, v_ssd_conv_w),
             ("ssd_conv_b", ssd_conv_b, g_cb, m_ssd_conv_b, v_ssd_conv_b),
             ("ssd_dt_bias", ssd_dt_bias, g_bias, m_ssd_dt_bias, v_ssd_dt_bias),
             ("ssd_a_log", ssd_a_log, g_alog, m_ssd_a_log, v_ssd_a_log),
             ("ssd_d", ssd_d, g_dskip, m_ssd_d, v_ssd_d),
             ("ssd_norm_w", ssd_norm_w, g_nw, m_ssd_norm_w, v_ssd_norm_w)]
    sw_shapes = [w.shape for _, w, _, _, _ in small]
    sw_rows = _pack_rows(sw_shapes)
    packs = [_pack([t[j] for t in small], sw_rows) for j in (1, 2, 3, 4)]
    d_p, m_p, v_p = _adamw(*packs, name="adamw_small", tr=sw_rows)
    d_s, m_s, v_s = _unpack(d_p, sw_shapes), _unpack(m_p, sw_shapes), _unpack(v_p, sw_shapes)
    for j, (nm, w, g, _, _) in enumerate(small):
        res[nm] = (g.reshape(w.shape), d_s[j], m_s[j], v_s[j])

    order = ["norm_mix_pre", "norm_mix_post", "norm_ffn_pre", "norm_ffn_post", "ret_w_in", "ret_gn_w", "ret_w_out",
             "ssd_w_in", "ssd_conv_w", "ssd_conv_b", "ssd_dt_bias", "ssd_a_log", "ssd_d", "ssd_norm_w", "ssd_w_out",
             "mlp_w_up", "mlp_w_down"]
    return (loss, grad_x[None], *[res[n][0] for n in order], *[res[n][1] for n in order],
            *[res[n][2] for n in order], *[res[n][3] for n in order])
```

```python
import functools
import math

import numpy as np
import jax
import jax.numpy as jnp
from jax import lax
from jax.experimental import pallas as pl
from jax.experimental.pallas import tpu as pltpu

F32 = jnp.float32
BF16 = jnp.bfloat16
HI = lax.Precision.HIGHEST
VMEM_LIMIT_BYTES = 56 * 1024 * 1024
MESH_AXES = ("x", "y", "c")
MESH_ID = pl.DeviceIdType.MESH

RMS_EPS = 1e-6
GN_EPS = 1e-5
RET_DK = 256
RET_DV = 512
ROPE_BASE = 10000.0
REF_CHUNK = 64
SSD_P = 64
SSD_N = 128
SSD_G = 8
SSD_CONV_W = 4
ADAM_LR, ADAM_B1, ADAM_B2, ADAM_EPS, ADAM_WD, ADAM_STEP = 0.001, 0.9, 0.999, 1e-08, 0.01, 10

NN = (((1,), (0,)), ((), ()))
NT = (((1,), (1,)), ((), ()))
TN = (((0,), (0,)), ((), ()))


def _cparams(*sem):
    return pltpu.CompilerParams(dimension_semantics=sem, vmem_limit_bytes=VMEM_LIMIT_BYTES)


def _dot(a, b, dims=NN):
    return lax.dot_general(a.astype(BF16), b.astype(BF16), dims, preferred_element_type=F32)


def _split_bf16(x, terms):
    parts, rest = [], x
    for _ in range(terms):
        p = rest.astype(BF16)
        parts.append(p)
        rest = rest - p.astype(F32)
    return parts


def _dot_sel(a, b, dims=NN, *, split, terms=3):
    if split == "a":
        sel = b.astype(BF16)
        return sum(lax.dot_general(p, sel, dims, preferred_element_type=F32) for p in _split_bf16(a, terms))
    sel = a.astype(BF16)
    return sum(lax.dot_general(sel, p, dims, preferred_element_type=F32) for p in _split_bf16(b, terms))


def _sigmoid(x):
    return 1.0 / (1.0 + jnp.exp(-x))


def _colsum(x):
    return jnp.sum(x, axis=0, keepdims=True)


MM_TILE = 1024
MM_FULL_K = 2048


def _mm_tiles(M, N, K):
    tm = min(M, MM_TILE)
    if K <= MM_FULL_K:
        return tm, min(N, MM_TILE), K
    return tm, min(N, 2 * MM_TILE), MM_TILE


def _matmul(a, b, mode, *, out_dtype, name, epi=None, extra=None):
    if mode == "nn":
        (M, K), (K2, N) = a.shape, b.shape
    elif mode == "nt":
        (M, K), (N, K2) = a.shape, b.shape
    else:
        (K, M), (K2, N) = a.shape, b.shape
    assert K == K2, (a.shape, b.shape, mode)
    tm, tn, tk = _mm_tiles(M, N, K)
    assert M % tm == 0 and N % tn == 0 and K % tk == 0, (M, N, K, tm, tn, tk)
    nk = K // tk
    if mode == "tn":
        a_spec = pl.BlockSpec((tk, tm), lambda i, j, k: (k, i))
    else:
        a_spec = pl.BlockSpec((tm, tk), lambda i, j, k: (i, k))
    if mode == "nt":
        b_spec = pl.BlockSpec((tn, tk), lambda i, j, k: (j, k))
    else:
        b_spec = pl.BlockSpec((tk, tn), lambda i, j, k: (k, j))
    dims = {"nn": NN, "nt": NT, "tn": TN}[mode]
    o_spec = pl.BlockSpec((tm, tn), lambda i, j, k: (i, j))
    has_extra = epi in ("drelu2", "add")
    n_out = 2 if epi == "relu2" else 1

    def body(*refs):
        a_ref, b_ref = refs[0], refs[1]
        e_ref = refs[2] if has_extra else None
        outs = refs[2 + int(has_extra):2 + int(has_extra) + n_out]

        def finish(r):
            if epi is None:
                outs[0][...] = r.astype(outs[0].dtype)
            elif epi == "relu2":
                outs[0][...] = r.astype(outs[0].dtype)
                h = jnp.maximum(r, 0.0)
                outs[1][...] = (h * h).astype(outs[1].dtype)
            elif epi == "drelu2":
                av = jnp.maximum(e_ref[...].astype(F32), 0.0)
                outs[0][...] = (r * (2.0 * av)).astype(outs[0].dtype)
            else:
                outs[0][...] = (r + e_ref[...].astype(F32)).astype(outs[0].dtype)

        if nk == 1:
            finish(_dot(a_ref[...], b_ref[...], dims))
            return
        acc = refs[-1]
        k = pl.program_id(2)

        @pl.when(k == 0)
        def _():
            acc[...] = jnp.zeros_like(acc)

        acc[...] += _dot(a_ref[...], b_ref[...], dims)

        @pl.when(k == nk - 1)
        def _():
            finish(acc[...])

    in_specs = [a_spec, b_spec] + ([o_spec] if has_extra else [])
    args = [a, b] + ([extra] if has_extra else [])
    out_shape = [jax.ShapeDtypeStruct((M, N), out_dtype)] * n_out
    res = pl.pallas_call(
        body, name=name, grid=(M // tm, N // tn, nk), in_specs=in_specs, out_specs=[o_spec] * n_out,
        out_shape=out_shape, scratch_shapes=[pltpu.VMEM((tm, tn), F32)] if nk > 1 else [],
        compiler_params=_cparams("parallel", "parallel", "arbitrary"),
    )(*args)
    return res if n_out == 2 else res[0]


def _rstd(x):
    return lax.rsqrt(jnp.mean(x * x, axis=-1, keepdims=True) + RMS_EPS)


def _row_call(body, ins, outs_shape, *, name, rows, tr, acc_outs=()):
    tr = min(tr, rows)
    assert rows % tr == 0
    in_specs = []
    for arr, blocked in ins:
        if blocked:
            in_specs.append(pl.BlockSpec((tr, arr.shape[1]), lambda i: (i, 0)))
        else:
            in_specs.append(pl.BlockSpec(arr.shape, lambda i: (0, 0)))
    out_specs = []
    for n, s in enumerate(outs_shape):
        if n in acc_outs:
            out_specs.append(pl.BlockSpec(s.shape, lambda i: (0, 0)))
        else:
            out_specs.append(pl.BlockSpec((tr, s.shape[1]), lambda i: (i, 0)))
    return pl.pallas_call(
        body, name=name, grid=(rows // tr,), in_specs=in_specs, out_specs=out_specs, out_shape=outs_shape,
        compiler_params=_cparams("arbitrary" if acc_outs else "parallel"),
    )(*[a for a, _ in ins])


def _rms_pre(h, w, *, name):
    T, D = h.shape

    def body(h_ref, w_ref, u_ref):
        x = h_ref[...]
        u_ref[...] = (x * _rstd(x) * w_ref[...]).astype(BF16)

    return _row_call(body, [(h, True), (w, False)], [jax.ShapeDtypeStruct((T, D), BF16)], name=name, rows=T, tr=256)[0]


def _rms_post_pre(h, m, w_post, w_pre, *, name):
    T, D = h.shape

    def body(h_ref, m_ref, wp_ref, wn_ref, hn_ref, u_ref):
        mm = m_ref[...]
        hn = h_ref[...] + mm * _rstd(mm) * wp_ref[...]
        hn_ref[...] = hn
        u_ref[...] = (hn * _rstd(hn) * wn_ref[...]).astype(BF16)

    return _row_call(body, [(h, True), (m, True), (w_post, False), (w_pre, False)],
                     [jax.ShapeDtypeStruct((T, D), F32), jax.ShapeDtypeStruct((T, D), BF16)], name=name, rows=T, tr=256)


def _rms_post_loss(h, m, w_post, tgt, *, name):
    T, D = h.shape

    def body(h_ref, m_ref, wp_ref, t_ref, dh_ref, loss_ref):
        @pl.when(pl.program_id(0) == 0)
        def _():
            loss_ref[...] = jnp.zeros_like(loss_ref)

        mm = m_ref[...]
        err = h_ref[...] + mm * _rstd(mm) * wp_ref[...] - t_ref[...]
        dh_ref[...] = err * (1.0 / D)
        loss_ref[...] += _colsum(jnp.sum(err * err, axis=1, keepdims=True))

    return _row_call(body, [(h, True), (m, True), (w_post, False), (tgt, True)],
                     [jax.ShapeDtypeStruct((T, D), F32), jax.ShapeDtypeStruct((1, 1), F32)],
                     name=name, rows=T, tr=256, acc_outs=(1,))


def _rms_bwd_vals(x, w, dy):
    r = _rstd(x)
    xh = x * r
    g = dy * w
    dx = r * (g - xh * jnp.mean(g * xh, axis=-1, keepdims=True))
    return dx, _colsum(dy * xh)


def _rms_post_bwd(m, w_post, dh, *, name):
    T, D = m.shape

    def body(m_ref, w_ref, dh_ref, dm_ref, dw_ref):
        @pl.when(pl.program_id(0) == 0)
        def _():
            dw_ref[...] = jnp.zeros_like(dw_ref)

        dx, dw = _rms_bwd_vals(m_ref[...], w_ref[...], dh_ref[...])
        dm_ref[...] = dx.astype(BF16)
        dw_ref[...] += dw

    return _row_call(body, [(m, True), (w_post, False), (dh, True)],
                     [jax.ShapeDtypeStruct((T, D), BF16), jax.ShapeDtypeStruct((1, D), F32)],
                     name=name, rows=T, tr=256, acc_outs=(1,))


def _rms_pre_bwd(h, w_pre, du, dh_out, *, name):
    T, D = h.shape

    def body(h_ref, w_ref, du_ref, dho_ref, dh_ref, dw_ref):
        @pl.when(pl.program_id(0) == 0)
        def _():
            dw_ref[...] = jnp.zeros_like(dw_ref)

        dx, dw = _rms_bwd_vals(h_ref[...], w_ref[...], du_ref[...])
        dh_ref[...] = dho_ref[...] + dx
        dw_ref[...] += dw

    return _row_call(body, [(h, True), (w_pre, False), (du, True), (dh_out, True)],
                     [jax.ShapeDtypeStruct((T, D), F32), jax.ShapeDtypeStruct((1, D), F32)],
                     name=name, rows=T, tr=256, acc_outs=(1,))


def _ret_consts(T, C, H):
    lg = np.log1p(-np.exp2(-5.0 - np.arange(H, dtype=np.float64)))
    idx = np.arange(C, dtype=np.float64)
    dist = np.abs(idx[:, None] - idx[None, :])
    vis = (idx[None, :] // REF_CHUNK) <= (idx[:, None] // REF_CHUNK)
    mask = np.exp(dist[None] * lg[:, None, None]) * vis[None]
    xi = np.exp((idx[None, :] + 1.0) * lg[:, None])[..., None]
    zeta = np.exp((C - 1.0 - idx)[None, :] * lg[:, None])[..., None]
    half = RET_DK // 2
    inv_freq = ROPE_BASE ** (-np.arange(half, dtype=np.float32) / np.float32(half))
    ang = np.arange(T, dtype=np.float32)[:, None] * inv_freq[None, :].astype(np.float32)
    return (jnp.asarray(mask, F32), jnp.asarray(xi, F32), jnp.asarray(zeta, F32),
            jnp.asarray(np.cos(ang), F32), jnp.asarray(np.sin(ang), F32))


def _rot(t, cos, sin):
    half = RET_DK // 2
    t1, t2 = t[:, :half], t[:, half:]
    return jnp.concatenate([t1 * cos - t2 * sin, t1 * sin + t2 * cos], axis=1)


def _unrot(d, cos, sin):
    half = RET_DK // 2
    d1, d2 = d[:, :half], d[:, half:]
    return jnp.concatenate([d1 * cos + d2 * sin, d2 * cos - d1 * sin], axis=1)


def _ret_specs(C, H, rev, NS):
    def ci(i):
        return NS - 1 - i if rev else i

    nq = H
    q_spec = pl.BlockSpec((C, RET_DK), lambda h, i: (ci(i), h))
    k_spec = pl.BlockSpec((C, RET_DK), lambda h, i: (ci(i), nq + h))
    v_spec = pl.BlockSpec((C, RET_DV), lambda h, i: (ci(i), H + h))
    g_spec = pl.BlockSpec((C, RET_DV), lambda h, i: (ci(i), 2 * H + h))
    cs_spec = pl.BlockSpec((C, RET_DK // 2), lambda h, i: (ci(i), 0))
    m_spec = pl.BlockSpec((None, C, C), lambda h, i: (h, 0, 0))
    vec_spec = pl.BlockSpec((None, C, 1), lambda h, i: (h, 0, 0))
    gn_spec = pl.BlockSpec((1, RET_DV), lambda h, i: (0, h))
    st_spec = pl.BlockSpec((None, None, RET_DK, RET_DV), lambda h, i: (h, ci(i), 0, 0))
    return q_spec, k_spec, v_spec, g_spec, cs_spec, m_spec, vec_spec, gn_spec, st_spec


def _ret_fwd_vals(q, k, v, cos, sin, mask, xi, s_in):
    qr = _rot(q, cos, sin)
    kr = _rot(k, cos, sin) * (RET_DK ** -0.5)
    a = _dot(qr, kr, NT) * mask
    o = _dot(a, v) + _dot(qr, s_in) * xi
    mu = jnp.mean(o, axis=1, keepdims=True)
    oc = o - mu
    rstd = lax.rsqrt(jnp.mean(oc * oc, axis=1, keepdims=True) + GN_EPS)
    return qr, kr, a, oc * rstd, rstd


def _ret_fwd(proj, gn_w, consts, *, C, name):
    T = proj.shape[0]
    H = gn_w.shape[1] // RET_DV
    NS = T // C
    mask, xi, zeta, cos, sin = consts
    q_spec, k_spec, v_spec, g_spec, cs_spec, m_spec, vec_spec, gn_spec, st_spec = _ret_specs(C, H, False, NS)
    y_spec = pl.BlockSpec((C, RET_DV), lambda h, i: (i, h))

    def body(q_ref, k_ref, v_ref, g_ref, cos_ref, sin_ref, m_ref, xi_ref, ze_ref, gn_ref, y_ref, st_ref, S):
        @pl.when(pl.program_id(1) == 0)
        def _():
            S[...] = jnp.zeros_like(S)

        s_in = S[...]
        st_ref[...] = s_in
        v = v_ref[...]
        xi_v = xi_ref[...]
        qr, kr, a, on, rstd = _ret_fwd_vals(q_ref[...], k_ref[...], v, cos_ref[...], sin_ref[...], m_ref[...], xi_v, s_in)
        g = g_ref[...]
        y_ref[...] = (g * _sigmoid(g) * on * gn_ref[...]).astype(BF16)
        S[...] = s_in * xi_v[C - 1:C, :] + _dot(kr * ze_ref[...], v, TN)

    return pl.pallas_call(
        body, name=name, grid=(H, NS),
        in_specs=[q_spec, k_spec, v_spec, g_spec, cs_spec, cs_spec, m_spec, vec_spec, vec_spec, gn_spec],
        out_specs=[y_spec, st_spec],
        out_shape=[jax.ShapeDtypeStruct((T, H * RET_DV), BF16), jax.ShapeDtypeStruct((H, NS, RET_DK, RET_DV), F32)],
        scratch_shapes=[pltpu.VMEM((RET_DK, RET_DV), F32)],
        compiler_params=_cparams("parallel", "arbitrary"),
    )(proj, proj, proj, proj, cos, sin, mask, xi, zeta, gn_w)


def _ret_bwd(proj, gn_w, consts, states, dy, *, C, name):
    T = proj.shape[0]
    H = gn_w.shape[1] // RET_DV
    NS = T // C
    mask, xi, zeta, cos, sin = consts
    q_spec, k_spec, v_spec, g_spec, cs_spec, m_spec, vec_spec, gn_spec, st_spec = _ret_specs(C, H, True, NS)
    dy_spec = pl.BlockSpec((C, RET_DV), lambda h, i: (NS - 1 - i, h))
    dqk_spec = pl.BlockSpec((C, RET_DK), lambda h, i: (NS - 1 - i, h))
    scale = RET_DK ** -0.5

    def body(q_ref, k_ref, v_ref, g_ref, cos_ref, sin_ref, m_ref, xi_ref, ze_ref, gn_ref, st_ref, dy_ref,
             dq_ref, dk_ref, dv_ref, dg_ref, dgn_ref, dS):
        @pl.when(pl.program_id(1) == 0)
        def _():
            dS[...] = jnp.zeros_like(dS)
            dgn_ref[...] = jnp.zeros_like(dgn_ref)

        s_in = st_ref[...]
        v = v_ref[...]
        cos, sin, mask, xi_v, ze = cos_ref[...], sin_ref[...], m_ref[...], xi_ref[...], ze_ref[...]
        qr, kr, a, on, rstd = _ret_fwd_vals(q_ref[...], k_ref[...], v, cos, sin, mask, xi_v, s_in)
        g = g_ref[...]
        sg = _sigmoid(g)
        silu = g * sg
        gnw = gn_ref[...]
        dy = dy_ref[...].astype(F32)
        dg_ref[...] = (dy * on * gnw * (sg * (1.0 + g * (1.0 - sg)))).astype(BF16)
        t = dy * silu
        dgn_ref[...] += _colsum(t * on)
        don = t * gnw
        do = rstd * (don - jnp.mean(don, axis=1, keepdims=True) - on * jnp.mean(don * on, axis=1, keepdims=True))
        dox = do * xi_v
        ds_out = dS[...]
        da = _dot(do, v, NT) * mask
        kz = kr * ze
        dv_ref[...] = (_dot(a, do, TN) + _dot(kz, ds_out)).astype(BF16)
        dqr = _dot(da, kr) + _dot(dox, s_in, NT)
        dkr = _dot(da, qr, TN) + _dot(v, ds_out, NT) * ze
        dS[...] = ds_out * xi_v[C - 1:C, :] + _dot(qr, dox, TN)
        dq_ref[...] = _unrot(dqr, cos, sin).astype(BF16)
        dk_ref[...] = _unrot(dkr * scale, cos, sin).astype(BF16)

    return pl.pallas_call(
        body, name=name, grid=(H, NS),
        in_specs=[q_spec, k_spec, v_spec, g_spec, cs_spec, cs_spec, m_spec, vec_spec, vec_spec, gn_spec, st_spec, dy_spec],
        out_specs=[dqk_spec, dqk_spec, dy_spec, dy_spec, gn_spec],
        out_shape=[jax.ShapeDtypeStruct((T, H * RET_DK), BF16), jax.ShapeDtypeStruct((T, H * RET_DK), BF16),
                   jax.ShapeDtypeStruct((T, H * RET_DV), BF16), jax.ShapeDtypeStruct((T, H * RET_DV), BF16),
                   jax.ShapeDtypeStruct((1, H * RET_DV), F32)],
        scratch_shapes=[pltpu.VMEM((RET_DK, RET_DV), F32)],
        compiler_params=_cparams("parallel", "arbitrary"),
    )(proj, proj, proj, proj, cos, sin, mask, xi, zeta, gn_w, states, dy)


def _shift_down(x, prev8, k):
    if k == 0:
        return x
    y = pltpu.roll(x, k, 0)
    row = lax.broadcasted_iota(jnp.int32, prev8.shape, 0)
    top = jnp.where(row < k, pltpu.roll(prev8, k, 0), y[:8])
    return jnp.concatenate([top, y[8:]], axis=0)


def _shift_up(x, next8, k):
    if k == 0:
        return x
    n = x.shape[0]
    y = pltpu.roll(x, n - k, 0)
    row = lax.broadcasted_iota(jnp.int32, next8.shape, 0)
    bot = jnp.where(row >= 8 - k, pltpu.roll(next8, 8 - k, 0), y[n - 8:])
    return jnp.concatenate([y[:n - 8], bot], axis=0)


def _conv_silu(raw, halo, w, b):
    cv = b
    for tap in range(SSD_CONV_W):
        cv = cv + _shift_down(raw, halo, SSD_CONV_W - 1 - tap) * w[tap:tap + 1, :]
    sg = _sigmoid(cv)
    return cv * sg, cv, sg


def _conv_silu_bwd(d_post, cv, sg, raw, halo, w, carry8):
    dcv = d_post * (sg * (1.0 + cv * (1.0 - sg)))
    d_raw = jnp.zeros_like(raw)
    dws = []
    for tap in range(SSD_CONV_W):
        k = SSD_CONV_W - 1 - tap
        d_raw = d_raw + _shift_up(dcv, carry8, k) * w[tap:tap + 1, :]
        dws.append(_colsum(dcv * _shift_down(raw, halo, k)))
    return d_raw, jnp.concatenate(dws, axis=0), _colsum(dcv), dcv[:8]


def _softplus(x):
    return jnp.maximum(x, 0.0) + jnp.log1p(jnp.exp(-jnp.abs(x)))


def _ssd_common(C, R, dt, dtT, bias, biasT, alog, alogT, E):
    p = dt + bias
    dtv = _softplus(p)
    a = -jnp.exp(alog)
    da = dtv * a
    daT = _softplus(dtT + biasT) * (-jnp.exp(alogT))
    row = lax.broadcasted_iota(jnp.int32, (C, C), 0)
    col = lax.broadcasted_iota(jnp.int32, (C, C), 1)
    tril = row >= col
    trilf = jnp.where(tril, 1.0, 0.0).astype(F32)
    triuf = jnp.where(col >= row, 1.0, 0.0).astype(F32)
    acum = _dot_sel(trilf, da, split="b")
    acumT = _dot_sel(daT, trilf, NT, split="a")
    al = acum[C - 1:C, :]
    ea = jnp.exp(acum)
    dte = jnp.exp(al - acum)
    eal = jnp.exp(al)
    return dict(p=p, dtv=dtv, a=a, da=da, tril=tril, triuf=triuf, acum=acum, acumT=acumT, al=al, ea=ea, dte=dte, eal=eal,
                dtv_e=_dot_sel(dtv, E, split="a", terms=2), ea_e=_dot_sel(ea, E, split="a", terms=2),
                dte_e=_dot_sel(dte, E, split="a", terms=2), eal_e=_dot_sel(eal, E, split="a"))


def _head_decay(q, r, C, R):
    seg = jnp.broadcast_to(q["acum"][:, r:r + 1], (C, C)) - q["acumT"][r:r + 1, :]
    return jnp.exp(jnp.where(q["tril"], seg, -1e30))


def _ssd_group_specs(C, R, NS, rev):
    RP = R * SSD_P
    G = SSD_G
    nz = 1
    hb = C // 8

    def ci(i):
        return NS - 1 - i if rev else i

    def halo_row(i):
        return jnp.maximum(ci(i) * hb - 1, 0)

    off_b = G * RP // SSD_N
    z_spec = pl.BlockSpec((C, RP), lambda g, i: (ci(i), g))
    x_spec = pl.BlockSpec((C, RP), lambda g, i: (ci(i), G + g))
    b_spec = pl.BlockSpec((C, SSD_N), lambda g, i: (ci(i), 2 * off_b + g))
    c_spec = pl.BlockSpec((C, SSD_N), lambda g, i: (ci(i), 2 * off_b + G + g))
    xh_spec = pl.BlockSpec((8, RP), lambda g, i: (halo_row(i), G + g))
    bh_spec = pl.BlockSpec((8, SSD_N), lambda g, i: (halo_row(i), 2 * off_b + g))
    ch_spec = pl.BlockSpec((8, SSD_N), lambda g, i: (halo_row(i), 2 * off_b + G + g))
    dt_spec = pl.BlockSpec((None, C, R), lambda g, i: (g, ci(i), 0))
    dtT_spec = pl.BlockSpec((None, R, C), lambda g, i: (g, 0, ci(i)))
    pr_spec = pl.BlockSpec((None, 1, R), lambda g, i: (g, 0, 0))
    prT_spec = pl.BlockSpec((None, R, 1), lambda g, i: (g, 0, 0))
    cwx_spec = pl.BlockSpec((SSD_CONV_W, RP), lambda g, i: (0, g))
    cwn_spec = pl.BlockSpec((SSD_CONV_W, SSD_N), lambda g, i: (0, g))
    cbx_spec = pl.BlockSpec((1, RP), lambda g, i: (0, g))
    cbn_spec = pl.BlockSpec((1, SSD_N), lambda g, i: (0, g))
    e_spec = pl.BlockSpec((R, RP), lambda g, i: (0, 0))
    st_spec = pl.BlockSpec((None, None, SSD_N, RP), lambda g, i: (g, ci(i), 0, 0))
    return dict(z=z_spec, x=x_spec, b=b_spec, c=c_spec, xh=xh_spec, bh=bh_spec, ch=ch_spec, dt=dt_spec, dtT=dtT_spec,
                pr=pr_spec, prT=prT_spec, cwx=cwx_spec, cwn=cwn_spec, cbx=cbx_spec, cbn=cbn_spec, e=e_spec, st=st_spec)


def _ssd_forward_vals(C, R, refs, first, s_in):
    E = refs["E"]
    halo_on = jnp.where(first, 0.0, 1.0)
    xh, bh, ch = refs["xh"] * halo_on, refs["bh"] * halo_on, refs["ch"] * halo_on
    xs, cvx, sgx = _conv_silu(refs["x"], xh, refs["cwx"], refs["cbx"])
    bm, cvb, sgb = _conv_silu(refs["b"], bh, refs["cwb"], refs["cbb"])
    cm, cvc, sgc = _conv_silu(refs["c"], ch, refs["cwc"], refs["cbc"])
    q = _ssd_common(C, R, refs["dt"], refs["dtT"], refs["bias"], refs["biasT"], refs["alog"], refs["alogT"], E)
    xdt = xs * q["dtv_e"]
    cb = _dot(cm, bm, NT)
    yoff_raw = _dot(cm, s_in)
    ydiag = jnp.zeros_like(xs)
    for r in range(R):
        w_r = cb * _head_decay(q, r, C, R)
        ydiag = ydiag + _dot(w_r, xdt * E[r:r + 1, :])
    d_e = _dot_sel(refs["dskip"], E, split="a")
    y = ydiag + yoff_raw * q["ea_e"] + d_e * xs
    xd = xdt * q["dte_e"]
    s_out = s_in * q["eal_e"] + _dot(bm, xd, TN)
    z = refs["z"]
    sgz = _sigmoid(z)
    yz = y * (z * sgz)
    rn = lax.rsqrt(jnp.mean(yz * yz, axis=1, keepdims=True) + RMS_EPS)
    return dict(q=q, xh=xh, bh=bh, ch=ch, xs=xs, cvx=cvx, sgx=sgx, bm=bm, cvb=cvb, sgb=sgb, cm=cm, cvc=cvc, sgc=sgc,
                xdt=xdt, cb=cb, yoff_raw=yoff_raw, d_e=d_e, y=y, xd=xd, s_out=s_out, z=z, sgz=sgz, yz=yz, rn=rn)


_SSD_IN_NAMES = ("z", "x", "b", "c", "xh", "bh", "ch", "dt", "dtT", "bias", "biasT", "alog", "alogT", "dskip",
                 "cwx", "cwb", "cwc", "cbx", "cbb", "cbc", "nw", "E")


def _ssd_inputs(pm, dt_g, dtT_g, prm, sp):
    bias, biasT, alog, alogT, dskip, cwx, cwb, cwc, cbx, cbb, cbc, nw, E = prm
    args = [pm, pm, pm, pm, pm, pm, pm, dt_g, dtT_g, bias, biasT, alog, alogT, dskip, cwx, cwb, cwc, cbx, cbb, cbc, nw, E]
    specs = [sp["z"], sp["x"], sp["b"], sp["c"], sp["xh"], sp["bh"], sp["ch"], sp["dt"], sp["dtT"], sp["pr"], sp["prT"],
             sp["pr"], sp["prT"], sp["pr"], sp["cwx"], sp["cwn"], sp["cwn"], sp["cbx"], sp["cbn"], sp["cbn"], sp["cbx"], sp["e"]]
    return args, specs


def _ssd_fwd(pm, dt_g, dtT_g, prm, *, C, R, name):
    T = pm.shape[0]
    NS = T // C
    RP = R * SSD_P
    G = SSD_G
    sp = _ssd_group_specs(C, R, NS, False)
    args, specs = _ssd_inputs(pm, dt_g, dtT_g, prm, sp)
    nin = len(args)

    def body(*refs):
        ins = {n: r[...] for n, r in zip(_SSD_IN_NAMES, refs[:nin])}
        y_ref, st_ref, S = refs[nin:]
        first = pl.program_id(1) == 0

        @pl.when(first)
        def _():
            S[...] = jnp.zeros_like(S)

        s_in = S[...]
        st_ref[...] = s_in
        f = _ssd_forward_vals(C, R, ins, first, s_in)
        y_ref[...] = (f["yz"] * f["rn"] * ins["nw"]).astype(BF16)
        S[...] = f["s_out"]

    return pl.pallas_call(
        body, name=name, grid=(G, NS), in_specs=specs,
        out_specs=[pl.BlockSpec((C, RP), lambda g, i: (i, g)), sp["st"]],
        out_shape=[jax.ShapeDtypeStruct((T, G * RP), BF16), jax.ShapeDtypeStruct((G, NS, SSD_N, RP), F32)],
        scratch_shapes=[pltpu.VMEM((SSD_N, RP), F32)],
        compiler_params=_cparams("parallel", "arbitrary"),
    )(*args)


def _ssd_bwd(pm, dt_g, dtT_g, prm, states, dout, *, C, R, name):
    T = pm.shape[0]
    NS = T // C
    RP = R * SSD_P
    G = SSD_G
    sp = _ssd_group_specs(C, R, NS, True)
    args, specs = _ssd_inputs(pm, dt_g, dtT_g, prm, sp)
    nin = len(args)
    rows_spec = pl.BlockSpec((C, RP), lambda g, i: (NS - 1 - i, g))
    rown_spec = pl.BlockSpec((C, SSD_N), lambda g, i: (NS - 1 - i, g))
    args = args + [states, dout]
    specs = specs + [sp["st"], rows_spec]

    def body(*refs):
        ins = {n: r[...] for n, r in zip(_SSD_IN_NAMES, refs[:nin])}
        st_ref, dout_ref = refs[nin], refs[nin + 1]
        (dz_ref, dx_ref, db_ref, dc_ref, ddt_ref, dbias_ref, dalog_ref, dd_ref, dcwx_ref, dcwb_ref, dcwc_ref,
         dcbx_ref, dcbb_ref, dcbc_ref, dnw_ref) = refs[nin + 2:nin + 17]
        dS, cx8, cb8, cc8 = refs[nin + 17:]
        acc_refs = (dbias_ref, dalog_ref, dd_ref, dcwx_ref, dcwb_ref, dcwc_ref, dcbx_ref, dcbb_ref, dcbc_ref, dnw_ref)
        step = pl.program_id(1)

        @pl.when(step == 0)
        def _():
            for r_ in acc_refs + (dS, cx8, cb8, cc8):
                r_[...] = jnp.zeros_like(r_)

        first = step == NS - 1
        E = ins["E"]
        s_in = st_ref[...]
        f = _ssd_forward_vals(C, R, ins, first, s_in)
        q = f["q"]
        xs, bm, cm, xdt, cb, y, z, sgz, yz, rn = (f[n] for n in ("xs", "bm", "cm", "xdt", "cb", "y", "z", "sgz", "yz", "rn"))
        nw = ins["nw"]
        dout = dout_ref[...].astype(F32)
        yh = yz * rn
        dnw_ref[...] += _colsum(dout * yh)
        g1 = dout * nw
        dyz = rn * (g1 - yh * jnp.mean(g1 * yh, axis=1, keepdims=True))
        dz_ref[...] = (dyz * y * (sgz * (1.0 + z * (1.0 - sgz)))).astype(BF16)
        dy = dyz * (z * sgz)
        dd_ref[...] += _dot_sel(_colsum(dy * xs), E, NT, split="a")
        dxs = dy * f["d_e"]
        dyo = dy * q["ea_e"]
        dcm = _dot(dyo, s_in, NT)
        ds_acc = _dot(cm, dyo, TN)
        dacum = _dot_sel(dy * f["yoff_raw"], E, NT, split="a", terms=2) * q["ea"]
        dacumT = jnp.zeros((R, C), F32)
        dxdt = jnp.zeros_like(xs)
        dcb = jnp.zeros((C, C), F32)
        rowR = lax.broadcasted_iota(jnp.int32, (1, R), 1)
        rowRT = lax.broadcasted_iota(jnp.int32, (R, 1), 0)
        for r in range(R):
            mk = E[r:r + 1, :]
            lr = _head_decay(q, r, C, R)
            w_r = cb * lr
            dw = _dot(dy * mk, xdt, NT)
            dxdt = dxdt + _dot(w_r, dy, TN) * mk
            dcb = dcb + dw * lr
            dseg = dw * w_r
            dacum = dacum + jnp.sum(dseg, axis=1, keepdims=True) * jnp.where(rowR == r, 1.0, 0.0)
            dacumT = dacumT - _colsum(dseg) * jnp.where(rowRT == r, 1.0, 0.0)
        dsn = dS[...]
        ds_acc = ds_acc + dsn * q["eal_e"]
        d_eal = _dot_sel(_colsum(dsn * s_in), E, NT, split="a")
        dbm = _dot(f["xd"], dsn, NT)
        dxd = _dot(bm, dsn)
        dxdt = dxdt + dxd * q["dte_e"]
        d_dte = _dot_sel(dxd * xdt, E, NT, split="a", terms=2) * q["dte"]
        d_al = _colsum(d_dte) + d_eal * q["eal"]
        dacum = dacum - d_dte
        rowC = lax.broadcasted_iota(jnp.int32, (C, 1), 0)
        dacum = dacum + jnp.where(rowC == C - 1, 1.0, 0.0) * d_al
        dS[...] = ds_acc
        dcm = dcm + _dot(dcb, bm)
        dbm = dbm + _dot(dcb, cm, TN)
        eye = jnp.where(lax.broadcasted_iota(jnp.int32, (C, C), 0) == lax.broadcasted_iota(jnp.int32, (C, C), 1), 1.0, 0.0)
        dacum = dacum + _dot_sel(eye, dacumT, NT, split="b")
        dda = _dot_sel(q["triuf"], dacum, split="b")
        ddtv = dda * q["a"] + _dot_sel(dxdt * xs, E, NT, split="a", terms=2)
        dalog_ref[...] += _colsum(dda * q["dtv"]) * q["a"]
        dxs = dxs + dxdt * q["dtv_e"]
        dp = ddtv * _sigmoid(q["p"])
        ddt_ref[...] = dp
        dbias_ref[...] += _colsum(dp)
        d_raw, d_w, d_b, c8 = _conv_silu_bwd(dxs, f["cvx"], f["sgx"], ins["x"], f["xh"], ins["cwx"], cx8[...])
        dx_ref[...] = d_raw.astype(BF16)
        dcwx_ref[...] += d_w
        dcbx_ref[...] += d_b
        cx8[...] = c8
        d_raw, d_w, d_b, c8 = _conv_silu_bwd(dbm, f["cvb"], f["sgb"], ins["b"], f["bh"], ins["cwb"], cb8[...])
        db_ref[...] = d_raw.astype(BF16)
        dcwb_ref[...] += d_w
        dcbb_ref[...] += d_b
        cb8[...] = c8
        d_raw, d_w, d_b, c8 = _conv_silu_bwd(dcm, f["cvc"], f["sgc"], ins["c"], f["ch"], ins["cwc"], cc8[...])
        dc_ref[...] = d_raw.astype(BF16)
        dcwc_ref[...] += d_w
        dcbc_ref[...] += d_b
        cc8[...] = c8

    out_specs = [rows_spec, rows_spec, rown_spec, rown_spec,
                 pl.BlockSpec((None, C, R), lambda g, i: (g, NS - 1 - i, 0)),
                 sp["pr"], sp["pr"], sp["pr"], sp["cwx"], sp["cwn"], sp["cwn"], sp["cbx"], sp["cbn"], sp["cbn"], sp["cbx"]]
    out_shape = [jax.ShapeDtypeStruct((T, G * RP), BF16), jax.ShapeDtypeStruct((T, G * RP), BF16),
                 jax.ShapeDtypeStruct((T, G * SSD_N), BF16), jax.ShapeDtypeStruct((T, G * SSD_N), BF16),
                 jax.ShapeDtypeStruct((G, T, R), F32),
                 jax.ShapeDtypeStruct((G, 1, R), F32), jax.ShapeDtypeStruct((G, 1, R), F32), jax.ShapeDtypeStruct((G, 1, R), F32),
                 jax.ShapeDtypeStruct((SSD_CONV_W, G * RP), F32), jax.ShapeDtypeStruct((SSD_CONV_W, G * SSD_N), F32),
                 jax.ShapeDtypeStruct((SSD_CONV_W, G * SSD_N), F32),
                 jax.ShapeDtypeStruct((1, G * RP), F32), jax.ShapeDtypeStruct((1, G * SSD_N), F32),
                 jax.ShapeDtypeStruct((1, G * SSD_N), F32), jax.ShapeDtypeStruct((1, G * RP), F32)]
    return pl.pallas_call(
        body, name=name, grid=(G, NS), in_specs=specs, out_specs=out_specs, out_shape=out_shape,
        scratch_shapes=[pltpu.VMEM((SSD_N, RP), F32), pltpu.VMEM((8, RP), F32), pltpu.VMEM((8, SSD_N), F32),
                        pltpu.VMEM((8, SSD_N), F32)],
        compiler_params=_cparams("parallel", "arbitrary"),
    )(*args)


_ANY = pl.BlockSpec(memory_space=pl.ANY)


def _chip_peer(k):
    x, y, c = lax.axis_index("x"), lax.axis_index("y"), lax.axis_index("c")
    return (x ^ (k >> 1), y ^ (k & 1), c)


def _my_chip():
    return 2 * lax.axis_index("x") + lax.axis_index("y")


def _all_gather_chips(shards, *, name):
    n = len(shards)

    def body(*refs):
        ins, outs = refs[:n], refs[n:2 * n]
        send, recv, loc = refs[2 * n:]
        s = _my_chip()
        copies = []
        for a in range(n):
            cp = pltpu.make_async_copy(ins[a], outs[a].at[s], loc.at[a])
            cp.start()
            copies.append(cp)
        remote = []
        for a in range(n):
            for k in (1, 2, 3):
                cp = pltpu.make_async_remote_copy(
                    src_ref=ins[a], dst_ref=outs[a].at[s], send_sem=send.at[3 * a + k - 1], recv_sem=recv.at[3 * a + k - 1],
                    device_id=_chip_peer(k), device_id_type=MESH_ID)
                cp.start()
                remote.append((a, k, cp))
        for a, k, cp in remote:
            cp.wait_send()
            pltpu.make_async_remote_copy(
                src_ref=ins[a], dst_ref=outs[a].at[s ^ k], send_sem=send.at[3 * a + k - 1], recv_sem=recv.at[3 * a + k - 1],
                device_id=_chip_peer(k), device_id_type=MESH_ID).wait_recv()
        for cp in copies:
            cp.wait()

    return pl.pallas_call(
        body, name=name, in_specs=[_ANY] * n, out_specs=[_ANY] * n,
        out_shape=[jax.ShapeDtypeStruct((4,) + a.shape, a.dtype) for a in shards],
        scratch_shapes=[pltpu.SemaphoreType.DMA((3 * n,)), pltpu.SemaphoreType.DMA((3 * n,)), pltpu.SemaphoreType.DMA((n,))],
        compiler_params=pltpu.CompilerParams(has_side_effects=True),
    )(*shards)


def _scatter_to_chips(parts, *, name):
    n = len(parts)

    def body(*refs):
        ins, outs = refs[:n], refs[n:2 * n]
        send, recv, loc = refs[2 * n:]
        s = _my_chip()
        copies = []
        for a in range(n):
            cp = pltpu.make_async_copy(ins[a].at[s], outs[a].at[s], loc.at[a])
            cp.start()
            copies.append(cp)
        remote = []
        for a in range(n):
            for k in (1, 2, 3):
                cp = pltpu.make_async_remote_copy(
                    src_ref=ins[a].at[s ^ k], dst_ref=outs[a].at[s], send_sem=send.at[3 * a + k - 1],
                    recv_sem=recv.at[3 * a + k - 1], device_id=_chip_peer(k), device_id_type=MESH_ID)
                cp.start()
                remote.append((a, k, cp))
        for a, k, cp in remote:
            cp.wait_send()
            pltpu.make_async_remote_copy(
                src_ref=ins[a].at[s ^ k], dst_ref=outs[a].at[s ^ k], send_sem=send.at[3 * a + k - 1],
                recv_sem=recv.at[3 * a + k - 1], device_id=_chip_peer(k), device_id_type=MESH_ID).wait_recv()
        for cp in copies:
            cp.wait()

    return pl.pallas_call(
        body, name=name, in_specs=[_ANY] * n, out_specs=[_ANY] * n,
        out_shape=[jax.ShapeDtypeStruct(a.shape, a.dtype) for a in parts],
        scratch_shapes=[pltpu.SemaphoreType.DMA((3 * n,)), pltpu.SemaphoreType.DMA((3 * n,)), pltpu.SemaphoreType.DMA((n,))],
        compiler_params=pltpu.CompilerParams(has_side_effects=True),
    )(*parts)


def _swap_with_sibling(arrs, *, name):
    n = len(arrs)

    def body(*refs):
        ins, outs = refs[:n], refs[n:2 * n]
        send, recv = refs[2 * n:]
        sib = (lax.axis_index("x"), lax.axis_index("y"), 1 - lax.axis_index("c"))
        cps = []
        for a in range(n):
            cp = pltpu.make_async_remote_copy(src_ref=ins[a], dst_ref=outs[a], send_sem=send.at[a], recv_sem=recv.at[a],
                                              device_id=sib, device_id_type=MESH_ID)
            cp.start()
            cps.append(cp)
        for cp in cps:
            cp.wait()

    return pl.pallas_call(
        body, name=name, in_specs=[_ANY] * n, out_specs=[_ANY] * n,
        out_shape=[jax.ShapeDtypeStruct(a.shape, a.dtype) for a in arrs],
        scratch_shapes=[pltpu.SemaphoreType.DMA((n,)), pltpu.SemaphoreType.DMA((n,))],
        compiler_params=pltpu.CompilerParams(has_side_effects=True),
    )(*arrs)


def _all_gather_devices(v, *, name):
    r = v.shape[0]

    def body(v_ref, out_ref, send, recv):
        x, y, c = lax.axis_index("x"), lax.axis_index("y"), lax.axis_index("c")
        me = 4 * x + 2 * y + c
        out_ref[me] = v_ref[...]
        cps = []
        for k in range(1, 8):
            peer = (x ^ (k >> 2), y ^ ((k >> 1) & 1), c ^ (k & 1))
            cp = pltpu.make_async_remote_copy(src_ref=v_ref, dst_ref=out_ref.at[me], send_sem=send.at[k - 1],
                                              recv_sem=recv.at[k - 1], device_id=peer, device_id_type=MESH_ID)
            cp.start()
            cps.append(cp)
        for k, cp in enumerate(cps, start=1):
            cp.wait_send()
            pltpu.make_async_remote_copy(src_ref=v_ref, dst_ref=out_ref.at[me ^ k], send_sem=send.at[k - 1],
                                         recv_sem=recv.at[k - 1], device_id=(x, y, c), device_id_type=MESH_ID).wait_recv()

    vm = pl.BlockSpec(memory_space=pltpu.VMEM)
    return pl.pallas_call(
        body, name=name, in_specs=[vm], out_specs=vm, out_shape=jax.ShapeDtypeStruct((8, r, 128), F32),
        scratch_shapes=[pltpu.SemaphoreType.DMA((7,)), pltpu.SemaphoreType.DMA((7,))],
        compiler_params=pltpu.CompilerParams(has_side_effects=True),
    )(v)


def _row_tile(r, target):
    best = None
    for t in range(16, min(target, r) + 1, 16):
        if r % t == 0:
            best = t
    return best or r


def _sum_slots(buf, *, name, tr=384):
    S, r, c = buf.shape
    tr = _row_tile(r, tr)

    def body(b_ref, o_ref):
        acc = b_ref[0].astype(F32)
        for j in range(1, S):
            acc = acc + b_ref[j].astype(F32)
        o_ref[...] = acc

    return pl.pallas_call(
        body, name=name, grid=(r // tr,), in_specs=[pl.BlockSpec((S, tr, c), lambda i: (0, i, 0))],
        out_specs=pl.BlockSpec((tr, c), lambda i: (i, 0)), out_shape=jax.ShapeDtypeStruct((r, c), F32),
        compiler_params=_cparams("parallel"),
    )(buf)


def _add2(a, b, *, name, tr=384):
    r, c = a.shape
    tr = _row_tile(r, tr)

    def body(a_ref, b_ref, o_ref):
        o_ref[...] = a_ref[...] + b_ref[...]

    spec = pl.BlockSpec((tr, c), lambda i: (i, 0))
    return pl.pallas_call(body, name=name, grid=(r // tr,), in_specs=[spec, spec], out_specs=spec,
                          out_shape=jax.ShapeDtypeStruct((r, c), F32), compiler_params=_cparams("parallel"))(a, b)


def _adamw(w, g, m, v, *, name, tr=256):
    r, c = w.shape
    tr = _row_tile(r, tr)
    bc1 = 1.0 - ADAM_B1 ** ADAM_STEP
    bc2 = 1.0 - ADAM_B2 ** ADAM_STEP

    def body(w_ref, g_ref, m_ref, v_ref, d_ref, mo_ref, vo_ref):
        gg = g_ref[...]
        mn = ADAM_B1 * m_ref[...] + (1.0 - ADAM_B1) * gg
        vn = ADAM_B2 * v_ref[...] + (1.0 - ADAM_B2) * (gg * gg)
        mo_ref[...] = mn
        vo_ref[...] = vn
        d_ref[...] = -ADAM_LR * ((mn / bc1) / (jnp.sqrt(vn / bc2) + ADAM_EPS) + ADAM_WD * w_ref[...])

    spec = pl.BlockSpec((tr, c), lambda i: (i, 0))
    return pl.pallas_call(body, name=name, grid=(r // tr,), in_specs=[spec] * 4, out_specs=[spec] * 3,
                          out_shape=[jax.ShapeDtypeStruct((r, c), F32)] * 3, compiler_params=_cparams("parallel"))(w, g, m, v)


def _pack(vecs, rows):
    flat = jnp.concatenate([v.reshape(-1).astype(F32) for v in vecs])
    return jnp.pad(flat, (0, rows * 128 - flat.shape[0])).reshape(rows, 128)


def _unpack(packed, shapes):
    flat = packed.reshape(-1)
    out, off = [], 0
    for s in shapes:
        n = math.prod(s)
        out.append(flat[off:off + n].reshape(s))
        off += n
    return out


def _pack_rows(shapes):
    n = sum(math.prod(s) for s in shapes)
    return -(-n // 1024) * 8


def kernel(x, norm_mix_pre, norm_mix_post, norm_ffn_pre, norm_ffn_post, ret_w_in, ret_gn_w, ret_w_out, ssd_w_in, ssd_conv_w, ssd_conv_b, ssd_dt_bias, ssd_a_log, ssd_d, ssd_norm_w, ssd_w_out, mlp_w_up, mlp_w_down, loss_target, m_norm_mix_pre, m_norm_mix_post, m_norm_ffn_pre, m_norm_ffn_post, m_ret_w_in, m_ret_gn_w, m_ret_w_out, m_ssd_w_in, m_ssd_conv_w, m_ssd_conv_b, m_ssd_dt_bias, m_ssd_a_log, m_ssd_d, m_ssd_norm_w, m_ssd_w_out, m_mlp_w_up, m_mlp_w_down, v_norm_mix_pre, v_norm_mix_post, v_norm_ffn_pre, v_norm_ffn_post, v_ret_w_in, v_ret_gn_w, v_ret_w_out, v_ssd_w_in, v_ssd_conv_w, v_ssd_conv_b, v_ssd_dt_bias, v_ssd_a_log, v_ssd_d, v_ssd_norm_w, v_ssd_w_out, v_mlp_w_up, v_mlp_w_down):
    T, D = x.shape[1], x.shape[2]
    H = D // RET_DK
    d_inner = 2 * D
    R = d_inner // SSD_P // SSD_G
    RP = R * SSD_P
    n_heads = SSD_G * R
    conv_dim = d_inner + 2 * SSD_G * SSD_N
    n_main = d_inner + conv_dim
    C = min(256, T)
    chip = _my_chip()
    xs, tgt = x[0], loss_target[0]

    conv_sh = ssd_conv_w.shape[2]
    small_shapes = [(SSD_CONV_W, conv_sh), (conv_sh,), (ssd_norm_w.shape[1],)]
    small_rows = _pack_rows(small_shapes)
    shards = [ret_w_in[0].T.astype(BF16), ret_w_out[0].astype(BF16), ssd_w_in[0].T.astype(BF16), ssd_w_out[0].astype(BF16),
              mlp_w_up[0].T.astype(BF16), mlp_w_up[1].T.astype(BF16), mlp_w_down[0].astype(BF16), mlp_w_down[1].astype(BF16),
              _pack([ssd_conv_w[0], ssd_conv_b[0], ssd_norm_w[0]], small_rows)]
    gathered = _all_gather_chips(shards, name="gather_weights")
    full = [g.reshape(4 * g.shape[1], g.shape[2]) for g in gathered[:8]]
    ret_in_t, ret_out, ssd_in_t, ssd_out, up_t0, up_t1, down0, down1 = full
    up_t, down = (up_t0, up_t1), (down0, down1)
    ssd_main_t, ssd_dt_t = ssd_in_t[:n_main], ssd_in_t[n_main:]
    sm = [_unpack(gathered[8][j], small_shapes) for j in range(4)]
    conv_w = jnp.concatenate([sm[j][0] for j in range(4)], axis=1)
    conv_b = jnp.concatenate([sm[j][1] for j in range(4)])[None, :]
    norm_w = jnp.concatenate([sm[j][2] for j in range(4)])[None, :]

    gb = SSD_G * SSD_N
    ssd_prm = (ssd_dt_bias.reshape(SSD_G, 1, R), ssd_dt_bias.reshape(SSD_G, R, 1),
               ssd_a_log.reshape(SSD_G, 1, R), ssd_a_log.reshape(SSD_G, R, 1), ssd_d.reshape(SSD_G, 1, R),
               conv_w[:, :d_inner], conv_w[:, d_inner:d_inner + gb], conv_w[:, d_inner + gb:],
               conv_b[:, :d_inner], conv_b[:, d_inner:d_inner + gb], conv_b[:, d_inner + gb:],
               norm_w, jnp.asarray(np.kron(np.eye(R), np.ones((1, SSD_P))), F32))
    ret_consts = _ret_consts(T, C, H)

    u0 = _rms_pre(xs, norm_mix_pre[0:1], name="pre0")
    proj = _matmul(u0, ret_in_t, "nt", out_dtype=F32, name="ret_in")
    y_ret, st_ret = _ret_fwd(proj, ret_gn_w, ret_consts, C=C, name="ret_fwd")
    m0 = _matmul(y_ret, ret_out, "nn", out_dtype=F32, name="ret_out")
    h1, u1 = _rms_post_pre(xs, m0, norm_mix_post[0:1], norm_ffn_pre[0:1], name="post_pre1")
    a0, hh0 = _matmul(u1, up_t[0], "nt", out_dtype=BF16, name="mlp_up0", epi="relu2")
    f0 = _matmul(hh0, down[0], "nn", out_dtype=F32, name="mlp_down0")
    h2, u2 = _rms_post_pre(h1, f0, norm_ffn_post[0:1], norm_mix_pre[1:2], name="post_pre2")
    pm = _matmul(u2, ssd_main_t, "nt", out_dtype=F32, name="ssd_in")
    pdt = _matmul(u2, ssd_dt_t, "nt", out_dtype=F32, name="ssd_in_dt")
    dt_g = pdt.reshape(T, SSD_G, R).transpose(1, 0, 2)
    dtT_g = pdt.reshape(T, SSD_G, R).transpose(1, 2, 0)
    y_ssd, st_ssd = _ssd_fwd(pm, dt_g, dtT_g, ssd_prm, C=C, R=R, name="ssd_fwd")
    m1 = _matmul(y_ssd, ssd_out, "nn", out_dtype=F32, name="ssd_out")
    h3, u3 = _rms_post_pre(h2, m1, norm_mix_post[1:2], norm_ffn_pre[1:2], name="post_pre3")
    a1, hh1 = _matmul(u3, up_t[1], "nt", out_dtype=BF16, name="mlp_up1", epi="relu2")
    f1 = _matmul(hh1, down[1], "nn", out_dtype=F32, name="mlp_down1")
    dh4, sq = _rms_post_loss(h3, f1, norm_ffn_post[1:2], tgt, name="post_loss")
    loss = lax.psum(sq[0, 0], MESH_AXES) * (0.5 / D)

    def mlp_bwd(i, dh_out, h_in, u, a, hh, f):
        df, d_post = _rms_post_bwd(f, norm_ffn_post[i:i + 1], dh_out, name=f"post_bwd_ffn{i}")
        g_down = _matmul(hh, df, "tn", out_dtype=BF16, name=f"mlp_down_wg{i}")
        da = _matmul(df, down[i], "nt", out_dtype=BF16, name=f"mlp_down_dg{i}", epi="drelu2", extra=a)
        g_up_t = _matmul(da, u, "tn", out_dtype=BF16, name=f"mlp_up_wg{i}")
        du = _matmul(da, up_t[i], "nn", out_dtype=F32, name=f"mlp_up_dg{i}")
        dh, d_pre = _rms_pre_bwd(h_in, norm_ffn_pre[i:i + 1], du, dh_out, name=f"pre_bwd_ffn{i}")
        return dh, g_up_t, g_down, d_pre, d_post

    dh3, g_up_t1, g_down1, d_nfp1, d_nfpost1 = mlp_bwd(1, dh4, h3, u3, a1, hh1, f1)
    dm1, d_nmpost1 = _rms_post_bwd(m1, norm_mix_post[1:2], dh3, name="post_bwd_mix1")
    g_ssd_out = _matmul(y_ssd, dm1, "tn", out_dtype=BF16, name="ssd_out_wg")
    dy_ssd = _matmul(dm1, ssd_out, "nt", out_dtype=F32, name="ssd_out_dg")
    (dz, dxr, dbr, dcr, ddt_g, d_bias, d_alog, d_dskip, dcwx, dcwb, dcwc, dcbx, dcbb, dcbc, d_normw) = _ssd_bwd(
        pm, dt_g, dtT_g, ssd_prm, st_ssd, dy_ssd, C=C, R=R, name="ssd_bwd")
    dpm = jnp.concatenate([dz, dxr, dbr, dcr], axis=1)
    dpdt = ddt_g.transpose(1, 0, 2).reshape(T, n_heads).astype(BF16)
    g_ssd_in_t = jnp.concatenate([_matmul(dpm, u2, "tn", out_dtype=BF16, name="ssd_in_wg"),
                                  _matmul(dpdt, u2, "tn", out_dtype=BF16, name="ssd_in_dt_wg")], axis=0)
    du2 = _matmul(dpm, ssd_main_t, "nn", out_dtype=F32, name="ssd_in_dg")
    du2 = _matmul(dpdt, ssd_dt_t, "nn", out_dtype=F32, name="ssd_in_dt_dg", epi="add", extra=du2)
    dh2, d_nmp1 = _rms_pre_bwd(h2, norm_mix_pre[1:2], du2, dh3, name="pre_bwd_mix1")
    dh1, g_up_t0, g_down0, d_nfp0, d_nfpost0 = mlp_bwd(0, dh2, h1, u1, a0, hh0, f0)
    dm0, d_nmpost0 = _rms_post_bwd(m0, norm_mix_post[0:1], dh1, name="post_bwd_mix0")
    g_ret_out = _matmul(y_ret, dm0, "tn", out_dtype=BF16, name="ret_out_wg")
    dy_ret = _matmul(dm0, ret_out, "nt", out_dtype=F32, name="ret_out_dg")
    dq, dk, dv, dg, d_gn = _ret_bwd(proj, ret_gn_w, ret_consts, st_ret, dy_ret, C=C, name="ret_bwd")
    dproj = jnp.concatenate([dq, dk, dv, dg], axis=1)
    g_ret_in_t = _matmul(dproj, u0, "tn", out_dtype=BF16, name="ret_in_wg")
    du0 = _matmul(dproj, ret_in_t, "nn", out_dtype=F32, name="ret_in_dg")
    grad_x, d_nmp0 = _rms_pre_bwd(xs, norm_mix_pre[0:1], du0, dh1, name="pre_bwd_mix0")

    big = [g_ret_in_t, g_ret_out, g_ssd_in_t, g_ssd_out, g_up_t0, g_up_t1, g_down0, g_down1]
    parts = [g.reshape(4, g.shape[0] // 4, g.shape[1]) for g in big]
    landed = _scatter_to_chips(parts, name="scatter_grads")
    partial = [_sum_slots(b, name=f"sum_chips{n}") for n, b in enumerate(landed)]
    other = _swap_with_sibling(partial, name="swap_partials")
    summed = [_add2(p, o, name=f"sum_cores{n}") for n, (p, o) in enumerate(zip(partial, other))]
    transposed = (True, False, True, False, True, True, False, False)
    gshard = [s.T if t else s for s, t in zip(summed, transposed)]
    g_ret_in, g_ret_out_s, g_ssd_in, g_ssd_out_s, g_up0, g_up1, g_dn0, g_dn1 = gshard

    def upd(w, g, m, v, name):
        shp = w.shape
        w2, g2, m2, v2 = (t.reshape(-1, shp[-1]) for t in (w, g, m, v))
        d, mn, vn = _adamw(w2, g2, m2, v2, name=name)
        return g.reshape(shp), d.reshape(shp), mn.reshape(shp), vn.reshape(shp)

    res = {}
    res["ret_w_in"] = upd(ret_w_in, g_ret_in[None], m_ret_w_in, v_ret_w_in, "adamw_ret_in")
    res["ret_w_out"] = upd(ret_w_out, g_ret_out_s[None], m_ret_w_out, v_ret_w_out, "adamw_ret_out")
    res["ssd_w_in"] = upd(ssd_w_in, g_ssd_in[None], m_ssd_w_in, v_ssd_w_in, "adamw_ssd_in")
    res["ssd_w_out"] = upd(ssd_w_out, g_ssd_out_s[None], m_ssd_w_out, v_ssd_w_out, "adamw_ssd_out")
    res["mlp_w_up"] = upd(mlp_w_up, jnp.stack([g_up0, g_up1]), m_mlp_w_up, v_mlp_w_up, "adamw_up")
    res["mlp_w_down"] = upd(mlp_w_down, jnp.stack([g_dn0, g_dn1]), m_mlp_w_down, v_mlp_w_down, "adamw_down")

    d_conv_w = jnp.concatenate([dcwx, dcwb, dcwc], axis=1)
    d_conv_b = jnp.concatenate([dcbx, dcbb, dcbc], axis=1)
    small_grads = [jnp.concatenate([d_nmp0, d_nmp1]), jnp.concatenate([d_nmpost0, d_nmpost1]),
                   jnp.concatenate([d_nfp0, d_nfp1]), jnp.concatenate([d_nfpost0, d_nfpost1]),
                   d_gn, d_bias.reshape(1, n_heads), d_alog.reshape(1, n_heads), d_dskip.reshape(1, n_heads),
                   d_conv_w, d_conv_b, d_normw]
    sg_shapes = [g.shape for g in small_grads]
    sg_rows = _pack_rows(sg_shapes)
    everyone = _all_gather_devices(_pack(small_grads, sg_rows), name="gather_small_grads")
    sg = _unpack(_sum_slots(everyone, name="sum_small_grads", tr=sg_rows), sg_shapes)
    (g_nmp, g_nmpost, g_nfp, g_nfpost, g_gn, g_bias, g_alog, g_dskip, g_cw_full, g_cb_full, g_nw_full) = sg
    g_cw = lax.dynamic_slice_in_dim(g_cw_full, chip * conv_sh, conv_sh, axis=1)[None]
    g_cb = lax.dynamic_slice_in_dim(g_cb_full, chip * conv_sh, conv_sh, axis=1)
    nw_sh = ssd_norm_w.shape[1]
    g_nw = lax.dynamic_slice_in_dim(g_nw_full, chip * nw_sh, nw_sh, axis=1)
    small = [("norm_mix_pre", norm_mix_pre, g_nmp, m_norm_mix_pre, v_norm_mix_pre),
             ("norm_mix_post", norm_mix_post, g_nmpost, m_norm_mix_post, v_norm_mix_post),
             ("norm_ffn_pre", norm_ffn_pre, g_nfp, m_norm_ffn_pre, v_norm_ffn_pre),
             ("norm_ffn_post", norm_ffn_post, g_nfpost, m_norm_ffn_post, v_norm_ffn_post),
             ("ret_gn_w", ret_gn_w, g_gn, m_ret_gn_w, v_ret_gn_w),
             ("ssd_conv_w", ssd_conv_w, g_cw, m_ssd_conv_w, v_ssd_conv_w),
             ("ssd_conv_b", ssd_conv_b, g_cb, m_ssd_conv_b, v_ssd_conv_b),
             ("ssd_dt_bias", ssd_dt_bias, g_bias, m_ssd_dt_bias, v_ssd_dt_bias),
             ("ssd_a_log", ssd_a_log, g_alog, m_ssd_a_log, v_ssd_a_log),
             ("ssd_d", ssd_d, g_dskip, m_ssd_d, v_ssd_d),
             ("ssd_norm_w", ssd_norm_w, g_nw, m_ssd_norm_w, v_ssd_norm_w)]
    sw_shapes = [w.shape for _, w, _, _, _ in small]
    sw_rows = _pack_rows(sw_shapes)
    packs = [_pack([t[j] for t in small], sw_rows) for j in (1, 2, 3, 4)]
    d_p, m_p, v_p = _adamw(*packs, name="adamw_small", tr=sw_rows)
    d_s, m_s, v_s = _unpack(d_p, sw_shapes), _unpack(m_p, sw_shapes), _unpack(v_p, sw_shapes)
    for j, (nm, w, g, _, _) in enumerate(small):
        res[nm] = (g.reshape(w.shape), d_s[j], m_s[j], v_s[j])

    order = ["norm_mix_pre", "norm_mix_post", "norm_ffn_pre", "norm_ffn_post", "ret_w_in", "ret_gn_w", "ret_w_out",
             "ssd_w_in", "ssd_conv_w", "ssd_conv_b", "ssd_dt_bias", "ssd_a_log", "ssd_d", "ssd_norm_w", "ssd_w_out",
             "mlp_w_up", "mlp_w_down"]
    return (loss, grad_x[None], *[res[n][0] for n in order], *[res[n][1] for n in order],
            *[res[n][2] for n in order], *[res[n][3] for n in order])
```

```python
import functools
import math

import numpy as np
import jax
import jax.numpy as jnp
from jax import lax
from jax.experimental import pallas as pl
from jax.experimental.pallas import tpu as pltpu

F32 = jnp.float32
BF16 = jnp.bfloat16
HI = lax.Precision.HIGHEST
VMEM_LIMIT_BYTES = 56 * 1024 * 1024
MESH_AXES = ("x", "y", "c")
MESH_ID = pl.DeviceIdType.MESH

RMS_EPS = 1e-6
GN_EPS = 1e-5
RET_DK = 256
RET_DV = 512
ROPE_BASE = 10000.0
REF_CHUNK = 64
SSD_P = 64
SSD_N = 128
SSD_G = 8
SSD_CONV_W = 4
ADAM_LR, ADAM_B1, ADAM_B2, ADAM_EPS, ADAM_WD, ADAM_STEP = 0.001, 0.9, 0.999, 1e-08, 0.01, 10

NN = (((1,), (0,)), ((), ()))
NT = (((1,), (1,)), ((), ()))
TN = (((0,), (0,)), ((), ()))


def _cparams(*sem):
    return pltpu.CompilerParams(dimension_semantics=sem, vmem_limit_bytes=VMEM_LIMIT_BYTES)


def _dot(a, b, dims=NN):
    return lax.dot_general(a.astype(BF16), b.astype(BF16), dims, preferred_element_type=F32)


def _split_bf16(x, terms):
    parts, rest = [], x
    for _ in range(terms):
        p = rest.astype(BF16)
        parts.append(p)
        rest = rest - p.astype(F32)
    return parts


def _dot_sel(a, b, dims=NN, *, split, terms=3):
    if split == "a":
        sel = b.astype(BF16)
        return sum(lax.dot_general(p, sel, dims, preferred_element_type=F32) for p in _split_bf16(a, terms))
    sel = a.astype(BF16)
    return sum(lax.dot_general(sel, p, dims, preferred_element_type=F32) for p in _split_bf16(b, terms))


def _sigmoid(x):
    return 1.0 / (1.0 + jnp.exp(-x))


def _colsum(x):
    return jnp.sum(x, axis=0, keepdims=True)


MM_TILE = 1024
MM_FULL_K = 2048


def _mm_tiles(M, N, K):
    tm = min(M, MM_TILE)
    if K <= MM_FULL_K:
        return tm, min(N, MM_TILE), K
    return tm, min(N, 2 * MM_TILE), MM_TILE


def _matmul(a, b, mode, *, out_dtype, name, epi=None, extra=None, after=None):
    if mode == "nn":
        (M, K), (K2, N) = a.shape, b.shape
    elif mode == "nt":
        (M, K), (N, K2) = a.shape, b.shape
    else:
        (K, M), (K2, N) = a.shape, b.shape
    assert K == K2, (a.shape, b.shape, mode)
    tm, tn, tk = _mm_tiles(M, N, K)
    assert M % tm == 0 and N % tn == 0 and K % tk == 0, (M, N, K, tm, tn, tk)
    nk = K // tk
    if mode == "tn":
        a_spec = pl.BlockSpec((tk, tm), lambda i, j, k: (k, i))
    else:
        a_spec = pl.BlockSpec((tm, tk), lambda i, j, k: (i, k))
    if mode == "nt":
        b_spec = pl.BlockSpec((tn, tk), lambda i, j, k: (j, k))
    else:
        b_spec = pl.BlockSpec((tk, tn), lambda i, j, k: (k, j))
    dims = {"nn": NN, "nt": NT, "tn": TN}[mode]
    o_spec = pl.BlockSpec((tm, tn), lambda i, j, k: (i, j))
    has_extra = epi in ("drelu2", "add")
    n_out = 2 if epi == "relu2" else 1
    n_in = 2 + int(has_extra) + int(after is not None)

    def body(*refs):
        a_ref, b_ref = refs[0], refs[1]
        e_ref = refs[2] if has_extra else None
        outs = refs[n_in:n_in + n_out]

        def finish(r):
            if epi is None:
                outs[0][...] = r.astype(outs[0].dtype)
            elif epi == "relu2":
                outs[0][...] = r.astype(outs[0].dtype)
                h = jnp.maximum(r, 0.0)
                outs[1][...] = (h * h).astype(outs[1].dtype)
            elif epi == "drelu2":
                av = jnp.maximum(e_ref[...].astype(F32), 0.0)
                outs[0][...] = (r * (2.0 * av)).astype(outs[0].dtype)
            else:
                outs[0][...] = (r + e_ref[...].astype(F32)).astype(outs[0].dtype)

        if nk == 1:
            finish(_dot(a_ref[...], b_ref[...], dims))
            return
        acc = refs[-1]
        k = pl.program_id(2)

        @pl.when(k == 0)
        def _():
            acc[...] = jnp.zeros_like(acc)

        acc[...] += _dot(a_ref[...], b_ref[...], dims)

        @pl.when(k == nk - 1)
        def _():
            finish(acc[...])

    in_specs = [a_spec, b_spec] + ([o_spec] if has_extra else [])
    args = [a, b] + ([extra] if has_extra else [])
    if after is not None:
        in_specs.append(pl.BlockSpec(after.shape, lambda i, j, k: (0, 0)))
        args.append(after)
    out_shape = [jax.ShapeDtypeStruct((M, N), out_dtype)] * n_out
    res = pl.pallas_call(
        body, name=name, grid=(M // tm, N // tn, nk), in_specs=in_specs, out_specs=[o_spec] * n_out,
        out_shape=out_shape, scratch_shapes=[pltpu.VMEM((tm, tn), F32)] if nk > 1 else [],
        compiler_params=_cparams("parallel", "parallel", "arbitrary"),
    )(*args)
    return res if n_out == 2 else res[0]


def _rstd(x):
    return lax.rsqrt(jnp.mean(x * x, axis=-1, keepdims=True) + RMS_EPS)


def _row_call(body, ins, outs_shape, *, name, rows, tr, acc_outs=()):
    tr = min(tr, rows)
    assert rows % tr == 0
    in_specs = []
    for arr, blocked in ins:
        if blocked:
            in_specs.append(pl.BlockSpec((tr, arr.shape[1]), lambda i: (i, 0)))
        else:
            in_specs.append(pl.BlockSpec(arr.shape, lambda i: (0, 0)))
    out_specs = []
    for n, s in enumerate(outs_shape):
        if n in acc_outs:
            out_specs.append(pl.BlockSpec(s.shape, lambda i: (0, 0)))
        else:
            out_specs.append(pl.BlockSpec((tr, s.shape[1]), lambda i: (i, 0)))
    return pl.pallas_call(
        body, name=name, grid=(rows // tr,), in_specs=in_specs, out_specs=out_specs, out_shape=outs_shape,
        compiler_params=_cparams("arbitrary" if acc_outs else "parallel"),
    )(*[a for a, _ in ins])


def _rms_pre(h, w, *, name):
    T, D = h.shape

    def body(h_ref, w_ref, u_ref):
        x = h_ref[...]
        u_ref[...] = (x * _rstd(x) * w_ref[...]).astype(BF16)

    return _row_call(body, [(h, True), (w, False)], [jax.ShapeDtypeStruct((T, D), BF16)], name=name, rows=T, tr=256)[0]


def _rms_post_pre(h, m, w_post, w_pre, *, name):
    T, D = h.shape

    def body(h_ref, m_ref, wp_ref, wn_ref, hn_ref, u_ref):
        mm = m_ref[...]
        hn = h_ref[...] + mm * _rstd(mm) * wp_ref[...]
        hn_ref[...] = hn
        u_ref[...] = (hn * _rstd(hn) * wn_ref[...]).astype(BF16)

    return _row_call(body, [(h, True), (m, True), (w_post, False), (w_pre, False)],
                     [jax.ShapeDtypeStruct((T, D), F32), jax.ShapeDtypeStruct((T, D), BF16)], name=name, rows=T, tr=256)


def _rms_post_loss(h, m, w_post, tgt, *, name):
    T, D = h.shape

    def body(h_ref, m_ref, wp_ref, t_ref, dh_ref, loss_ref):
        @pl.when(pl.program_id(0) == 0)
        def _():
            loss_ref[...] = jnp.zeros_like(loss_ref)

        mm = m_ref[...]
        err = h_ref[...] + mm * _rstd(mm) * wp_ref[...] - t_ref[...]
        dh_ref[...] = err * (1.0 / D)
        loss_ref[...] += _colsum(jnp.sum(err * err, axis=1, keepdims=True))

    return _row_call(body, [(h, True), (m, True), (w_post, False), (tgt, True)],
                     [jax.ShapeDtypeStruct((T, D), F32), jax.ShapeDtypeStruct((1, 1), F32)],
                     name=name, rows=T, tr=256, acc_outs=(1,))


def _rms_bwd_vals(x, w, dy):
    r = _rstd(x)
    xh = x * r
    g = dy * w
    dx = r * (g - xh * jnp.mean(g * xh, axis=-1, keepdims=True))
    return dx, _colsum(dy * xh)


def _rms_post_bwd(m, w_post, dh, *, name):
    T, D = m.shape

    def body(m_ref, w_ref, dh_ref, dm_ref, dw_ref):
        @pl.when(pl.program_id(0) == 0)
        def _():
            dw_ref[...] = jnp.zeros_like(dw_ref)

        dx, dw = _rms_bwd_vals(m_ref[...], w_ref[...], dh_ref[...])
        dm_ref[...] = dx.astype(BF16)
        dw_ref[...] += dw

    return _row_call(body, [(m, True), (w_post, False), (dh, True)],
                     [jax.ShapeDtypeStruct((T, D), BF16), jax.ShapeDtypeStruct((1, D), F32)],
                     name=name, rows=T, tr=256, acc_outs=(1,))


def _rms_pre_bwd(h, w_pre, du, dh_out, *, name):
    T, D = h.shape

    def body(h_ref, w_ref, du_ref, dho_ref, dh_ref, dw_ref):
        @pl.when(pl.program_id(0) == 0)
        def _():
            dw_ref[...] = jnp.zeros_like(dw_ref)

        dx, dw = _rms_bwd_vals(h_ref[...], w_ref[...], du_ref[...])
        dh_ref[...] = dho_ref[...] + dx
        dw_ref[...] += dw

    return _row_call(body, [(h, True), (w_pre, False), (du, True), (dh_out, True)],
                     [jax.ShapeDtypeStruct((T, D), F32), jax.ShapeDtypeStruct((1, D), F32)],
                     name=name, rows=T, tr=256, acc_outs=(1,))


def _ret_consts(T, C, H):
    lg = np.log1p(-np.exp2(-5.0 - np.arange(H, dtype=np.float64)))
    idx = np.arange(C, dtype=np.float64)
    dist = np.abs(idx[:, None] - idx[None, :])
    vis = (idx[None, :] // REF_CHUNK) <= (idx[:, None] // REF_CHUNK)
    mask = np.exp(dist[None] * lg[:, None, None]) * vis[None]
    xi = np.exp((idx[None, :] + 1.0) * lg[:, None])[..., None]
    zeta = np.exp((C - 1.0 - idx)[None, :] * lg[:, None])[..., None]
    half = RET_DK // 2
    inv_freq = ROPE_BASE ** (-np.arange(half, dtype=np.float32) / np.float32(half))
    ang = np.arange(T, dtype=np.float32)[:, None] * inv_freq[None, :].astype(np.float32)
    return (jnp.asarray(mask, F32), jnp.asarray(xi, F32), jnp.asarray(zeta, F32),
            jnp.asarray(np.cos(ang), F32), jnp.asarray(np.sin(ang), F32))


def _rot(t, cos, sin):
    half = RET_DK // 2
    t1, t2 = t[:, :half], t[:, half:]
    return jnp.concatenate([t1 * cos - t2 * sin, t1 * sin + t2 * cos], axis=1)


def _unrot(d, cos, sin):
    half = RET_DK // 2
    d1, d2 = d[:, :half], d[:, half:]
    return jnp.concatenate([d1 * cos + d2 * sin, d2 * cos - d1 * sin], axis=1)


def _ret_specs(C, H, rev, NS):
    def ci(i):
        return NS - 1 - i if rev else i

    nq = H
    q_spec = pl.BlockSpec((C, RET_DK), lambda h, i: (ci(i), h))
    k_spec = pl.BlockSpec((C, RET_DK), lambda h, i: (ci(i), nq + h))
    v_spec = pl.BlockSpec((C, RET_DV), lambda h, i: (ci(i), H + h))
    g_spec = pl.BlockSpec((C, RET_DV), lambda h, i: (ci(i), 2 * H + h))
    cs_spec = pl.BlockSpec((C, RET_DK // 2), lambda h, i: (ci(i), 0))
    m_spec = pl.BlockSpec((None, C, C), lambda h, i: (h, 0, 0))
    vec_spec = pl.BlockSpec((None, C, 1), lambda h, i: (h, 0, 0))
    gn_spec = pl.BlockSpec((1, RET_DV), lambda h, i: (0, h))
    st_spec = pl.BlockSpec((None, None, RET_DK, RET_DV), lambda h, i: (h, ci(i), 0, 0))
    return q_spec, k_spec, v_spec, g_spec, cs_spec, m_spec, vec_spec, gn_spec, st_spec


def _ret_fwd_vals(q, k, v, cos, sin, mask, xi, s_in):
    qr = _rot(q, cos, sin)
    kr = _rot(k, cos, sin) * (RET_DK ** -0.5)
    a = _dot(qr, kr, NT) * mask
    o = _dot(a, v) + _dot(qr, s_in) * xi
    mu = jnp.mean(o, axis=1, keepdims=True)
    oc = o - mu
    rstd = lax.rsqrt(jnp.mean(oc * oc, axis=1, keepdims=True) + GN_EPS)
    return qr, kr, a, oc * rstd, rstd


def _ret_fwd(proj, gn_w, consts, *, C, name):
    T = proj.shape[0]
    H = gn_w.shape[1] // RET_DV
    NS = T // C
    mask, xi, zeta, cos, sin = consts
    q_spec, k_spec, v_spec, g_spec, cs_spec, m_spec, vec_spec, gn_spec, st_spec = _ret_specs(C, H, False, NS)
    y_spec = pl.BlockSpec((C, RET_DV), lambda h, i: (i, h))

    def body(q_ref, k_ref, v_ref, g_ref, cos_ref, sin_ref, m_ref, xi_ref, ze_ref, gn_ref, y_ref, st_ref, S):
        @pl.when(pl.program_id(1) == 0)
        def _():
            S[...] = jnp.zeros_like(S)

        s_in = S[...]
        st_ref[...] = s_in
        v = v_ref[...]
        xi_v = xi_ref[...]
        qr, kr, a, on, rstd = _ret_fwd_vals(q_ref[...], k_ref[...], v, cos_ref[...], sin_ref[...], m_ref[...], xi_v, s_in)
        g = g_ref[...]
        y_ref[...] = (g * _sigmoid(g) * on * gn_ref[...]).astype(BF16)
        S[...] = s_in * xi_v[C - 1:C, :] + _dot(kr * ze_ref[...], v, TN)

    return pl.pallas_call(
        body, name=name, grid=(H, NS),
        in_specs=[q_spec, k_spec, v_spec, g_spec, cs_spec, cs_spec, m_spec, vec_spec, vec_spec, gn_spec],
        out_specs=[y_spec, st_spec],
        out_shape=[jax.ShapeDtypeStruct((T, H * RET_DV), BF16), jax.ShapeDtypeStruct((H, NS, RET_DK, RET_DV), F32)],
        scratch_shapes=[pltpu.VMEM((RET_DK, RET_DV), F32)],
        compiler_params=_cparams("parallel", "arbitrary"),
    )(proj, proj, proj, proj, cos, sin, mask, xi, zeta, gn_w)


def _ret_bwd(proj, gn_w, consts, states, dy, *, C, name):
    T = proj.shape[0]
    H = gn_w.shape[1] // RET_DV
    NS = T // C
    mask, xi, zeta, cos, sin = consts
    q_spec, k_spec, v_spec, g_spec, cs_spec, m_spec, vec_spec, gn_spec, st_spec = _ret_specs(C, H, True, NS)
    dy_spec = pl.BlockSpec((C, RET_DV), lambda h, i: (NS - 1 - i, h))
    dqk_spec = pl.BlockSpec((C, RET_DK), lambda h, i: (NS - 1 - i, h))
    scale = RET_DK ** -0.5

    def body(q_ref, k_ref, v_ref, g_ref, cos_ref, sin_ref, m_ref, xi_ref, ze_ref, gn_ref, st_ref, dy_ref,
             dq_ref, dk_ref, dv_ref, dg_ref, dgn_ref, dS):
        @pl.when(pl.program_id(1) == 0)
        def _():
            dS[...] = jnp.zeros_like(dS)
            dgn_ref[...] = jnp.zeros_like(dgn_ref)

        s_in = st_ref[...]
        v = v_ref[...]
        cos, sin, mask, xi_v, ze = cos_ref[...], sin_ref[...], m_ref[...], xi_ref[...], ze_ref[...]
        qr, kr, a, on, rstd = _ret_fwd_vals(q_ref[...], k_ref[...], v, cos, sin, mask, xi_v, s_in)
        g = g_ref[...]
        sg = _sigmoid(g)
        silu = g * sg
        gnw = gn_ref[...]
        dy = dy_ref[...].astype(F32)
        dg_ref[...] = (dy * on * gnw * (sg * (1.0 + g * (1.0 - sg)))).astype(BF16)
        t = dy * silu
        dgn_ref[...] += _colsum(t * on)
        don = t * gnw
        do = rstd * (don - jnp.mean(don, axis=1, keepdims=True) - on * jnp.mean(don * on, axis=1, keepdims=True))
        dox = do * xi_v
        ds_out = dS[...]
        da = _dot(do, v, NT) * mask
        kz = kr * ze
        dv_ref[...] = (_dot(a, do, TN) + _dot(kz, ds_out)).astype(BF16)
        dqr = _dot(da, kr) + _dot(dox, s_in, NT)
        dkr = _dot(da, qr, TN) + _dot(v, ds_out, NT) * ze
        dS[...] = ds_out * xi_v[C - 1:C, :] + _dot(qr, dox, TN)
        dq_ref[...] = _unrot(dqr, cos, sin).astype(BF16)
        dk_ref[...] = _unrot(dkr * scale, cos, sin).astype(BF16)

    return pl.pallas_call(
        body, name=name, grid=(H, NS),
        in_specs=[q_spec, k_spec, v_spec, g_spec, cs_spec, cs_spec, m_spec, vec_spec, vec_spec, gn_spec, st_spec, dy_spec],
        out_specs=[dqk_spec, dqk_spec, dy_spec, dy_spec, gn_spec],
        out_shape=[jax.ShapeDtypeStruct((T, H * RET_DK), BF16), jax.ShapeDtypeStruct((T, H * RET_DK), BF16),
                   jax.ShapeDtypeStruct((T, H * RET_DV), BF16), jax.ShapeDtypeStruct((T, H * RET_DV), BF16),
                   jax.ShapeDtypeStruct((1, H * RET_DV), F32)],
        scratch_shapes=[pltpu.VMEM((RET_DK, RET_DV), F32)],
        compiler_params=_cparams("parallel", "arbitrary"),
    )(proj, proj, proj, proj, cos, sin, mask, xi, zeta, gn_w, states, dy)


def _shift_down(x, prev8, k):
    if k == 0:
        return x
    y = pltpu.roll(x, k, 0)
    row = lax.broadcasted_iota(jnp.int32, prev8.shape, 0)
    top = jnp.where(row < k, pltpu.roll(prev8, k, 0), y[:8])
    return jnp.concatenate([top, y[8:]], axis=0)


def _shift_up(x, next8, k):
    if k == 0:
        return x
    n = x.shape[0]
    y = pltpu.roll(x, n - k, 0)
    row = lax.broadcasted_iota(jnp.int32, next8.shape, 0)
    bot = jnp.where(row >= 8 - k, pltpu.roll(next8, 8 - k, 0), y[n - 8:])
    return jnp.concatenate([y[:n - 8], bot], axis=0)


def _conv_silu(raw, halo, w, b):
    cv = b
    for tap in range(SSD_CONV_W):
        cv = cv + _shift_down(raw, halo, SSD_CONV_W - 1 - tap) * w[tap:tap + 1, :]
    sg = _sigmoid(cv)
    return cv * sg, cv, sg


def _conv_silu_bwd(d_post, cv, sg, raw, halo, w, carry8):
    dcv = d_post * (sg * (1.0 + cv * (1.0 - sg)))
    d_raw = jnp.zeros_like(raw)
    dws = []
    for tap in range(SSD_CONV_W):
        k = SSD_CONV_W - 1 - tap
        d_raw = d_raw + _shift_up(dcv, carry8, k) * w[tap:tap + 1, :]
        dws.append(_colsum(dcv * _shift_down(raw, halo, k)))
    return d_raw, jnp.concatenate(dws, axis=0), _colsum(dcv), dcv[:8]


def _softplus(x):
    return jnp.maximum(x, 0.0) + jnp.log1p(jnp.exp(-jnp.abs(x)))


def _ssd_common(C, R, dt, dtT, bias, biasT, alog, alogT, E):
    p = dt + bias
    dtv = _softplus(p)
    a = -jnp.exp(alog)
    da = dtv * a
    daT = _softplus(dtT + biasT) * (-jnp.exp(alogT))
    row = lax.broadcasted_iota(jnp.int32, (C, C), 0)
    col = lax.broadcasted_iota(jnp.int32, (C, C), 1)
    tril = row >= col
    trilf = jnp.where(tril, 1.0, 0.0).astype(F32)
    triuf = jnp.where(col >= row, 1.0, 0.0).astype(F32)
    acum = _dot_sel(trilf, da, split="b")
    acumT = _dot_sel(daT, trilf, NT, split="a")
    al = acum[C - 1:C, :]
    ea = jnp.exp(acum)
    dte = jnp.exp(al - acum)
    eal = jnp.exp(al)
    return dict(p=p, dtv=dtv, a=a, da=da, tril=tril, triuf=triuf, acum=acum, acumT=acumT, al=al, ea=ea, dte=dte, eal=eal,
                dtv_e=_dot_sel(dtv, E, split="a", terms=2), ea_e=_dot_sel(ea, E, split="a", terms=2),
                dte_e=_dot_sel(dte, E, split="a", terms=2), eal_e=_dot_sel(eal, E, split="a"))


def _head_decay(q, r, C, R):
    seg = jnp.broadcast_to(q["acum"][:, r:r + 1], (C, C)) - q["acumT"][r:r + 1, :]
    return jnp.exp(jnp.where(q["tril"], seg, -1e30))


def _ssd_group_specs(C, R, NS, rev):
    RP = R * SSD_P
    G = SSD_G
    nz = 1
    hb = C // 8

    def ci(i):
        return NS - 1 - i if rev else i

    def halo_row(i):
        return jnp.maximum(ci(i) * hb - 1, 0)

    off_b = G * RP // SSD_N
    z_spec = pl.BlockSpec((C, RP), lambda g, i: (ci(i), g))
    x_spec = pl.BlockSpec((C, RP), lambda g, i: (ci(i), G + g))
    b_spec = pl.BlockSpec((C, SSD_N), lambda g, i: (ci(i), 2 * off_b + g))
    c_spec = pl.BlockSpec((C, SSD_N), lambda g, i: (ci(i), 2 * off_b + G + g))
    xh_spec = pl.BlockSpec((8, RP), lambda g, i: (halo_row(i), G + g))
    bh_spec = pl.BlockSpec((8, SSD_N), lambda g, i: (halo_row(i), 2 * off_b + g))
    ch_spec = pl.BlockSpec((8, SSD_N), lambda g, i: (halo_row(i), 2 * off_b + G + g))
    dt_spec = pl.BlockSpec((None, C, R), lambda g, i: (g, ci(i), 0))
    dtT_spec = pl.BlockSpec((None, R, C), lambda g, i: (g, 0, ci(i)))
    pr_spec = pl.BlockSpec((None, 1, R), lambda g, i: (g, 0, 0))
    prT_spec = pl.BlockSpec((None, R, 1), lambda g, i: (g, 0, 0))
    cwx_spec = pl.BlockSpec((SSD_CONV_W, RP), lambda g, i: (0, g))
    cwn_spec = pl.BlockSpec((SSD_CONV_W, SSD_N), lambda g, i: (0, g))
    cbx_spec = pl.BlockSpec((1, RP), lambda g, i: (0, g))
    cbn_spec = pl.BlockSpec((1, SSD_N), lambda g, i: (0, g))
    e_spec = pl.BlockSpec((R, RP), lambda g, i: (0, 0))
    st_spec = pl.BlockSpec((None, None, SSD_N, RP), lambda g, i: (g, ci(i), 0, 0))
    return dict(z=z_spec, x=x_spec, b=b_spec, c=c_spec, xh=xh_spec, bh=bh_spec, ch=ch_spec, dt=dt_spec, dtT=dtT_spec,
                pr=pr_spec, prT=prT_spec, cwx=cwx_spec, cwn=cwn_spec, cbx=cbx_spec, cbn=cbn_spec, e=e_spec, st=st_spec)


def _ssd_forward_vals(C, R, refs, first, s_in):
    E = refs["E"]
    halo_on = jnp.where(first, 0.0, 1.0)
    xh, bh, ch = refs["xh"] * halo_on, refs["bh"] * halo_on, refs["ch"] * halo_on
    xs, cvx, sgx = _conv_silu(refs["x"], xh, refs["cwx"], refs["cbx"])
    bm, cvb, sgb = _conv_silu(refs["b"], bh, refs["cwb"], refs["cbb"])
    cm, cvc, sgc = _conv_silu(refs["c"], ch, refs["cwc"], refs["cbc"])
    q = _ssd_common(C, R, refs["dt"], refs["dtT"], refs["bias"], refs["biasT"], refs["alog"], refs["alogT"], E)
    xdt = xs * q["dtv_e"]
    cb = _dot(cm, bm, NT)
    yoff_raw = _dot(cm, s_in)
    ydiag = jnp.zeros_like(xs)
    for r in range(R):
        w_r = cb * _head_decay(q, r, C, R)
        ydiag = ydiag + _dot(w_r, xdt * E[r:r + 1, :])
    d_e = _dot_sel(refs["dskip"], E, split="a")
    y = ydiag + yoff_raw * q["ea_e"] + d_e * xs
    xd = xdt * q["dte_e"]
    s_out = s_in * q["eal_e"] + _dot(bm, xd, TN)
    z = refs["z"]
    sgz = _sigmoid(z)
    yz = y * (z * sgz)
    rn = lax.rsqrt(jnp.mean(yz * yz, axis=1, keepdims=True) + RMS_EPS)
    return dict(q=q, xh=xh, bh=bh, ch=ch, xs=xs, cvx=cvx, sgx=sgx, bm=bm, cvb=cvb, sgb=sgb, cm=cm, cvc=cvc, sgc=sgc,
                xdt=xdt, cb=cb, yoff_raw=yoff_raw, d_e=d_e, y=y, xd=xd, s_out=s_out, z=z, sgz=sgz, yz=yz, rn=rn)


_SSD_IN_NAMES = ("z", "x", "b", "c", "xh", "bh", "ch", "dt", "dtT", "bias", "biasT", "alog", "alogT", "dskip",
                 "cwx", "cwb", "cwc", "cbx", "cbb", "cbc", "nw", "E")


def _ssd_inputs(pm, dt_g, dtT_g, prm, sp):
    bias, biasT, alog, alogT, dskip, cwx, cwb, cwc, cbx, cbb, cbc, nw, E = prm
    args = [pm, pm, pm, pm, pm, pm, pm, dt_g, dtT_g, bias, biasT, alog, alogT, dskip, cwx, cwb, cwc, cbx, cbb, cbc, nw, E]
    specs = [sp["z"], sp["x"], sp["b"], sp["c"], sp["xh"], sp["bh"], sp["ch"], sp["dt"], sp["dtT"], sp["pr"], sp["prT"],
             sp["pr"], sp["prT"], sp["pr"], sp["cwx"], sp["cwn"], sp["cwn"], sp["cbx"], sp["cbn"], sp["cbn"], sp["cbx"], sp["e"]]
    return args, specs


def _ssd_fwd(pm, dt_g, dtT_g, prm, *, C, R, name):
    T = pm.shape[0]
    NS = T // C
    RP = R * SSD_P
    G = SSD_G
    sp = _ssd_group_specs(C, R, NS, False)
    args, specs = _ssd_inputs(pm, dt_g, dtT_g, prm, sp)
    nin = len(args)

    def body(*refs):
        ins = {n: r[...] for n, r in zip(_SSD_IN_NAMES, refs[:nin])}
        y_ref, st_ref, S = refs[nin:]
        first = pl.program_id(1) == 0

        @pl.when(first)
        def _():
            S[...] = jnp.zeros_like(S)

        s_in = S[...]
        st_ref[...] = s_in
        f = _ssd_forward_vals(C, R, ins, first, s_in)
        y_ref[...] = (f["yz"] * f["rn"] * ins["nw"]).astype(BF16)
        S[...] = f["s_out"]

    return pl.pallas_call(
        body, name=name, grid=(G, NS), in_specs=specs,
        out_specs=[pl.BlockSpec((C, RP), lambda g, i: (i, g)), sp["st"]],
        out_shape=[jax.ShapeDtypeStruct((T, G * RP), BF16), jax.ShapeDtypeStruct((G, NS, SSD_N, RP), F32)],
        scratch_shapes=[pltpu.VMEM((SSD_N, RP), F32)],
        compiler_params=_cparams("parallel", "arbitrary"),
    )(*args)


def _ssd_bwd(pm, dt_g, dtT_g, prm, states, dout, *, C, R, name):
    T = pm.shape[0]
    NS = T // C
    RP = R * SSD_P
    G = SSD_G
    sp = _ssd_group_specs(C, R, NS, True)
    args, specs = _ssd_inputs(pm, dt_g, dtT_g, prm, sp)
    nin = len(args)
    rows_spec = pl.BlockSpec((C, RP), lambda g, i: (NS - 1 - i, g))
    rown_spec = pl.BlockSpec((C, SSD_N), lambda g, i: (NS - 1 - i, g))
    args = args + [states, dout]
    specs = specs + [sp["st"], rows_spec]

    def body(*refs):
        ins = {n: r[...] for n, r in zip(_SSD_IN_NAMES, refs[:nin])}
        st_ref, dout_ref = refs[nin], refs[nin + 1]
        (dz_ref, dx_ref, db_ref, dc_ref, ddt_ref, dbias_ref, dalog_ref, dd_ref, dcwx_ref, dcwb_ref, dcwc_ref,
         dcbx_ref, dcbb_ref, dcbc_ref, dnw_ref) = refs[nin + 2:nin + 17]
        dS, cx8, cb8, cc8 = refs[nin + 17:]
        acc_refs = (dbias_ref, dalog_ref, dd_ref, dcwx_ref, dcwb_ref, dcwc_ref, dcbx_ref, dcbb_ref, dcbc_ref, dnw_ref)
        step = pl.program_id(1)

        @pl.when(step == 0)
        def _():
            for r_ in acc_refs + (dS, cx8, cb8, cc8):
                r_[...] = jnp.zeros_like(r_)

        first = step == NS - 1
        E = ins["E"]
        s_in = st_ref[...]
        f = _ssd_forward_vals(C, R, ins, first, s_in)
        q = f["q"]
        xs, bm, cm, xdt, cb, y, z, sgz, yz, rn = (f[n] for n in ("xs", "bm", "cm", "xdt", "cb", "y", "z", "sgz", "yz", "rn"))
        nw = ins["nw"]
        dout = dout_ref[...].astype(F32)
        yh = yz * rn
        dnw_ref[...] += _colsum(dout * yh)
        g1 = dout * nw
        dyz = rn * (g1 - yh * jnp.mean(g1 * yh, axis=1, keepdims=True))
        dz_ref[...] = (dyz * y * (sgz * (1.0 + z * (1.0 - sgz)))).astype(BF16)
        dy = dyz * (z * sgz)
        dd_ref[...] += _dot_sel(_colsum(dy * xs), E, NT, split="a")
        dxs = dy * f["d_e"]
        dyo = dy * q["ea_e"]
        dcm = _dot(dyo, s_in, NT)
        ds_acc = _dot(cm, dyo, TN)
        dacum = _dot_sel(dy * f["yoff_raw"], E, NT, split="a", terms=2) * q["ea"]
        dacumT = jnp.zeros((R, C), F32)
        dxdt = jnp.zeros_like(xs)
        dcb = jnp.zeros((C, C), F32)
        rowR = lax.broadcasted_iota(jnp.int32, (1, R), 1)
        rowRT = lax.broadcasted_iota(jnp.int32, (R, 1), 0)
        for r in range(R):
            mk = E[r:r + 1, :]
            lr = _head_decay(q, r, C, R)
            w_r = cb * lr
            dw = _dot(dy * mk, xdt, NT)
            dxdt = dxdt + _dot(w_r, dy, TN) * mk
            dcb = dcb + dw * lr
            dseg = dw * w_r
            dacum = dacum + jnp.sum(dseg, axis=1, keepdims=True) * jnp.where(rowR == r, 1.0, 0.0)
            dacumT = dacumT - _colsum(dseg) * jnp.where(rowRT == r, 1.0, 0.0)
        dsn = dS[...]
        ds_acc = ds_acc + dsn * q["eal_e"]
        d_eal = _dot_sel(_colsum(dsn * s_in), E, NT, split="a")
        dbm = _dot(f["xd"], dsn, NT)
        dxd = _dot(bm, dsn)
        dxdt = dxdt + dxd * q["dte_e"]
        d_dte = _dot_sel(dxd * xdt, E, NT, split="a", terms=2) * q["dte"]
        d_al = _colsum(d_dte) + d_eal * q["eal"]
        dacum = dacum - d_dte
        rowC = lax.broadcasted_iota(jnp.int32, (C, 1), 0)
        dacum = dacum + jnp.where(rowC == C - 1, 1.0, 0.0) * d_al
        dS[...] = ds_acc
        dcm = dcm + _dot(dcb, bm)
        dbm = dbm + _dot(dcb, cm, TN)
        eye = jnp.where(lax.broadcasted_iota(jnp.int32, (C, C), 0) == lax.broadcasted_iota(jnp.int32, (C, C), 1), 1.0, 0.0)
        dacum = dacum + _dot_sel(eye, dacumT, NT, split="b")
        dda = _dot_sel(q["triuf"], dacum, split="b")
        ddtv = dda * q["a"] + _dot_sel(dxdt * xs, E, NT, split="a", terms=2)
        dalog_ref[...] += _colsum(dda * q["dtv"]) * q["a"]
        dxs = dxs + dxdt * q["dtv_e"]
        dp = ddtv * _sigmoid(q["p"])
        ddt_ref[...] = dp
        dbias_ref[...] += _colsum(dp)
        d_raw, d_w, d_b, c8 = _conv_silu_bwd(dxs, f["cvx"], f["sgx"], ins["x"], f["xh"], ins["cwx"], cx8[...])
        dx_ref[...] = d_raw.astype(BF16)
        dcwx_ref[...] += d_w
        dcbx_ref[...] += d_b
        cx8[...] = c8
        d_raw, d_w, d_b, c8 = _conv_silu_bwd(dbm, f["cvb"], f["sgb"], ins["b"], f["bh"], ins["cwb"], cb8[...])
        db_ref[...] = d_raw.astype(BF16)
        dcwb_ref[...] += d_w
        dcbb_ref[...] += d_b
        cb8[...] = c8
        d_raw, d_w, d_b, c8 = _conv_silu_bwd(dcm, f["cvc"], f["sgc"], ins["c"], f["ch"], ins["cwc"], cc8[...])
        dc_ref[...] = d_raw.astype(BF16)
        dcwc_ref[...] += d_w
        dcbc_ref[...] += d_b
        cc8[...] = c8

    out_specs = [rows_spec, rows_spec, rown_spec, rown_spec,
                 pl.BlockSpec((None, C, R), lambda g, i: (g, NS - 1 - i, 0)),
                 sp["pr"], sp["pr"], sp["pr"], sp["cwx"], sp["cwn"], sp["cwn"], sp["cbx"], sp["cbn"], sp["cbn"], sp["cbx"]]
    out_shape = [jax.ShapeDtypeStruct((T, G * RP), BF16), jax.ShapeDtypeStruct((T, G * RP), BF16),
                 jax.ShapeDtypeStruct((T, G * SSD_N), BF16), jax.ShapeDtypeStruct((T, G * SSD_N), BF16),
                 jax.ShapeDtypeStruct((G, T, R), F32),
                 jax.ShapeDtypeStruct((G, 1, R), F32), jax.ShapeDtypeStruct((G, 1, R), F32), jax.ShapeDtypeStruct((G, 1, R), F32),
                 jax.ShapeDtypeStruct((SSD_CONV_W, G * RP), F32), jax.ShapeDtypeStruct((SSD_CONV_W, G * SSD_N), F32),
                 jax.ShapeDtypeStruct((SSD_CONV_W, G * SSD_N), F32),
                 jax.ShapeDtypeStruct((1, G * RP), F32), jax.ShapeDtypeStruct((1, G * SSD_N), F32),
                 jax.ShapeDtypeStruct((1, G * SSD_N), F32), jax.ShapeDtypeStruct((1, G * RP), F32)]
    return pl.pallas_call(
        body, name=name, grid=(G, NS), in_specs=specs, out_specs=out_specs, out_shape=out_shape,
        scratch_shapes=[pltpu.VMEM((SSD_N, RP), F32), pltpu.VMEM((8, RP), F32), pltpu.VMEM((8, SSD_N), F32),
                        pltpu.VMEM((8, SSD_N), F32)],
        compiler_params=_cparams("parallel", "arbitrary"),
    )(*args)


_ANY = pl.BlockSpec(memory_space=pl.ANY)


def _chip_peer(k):
    x, y, c = lax.axis_index("x"), lax.axis_index("y"), lax.axis_index("c")
    return (x ^ (k >> 1), y ^ (k & 1), c)


def _my_chip():
    return 2 * lax.axis_index("x") + lax.axis_index("y")


def _all_gather_chips(shards, *, name):
    n = len(shards)

    def body(*refs):
        ins, outs = refs[:n], refs[n:2 * n]
        send, recv, loc = refs[2 * n:]
        s = _my_chip()
        copies = []
        for a in range(n):
            cp = pltpu.make_async_copy(ins[a], outs[a].at[s], loc.at[a])
            cp.start()
            copies.append(cp)
        remote = []
        for a in range(n):
            for k in (1, 2, 3):
                cp = pltpu.make_async_remote_copy(
                    src_ref=ins[a], dst_ref=outs[a].at[s], send_sem=send.at[3 * a + k - 1], recv_sem=recv.at[3 * a + k - 1],
                    device_id=_chip_peer(k), device_id_type=MESH_ID)
                cp.start()
                remote.append((a, k, cp))
        for a, k, cp in remote:
            cp.wait_send()
            pltpu.make_async_remote_copy(
                src_ref=ins[a], dst_ref=outs[a].at[s ^ k], send_sem=send.at[3 * a + k - 1], recv_sem=recv.at[3 * a + k - 1],
                device_id=_chip_peer(k), device_id_type=MESH_ID).wait_recv()
        for cp in copies:
            cp.wait()

    return pl.pallas_call(
        body, name=name, in_specs=[_ANY] * n, out_specs=[_ANY] * n,
        out_shape=[jax.ShapeDtypeStruct((4,) + a.shape, a.dtype) for a in shards],
        scratch_shapes=[pltpu.SemaphoreType.DMA((3 * n,)), pltpu.SemaphoreType.DMA((3 * n,)), pltpu.SemaphoreType.DMA((n,))],
        compiler_params=pltpu.CompilerParams(has_side_effects=True),
    )(*shards)


_HBM = pl.BlockSpec(memory_space=pltpu.HBM)
_SEM = pl.BlockSpec(memory_space=pltpu.SEMAPHORE)
_EFFECT = pltpu.SideEffectType.DATAFLOW_SIDE_EFFECTING


def _split_copies(src, land, send, recv, loc, a, scatter):
    s = _my_chip()
    mine = pltpu.make_async_copy(src.at[s] if scatter else src, land.at[s], loc.at[a])
    pairs = []
    for k in (1, 2, 3):
        sems = dict(send_sem=send.at[3 * a + k - 1], recv_sem=recv.at[3 * a + k - 1],
                    device_id=_chip_peer(k), device_id_type=MESH_ID)
        out = pltpu.make_async_remote_copy(src_ref=src.at[s ^ k] if scatter else src, dst_ref=land.at[s], **sems)
        arriving = pltpu.make_async_remote_copy(src_ref=src.at[s ^ k] if scatter else src, dst_ref=land.at[s ^ k], **sems)
        pairs.append((out, arriving))
    return mine, pairs


def _split_start(arrs, *, scatter, after, name):
    n = len(arrs)
    zones = [lax.empty(a.shape if scatter else (4,) + a.shape, a.dtype) for a in arrs]

    def body(*refs):
        srcs, lands = refs[:n], refs[n:2 * n]
        send, recv, loc = refs[2 * n + 1:2 * n + 4]
        token = refs[-1]
        for a in range(n):
            mine, pairs = _split_copies(srcs[a], lands[a], send, recv, loc, a, scatter)
            mine.start()
            for out, _ in pairs:
                out.start()
        token[...] = jnp.zeros_like(token)

    res = pl.pallas_call(
        body, name=name,
        out_shape=(pltpu.SemaphoreType.DMA((3 * n,)), pltpu.SemaphoreType.DMA((3 * n,)), pltpu.SemaphoreType.DMA((n,)),
                   *[pltpu.HBM(a.shape, a.dtype) for a in arrs], *[pltpu.HBM(z.shape, z.dtype) for z in zones],
                   jax.ShapeDtypeStruct((8, 128), F32)),
        in_specs=[_HBM] * (2 * n) + [_ANY],
        out_specs=(_SEM, _SEM, _SEM, *([_HBM] * (2 * n)), pl.BlockSpec(memory_space=pltpu.VMEM)),
        input_output_aliases={i: 3 + i for i in range(2 * n)},
        compiler_params=pltpu.CompilerParams(has_side_effects=_EFFECT),
    )(*[pltpu.with_memory_space_constraint(a, pltpu.HBM) for a in arrs],
      *[pltpu.with_memory_space_constraint(z, pltpu.HBM) for z in zones], after)
    return res[:3], list(res[3:3 + n]), list(res[3 + n:3 + 2 * n]), res[-1]


def _split_wait(sems, src, land, a, *, scatter, after, name):
    def body(src_ref, land_ref, send, recv, loc, after_ref, src_out, land_out):
        mine, pairs = _split_copies(src_ref, land_ref, send, recv, loc, a, scatter)
        mine.wait()
        for out, arriving in pairs:
            out.wait_send()
            arriving.wait_recv()

    return pl.pallas_call(
        body, name=name, out_shape=(pltpu.HBM(src.shape, src.dtype), pltpu.HBM(land.shape, land.dtype)),
        in_specs=[_HBM, _HBM, _SEM, _SEM, _SEM, _ANY], out_specs=(_HBM, _HBM), input_output_aliases={0: 0, 1: 1},
        compiler_params=pltpu.CompilerParams(has_side_effects=_EFFECT),
    )(src, land, *sems, after)[1]


def _swap_with_sibling(arrs, *, name):
    n = len(arrs)

    def body(*refs):
        ins, outs = refs[:n], refs[n:2 * n]
        send, recv = refs[2 * n:]
        sib = (lax.axis_index("x"), lax.axis_index("y"), 1 - lax.axis_index("c"))
        cps = []
        for a in range(n):
            cp = pltpu.make_async_remote_copy(src_ref=ins[a], dst_ref=outs[a], send_sem=send.at[a], recv_sem=recv.at[a],
                                              device_id=sib, device_id_type=MESH_ID)
            cp.start()
            cps.append(cp)
        for cp in cps:
            cp.wait()

    return pl.pallas_call(
        body, name=name, in_specs=[_ANY] * n, out_specs=[_ANY] * n,
        out_shape=[jax.ShapeDtypeStruct(a.shape, a.dtype) for a in arrs],
        scratch_shapes=[pltpu.SemaphoreType.DMA((n,)), pltpu.SemaphoreType.DMA((n,))],
        compiler_params=pltpu.CompilerParams(has_side_effects=True),
    )(*arrs)


def _all_gather_devices(v, *, name):
    r = v.shape[0]

    def body(v_ref, out_ref, send, recv):
        x, y, c = lax.axis_index("x"), lax.axis_index("y"), lax.axis_index("c")
        me = 4 * x + 2 * y + c
        out_ref[me] = v_ref[...]
        cps = []
        for k in range(1, 8):
            peer = (x ^ (k >> 2), y ^ ((k >> 1) & 1), c ^ (k & 1))
            cp = pltpu.make_async_remote_copy(src_ref=v_ref, dst_ref=out_ref.at[me], send_sem=send.at[k - 1],
                                              recv_sem=recv.at[k - 1], device_id=peer, device_id_type=MESH_ID)
            cp.start()
            cps.append(cp)
        for k, cp in enumerate(cps, start=1):
            cp.wait_send()
            pltpu.make_async_remote_copy(src_ref=v_ref, dst_ref=out_ref.at[me ^ k], send_sem=send.at[k - 1],
                                         recv_sem=recv.at[k - 1], device_id=(x, y, c), device_id_type=MESH_ID).wait_recv()

    vm = pl.BlockSpec(memory_space=pltpu.VMEM)
    return pl.pallas_call(
        body, name=name, in_specs=[vm], out_specs=vm, out_shape=jax.ShapeDtypeStruct((8, r, 128), F32),
        scratch_shapes=[pltpu.SemaphoreType.DMA((7,)), pltpu.SemaphoreType.DMA((7,))],
        compiler_params=pltpu.CompilerParams(has_side_effects=True),
    )(v)


def _row_tile(r, target):
    best = None
    for t in range(16, min(target, r) + 1, 16):
        if r % t == 0:
            best = t
    return best or r


def _sum_slots(buf, *, name, tr=384):
    S, r, c = buf.shape
    tr = _row_tile(r, tr)

    def body(b_ref, o_ref):
        acc = b_ref[0].astype(F32)
        for j in range(1, S):
            acc = acc + b_ref[j].astype(F32)
        o_ref[...] = acc

    return pl.pallas_call(
        body, name=name, grid=(r // tr,), in_specs=[pl.BlockSpec((S, tr, c), lambda i: (0, i, 0))],
        out_specs=pl.BlockSpec((tr, c), lambda i: (i, 0)), out_shape=jax.ShapeDtypeStruct((r, c), F32),
        compiler_params=_cparams("parallel"),
    )(buf)


def _add2(a, b, *, name, tr=384):
    r, c = a.shape
    tr = _row_tile(r, tr)

    def body(a_ref, b_ref, o_ref):
        o_ref[...] = a_ref[...] + b_ref[...]

    spec = pl.BlockSpec((tr, c), lambda i: (i, 0))
    return pl.pallas_call(body, name=name, grid=(r // tr,), in_specs=[spec, spec], out_specs=spec,
                          out_shape=jax.ShapeDtypeStruct((r, c), F32), compiler_params=_cparams("parallel"))(a, b)


def _adamw(w, g, m, v, *, name, tr=256):
    r, c = w.shape
    tr = _row_tile(r, tr)
    bc1 = 1.0 - ADAM_B1 ** ADAM_STEP
    bc2 = 1.0 - ADAM_B2 ** ADAM_STEP

    def body(w_ref, g_ref, m_ref, v_ref, d_ref, mo_ref, vo_ref):
        gg = g_ref[...]
        mn = ADAM_B1 * m_ref[...] + (1.0 - ADAM_B1) * gg
        vn = ADAM_B2 * v_ref[...] + (1.0 - ADAM_B2) * (gg * gg)
        mo_ref[...] = mn
        vo_ref[...] = vn
        d_ref[...] = -ADAM_LR * ((mn / bc1) / (jnp.sqrt(vn / bc2) + ADAM_EPS) + ADAM_WD * w_ref[...])

    spec = pl.BlockSpec((tr, c), lambda i: (i, 0))
    return pl.pallas_call(body, name=name, grid=(r // tr,), in_specs=[spec] * 4, out_specs=[spec] * 3,
                          out_shape=[jax.ShapeDtypeStruct((r, c), F32)] * 3, compiler_params=_cparams("parallel"))(w, g, m, v)


def _pack(vecs, rows):
    flat = jnp.concatenate([v.reshape(-1).astype(F32) for v in vecs])
    return jnp.pad(flat, (0, rows * 128 - flat.shape[0])).reshape(rows, 128)


def _unpack(packed, shapes):
    flat = packed.reshape(-1)
    out, off = [], 0
    for s in shapes:
        n = math.prod(s)
        out.append(flat[off:off + n].reshape(s))
        off += n
    return out


def _pack_rows(shapes):
    n = sum(math.prod(s) for s in shapes)
    return -(-n // 1024) * 8


def kernel(x, norm_mix_pre, norm_mix_post, norm_ffn_pre, norm_ffn_post, ret_w_in, ret_gn_w, ret_w_out, ssd_w_in, ssd_conv_w, ssd_conv_b, ssd_dt_bias, ssd_a_log, ssd_d, ssd_norm_w, ssd_w_out, mlp_w_up, mlp_w_down, loss_target, m_norm_mix_pre, m_norm_mix_post, m_norm_ffn_pre, m_norm_ffn_post, m_ret_w_in, m_ret_gn_w, m_ret_w_out, m_ssd_w_in, m_ssd_conv_w, m_ssd_conv_b, m_ssd_dt_bias, m_ssd_a_log, m_ssd_d, m_ssd_norm_w, m_ssd_w_out, m_mlp_w_up, m_mlp_w_down, v_norm_mix_pre, v_norm_mix_post, v_norm_ffn_pre, v_norm_ffn_post, v_ret_w_in, v_ret_gn_w, v_ret_w_out, v_ssd_w_in, v_ssd_conv_w, v_ssd_conv_b, v_ssd_dt_bias, v_ssd_a_log, v_ssd_d, v_ssd_norm_w, v_ssd_w_out, v_mlp_w_up, v_mlp_w_down):
    T, D = x.shape[1], x.shape[2]
    H = D // RET_DK
    d_inner = 2 * D
    R = d_inner // SSD_P // SSD_G
    RP = R * SSD_P
    n_heads = SSD_G * R
    conv_dim = d_inner + 2 * SSD_G * SSD_N
    n_main = d_inner + conv_dim
    C = min(256, T)
    chip = _my_chip()
    xs, tgt = x[0], loss_target[0]

    conv_sh = ssd_conv_w.shape[2]
    small_shapes = [(SSD_CONV_W, conv_sh), (conv_sh,), (ssd_norm_w.shape[1],)]
    small_rows = _pack_rows(small_shapes)
    shards = [ret_w_in[0].T.astype(BF16), ret_w_out[0].astype(BF16), ssd_w_in[0].T.astype(BF16), ssd_w_out[0].astype(BF16),
              mlp_w_up[0].T.astype(BF16), mlp_w_up[1].T.astype(BF16), mlp_w_down[0].astype(BF16), mlp_w_down[1].astype(BF16)]
    (ret_in_g, small_g) = _all_gather_chips([shards[0], _pack([ssd_conv_w[0], ssd_conv_b[0], ssd_norm_w[0]], small_rows)],
                                            name="gather_first")

    def full(g):
        return g.reshape(4 * g.shape[1], g.shape[2])

    def start_gather(idx, after, name):
        sems, srcs, zones, tok = _split_start([shards[i] for i in idx], scatter=False, after=after, name=name)
        return {i: (sems, srcs[n], zones[n], n) for n, i in enumerate(idx)}, tok

    def arrived(stage, i, after, name):
        sems, src, zone, n = stage[i]
        return full(_split_wait(sems, src, zone, n, scatter=False, after=after, name=name))

    ret_in_t = full(ret_in_g)
    sm = [_unpack(small_g[j], small_shapes) for j in range(4)]
    conv_w = jnp.concatenate([sm[j][0] for j in range(4)], axis=1)
    conv_b = jnp.concatenate([sm[j][1] for j in range(4)])[None, :]
    norm_w = jnp.concatenate([sm[j][2] for j in range(4)])[None, :]

    gb = SSD_G * SSD_N
    ssd_prm = (ssd_dt_bias.reshape(SSD_G, 1, R), ssd_dt_bias.reshape(SSD_G, R, 1),
               ssd_a_log.reshape(SSD_G, 1, R), ssd_a_log.reshape(SSD_G, R, 1), ssd_d.reshape(SSD_G, 1, R),
               conv_w[:, :d_inner], conv_w[:, d_inner:d_inner + gb], conv_w[:, d_inner + gb:],
               conv_b[:, :d_inner], conv_b[:, d_inner:d_inner + gb], conv_b[:, d_inner + gb:],
               norm_w, jnp.asarray(np.kron(np.eye(R), np.ones((1, SSD_P))), F32))
    ret_consts = _ret_consts(T, C, H)

    u0 = _rms_pre(xs, norm_mix_pre[0:1], name="pre0")
    stage1, tok = start_gather((1, 4), ret_in_g, "gather_start1")
    proj = _matmul(u0, ret_in_t, "nt", out_dtype=F32, name="ret_in", after=tok)
    stage2, tok = start_gather((6, 2), proj, "gather_start2")
    y_ret, st_ret = _ret_fwd(proj, ret_gn_w, ret_consts, C=C, name="ret_fwd")
    ret_out = arrived(stage1, 1, y_ret, "gather_wait_ret_out")
    m0 = _matmul(y_ret, ret_out, "nn", out_dtype=F32, name="ret_out", after=tok)
    h1, u1 = _rms_post_pre(xs, m0, norm_mix_post[0:1], norm_ffn_pre[0:1], name="post_pre1")
    up_t0 = arrived(stage1, 4, u1, "gather_wait_up0")
    a0, hh0 = _matmul(u1, up_t0, "nt", out_dtype=BF16, name="mlp_up0", epi="relu2")
    stage3, tok = start_gather((3, 5, 7), hh0, "gather_start3")
    down0 = arrived(stage2, 6, hh0, "gather_wait_down0")
    f0 = _matmul(hh0, down0, "nn", out_dtype=F32, name="mlp_down0", after=tok)
    h2, u2 = _rms_post_pre(h1, f0, norm_ffn_post[0:1], norm_mix_pre[1:2], name="post_pre2")
    ssd_in_t = arrived(stage2, 2, u2, "gather_wait_ssd_in")
    ssd_main_t, ssd_dt_t = ssd_in_t[:n_main], ssd_in_t[n_main:]
    pm = _matmul(u2, ssd_main_t, "nt", out_dtype=F32, name="ssd_in")
    pdt = _matmul(u2, ssd_dt_t, "nt", out_dtype=F32, name="ssd_in_dt")
    dt_g = pdt.reshape(T, SSD_G, R).transpose(1, 0, 2)
    dtT_g = pdt.reshape(T, SSD_G, R).transpose(1, 2, 0)
    y_ssd, st_ssd = _ssd_fwd(pm, dt_g, dtT_g, ssd_prm, C=C, R=R, name="ssd_fwd")
    ssd_out = arrived(stage3, 3, y_ssd, "gather_wait_ssd_out")
    m1 = _matmul(y_ssd, ssd_out, "nn", out_dtype=F32, name="ssd_out")
    h3, u3 = _rms_post_pre(h2, m1, norm_mix_post[1:2], norm_ffn_pre[1:2], name="post_pre3")
    up_t1 = arrived(stage3, 5, u3, "gather_wait_up1")
    a1, hh1 = _matmul(u3, up_t1, "nt", out_dtype=BF16, name="mlp_up1", epi="relu2")
    down1 = arrived(stage3, 7, hh1, "gather_wait_down1")
    f1 = _matmul(hh1, down1, "nn", out_dtype=F32, name="mlp_down1")
    up_t, down = (up_t0, up_t1), (down0, down1)
    dh4, sq = _rms_post_loss(h3, f1, norm_ffn_post[1:2], tgt, name="post_loss")
    loss = lax.psum(sq[0, 0], MESH_AXES) * (0.5 / D)

    in_flight = []

    def send_grad(g, name):
        part = g.reshape(4, g.shape[0] // 4, g.shape[1])
        sems, srcs, zones, tok = _split_start([part], scatter=True, after=part, name=f"scatter_start_{name}")
        in_flight.append((name, sems, srcs[0], zones[0]))
        return tok

    def mlp_bwd(i, dh_out, h_in, u, a, hh, f):
        df, d_post = _rms_post_bwd(f, norm_ffn_post[i:i + 1], dh_out, name=f"post_bwd_ffn{i}")
        tok = send_grad(_matmul(hh, df, "tn", out_dtype=BF16, name=f"mlp_down_wg{i}"), f"down{i}")
        da = _matmul(df, down[i], "nt", out_dtype=BF16, name=f"mlp_down_dg{i}", epi="drelu2", extra=a, after=tok)
        tok = send_grad(_matmul(da, u, "tn", out_dtype=BF16, name=f"mlp_up_wg{i}"), f"up{i}")
        du = _matmul(da, up_t[i], "nn", out_dtype=F32, name=f"mlp_up_dg{i}", after=tok)
        dh, d_pre = _rms_pre_bwd(h_in, norm_ffn_pre[i:i + 1], du, dh_out, name=f"pre_bwd_ffn{i}")
        return dh, d_pre, d_post

    dh3, d_nfp1, d_nfpost1 = mlp_bwd(1, dh4, h3, u3, a1, hh1, f1)
    dm1, d_nmpost1 = _rms_post_bwd(m1, norm_mix_post[1:2], dh3, name="post_bwd_mix1")
    tok = send_grad(_matmul(y_ssd, dm1, "tn", out_dtype=BF16, name="ssd_out_wg"), "ssd_out")
    dy_ssd = _matmul(dm1, ssd_out, "nt", out_dtype=F32, name="ssd_out_dg", after=tok)
    (dz, dxr, dbr, dcr, ddt_g, d_bias, d_alog, d_dskip, dcwx, dcwb, dcwc, dcbx, dcbb, dcbc, d_normw) = _ssd_bwd(
        pm, dt_g, dtT_g, ssd_prm, st_ssd, dy_ssd, C=C, R=R, name="ssd_bwd")
    dpm = jnp.concatenate([dz, dxr, dbr, dcr], axis=1)
    dpdt = ddt_g.transpose(1, 0, 2).reshape(T, n_heads).astype(BF16)
    tok = send_grad(jnp.concatenate([_matmul(dpm, u2, "tn", out_dtype=BF16, name="ssd_in_wg"),
                                     _matmul(dpdt, u2, "tn", out_dtype=BF16, name="ssd_in_dt_wg")], axis=0), "ssd_in")
    du2 = _matmul(dpm, ssd_main_t, "nn", out_dtype=F32, name="ssd_in_dg", after=tok)
    du2 = _matmul(dpdt, ssd_dt_t, "nn", out_dtype=F32, name="ssd_in_dt_dg", epi="add", extra=du2)
    dh2, d_nmp1 = _rms_pre_bwd(h2, norm_mix_pre[1:2], du2, dh3, name="pre_bwd_mix1")
    dh1, d_nfp0, d_nfpost0 = mlp_bwd(0, dh2, h1, u1, a0, hh0, f0)
    dm0, d_nmpost0 = _rms_post_bwd(m0, norm_mix_post[0:1], dh1, name="post_bwd_mix0")
    tok = send_grad(_matmul(y_ret, dm0, "tn", out_dtype=BF16, name="ret_out_wg"), "ret_out")
    dy_ret = _matmul(dm0, ret_out, "nt", out_dtype=F32, name="ret_out_dg", after=tok)
    dq, dk, dv, dg, d_gn = _ret_bwd(proj, ret_gn_w, ret_consts, st_ret, dy_ret, C=C, name="ret_bwd")
    dproj = jnp.concatenate([dq, dk, dv, dg], axis=1)
    tok = send_grad(_matmul(dproj, u0, "tn", out_dtype=BF16, name="ret_in_wg"), "ret_in")
    du0 = _matmul(dproj, ret_in_t, "nn", out_dtype=F32, name="ret_in_dg", after=tok)
    grad_x, d_nmp0 = _rms_pre_bwd(xs, norm_mix_pre[0:1], du0, dh1, name="pre_bwd_mix0")

    landed = {nm: _split_wait(sems, src, zone, 0, scatter=True, after=grad_x, name=f"scatter_wait_{nm}")
              for nm, sems, src, zone in in_flight}
    sent_order = [nm for nm, _, _, _ in in_flight]
    partial = [_sum_slots(landed[nm], name=f"sum_chips_{nm}") for nm in sent_order]
    other = _swap_with_sibling(partial, name="swap_partials")
    both = {nm: _add2(p, o, name=f"sum_cores_{nm}") for nm, p, o in zip(sent_order, partial, other)}
    summed = [both[nm] for nm in ("ret_in", "ret_out", "ssd_in", "ssd_out", "up0", "up1", "down0", "down1")]
    transposed = (True, False, True, False, True, True, False, False)
    gshard = [s.T if t else s for s, t in zip(summed, transposed)]
    g_ret_in, g_ret_out_s, g_ssd_in, g_ssd_out_s, g_up0, g_up1, g_dn0, g_dn1 = gshard

    def upd(w, g, m, v, name):
        shp = w.shape
        w2, g2, m2, v2 = (t.reshape(-1, shp[-1]) for t in (w, g, m, v))
        d, mn, vn = _adamw(w2, g2, m2, v2, name=name)
        return g.reshape(shp), d.reshape(shp), mn.reshape(shp), vn.reshape(shp)

    res = {}
    res["ret_w_in"] = upd(ret_w_in, g_ret_in[None], m_ret_w_in, v_ret_w_in, "adamw_ret_in")
    res["ret_w_out"] = upd(ret_w_out, g_ret_out_s[None], m_ret_w_out, v_ret_w_out, "adamw_ret_out")
    res["ssd_w_in"] = upd(ssd_w_in, g_ssd_in[None], m_ssd_w_in, v_ssd_w_in, "adamw_ssd_in")
    res["ssd_w_out"] = upd(ssd_w_out, g_ssd_out_s[None], m_ssd_w_out, v_ssd_w_out, "adamw_ssd_out")
    res["mlp_w_up"] = upd(mlp_w_up, jnp.stack([g_up0, g_up1]), m_mlp_w_up, v_mlp_w_up, "adamw_up")
    res["mlp_w_down"] = upd(mlp_w_down, jnp.stack([g_dn0, g_dn1]), m_mlp_w_down, v_mlp_w_down, "adamw_down")

    d_conv_w = jnp.concatenate([dcwx, dcwb, dcwc], axis=1)
    d_conv_b = jnp.concatenate([dcbx, dcbb, dcbc], axis=1)
    small_grads = [jnp.concatenate([d_nmp0, d_nmp1]), jnp.concatenate([d_nmpost0, d_nmpost1]),
                   jnp.concatenate([d_nfp0, d_nfp1]), jnp.concatenate([d_nfpost0, d_nfpost1]),
                   d_gn, d_bias.reshape(1, n_heads), d_alog.reshape(1, n_heads), d_dskip.reshape(1, n_heads),
                   d_conv_w, d_conv_b, d_normw]
    sg_shapes = [g.shape for g in small_grads]
    sg_rows = _pack_rows(sg_shapes)
    everyone = _all_gather_devices(_pack(small_grads, sg_rows), name="gather_small_grads")
    sg = _unpack(_sum_slots(everyone, name="sum_small_grads", tr=sg_rows), sg_shapes)
    (g_nmp, g_nmpost, g_nfp, g_nfpost, g_gn, g_bias, g_alog, g_dskip, g_cw_full, g_cb_full, g_nw_full) = sg
    g_cw = lax.dynamic_slice_in_dim(g_cw_full, chip * conv_sh, conv_sh, axis=1)[None]
    g_cb = lax.dynamic_slice_in_dim(g_cb_full, chip * conv_sh, conv_sh, axis=1)
    nw_sh = ssd_norm_w.shape[1]
    g_nw = lax.dynamic_slice_in_dim(g_nw_full, chip * nw_sh, nw_sh, axis=1)
    small = [("norm_mix_pre", norm_mix_pre, g_nmp, m_norm_mix_pre, v_norm_mix_pre),
             ("norm_mix_post", norm_mix_post, g_nmpost, m_norm_mix_post, v_norm_mix_post),
             ("norm_ffn_pre", norm_ffn_pre, g_nfp, m_norm_ffn_pre, v_norm_ffn_pre),
             ("norm_ffn_post", norm_ffn_post, g_nfpost, m_norm_ffn_post, v_norm_ffn_post),
             ("ret_gn_w", ret_gn_w, g_gn, m_ret_gn_w, v_ret_gn_w),
             ("ssd_conv_w", ssd_conv_w, g_cw, m_ssd_conv_w, v_ssd_conv_w),
             ("ssd_conv_b", ssd_conv_b, g_cb, m_ssd_conv_b, v_ssd_conv_b),
             ("ssd_dt_bias", ssd_dt_bias, g_bias, m_ssd_dt_bias, v_ssd_dt_bias),
             ("ssd_a_log", ssd_a_log, g_alog, m_ssd_a_log, v_ssd_a_log),
             ("ssd_d", ssd_d, g_dskip, m_ssd_d, v_ssd_d),
             ("ssd_norm_w", ssd_norm_w, g_nw, m_ssd_norm_w, v_ssd_norm_w)]
    sw_shapes = [w.shape for _, w, _, _, _ in small]
    sw_rows = _pack_rows(sw_shapes)
    packs = [_pack([t[j] for t in small], sw_rows) for j in (1, 2, 3, 4)]
    d_p, m_p, v_p = _adamw(*packs, name="adamw_small", tr=sw_rows)
    d_s, m_s, v_s = _unpack(d_p, sw_shapes), _unpack(m_p, sw_shapes), _unpack(v_p, sw_shapes)
    for j, (nm, w, g, _, _) in enumerate(small):
        res[nm] = (g.reshape(w.shape), d_s[j], m_s[j], v_s[j])

    order = ["norm_mix_pre", "norm_mix_post", "norm_ffn_pre", "norm_ffn_post", "ret_w_in", "ret_gn_w", "ret_w_out",
             "ssd_w_in", "ssd_conv_w", "ssd_conv_b", "ssd_dt_bias", "ssd_a_log", "ssd_d", "ssd_norm_w", "ssd_w_out",
             "mlp_w_up", "mlp_w_down"]
    return (loss, grad_x[None], *[res[n][0] for n in order], *[res[n][1] for n in order],
            *[res[n][2] for n in order], *[res[n][3] for n in order])
```

```python
import functools
import math

import numpy as np
import jax
import jax.numpy as jnp
from jax import lax
from jax.experimental import pallas as pl
from jax.experimental.pallas import tpu as pltpu

F32 = jnp.float32
BF16 = jnp.bfloat16
HI = lax.Precision.HIGHEST
VMEM_LIMIT_BYTES = 56 * 1024 * 1024
MESH_AXES = ("x", "y", "c")
MESH_ID = pl.DeviceIdType.MESH

RMS_EPS = 1e-6
GN_EPS = 1e-5
RET_DK = 256
RET_DV = 512
ROPE_BASE = 10000.0
REF_CHUNK = 64
SSD_P = 64
SSD_N = 128
SSD_G = 8
SSD_CONV_W = 4
ADAM_LR, ADAM_B1, ADAM_B2, ADAM_EPS, ADAM_WD, ADAM_STEP = 0.001, 0.9, 0.999, 1e-08, 0.01, 10

NN = (((1,), (0,)), ((), ()))
NT = (((1,), (1,)), ((), ()))
TN = (((0,), (0,)), ((), ()))


def _cparams(*sem):
    return pltpu.CompilerParams(dimension_semantics=sem, vmem_limit_bytes=VMEM_LIMIT_BYTES)


def _dot(a, b, dims=NN):
    return lax.dot_general(a.astype(BF16), b.astype(BF16), dims, preferred_element_type=F32)


def _split_bf16(x, terms):
    parts, rest = [], x
    for _ in range(terms):
        p = rest.astype(BF16)
        parts.append(p)
        rest = rest - p.astype(F32)
    return parts


def _dot_sel(a, b, dims=NN, *, split, terms=3):
    if split == "a":
        sel = b.astype(BF16)
        return sum(lax.dot_general(p, sel, dims, preferred_element_type=F32) for p in _split_bf16(a, terms))
    sel = a.astype(BF16)
    return sum(lax.dot_general(sel, p, dims, preferred_element_type=F32) for p in _split_bf16(b, terms))


def _sigmoid(x):
    return 1.0 / (1.0 + jnp.exp(-x))


def _colsum(x):
    return jnp.sum(x, axis=0, keepdims=True)


MM_TILE = 1024
MM_FULL_K = 2048


def _mm_tiles(M, N, K):
    tm = min(M, MM_TILE)
    if K <= MM_FULL_K:
        return tm, min(N, MM_TILE), K
    return tm, min(N, 2 * MM_TILE), MM_TILE


def _matmul(a, b, mode, *, out_dtype, name, epi=None, extra=None, after=None, col_parts=None):
    if mode == "nn":
        (M, K), (K2, N) = a.shape, b.shape
    elif mode == "nt":
        (M, K), (N, K2) = a.shape, b.shape
    else:
        (K, M), (K2, N) = a.shape, b.shape
    assert K == K2, (a.shape, b.shape, mode)
    tm, tn, tk = _mm_tiles(M, N, K)
    if col_parts:
        while (N // col_parts) % tn:
            tn //= 2
    assert M % tm == 0 and N % tn == 0 and K % tk == 0, (M, N, K, tm, tn, tk)
    nk = K // tk
    if mode == "tn":
        a_spec = pl.BlockSpec((tk, tm), lambda i, j, k: (k, i))
    else:
        a_spec = pl.BlockSpec((tm, tk), lambda i, j, k: (i, k))
    if mode == "nt":
        b_spec = pl.BlockSpec((tn, tk), lambda i, j, k: (j, k))
    else:
        b_spec = pl.BlockSpec((tk, tn), lambda i, j, k: (k, j))
    dims = {"nn": NN, "nt": NT, "tn": TN}[mode]
    o_spec = pl.BlockSpec((tm, tn), lambda i, j, k: (i, j))
    out_dims = (M, N)
    if col_parts:
        per = N // col_parts // tn
        o_spec = pl.BlockSpec((None, tm, tn), lambda i, j, k: (j // per, i, j % per))
        out_dims = (col_parts, M, N // col_parts)
    has_extra = epi in ("drelu2", "add")
    n_out = 2 if epi == "relu2" else 1
    n_in = 2 + int(has_extra) + int(after is not None)

    def body(*refs):
        a_ref, b_ref = refs[0], refs[1]
        e_ref = refs[2] if has_extra else None
        outs = refs[n_in:n_in + n_out]

        def finish(r):
            if epi is None:
                outs[0][...] = r.astype(outs[0].dtype)
            elif epi == "relu2":
                outs[0][...] = r.astype(outs[0].dtype)
                h = jnp.maximum(r, 0.0)
                outs[1][...] = (h * h).astype(outs[1].dtype)
            elif epi == "drelu2":
                av = jnp.maximum(e_ref[...].astype(F32), 0.0)
                outs[0][...] = (r * (2.0 * av)).astype(outs[0].dtype)
            else:
                outs[0][...] = (r + e_ref[...].astype(F32)).astype(outs[0].dtype)

        if nk == 1:
            finish(_dot(a_ref[...], b_ref[...], dims))
            return
        acc = refs[-1]
        k = pl.program_id(2)

        @pl.when(k == 0)
        def _():
            acc[...] = jnp.zeros_like(acc)

        acc[...] += _dot(a_ref[...], b_ref[...], dims)

        @pl.when(k == nk - 1)
        def _():
            finish(acc[...])

    in_specs = [a_spec, b_spec] + ([o_spec] if has_extra else [])
    args = [a, b] + ([extra] if has_extra else [])
    if after is not None:
        in_specs.append(pl.BlockSpec(after.shape, lambda i, j, k: (0, 0)))
        args.append(after)
    out_shape = [jax.ShapeDtypeStruct(out_dims, out_dtype)] * n_out
    res = pl.pallas_call(
        body, name=name, grid=(M // tm, N // tn, nk), in_specs=in_specs, out_specs=[o_spec] * n_out,
        out_shape=out_shape, scratch_shapes=[pltpu.VMEM((tm, tn), F32)] if nk > 1 else [],
        compiler_params=_cparams("parallel", "parallel", "arbitrary"),
    )(*args)
    return res if n_out == 2 else res[0]


def _rstd(x):
    return lax.rsqrt(jnp.mean(x * x, axis=-1, keepdims=True) + RMS_EPS)


def _row_call(body, ins, outs_shape, *, name, rows, tr, acc_outs=()):
    tr = min(tr, rows)
    assert rows % tr == 0
    in_specs = []
    for arr, blocked in ins:
        if blocked:
            in_specs.append(pl.BlockSpec((tr, arr.shape[1]), lambda i: (i, 0)))
        else:
            in_specs.append(pl.BlockSpec(arr.shape, lambda i: (0, 0)))
    out_specs = []
    for n, s in enumerate(outs_shape):
        if n in acc_outs:
            out_specs.append(pl.BlockSpec(s.shape, lambda i: (0, 0)))
        else:
            out_specs.append(pl.BlockSpec((tr, s.shape[1]), lambda i: (i, 0)))
    return pl.pallas_call(
        body, name=name, grid=(rows // tr,), in_specs=in_specs, out_specs=out_specs, out_shape=outs_shape,
        compiler_params=_cparams("arbitrary" if acc_outs else "parallel"),
    )(*[a for a, _ in ins])


def _rms_pre(h, w, *, name):
    T, D = h.shape

    def body(h_ref, w_ref, u_ref):
        x = h_ref[...]
        u_ref[...] = (x * _rstd(x) * w_ref[...]).astype(BF16)

    return _row_call(body, [(h, True), (w, False)], [jax.ShapeDtypeStruct((T, D), BF16)], name=name, rows=T, tr=256)[0]


def _rms_post_pre(h, m, w_post, w_pre, *, name):
    T, D = h.shape

    def body(h_ref, m_ref, wp_ref, wn_ref, hn_ref, u_ref):
        mm = m_ref[...]
        hn = h_ref[...] + mm * _rstd(mm) * wp_ref[...]
        hn_ref[...] = hn
        u_ref[...] = (hn * _rstd(hn) * wn_ref[...]).astype(BF16)

    return _row_call(body, [(h, True), (m, True), (w_post, False), (w_pre, False)],
                     [jax.ShapeDtypeStruct((T, D), F32), jax.ShapeDtypeStruct((T, D), BF16)], name=name, rows=T, tr=256)


def _rms_post_loss(h, m, w_post, tgt, *, name):
    T, D = h.shape

    def body(h_ref, m_ref, wp_ref, t_ref, dh_ref, loss_ref):
        @pl.when(pl.program_id(0) == 0)
        def _():
            loss_ref[...] = jnp.zeros_like(loss_ref)

        mm = m_ref[...]
        err = h_ref[...] + mm * _rstd(mm) * wp_ref[...] - t_ref[...]
        dh_ref[...] = err * (1.0 / D)
        loss_ref[...] += _colsum(jnp.sum(err * err, axis=1, keepdims=True))

    return _row_call(body, [(h, True), (m, True), (w_post, False), (tgt, True)],
                     [jax.ShapeDtypeStruct((T, D), F32), jax.ShapeDtypeStruct((1, 1), F32)],
                     name=name, rows=T, tr=256, acc_outs=(1,))


def _rms_bwd_vals(x, w, dy):
    r = _rstd(x)
    xh = x * r
    g = dy * w
    dx = r * (g - xh * jnp.mean(g * xh, axis=-1, keepdims=True))
    return dx, _colsum(dy * xh)


def _rms_post_bwd(m, w_post, dh, *, name):
    T, D = m.shape

    def body(m_ref, w_ref, dh_ref, dm_ref, dw_ref):
        @pl.when(pl.program_id(0) == 0)
        def _():
            dw_ref[...] = jnp.zeros_like(dw_ref)

        dx, dw = _rms_bwd_vals(m_ref[...], w_ref[...], dh_ref[...])
        dm_ref[...] = dx.astype(BF16)
        dw_ref[...] += dw

    return _row_call(body, [(m, True), (w_post, False), (dh, True)],
                     [jax.ShapeDtypeStruct((T, D), BF16), jax.ShapeDtypeStruct((1, D), F32)],
                     name=name, rows=T, tr=256, acc_outs=(1,))


def _rms_pre_bwd(h, w_pre, du, dh_out, *, name):
    T, D = h.shape

    def body(h_ref, w_ref, du_ref, dho_ref, dh_ref, dw_ref):
        @pl.when(pl.program_id(0) == 0)
        def _():
            dw_ref[...] = jnp.zeros_like(dw_ref)

        dx, dw = _rms_bwd_vals(h_ref[...], w_ref[...], du_ref[...])
        dh_ref[...] = dho_ref[...] + dx
        dw_ref[...] += dw

    return _row_call(body, [(h, True), (w_pre, False), (du, True), (dh_out, True)],
                     [jax.ShapeDtypeStruct((T, D), F32), jax.ShapeDtypeStruct((1, D), F32)],
                     name=name, rows=T, tr=256, acc_outs=(1,))


def _ret_consts(T, C, H):
    lg = np.log1p(-np.exp2(-5.0 - np.arange(H, dtype=np.float64)))
    idx = np.arange(C, dtype=np.float64)
    dist = np.abs(idx[:, None] - idx[None, :])
    vis = (idx[None, :] // REF_CHUNK) <= (idx[:, None] // REF_CHUNK)
    mask = np.exp(dist[None] * lg[:, None, None]) * vis[None]
    xi = np.exp((idx[None, :] + 1.0) * lg[:, None])[..., None]
    zeta = np.exp((C - 1.0 - idx)[None, :] * lg[:, None])[..., None]
    half = RET_DK // 2
    inv_freq = ROPE_BASE ** (-np.arange(half, dtype=np.float32) / np.float32(half))
    ang = np.arange(T, dtype=np.float32)[:, None] * inv_freq[None, :].astype(np.float32)
    return (jnp.asarray(mask, F32), jnp.asarray(xi, F32), jnp.asarray(zeta, F32),
            jnp.asarray(np.cos(ang), F32), jnp.asarray(np.sin(ang), F32))


def _rot(t, cos, sin):
    half = RET_DK // 2
    t1, t2 = t[:, :half], t[:, half:]
    return jnp.concatenate([t1 * cos - t2 * sin, t1 * sin + t2 * cos], axis=1)


def _unrot(d, cos, sin):
    half = RET_DK // 2
    d1, d2 = d[:, :half], d[:, half:]
    return jnp.concatenate([d1 * cos + d2 * sin, d2 * cos - d1 * sin], axis=1)


def _ret_specs(C, H, rev, NS):
    def ci(i):
        return NS - 1 - i if rev else i

    nq = H
    q_spec = pl.BlockSpec((C, RET_DK), lambda h, i: (ci(i), h))
    k_spec = pl.BlockSpec((C, RET_DK), lambda h, i: (ci(i), nq + h))
    v_spec = pl.BlockSpec((C, RET_DV), lambda h, i: (ci(i), H + h))
    g_spec = pl.BlockSpec((C, RET_DV), lambda h, i: (ci(i), 2 * H + h))
    cs_spec = pl.BlockSpec((C, RET_DK // 2), lambda h, i: (ci(i), 0))
    m_spec = pl.BlockSpec((None, C, C), lambda h, i: (h, 0, 0))
    vec_spec = pl.BlockSpec((None, C, 1), lambda h, i: (h, 0, 0))
    gn_spec = pl.BlockSpec((1, RET_DV), lambda h, i: (0, h))
    st_spec = pl.BlockSpec((None, None, RET_DK, RET_DV), lambda h, i: (h, ci(i), 0, 0))
    return q_spec, k_spec, v_spec, g_spec, cs_spec, m_spec, vec_spec, gn_spec, st_spec


def _ret_fwd_vals(q, k, v, cos, sin, mask, xi, s_in):
    qr = _rot(q, cos, sin)
    kr = _rot(k, cos, sin) * (RET_DK ** -0.5)
    a = _dot(qr, kr, NT) * mask
    o = _dot(a, v) + _dot(qr, s_in) * xi
    mu = jnp.mean(o, axis=1, keepdims=True)
    oc = o - mu
    rstd = lax.rsqrt(jnp.mean(oc * oc, axis=1, keepdims=True) + GN_EPS)
    return qr, kr, a, oc * rstd, rstd


def _ret_fwd(proj, gn_w, consts, *, C, name):
    T = proj.shape[0]
    H = gn_w.shape[1] // RET_DV
    NS = T // C
    mask, xi, zeta, cos, sin = consts
    q_spec, k_spec, v_spec, g_spec, cs_spec, m_spec, vec_spec, gn_spec, st_spec = _ret_specs(C, H, False, NS)
    y_spec = pl.BlockSpec((C, RET_DV), lambda h, i: (i, h))

    def body(q_ref, k_ref, v_ref, g_ref, cos_ref, sin_ref, m_ref, xi_ref, ze_ref, gn_ref, y_ref, st_ref, S):
        @pl.when(pl.program_id(1) == 0)
        def _():
            S[...] = jnp.zeros_like(S)

        s_in = S[...]
        st_ref[...] = s_in
        v = v_ref[...]
        xi_v = xi_ref[...]
        qr, kr, a, on, rstd = _ret_fwd_vals(q_ref[...], k_ref[...], v, cos_ref[...], sin_ref[...], m_ref[...], xi_v, s_in)
        g = g_ref[...]
        y_ref[...] = (g * _sigmoid(g) * on * gn_ref[...]).astype(BF16)
        S[...] = s_in * xi_v[C - 1:C, :] + _dot(kr * ze_ref[...], v, TN)

    return pl.pallas_call(
        body, name=name, grid=(H, NS),
        in_specs=[q_spec, k_spec, v_spec, g_spec, cs_spec, cs_spec, m_spec, vec_spec, vec_spec, gn_spec],
        out_specs=[y_spec, st_spec],
        out_shape=[jax.ShapeDtypeStruct((T, H * RET_DV), BF16), jax.ShapeDtypeStruct((H, NS, RET_DK, RET_DV), F32)],
        scratch_shapes=[pltpu.VMEM((RET_DK, RET_DV), F32)],
        compiler_params=_cparams("parallel", "arbitrary"),
    )(proj, proj, proj, proj, cos, sin, mask, xi, zeta, gn_w)


def _ret_bwd(proj, gn_w, consts, states, dy, *, C, name):
    T = proj.shape[0]
    H = gn_w.shape[1] // RET_DV
    NS = T // C
    mask, xi, zeta, cos, sin = consts
    q_spec, k_spec, v_spec, g_spec, cs_spec, m_spec, vec_spec, gn_spec, st_spec = _ret_specs(C, H, True, NS)
    dy_spec = pl.BlockSpec((C, RET_DV), lambda h, i: (NS - 1 - i, h))
    dqk_spec = pl.BlockSpec((C, RET_DK), lambda h, i: (NS - 1 - i, h))
    scale = RET_DK ** -0.5

    def body(q_ref, k_ref, v_ref, g_ref, cos_ref, sin_ref, m_ref, xi_ref, ze_ref, gn_ref, st_ref, dy_ref,
             dq_ref, dk_ref, dv_ref, dg_ref, dgn_ref, dS):
        @pl.when(pl.program_id(1) == 0)
        def _():
            dS[...] = jnp.zeros_like(dS)
            dgn_ref[...] = jnp.zeros_like(dgn_ref)

        s_in = st_ref[...]
        v = v_ref[...]
        cos, sin, mask, xi_v, ze = cos_ref[...], sin_ref[...], m_ref[...], xi_ref[...], ze_ref[...]
        qr, kr, a, on, rstd = _ret_fwd_vals(q_ref[...], k_ref[...], v, cos, sin, mask, xi_v, s_in)
        g = g_ref[...]
        sg = _sigmoid(g)
        silu = g * sg
        gnw = gn_ref[...]
        dy = dy_ref[...].astype(F32)
        dg_ref[...] = (dy * on * gnw * (sg * (1.0 + g * (1.0 - sg)))).astype(BF16)
        t = dy * silu
        dgn_ref[...] += _colsum(t * on)
        don = t * gnw
        do = rstd * (don - jnp.mean(don, axis=1, keepdims=True) - on * jnp.mean(don * on, axis=1, keepdims=True))
        dox = do * xi_v
        ds_out = dS[...]
        da = _dot(do, v, NT) * mask
        kz = kr * ze
        dv_ref[...] = (_dot(a, do, TN) + _dot(kz, ds_out)).astype(BF16)
        dqr = _dot(da, kr) + _dot(dox, s_in, NT)
        dkr = _dot(da, qr, TN) + _dot(v, ds_out, NT) * ze
        dS[...] = ds_out * xi_v[C - 1:C, :] + _dot(qr, dox, TN)
        dq_ref[...] = _unrot(dqr, cos, sin).astype(BF16)
        dk_ref[...] = _unrot(dkr * scale, cos, sin).astype(BF16)

    return pl.pallas_call(
        body, name=name, grid=(H, NS),
        in_specs=[q_spec, k_spec, v_spec, g_spec, cs_spec, cs_spec, m_spec, vec_spec, vec_spec, gn_spec, st_spec, dy_spec],
        out_specs=[dqk_spec, dqk_spec, dy_spec, dy_spec, gn_spec],
        out_shape=[jax.ShapeDtypeStruct((T, H * RET_DK), BF16), jax.ShapeDtypeStruct((T, H * RET_DK), BF16),
                   jax.ShapeDtypeStruct((T, H * RET_DV), BF16), jax.ShapeDtypeStruct((T, H * RET_DV), BF16),
                   jax.ShapeDtypeStruct((1, H * RET_DV), F32)],
        scratch_shapes=[pltpu.VMEM((RET_DK, RET_DV), F32)],
        compiler_params=_cparams("parallel", "arbitrary"),
    )(proj, proj, proj, proj, cos, sin, mask, xi, zeta, gn_w, states, dy)


def _shift_down(x, prev8, k):
    if k == 0:
        return x
    y = pltpu.roll(x, k, 0)
    row = lax.broadcasted_iota(jnp.int32, prev8.shape, 0)
    top = jnp.where(row < k, pltpu.roll(prev8, k, 0), y[:8])
    return jnp.concatenate([top, y[8:]], axis=0)


def _shift_up(x, next8, k):
    if k == 0:
        return x
    n = x.shape[0]
    y = pltpu.roll(x, n - k, 0)
    row = lax.broadcasted_iota(jnp.int32, next8.shape, 0)
    bot = jnp.where(row >= 8 - k, pltpu.roll(next8, 8 - k, 0), y[n - 8:])
    return jnp.concatenate([y[:n - 8], bot], axis=0)


def _conv_silu(raw, halo, w, b):
    cv = b
    for tap in range(SSD_CONV_W):
        cv = cv + _shift_down(raw, halo, SSD_CONV_W - 1 - tap) * w[tap:tap + 1, :]
    sg = _sigmoid(cv)
    return cv * sg, cv, sg


def _conv_silu_bwd(d_post, cv, sg, raw, halo, w, carry8):
    dcv = d_post * (sg * (1.0 + cv * (1.0 - sg)))
    d_raw = jnp.zeros_like(raw)
    dws = []
    for tap in range(SSD_CONV_W):
        k = SSD_CONV_W - 1 - tap
        d_raw = d_raw + _shift_up(dcv, carry8, k) * w[tap:tap + 1, :]
        dws.append(_colsum(dcv * _shift_down(raw, halo, k)))
    return d_raw, jnp.concatenate(dws, axis=0), _colsum(dcv), dcv[:8]


def _softplus(x):
    return jnp.maximum(x, 0.0) + jnp.log1p(jnp.exp(-jnp.abs(x)))


def _ssd_common(C, R, dt, dtT, bias, biasT, alog, alogT, E):
    p = dt + bias
    dtv = _softplus(p)
    a = -jnp.exp(alog)
    da = dtv * a
    daT = _softplus(dtT + biasT) * (-jnp.exp(alogT))
    row = lax.broadcasted_iota(jnp.int32, (C, C), 0)
    col = lax.broadcasted_iota(jnp.int32, (C, C), 1)
    tril = row >= col
    trilf = jnp.where(tril, 1.0, 0.0).astype(F32)
    triuf = jnp.where(col >= row, 1.0, 0.0).astype(F32)
    acum = _dot_sel(trilf, da, split="b")
    acumT = _dot_sel(daT, trilf, NT, split="a")
    al = acum[C - 1:C, :]
    ea = jnp.exp(acum)
    dte = jnp.exp(al - acum)
    eal = jnp.exp(al)
    return dict(p=p, dtv=dtv, a=a, da=da, tril=tril, triuf=triuf, acum=acum, acumT=acumT, al=al, ea=ea, dte=dte, eal=eal,
                dtv_e=_dot_sel(dtv, E, split="a", terms=2), ea_e=_dot_sel(ea, E, split="a", terms=2),
                dte_e=_dot_sel(dte, E, split="a", terms=2), eal_e=_dot_sel(eal, E, split="a"))


def _head_decay(q, r, C, R):
    seg = jnp.broadcast_to(q["acum"][:, r:r + 1], (C, C)) - q["acumT"][r:r + 1, :]
    return jnp.exp(jnp.where(q["tril"], seg, -1e30))


def _ssd_group_specs(C, R, NS, rev):
    RP = R * SSD_P
    G = SSD_G
    nz = 1
    hb = C // 8

    def ci(i):
        return NS - 1 - i if rev else i

    def halo_row(i):
        return jnp.maximum(ci(i) * hb - 1, 0)

    off_b = G * RP // SSD_N
    z_spec = pl.BlockSpec((C, RP), lambda g, i: (ci(i), g))
    x_spec = pl.BlockSpec((C, RP), lambda g, i: (ci(i), G + g))
    b_spec = pl.BlockSpec((C, SSD_N), lambda g, i: (ci(i), 2 * off_b + g))
    c_spec = pl.BlockSpec((C, SSD_N), lambda g, i: (ci(i), 2 * off_b + G + g))
    xh_spec = pl.BlockSpec((8, RP), lambda g, i: (halo_row(i), G + g))
    bh_spec = pl.BlockSpec((8, SSD_N), lambda g, i: (halo_row(i), 2 * off_b + g))
    ch_spec = pl.BlockSpec((8, SSD_N), lambda g, i: (halo_row(i), 2 * off_b + G + g))
    dt_spec = pl.BlockSpec((None, C, R), lambda g, i: (g, ci(i), 0))
    dtT_spec = pl.BlockSpec((None, R, C), lambda g, i: (g, 0, ci(i)))
    pr_spec = pl.BlockSpec((None, 1, R), lambda g, i: (g, 0, 0))
    prT_spec = pl.BlockSpec((None, R, 1), lambda g, i: (g, 0, 0))
    cwx_spec = pl.BlockSpec((SSD_CONV_W, RP), lambda g, i: (0, g))
    cwn_spec = pl.BlockSpec((SSD_CONV_W, SSD_N), lambda g, i: (0, g))
    cbx_spec = pl.BlockSpec((1, RP), lambda g, i: (0, g))
    cbn_spec = pl.BlockSpec((1, SSD_N), lambda g, i: (0, g))
    e_spec = pl.BlockSpec((R, RP), lambda g, i: (0, 0))
    st_spec = pl.BlockSpec((None, None, SSD_N, RP), lambda g, i: (g, ci(i), 0, 0))
    return dict(z=z_spec, x=x_spec, b=b_spec, c=c_spec, xh=xh_spec, bh=bh_spec, ch=ch_spec, dt=dt_spec, dtT=dtT_spec,
                pr=pr_spec, prT=prT_spec, cwx=cwx_spec, cwn=cwn_spec, cbx=cbx_spec, cbn=cbn_spec, e=e_spec, st=st_spec)


def _ssd_forward_vals(C, R, refs, first, s_in):
    E = refs["E"]
    halo_on = jnp.where(first, 0.0, 1.0)
    xh, bh, ch = refs["xh"] * halo_on, refs["bh"] * halo_on, refs["ch"] * halo_on
    xs, cvx, sgx = _conv_silu(refs["x"], xh, refs["cwx"], refs["cbx"])
    bm, cvb, sgb = _conv_silu(refs["b"], bh, refs["cwb"], refs["cbb"])
    cm, cvc, sgc = _conv_silu(refs["c"], ch, refs["cwc"], refs["cbc"])
    q = _ssd_common(C, R, refs["dt"], refs["dtT"], refs["bias"], refs["biasT"], refs["alog"], refs["alogT"], E)
    xdt = xs * q["dtv_e"]
    cb = _dot(cm, bm, NT)
    yoff_raw = _dot(cm, s_in)
    xdt_b = xdt.astype(BF16)
    low = lax.broadcasted_iota(jnp.int32, (1, 2 * SSD_P), 1) < SSD_P
    pairs = []
    for j in range(R // 2):
        xp = xdt_b[:, 2 * SSD_P * j:2 * SSD_P * (j + 1)]
        y0 = _dot(cb * _head_decay(q, 2 * j, C, R), xp)
        y1 = _dot(cb * _head_decay(q, 2 * j + 1, C, R), xp)
        pairs.append(jnp.where(low, y0, y1))
    ydiag = jnp.concatenate(pairs, axis=1)
    d_e =_dot_sel(refs["dskip"], E, split="a")
    y = ydiag + yoff_raw * q["ea_e"] + d_e * xs
    xd = xdt * q["dte_e"]
    s_out = s_in * q["eal_e"] + _dot(bm, xd, TN)
    z = refs["z"]
    sgz = _sigmoid(z)
    yz = y * (z * sgz)
    rn = lax.rsqrt(jnp.mean(yz * yz, axis=1, keepdims=True) + RMS_EPS)
    return dict(q=q, xh=xh, bh=bh, ch=ch, xs=xs, cvx=cvx, sgx=sgx, bm=bm, cvb=cvb, sgb=sgb, cm=cm, cvc=cvc, sgc=sgc,
                xdt=xdt, cb=cb, yoff_raw=yoff_raw, d_e=d_e, y=y, xd=xd, s_out=s_out, z=z, sgz=sgz, yz=yz, rn=rn)


_SSD_IN_NAMES = ("z", "x", "b", "c", "xh", "bh", "ch", "dt", "dtT", "bias", "biasT", "alog", "alogT", "dskip",
                 "cwx", "cwb", "cwc", "cbx", "cbb", "cbc", "nw", "E")


def _ssd_inputs(pm, dt_g, dtT_g, prm, sp):
    bias, biasT, alog, alogT, dskip, cwx, cwb, cwc, cbx, cbb, cbc, nw, E = prm
    args = [pm, pm, pm, pm, pm, pm, pm, dt_g, dtT_g, bias, biasT, alog, alogT, dskip, cwx, cwb, cwc, cbx, cbb, cbc, nw, E]
    specs = [sp["z"], sp["x"], sp["b"], sp["c"], sp["xh"], sp["bh"], sp["ch"], sp["dt"], sp["dtT"], sp["pr"], sp["prT"],
             sp["pr"], sp["prT"], sp["pr"], sp["cwx"], sp["cwn"], sp["cwn"], sp["cbx"], sp["cbn"], sp["cbn"], sp["cbx"], sp["e"]]
    return args, specs


def _ssd_fwd(pm, dt_g, dtT_g, prm, *, C, R, name):
    T = pm.shape[0]
    NS = T // C
    RP = R * SSD_P
    G = SSD_G
    sp = _ssd_group_specs(C, R, NS, False)
    args, specs = _ssd_inputs(pm, dt_g, dtT_g, prm, sp)
    nin = len(args)

    def body(*refs):
        ins = {n: r[...] for n, r in zip(_SSD_IN_NAMES, refs[:nin])}
        y_ref, st_ref, S = refs[nin:]
        first = pl.program_id(1) == 0

        @pl.when(first)
        def _():
            S[...] = jnp.zeros_like(S)

        s_in = S[...]
        st_ref[...] = s_in
        f = _ssd_forward_vals(C, R, ins, first, s_in)
        y_ref[...] = (f["yz"] * f["rn"] * ins["nw"]).astype(BF16)
        S[...] = f["s_out"]

    return pl.pallas_call(
        body, name=name, grid=(G, NS), in_specs=specs,
        out_specs=[pl.BlockSpec((C, RP), lambda g, i: (i, g)), sp["st"]],
        out_shape=[jax.ShapeDtypeStruct((T, G * RP), BF16), jax.ShapeDtypeStruct((G, NS, SSD_N, RP), F32)],
        scratch_shapes=[pltpu.VMEM((SSD_N, RP), F32)],
        compiler_params=_cparams("parallel", "arbitrary"),
    )(*args)


def _ssd_bwd(pm, dt_g, dtT_g, prm, states, dout, *, C, R, name):
    T = pm.shape[0]
    NS = T // C
    RP = R * SSD_P
    G = SSD_G
    sp = _ssd_group_specs(C, R, NS, True)
    args, specs = _ssd_inputs(pm, dt_g, dtT_g, prm, sp)
    nin = len(args)
    rows_spec = pl.BlockSpec((C, RP), lambda g, i: (NS - 1 - i, g))
    rown_spec = pl.BlockSpec((C, SSD_N), lambda g, i: (NS - 1 - i, g))
    args = args + [states, dout]
    specs = specs + [sp["st"], rows_spec]

    def body(*refs):
        ins = {n: r[...] for n, r in zip(_SSD_IN_NAMES, refs[:nin])}
        st_ref, dout_ref = refs[nin], refs[nin + 1]
        (dz_ref, dx_ref, db_ref, dc_ref, ddt_ref, dbias_ref, dalog_ref, dd_ref, dcwx_ref, dcwb_ref, dcwc_ref,
         dcbx_ref, dcbb_ref, dcbc_ref, dnw_ref) = refs[nin + 2:nin + 17]
        dS, cx8, cb8, cc8 = refs[nin + 17:]
        acc_refs = (dbias_ref, dalog_ref, dd_ref, dcwx_ref, dcwb_ref, dcwc_ref, dcbx_ref, dcbb_ref, dcbc_ref, dnw_ref)
        step = pl.program_id(1)

        @pl.when(step == 0)
        def _():
            for r_ in acc_refs + (dS, cx8, cb8, cc8):
                r_[...] = jnp.zeros_like(r_)

        first = step == NS - 1
        E = ins["E"]
        s_in = st_ref[...]
        f = _ssd_forward_vals(C, R, ins, first, s_in)
        q = f["q"]
        xs, bm, cm, xdt, cb, y, z, sgz, yz, rn = (f[n] for n in ("xs", "bm", "cm", "xdt", "cb", "y", "z", "sgz", "yz", "rn"))
        nw = ins["nw"]
        dout = dout_ref[...].astype(F32)
        yh = yz * rn
        dnw_ref[...] += _colsum(dout * yh)
        g1 = dout * nw
        dyz = rn * (g1 - yh * jnp.mean(g1 * yh, axis=1, keepdims=True))
        dz_ref[...] = (dyz * y * (sgz * (1.0 + z * (1.0 - sgz)))).astype(BF16)
        dy = dyz * (z * sgz)
        dd_ref[...] += _dot_sel(_colsum(dy * xs), E, NT, split="a")
        dxs = dy * f["d_e"]
        dyo = dy * q["ea_e"]
        dcm = _dot(dyo, s_in, NT)
        ds_acc = _dot(cm, dyo, TN)
        dacum = _dot_sel(dy * f["yoff_raw"], E, NT, split="a", terms=2) * q["ea"]
        dacumT = jnp.zeros((R, C), F32)
        dcb = jnp.zeros((C, C), F32)
        rowR = lax.broadcasted_iota(jnp.int32, (1, R), 1)
        rowRT = lax.broadcasted_iota(jnp.int32, (R, 1), 0)
        dy_b, xdt_b = dy.astype(BF16), xdt.astype(BF16)
        low = lax.broadcasted_iota(jnp.int32, (1, 2 * SSD_P), 1) < SSD_P
        dxdt_pairs = []
        for j in range(R // 2):
            lanes = slice(2 * SSD_P * j, 2 * SSD_P * (j + 1))
            dyp, xp = dy_b[:, lanes], xdt_b[:, lanes]
            halves = []
            for r, mine in ((2 * j, low), (2 * j + 1, jnp.logical_not(low))):
                lr = _head_decay(q, r, C, R)
                w_r = cb * lr
                dw = _dot(jnp.where(mine, dyp, jnp.zeros_like(dyp)), xp, NT)
                halves.append(_dot(w_r, dyp, TN))
                dcb = dcb + dw * lr
                dseg = dw * w_r
                dacum = dacum + jnp.sum(dseg, axis=1, keepdims=True) * jnp.where(rowR == r, 1.0, 0.0)
                dacumT = dacumT - _colsum(dseg) * jnp.where(rowRT == r, 1.0, 0.0)
            dxdt_pairs.append(jnp.where(low, halves[0], halves[1]))
        dxdt = jnp.concatenate(dxdt_pairs, axis=1)
        dsn = dS[...]
        ds_acc = ds_acc + dsn * q["eal_e"]
        d_eal = _dot_sel(_colsum(dsn * s_in), E, NT, split="a")
        dbm = _dot(f["xd"], dsn, NT)
        dxd = _dot(bm, dsn)
        dxdt = dxdt + dxd * q["dte_e"]
        d_dte = _dot_sel(dxd * xdt, E, NT, split="a", terms=2) * q["dte"]
        d_al = _colsum(d_dte) + d_eal * q["eal"]
        dacum = dacum - d_dte
        rowC = lax.broadcasted_iota(jnp.int32, (C, 1), 0)
        dacum = dacum + jnp.where(rowC == C - 1, 1.0, 0.0) * d_al
        dS[...] = ds_acc
        dcm = dcm + _dot(dcb, bm)
        dbm = dbm + _dot(dcb, cm, TN)
        eye = jnp.where(lax.broadcasted_iota(jnp.int32, (C, C), 0) == lax.broadcasted_iota(jnp.int32, (C, C), 1), 1.0, 0.0)
        dacum = dacum + _dot_sel(eye, dacumT, NT, split="b")
        dda = _dot_sel(q["triuf"], dacum, split="b")
        ddtv = dda * q["a"] + _dot_sel(dxdt * xs, E, NT, split="a", terms=2)
        dalog_ref[...] += _colsum(dda * q["dtv"]) * q["a"]
        dxs = dxs + dxdt * q["dtv_e"]
        dp = ddtv * _sigmoid(q["p"])
        ddt_ref[...] = dp
        dbias_ref[...] += _colsum(dp)
        d_raw, d_w, d_b, c8 = _conv_silu_bwd(dxs, f["cvx"], f["sgx"], ins["x"], f["xh"], ins["cwx"], cx8[...])
        dx_ref[...] = d_raw.astype(BF16)
        dcwx_ref[...] += d_w
        dcbx_ref[...] += d_b
        cx8[...] = c8
        d_raw, d_w, d_b, c8 = _conv_silu_bwd(dbm, f["cvb"], f["sgb"], ins["b"], f["bh"], ins["cwb"], cb8[...])
        db_ref[...] = d_raw.astype(BF16)
        dcwb_ref[...] += d_w
        dcbb_ref[...] += d_b
        cb8[...] = c8
        d_raw, d_w, d_b, c8 = _conv_silu_bwd(dcm, f["cvc"], f["sgc"], ins["c"], f["ch"], ins["cwc"], cc8[...])
        dc_ref[...] = d_raw.astype(BF16)
        dcwc_ref[...] += d_w
        dcbc_ref[...] += d_b
        cc8[...] = c8

    out_specs = [rows_spec, rows_spec, rown_spec, rown_spec,
                 pl.BlockSpec((None, C, R), lambda g, i: (g, NS - 1 - i, 0)),
                 sp["pr"], sp["pr"], sp["pr"], sp["cwx"], sp["cwn"], sp["cwn"], sp["cbx"], sp["cbn"], sp["cbn"], sp["cbx"]]
    out_shape = [jax.ShapeDtypeStruct((T, G * RP), BF16), jax.ShapeDtypeStruct((T, G * RP), BF16),
                 jax.ShapeDtypeStruct((T, G * SSD_N), BF16), jax.ShapeDtypeStruct((T, G * SSD_N), BF16),
                 jax.ShapeDtypeStruct((G, T, R), F32),
                 jax.ShapeDtypeStruct((G, 1, R), F32), jax.ShapeDtypeStruct((G, 1, R), F32), jax.ShapeDtypeStruct((G, 1, R), F32),
                 jax.ShapeDtypeStruct((SSD_CONV_W, G * RP), F32), jax.ShapeDtypeStruct((SSD_CONV_W, G * SSD_N), F32),
                 jax.ShapeDtypeStruct((SSD_CONV_W, G * SSD_N), F32),
                 jax.ShapeDtypeStruct((1, G * RP), F32), jax.ShapeDtypeStruct((1, G * SSD_N), F32),
                 jax.ShapeDtypeStruct((1, G * SSD_N), F32), jax.ShapeDtypeStruct((1, G * RP), F32)]
    return pl.pallas_call(
        body, name=name, grid=(G, NS), in_specs=specs, out_specs=out_specs, out_shape=out_shape,
        scratch_shapes=[pltpu.VMEM((SSD_N, RP), F32), pltpu.VMEM((8, RP), F32), pltpu.VMEM((8, SSD_N), F32),
                        pltpu.VMEM((8, SSD_N), F32)],
        compiler_params=_cparams("parallel", "arbitrary"),
    )(*args)


_ANY = pl.BlockSpec(memory_space=pl.ANY)


def _chip_peer(k):
    x, y, c = lax.axis_index("x"), lax.axis_index("y"), lax.axis_index("c")
    return (x ^ (k >> 1), y ^ (k & 1), c)


def _my_chip():
    return 2 * lax.axis_index("x") + lax.axis_index("y")


def _all_gather_chips(shards, halved, *, name):
    n = len(shards)

    def body(*refs):
        ins, outs = refs[:n], refs[n:2 * n]
        send, recv, fsend, frecv, loc = refs[2 * n:]
        s = _my_chip()
        c = lax.axis_index("c")
        sibling = (lax.axis_index("x"), lax.axis_index("y"), 1 - c)
        copies = []
        for a in range(n):
            cp = pltpu.make_async_copy(ins[a], outs[a].at[s], loc.at[a])
            cp.start()
            copies.append(cp)

        def rows(a, core):
            if not halved[a]:
                return slice(None)
            half = shards[a].shape[0] // 2
            return pl.ds(pl.multiple_of(core * half, 16), half)

        def over_ici(a, k, slot, core):
            return pltpu.make_async_remote_copy(
                src_ref=ins[a].at[rows(a, core)], dst_ref=outs[a].at[slot, rows(a, core)],
                send_sem=send.at[3 * a + k - 1], recv_sem=recv.at[3 * a + k - 1],
                device_id=_chip_peer(k), device_id_type=MESH_ID)

        def over_d2d(a, k, core):
            z = outs[a].at[s ^ k, rows(a, core)]
            return pltpu.make_async_remote_copy(
                src_ref=z, dst_ref=z, send_sem=fsend.at[3 * a + k - 1], recv_sem=frecv.at[3 * a + k - 1],
                device_id=sibling, device_id_type=MESH_ID)

        sent = []
        for a in range(n):
            for k in (1, 2, 3):
                cp = over_ici(a, k, s, c)
                cp.start()
                sent.append(cp)
        passed = []
        for a in range(n):
            for k in (1, 2, 3):
                over_ici(a, k, s ^ k, c).wait_recv()
                if halved[a]:
                    cp = over_d2d(a, k, c)
                    cp.start()
                    passed.append(cp)
        for a in range(n):
            if halved[a]:
                for k in (1, 2, 3):
                    over_d2d(a, k, 1 - c).wait_recv()
        for cp in sent + passed:
            cp.wait_send()
        for cp in copies:
            cp.wait()

    for a, h in zip(shards, halved):
        assert not h or a.shape[0] % 32 == 0, a.shape
    return pl.pallas_call(
        body, name=name, in_specs=[_ANY] * n, out_specs=[_ANY] * n,
        out_shape=[jax.ShapeDtypeStruct((4,) + a.shape, a.dtype) for a in shards],
        scratch_shapes=[pltpu.SemaphoreType.DMA((3 * n,))] * 4 + [pltpu.SemaphoreType.DMA((n,))],
        compiler_params=pltpu.CompilerParams(has_side_effects=True),
    )(*shards)


_HBM = pl.BlockSpec(memory_space=pltpu.HBM)
_SEM = pl.BlockSpec(memory_space=pltpu.SEMAPHORE)
_EFFECT = pltpu.SideEffectType.DATAFLOW_SIDE_EFFECTING


def _split_copies(src, land, send, recv, loc, a, scatter):
    s = _my_chip()
    mine = pltpu.make_async_copy(src.at[s] if scatter else src, land.at[s], loc.at[a])
    pairs = []
    for k in (1, 2, 3):
        sems = dict(send_sem=send.at[3 * a + k - 1], recv_sem=recv.at[3 * a + k - 1],
                    device_id=_chip_peer(k), device_id_type=MESH_ID)
        out = pltpu.make_async_remote_copy(src_ref=src.at[s ^ k] if scatter else src, dst_ref=land.at[s], **sems)
        arriving = pltpu.make_async_remote_copy(src_ref=src.at[s ^ k] if scatter else src, dst_ref=land.at[s ^ k], **sems)
        pairs.append((out, arriving))
    return mine, pairs


def _split_start(arrs, *, scatter, after, name):
    n = len(arrs)
    zones = [lax.empty(a.shape if scatter else (4,) + a.shape, a.dtype) for a in arrs]

    def body(*refs):
        srcs, lands = refs[:n], refs[n:2 * n]
        send, recv, loc = refs[2 * n + 1:2 * n + 4]
        token = refs[-1]
        for a in range(n):
            mine, pairs = _split_copies(srcs[a], lands[a], send, recv, loc, a, scatter)
            mine.start()
            for out, _ in pairs:
                out.start()
        token[...] = jnp.zeros_like(token)

    res = pl.pallas_call(
        body, name=name,
        out_shape=(pltpu.SemaphoreType.DMA((3 * n,)), pltpu.SemaphoreType.DMA((3 * n,)), pltpu.SemaphoreType.DMA((n,)),
                   *[pltpu.HBM(a.shape, a.dtype) for a in arrs], *[pltpu.HBM(z.shape, z.dtype) for z in zones],
                   jax.ShapeDtypeStruct((8, 128), F32)),
        in_specs=[_HBM] * (2 * n) + [_ANY],
        out_specs=(_SEM, _SEM, _SEM, *([_HBM] * (2 * n)), pl.BlockSpec(memory_space=pltpu.VMEM)),
        input_output_aliases={i: 3 + i for i in range(2 * n)},
        compiler_params=pltpu.CompilerParams(has_side_effects=_EFFECT),
    )(*[pltpu.with_memory_space_constraint(a, pltpu.HBM) for a in arrs],
      *[pltpu.with_memory_space_constraint(z, pltpu.HBM) for z in zones], after)
    return res[:3], list(res[3:3 + n]), list(res[3 + n:3 + 2 * n]), res[-1]


def _split_wait(sems, src, land, a, *, scatter, after, name):
    def body(src_ref, land_ref, send, recv, loc, after_ref, src_out, land_out):
        mine, pairs = _split_copies(src_ref, land_ref, send, recv, loc, a, scatter)
        mine.wait()
        for out, arriving in pairs:
            out.wait_send()
            arriving.wait_recv()

    return pl.pallas_call(
        body, name=name, out_shape=(pltpu.HBM(src.shape, src.dtype), pltpu.HBM(land.shape, land.dtype)),
        in_specs=[_HBM, _HBM, _SEM, _SEM, _SEM, _ANY], out_specs=(_HBM, _HBM), input_output_aliases={0: 0, 1: 1},
        compiler_params=pltpu.CompilerParams(has_side_effects=_EFFECT),
    )(src, land, *sems, after)[1]


def _swap_with_sibling(arrs, *, name):
    n = len(arrs)

    def body(*refs):
        ins, outs = refs[:n], refs[n:2 * n]
        send, recv = refs[2 * n:]
        sib = (lax.axis_index("x"), lax.axis_index("y"), 1 - lax.axis_index("c"))
        cps = []
        for a in range(n):
            cp = pltpu.make_async_remote_copy(src_ref=ins[a], dst_ref=outs[a], send_sem=send.at[a], recv_sem=recv.at[a],
                                              device_id=sib, device_id_type=MESH_ID)
            cp.start()
            cps.append(cp)
        for cp in cps:
            cp.wait()

    return pl.pallas_call(
        body, name=name, in_specs=[_ANY] * n, out_specs=[_ANY] * n,
        out_shape=[jax.ShapeDtypeStruct(a.shape, a.dtype) for a in arrs],
        scratch_shapes=[pltpu.SemaphoreType.DMA((n,)), pltpu.SemaphoreType.DMA((n,))],
        compiler_params=pltpu.CompilerParams(has_side_effects=True),
    )(*arrs)


def _all_gather_devices(v, *, name):
    r = v.shape[0]

    def body(v_ref, out_ref, send, recv):
        x, y, c = lax.axis_index("x"), lax.axis_index("y"), lax.axis_index("c")
        me = 4 * x + 2 * y + c
        out_ref[me] = v_ref[...]
        cps = []
        for k in range(1, 8):
            peer = (x ^ (k >> 2), y ^ ((k >> 1) & 1), c ^ (k & 1))
            cp = pltpu.make_async_remote_copy(src_ref=v_ref, dst_ref=out_ref.at[me], send_sem=send.at[k - 1],
                                              recv_sem=recv.at[k - 1], device_id=peer, device_id_type=MESH_ID)
            cp.start()
            cps.append(cp)
        for k, cp in enumerate(cps, start=1):
            cp.wait_send()
            pltpu.make_async_remote_copy(src_ref=v_ref, dst_ref=out_ref.at[me ^ k], send_sem=send.at[k - 1],
                                         recv_sem=recv.at[k - 1], device_id=(x, y, c), device_id_type=MESH_ID).wait_recv()

    vm = pl.BlockSpec(memory_space=pltpu.VMEM)
    return pl.pallas_call(
        body, name=name, in_specs=[vm], out_specs=vm, out_shape=jax.ShapeDtypeStruct((8, r, 128), F32),
        scratch_shapes=[pltpu.SemaphoreType.DMA((7,)), pltpu.SemaphoreType.DMA((7,))],
        compiler_params=pltpu.CompilerParams(has_side_effects=True),
    )(v)


def _row_tile(r, target):
    best = None
    for t in range(16, min(target, r) + 1, 16):
        if r % t == 0:
            best = t
    return best or r


def _sum_slots(buf, *, name, tr=384, layer=None, stack=None):
    S, r, c = buf.shape
    tr = _row_tile(r, tr)

    def body(*refs):
        b_ref, o_ref = refs[0], refs[-1]
        acc = b_ref[0].astype(F32)
        for j in range(1, S):
            acc = acc + b_ref[j].astype(F32)
        o_ref[...] = acc

    in_specs = [pl.BlockSpec((S, tr, c), lambda i: (0, i, 0))]
    args, alias = [buf], {}
    if layer is None:
        out_spec, out_shape = pl.BlockSpec((tr, c), lambda i: (i, 0)), jax.ShapeDtypeStruct((r, c), F32)
    else:
        out_spec = pl.BlockSpec((None, tr, c), lambda i: (layer, i, 0))
        out_shape = jax.ShapeDtypeStruct((2, r, c), F32)
        if stack is not None:
            in_specs.append(_ANY)
            args.append(stack)
            alias = {1: 0}
    return pl.pallas_call(
        body, name=name, grid=(r // tr,), in_specs=in_specs, out_specs=out_spec, out_shape=out_shape,
        input_output_aliases=alias, compiler_params=_cparams("parallel"),
    )(*args)


def _add2(a, b, *, name, tr=384):
    r, c = a.shape
    tr = _row_tile(r, tr)

    def body(a_ref, b_ref, o_ref):
        o_ref[...] = a_ref[...] + b_ref[...]

    spec = pl.BlockSpec((tr, c), lambda i: (i, 0))
    return pl.pallas_call(body, name=name, grid=(r // tr,), in_specs=[spec, spec], out_specs=spec,
                          out_shape=jax.ShapeDtypeStruct((r, c), F32), compiler_params=_cparams("parallel"))(a, b)


ADAMW_BLOCK_ELEMS = 1 << 18


def _adamw(w, gs, m, v, *, name, tr=256):
    r, c = w.shape
    tr = _row_tile(r, min(tr, max(16, ADAMW_BLOCK_ELEMS // c)))
    bc1 = 1.0 - ADAM_B1 ** ADAM_STEP
    bc2 = 1.0 - ADAM_B2 ** ADAM_STEP
    ng = len(gs)

    def body(*refs):
        w_ref, m_ref, v_ref = refs[0], refs[1 + ng], refs[2 + ng]
        g_ref, d_ref, mo_ref, vo_ref = refs[3 + ng:]
        gg = refs[1][...] if ng == 1 else refs[1][...] + refs[2][...]
        mn = ADAM_B1 * m_ref[...] + (1.0 - ADAM_B1) * gg
        vn = ADAM_B2 * v_ref[...] + (1.0 - ADAM_B2) * (gg * gg)
        g_ref[...] = gg
        mo_ref[...] = mn
        vo_ref[...] = vn
        d_ref[...] = -ADAM_LR * ((mn / bc1) / (jnp.sqrt(vn / bc2) + ADAM_EPS) + ADAM_WD * w_ref[...])

    spec = pl.BlockSpec((tr, c), lambda i: (i, 0))
    return pl.pallas_call(body, name=name, grid=(r // tr,), in_specs=[spec] * (3 + ng), out_specs=[spec] * 4,
                          out_shape=[jax.ShapeDtypeStruct((r, c), F32)] * 4,
                          compiler_params=_cparams("parallel"))(w, *gs, m, v)


def _pack(vecs, rows):
    flat = jnp.concatenate([v.reshape(-1).astype(F32) for v in vecs])
    return jnp.pad(flat, (0, rows * 128 - flat.shape[0])).reshape(rows, 128)


def _unpack(packed, shapes):
    flat = packed.reshape(-1)
    out, off = [], 0
    for s in shapes:
        n = math.prod(s)
        out.append(flat[off:off + n].reshape(s))
        off += n
    return out


def _pack_rows(shapes):
    n = sum(math.prod(s) for s in shapes)
    return -(-n // 1024) * 8


def kernel(x, norm_mix_pre, norm_mix_post, norm_ffn_pre, norm_ffn_post, ret_w_in, ret_gn_w, ret_w_out, ssd_w_in, ssd_conv_w, ssd_conv_b, ssd_dt_bias, ssd_a_log, ssd_d, ssd_norm_w, ssd_w_out, mlp_w_up, mlp_w_down, loss_target, m_norm_mix_pre, m_norm_mix_post, m_norm_ffn_pre, m_norm_ffn_post, m_ret_w_in, m_ret_gn_w, m_ret_w_out, m_ssd_w_in, m_ssd_conv_w, m_ssd_conv_b, m_ssd_dt_bias, m_ssd_a_log, m_ssd_d, m_ssd_norm_w, m_ssd_w_out, m_mlp_w_up, m_mlp_w_down, v_norm_mix_pre, v_norm_mix_post, v_norm_ffn_pre, v_norm_ffn_post, v_ret_w_in, v_ret_gn_w, v_ret_w_out, v_ssd_w_in, v_ssd_conv_w, v_ssd_conv_b, v_ssd_dt_bias, v_ssd_a_log, v_ssd_d, v_ssd_norm_w, v_ssd_w_out, v_mlp_w_up, v_mlp_w_down):
    T, D = x.shape[1], x.shape[2]
    H = D // RET_DK
    d_inner = 2 * D
    R = d_inner // SSD_P // SSD_G
    RP = R * SSD_P
    n_heads = SSD_G * R
    conv_dim = d_inner + 2 * SSD_G * SSD_N
    n_main = d_inner + conv_dim
    C = min(256, T)
    chip = _my_chip()
    xs, tgt = x[0], loss_target[0]

    conv_sh = ssd_conv_w.shape[2]
    small_shapes = [(SSD_CONV_W, conv_sh), (conv_sh,), (ssd_norm_w.shape[1],)]
    small_rows = _pack_rows(small_shapes)
    shards = [ret_w_in[0].T.astype(BF16), ret_w_out[0].astype(BF16), ssd_w_in[0].T.astype(BF16), ssd_w_out[0].astype(BF16),
              mlp_w_up[0].T.astype(BF16), mlp_w_up[1].T.astype(BF16), mlp_w_down[0].astype(BF16), mlp_w_down[1].astype(BF16)]
    (ret_in_g, small_g) = _all_gather_chips([shards[0], _pack([ssd_conv_w[0], ssd_conv_b[0], ssd_norm_w[0]], small_rows)],
                                            [True, False], name="gather_first")

    def full(g):
        return g.reshape(4 * g.shape[1], g.shape[2])

    def start_gather(idx, after, name):
        sems, srcs, zones, tok = _split_start([shards[i] for i in idx], scatter=False, after=after, name=name)
        return {i: (sems, srcs[n], zones[n], n) for n, i in enumerate(idx)}, tok

    def arrived(stage, i, after, name):
        sems, src, zone, n = stage[i]
        return full(_split_wait(sems, src, zone, n, scatter=False, after=after, name=name))

    ret_in_t = full(ret_in_g)
    sm = [_unpack(small_g[j], small_shapes) for j in range(4)]
    conv_w = jnp.concatenate([sm[j][0] for j in range(4)], axis=1)
    conv_b = jnp.concatenate([sm[j][1] for j in range(4)])[None, :]
    norm_w = jnp.concatenate([sm[j][2] for j in range(4)])[None, :]

    gb = SSD_G * SSD_N
    ssd_prm = (ssd_dt_bias.reshape(SSD_G, 1, R), ssd_dt_bias.reshape(SSD_G, R, 1),
               ssd_a_log.reshape(SSD_G, 1, R), ssd_a_log.reshape(SSD_G, R, 1), ssd_d.reshape(SSD_G, 1, R),
               conv_w[:, :d_inner], conv_w[:, d_inner:d_inner + gb], conv_w[:, d_inner + gb:],
               conv_b[:, :d_inner], conv_b[:, d_inner:d_inner + gb], conv_b[:, d_inner + gb:],
               norm_w, jnp.asarray(np.kron(np.eye(R), np.ones((1, SSD_P))), F32))
    ret_consts = _ret_consts(T, C, H)

    u0 = _rms_pre(xs, norm_mix_pre[0:1], name="pre0")
    stage1, tok = start_gather((1, 4), ret_in_g, "gather_start1")
    proj = _matmul(u0, ret_in_t, "nt", out_dtype=F32, name="ret_in", after=tok)
    stage2, tok = start_gather((6, 2), proj, "gather_start2")
    y_ret, st_ret = _ret_fwd(proj, ret_gn_w, ret_consts, C=C, name="ret_fwd")
    ret_out = arrived(stage1, 1, y_ret, "gather_wait_ret_out")
    m0 = _matmul(y_ret, ret_out, "nn", out_dtype=F32, name="ret_out", after=tok)
    h1, u1 = _rms_post_pre(xs, m0, norm_mix_post[0:1], norm_ffn_pre[0:1], name="post_pre1")
    up_t0 = arrived(stage1, 4, u1, "gather_wait_up0")
    a0, hh0 = _matmul(u1, up_t0, "nt", out_dtype=BF16, name="mlp_up0", epi="relu2")
    stage3, tok = start_gather((3, 5, 7), hh0, "gather_start3")
    down0 = arrived(stage2, 6, hh0, "gather_wait_down0")
    f0 = _matmul(hh0, down0, "nn", out_dtype=F32, name="mlp_down0", after=tok)
    h2, u2 = _rms_post_pre(h1, f0, norm_ffn_post[0:1], norm_mix_pre[1:2], name="post_pre2")
    ssd_in_t = arrived(stage2, 2, u2, "gather_wait_ssd_in")
    ssd_main_t, ssd_dt_t = ssd_in_t[:n_main], ssd_in_t[n_main:]
    pm = _matmul(u2, ssd_main_t, "nt", out_dtype=F32, name="ssd_in")
    pdt = _matmul(u2, ssd_dt_t, "nt", out_dtype=F32, name="ssd_in_dt")
    dt_g = pdt.reshape(T, SSD_G, R).transpose(1, 0, 2)
    dtT_g = pdt.reshape(T, SSD_G, R).transpose(1, 2, 0)
    y_ssd, st_ssd = _ssd_fwd(pm, dt_g, dtT_g, ssd_prm, C=C, R=R, name="ssd_fwd")
    ssd_out = arrived(stage3, 3, y_ssd, "gather_wait_ssd_out")
    m1 = _matmul(y_ssd, ssd_out, "nn", out_dtype=F32, name="ssd_out")
    h3, u3 = _rms_post_pre(h2, m1, norm_mix_post[1:2], norm_ffn_pre[1:2], name="post_pre3")
    up_t1 = arrived(stage3, 5, u3, "gather_wait_up1")
    a1, hh1 = _matmul(u3, up_t1, "nt", out_dtype=BF16, name="mlp_up1", epi="relu2")
    down1 = arrived(stage3, 7, hh1, "gather_wait_down1")
    f1 = _matmul(hh1, down1, "nn", out_dtype=F32, name="mlp_down1")
    up_t, down = (up_t0, up_t1), (down0, down1)
    dh4, sq = _rms_post_loss(h3, f1, norm_ffn_post[1:2], tgt, name="post_loss")
    loss = lax.psum(sq[0, 0], MESH_AXES) * (0.5 / D)

    in_flight = []

    def send_grad(g, name):
        part = g if g.ndim == 3 else g.reshape(4, g.shape[0] // 4, g.shape[1])
        sems, srcs, zones, tok = _split_start([part], scatter=True, after=part, name=f"scatter_start_{name}")
        in_flight.append((name, sems, srcs[0], zones[0]))
        return tok

    def mlp_bwd(i, dh_out, h_in, u, a, hh, f):
        df, d_post = _rms_post_bwd(f, norm_ffn_post[i:i + 1], dh_out, name=f"post_bwd_ffn{i}")
        tok = send_grad(_matmul(hh, df, "tn", out_dtype=BF16, name=f"mlp_down_wg{i}"), f"down{i}")
        da = _matmul(df, down[i], "nt", out_dtype=BF16, name=f"mlp_down_dg{i}", epi="drelu2", extra=a, after=tok)
        tok = send_grad(_matmul(u, da, "tn", out_dtype=BF16, name=f"mlp_up_wg{i}", col_parts=4), f"up{i}")
        du = _matmul(da, up_t[i], "nn", out_dtype=F32, name=f"mlp_up_dg{i}", after=tok)
        dh, d_pre = _rms_pre_bwd(h_in, norm_ffn_pre[i:i + 1], du, dh_out, name=f"pre_bwd_ffn{i}")
        return dh, d_pre, d_post

    dh3, d_nfp1, d_nfpost1 = mlp_bwd(1, dh4, h3, u3, a1, hh1, f1)
    dm1, d_nmpost1 = _rms_post_bwd(m1, norm_mix_post[1:2], dh3, name="post_bwd_mix1")
    tok = send_grad(_matmul(y_ssd, dm1, "tn", out_dtype=BF16, name="ssd_out_wg"), "ssd_out")
    dy_ssd = _matmul(dm1, ssd_out, "nt", out_dtype=F32, name="ssd_out_dg", after=tok)
    (dz, dxr, dbr, dcr, ddt_g, d_bias, d_alog, d_dskip, dcwx, dcwb, dcwc, dcbx, dcbb, dcbc, d_normw) = _ssd_bwd(
        pm, dt_g, dtT_g, ssd_prm, st_ssd, dy_ssd, C=C, R=R, name="ssd_bwd")
    dpm = jnp.concatenate([dz, dxr, dbr, dcr], axis=1)
    dpdt = ddt_g.transpose(1, 0, 2).reshape(T, n_heads).astype(BF16)
    tok = send_grad(jnp.concatenate([_matmul(dpm, u2, "tn", out_dtype=BF16, name="ssd_in_wg"),
                                     _matmul(dpdt, u2, "tn", out_dtype=BF16, name="ssd_in_dt_wg")], axis=0), "ssd_in")
    du2 = _matmul(dpm, ssd_main_t, "nn", out_dtype=F32, name="ssd_in_dg", after=tok)
    du2 = _matmul(dpdt, ssd_dt_t, "nn", out_dtype=F32, name="ssd_in_dt_dg", epi="add", extra=du2)
    dh2, d_nmp1 = _rms_pre_bwd(h2, norm_mix_pre[1:2], du2, dh3, name="pre_bwd_mix1")
    dh1, d_nfp0, d_nfpost0 = mlp_bwd(0, dh2, h1, u1, a0, hh0, f0)
    dm0, d_nmpost0 = _rms_post_bwd(m0, norm_mix_post[0:1], dh1, name="post_bwd_mix0")
    tok = send_grad(_matmul(y_ret, dm0, "tn", out_dtype=BF16, name="ret_out_wg"), "ret_out")
    dy_ret = _matmul(dm0, ret_out, "nt", out_dtype=F32, name="ret_out_dg", after=tok)
    dq, dk, dv, dg, d_gn = _ret_bwd(proj, ret_gn_w, ret_consts, st_ret, dy_ret, C=C, name="ret_bwd")
    dproj = jnp.concatenate([dq, dk, dv, dg], axis=1)
    tok = send_grad(_matmul(u0, dproj, "tn", out_dtype=BF16, name="ret_in_wg", col_parts=4), "ret_in")
    du0 = _matmul(dproj, ret_in_t, "nn", out_dtype=F32, name="ret_in_dg", after=tok)
    grad_x, d_nmp0 = _rms_pre_bwd(xs, norm_mix_pre[0:1], du0, dh1, name="pre_bwd_mix0")

    landed = {nm: _split_wait(sems, src, zone, 0, scatter=True, after=grad_x, name=f"scatter_wait_{nm}")
              for nm, sems, src, zone in in_flight}
    part_sum = {}
    for nm in ("down1", "up1", "ssd_out", "ssd_in", "down0", "up0", "ret_out", "ret_in"):
        if nm[-1] in "01" and nm[:-1] in ("up", "down"):
            fam, layer = nm[:-1], int(nm[-1])
            part_sum[fam] = _sum_slots(landed[nm], name=f"sum_chips_{nm}", layer=layer, stack=part_sum.get(fam))
        else:
            part_sum[nm] = _sum_slots(landed[nm], name=f"sum_chips_{nm}")
    fams = ("ret_in", "ret_out", "ssd_in", "ssd_out", "up", "down")
    swapped = _swap_with_sibling([part_sum[f].reshape(-1, part_sum[f].shape[-1]) for f in fams], name="swap_partials")
    other = dict(zip(fams, swapped))

    def upd(w, gs, m, v, name):
        shp = w.shape
        w2, m2, v2 = (t.reshape(-1, shp[-1]) for t in (w, m, v))
        return tuple(t.reshape(shp) for t in _adamw(w2, [g.reshape(w2.shape) for g in gs], m2, v2, name=name))

    g_ssd_in = _add2(part_sum["ssd_in"], other["ssd_in"], name="sum_cores_ssd_in").T
    res = {}
    res["ret_w_in"] = upd(ret_w_in, [part_sum["ret_in"], other["ret_in"]], m_ret_w_in, v_ret_w_in, "adamw_ret_in")
    res["ret_w_out"] = upd(ret_w_out, [part_sum["ret_out"], other["ret_out"]], m_ret_w_out, v_ret_w_out, "adamw_ret_out")
    res["ssd_w_in"] = upd(ssd_w_in, [g_ssd_in], m_ssd_w_in, v_ssd_w_in, "adamw_ssd_in")
    res["ssd_w_out"] = upd(ssd_w_out, [part_sum["ssd_out"], other["ssd_out"]], m_ssd_w_out, v_ssd_w_out, "adamw_ssd_out")
    res["mlp_w_up"] = upd(mlp_w_up, [part_sum["up"], other["up"]], m_mlp_w_up, v_mlp_w_up, "adamw_up")
    res["mlp_w_down"] = upd(mlp_w_down, [part_sum["down"], other["down"]], m_mlp_w_down, v_mlp_w_down, "adamw_down")

    d_conv_w = jnp.concatenate([dcwx, dcwb, dcwc], axis=1)
    d_conv_b = jnp.concatenate([dcbx, dcbb, dcbc], axis=1)
    small_grads = [jnp.concatenate([d_nmp0, d_nmp1]), jnp.concatenate([d_nmpost0, d_nmpost1]),
                   jnp.concatenate([d_nfp0, d_nfp1]), jnp.concatenate([d_nfpost0, d_nfpost1]),
                   d_gn, d_bias.reshape(1, n_heads), d_alog.reshape(1, n_heads), d_dskip.reshape(1, n_heads),
                   d_conv_w, d_conv_b, d_normw]
    sg_shapes = [g.shape for g in small_grads]
    sg_rows = _pack_rows(sg_shapes)
    everyone = _all_gather_devices(_pack(small_grads, sg_rows), name="gather_small_grads")
    sg = _unpack(_sum_slots(everyone, name="sum_small_grads", tr=sg_rows), sg_shapes)
    (g_nmp, g_nmpost, g_nfp, g_nfpost, g_gn, g_bias, g_alog, g_dskip, g_cw_full, g_cb_full, g_nw_full) = sg
    g_cw = lax.dynamic_slice_in_dim(g_cw_full, chip * conv_sh, conv_sh, axis=1)[None]
    g_cb = lax.dynamic_slice_in_dim(g_cb_full, chip * conv_sh, conv_sh, axis=1)
    nw_sh = ssd_norm_w.shape[1]
    g_nw = lax.dynamic_slice_in_dim(g_nw_full, chip * nw_sh, nw_sh, axis=1)
    small = [("norm_mix_pre", norm_mix_pre, g_nmp, m_norm_mix_pre, v_norm_mix_pre),
             ("norm_mix_post", norm_mix_post, g_nmpost, m_norm_mix_post, v_norm_mix_post),
             ("norm_ffn_pre", norm_ffn_pre, g_nfp, m_norm_ffn_pre, v_norm_ffn_pre),
             ("norm_ffn_post", norm_ffn_post, g_nfpost, m_norm_ffn_post, v_norm_ffn_post),
             ("ret_gn_w", ret_gn_w, g_gn, m_ret_gn_w, v_ret_gn_w),
             ("ssd_conv_w", ssd_conv_w, g_cw, m_ssd_conv_w, v_ssd_conv_w),
             ("ssd_conv_b", ssd_conv_b, g_cb, m_ssd_conv_b, v_ssd_conv_b),
             ("ssd_dt_bias", ssd_dt_bias, g_bias, m_ssd_dt_bias, v_ssd_dt_bias),
             ("ssd_a_log", ssd_a_log, g_alog, m_ssd_a_log, v_ssd_a_log),
             ("ssd_d", ssd_d, g_dskip, m_ssd_d, v_ssd_d),
             ("ssd_norm_w", ssd_norm_w, g_nw, m_ssd_norm_w, v_ssd_norm_w)]
    sw_shapes = [w.shape for _, w, _, _, _ in small]
    sw_rows = _pack_rows(sw_shapes)
    packs = [_pack([t[j] for t in small], sw_rows) for j in (1, 2, 3, 4)]
    _, d_p, m_p, v_p = _adamw(packs[0], [packs[1]], packs[2], packs[3], name="adamw_small", tr=sw_rows)
    d_s, m_s, v_s = _unpack(d_p, sw_shapes), _unpack(m_p, sw_shapes), _unpack(v_p, sw_shapes)
    for j, (nm, w, g, _, _) in enumerate(small):
        res[nm] = (g.reshape(w.shape), d_s[j], m_s[j], v_s[j])

    order = ["norm_mix_pre", "norm_mix_post", "norm_ffn_pre", "norm_ffn_post", "ret_w_in", "ret_gn_w", "ret_w_out",
             "ssd_w_in", "ssd_conv_w", "ssd_conv_b", "ssd_dt_bias", "ssd_a_log", "ssd_d", "ssd_norm_w", "ssd_w_out",
             "mlp_w_up", "mlp_w_down"]
    return (loss, grad_x[None], *[res[n][0] for n in order], *[res[n][1] for n in order],
            *[res[n][2] for n in order], *[res[n][3] for n in order])
```

```python
import functools
import math

import numpy as np
import jax
import jax.numpy as jnp
from jax import lax
from jax.experimental import pallas as pl
from jax.experimental.pallas import tpu as pltpu

F32 = jnp.float32
BF16 = jnp.bfloat16
HI = lax.Precision.HIGHEST
VMEM_LIMIT_BYTES = 56 * 1024 * 1024
MESH_AXES = ("x", "y", "c")
MESH_ID = pl.DeviceIdType.MESH

RMS_EPS = 1e-6
GN_EPS = 1e-5
RET_DK = 256
RET_DV = 512
ROPE_BASE = 10000.0
REF_CHUNK = 64
SSD_P = 64
SSD_N = 128
SSD_G = 8
SSD_CONV_W = 4
ADAM_LR, ADAM_B1, ADAM_B2, ADAM_EPS, ADAM_WD, ADAM_STEP = 0.001, 0.9, 0.999, 1e-08, 0.01, 10

NN = (((1,), (0,)), ((), ()))
NT = (((1,), (1,)), ((), ()))
TN = (((0,), (0,)), ((), ()))


def _cparams(*sem):
    return pltpu.CompilerParams(dimension_semantics=sem, vmem_limit_bytes=VMEM_LIMIT_BYTES)


def _dot(a, b, dims=NN):
    return lax.dot_general(a.astype(BF16), b.astype(BF16), dims, preferred_element_type=F32)


def _split_bf16(x, terms):
    parts, rest = [], x
    for _ in range(terms):
        p = rest.astype(BF16)
        parts.append(p)
        rest = rest - p.astype(F32)
    return parts


def _dot_sel(a, b, dims=NN, *, split, terms=3):
    if split == "a":
        sel = b.astype(BF16)
        return sum(lax.dot_general(p, sel, dims, preferred_element_type=F32) for p in _split_bf16(a, terms))
    sel = a.astype(BF16)
    return sum(lax.dot_general(sel, p, dims, preferred_element_type=F32) for p in _split_bf16(b, terms))


def _sigmoid(x):
    return 1.0 / (1.0 + jnp.exp(-x))


def _colsum(x):
    return jnp.sum(x, axis=0, keepdims=True)


MM_TILE = 1024
MM_FULL_K = 2048


def _mm_tiles(M, N, K):
    tm = min(M, MM_TILE)
    if K <= MM_FULL_K:
        return tm, min(N, MM_TILE), K
    return tm, min(N, 2 * MM_TILE), MM_TILE


def _matmul(a, b, mode, *, out_dtype, name, epi=None, extra=None, after=None, col_parts=None):
    if mode == "nn":
        (M, K), (K2, N) = a.shape, b.shape
    elif mode == "nt":
        (M, K), (N, K2) = a.shape, b.shape
    else:
        (K, M), (K2, N) = a.shape, b.shape
    assert K == K2, (a.shape, b.shape, mode)
    tm, tn, tk = _mm_tiles(M, N, K)
    if col_parts:
        tm, tn = min(M, 2 * MM_TILE), min(tn, MM_TILE)
        while (N // col_parts) % tn:
            tn //= 2
    assert M % tm == 0 and N % tn == 0 and K % tk == 0, (M, N, K, tm, tn, tk)
    nk = K // tk
    if mode == "tn":
        a_spec = pl.BlockSpec((tk, tm), lambda i, j, k: (k, i))
    else:
        a_spec = pl.BlockSpec((tm, tk), lambda i, j, k: (i, k))
    if mode == "nt":
        b_spec = pl.BlockSpec((tn, tk), lambda i, j, k: (j, k))
    else:
        b_spec = pl.BlockSpec((tk, tn), lambda i, j, k: (k, j))
    dims = {"nn": NN, "nt": NT, "tn": TN}[mode]
    o_spec = pl.BlockSpec((tm, tn), lambda i, j, k: (i, j))
    out_dims = (M, N)
    if col_parts:
        per = N // col_parts // tn
        o_spec = pl.BlockSpec((None, tm, tn), lambda i, j, k: (j // per, i, j % per))
        out_dims = (col_parts, M, N // col_parts)
    has_extra = epi in ("drelu2", "add")
    n_out = 2 if epi == "relu2" else 1
    n_in = 2 + int(has_extra) + int(after is not None)

    def body(*refs):
        a_ref, b_ref = refs[0], refs[1]
        e_ref = refs[2] if has_extra else None
        outs = refs[n_in:n_in + n_out]

        def finish(r):
            if epi is None:
                outs[0][...] = r.astype(outs[0].dtype)
            elif epi == "relu2":
                outs[0][...] = r.astype(outs[0].dtype)
                h = jnp.maximum(r, 0.0)
                outs[1][...] = (h * h).astype(outs[1].dtype)
            elif epi == "drelu2":
                av = jnp.maximum(e_ref[...].astype(F32), 0.0)
                outs[0][...] = (r * (2.0 * av)).astype(outs[0].dtype)
            else:
                outs[0][...] = (r + e_ref[...].astype(F32)).astype(outs[0].dtype)

        if nk == 1:
            finish(_dot(a_ref[...], b_ref[...], dims))
            return
        acc = refs[-1]
        k = pl.program_id(2)

        @pl.when(k == 0)
        def _():
            acc[...] = jnp.zeros_like(acc)

        acc[...] += _dot(a_ref[...], b_ref[...], dims)

        @pl.when(k == nk - 1)
        def _():
            finish(acc[...])

    in_specs = [a_spec, b_spec] + ([o_spec] if has_extra else [])
    args = [a, b] + ([extra] if has_extra else [])
    if after is not None:
        in_specs.append(pl.BlockSpec(after.shape, lambda i, j, k: (0, 0)))
        args.append(after)
    out_shape = [jax.ShapeDtypeStruct(out_dims, out_dtype)] * n_out
    res = pl.pallas_call(
        body, name=name, grid=(M // tm, N // tn, nk), in_specs=in_specs, out_specs=[o_spec] * n_out,
        out_shape=out_shape, scratch_shapes=[pltpu.VMEM((tm, tn), F32)] if nk > 1 else [],
        compiler_params=_cparams("parallel", "parallel", "arbitrary"),
    )(*args)
    return res if n_out == 2 else res[0]


def _rstd(x):
    return lax.rsqrt(jnp.mean(x * x, axis=-1, keepdims=True) + RMS_EPS)


def _row_call(body, ins, outs_shape, *, name, rows, tr, acc_outs=()):
    tr = min(tr, rows)
    assert rows % tr == 0
    in_specs = []
    for arr, blocked in ins:
        if blocked:
            in_specs.append(pl.BlockSpec((tr, arr.shape[1]), lambda i: (i, 0)))
        else:
            in_specs.append(pl.BlockSpec(arr.shape, lambda i: (0, 0)))
    out_specs = []
    for n, s in enumerate(outs_shape):
        if n in acc_outs:
            out_specs.append(pl.BlockSpec(s.shape, lambda i: (0, 0)))
        else:
            out_specs.append(pl.BlockSpec((tr, s.shape[1]), lambda i: (i, 0)))
    return pl.pallas_call(
        body, name=name, grid=(rows // tr,), in_specs=in_specs, out_specs=out_specs, out_shape=outs_shape,
        compiler_params=_cparams("arbitrary" if acc_outs else "parallel"),
    )(*[a for a, _ in ins])


def _rms_pre(h, w, *, name):
    T, D = h.shape

    def body(h_ref, w_ref, u_ref):
        x = h_ref[...]
        u_ref[...] = (x * _rstd(x) * w_ref[...]).astype(BF16)

    return _row_call(body, [(h, True), (w, False)], [jax.ShapeDtypeStruct((T, D), BF16)], name=name, rows=T, tr=256)[0]


def _rms_post_pre(h, m, w_post, w_pre, *, name):
    T, D = h.shape

    def body(h_ref, m_ref, wp_ref, wn_ref, hn_ref, u_ref):
        mm = m_ref[...]
        hn = h_ref[...] + mm * _rstd(mm) * wp_ref[...]
        hn_ref[...] = hn
        u_ref[...] = (hn * _rstd(hn) * wn_ref[...]).astype(BF16)

    return _row_call(body, [(h, True), (m, True), (w_post, False), (w_pre, False)],
                     [jax.ShapeDtypeStruct((T, D), F32), jax.ShapeDtypeStruct((T, D), BF16)], name=name, rows=T, tr=256)


def _rms_post_loss(h, m, w_post, tgt, *, name):
    T, D = h.shape

    def body(h_ref, m_ref, wp_ref, t_ref, dh_ref, loss_ref):
        @pl.when(pl.program_id(0) == 0)
        def _():
            loss_ref[...] = jnp.zeros_like(loss_ref)

        mm = m_ref[...]
        err = h_ref[...] + mm * _rstd(mm) * wp_ref[...] - t_ref[...]
        dh_ref[...] = err * (1.0 / D)
        loss_ref[...] += _colsum(jnp.sum(err * err, axis=1, keepdims=True))

    return _row_call(body, [(h, True), (m, True), (w_post, False), (tgt, True)],
                     [jax.ShapeDtypeStruct((T, D), F32), jax.ShapeDtypeStruct((1, 1), F32)],
                     name=name, rows=T, tr=256, acc_outs=(1,))


def _rms_bwd_vals(x, w, dy):
    r = _rstd(x)
    xh = x * r
    g = dy * w
    dx = r * (g - xh * jnp.mean(g * xh, axis=-1, keepdims=True))
    return dx, _colsum(dy * xh)


def _rms_post_bwd(m, w_post, dh, *, name):
    T, D = m.shape

    def body(m_ref, w_ref, dh_ref, dm_ref, dw_ref):
        @pl.when(pl.program_id(0) == 0)
        def _():
            dw_ref[...] = jnp.zeros_like(dw_ref)

        dx, dw = _rms_bwd_vals(m_ref[...], w_ref[...], dh_ref[...])
        dm_ref[...] = dx.astype(BF16)
        dw_ref[...] += dw

    return _row_call(body, [(m, True), (w_post, False), (dh, True)],
                     [jax.ShapeDtypeStruct((T, D), BF16), jax.ShapeDtypeStruct((1, D), F32)],
                     name=name, rows=T, tr=256, acc_outs=(1,))


def _rms_pre_bwd(h, w_pre, du, dh_out, *, name):
    T, D = h.shape

    def body(h_ref, w_ref, du_ref, dho_ref, dh_ref, dw_ref):
        @pl.when(pl.program_id(0) == 0)
        def _():
            dw_ref[...] = jnp.zeros_like(dw_ref)

        dx, dw = _rms_bwd_vals(h_ref[...], w_ref[...], du_ref[...])
        dh_ref[...] = dho_ref[...] + dx
        dw_ref[...] += dw

    return _row_call(body, [(h, True), (w_pre, False), (du, True), (dh_out, True)],
                     [jax.ShapeDtypeStruct((T, D), F32), jax.ShapeDtypeStruct((1, D), F32)],
                     name=name, rows=T, tr=256, acc_outs=(1,))


def _ret_consts(T, C, H):
    lg = np.log1p(-np.exp2(-5.0 - np.arange(H, dtype=np.float64)))
    idx = np.arange(C, dtype=np.float64)
    dist = np.abs(idx[:, None] - idx[None, :])
    vis = (idx[None, :] // REF_CHUNK) <= (idx[:, None] // REF_CHUNK)
    mask = np.exp(dist[None] * lg[:, None, None]) * vis[None]
    xi = np.exp((idx[None, :] + 1.0) * lg[:, None])[..., None]
    zeta = np.exp((C - 1.0 - idx)[None, :] * lg[:, None])[..., None]
    half = RET_DK // 2
    inv_freq = ROPE_BASE ** (-np.arange(half, dtype=np.float32) / np.float32(half))
    ang = np.arange(T, dtype=np.float32)[:, None] * inv_freq[None, :].astype(np.float32)
    return (jnp.asarray(mask, F32), jnp.asarray(xi, F32), jnp.asarray(zeta, F32),
            jnp.asarray(np.cos(ang), F32), jnp.asarray(np.sin(ang), F32))


def _rot(t, cos, sin):
    half = RET_DK // 2
    t1, t2 = t[:, :half], t[:, half:]
    return jnp.concatenate([t1 * cos - t2 * sin, t1 * sin + t2 * cos], axis=1)


def _unrot(d, cos, sin):
    half = RET_DK // 2
    d1, d2 = d[:, :half], d[:, half:]
    return jnp.concatenate([d1 * cos + d2 * sin, d2 * cos - d1 * sin], axis=1)


def _ret_specs(C, H, rev, NS):
    def ci(i):
        return NS - 1 - i if rev else i

    nq = H
    q_spec = pl.BlockSpec((C, RET_DK), lambda h, i: (ci(i), h))
    k_spec = pl.BlockSpec((C, RET_DK), lambda h, i: (ci(i), nq + h))
    v_spec = pl.BlockSpec((C, RET_DV), lambda h, i: (ci(i), H + h))
    g_spec = pl.BlockSpec((C, RET_DV), lambda h, i: (ci(i), 2 * H + h))
    cs_spec = pl.BlockSpec((C, RET_DK // 2), lambda h, i: (ci(i), 0))
    m_spec = pl.BlockSpec((None, C, C), lambda h, i: (h, 0, 0))
    vec_spec = pl.BlockSpec((None, C, 1), lambda h, i: (h, 0, 0))
    gn_spec = pl.BlockSpec((1, RET_DV), lambda h, i: (0, h))
    st_spec = pl.BlockSpec((None, None, RET_DK, RET_DV), lambda h, i: (h, ci(i), 0, 0))
    return q_spec, k_spec, v_spec, g_spec, cs_spec, m_spec, vec_spec, gn_spec, st_spec


def _ret_fwd_vals(q, k, v, cos, sin, mask, xi, s_in):
    qr = _rot(q, cos, sin)
    kr = _rot(k, cos, sin) * (RET_DK ** -0.5)
    a = _dot(qr, kr, NT) * mask
    o = _dot(a, v) + _dot(qr, s_in) * xi
    mu = jnp.mean(o, axis=1, keepdims=True)
    oc = o - mu
    rstd = lax.rsqrt(jnp.mean(oc * oc, axis=1, keepdims=True) + GN_EPS)
    return qr, kr, a, oc * rstd, rstd


def _ret_fwd(proj, gn_w, consts, *, C, name):
    T = proj.shape[0]
    H = gn_w.shape[1] // RET_DV
    NS = T // C
    mask, xi, zeta, cos, sin = consts
    q_spec, k_spec, v_spec, g_spec, cs_spec, m_spec, vec_spec, gn_spec, st_spec = _ret_specs(C, H, False, NS)
    y_spec = pl.BlockSpec((C, RET_DV), lambda h, i: (i, h))

    def body(q_ref, k_ref, v_ref, g_ref, cos_ref, sin_ref, m_ref, xi_ref, ze_ref, gn_ref, y_ref, st_ref, S):
        @pl.when(pl.program_id(1) == 0)
        def _():
            S[...] = jnp.zeros_like(S)

        s_in = S[...]
        st_ref[...] = s_in
        v = v_ref[...]
        xi_v = xi_ref[...]
        qr, kr, a, on, rstd = _ret_fwd_vals(q_ref[...], k_ref[...], v, cos_ref[...], sin_ref[...], m_ref[...], xi_v, s_in)
        g = g_ref[...]
        y_ref[...] = (g * _sigmoid(g) * on * gn_ref[...]).astype(BF16)
        S[...] = s_in * xi_v[C - 1:C, :] + _dot(kr * ze_ref[...], v, TN)

    return pl.pallas_call(
        body, name=name, grid=(H, NS),
        in_specs=[q_spec, k_spec, v_spec, g_spec, cs_spec, cs_spec, m_spec, vec_spec, vec_spec, gn_spec],
        out_specs=[y_spec, st_spec],
        out_shape=[jax.ShapeDtypeStruct((T, H * RET_DV), BF16), jax.ShapeDtypeStruct((H, NS, RET_DK, RET_DV), F32)],
        scratch_shapes=[pltpu.VMEM((RET_DK, RET_DV), F32)],
        compiler_params=_cparams("parallel", "arbitrary"),
    )(proj, proj, proj, proj, cos, sin, mask, xi, zeta, gn_w)


def _ret_bwd(proj, gn_w, consts, states, dy, *, C, name):
    T = proj.shape[0]
    H = gn_w.shape[1] // RET_DV
    NS = T // C
    mask, xi, zeta, cos, sin = consts
    q_spec, k_spec, v_spec, g_spec, cs_spec, m_spec, vec_spec, gn_spec, st_spec = _ret_specs(C, H, True, NS)
    dy_spec = pl.BlockSpec((C, RET_DV), lambda h, i: (NS - 1 - i, h))
    dqk_spec = pl.BlockSpec((C, RET_DK), lambda h, i: (NS - 1 - i, h))
    scale = RET_DK ** -0.5

    def body(q_ref, k_ref, v_ref, g_ref, cos_ref, sin_ref, m_ref, xi_ref, ze_ref, gn_ref, st_ref, dy_ref,
             dq_ref, dk_ref, dv_ref, dg_ref, dgn_ref, dS):
        @pl.when(pl.program_id(1) == 0)
        def _():
            dS[...] = jnp.zeros_like(dS)
            dgn_ref[...] = jnp.zeros_like(dgn_ref)

        s_in = st_ref[...]
        v = v_ref[...]
        cos, sin, mask, xi_v, ze = cos_ref[...], sin_ref[...], m_ref[...], xi_ref[...], ze_ref[...]
        qr, kr, a, on, rstd = _ret_fwd_vals(q_ref[...], k_ref[...], v, cos, sin, mask, xi_v, s_in)
        g = g_ref[...]
        sg = _sigmoid(g)
        silu = g * sg
        gnw = gn_ref[...]
        dy = dy_ref[...].astype(F32)
        dg_ref[...] = (dy * on * gnw * (sg * (1.0 + g * (1.0 - sg)))).astype(BF16)
        t = dy * silu
        dgn_ref[...] += _colsum(t * on)
        don = t * gnw
        do = rstd * (don - jnp.mean(don, axis=1, keepdims=True) - on * jnp.mean(don * on, axis=1, keepdims=True))
        dox = do * xi_v
        ds_out = dS[...]
        da = _dot(do, v, NT) * mask
        kz = kr * ze
        dv_ref[...] = (_dot(a, do, TN) + _dot(kz, ds_out)).astype(BF16)
        dqr = _dot(da, kr) + _dot(dox, s_in, NT)
        dkr = _dot(da, qr, TN) + _dot(v, ds_out, NT) * ze
        dS[...] = ds_out * xi_v[C - 1:C, :] + _dot(qr, dox, TN)
        dq_ref[...] = _unrot(dqr, cos, sin).astype(BF16)
        dk_ref[...] = _unrot(dkr * scale, cos, sin).astype(BF16)

    return pl.pallas_call(
        body, name=name, grid=(H, NS),
        in_specs=[q_spec, k_spec, v_spec, g_spec, cs_spec, cs_spec, m_spec, vec_spec, vec_spec, gn_spec, st_spec, dy_spec],
        out_specs=[dqk_spec, dqk_spec, dy_spec, dy_spec, gn_spec],
        out_shape=[jax.ShapeDtypeStruct((T, H * RET_DK), BF16), jax.ShapeDtypeStruct((T, H * RET_DK), BF16),
                   jax.ShapeDtypeStruct((T, H * RET_DV), BF16), jax.ShapeDtypeStruct((T, H * RET_DV), BF16),
                   jax.ShapeDtypeStruct((1, H * RET_DV), F32)],
        scratch_shapes=[pltpu.VMEM((RET_DK, RET_DV), F32)],
        compiler_params=_cparams("parallel", "arbitrary"),
    )(proj, proj, proj, proj, cos, sin, mask, xi, zeta, gn_w, states, dy)


def _shift_down(x, prev8, k):
    if k == 0:
        return x
    y = pltpu.roll(x, k, 0)
    row = lax.broadcasted_iota(jnp.int32, prev8.shape, 0)
    top = jnp.where(row < k, pltpu.roll(prev8, k, 0), y[:8])
    return jnp.concatenate([top, y[8:]], axis=0)


def _shift_up(x, next8, k):
    if k == 0:
        return x
    n = x.shape[0]
    y = pltpu.roll(x, n - k, 0)
    row = lax.broadcasted_iota(jnp.int32, next8.shape, 0)
    bot = jnp.where(row >= 8 - k, pltpu.roll(next8, 8 - k, 0), y[n - 8:])
    return jnp.concatenate([y[:n - 8], bot], axis=0)


def _conv_silu(raw, halo, w, b):
    cv = b
    for tap in range(SSD_CONV_W):
        cv = cv + _shift_down(raw, halo, SSD_CONV_W - 1 - tap) * w[tap:tap + 1, :]
    sg = _sigmoid(cv)
    return cv * sg, cv, sg


def _conv_silu_bwd(d_post, cv, sg, raw, halo, w, carry8):
    dcv = d_post * (sg * (1.0 + cv * (1.0 - sg)))
    d_raw = jnp.zeros_like(raw)
    dws = []
    for tap in range(SSD_CONV_W):
        k = SSD_CONV_W - 1 - tap
        d_raw = d_raw + _shift_up(dcv, carry8, k) * w[tap:tap + 1, :]
        dws.append(_colsum(dcv * _shift_down(raw, halo, k)))
    return d_raw, jnp.concatenate(dws, axis=0), _colsum(dcv), dcv[:8]


def _softplus(x):
    return jnp.maximum(x, 0.0) + jnp.log1p(jnp.exp(-jnp.abs(x)))


def _ssd_common(C, R, dt, dtT, bias, biasT, alog, alogT, E):
    p = dt + bias
    dtv = _softplus(p)
    a = -jnp.exp(alog)
    da = dtv * a
    daT = _softplus(dtT + biasT) * (-jnp.exp(alogT))
    row = lax.broadcasted_iota(jnp.int32, (C, C), 0)
    col = lax.broadcasted_iota(jnp.int32, (C, C), 1)
    tril = row >= col
    trilf = jnp.where(tril, 1.0, 0.0).astype(F32)
    triuf = jnp.where(col >= row, 1.0, 0.0).astype(F32)
    acum = _dot_sel(trilf, da, split="b")
    acumT = _dot_sel(daT, trilf, NT, split="a")
    al = acum[C - 1:C, :]
    ea = jnp.exp(acum)
    dte = jnp.exp(al - acum)
    eal = jnp.exp(al)
    return dict(p=p, dtv=dtv, a=a, da=da, tril=tril, triuf=triuf, acum=acum, acumT=acumT, al=al, ea=ea, dte=dte, eal=eal,
                dtv_e=_dot_sel(dtv, E, split="a", terms=2), ea_e=_dot_sel(ea, E, split="a", terms=2),
                dte_e=_dot_sel(dte, E, split="a", terms=2), eal_e=_dot_sel(eal, E, split="a"))


def _head_decay(q, r, C, R):
    seg = jnp.broadcast_to(q["acum"][:, r:r + 1], (C, C)) - q["acumT"][r:r + 1, :]
    return jnp.exp(jnp.where(q["tril"], seg, -1e30))


def _ssd_group_specs(C, R, NS, rev):
    RP = R * SSD_P
    G = SSD_G
    nz = 1
    hb = C // 8

    def ci(i):
        return NS - 1 - i if rev else i

    def halo_row(i):
        return jnp.maximum(ci(i) * hb - 1, 0)

    off_b = G * RP // SSD_N
    z_spec = pl.BlockSpec((C, RP), lambda g, i: (ci(i), g))
    x_spec = pl.BlockSpec((C, RP), lambda g, i: (ci(i), G + g))
    b_spec = pl.BlockSpec((C, SSD_N), lambda g, i: (ci(i), 2 * off_b + g))
    c_spec = pl.BlockSpec((C, SSD_N), lambda g, i: (ci(i), 2 * off_b + G + g))
    xh_spec = pl.BlockSpec((8, RP), lambda g, i: (halo_row(i), G + g))
    bh_spec = pl.BlockSpec((8, SSD_N), lambda g, i: (halo_row(i), 2 * off_b + g))
    ch_spec = pl.BlockSpec((8, SSD_N), lambda g, i: (halo_row(i), 2 * off_b + G + g))
    dt_spec = pl.BlockSpec((None, C, R), lambda g, i: (g, ci(i), 0))
    dtT_spec = pl.BlockSpec((None, R, C), lambda g, i: (g, 0, ci(i)))
    pr_spec = pl.BlockSpec((None, 1, R), lambda g, i: (g, 0, 0))
    prT_spec = pl.BlockSpec((None, R, 1), lambda g, i: (g, 0, 0))
    cwx_spec = pl.BlockSpec((SSD_CONV_W, RP), lambda g, i: (0, g))
    cwn_spec = pl.BlockSpec((SSD_CONV_W, SSD_N), lambda g, i: (0, g))
    cbx_spec = pl.BlockSpec((1, RP), lambda g, i: (0, g))
    cbn_spec = pl.BlockSpec((1, SSD_N), lambda g, i: (0, g))
    e_spec = pl.BlockSpec((R, RP), lambda g, i: (0, 0))
    st_spec = pl.BlockSpec((None, None, SSD_N, RP), lambda g, i: (g, ci(i), 0, 0))
    return dict(z=z_spec, x=x_spec, b=b_spec, c=c_spec, xh=xh_spec, bh=bh_spec, ch=ch_spec, dt=dt_spec, dtT=dtT_spec,
                pr=pr_spec, prT=prT_spec, cwx=cwx_spec, cwn=cwn_spec, cbx=cbx_spec, cbn=cbn_spec, e=e_spec, st=st_spec)


def _ssd_forward_vals(C, R, refs, first, s_in):
    E = refs["E"]
    halo_on = jnp.where(first, 0.0, 1.0)
    xh, bh, ch = refs["xh"] * halo_on, refs["bh"] * halo_on, refs["ch"] * halo_on
    xs, cvx, sgx = _conv_silu(refs["x"], xh, refs["cwx"], refs["cbx"])
    bm, cvb, sgb = _conv_silu(refs["b"], bh, refs["cwb"], refs["cbb"])
    cm, cvc, sgc = _conv_silu(refs["c"], ch, refs["cwc"], refs["cbc"])
    q = _ssd_common(C, R, refs["dt"], refs["dtT"], refs["bias"], refs["biasT"], refs["alog"], refs["alogT"], E)
    xdt = xs * q["dtv_e"]
    cb = _dot(cm, bm, NT)
    yoff_raw = _dot(cm, s_in)
    xdt_b = xdt.astype(BF16)
    low = lax.broadcasted_iota(jnp.int32, (1, 2 * SSD_P), 1) < SSD_P
    pairs = []
    for j in range(R // 2):
        xp = xdt_b[:, 2 * SSD_P * j:2 * SSD_P * (j + 1)]
        y0 = _dot(cb * _head_decay(q, 2 * j, C, R), xp)
        y1 = _dot(cb * _head_decay(q, 2 * j + 1, C, R), xp)
        pairs.append(jnp.where(low, y0, y1))
    ydiag = jnp.concatenate(pairs, axis=1)
    d_e =_dot_sel(refs["dskip"], E, split="a")
    y = ydiag + yoff_raw * q["ea_e"] + d_e * xs
    xd = xdt * q["dte_e"]
    s_out = s_in * q["eal_e"] + _dot(bm, xd, TN)
    z = refs["z"]
    sgz = _sigmoid(z)
    yz = y * (z * sgz)
    rn = lax.rsqrt(jnp.mean(yz * yz, axis=1, keepdims=True) + RMS_EPS)
    return dict(q=q, xh=xh, bh=bh, ch=ch, xs=xs, cvx=cvx, sgx=sgx, bm=bm, cvb=cvb, sgb=sgb, cm=cm, cvc=cvc, sgc=sgc,
                xdt=xdt, cb=cb, yoff_raw=yoff_raw, d_e=d_e, y=y, xd=xd, s_out=s_out, z=z, sgz=sgz, yz=yz, rn=rn)


_SSD_IN_NAMES = ("z", "x", "b", "c", "xh", "bh", "ch", "dt", "dtT", "bias", "biasT", "alog", "alogT", "dskip",
                 "cwx", "cwb", "cwc", "cbx", "cbb", "cbc", "nw", "E")


def _ssd_inputs(pm, dt_g, dtT_g, prm, sp):
    bias, biasT, alog, alogT, dskip, cwx, cwb, cwc, cbx, cbb, cbc, nw, E = prm
    args = [pm, pm, pm, pm, pm, pm, pm, dt_g, dtT_g, bias, biasT, alog, alogT, dskip, cwx, cwb, cwc, cbx, cbb, cbc, nw, E]
    specs = [sp["z"], sp["x"], sp["b"], sp["c"], sp["xh"], sp["bh"], sp["ch"], sp["dt"], sp["dtT"], sp["pr"], sp["prT"],
             sp["pr"], sp["prT"], sp["pr"], sp["cwx"], sp["cwn"], sp["cwn"], sp["cbx"], sp["cbn"], sp["cbn"], sp["cbx"], sp["e"]]
    return args, specs


def _ssd_fwd(pm, dt_g, dtT_g, prm, *, C, R, name):
    T = pm.shape[0]
    NS = T // C
    RP = R * SSD_P
    G = SSD_G
    sp = _ssd_group_specs(C, R, NS, False)
    args, specs = _ssd_inputs(pm, dt_g, dtT_g, prm, sp)
    nin = len(args)

    def body(*refs):
        ins = {n: r[...] for n, r in zip(_SSD_IN_NAMES, refs[:nin])}
        y_ref, st_ref, S = refs[nin:]
        first = pl.program_id(1) == 0

        @pl.when(first)
        def _():
            S[...] = jnp.zeros_like(S)

        s_in = S[...]
        st_ref[...] = s_in
        f = _ssd_forward_vals(C, R, ins, first, s_in)
        y_ref[...] = (f["yz"] * f["rn"] * ins["nw"]).astype(BF16)
        S[...] = f["s_out"]

    return pl.pallas_call(
        body, name=name, grid=(G, NS), in_specs=specs,
        out_specs=[pl.BlockSpec((C, RP), lambda g, i: (i, g)), sp["st"]],
        out_shape=[jax.ShapeDtypeStruct((T, G * RP), BF16), jax.ShapeDtypeStruct((G, NS, SSD_N, RP), F32)],
        scratch_shapes=[pltpu.VMEM((SSD_N, RP), F32)],
        compiler_params=_cparams("parallel", "arbitrary"),
    )(*args)


def _ssd_bwd(pm, dt_g, dtT_g, prm, states, dout, *, C, R, name):
    T = pm.shape[0]
    NS = T // C
    RP = R * SSD_P
    G = SSD_G
    sp = _ssd_group_specs(C, R, NS, True)
    args, specs = _ssd_inputs(pm, dt_g, dtT_g, prm, sp)
    nin = len(args)
    rows_spec = pl.BlockSpec((C, RP), lambda g, i: (NS - 1 - i, g))
    rown_spec = pl.BlockSpec((C, SSD_N), lambda g, i: (NS - 1 - i, g))
    args = args + [states, dout]
    specs = specs + [sp["st"], rows_spec]

    def body(*refs):
        ins = {n: r[...] for n, r in zip(_SSD_IN_NAMES, refs[:nin])}
        st_ref, dout_ref = refs[nin], refs[nin + 1]
        (dz_ref, dx_ref, db_ref, dc_ref, ddt_ref, dbias_ref, dalog_ref, dd_ref, dcwx_ref, dcwb_ref, dcwc_ref,
         dcbx_ref, dcbb_ref, dcbc_ref, dnw_ref) = refs[nin + 2:nin + 17]
        dS, cx8, cb8, cc8 = refs[nin + 17:]
        acc_refs = (dbias_ref, dalog_ref, dd_ref, dcwx_ref, dcwb_ref, dcwc_ref, dcbx_ref, dcbb_ref, dcbc_ref, dnw_ref)
        step = pl.program_id(1)

        @pl.when(step == 0)
        def _():
            for r_ in acc_refs + (dS, cx8, cb8, cc8):
                r_[...] = jnp.zeros_like(r_)

        first = step == NS - 1
        E = ins["E"]
        s_in = st_ref[...]
        f = _ssd_forward_vals(C, R, ins, first, s_in)
        q = f["q"]
        xs, bm, cm, xdt, cb, y, z, sgz, yz, rn = (f[n] for n in ("xs", "bm", "cm", "xdt", "cb", "y", "z", "sgz", "yz", "rn"))
        nw = ins["nw"]
        dout = dout_ref[...].astype(F32)
        yh = yz * rn
        dnw_ref[...] += _colsum(dout * yh)
        g1 = dout * nw
        dyz = rn * (g1 - yh * jnp.mean(g1 * yh, axis=1, keepdims=True))
        dz_ref[...] = (dyz * y * (sgz * (1.0 + z * (1.0 - sgz)))).astype(BF16)
        dy = dyz * (z * sgz)
        dd_ref[...] += _dot_sel(_colsum(dy * xs), E, NT, split="a")
        dxs = dy * f["d_e"]
        dyo = dy * q["ea_e"]
        dcm = _dot(dyo, s_in, NT)
        ds_acc = _dot(cm, dyo, TN)
        dacum = _dot_sel(dy * f["yoff_raw"], E, NT, split="a", terms=2) * q["ea"]
        dacumT = jnp.zeros((R, C), F32)
        dcb = jnp.zeros((C, C), F32)
        rowR = lax.broadcasted_iota(jnp.int32, (1, R), 1)
        rowRT = lax.broadcasted_iota(jnp.int32, (R, 1), 0)
        dy_b, xdt_b = dy.astype(BF16), xdt.astype(BF16)
        low = lax.broadcasted_iota(jnp.int32, (1, 2 * SSD_P), 1) < SSD_P
        dxdt_pairs = []
        for j in range(R // 2):
            lanes = slice(2 * SSD_P * j, 2 * SSD_P * (j + 1))
            dyp, xp = dy_b[:, lanes], xdt_b[:, lanes]
            halves = []
            for r, mine in ((2 * j, low), (2 * j + 1, jnp.logical_not(low))):
                lr = _head_decay(q, r, C, R)
                w_r = cb * lr
                dw = _dot(jnp.where(mine, dyp, jnp.zeros_like(dyp)), xp, NT)
                halves.append(_dot(w_r, dyp, TN))
                dcb = dcb + dw * lr
                dseg = dw * w_r
                dacum = dacum + jnp.sum(dseg, axis=1, keepdims=True) * jnp.where(rowR == r, 1.0, 0.0)
                dacumT = dacumT - _colsum(dseg) * jnp.where(rowRT == r, 1.0, 0.0)
            dxdt_pairs.append(jnp.where(low, halves[0], halves[1]))
        dxdt = jnp.concatenate(dxdt_pairs, axis=1)
        dsn = dS[...]
        ds_acc = ds_acc + dsn * q["eal_e"]
        d_eal = _dot_sel(_colsum(dsn * s_in), E, NT, split="a")
        dbm = _dot(f["xd"], dsn, NT)
        dxd = _dot(bm, dsn)
        dxdt = dxdt + dxd * q["dte_e"]
        d_dte = _dot_sel(dxd * xdt, E, NT, split="a", terms=2) * q["dte"]
        d_al = _colsum(d_dte) + d_eal * q["eal"]
        dacum = dacum - d_dte
        rowC = lax.broadcasted_iota(jnp.int32, (C, 1), 0)
        dacum = dacum + jnp.where(rowC == C - 1, 1.0, 0.0) * d_al
        dS[...] = ds_acc
        dcm = dcm + _dot(dcb, bm)
        dbm = dbm + _dot(dcb, cm, TN)
        eye = jnp.where(lax.broadcasted_iota(jnp.int32, (C, C), 0) == lax.broadcasted_iota(jnp.int32, (C, C), 1), 1.0, 0.0)
        dacum = dacum + _dot_sel(eye, dacumT, NT, split="b")
        dda = _dot_sel(q["triuf"], dacum, split="b")
        ddtv = dda * q["a"] + _dot_sel(dxdt * xs, E, NT, split="a", terms=2)
        dalog_ref[...] += _colsum(dda * q["dtv"]) * q["a"]
        dxs = dxs + dxdt * q["dtv_e"]
        dp = ddtv * _sigmoid(q["p"])
        ddt_ref[...] = dp
        dbias_ref[...] += _colsum(dp)
        d_raw, d_w, d_b, c8 = _conv_silu_bwd(dxs, f["cvx"], f["sgx"], ins["x"], f["xh"], ins["cwx"], cx8[...])
        dx_ref[...] = d_raw.astype(BF16)
        dcwx_ref[...] += d_w
        dcbx_ref[...] += d_b
        cx8[...] = c8
        d_raw, d_w, d_b, c8 = _conv_silu_bwd(dbm, f["cvb"], f["sgb"], ins["b"], f["bh"], ins["cwb"], cb8[...])
        db_ref[...] = d_raw.astype(BF16)
        dcwb_ref[...] += d_w
        dcbb_ref[...] += d_b
        cb8[...] = c8
        d_raw, d_w, d_b, c8 = _conv_silu_bwd(dcm, f["cvc"], f["sgc"], ins["c"], f["ch"], ins["cwc"], cc8[...])
        dc_ref[...] = d_raw.astype(BF16)
        dcwc_ref[...] += d_w
        dcbc_ref[...] += d_b
        cc8[...] = c8

    out_specs = [rows_spec, rows_spec, rown_spec, rown_spec,
                 pl.BlockSpec((None, C, R), lambda g, i: (g, NS - 1 - i, 0)),
                 sp["pr"], sp["pr"], sp["pr"], sp["cwx"], sp["cwn"], sp["cwn"], sp["cbx"], sp["cbn"], sp["cbn"], sp["cbx"]]
    out_shape = [jax.ShapeDtypeStruct((T, G * RP), BF16), jax.ShapeDtypeStruct((T, G * RP), BF16),
                 jax.ShapeDtypeStruct((T, G * SSD_N), BF16), jax.ShapeDtypeStruct((T, G * SSD_N), BF16),
                 jax.ShapeDtypeStruct((G, T, R), F32),
                 jax.ShapeDtypeStruct((G, 1, R), F32), jax.ShapeDtypeStruct((G, 1, R), F32), jax.ShapeDtypeStruct((G, 1, R), F32),
                 jax.ShapeDtypeStruct((SSD_CONV_W, G * RP), F32), jax.ShapeDtypeStruct((SSD_CONV_W, G * SSD_N), F32),
                 jax.ShapeDtypeStruct((SSD_CONV_W, G * SSD_N), F32),
                 jax.ShapeDtypeStruct((1, G * RP), F32), jax.ShapeDtypeStruct((1, G * SSD_N), F32),
                 jax.ShapeDtypeStruct((1, G * SSD_N), F32), jax.ShapeDtypeStruct((1, G * RP), F32)]
    return pl.pallas_call(
        body, name=name, grid=(G, NS), in_specs=specs, out_specs=out_specs, out_shape=out_shape,
        scratch_shapes=[pltpu.VMEM((SSD_N, RP), F32), pltpu.VMEM((8, RP), F32), pltpu.VMEM((8, SSD_N), F32),
                        pltpu.VMEM((8, SSD_N), F32)],
        compiler_params=_cparams("parallel", "arbitrary"),
    )(*args)


_ANY = pl.BlockSpec(memory_space=pl.ANY)


def _chip_peer(k):
    x, y, c = lax.axis_index("x"), lax.axis_index("y"), lax.axis_index("c")
    return (x ^ (k >> 1), y ^ (k & 1), c)


def _my_chip():
    return 2 * lax.axis_index("x") + lax.axis_index("y")


def _all_gather_chips(shards, halved, *, name):
    n = len(shards)

    def body(*refs):
        ins, outs = refs[:n], refs[n:2 * n]
        send, recv, fsend, frecv, loc = refs[2 * n:]
        s = _my_chip()
        c = lax.axis_index("c")
        sibling = (lax.axis_index("x"), lax.axis_index("y"), 1 - c)
        copies = []
        for a in range(n):
            cp = pltpu.make_async_copy(ins[a], outs[a].at[s], loc.at[a])
            cp.start()
            copies.append(cp)

        def rows(a, core):
            if not halved[a]:
                return slice(None)
            half = shards[a].shape[0] // 2
            return pl.ds(pl.multiple_of(core * half, 16), half)

        def over_ici(a, k, slot, core):
            return pltpu.make_async_remote_copy(
                src_ref=ins[a].at[rows(a, core)], dst_ref=outs[a].at[slot, rows(a, core)],
                send_sem=send.at[3 * a + k - 1], recv_sem=recv.at[3 * a + k - 1],
                device_id=_chip_peer(k), device_id_type=MESH_ID)

        def over_d2d(a, k, core):
            z = outs[a].at[s ^ k, rows(a, core)]
            return pltpu.make_async_remote_copy(
                src_ref=z, dst_ref=z, send_sem=fsend.at[3 * a + k - 1], recv_sem=frecv.at[3 * a + k - 1],
                device_id=sibling, device_id_type=MESH_ID)

        sent = []
        for a in range(n):
            for k in (1, 2, 3):
                cp = over_ici(a, k, s, c)
                cp.start()
                sent.append(cp)
        passed = []
        for a in range(n):
            for k in (1, 2, 3):
                over_ici(a, k, s ^ k, c).wait_recv()
                if halved[a]:
                    cp = over_d2d(a, k, c)
                    cp.start()
                    passed.append(cp)
        for a in range(n):
            if halved[a]:
                for k in (1, 2, 3):
                    over_d2d(a, k, 1 - c).wait_recv()
        for cp in sent + passed:
            cp.wait_send()
        for cp in copies:
            cp.wait()

    for a, h in zip(shards, halved):
        assert not h or a.shape[0] % 32 == 0, a.shape
    return pl.pallas_call(
        body, name=name, in_specs=[_ANY] * n, out_specs=[_ANY] * n,
        out_shape=[jax.ShapeDtypeStruct((4,) + a.shape, a.dtype) for a in shards],
        scratch_shapes=[pltpu.SemaphoreType.DMA((3 * n,))] * 4 + [pltpu.SemaphoreType.DMA((n,))],
        compiler_params=pltpu.CompilerParams(has_side_effects=True),
    )(*shards)


_HBM = pl.BlockSpec(memory_space=pltpu.HBM)
_SEM = pl.BlockSpec(memory_space=pltpu.SEMAPHORE)
_EFFECT = pltpu.SideEffectType.DATAFLOW_SIDE_EFFECTING


def _split_copies(src, land, send, recv, loc, a, scatter):
    s = _my_chip()
    mine = pltpu.make_async_copy(src.at[s] if scatter else src, land.at[s], loc.at[a])
    pairs = []
    for k in (1, 2, 3):
        sems = dict(send_sem=send.at[3 * a + k - 1], recv_sem=recv.at[3 * a + k - 1],
                    device_id=_chip_peer(k), device_id_type=MESH_ID)
        out = pltpu.make_async_remote_copy(src_ref=src.at[s ^ k] if scatter else src, dst_ref=land.at[s], **sems)
        arriving = pltpu.make_async_remote_copy(src_ref=src.at[s ^ k] if scatter else src, dst_ref=land.at[s ^ k], **sems)
        pairs.append((out, arriving))
    return mine, pairs


def _split_start(arrs, *, scatter, after, name):
    n = len(arrs)
    zones = [lax.empty(a.shape if scatter else (4,) + a.shape, a.dtype) for a in arrs]

    def body(*refs):
        srcs, lands = refs[:n], refs[n:2 * n]
        send, recv, loc = refs[2 * n + 1:2 * n + 4]
        token = refs[-1]
        for a in range(n):
            mine, pairs = _split_copies(srcs[a], lands[a], send, recv, loc, a, scatter)
            mine.start()
            for out, _ in pairs:
                out.start()
        token[...] = jnp.zeros_like(token)

    res = pl.pallas_call(
        body, name=name,
        out_shape=(pltpu.SemaphoreType.DMA((3 * n,)), pltpu.SemaphoreType.DMA((3 * n,)), pltpu.SemaphoreType.DMA((n,)),
                   *[pltpu.HBM(a.shape, a.dtype) for a in arrs], *[pltpu.HBM(z.shape, z.dtype) for z in zones],
                   jax.ShapeDtypeStruct((8, 128), F32)),
        in_specs=[_HBM] * (2 * n) + [_ANY],
        out_specs=(_SEM, _SEM, _SEM, *([_HBM] * (2 * n)), pl.BlockSpec(memory_space=pltpu.VMEM)),
        input_output_aliases={i: 3 + i for i in range(2 * n)},
        compiler_params=pltpu.CompilerParams(has_side_effects=_EFFECT),
    )(*[pltpu.with_memory_space_constraint(a, pltpu.HBM) for a in arrs],
      *[pltpu.with_memory_space_constraint(z, pltpu.HBM) for z in zones], after)
    return res[:3], list(res[3:3 + n]), list(res[3 + n:3 + 2 * n]), res[-1]


def _split_wait(sems, src, land, a, *, scatter, after, name):
    def body(src_ref, land_ref, send, recv, loc, after_ref, src_out, land_out):
        mine, pairs = _split_copies(src_ref, land_ref, send, recv, loc, a, scatter)
        mine.wait()
        for out, arriving in pairs:
            out.wait_send()
            arriving.wait_recv()

    return pl.pallas_call(
        body, name=name, out_shape=(pltpu.HBM(src.shape, src.dtype), pltpu.HBM(land.shape, land.dtype)),
        in_specs=[_HBM, _HBM, _SEM, _SEM, _SEM, _ANY], out_specs=(_HBM, _HBM), input_output_aliases={0: 0, 1: 1},
        compiler_params=pltpu.CompilerParams(has_side_effects=_EFFECT),
    )(src, land, *sems, after)[1]


def _sibling_copies(srcs, lands, send, recv):
    sib = (lax.axis_index("x"), lax.axis_index("y"), 1 - lax.axis_index("c"))
    return [pltpu.make_async_remote_copy(src_ref=srcs[a], dst_ref=lands[a], send_sem=send.at[a], recv_sem=recv.at[a],
                                         device_id=sib, device_id_type=MESH_ID) for a in range(len(srcs))]


def _swap_start(arrs, *, after, name):
    n = len(arrs)
    zones = [lax.empty(a.shape, a.dtype) for a in arrs]

    def body(*refs):
        for cp in _sibling_copies(refs[:n], refs[n:2 * n], refs[2 * n + 1], refs[2 * n + 2]):
            cp.start()

    res = pl.pallas_call(
        body, name=name,
        out_shape=(pltpu.SemaphoreType.DMA((n,)), pltpu.SemaphoreType.DMA((n,)),
                   *[pltpu.HBM(a.shape, a.dtype) for a in arrs], *[pltpu.HBM(z.shape, z.dtype) for z in zones]),
        in_specs=[_HBM] * (2 * n) + [_ANY], out_specs=(_SEM, _SEM, *([_HBM] * (2 * n))),
        input_output_aliases={i: 2 + i for i in range(2 * n)},
        compiler_params=pltpu.CompilerParams(has_side_effects=_EFFECT),
    )(*[pltpu.with_memory_space_constraint(a, pltpu.HBM) for a in arrs],
      *[pltpu.with_memory_space_constraint(z, pltpu.HBM) for z in zones], after)
    return res[:2], list(res[2:2 + n]), list(res[2 + n:])


def _swap_wait(sems, srcs, lands, *, after, name):
    n = len(srcs)

    def body(*refs):
        for cp in _sibling_copies(refs[:n], refs[n:2 * n], refs[2 * n], refs[2 * n + 1]):
            cp.wait_send()
            cp.wait_recv()

    res = pl.pallas_call(
        body, name=name, out_shape=tuple(pltpu.HBM(a.shape, a.dtype) for a in list(srcs) + list(lands)),
        in_specs=[_HBM] * (2 * n) + [_SEM, _SEM, _ANY], out_specs=tuple([_HBM] * (2 * n)),
        input_output_aliases={i: i for i in range(2 * n)},
        compiler_params=pltpu.CompilerParams(has_side_effects=_EFFECT),
    )(*srcs, *lands, *sems, after)
    return list(res[n:])


def _all_gather_devices(v, *, name):
    r = v.shape[0]

    def body(v_ref, out_ref, send, recv):
        x, y, c = lax.axis_index("x"), lax.axis_index("y"), lax.axis_index("c")
        me = 4 * x + 2 * y + c
        out_ref[me] = v_ref[...]
        cps = []
        for k in range(1, 8):
            peer = (x ^ (k >> 2), y ^ ((k >> 1) & 1), c ^ (k & 1))
            cp = pltpu.make_async_remote_copy(src_ref=v_ref, dst_ref=out_ref.at[me], send_sem=send.at[k - 1],
                                              recv_sem=recv.at[k - 1], device_id=peer, device_id_type=MESH_ID)
            cp.start()
            cps.append(cp)
        for k, cp in enumerate(cps, start=1):
            cp.wait_send()
            pltpu.make_async_remote_copy(src_ref=v_ref, dst_ref=out_ref.at[me ^ k], send_sem=send.at[k - 1],
                                         recv_sem=recv.at[k - 1], device_id=(x, y, c), device_id_type=MESH_ID).wait_recv()

    vm = pl.BlockSpec(memory_space=pltpu.VMEM)
    return pl.pallas_call(
        body, name=name, in_specs=[vm], out_specs=vm, out_shape=jax.ShapeDtypeStruct((8, r, 128), F32),
        scratch_shapes=[pltpu.SemaphoreType.DMA((7,)), pltpu.SemaphoreType.DMA((7,))],
        compiler_params=pltpu.CompilerParams(has_side_effects=True),
    )(v)


def _row_tile(r, target):
    best = None
    for t in range(16, min(target, r) + 1, 16):
        if r % t == 0:
            best = t
    return best or r


def _sum_slots(buf, *, name, tr=384, layer=None, stack=None):
    S, r, c = buf.shape
    tr = _row_tile(r, tr)

    def body(*refs):
        b_ref, o_ref = refs[0], refs[-1]
        acc = b_ref[0].astype(F32)
        for j in range(1, S):
            acc = acc + b_ref[j].astype(F32)
        o_ref[...] = acc

    in_specs = [pl.BlockSpec((S, tr, c), lambda i: (0, i, 0))]
    args, alias = [buf], {}
    if layer is None:
        out_spec, out_shape = pl.BlockSpec((tr, c), lambda i: (i, 0)), jax.ShapeDtypeStruct((r, c), F32)
    else:
        out_spec = pl.BlockSpec((None, tr, c), lambda i: (layer, i, 0))
        out_shape = jax.ShapeDtypeStruct((2, r, c), F32)
        if stack is not None:
            in_specs.append(_ANY)
            args.append(stack)
            alias = {1: 0}
    return pl.pallas_call(
        body, name=name, grid=(r // tr,), in_specs=in_specs, out_specs=out_spec, out_shape=out_shape,
        input_output_aliases=alias, compiler_params=_cparams("parallel"),
    )(*args)


def _add2(a, b, *, name, tr=384):
    r, c = a.shape
    tr = _row_tile(r, tr)

    def body(a_ref, b_ref, o_ref):
        o_ref[...] = a_ref[...] + b_ref[...]

    spec = pl.BlockSpec((tr, c), lambda i: (i, 0))
    return pl.pallas_call(body, name=name, grid=(r // tr,), in_specs=[spec, spec], out_specs=spec,
                          out_shape=jax.ShapeDtypeStruct((r, c), F32), compiler_params=_cparams("parallel"))(a, b)


ADAMW_BLOCK_ELEMS = 1 << 18


def _adamw(w, gs, m, v, *, name, tr=256):
    r, c = w.shape
    tr = _row_tile(r, min(tr, max(16, ADAMW_BLOCK_ELEMS // c)))
    bc1 = 1.0 - ADAM_B1 ** ADAM_STEP
    bc2 = 1.0 - ADAM_B2 ** ADAM_STEP
    ng = len(gs)

    def body(*refs):
        w_ref, m_ref, v_ref = refs[0], refs[1 + ng], refs[2 + ng]
        g_ref, d_ref, mo_ref, vo_ref = refs[3 + ng:]
        gg = refs[1][...] if ng == 1 else refs[1][...] + refs[2][...]
        mn = ADAM_B1 * m_ref[...] + (1.0 - ADAM_B1) * gg
        vn = ADAM_B2 * v_ref[...] + (1.0 - ADAM_B2) * (gg * gg)
        g_ref[...] = gg
        mo_ref[...] = mn
        vo_ref[...] = vn
        d_ref[...] = -ADAM_LR * ((mn / bc1) / (jnp.sqrt(vn / bc2) + ADAM_EPS) + ADAM_WD * w_ref[...])

    spec = pl.BlockSpec((tr, c), lambda i: (i, 0))
    return pl.pallas_call(body, name=name, grid=(r // tr,), in_specs=[spec] * (3 + ng), out_specs=[spec] * 4,
                          out_shape=[jax.ShapeDtypeStruct((r, c), F32)] * 4,
                          compiler_params=_cparams("parallel"))(w, *gs, m, v)


def _pack(vecs, rows):
    flat = jnp.concatenate([v.reshape(-1).astype(F32) for v in vecs])
    return jnp.pad(flat, (0, rows * 128 - flat.shape[0])).reshape(rows, 128)


def _unpack(packed, shapes):
    flat = packed.reshape(-1)
    out, off = [], 0
    for s in shapes:
        n = math.prod(s)
        out.append(flat[off:off + n].reshape(s))
        off += n
    return out


def _pack_rows(shapes):
    n = sum(math.prod(s) for s in shapes)
    return -(-n // 1024) * 8


def kernel(x, norm_mix_pre, norm_mix_post, norm_ffn_pre, norm_ffn_post, ret_w_in, ret_gn_w, ret_w_out, ssd_w_in, ssd_conv_w, ssd_conv_b, ssd_dt_bias, ssd_a_log, ssd_d, ssd_norm_w, ssd_w_out, mlp_w_up, mlp_w_down, loss_target, m_norm_mix_pre, m_norm_mix_post, m_norm_ffn_pre, m_norm_ffn_post, m_ret_w_in, m_ret_gn_w, m_ret_w_out, m_ssd_w_in, m_ssd_conv_w, m_ssd_conv_b, m_ssd_dt_bias, m_ssd_a_log, m_ssd_d, m_ssd_norm_w, m_ssd_w_out, m_mlp_w_up, m_mlp_w_down, v_norm_mix_pre, v_norm_mix_post, v_norm_ffn_pre, v_norm_ffn_post, v_ret_w_in, v_ret_gn_w, v_ret_w_out, v_ssd_w_in, v_ssd_conv_w, v_ssd_conv_b, v_ssd_dt_bias, v_ssd_a_log, v_ssd_d, v_ssd_norm_w, v_ssd_w_out, v_mlp_w_up, v_mlp_w_down):
    T, D = x.shape[1], x.shape[2]
    H = D // RET_DK
    d_inner = 2 * D
    R = d_inner // SSD_P // SSD_G
    RP = R * SSD_P
    n_heads = SSD_G * R
    conv_dim = d_inner + 2 * SSD_G * SSD_N
    n_main = d_inner + conv_dim
    C = min(256, T)
    chip = _my_chip()
    xs, tgt = x[0], loss_target[0]

    conv_sh = ssd_conv_w.shape[2]
    small_shapes = [(SSD_CONV_W, conv_sh), (conv_sh,), (ssd_norm_w.shape[1],)]
    small_rows = _pack_rows(small_shapes)
    shards = [ret_w_in[0].T.astype(BF16), ret_w_out[0].astype(BF16), ssd_w_in[0].T.astype(BF16), ssd_w_out[0].astype(BF16),
              mlp_w_up[0].T.astype(BF16), mlp_w_up[1].T.astype(BF16), mlp_w_down[0].astype(BF16), mlp_w_down[1].astype(BF16)]
    (ret_in_g, small_g) = _all_gather_chips([shards[0], _pack([ssd_conv_w[0], ssd_conv_b[0], ssd_norm_w[0]], small_rows)],
                                            [True, False], name="gather_first")

    def full(g):
        return g.reshape(4 * g.shape[1], g.shape[2])

    def start_gather(idx, after, name):
        sems, srcs, zones, tok = _split_start([shards[i] for i in idx], scatter=False, after=after, name=name)
        return {i: (sems, srcs[n], zones[n], n) for n, i in enumerate(idx)}, tok

    def arrived(stage, i, after, name):
        sems, src, zone, n = stage[i]
        return full(_split_wait(sems, src, zone, n, scatter=False, after=after, name=name))

    ret_in_t = full(ret_in_g)
    sm = [_unpack(small_g[j], small_shapes) for j in range(4)]
    conv_w = jnp.concatenate([sm[j][0] for j in range(4)], axis=1)
    conv_b = jnp.concatenate([sm[j][1] for j in range(4)])[None, :]
    norm_w = jnp.concatenate([sm[j][2] for j in range(4)])[None, :]

    gb = SSD_G * SSD_N
    ssd_prm = (ssd_dt_bias.reshape(SSD_G, 1, R), ssd_dt_bias.reshape(SSD_G, R, 1),
               ssd_a_log.reshape(SSD_G, 1, R), ssd_a_log.reshape(SSD_G, R, 1), ssd_d.reshape(SSD_G, 1, R),
               conv_w[:, :d_inner], conv_w[:, d_inner:d_inner + gb], conv_w[:, d_inner + gb:],
               conv_b[:, :d_inner], conv_b[:, d_inner:d_inner + gb], conv_b[:, d_inner + gb:],
               norm_w, jnp.asarray(np.kron(np.eye(R), np.ones((1, SSD_P))), F32))
    ret_consts = _ret_consts(T, C, H)

    u0 = _rms_pre(xs, norm_mix_pre[0:1], name="pre0")
    stage1, tok = start_gather((1, 4), ret_in_g, "gather_start1")
    proj = _matmul(u0, ret_in_t, "nt", out_dtype=F32, name="ret_in", after=tok)
    stage2, tok = start_gather((6, 2), proj, "gather_start2")
    y_ret, st_ret = _ret_fwd(proj, ret_gn_w, ret_consts, C=C, name="ret_fwd")
    ret_out = arrived(stage1, 1, y_ret, "gather_wait_ret_out")
    m0 = _matmul(y_ret, ret_out, "nn", out_dtype=F32, name="ret_out", after=tok)
    h1, u1 = _rms_post_pre(xs, m0, norm_mix_post[0:1], norm_ffn_pre[0:1], name="post_pre1")
    up_t0 = arrived(stage1, 4, u1, "gather_wait_up0")
    a0, hh0 = _matmul(u1, up_t0, "nt", out_dtype=BF16, name="mlp_up0", epi="relu2")
    stage3, tok = start_gather((3, 5, 7), hh0, "gather_start3")
    down0 = arrived(stage2, 6, hh0, "gather_wait_down0")
    f0 = _matmul(hh0, down0, "nn", out_dtype=F32, name="mlp_down0", after=tok)
    h2, u2 = _rms_post_pre(h1, f0, norm_ffn_post[0:1], norm_mix_pre[1:2], name="post_pre2")
    ssd_in_t = arrived(stage2, 2, u2, "gather_wait_ssd_in")
    ssd_main_t, ssd_dt_t = ssd_in_t[:n_main], ssd_in_t[n_main:]
    pm = _matmul(u2, ssd_main_t, "nt", out_dtype=F32, name="ssd_in")
    pdt = _matmul(u2, ssd_dt_t, "nt", out_dtype=F32, name="ssd_in_dt")
    dt_g = pdt.reshape(T, SSD_G, R).transpose(1, 0, 2)
    dtT_g = pdt.reshape(T, SSD_G, R).transpose(1, 2, 0)
    y_ssd, st_ssd = _ssd_fwd(pm, dt_g, dtT_g, ssd_prm, C=C, R=R, name="ssd_fwd")
    ssd_out = arrived(stage3, 3, y_ssd, "gather_wait_ssd_out")
    m1 = _matmul(y_ssd, ssd_out, "nn", out_dtype=F32, name="ssd_out")
    h3, u3 = _rms_post_pre(h2, m1, norm_mix_post[1:2], norm_ffn_pre[1:2], name="post_pre3")
    up_t1 = arrived(stage3, 5, u3, "gather_wait_up1")
    a1, hh1 = _matmul(u3, up_t1, "nt", out_dtype=BF16, name="mlp_up1", epi="relu2")
    down1 = arrived(stage3, 7, hh1, "gather_wait_down1")
    f1 = _matmul(hh1, down1, "nn", out_dtype=F32, name="mlp_down1")
    up_t, down = (up_t0, up_t1), (down0, down1)
    dh4, sq = _rms_post_loss(h3, f1, norm_ffn_post[1:2], tgt, name="post_loss")
    loss = lax.psum(sq[0, 0], MESH_AXES) * (0.5 / D)

    in_flight = []

    def send_grad(g, name):
        part = g if g.ndim == 3 else g.reshape(4, g.shape[0] // 4, g.shape[1])
        sems, srcs, zones, tok = _split_start([part], scatter=True, after=part, name=f"scatter_start_{name}")
        in_flight.append((name, sems, srcs[0], zones[0]))
        return tok

    def mlp_bwd(i, dh_out, h_in, u, a, hh, f):
        df, d_post = _rms_post_bwd(f, norm_ffn_post[i:i + 1], dh_out, name=f"post_bwd_ffn{i}")
        tok = send_grad(_matmul(hh, df, "tn", out_dtype=BF16, name=f"mlp_down_wg{i}"), f"down{i}")
        da = _matmul(df, down[i], "nt", out_dtype=BF16, name=f"mlp_down_dg{i}", epi="drelu2", extra=a, after=tok)
        tok = send_grad(_matmul(u, da, "tn", out_dtype=BF16, name=f"mlp_up_wg{i}", col_parts=4), f"up{i}")
        du = _matmul(da, up_t[i], "nn", out_dtype=F32, name=f"mlp_up_dg{i}", after=tok)
        dh, d_pre = _rms_pre_bwd(h_in, norm_ffn_pre[i:i + 1], du, dh_out, name=f"pre_bwd_ffn{i}")
        return dh, d_pre, d_post

    dh3, d_nfp1, d_nfpost1 = mlp_bwd(1, dh4, h3, u3, a1, hh1, f1)
    dm1, d_nmpost1 = _rms_post_bwd(m1, norm_mix_post[1:2], dh3, name="post_bwd_mix1")
    tok = send_grad(_matmul(y_ssd, dm1, "tn", out_dtype=BF16, name="ssd_out_wg"), "ssd_out")
    dy_ssd = _matmul(dm1, ssd_out, "nt", out_dtype=F32, name="ssd_out_dg", after=tok)
    (dz, dxr, dbr, dcr, ddt_g, d_bias, d_alog, d_dskip, dcwx, dcwb, dcwc, dcbx, dcbb, dcbc, d_normw) = _ssd_bwd(
        pm, dt_g, dtT_g, ssd_prm, st_ssd, dy_ssd, C=C, R=R, name="ssd_bwd")
    dpm = jnp.concatenate([dz, dxr, dbr, dcr], axis=1)
    dpdt = ddt_g.transpose(1, 0, 2).reshape(T, n_heads).astype(BF16)
    tok = send_grad(jnp.concatenate([_matmul(dpm, u2, "tn", out_dtype=BF16, name="ssd_in_wg"),
                                     _matmul(dpdt, u2, "tn", out_dtype=BF16, name="ssd_in_dt_wg")], axis=0), "ssd_in")
    du2 = _matmul(dpm, ssd_main_t, "nn", out_dtype=F32, name="ssd_in_dg", after=tok)
    du2 = _matmul(dpdt, ssd_dt_t, "nn", out_dtype=F32, name="ssd_in_dt_dg", epi="add", extra=du2)
    dh2, d_nmp1 = _rms_pre_bwd(h2, norm_mix_pre[1:2], du2, dh3, name="pre_bwd_mix1")
    dh1, d_nfp0, d_nfpost0 = mlp_bwd(0, dh2, h1, u1, a0, hh0, f0)
    dm0, d_nmpost0 = _rms_post_bwd(m0, norm_mix_post[0:1], dh1, name="post_bwd_mix0")
    tok = send_grad(_matmul(y_ret, dm0, "tn", out_dtype=BF16, name="ret_out_wg"), "ret_out")
    dy_ret = _matmul(dm0, ret_out, "nt", out_dtype=F32, name="ret_out_dg", after=tok)
    dq, dk, dv, dg, d_gn = _ret_bwd(proj, ret_gn_w, ret_consts, st_ret, dy_ret, C=C, name="ret_bwd")
    dproj = jnp.concatenate([dq, dk, dv, dg], axis=1)
    tok = send_grad(_matmul(u0, dproj, "tn", out_dtype=BF16, name="ret_in_wg", col_parts=4), "ret_in")
    du0 = _matmul(dproj, ret_in_t, "nn", out_dtype=F32, name="ret_in_dg", after=tok)
    grad_x, d_nmp0 = _rms_pre_bwd(xs, norm_mix_pre[0:1], du0, dh1, name="pre_bwd_mix0")

    landed = {nm: _split_wait(sems, src, zone, 0, scatter=True, after=grad_x, name=f"scatter_wait_{nm}")
              for nm, sems, src, zone in in_flight}
    part_sum = {}

    def two_d(t):
        return t.reshape(-1, t.shape[-1])

    def sum_chips(nm):
        if nm[-1] in "01" and nm[:-1] in ("up", "down"):
            fam, layer = nm[:-1], int(nm[-1])
            part_sum[fam] = _sum_slots(landed[nm], name=f"sum_chips_{nm}", layer=layer, stack=part_sum.get(fam))
        else:
            part_sum[nm] = _sum_slots(landed[nm], name=f"sum_chips_{nm}")

    swaps = []
    for names_, fams in ((("down1", "up1", "ssd_out", "ssd_in", "down0"), ("ssd_out", "ssd_in", "down")),
                         (("up0", "ret_out", "ret_in"), ("up", "ret_out", "ret_in"))):
        for nm in names_:
            sum_chips(nm)
        mine = [two_d(part_sum[f]) for f in fams]
        swaps.append((fams, _swap_start(mine, after=mine[-1], name=f"swap_start{len(swaps)}")))

    def upd(w, gs, m, v, name):
        shp = w.shape
        w2, m2, v2 = (two_d(t) for t in (w, m, v))
        return tuple(t.reshape(shp) for t in _adamw(w2, [g.reshape(w2.shape) for g in gs], m2, v2, name=name))

    def upd_t(w, gs, m, v, name):
        return tuple(t.T[None] for t in _adamw(w[0].T, gs, m[0].T, v[0].T, name=name))

    todo = {"ret_in": (upd, ret_w_in, m_ret_w_in, v_ret_w_in, "ret_w_in"),
            "ret_out": (upd, ret_w_out, m_ret_w_out, v_ret_w_out, "ret_w_out"),
            "ssd_in": (upd_t, ssd_w_in, m_ssd_w_in, v_ssd_w_in, "ssd_w_in"),
            "ssd_out": (upd, ssd_w_out, m_ssd_w_out, v_ssd_w_out, "ssd_w_out"),
            "up": (upd, mlp_w_up, m_mlp_w_up, v_mlp_w_up, "mlp_w_up"),
            "down": (upd, mlp_w_down, m_mlp_w_down, v_mlp_w_down, "mlp_w_down")}
    res = {}
    prev = grad_x
    for n_, (fams, (sems, srcs, zones)) in enumerate(swaps):
        theirs = _swap_wait(sems, srcs, zones, after=prev, name=f"swap_wait{n_}")
        for f, mine, other in zip(fams, srcs, theirs):
            fn, w, m, v, out_name = todo[f]
            res[out_name] = fn(w, [mine, other], m, v, f"adamw_{f}")
            prev = res[out_name][0]

    d_conv_w = jnp.concatenate([dcwx, dcwb, dcwc], axis=1)
    d_conv_b = jnp.concatenate([dcbx, dcbb, dcbc], axis=1)
    small_grads = [jnp.concatenate([d_nmp0, d_nmp1]), jnp.concatenate([d_nmpost0, d_nmpost1]),
                   jnp.concatenate([d_nfp0, d_nfp1]), jnp.concatenate([d_nfpost0, d_nfpost1]),
                   d_gn, d_bias.reshape(1, n_heads), d_alog.reshape(1, n_heads), d_dskip.reshape(1, n_heads),
                   d_conv_w, d_conv_b, d_normw]
    sg_shapes = [g.shape for g in small_grads]
    sg_rows = _pack_rows(sg_shapes)
    everyone = _all_gather_devices(_pack(small_grads, sg_rows), name="gather_small_grads")
    sg = _unpack(_sum_slots(everyone, name="sum_small_grads", tr=sg_rows), sg_shapes)
    (g_nmp, g_nmpost, g_nfp, g_nfpost, g_gn, g_bias, g_alog, g_dskip, g_cw_full, g_cb_full, g_nw_full) = sg
    g_cw = lax.dynamic_slice_in_dim(g_cw_full, chip * conv_sh, conv_sh, axis=1)[None]
    g_cb = lax.dynamic_slice_in_dim(g_cb_full, chip * conv_sh, conv_sh, axis=1)
    nw_sh = ssd_norm_w.shape[1]
    g_nw = lax.dynamic_slice_in_dim(g_nw_full, chip * nw_sh, nw_sh, axis=1)
    small = [("norm_mix_pre", norm_mix_pre, g_nmp, m_norm_mix_pre, v_norm_mix_pre),
             ("norm_mix_post", norm_mix_post, g_nmpost, m_norm_mix_post, v_norm_mix_post),
             ("norm_ffn_pre", norm_ffn_pre, g_nfp, m_norm_ffn_pre, v_norm_ffn_pre),
             ("norm_ffn_post", norm_ffn_post, g_nfpost, m_norm_ffn_post, v_norm_ffn_post),
             ("ret_gn_w", ret_gn_w, g_gn, m_ret_gn_w, v_ret_gn_w),
             ("ssd_conv_w", ssd_conv_w, g_cw, m_ssd_conv_w, v_ssd_conv_w),
             ("ssd_conv_b", ssd_conv_b, g_cb, m_ssd_conv_b, v_ssd_conv_b),
             ("ssd_dt_bias", ssd_dt_bias, g_bias, m_ssd_dt_bias, v_ssd_dt_bias),
             ("ssd_a_log", ssd_a_log, g_alog, m_ssd_a_log, v_ssd_a_log),
             ("ssd_d", ssd_d, g_dskip, m_ssd_d, v_ssd_d),
             ("ssd_norm_w", ssd_norm_w, g_nw, m_ssd_norm_w, v_ssd_norm_w)]
    sw_shapes = [w.shape for _, w, _, _, _ in small]
    sw_rows = _pack_rows(sw_shapes)
    packs = [_pack([t[j] for t in small], sw_rows) for j in (1, 2, 3, 4)]
    _, d_p, m_p, v_p = _adamw(packs[0], [packs[1]], packs[2], packs[3], name="adamw_small", tr=sw_rows)
    d_s, m_s, v_s = _unpack(d_p, sw_shapes), _unpack(m_p, sw_shapes), _unpack(v_p, sw_shapes)
    for j, (nm, w, g, _, _) in enumerate(small):
        res[nm] = (g.reshape(w.shape), d_s[j], m_s[j], v_s[j])

    order = ["norm_mix_pre", "norm_mix_post", "norm_ffn_pre", "norm_ffn_post", "ret_w_in", "ret_gn_w", "ret_w_out",
             "ssd_w_in", "ssd_conv_w", "ssd_conv_b", "ssd_dt_bias", "ssd_a_log", "ssd_d", "ssd_norm_w", "ssd_w_out",
             "mlp_w_up", "mlp_w_down"]
    return (loss, grad_x[None], *[res[n][0] for n in order], *[res[n][1] for n in order],
            *[res[n][2] for n in order], *[res[n][3] for n in order])
```

```python
import functools
import math

import numpy as np
import jax
import jax.numpy as jnp
from jax import lax
from jax.experimental import pallas as pl
from jax.experimental.pallas import tpu as pltpu

F32 = jnp.float32
BF16 = jnp.bfloat16
HI = lax.Precision.HIGHEST
VMEM_LIMIT_BYTES = 56 * 1024 * 1024
MESH_AXES = ("x", "y", "c")
MESH_ID = pl.DeviceIdType.MESH

RMS_EPS = 1e-6
GN_EPS = 1e-5
RET_DK = 256
RET_DV = 512
ROPE_BASE = 10000.0
REF_CHUNK = 64
SSD_P = 64
SSD_N = 128
SSD_G = 8
SSD_CONV_W = 4
ADAM_LR, ADAM_B1, ADAM_B2, ADAM_EPS, ADAM_WD, ADAM_STEP = 0.001, 0.9, 0.999, 1e-08, 0.01, 10

NN = (((1,), (0,)), ((), ()))
NT = (((1,), (1,)), ((), ()))
TN = (((0,), (0,)), ((), ()))


def _cparams(*sem):
    return pltpu.CompilerParams(dimension_semantics=sem, vmem_limit_bytes=VMEM_LIMIT_BYTES)


def _dot(a, b, dims=NN):
    return lax.dot_general(a.astype(BF16), b.astype(BF16), dims, preferred_element_type=F32)


def _split_bf16(x, terms):
    parts, rest = [], x
    for _ in range(terms):
        p = rest.astype(BF16)
        parts.append(p)
        rest = rest - p.astype(F32)
    return parts


def _dot_sel(a, b, dims=NN, *, split, terms=3):
    if split == "a":
        sel = b.astype(BF16)
        return sum(lax.dot_general(p, sel, dims, preferred_element_type=F32) for p in _split_bf16(a, terms))
    sel = a.astype(BF16)
    return sum(lax.dot_general(sel, p, dims, preferred_element_type=F32) for p in _split_bf16(b, terms))


def _sigmoid(x):
    return 1.0 / (1.0 + jnp.exp(-x))


def _colsum(x):
    return jnp.sum(x, axis=0, keepdims=True)


MM_TILE = 1024
MM_FULL_K = 2048


def _mm_tiles(M, N, K):
    tm = min(M, MM_TILE)
    if K <= MM_FULL_K:
        return tm, min(N, MM_TILE), K
    return tm, min(N, 2 * MM_TILE), MM_TILE


def _matmul(a, b, mode, *, out_dtype, name, epi=None, extra=None, after=None, col_parts=None):
    if mode == "nn":
        (M, K), (K2, N) = a.shape, b.shape
    elif mode == "nt":
        (M, K), (N, K2) = a.shape, b.shape
    else:
        (K, M), (K2, N) = a.shape, b.shape
    assert K == K2, (a.shape, b.shape, mode)
    tm, tn, tk = _mm_tiles(M, N, K)
    if col_parts:
        tm, tn = min(M, 2 * MM_TILE), min(tn, MM_TILE)
        while (N // col_parts) % tn:
            tn //= 2
    assert M % tm == 0 and N % tn == 0 and K % tk == 0, (M, N, K, tm, tn, tk)
    nk = K // tk
    if mode == "tn":
        a_spec = pl.BlockSpec((tk, tm), lambda i, j, k: (k, i))
    else:
        a_spec = pl.BlockSpec((tm, tk), lambda i, j, k: (i, k))
    if mode == "nt":
        b_spec = pl.BlockSpec((tn, tk), lambda i, j, k: (j, k))
    else:
        b_spec = pl.BlockSpec((tk, tn), lambda i, j, k: (k, j))
    dims = {"nn": NN, "nt": NT, "tn": TN}[mode]
    o_spec = pl.BlockSpec((tm, tn), lambda i, j, k: (i, j))
    out_dims = (M, N)
    if col_parts:
        per = N // col_parts // tn
        o_spec = pl.BlockSpec((None, tm, tn), lambda i, j, k: (j // per, i, j % per))
        out_dims = (col_parts, M, N // col_parts)
    has_extra = epi in ("drelu2", "add")
    n_out = 2 if epi == "relu2" else 1
    n_in = 2 + int(has_extra) + int(after is not None)

    def body(*refs):
        a_ref, b_ref = refs[0], refs[1]
        e_ref = refs[2] if has_extra else None
        outs = refs[n_in:n_in + n_out]

        def finish(r):
            if epi is None:
                outs[0][...] = r.astype(outs[0].dtype)
            elif epi == "relu2":
                outs[0][...] = r.astype(outs[0].dtype)
                h = jnp.maximum(r, 0.0)
                outs[1][...] = (h * h).astype(outs[1].dtype)
            elif epi == "drelu2":
                av = jnp.maximum(e_ref[...].astype(F32), 0.0)
                outs[0][...] = (r * (2.0 * av)).astype(outs[0].dtype)
            else:
                outs[0][...] = (r + e_ref[...].astype(F32)).astype(outs[0].dtype)

        if nk == 1:
            finish(_dot(a_ref[...], b_ref[...], dims))
            return
        acc = refs[-1]
        k = pl.program_id(2)

        @pl.when(k == 0)
        def _():
            acc[...] = jnp.zeros_like(acc)

        acc[...] += _dot(a_ref[...], b_ref[...], dims)

        @pl.when(k == nk - 1)
        def _():
            finish(acc[...])

    in_specs = [a_spec, b_spec] + ([o_spec] if has_extra else [])
    args = [a, b] + ([extra] if has_extra else [])
    if after is not None:
        in_specs.append(pl.BlockSpec(after.shape, lambda i, j, k: (0, 0)))
        args.append(after)
    out_shape = [jax.ShapeDtypeStruct(out_dims, out_dtype)] * n_out
    res = pl.pallas_call(
        body, name=name, grid=(M // tm, N // tn, nk), in_specs=in_specs, out_specs=[o_spec] * n_out,
        out_shape=out_shape, scratch_shapes=[pltpu.VMEM((tm, tn), F32)] if nk > 1 else [],
        compiler_params=_cparams("parallel", "parallel", "arbitrary"),
    )(*args)
    return res if n_out == 2 else res[0]


def _rstd(x):
    return lax.rsqrt(jnp.mean(x * x, axis=-1, keepdims=True) + RMS_EPS)


def _row_call(body, ins, outs_shape, *, name, rows, tr, acc_outs=()):
    tr = min(tr, rows)
    assert rows % tr == 0
    in_specs = []
    for arr, blocked in ins:
        if blocked:
            in_specs.append(pl.BlockSpec((tr, arr.shape[1]), lambda i: (i, 0)))
        else:
            in_specs.append(pl.BlockSpec(arr.shape, lambda i: (0, 0)))
    out_specs = []
    for n, s in enumerate(outs_shape):
        if n in acc_outs:
            out_specs.append(pl.BlockSpec(s.shape, lambda i: (0, 0)))
        else:
            out_specs.append(pl.BlockSpec((tr, s.shape[1]), lambda i: (i, 0)))
    return pl.pallas_call(
        body, name=name, grid=(rows // tr,), in_specs=in_specs, out_specs=out_specs, out_shape=outs_shape,
        compiler_params=_cparams("arbitrary" if acc_outs else "parallel"),
    )(*[a for a, _ in ins])


def _rms_pre(h, w, *, name):
    T, D = h.shape

    def body(h_ref, w_ref, u_ref):
        x = h_ref[...]
        u_ref[...] = (x * _rstd(x) * w_ref[...]).astype(BF16)

    return _row_call(body, [(h, True), (w, False)], [jax.ShapeDtypeStruct((T, D), BF16)], name=name, rows=T, tr=256)[0]


def _rms_post_pre(h, m, w_post, w_pre, *, name):
    T, D = h.shape

    def body(h_ref, m_ref, wp_ref, wn_ref, hn_ref, u_ref):
        mm = m_ref[...]
        hn = h_ref[...] + mm * _rstd(mm) * wp_ref[...]
        hn_ref[...] = hn
        u_ref[...] = (hn * _rstd(hn) * wn_ref[...]).astype(BF16)

    return _row_call(body, [(h, True), (m, True), (w_post, False), (w_pre, False)],
                     [jax.ShapeDtypeStruct((T, D), F32), jax.ShapeDtypeStruct((T, D), BF16)], name=name, rows=T, tr=256)


def _rms_post_loss(h, m, w_post, tgt, *, name):
    T, D = h.shape

    def body(h_ref, m_ref, wp_ref, t_ref, dh_ref, loss_ref):
        @pl.when(pl.program_id(0) == 0)
        def _():
            loss_ref[...] = jnp.zeros_like(loss_ref)

        mm = m_ref[...]
        err = h_ref[...] + mm * _rstd(mm) * wp_ref[...] - t_ref[...]
        dh_ref[...] = err * (1.0 / D)
        loss_ref[...] += _colsum(jnp.sum(err * err, axis=1, keepdims=True))

    return _row_call(body, [(h, True), (m, True), (w_post, False), (tgt, True)],
                     [jax.ShapeDtypeStruct((T, D), F32), jax.ShapeDtypeStruct((1, 1), F32)],
                     name=name, rows=T, tr=256, acc_outs=(1,))


def _rms_bwd_vals(x, w, dy):
    r = _rstd(x)
    xh = x * r
    g = dy * w
    dx = r * (g - xh * jnp.mean(g * xh, axis=-1, keepdims=True))
    return dx, _colsum(dy * xh)


def _rms_post_bwd(m, w_post, dh, *, name):
    T, D = m.shape

    def body(m_ref, w_ref, dh_ref, dm_ref, dw_ref):
        @pl.when(pl.program_id(0) == 0)
        def _():
            dw_ref[...] = jnp.zeros_like(dw_ref)

        dx, dw = _rms_bwd_vals(m_ref[...], w_ref[...], dh_ref[...])
        dm_ref[...] = dx.astype(BF16)
        dw_ref[...] += dw

    return _row_call(body, [(m, True), (w_post, False), (dh, True)],
                     [jax.ShapeDtypeStruct((T, D), BF16), jax.ShapeDtypeStruct((1, D), F32)],
                     name=name, rows=T, tr=256, acc_outs=(1,))


def _rms_pre_bwd(h, w_pre, du, dh_out, *, name):
    T, D = h.shape

    def body(h_ref, w_ref, du_ref, dho_ref, dh_ref, dw_ref):
        @pl.when(pl.program_id(0) == 0)
        def _():
            dw_ref[...] = jnp.zeros_like(dw_ref)

        dx, dw = _rms_bwd_vals(h_ref[...], w_ref[...], du_ref[...])
        dh_ref[...] = dho_ref[...] + dx
        dw_ref[...] += dw

    return _row_call(body, [(h, True), (w_pre, False), (du, True), (dh_out, True)],
                     [jax.ShapeDtypeStruct((T, D), F32), jax.ShapeDtypeStruct((1, D), F32)],
                     name=name, rows=T, tr=256, acc_outs=(1,))


def _ret_consts(T, C, H):
    lg = np.log1p(-np.exp2(-5.0 - np.arange(H, dtype=np.float64)))
    idx = np.arange(C, dtype=np.float64)
    dist = np.abs(idx[:, None] - idx[None, :])
    vis = (idx[None, :] // REF_CHUNK) <= (idx[:, None] // REF_CHUNK)
    mask = np.exp(dist[None] * lg[:, None, None]) * vis[None]
    xi = np.exp((idx[None, :] + 1.0) * lg[:, None])[..., None]
    zeta = np.exp((C - 1.0 - idx)[None, :] * lg[:, None])[..., None]
    half = RET_DK // 2
    inv_freq = ROPE_BASE ** (-np.arange(half, dtype=np.float32) / np.float32(half))
    ang = np.arange(T, dtype=np.float32)[:, None] * inv_freq[None, :].astype(np.float32)
    return (jnp.asarray(mask, F32), jnp.asarray(xi, F32), jnp.asarray(zeta, F32),
            jnp.asarray(np.cos(ang), F32), jnp.asarray(np.sin(ang), F32))


def _rot(t, cos, sin):
    half = RET_DK // 2
    t1, t2 = t[:, :half], t[:, half:]
    return jnp.concatenate([t1 * cos - t2 * sin, t1 * sin + t2 * cos], axis=1)


def _unrot(d, cos, sin):
    half = RET_DK // 2
    d1, d2 = d[:, :half], d[:, half:]
    return jnp.concatenate([d1 * cos + d2 * sin, d2 * cos - d1 * sin], axis=1)


def _ret_specs(C, H, rev, NS):
    def ci(i):
        return NS - 1 - i if rev else i

    nq = H
    q_spec = pl.BlockSpec((C, RET_DK), lambda h, i: (ci(i), h))
    k_spec = pl.BlockSpec((C, RET_DK), lambda h, i: (ci(i), nq + h))
    v_spec = pl.BlockSpec((C, RET_DV), lambda h, i: (ci(i), H + h))
    g_spec = pl.BlockSpec((C, RET_DV), lambda h, i: (ci(i), 2 * H + h))
    cs_spec = pl.BlockSpec((C, RET_DK // 2), lambda h, i: (ci(i), 0))
    m_spec = pl.BlockSpec((None, C, C), lambda h, i: (h, 0, 0))
    vec_spec = pl.BlockSpec((None, C, 1), lambda h, i: (h, 0, 0))
    gn_spec = pl.BlockSpec((1, RET_DV), lambda h, i: (0, h))
    st_spec = pl.BlockSpec((None, None, RET_DK, RET_DV), lambda h, i: (h, ci(i), 0, 0))
    return q_spec, k_spec, v_spec, g_spec, cs_spec, m_spec, vec_spec, gn_spec, st_spec


def _ret_fwd_vals(q, k, v, cos, sin, mask, xi, s_in):
    qr = _rot(q, cos, sin)
    kr = _rot(k, cos, sin) * (RET_DK ** -0.5)
    a = _dot(qr, kr, NT) * mask
    o = _dot(a, v) + _dot(qr, s_in) * xi
    mu = jnp.mean(o, axis=1, keepdims=True)
    oc = o - mu
    rstd = lax.rsqrt(jnp.mean(oc * oc, axis=1, keepdims=True) + GN_EPS)
    return qr, kr, a, oc * rstd, rstd


def _ret_fwd(proj, gn_w, consts, *, C, name):
    T = proj.shape[0]
    H = gn_w.shape[1] // RET_DV
    NS = T // C
    mask, xi, zeta, cos, sin = consts
    q_spec, k_spec, v_spec, g_spec, cs_spec, m_spec, vec_spec, gn_spec, st_spec = _ret_specs(C, H, False, NS)
    y_spec = pl.BlockSpec((C, RET_DV), lambda h, i: (i, h))

    def body(q_ref, k_ref, v_ref, g_ref, cos_ref, sin_ref, m_ref, xi_ref, ze_ref, gn_ref, y_ref, st_ref, S):
        @pl.when(pl.program_id(1) == 0)
        def _():
            S[...] = jnp.zeros_like(S)

        s_in = S[...]
        st_ref[...] = s_in
        v = v_ref[...]
        xi_v = xi_ref[...]
        qr, kr, a, on, rstd = _ret_fwd_vals(q_ref[...], k_ref[...], v, cos_ref[...], sin_ref[...], m_ref[...], xi_v, s_in)
        g = g_ref[...]
        y_ref[...] = (g * _sigmoid(g) * on * gn_ref[...]).astype(BF16)
        S[...] = s_in * xi_v[C - 1:C, :] + _dot(kr * ze_ref[...], v, TN)

    return pl.pallas_call(
        body, name=name, grid=(H, NS),
        in_specs=[q_spec, k_spec, v_spec, g_spec, cs_spec, cs_spec, m_spec, vec_spec, vec_spec, gn_spec],
        out_specs=[y_spec, st_spec],
        out_shape=[jax.ShapeDtypeStruct((T, H * RET_DV), BF16), jax.ShapeDtypeStruct((H, NS, RET_DK, RET_DV), F32)],
        scratch_shapes=[pltpu.VMEM((RET_DK, RET_DV), F32)],
        compiler_params=_cparams("parallel", "arbitrary"),
    )(proj, proj, proj, proj, cos, sin, mask, xi, zeta, gn_w)


def _stage_out(out_hbm, stage, sems, step, n_steps, row0, pieces, values):
    C = stage.shape[1]
    slot = step % 2

    def copies(sl):
        return [pltpu.make_async_copy(stage.at[sl, :, pl.ds(c0, w)],
                                      out_hbm.at[pl.ds(pl.multiple_of(row0, 16), C), pl.ds(pl.multiple_of(dc, 128), w)],
                                      sems.at[sl, n]) for n, (c0, w, dc) in enumerate(pieces)]

    @pl.when(step >= 2)
    def _():
        for cp in copies(slot):
            cp.wait()

    for (c0, w, _), v in zip(pieces, values):
        stage[slot, :, c0:c0 + w] = v
    for cp in copies(slot):
        cp.start()

    @pl.when(step == n_steps - 1)
    def _():
        for cp in copies(slot):
            cp.wait()
        if n_steps >= 2:
            for cp in copies(1 - slot):
                cp.wait()


def _ret_bwd(proj, gn_w, consts, states, dy, *, C, name):
    T = proj.shape[0]
    H = gn_w.shape[1] // RET_DV
    NS = T // C
    mask, xi, zeta, cos, sin = consts
    q_spec, k_spec, v_spec, g_spec, cs_spec, m_spec, vec_spec, gn_spec, st_spec = _ret_specs(C, H, True, NS)
    dy_spec = pl.BlockSpec((C, RET_DV), lambda h, i: (NS - 1 - i, h))
    scale = RET_DK ** -0.5
    wq, wv = H * RET_DK, H * RET_DV

    def body(q_ref, k_ref, v_ref, g_ref, cos_ref, sin_ref, m_ref, xi_ref, ze_ref, gn_ref, st_ref, dy_ref,
             dproj_ref, dgn_ref, dS, stage, sems):
        @pl.when(pl.program_id(1) == 0)
        def _():
            dS[...] = jnp.zeros_like(dS)
            dgn_ref[...] = jnp.zeros_like(dgn_ref)

        s_in = st_ref[...]
        v = v_ref[...]
        cos, sin, mask, xi_v, ze = cos_ref[...], sin_ref[...], m_ref[...], xi_ref[...], ze_ref[...]
        qr, kr, a, on, rstd = _ret_fwd_vals(q_ref[...], k_ref[...], v, cos, sin, mask, xi_v, s_in)
        g = g_ref[...]
        sg = _sigmoid(g)
        silu = g * sg
        gnw = gn_ref[...]
        dy = dy_ref[...].astype(F32)
        dg = (dy * on * gnw * (sg * (1.0 + g * (1.0 - sg)))).astype(BF16)
        t = dy * silu
        dgn_ref[...] += _colsum(t * on)
        don = t * gnw
        do = rstd * (don - jnp.mean(don, axis=1, keepdims=True) - on * jnp.mean(don * on, axis=1, keepdims=True))
        dox = do * xi_v
        ds_out = dS[...]
        da = _dot(do, v, NT) * mask
        kz = kr * ze
        dv = (_dot(a, do, TN) + _dot(kz, ds_out)).astype(BF16)
        dqr = _dot(da, kr) + _dot(dox, s_in, NT)
        dkr = _dot(da, qr, TN) + _dot(v, ds_out, NT) * ze
        dS[...] = ds_out * xi_v[C - 1:C, :] + _dot(qr, dox, TN)
        dq = _unrot(dqr, cos, sin).astype(BF16)
        dk = _unrot(dkr * scale, cos, sin).astype(BF16)
        h, i = pl.program_id(0), pl.program_id(1)
        pieces = [(0, RET_DK, h * RET_DK), (RET_DK, RET_DK, wq + h * RET_DK),
                  (2 * RET_DK, RET_DV, 2 * wq + h * RET_DV), (2 * RET_DK + RET_DV, RET_DV, 2 * wq + wv + h * RET_DV)]
        _stage_out(dproj_ref, stage, sems, h * NS + i, H * NS, (NS - 1 - i) * C, pieces, [dq, dk, dv, dg])

    return pl.pallas_call(
        body, name=name, grid=(H, NS),
        in_specs=[q_spec, k_spec, v_spec, g_spec, cs_spec, cs_spec, m_spec, vec_spec, vec_spec, gn_spec, st_spec, dy_spec],
        out_specs=[_ANY, gn_spec],
        out_shape=[jax.ShapeDtypeStruct((T, 2 * wq + 2 * wv), BF16), jax.ShapeDtypeStruct((1, H * RET_DV), F32)],
        scratch_shapes=[pltpu.VMEM((RET_DK, RET_DV), F32), pltpu.VMEM((2, C, 2 * RET_DK + 2 * RET_DV), BF16),
                        pltpu.SemaphoreType.DMA((2, 4))],
        compiler_params=_cparams("arbitrary", "arbitrary"),
    )(proj, proj, proj, proj, cos, sin, mask, xi, zeta, gn_w, states, dy)


def _shift_down(x, prev8, k):
    if k == 0:
        return x
    y = pltpu.roll(x, k, 0)
    row = lax.broadcasted_iota(jnp.int32, prev8.shape, 0)
    top = jnp.where(row < k, pltpu.roll(prev8, k, 0), y[:8])
    return jnp.concatenate([top, y[8:]], axis=0)


def _shift_up(x, next8, k):
    if k == 0:
        return x
    n = x.shape[0]
    y = pltpu.roll(x, n - k, 0)
    row = lax.broadcasted_iota(jnp.int32, next8.shape, 0)
    bot = jnp.where(row >= 8 - k, pltpu.roll(next8, 8 - k, 0), y[n - 8:])
    return jnp.concatenate([y[:n - 8], bot], axis=0)


def _conv_silu(raw, halo, w, b):
    cv = b
    for tap in range(SSD_CONV_W):
        cv = cv + _shift_down(raw, halo, SSD_CONV_W - 1 - tap) * w[tap:tap + 1, :]
    sg = _sigmoid(cv)
    return cv * sg, cv, sg


def _conv_silu_bwd(d_post, cv, sg, raw, halo, w, carry8):
    dcv = d_post * (sg * (1.0 + cv * (1.0 - sg)))
    d_raw = jnp.zeros_like(raw)
    dws = []
    for tap in range(SSD_CONV_W):
        k = SSD_CONV_W - 1 - tap
        d_raw = d_raw + _shift_up(dcv, carry8, k) * w[tap:tap + 1, :]
        dws.append(_colsum(dcv * _shift_down(raw, halo, k)))
    return d_raw, jnp.concatenate(dws, axis=0), _colsum(dcv), dcv[:8]


def _softplus(x):
    return jnp.maximum(x, 0.0) + jnp.log1p(jnp.exp(-jnp.abs(x)))


def _ssd_common(C, R, dt, dtT, bias, biasT, alog, alogT, E):
    p = dt + bias
    dtv = _softplus(p)
    a = -jnp.exp(alog)
    da = dtv * a
    daT = _softplus(dtT + biasT) * (-jnp.exp(alogT))
    row = lax.broadcasted_iota(jnp.int32, (C, C), 0)
    col = lax.broadcasted_iota(jnp.int32, (C, C), 1)
    tril = row >= col
    trilf = jnp.where(tril, 1.0, 0.0).astype(F32)
    triuf = jnp.where(col >= row, 1.0, 0.0).astype(F32)
    acum = _dot_sel(trilf, da, split="b")
    acumT = _dot_sel(daT, trilf, NT, split="a")
    al = acum[C - 1:C, :]
    ea = jnp.exp(acum)
    dte = jnp.exp(al - acum)
    eal = jnp.exp(al)
    return dict(p=p, dtv=dtv, a=a, da=da, tril=tril, triuf=triuf, acum=acum, acumT=acumT, al=al, ea=ea, dte=dte, eal=eal,
                dtv_e=_dot_sel(dtv, E, split="a", terms=2), ea_e=_dot_sel(ea, E, split="a", terms=2),
                dte_e=_dot_sel(dte, E, split="a", terms=2), eal_e=_dot_sel(eal, E, split="a"))


def _head_decay(q, r, C, R):
    seg = jnp.broadcast_to(q["acum"][:, r:r + 1], (C, C)) - q["acumT"][r:r + 1, :]
    return jnp.exp(jnp.where(q["tril"], seg, -1e30))


def _ssd_group_specs(C, R, NS, rev):
    RP = R * SSD_P
    G = SSD_G
    nz = 1
    hb = C // 8

    def ci(i):
        return NS - 1 - i if rev else i

    def halo_row(i):
        return jnp.maximum(ci(i) * hb - 1, 0)

    off_b = G * RP // SSD_N
    z_spec = pl.BlockSpec((C, RP), lambda g, i: (ci(i), g))
    x_spec = pl.BlockSpec((C, RP), lambda g, i: (ci(i), G + g))
    b_spec = pl.BlockSpec((C, SSD_N), lambda g, i: (ci(i), 2 * off_b + g))
    c_spec = pl.BlockSpec((C, SSD_N), lambda g, i: (ci(i), 2 * off_b + G + g))
    xh_spec = pl.BlockSpec((8, RP), lambda g, i: (halo_row(i), G + g))
    bh_spec = pl.BlockSpec((8, SSD_N), lambda g, i: (halo_row(i), 2 * off_b + g))
    ch_spec = pl.BlockSpec((8, SSD_N), lambda g, i: (halo_row(i), 2 * off_b + G + g))
    dt_spec = pl.BlockSpec((None, C, R), lambda g, i: (g, ci(i), 0))
    dtT_spec = pl.BlockSpec((None, R, C), lambda g, i: (g, 0, ci(i)))
    pr_spec = pl.BlockSpec((None, 1, R), lambda g, i: (g, 0, 0))
    prT_spec = pl.BlockSpec((None, R, 1), lambda g, i: (g, 0, 0))
    cwx_spec = pl.BlockSpec((SSD_CONV_W, RP), lambda g, i: (0, g))
    cwn_spec = pl.BlockSpec((SSD_CONV_W, SSD_N), lambda g, i: (0, g))
    cbx_spec = pl.BlockSpec((1, RP), lambda g, i: (0, g))
    cbn_spec = pl.BlockSpec((1, SSD_N), lambda g, i: (0, g))
    e_spec = pl.BlockSpec((R, RP), lambda g, i: (0, 0))
    st_spec = pl.BlockSpec((None, None, SSD_N, RP), lambda g, i: (g, ci(i), 0, 0))
    return dict(z=z_spec, x=x_spec, b=b_spec, c=c_spec, xh=xh_spec, bh=bh_spec, ch=ch_spec, dt=dt_spec, dtT=dtT_spec,
                pr=pr_spec, prT=prT_spec, cwx=cwx_spec, cwn=cwn_spec, cbx=cbx_spec, cbn=cbn_spec, e=e_spec, st=st_spec)


def _ssd_forward_vals(C, R, refs, first, s_in):
    E = refs["E"]
    halo_on = jnp.where(first, 0.0, 1.0)
    xh, bh, ch = refs["xh"] * halo_on, refs["bh"] * halo_on, refs["ch"] * halo_on
    xs, cvx, sgx = _conv_silu(refs["x"], xh, refs["cwx"], refs["cbx"])
    bm, cvb, sgb = _conv_silu(refs["b"], bh, refs["cwb"], refs["cbb"])
    cm, cvc, sgc = _conv_silu(refs["c"], ch, refs["cwc"], refs["cbc"])
    q = _ssd_common(C, R, refs["dt"], refs["dtT"], refs["bias"], refs["biasT"], refs["alog"], refs["alogT"], E)
    xdt = xs * q["dtv_e"]
    cb = _dot(cm, bm, NT)
    yoff_raw = _dot(cm, s_in)
    xdt_b = xdt.astype(BF16)
    low = lax.broadcasted_iota(jnp.int32, (1, 2 * SSD_P), 1) < SSD_P
    pairs = []
    for j in range(R // 2):
        xp = xdt_b[:, 2 * SSD_P * j:2 * SSD_P * (j + 1)]
        y0 = _dot(cb * _head_decay(q, 2 * j, C, R), xp)
        y1 = _dot(cb * _head_decay(q, 2 * j + 1, C, R), xp)
        pairs.append(jnp.where(low, y0, y1))
    ydiag = jnp.concatenate(pairs, axis=1)
    d_e =_dot_sel(refs["dskip"], E, split="a")
    y = ydiag + yoff_raw * q["ea_e"] + d_e * xs
    xd = xdt * q["dte_e"]
    s_out = s_in * q["eal_e"] + _dot(bm, xd, TN)
    z = refs["z"]
    sgz = _sigmoid(z)
    yz = y * (z * sgz)
    rn = lax.rsqrt(jnp.mean(yz * yz, axis=1, keepdims=True) + RMS_EPS)
    return dict(q=q, xh=xh, bh=bh, ch=ch, xs=xs, cvx=cvx, sgx=sgx, bm=bm, cvb=cvb, sgb=sgb, cm=cm, cvc=cvc, sgc=sgc,
                xdt=xdt, cb=cb, yoff_raw=yoff_raw, d_e=d_e, y=y, xd=xd, s_out=s_out, z=z, sgz=sgz, yz=yz, rn=rn)


_SSD_IN_NAMES = ("z", "x", "b", "c", "xh", "bh", "ch", "dt", "dtT", "bias", "biasT", "alog", "alogT", "dskip",
                 "cwx", "cwb", "cwc", "cbx", "cbb", "cbc", "nw", "E")


def _ssd_inputs(pm, dt_g, dtT_g, prm, sp):
    bias, biasT, alog, alogT, dskip, cwx, cwb, cwc, cbx, cbb, cbc, nw, E = prm
    args = [pm, pm, pm, pm, pm, pm, pm, dt_g, dtT_g, bias, biasT, alog, alogT, dskip, cwx, cwb, cwc, cbx, cbb, cbc, nw, E]
    specs = [sp["z"], sp["x"], sp["b"], sp["c"], sp["xh"], sp["bh"], sp["ch"], sp["dt"], sp["dtT"], sp["pr"], sp["prT"],
             sp["pr"], sp["prT"], sp["pr"], sp["cwx"], sp["cwn"], sp["cwn"], sp["cbx"], sp["cbn"], sp["cbn"], sp["cbx"], sp["e"]]
    return args, specs


def _ssd_fwd(pm, dt_g, dtT_g, prm, *, C, R, name):
    T = pm.shape[0]
    NS = T // C
    RP = R * SSD_P
    G = SSD_G
    sp = _ssd_group_specs(C, R, NS, False)
    args, specs = _ssd_inputs(pm, dt_g, dtT_g, prm, sp)
    nin = len(args)

    def body(*refs):
        ins = {n: r[...] for n, r in zip(_SSD_IN_NAMES, refs[:nin])}
        y_ref, st_ref, S = refs[nin:]
        first = pl.program_id(1) == 0

        @pl.when(first)
        def _():
            S[...] = jnp.zeros_like(S)

        s_in = S[...]
        st_ref[...] = s_in
        f = _ssd_forward_vals(C, R, ins, first, s_in)
        y_ref[...] = (f["yz"] * f["rn"] * ins["nw"]).astype(BF16)
        S[...] = f["s_out"]

    return pl.pallas_call(
        body, name=name, grid=(G, NS), in_specs=specs,
        out_specs=[pl.BlockSpec((C, RP), lambda g, i: (i, g)), sp["st"]],
        out_shape=[jax.ShapeDtypeStruct((T, G * RP), BF16), jax.ShapeDtypeStruct((G, NS, SSD_N, RP), F32)],
        scratch_shapes=[pltpu.VMEM((SSD_N, RP), F32)],
        compiler_params=_cparams("parallel", "arbitrary"),
    )(*args)


def _ssd_bwd(pm, dt_g, dtT_g, prm, states, dout, *, C, R, name):
    T = pm.shape[0]
    NS = T // C
    RP = R * SSD_P
    G = SSD_G
    sp = _ssd_group_specs(C, R, NS, True)
    args, specs = _ssd_inputs(pm, dt_g, dtT_g, prm, sp)
    nin = len(args)
    rows_spec = pl.BlockSpec((C, RP), lambda g, i: (NS - 1 - i, g))
    rown_spec = pl.BlockSpec((C, SSD_N), lambda g, i: (NS - 1 - i, g))
    args = args + [states, dout]
    specs = specs + [sp["st"], rows_spec]

    def body(*refs):
        ins = {n: r[...] for n, r in zip(_SSD_IN_NAMES, refs[:nin])}
        st_ref, dout_ref = refs[nin], refs[nin + 1]
        (dpm_ref, ddt_ref, dbias_ref, dalog_ref, dd_ref, dcwx_ref, dcwb_ref, dcwc_ref,
         dcbx_ref, dcbb_ref, dcbc_ref, dnw_ref) = refs[nin + 2:nin + 14]
        dS, cx8, cb8, cc8, stage, sems = refs[nin + 14:]
        acc_refs = (dbias_ref, dalog_ref, dd_ref, dcwx_ref, dcwb_ref, dcwc_ref, dcbx_ref, dcbb_ref, dcbc_ref, dnw_ref)
        step = pl.program_id(1)

        @pl.when(step == 0)
        def _():
            for r_ in acc_refs + (dS, cx8, cb8, cc8):
                r_[...] = jnp.zeros_like(r_)

        first = step == NS - 1
        E = ins["E"]
        s_in = st_ref[...]
        f = _ssd_forward_vals(C, R, ins, first, s_in)
        q = f["q"]
        xs, bm, cm, xdt, cb, y, z, sgz, yz, rn = (f[n] for n in ("xs", "bm", "cm", "xdt", "cb", "y", "z", "sgz", "yz", "rn"))
        nw = ins["nw"]
        dout = dout_ref[...].astype(F32)
        yh = yz * rn
        dnw_ref[...] += _colsum(dout * yh)
        g1 = dout * nw
        dyz = rn * (g1 - yh * jnp.mean(g1 * yh, axis=1, keepdims=True))
        dz = (dyz * y * (sgz * (1.0 + z * (1.0 - sgz)))).astype(BF16)
        dy = dyz * (z * sgz)
        dd_ref[...] += _dot_sel(_colsum(dy * xs), E, NT, split="a")
        dxs = dy * f["d_e"]
        dyo = dy * q["ea_e"]
        dcm = _dot(dyo, s_in, NT)
        ds_acc = _dot(cm, dyo, TN)
        dacum = _dot_sel(dy * f["yoff_raw"], E, NT, split="a", terms=2) * q["ea"]
        dacumT = jnp.zeros((R, C), F32)
        dcb = jnp.zeros((C, C), F32)
        rowR = lax.broadcasted_iota(jnp.int32, (1, R), 1)
        rowRT = lax.broadcasted_iota(jnp.int32, (R, 1), 0)
        dy_b, xdt_b = dy.astype(BF16), xdt.astype(BF16)
        low = lax.broadcasted_iota(jnp.int32, (1, 2 * SSD_P), 1) < SSD_P
        dxdt_pairs = []
        for j in range(R // 2):
            lanes = slice(2 * SSD_P * j, 2 * SSD_P * (j + 1))
            dyp, xp = dy_b[:, lanes], xdt_b[:, lanes]
            halves = []
            for r, mine in ((2 * j, low), (2 * j + 1, jnp.logical_not(low))):
                lr = _head_decay(q, r, C, R)
                w_r = cb * lr
                dw = _dot(jnp.where(mine, dyp, jnp.zeros_like(dyp)), xp, NT)
                halves.append(_dot(w_r, dyp, TN))
                dcb = dcb + dw * lr
                dseg = dw * w_r
                dacum = dacum + jnp.sum(dseg, axis=1, keepdims=True) * jnp.where(rowR == r, 1.0, 0.0)
                dacumT = dacumT - _colsum(dseg) * jnp.where(rowRT == r, 1.0, 0.0)
            dxdt_pairs.append(jnp.where(low, halves[0], halves[1]))
        dxdt = jnp.concatenate(dxdt_pairs, axis=1)
        dsn = dS[...]
        ds_acc = ds_acc + dsn * q["eal_e"]
        d_eal = _dot_sel(_colsum(dsn * s_in), E, NT, split="a")
        dbm = _dot(f["xd"], dsn, NT)
        dxd = _dot(bm, dsn)
        dxdt = dxdt + dxd * q["dte_e"]
        d_dte = _dot_sel(dxd * xdt, E, NT, split="a", terms=2) * q["dte"]
        d_al = _colsum(d_dte) + d_eal * q["eal"]
        dacum = dacum - d_dte
        rowC = lax.broadcasted_iota(jnp.int32, (C, 1), 0)
        dacum = dacum + jnp.where(rowC == C - 1, 1.0, 0.0) * d_al
        dS[...] = ds_acc
        dcm = dcm + _dot(dcb, bm)
        dbm = dbm + _dot(dcb, cm, TN)
        eye = jnp.where(lax.broadcasted_iota(jnp.int32, (C, C), 0) == lax.broadcasted_iota(jnp.int32, (C, C), 1), 1.0, 0.0)
        dacum = dacum + _dot_sel(eye, dacumT, NT, split="b")
        dda = _dot_sel(q["triuf"], dacum, split="b")
        ddtv = dda * q["a"] + _dot_sel(dxdt * xs, E, NT, split="a", terms=2)
        dalog_ref[...] += _colsum(dda * q["dtv"]) * q["a"]
        dxs = dxs + dxdt * q["dtv_e"]
        dp = ddtv * _sigmoid(q["p"])
        ddt_ref[...] = dp
        dbias_ref[...] += _colsum(dp)
        d_raw, d_w, d_b, c8 = _conv_silu_bwd(dxs, f["cvx"], f["sgx"], ins["x"], f["xh"], ins["cwx"], cx8[...])
        dx = d_raw.astype(BF16)
        dcwx_ref[...] += d_w
        dcbx_ref[...] += d_b
        cx8[...] = c8
        d_raw, d_w, d_b, c8 = _conv_silu_bwd(dbm, f["cvb"], f["sgb"], ins["b"], f["bh"], ins["cwb"], cb8[...])
        db = d_raw.astype(BF16)
        dcwb_ref[...] += d_w
        dcbb_ref[...] += d_b
        cb8[...] = c8
        d_raw, d_w, d_b, c8 = _conv_silu_bwd(dcm, f["cvc"], f["sgc"], ins["c"], f["ch"], ins["cwc"], cc8[...])
        dc = d_raw.astype(BF16)
        dcwc_ref[...] += d_w
        dcbc_ref[...] += d_b
        cc8[...] = c8
        g_ = pl.program_id(0)
        pieces = [(0, RP, g_ * RP), (RP, RP, G * RP + g_ * RP), (2 * RP, SSD_N, 2 * G * RP + g_ * SSD_N),
                  (2 * RP + SSD_N, SSD_N, 2 * G * RP + G * SSD_N + g_ * SSD_N)]
        _stage_out(dpm_ref, stage, sems, g_ * NS + step, G * NS, (NS - 1 - step) * C, pieces, [dz, dx, db, dc])

    out_specs = [_ANY, pl.BlockSpec((None, C, R), lambda g, i: (g, NS - 1 - i, 0)),
                 sp["pr"], sp["pr"], sp["pr"], sp["cwx"], sp["cwn"], sp["cwn"], sp["cbx"], sp["cbn"], sp["cbn"], sp["cbx"]]
    out_shape = [jax.ShapeDtypeStruct((T, 2 * G * RP + 2 * G * SSD_N), BF16),
                 jax.ShapeDtypeStruct((G, T, R), F32),
                 jax.ShapeDtypeStruct((G, 1, R), F32), jax.ShapeDtypeStruct((G, 1, R), F32), jax.ShapeDtypeStruct((G, 1, R), F32),
                 jax.ShapeDtypeStruct((SSD_CONV_W, G * RP), F32), jax.ShapeDtypeStruct((SSD_CONV_W, G * SSD_N), F32),
                 jax.ShapeDtypeStruct((SSD_CONV_W, G * SSD_N), F32),
                 jax.ShapeDtypeStruct((1, G * RP), F32), jax.ShapeDtypeStruct((1, G * SSD_N), F32),
                 jax.ShapeDtypeStruct((1, G * SSD_N), F32), jax.ShapeDtypeStruct((1, G * RP), F32)]
    return pl.pallas_call(
        body, name=name, grid=(G, NS), in_specs=specs, out_specs=out_specs, out_shape=out_shape,
        scratch_shapes=[pltpu.VMEM((SSD_N, RP), F32), pltpu.VMEM((8, RP), F32), pltpu.VMEM((8, SSD_N), F32),
                        pltpu.VMEM((8, SSD_N), F32), pltpu.VMEM((2, C, 2 * RP + 2 * SSD_N), BF16),
                        pltpu.SemaphoreType.DMA((2, 4))],
        compiler_params=_cparams("arbitrary", "arbitrary"),
    )(*args)


_ANY = pl.BlockSpec(memory_space=pl.ANY)


def _chip_peer(k):
    x, y, c = lax.axis_index("x"), lax.axis_index("y"), lax.axis_index("c")
    return (x ^ (k >> 1), y ^ (k & 1), c)


def _my_chip():
    return 2 * lax.axis_index("x") + lax.axis_index("y")


def _all_gather_chips(shards, halved, *, name):
    n = len(shards)

    def body(*refs):
        ins, outs = refs[:n], refs[n:2 * n]
        send, recv, fsend, frecv, loc = refs[2 * n:]
        s = _my_chip()
        c = lax.axis_index("c")
        sibling = (lax.axis_index("x"), lax.axis_index("y"), 1 - c)
        copies = []
        for a in range(n):
            cp = pltpu.make_async_copy(ins[a], outs[a].at[s], loc.at[a])
            cp.start()
            copies.append(cp)

        def rows(a, core):
            if not halved[a]:
                return slice(None)
            half = shards[a].shape[0] // 2
            return pl.ds(pl.multiple_of(core * half, 16), half)

        def over_ici(a, k, slot, core):
            return pltpu.make_async_remote_copy(
                src_ref=ins[a].at[rows(a, core)], dst_ref=outs[a].at[slot, rows(a, core)],
                send_sem=send.at[3 * a + k - 1], recv_sem=recv.at[3 * a + k - 1],
                device_id=_chip_peer(k), device_id_type=MESH_ID)

        def over_d2d(a, k, core):
            z = outs[a].at[s ^ k, rows(a, core)]
            return pltpu.make_async_remote_copy(
                src_ref=z, dst_ref=z, send_sem=fsend.at[3 * a + k - 1], recv_sem=frecv.at[3 * a + k - 1],
                device_id=sibling, device_id_type=MESH_ID)

        sent = []
        for a in range(n):
            for k in (1, 2, 3):
                cp = over_ici(a, k, s, c)
                cp.start()
                sent.append(cp)
        passed = []
        for a in range(n):
            for k in (1, 2, 3):
                over_ici(a, k, s ^ k, c).wait_recv()
                if halved[a]:
                    cp = over_d2d(a, k, c)
                    cp.start()
                    passed.append(cp)
        for a in range(n):
            if halved[a]:
                for k in (1, 2, 3):
                    over_d2d(a, k, 1 - c).wait_recv()
        for cp in sent + passed:
            cp.wait_send()
        for cp in copies:
            cp.wait()

    for a, h in zip(shards, halved):
        assert not h or a.shape[0] % 32 == 0, a.shape
    return pl.pallas_call(
        body, name=name, in_specs=[_ANY] * n, out_specs=[_ANY] * n,
        out_shape=[jax.ShapeDtypeStruct((4,) + a.shape, a.dtype) for a in shards],
        scratch_shapes=[pltpu.SemaphoreType.DMA((3 * n,))] * 4 + [pltpu.SemaphoreType.DMA((n,))],
        compiler_params=pltpu.CompilerParams(has_side_effects=True),
    )(*shards)


_HBM = pl.BlockSpec(memory_space=pltpu.HBM)
_SEM = pl.BlockSpec(memory_space=pltpu.SEMAPHORE)
_EFFECT = pltpu.SideEffectType.DATAFLOW_SIDE_EFFECTING


def _split_copies(src, land, send, recv, loc, a, scatter):
    s = _my_chip()
    mine = pltpu.make_async_copy(src.at[s] if scatter else src, land.at[s], loc.at[a])
    pairs = []
    for k in (1, 2, 3):
        sems = dict(send_sem=send.at[3 * a + k - 1], recv_sem=recv.at[3 * a + k - 1],
                    device_id=_chip_peer(k), device_id_type=MESH_ID)
        out = pltpu.make_async_remote_copy(src_ref=src.at[s ^ k] if scatter else src, dst_ref=land.at[s], **sems)
        arriving = pltpu.make_async_remote_copy(src_ref=src.at[s ^ k] if scatter else src, dst_ref=land.at[s ^ k], **sems)
        pairs.append((out, arriving))
    return mine, pairs


def _split_start(arrs, *, scatter, after, name):
    n = len(arrs)
    zones = [lax.empty(a.shape if scatter else (4,) + a.shape, a.dtype) for a in arrs]

    def body(*refs):
        srcs, lands = refs[:n], refs[n:2 * n]
        send, recv, loc = refs[2 * n + 1:2 * n + 4]
        token = refs[-1]
        for a in range(n):
            mine, pairs = _split_copies(srcs[a], lands[a], send, recv, loc, a, scatter)
            mine.start()
            for out, _ in pairs:
                out.start()
        token[...] = jnp.zeros_like(token)

    res = pl.pallas_call(
        body, name=name,
        out_shape=(pltpu.SemaphoreType.DMA((3 * n,)), pltpu.SemaphoreType.DMA((3 * n,)), pltpu.SemaphoreType.DMA((n,)),
                   *[pltpu.HBM(a.shape, a.dtype) for a in arrs], *[pltpu.HBM(z.shape, z.dtype) for z in zones],
                   jax.ShapeDtypeStruct((8, 128), F32)),
        in_specs=[_HBM] * (2 * n) + [_ANY],
        out_specs=(_SEM, _SEM, _SEM, *([_HBM] * (2 * n)), pl.BlockSpec(memory_space=pltpu.VMEM)),
        input_output_aliases={i: 3 + i for i in range(2 * n)},
        compiler_params=pltpu.CompilerParams(has_side_effects=_EFFECT),
    )(*[pltpu.with_memory_space_constraint(a, pltpu.HBM) for a in arrs],
      *[pltpu.with_memory_space_constraint(z, pltpu.HBM) for z in zones], after)
    return res[:3], list(res[3:3 + n]), list(res[3 + n:3 + 2 * n]), res[-1]


def _split_wait(sems, src, land, a, *, scatter, after, name):
    def body(src_ref, land_ref, send, recv, loc, after_ref, src_out, land_out):
        mine, pairs = _split_copies(src_ref, land_ref, send, recv, loc, a, scatter)
        mine.wait()
        for out, arriving in pairs:
            out.wait_send()
            arriving.wait_recv()

    return pl.pallas_call(
        body, name=name, out_shape=(pltpu.HBM(src.shape, src.dtype), pltpu.HBM(land.shape, land.dtype)),
        in_specs=[_HBM, _HBM, _SEM, _SEM, _SEM, _ANY], out_specs=(_HBM, _HBM), input_output_aliases={0: 0, 1: 1},
        compiler_params=pltpu.CompilerParams(has_side_effects=_EFFECT),
    )(src, land, *sems, after)[1]


def _sibling_copies(srcs, lands, send, recv):
    sib = (lax.axis_index("x"), lax.axis_index("y"), 1 - lax.axis_index("c"))
    return [pltpu.make_async_remote_copy(src_ref=srcs[a], dst_ref=lands[a], send_sem=send.at[a], recv_sem=recv.at[a],
                                         device_id=sib, device_id_type=MESH_ID) for a in range(len(srcs))]


def _swap_start(arrs, *, after, name):
    n = len(arrs)
    zones = [lax.empty(a.shape, a.dtype) for a in arrs]

    def body(*refs):
        for cp in _sibling_copies(refs[:n], refs[n:2 * n], refs[2 * n + 1], refs[2 * n + 2]):
            cp.start()

    res = pl.pallas_call(
        body, name=name,
        out_shape=(pltpu.SemaphoreType.DMA((n,)), pltpu.SemaphoreType.DMA((n,)),
                   *[pltpu.HBM(a.shape, a.dtype) for a in arrs], *[pltpu.HBM(z.shape, z.dtype) for z in zones]),
        in_specs=[_HBM] * (2 * n) + [_ANY], out_specs=(_SEM, _SEM, *([_HBM] * (2 * n))),
        input_output_aliases={i: 2 + i for i in range(2 * n)},
        compiler_params=pltpu.CompilerParams(has_side_effects=_EFFECT),
    )(*[pltpu.with_memory_space_constraint(a, pltpu.HBM) for a in arrs],
      *[pltpu.with_memory_space_constraint(z, pltpu.HBM) for z in zones], after)
    return res[:2], list(res[2:2 + n]), list(res[2 + n:])


def _swap_wait(sems, srcs, lands, *, after, name):
    n = len(srcs)

    def body(*refs):
        for cp in _sibling_copies(refs[:n], refs[n:2 * n], refs[2 * n], refs[2 * n + 1]):
            cp.wait_send()
            cp.wait_recv()

    res = pl.pallas_call(
        body, name=name, out_shape=tuple(pltpu.HBM(a.shape, a.dtype) for a in list(srcs) + list(lands)),
        in_specs=[_HBM] * (2 * n) + [_SEM, _SEM, _ANY], out_specs=tuple([_HBM] * (2 * n)),
        input_output_aliases={i: i for i in range(2 * n)},
        compiler_params=pltpu.CompilerParams(has_side_effects=_EFFECT),
    )(*srcs, *lands, *sems, after)
    return list(res[n:])


def _all_gather_devices(v, *, name):
    r = v.shape[0]

    def body(v_ref, out_ref, send, recv):
        x, y, c = lax.axis_index("x"), lax.axis_index("y"), lax.axis_index("c")
        me = 4 * x + 2 * y + c
        out_ref[me] = v_ref[...]
        cps = []
        for k in range(1, 8):
            peer = (x ^ (k >> 2), y ^ ((k >> 1) & 1), c ^ (k & 1))
            cp = pltpu.make_async_remote_copy(src_ref=v_ref, dst_ref=out_ref.at[me], send_sem=send.at[k - 1],
                                              recv_sem=recv.at[k - 1], device_id=peer, device_id_type=MESH_ID)
            cp.start()
            cps.append(cp)
        for k, cp in enumerate(cps, start=1):
            cp.wait_send()
            pltpu.make_async_remote_copy(src_ref=v_ref, dst_ref=out_ref.at[me ^ k], send_sem=send.at[k - 1],
                                         recv_sem=recv.at[k - 1], device_id=(x, y, c), device_id_type=MESH_ID).wait_recv()

    vm = pl.BlockSpec(memory_space=pltpu.VMEM)
    return pl.pallas_call(
        body, name=name, in_specs=[vm], out_specs=vm, out_shape=jax.ShapeDtypeStruct((8, r, 128), F32),
        scratch_shapes=[pltpu.SemaphoreType.DMA((7,)), pltpu.SemaphoreType.DMA((7,))],
        compiler_params=pltpu.CompilerParams(has_side_effects=True),
    )(v)


def _row_tile(r, target):
    best = None
    for t in range(16, min(target, r) + 1, 16):
        if r % t == 0:
            best = t
    return best or r


def _sum_slots(buf, *, name, tr=384, layer=None, stack=None):
    S, r, c = buf.shape
    tr = _row_tile(r, tr)

    def body(*refs):
        b_ref, o_ref = refs[0], refs[-1]
        acc = b_ref[0].astype(F32)
        for j in range(1, S):
            acc = acc + b_ref[j].astype(F32)
        o_ref[...] = acc

    in_specs = [pl.BlockSpec((S, tr, c), lambda i: (0, i, 0))]
    args, alias = [buf], {}
    if layer is None:
        out_spec, out_shape = pl.BlockSpec((tr, c), lambda i: (i, 0)), jax.ShapeDtypeStruct((r, c), F32)
    else:
        out_spec = pl.BlockSpec((None, tr, c), lambda i: (layer, i, 0))
        out_shape = jax.ShapeDtypeStruct((2, r, c), F32)
        if stack is not None:
            in_specs.append(_ANY)
            args.append(stack)
            alias = {1: 0}
    return pl.pallas_call(
        body, name=name, grid=(r // tr,), in_specs=in_specs, out_specs=out_spec, out_shape=out_shape,
        input_output_aliases=alias, compiler_params=_cparams("parallel"),
    )(*args)


def _add2(a, b, *, name, tr=384):
    r, c = a.shape
    tr = _row_tile(r, tr)

    def body(a_ref, b_ref, o_ref):
        o_ref[...] = a_ref[...] + b_ref[...]

    spec = pl.BlockSpec((tr, c), lambda i: (i, 0))
    return pl.pallas_call(body, name=name, grid=(r // tr,), in_specs=[spec, spec], out_specs=spec,
                          out_shape=jax.ShapeDtypeStruct((r, c), F32), compiler_params=_cparams("parallel"))(a, b)


ADAMW_BLOCK_ELEMS = 1 << 18


def _adamw(w, gs, m, v, *, name, tr=256):
    r, c = w.shape
    tr = _row_tile(r, min(tr, max(16, ADAMW_BLOCK_ELEMS // c)))
    bc1 = 1.0 - ADAM_B1 ** ADAM_STEP
    bc2 = 1.0 - ADAM_B2 ** ADAM_STEP
    ng = len(gs)

    def body(*refs):
        w_ref, m_ref, v_ref = refs[0], refs[1 + ng], refs[2 + ng]
        g_ref, d_ref, mo_ref, vo_ref = refs[3 + ng:]
        gg = refs[1][...] if ng == 1 else refs[1][...] + refs[2][...]
        mn = ADAM_B1 * m_ref[...] + (1.0 - ADAM_B1) * gg
        vn = ADAM_B2 * v_ref[...] + (1.0 - ADAM_B2) * (gg * gg)
        g_ref[...] = gg
        mo_ref[...] = mn
        vo_ref[...] = vn
        d_ref[...] = -ADAM_LR * ((mn / bc1) / (jnp.sqrt(vn / bc2) + ADAM_EPS) + ADAM_WD * w_ref[...])

    spec = pl.BlockSpec((tr, c), lambda i: (i, 0))
    return pl.pallas_call(body, name=name, grid=(r // tr,), in_specs=[spec] * (3 + ng), out_specs=[spec] * 4,
                          out_shape=[jax.ShapeDtypeStruct((r, c), F32)] * 4,
                          compiler_params=_cparams("parallel"))(w, *gs, m, v)


def _pack(vecs, rows):
    flat = jnp.concatenate([v.reshape(-1).astype(F32) for v in vecs])
    return jnp.pad(flat, (0, rows * 128 - flat.shape[0])).reshape(rows, 128)


def _unpack(packed, shapes):
    flat = packed.reshape(-1)
    out, off = [], 0
    for s in shapes:
        n = math.prod(s)
        out.append(flat[off:off + n].reshape(s))
        off += n
    return out


def _pack_rows(shapes):
    n = sum(math.prod(s) for s in shapes)
    return -(-n // 1024) * 8


def kernel(x, norm_mix_pre, norm_mix_post, norm_ffn_pre, norm_ffn_post, ret_w_in, ret_gn_w, ret_w_out, ssd_w_in, ssd_conv_w, ssd_conv_b, ssd_dt_bias, ssd_a_log, ssd_d, ssd_norm_w, ssd_w_out, mlp_w_up, mlp_w_down, loss_target, m_norm_mix_pre, m_norm_mix_post, m_norm_ffn_pre, m_norm_ffn_post, m_ret_w_in, m_ret_gn_w, m_ret_w_out, m_ssd_w_in, m_ssd_conv_w, m_ssd_conv_b, m_ssd_dt_bias, m_ssd_a_log, m_ssd_d, m_ssd_norm_w, m_ssd_w_out, m_mlp_w_up, m_mlp_w_down, v_norm_mix_pre, v_norm_mix_post, v_norm_ffn_pre, v_norm_ffn_post, v_ret_w_in, v_ret_gn_w, v_ret_w_out, v_ssd_w_in, v_ssd_conv_w, v_ssd_conv_b, v_ssd_dt_bias, v_ssd_a_log, v_ssd_d, v_ssd_norm_w, v_ssd_w_out, v_mlp_w_up, v_mlp_w_down):
    T, D = x.shape[1], x.shape[2]
    H = D // RET_DK
    d_inner = 2 * D
    R = d_inner // SSD_P // SSD_G
    RP = R * SSD_P
    n_heads = SSD_G * R
    conv_dim = d_inner + 2 * SSD_G * SSD_N
    n_main = d_inner + conv_dim
    C = min(256, T)
    chip = _my_chip()
    xs, tgt = x[0], loss_target[0]

    conv_sh = ssd_conv_w.shape[2]
    small_shapes = [(SSD_CONV_W, conv_sh), (conv_sh,), (ssd_norm_w.shape[1],)]
    small_rows = _pack_rows(small_shapes)
    shards = [ret_w_in[0].T.astype(BF16), ret_w_out[0].astype(BF16), ssd_w_in[0].T.astype(BF16), ssd_w_out[0].astype(BF16),
              mlp_w_up[0].T.astype(BF16), mlp_w_up[1].T.astype(BF16), mlp_w_down[0].astype(BF16), mlp_w_down[1].astype(BF16)]
    (ret_in_g, small_g) = _all_gather_chips([shards[0], _pack([ssd_conv_w[0], ssd_conv_b[0], ssd_norm_w[0]], small_rows)],
                                            [True, False], name="gather_first")

    def full(g):
        return g.reshape(4 * g.shape[1], g.shape[2])

    def start_gather(idx, after, name):
        sems, srcs, zones, tok = _split_start([shards[i] for i in idx], scatter=False, after=after, name=name)
        return {i: (sems, srcs[n], zones[n], n) for n, i in enumerate(idx)}, tok

    def arrived(stage, i, after, name):
        sems, src, zone, n = stage[i]
        return full(_split_wait(sems, src, zone, n, scatter=False, after=after, name=name))

    ret_in_t = full(ret_in_g)
    sm = [_unpack(small_g[j], small_shapes) for j in range(4)]
    conv_w = jnp.concatenate([sm[j][0] for j in range(4)], axis=1)
    conv_b = jnp.concatenate([sm[j][1] for j in range(4)])[None, :]
    norm_w = jnp.concatenate([sm[j][2] for j in range(4)])[None, :]

    gb = SSD_G * SSD_N
    ssd_prm = (ssd_dt_bias.reshape(SSD_G, 1, R), ssd_dt_bias.reshape(SSD_G, R, 1),
               ssd_a_log.reshape(SSD_G, 1, R), ssd_a_log.reshape(SSD_G, R, 1), ssd_d.reshape(SSD_G, 1, R),
               conv_w[:, :d_inner], conv_w[:, d_inner:d_inner + gb], conv_w[:, d_inner + gb:],
               conv_b[:, :d_inner], conv_b[:, d_inner:d_inner + gb], conv_b[:, d_inner + gb:],
               norm_w, jnp.asarray(np.kron(np.eye(R), np.ones((1, SSD_P))), F32))
    ret_consts = _ret_consts(T, C, H)

    u0 = _rms_pre(xs, norm_mix_pre[0:1], name="pre0")
    stage1, tok = start_gather((1, 4), ret_in_g, "gather_start1")
    proj = _matmul(u0, ret_in_t, "nt", out_dtype=F32, name="ret_in", after=tok)
    stage2, tok = start_gather((6, 2), proj, "gather_start2")
    y_ret, st_ret = _ret_fwd(proj, ret_gn_w, ret_consts, C=C, name="ret_fwd")
    ret_out = arrived(stage1, 1, y_ret, "gather_wait_ret_out")
    m0 = _matmul(y_ret, ret_out, "nn", out_dtype=F32, name="ret_out", after=tok)
    h1, u1 = _rms_post_pre(xs, m0, norm_mix_post[0:1], norm_ffn_pre[0:1], name="post_pre1")
    up_t0 = arrived(stage1, 4, u1, "gather_wait_up0")
    a0, hh0 = _matmul(u1, up_t0, "nt", out_dtype=BF16, name="mlp_up0", epi="relu2")
    stage3, tok = start_gather((3, 5, 7), hh0, "gather_start3")
    down0 = arrived(stage2, 6, hh0, "gather_wait_down0")
    f0 = _matmul(hh0, down0, "nn", out_dtype=F32, name="mlp_down0", after=tok)
    h2, u2 = _rms_post_pre(h1, f0, norm_ffn_post[0:1], norm_mix_pre[1:2], name="post_pre2")
    ssd_in_t = arrived(stage2, 2, u2, "gather_wait_ssd_in")
    ssd_main_t, ssd_dt_t = ssd_in_t[:n_main], ssd_in_t[n_main:]
    pm = _matmul(u2, ssd_main_t, "nt", out_dtype=F32, name="ssd_in")
    pdt = _matmul(u2, ssd_dt_t, "nt", out_dtype=F32, name="ssd_in_dt")
    dt_g = pdt.reshape(T, SSD_G, R).transpose(1, 0, 2)
    dtT_g = pdt.reshape(T, SSD_G, R).transpose(1, 2, 0)
    y_ssd, st_ssd = _ssd_fwd(pm, dt_g, dtT_g, ssd_prm, C=C, R=R, name="ssd_fwd")
    ssd_out = arrived(stage3, 3, y_ssd, "gather_wait_ssd_out")
    m1 = _matmul(y_ssd, ssd_out, "nn", out_dtype=F32, name="ssd_out")
    h3, u3 = _rms_post_pre(h2, m1, norm_mix_post[1:2], norm_ffn_pre[1:2], name="post_pre3")
    up_t1 = arrived(stage3, 5, u3, "gather_wait_up1")
    a1, hh1 = _matmul(u3, up_t1, "nt", out_dtype=BF16, name="mlp_up1", epi="relu2")
    down1 = arrived(stage3, 7, hh1, "gather_wait_down1")
    f1 = _matmul(hh1, down1, "nn", out_dtype=F32, name="mlp_down1")
    up_t, down = (up_t0, up_t1), (down0, down1)
    dh4, sq = _rms_post_loss(h3, f1, norm_ffn_post[1:2], tgt, name="post_loss")
    loss = lax.psum(sq[0, 0], MESH_AXES) * (0.5 / D)

    in_flight = []

    def send_grad(g, name):
        part = g if g.ndim == 3 else g.reshape(4, g.shape[0] // 4, g.shape[1])
        sems, srcs, zones, tok = _split_start([part], scatter=True, after=part, name=f"scatter_start_{name}")
        in_flight.append((name, sems, srcs[0], zones[0]))
        return tok

    def mlp_bwd(i, dh_out, h_in, u, a, hh, f):
        df, d_post = _rms_post_bwd(f, norm_ffn_post[i:i + 1], dh_out, name=f"post_bwd_ffn{i}")
        tok = send_grad(_matmul(hh, df, "tn", out_dtype=BF16, name=f"mlp_down_wg{i}"), f"down{i}")
        da = _matmul(df, down[i], "nt", out_dtype=BF16, name=f"mlp_down_dg{i}", epi="drelu2", extra=a, after=tok)
        tok = send_grad(_matmul(u, da, "tn", out_dtype=BF16, name=f"mlp_up_wg{i}", col_parts=4), f"up{i}")
        du = _matmul(da, up_t[i], "nn", out_dtype=F32, name=f"mlp_up_dg{i}", after=tok)
        dh, d_pre = _rms_pre_bwd(h_in, norm_ffn_pre[i:i + 1], du, dh_out, name=f"pre_bwd_ffn{i}")
        return dh, d_pre, d_post

    dh3, d_nfp1, d_nfpost1 = mlp_bwd(1, dh4, h3, u3, a1, hh1, f1)
    dm1, d_nmpost1 = _rms_post_bwd(m1, norm_mix_post[1:2], dh3, name="post_bwd_mix1")
    tok = send_grad(_matmul(y_ssd, dm1, "tn", out_dtype=BF16, name="ssd_out_wg"), "ssd_out")
    dy_ssd = _matmul(dm1, ssd_out, "nt", out_dtype=F32, name="ssd_out_dg", after=tok)
    (dpm, ddt_g, d_bias, d_alog, d_dskip, dcwx, dcwb, dcwc, dcbx, dcbb, dcbc, d_normw) = _ssd_bwd(
        pm, dt_g, dtT_g, ssd_prm, st_ssd, dy_ssd, C=C, R=R, name="ssd_bwd")
    dpdt = ddt_g.transpose(1, 0, 2).reshape(T, n_heads).astype(BF16)
    tok = send_grad(jnp.concatenate([_matmul(dpm, u2, "tn", out_dtype=BF16, name="ssd_in_wg"),
                                     _matmul(dpdt, u2, "tn", out_dtype=BF16, name="ssd_in_dt_wg")], axis=0), "ssd_in")
    du2 = _matmul(dpm, ssd_main_t, "nn", out_dtype=F32, name="ssd_in_dg", after=tok)
    du2 = _matmul(dpdt, ssd_dt_t, "nn", out_dtype=F32, name="ssd_in_dt_dg", epi="add", extra=du2)
    dh2, d_nmp1 = _rms_pre_bwd(h2, norm_mix_pre[1:2], du2, dh3, name="pre_bwd_mix1")
    dh1, d_nfp0, d_nfpost0 = mlp_bwd(0, dh2, h1, u1, a0, hh0, f0)
    dm0, d_nmpost0 = _rms_post_bwd(m0, norm_mix_post[0:1], dh1, name="post_bwd_mix0")
    tok = send_grad(_matmul(y_ret, dm0, "tn", out_dtype=BF16, name="ret_out_wg"), "ret_out")
    dy_ret = _matmul(dm0, ret_out, "nt", out_dtype=F32, name="ret_out_dg", after=tok)
    dproj, d_gn = _ret_bwd(proj, ret_gn_w, ret_consts, st_ret, dy_ret, C=C, name="ret_bwd")
    tok = send_grad(_matmul(u0, dproj, "tn", out_dtype=BF16, name="ret_in_wg", col_parts=4), "ret_in")
    du0 = _matmul(dproj, ret_in_t, "nn", out_dtype=F32, name="ret_in_dg", after=tok)
    grad_x, d_nmp0 = _rms_pre_bwd(xs, norm_mix_pre[0:1], du0, dh1, name="pre_bwd_mix0")

    landed = {nm: _split_wait(sems, src, zone, 0, scatter=True, after=grad_x, name=f"scatter_wait_{nm}")
              for nm, sems, src, zone in in_flight}
    part_sum = {}

    def two_d(t):
        return t.reshape(-1, t.shape[-1])

    def sum_chips(nm):
        if nm[-1] in "01" and nm[:-1] in ("up", "down"):
            fam, layer = nm[:-1], int(nm[-1])
            part_sum[fam] = _sum_slots(landed[nm], name=f"sum_chips_{nm}", layer=layer, stack=part_sum.get(fam))
        else:
            part_sum[nm] = _sum_slots(landed[nm], name=f"sum_chips_{nm}")

    swaps = []
    for names_, fams in ((("down1", "up1", "ssd_out", "ssd_in", "down0"), ("ssd_out", "ssd_in", "down")),
                         (("up0", "ret_out", "ret_in"), ("up", "ret_out", "ret_in"))):
        for nm in names_:
            sum_chips(nm)
        mine = [two_d(part_sum[f]) for f in fams]
        swaps.append((fams, _swap_start(mine, after=mine[-1], name=f"swap_start{len(swaps)}")))

    def upd(w, gs, m, v, name):
        shp = w.shape
        w2, m2, v2 = (two_d(t) for t in (w, m, v))
        return tuple(t.reshape(shp) for t in _adamw(w2, [g.reshape(w2.shape) for g in gs], m2, v2, name=name))

    def upd_t(w, gs, m, v, name):
        return tuple(t.T[None] for t in _adamw(w[0].T, gs, m[0].T, v[0].T, name=name))

    todo = {"ret_in": (upd, ret_w_in, m_ret_w_in, v_ret_w_in, "ret_w_in"),
            "ret_out": (upd, ret_w_out, m_ret_w_out, v_ret_w_out, "ret_w_out"),
            "ssd_in": (upd_t, ssd_w_in, m_ssd_w_in, v_ssd_w_in, "ssd_w_in"),
            "ssd_out": (upd, ssd_w_out, m_ssd_w_out, v_ssd_w_out, "ssd_w_out"),
            "up": (upd, mlp_w_up, m_mlp_w_up, v_mlp_w_up, "mlp_w_up"),
            "down": (upd, mlp_w_down, m_mlp_w_down, v_mlp_w_down, "mlp_w_down")}
    res = {}
    prev = grad_x
    for n_, (fams, (sems, srcs, zones)) in enumerate(swaps):
        theirs = _swap_wait(sems, srcs, zones, after=prev, name=f"swap_wait{n_}")
        for f, mine, other in zip(fams, srcs, theirs):
            fn, w, m, v, out_name = todo[f]
            res[out_name] = fn(w, [mine, other], m, v, f"adamw_{f}")
            prev = res[out_name][0]

    d_conv_w = jnp.concatenate([dcwx, dcwb, dcwc], axis=1)
    d_conv_b = jnp.concatenate([dcbx, dcbb, dcbc], axis=1)
    small_grads = [jnp.concatenate([d_nmp0, d_nmp1]), jnp.concatenate([d_nmpost0, d_nmpost1]),
                   jnp.concatenate([d_nfp0, d_nfp1]), jnp.concatenate([d_nfpost0, d_nfpost1]),
                   d_gn, d_bias.reshape(1, n_heads), d_alog.reshape(1, n_heads), d_dskip.reshape(1, n_heads),
                   d_conv_w, d_conv_b, d_normw]
    sg_shapes = [g.shape for g in small_grads]
    sg_rows = _pack_rows(sg_shapes)
    everyone = _all_gather_devices(_pack(small_grads, sg_rows), name="gather_small_grads")
    sg = _unpack(_sum_slots(everyone, name="sum_small_grads", tr=sg_rows), sg_shapes)
    (g_nmp, g_nmpost, g_nfp, g_nfpost, g_gn, g_bias, g_alog, g_dskip, g_cw_full, g_cb_full, g_nw_full) = sg
    g_cw = lax.dynamic_slice_in_dim(g_cw_full, chip * conv_sh, conv_sh, axis=1)[None]
    g_cb = lax.dynamic_slice_in_dim(g_cb_full, chip * conv_sh, conv_sh, axis=1)
    nw_sh = ssd_norm_w.shape[1]
    g_nw = lax.dynamic_slice_in_dim(g_nw_full, chip * nw_sh, nw_sh, axis=1)
    small = [("norm_mix_pre", norm_mix_pre, g_nmp, m_norm_mix_pre, v_norm_mix_pre),
             ("norm_mix_post", norm_mix_post, g_nmpost, m_norm_mix_post, v_norm_mix_post),
             ("norm_ffn_pre", norm_ffn_pre, g_nfp, m_norm_ffn_pre, v_norm_ffn_pre),
             ("norm_ffn_post", norm_ffn_post, g_nfpost, m_norm_ffn_post, v_norm_ffn_post),
             ("ret_gn_w", ret_gn_w, g_gn, m_ret_gn_w, v_ret_gn_w),
             ("ssd_conv_w", ssd_conv_w, g_cw, m_ssd_conv_w, v_ssd_conv_w),
             ("ssd_conv_b", ssd_conv_b, g_cb, m_ssd_conv_b, v_ssd_conv_b),
             ("ssd_dt_bias", ssd_dt_bias, g_bias, m_ssd_dt_bias, v_ssd_dt_bias),
             ("ssd_a_log", ssd_a_log, g_alog, m_ssd_a_log, v_ssd_a_log),
             ("ssd_d", ssd_d, g_dskip, m_ssd_d, v_ssd_d),
             ("ssd_norm_w", ssd_norm_w, g_nw, m_ssd_norm_w, v_ssd_norm_w)]
    sw_shapes = [w.shape for _, w, _, _, _ in small]
    sw_rows = _pack_rows(sw_shapes)
    packs = [_pack([t[j] for t in small], sw_rows) for j in (1, 2, 3, 4)]
    _, d_p, m_p, v_p = _adamw(packs[0], [packs[1]], packs[2], packs[3], name="adamw_small", tr=sw_rows)
    d_s, m_s, v_s = _unpack(d_p, sw_shapes), _unpack(m_p, sw_shapes), _unpack(v_p, sw_shapes)
    for j, (nm, w, g, _, _) in enumerate(small):
        res[nm] = (g.reshape(w.shape), d_s[j], m_s[j], v_s[j])

    order = ["norm_mix_pre", "norm_mix_post", "norm_ffn_pre", "norm_ffn_post", "ret_w_in", "ret_gn_w", "ret_w_out",
             "ssd_w_in", "ssd_conv_w", "ssd_conv_b", "ssd_dt_bias", "ssd_a_log", "ssd_d", "ssd_norm_w", "ssd_w_out",
             "mlp_w_up", "mlp_w_down"]
    return (loss, grad_x[None], *[res[n][0] for n in order], *[res[n][1] for n in order],
            *[res[n][2] for n in order], *[res[n][3] for n in order])
```

```python
import functools
import math

import numpy as np
import jax
import jax.numpy as jnp
from jax import lax
from jax.experimental import pallas as pl
from jax.experimental.pallas import tpu as pltpu

F32 = jnp.float32
BF16 = jnp.bfloat16
HI = lax.Precision.HIGHEST
VMEM_LIMIT_BYTES = 56 * 1024 * 1024
MESH_AXES = ("x", "y", "c")
MESH_ID = pl.DeviceIdType.MESH

RMS_EPS = 1e-6
GN_EPS = 1e-5
RET_DK = 256
RET_DV = 512
ROPE_BASE = 10000.0
REF_CHUNK = 64
SSD_P = 64
SSD_N = 128
SSD_G = 8
SSD_CONV_W = 4
ADAM_LR, ADAM_B1, ADAM_B2, ADAM_EPS, ADAM_WD, ADAM_STEP = 0.001, 0.9, 0.999, 1e-08, 0.01, 10

NN = (((1,), (0,)), ((), ()))
NT = (((1,), (1,)), ((), ()))
TN = (((0,), (0,)), ((), ()))


def _cparams(*sem):
    return pltpu.CompilerParams(dimension_semantics=sem, vmem_limit_bytes=VMEM_LIMIT_BYTES)


def _dot(a, b, dims=NN):
    return lax.dot_general(a.astype(BF16), b.astype(BF16), dims, preferred_element_type=F32)


def _split_bf16(x, terms):
    parts, rest = [], x
    for _ in range(terms):
        p = rest.astype(BF16)
        parts.append(p)
        rest = rest - p.astype(F32)
    return parts


def _dot_sel(a, b, dims=NN, *, split, terms=3):
    if split == "a":
        sel = b.astype(BF16)
        return sum(lax.dot_general(p, sel, dims, preferred_element_type=F32) for p in _split_bf16(a, terms))
    sel = a.astype(BF16)
    return sum(lax.dot_general(sel, p, dims, preferred_element_type=F32) for p in _split_bf16(b, terms))


def _sigmoid(x):
    return 1.0 / (1.0 + jnp.exp(-x))


def _colsum(x):
    return jnp.sum(x, axis=0, keepdims=True)


MM_TILE = 1024
MM_FULL_K = 2048


def _mm_tiles(M, N, K):
    tm = min(M, MM_TILE)
    if K <= MM_FULL_K:
        return tm, min(N, MM_TILE), K
    return tm, min(N, 2 * MM_TILE), MM_TILE


def _matmul(a, b, mode, *, out_dtype, name, epi=None, extra=None, after=None, col_parts=None):
    if mode == "nn":
        (M, K), (K2, N) = a.shape, b.shape
    elif mode == "nt":
        (M, K), (N, K2) = a.shape, b.shape
    else:
        (K, M), (K2, N) = a.shape, b.shape
    assert K == K2, (a.shape, b.shape, mode)
    tm, tn, tk = _mm_tiles(M, N, K)
    if col_parts:
        tm, tn = min(M, 2 * MM_TILE), min(tn, MM_TILE)
        while (N // col_parts) % tn:
            tn //= 2
    assert M % tm == 0 and N % tn == 0 and K % tk == 0, (M, N, K, tm, tn, tk)
    nk = K // tk
    if mode == "tn":
        a_spec = pl.BlockSpec((tk, tm), lambda i, j, k: (k, i))
    else:
        a_spec = pl.BlockSpec((tm, tk), lambda i, j, k: (i, k))
    if mode == "nt":
        b_spec = pl.BlockSpec((tn, tk), lambda i, j, k: (j, k))
    else:
        b_spec = pl.BlockSpec((tk, tn), lambda i, j, k: (k, j))
    dims = {"nn": NN, "nt": NT, "tn": TN}[mode]
    o_spec = pl.BlockSpec((tm, tn), lambda i, j, k: (i, j))
    out_dims = (M, N)
    if col_parts:
        per = N // col_parts // tn
        o_spec = pl.BlockSpec((None, tm, tn), lambda i, j, k: (j // per, i, j % per))
        out_dims = (col_parts, M, N // col_parts)
    has_extra = epi in ("drelu2", "add")
    n_out = 2 if epi == "relu2" else 1
    n_in = 2 + int(has_extra) + int(after is not None)

    def body(*refs):
        a_ref, b_ref = refs[0], refs[1]
        e_ref = refs[2] if has_extra else None
        outs = refs[n_in:n_in + n_out]

        def finish(r):
            if epi is None:
                outs[0][...] = r.astype(outs[0].dtype)
            elif epi == "relu2":
                outs[0][...] = r.astype(outs[0].dtype)
                h = jnp.maximum(r, 0.0)
                outs[1][...] = (h * h).astype(outs[1].dtype)
            elif epi == "drelu2":
                av = jnp.maximum(e_ref[...].astype(F32), 0.0)
                outs[0][...] = (r * (2.0 * av)).astype(outs[0].dtype)
            else:
                outs[0][...] = (r + e_ref[...].astype(F32)).astype(outs[0].dtype)

        if nk == 1:
            finish(_dot(a_ref[...], b_ref[...], dims))
            return
        acc = refs[-1]
        k = pl.program_id(2)

        @pl.when(k == 0)
        def _():
            acc[...] = jnp.zeros_like(acc)

        acc[...] += _dot(a_ref[...], b_ref[...], dims)

        @pl.when(k == nk - 1)
        def _():
            finish(acc[...])

    in_specs = [a_spec, b_spec] + ([o_spec] if has_extra else [])
    args = [a, b] + ([extra] if has_extra else [])
    if after is not None:
        in_specs.append(pl.BlockSpec(after.shape, lambda i, j, k: (0, 0)))
        args.append(after)
    out_shape = [jax.ShapeDtypeStruct(out_dims, out_dtype)] * n_out
    res = pl.pallas_call(
        body, name=name, grid=(M // tm, N // tn, nk), in_specs=in_specs, out_specs=[o_spec] * n_out,
        out_shape=out_shape, scratch_shapes=[pltpu.VMEM((tm, tn), F32)] if nk > 1 else [],
        compiler_params=_cparams("parallel", "parallel", "arbitrary"),
    )(*args)
    return res if n_out == 2 else res[0]


def _rstd(x):
    return lax.rsqrt(jnp.mean(x * x, axis=-1, keepdims=True) + RMS_EPS)


def _row_call(body, ins, outs_shape, *, name, rows, tr, acc_outs=()):
    tr = min(tr, rows)
    assert rows % tr == 0
    in_specs = []
    for arr, blocked in ins:
        if blocked:
            in_specs.append(pl.BlockSpec((tr, arr.shape[1]), lambda i: (i, 0)))
        else:
            in_specs.append(pl.BlockSpec(arr.shape, lambda i: (0, 0)))
    out_specs = []
    for n, s in enumerate(outs_shape):
        if n in acc_outs:
            out_specs.append(pl.BlockSpec(s.shape, lambda i: (0, 0)))
        else:
            out_specs.append(pl.BlockSpec((tr, s.shape[1]), lambda i: (i, 0)))
    return pl.pallas_call(
        body, name=name, grid=(rows // tr,), in_specs=in_specs, out_specs=out_specs, out_shape=outs_shape,
        compiler_params=_cparams("arbitrary" if acc_outs else "parallel"),
    )(*[a for a, _ in ins])


def _rms_pre(h, w, *, name):
    T, D = h.shape

    def body(h_ref, w_ref, u_ref):
        x = h_ref[...]
        u_ref[...] = (x * _rstd(x) * w_ref[...]).astype(BF16)

    return _row_call(body, [(h, True), (w, False)], [jax.ShapeDtypeStruct((T, D), BF16)], name=name, rows=T, tr=256)[0]


def _rms_post_pre(h, m, w_post, w_pre, *, name):
    T, D = h.shape

    def body(h_ref, m_ref, wp_ref, wn_ref, hn_ref, u_ref):
        mm = m_ref[...]
        hn = h_ref[...] + mm * _rstd(mm) * wp_ref[...]
        hn_ref[...] = hn
        u_ref[...] = (hn * _rstd(hn) * wn_ref[...]).astype(BF16)

    return _row_call(body, [(h, True), (m, True), (w_post, False), (w_pre, False)],
                     [jax.ShapeDtypeStruct((T, D), F32), jax.ShapeDtypeStruct((T, D), BF16)], name=name, rows=T, tr=256)


def _rms_post_loss(h, m, w_post, tgt, *, name):
    T, D = h.shape

    def body(h_ref, m_ref, wp_ref, t_ref, dh_ref, loss_ref):
        @pl.when(pl.program_id(0) == 0)
        def _():
            loss_ref[...] = jnp.zeros_like(loss_ref)

        mm = m_ref[...]
        err = h_ref[...] + mm * _rstd(mm) * wp_ref[...] - t_ref[...]
        dh_ref[...] = err * (1.0 / D)
        loss_ref[...] += _colsum(jnp.sum(err * err, axis=1, keepdims=True))

    return _row_call(body, [(h, True), (m, True), (w_post, False), (tgt, True)],
                     [jax.ShapeDtypeStruct((T, D), F32), jax.ShapeDtypeStruct((1, 1), F32)],
                     name=name, rows=T, tr=256, acc_outs=(1,))


def _rms_bwd_vals(x, w, dy):
    r = _rstd(x)
    xh = x * r
    g = dy * w
    dx = r * (g - xh * jnp.mean(g * xh, axis=-1, keepdims=True))
    return dx, _colsum(dy * xh)


def _rms_post_bwd(m, w_post, dh, *, name):
    T, D = m.shape

    def body(m_ref, w_ref, dh_ref, dm_ref, dw_ref):
        @pl.when(pl.program_id(0) == 0)
        def _():
            dw_ref[...] = jnp.zeros_like(dw_ref)

        dx, dw = _rms_bwd_vals(m_ref[...], w_ref[...], dh_ref[...])
        dm_ref[...] = dx.astype(BF16)
        dw_ref[...] += dw

    return _row_call(body, [(m, True), (w_post, False), (dh, True)],
                     [jax.ShapeDtypeStruct((T, D), BF16), jax.ShapeDtypeStruct((1, D), F32)],
                     name=name, rows=T, tr=256, acc_outs=(1,))


def _rms_pre_bwd(h, w_pre, du, dh_out, *, name):
    T, D = h.shape

    def body(h_ref, w_ref, du_ref, dho_ref, dh_ref, dw_ref):
        @pl.when(pl.program_id(0) == 0)
        def _():
            dw_ref[...] = jnp.zeros_like(dw_ref)

        dx, dw = _rms_bwd_vals(h_ref[...], w_ref[...], du_ref[...])
        dh_ref[...] = dho_ref[...] + dx
        dw_ref[...] += dw

    return _row_call(body, [(h, True), (w_pre, False), (du, True), (dh_out, True)],
                     [jax.ShapeDtypeStruct((T, D), F32), jax.ShapeDtypeStruct((1, D), F32)],
                     name=name, rows=T, tr=256, acc_outs=(1,))


def _ret_consts(T, C, H):
    lg = np.log1p(-np.exp2(-5.0 - np.arange(H, dtype=np.float64)))
    idx = np.arange(C, dtype=np.float64)
    dist = np.abs(idx[:, None] - idx[None, :])
    vis = (idx[None, :] // REF_CHUNK) <= (idx[:, None] // REF_CHUNK)
    mask = np.exp(dist[None] * lg[:, None, None]) * vis[None]
    xi = np.exp((idx[None, :] + 1.0) * lg[:, None])[..., None]
    zeta = np.exp((C - 1.0 - idx)[None, :] * lg[:, None])[..., None]
    half = RET_DK // 2
    inv_freq = ROPE_BASE ** (-np.arange(half, dtype=np.float32) / np.float32(half))
    ang = np.arange(T, dtype=np.float32)[:, None] * inv_freq[None, :].astype(np.float32)
    return (jnp.asarray(mask, F32), jnp.asarray(xi, F32), jnp.asarray(zeta, F32),
            jnp.asarray(np.cos(ang), F32), jnp.asarray(np.sin(ang), F32))


def _rot(t, cos, sin):
    half = RET_DK // 2
    t1, t2 = t[:, :half], t[:, half:]
    return jnp.concatenate([t1 * cos - t2 * sin, t1 * sin + t2 * cos], axis=1)


def _unrot(d, cos, sin):
    half = RET_DK // 2
    d1, d2 = d[:, :half], d[:, half:]
    return jnp.concatenate([d1 * cos + d2 * sin, d2 * cos - d1 * sin], axis=1)


def _ret_specs(C, H, rev, NS):
    def ci(i):
        return NS - 1 - i if rev else i

    nq = H
    q_spec = pl.BlockSpec((C, RET_DK), lambda h, i: (ci(i), h))
    k_spec = pl.BlockSpec((C, RET_DK), lambda h, i: (ci(i), nq + h))
    v_spec = pl.BlockSpec((C, RET_DV), lambda h, i: (ci(i), H + h))
    g_spec = pl.BlockSpec((C, RET_DV), lambda h, i: (ci(i), 2 * H + h))
    cs_spec = pl.BlockSpec((C, RET_DK // 2), lambda h, i: (ci(i), 0))
    m_spec = pl.BlockSpec((None, C, C), lambda h, i: (h, 0, 0))
    vec_spec = pl.BlockSpec((None, C, 1), lambda h, i: (h, 0, 0))
    gn_spec = pl.BlockSpec((1, RET_DV), lambda h, i: (0, h))
    st_spec = pl.BlockSpec((None, None, RET_DK, RET_DV), lambda h, i: (h, ci(i), 0, 0))
    return q_spec, k_spec, v_spec, g_spec, cs_spec, m_spec, vec_spec, gn_spec, st_spec


def _ret_fwd_vals(q, k, v, cos, sin, mask, xi, s_in):
    qr = _rot(q, cos, sin)
    kr = _rot(k, cos, sin) * (RET_DK ** -0.5)
    a = _dot(qr, kr, NT) * mask
    o = _dot(a, v) + _dot(qr, s_in) * xi
    mu = jnp.mean(o, axis=1, keepdims=True)
    oc = o - mu
    rstd = lax.rsqrt(jnp.mean(oc * oc, axis=1, keepdims=True) + GN_EPS)
    return qr, kr, a, oc * rstd, rstd


def _ret_fwd(proj, gn_w, consts, *, C, name):
    T = proj.shape[0]
    H = gn_w.shape[1] // RET_DV
    NS = T // C
    mask, xi, zeta, cos, sin = consts
    q_spec, k_spec, v_spec, g_spec, cs_spec, m_spec, vec_spec, gn_spec, st_spec = _ret_specs(C, H, False, NS)
    y_spec = pl.BlockSpec((C, RET_DV), lambda h, i: (i, h))

    def body(q_ref, k_ref, v_ref, g_ref, cos_ref, sin_ref, m_ref, xi_ref, ze_ref, gn_ref, y_ref, st_ref, S):
        @pl.when(pl.program_id(1) == 0)
        def _():
            S[...] = jnp.zeros_like(S)

        s_in = S[...]
        st_ref[...] = s_in
        v = v_ref[...]
        xi_v = xi_ref[...]
        qr, kr, a, on, rstd = _ret_fwd_vals(q_ref[...], k_ref[...], v, cos_ref[...], sin_ref[...], m_ref[...], xi_v, s_in)
        g = g_ref[...]
        y_ref[...] = (g * _sigmoid(g) * on * gn_ref[...]).astype(BF16)
        S[...] = s_in * xi_v[C - 1:C, :] + _dot(kr * ze_ref[...], v, TN)

    return pl.pallas_call(
        body, name=name, grid=(H, NS),
        in_specs=[q_spec, k_spec, v_spec, g_spec, cs_spec, cs_spec, m_spec, vec_spec, vec_spec, gn_spec],
        out_specs=[y_spec, st_spec],
        out_shape=[jax.ShapeDtypeStruct((T, H * RET_DV), BF16), jax.ShapeDtypeStruct((H, NS, RET_DK, RET_DV), F32)],
        scratch_shapes=[pltpu.VMEM((RET_DK, RET_DV), F32)],
        compiler_params=_cparams("parallel", "arbitrary"),
    )(proj, proj, proj, proj, cos, sin, mask, xi, zeta, gn_w)


def _stage_out(out_hbm, stage, sems, step, n_steps, row0, pieces, values):
    C = stage.shape[1]
    slot = step % 2

    def copies(sl):
        return [pltpu.make_async_copy(stage.at[sl, :, pl.ds(c0, w)],
                                      out_hbm.at[pl.ds(pl.multiple_of(row0, 16), C), pl.ds(pl.multiple_of(dc, 128), w)],
                                      sems.at[sl, n]) for n, (c0, w, dc) in enumerate(pieces)]

    @pl.when(step >= 2)
    def _():
        for cp in copies(slot):
            cp.wait()

    for (c0, w, _), v in zip(pieces, values):
        stage[slot, :, c0:c0 + w] = v
    for cp in copies(slot):
        cp.start()

    @pl.when(step == n_steps - 1)
    def _():
        for cp in copies(slot):
            cp.wait()
        if n_steps >= 2:
            for cp in copies(1 - slot):
                cp.wait()


def _ret_bwd(proj, gn_w, consts, states, dy, *, C, name):
    T = proj.shape[0]
    H = gn_w.shape[1] // RET_DV
    NS = T // C
    mask, xi, zeta, cos, sin = consts
    q_spec, k_spec, v_spec, g_spec, cs_spec, m_spec, vec_spec, gn_spec, st_spec = _ret_specs(C, H, True, NS)
    dy_spec = pl.BlockSpec((C, RET_DV), lambda h, i: (NS - 1 - i, h))
    scale = RET_DK ** -0.5
    wq, wv = H * RET_DK, H * RET_DV

    def body(q_ref, k_ref, v_ref, g_ref, cos_ref, sin_ref, m_ref, xi_ref, ze_ref, gn_ref, st_ref, dy_ref,
             dproj_ref, dgn_ref, dS, stage, sems):
        @pl.when(pl.program_id(1) == 0)
        def _():
            dS[...] = jnp.zeros_like(dS)
            dgn_ref[...] = jnp.zeros_like(dgn_ref)

        s_in = st_ref[...]
        v = v_ref[...]
        cos, sin, mask, xi_v, ze = cos_ref[...], sin_ref[...], m_ref[...], xi_ref[...], ze_ref[...]
        qr, kr, a, on, rstd = _ret_fwd_vals(q_ref[...], k_ref[...], v, cos, sin, mask, xi_v, s_in)
        g = g_ref[...]
        sg = _sigmoid(g)
        silu = g * sg
        gnw = gn_ref[...]
        dy = dy_ref[...].astype(F32)
        dg = (dy * on * gnw * (sg * (1.0 + g * (1.0 - sg)))).astype(BF16)
        t = dy * silu
        dgn_ref[...] += _colsum(t * on)
        don = t * gnw
        do = rstd * (don - jnp.mean(don, axis=1, keepdims=True) - on * jnp.mean(don * on, axis=1, keepdims=True))
        dox = do * xi_v
        ds_out = dS[...]
        da = _dot(do, v, NT) * mask
        kz = kr * ze
        dv = (_dot(a, do, TN) + _dot(kz, ds_out)).astype(BF16)
        dqr = _dot(da, kr) + _dot(dox, s_in, NT)
        dkr = _dot(da, qr, TN) + _dot(v, ds_out, NT) * ze
        dS[...] = ds_out * xi_v[C - 1:C, :] + _dot(qr, dox, TN)
        dq = _unrot(dqr, cos, sin).astype(BF16)
        dk = _unrot(dkr * scale, cos, sin).astype(BF16)
        h, i = pl.program_id(0), pl.program_id(1)
        pieces = [(0, RET_DK, h * RET_DK), (RET_DK, RET_DK, wq + h * RET_DK),
                  (2 * RET_DK, RET_DV, 2 * wq + h * RET_DV), (2 * RET_DK + RET_DV, RET_DV, 2 * wq + wv + h * RET_DV)]
        _stage_out(dproj_ref, stage, sems, h * NS + i, H * NS, (NS - 1 - i) * C, pieces, [dq, dk, dv, dg])

    return pl.pallas_call(
        body, name=name, grid=(H, NS),
        in_specs=[q_spec, k_spec, v_spec, g_spec, cs_spec, cs_spec, m_spec, vec_spec, vec_spec, gn_spec, st_spec, dy_spec],
        out_specs=[_ANY, gn_spec],
        out_shape=[jax.ShapeDtypeStruct((T, 2 * wq + 2 * wv), BF16), jax.ShapeDtypeStruct((1, H * RET_DV), F32)],
        scratch_shapes=[pltpu.VMEM((RET_DK, RET_DV), F32), pltpu.VMEM((2, C, 2 * RET_DK + 2 * RET_DV), BF16),
                        pltpu.SemaphoreType.DMA((2, 4))],
        compiler_params=_cparams("arbitrary", "arbitrary"),
    )(proj, proj, proj, proj, cos, sin, mask, xi, zeta, gn_w, states, dy)


def _shift_down(x, prev8, k):
    if k == 0:
        return x
    y = pltpu.roll(x, k, 0)
    row = lax.broadcasted_iota(jnp.int32, prev8.shape, 0)
    top = jnp.where(row < k, pltpu.roll(prev8, k, 0), y[:8])
    return jnp.concatenate([top, y[8:]], axis=0)


def _shift_up(x, next8, k):
    if k == 0:
        return x
    n = x.shape[0]
    y = pltpu.roll(x, n - k, 0)
    row = lax.broadcasted_iota(jnp.int32, next8.shape, 0)
    bot = jnp.where(row >= 8 - k, pltpu.roll(next8, 8 - k, 0), y[n - 8:])
    return jnp.concatenate([y[:n - 8], bot], axis=0)


def _conv_silu(raw, halo, w, b):
    cv = b
    for tap in range(SSD_CONV_W):
        cv = cv + _shift_down(raw, halo, SSD_CONV_W - 1 - tap) * w[tap:tap + 1, :]
    sg = _sigmoid(cv)
    return cv * sg, cv, sg


def _conv_silu_bwd(d_post, cv, sg, raw, halo, w, carry8):
    dcv = d_post * (sg * (1.0 + cv * (1.0 - sg)))
    d_raw = jnp.zeros_like(raw)
    dws = []
    for tap in range(SSD_CONV_W):
        k = SSD_CONV_W - 1 - tap
        d_raw = d_raw + _shift_up(dcv, carry8, k) * w[tap:tap + 1, :]
        dws.append(_colsum(dcv * _shift_down(raw, halo, k)))
    return d_raw, jnp.concatenate(dws, axis=0), _colsum(dcv), dcv[:8]


def _softplus(x):
    return jnp.maximum(x, 0.0) + jnp.log1p(jnp.exp(-jnp.abs(x)))


def _ssd_common(C, R, dt, dtT, bias, biasT, alog, alogT, E):
    p = dt + bias
    dtv = _softplus(p)
    a = -jnp.exp(alog)
    da = dtv * a
    daT = _softplus(dtT + biasT) * (-jnp.exp(alogT))
    row = lax.broadcasted_iota(jnp.int32, (C, C), 0)
    col = lax.broadcasted_iota(jnp.int32, (C, C), 1)
    tril = row >= col
    trilf = jnp.where(tril, 1.0, 0.0).astype(F32)
    triuf = jnp.where(col >= row, 1.0, 0.0).astype(F32)
    acum = _dot_sel(trilf, da, split="b")
    acumT = _dot_sel(daT, trilf, NT, split="a")
    al = acum[C - 1:C, :]
    ea = jnp.exp(acum)
    dte = jnp.exp(al - acum)
    eal = jnp.exp(al)
    return dict(p=p, dtv=dtv, a=a, da=da, tril=tril, triuf=triuf, acum=acum, acumT=acumT, al=al, ea=ea, dte=dte, eal=eal,
                dtv_e=_dot_sel(dtv, E, split="a", terms=2), ea_e=_dot_sel(ea, E, split="a", terms=2),
                dte_e=_dot_sel(dte, E, split="a", terms=2), eal_e=_dot_sel(eal, E, split="a"))


def _head_decay(q, r, C, R):
    seg = jnp.broadcast_to(q["acum"][:, r:r + 1], (C, C)) - q["acumT"][r:r + 1, :]
    return jnp.exp(jnp.where(q["tril"], seg, -1e30))


def _ssd_group_specs(C, R, NS, rev):
    RP = R * SSD_P
    G = SSD_G
    nz = 1
    hb = C // 8

    def ci(i):
        return NS - 1 - i if rev else i

    def halo_row(i):
        return jnp.maximum(ci(i) * hb - 1, 0)

    off_b = G * RP // SSD_N
    z_spec = pl.BlockSpec((C, RP), lambda g, i: (ci(i), g))
    x_spec = pl.BlockSpec((C, RP), lambda g, i: (ci(i), G + g))
    b_spec = pl.BlockSpec((C, SSD_N), lambda g, i: (ci(i), 2 * off_b + g))
    c_spec = pl.BlockSpec((C, SSD_N), lambda g, i: (ci(i), 2 * off_b + G + g))
    xh_spec = pl.BlockSpec((8, RP), lambda g, i: (halo_row(i), G + g))
    bh_spec = pl.BlockSpec((8, SSD_N), lambda g, i: (halo_row(i), 2 * off_b + g))
    ch_spec = pl.BlockSpec((8, SSD_N), lambda g, i: (halo_row(i), 2 * off_b + G + g))
    dt_spec = pl.BlockSpec((None, C, R), lambda g, i: (g, ci(i), 0))
    dtT_spec = pl.BlockSpec((None, R, C), lambda g, i: (g, 0, ci(i)))
    pr_spec = pl.BlockSpec((None, 1, R), lambda g, i: (g, 0, 0))
    prT_spec = pl.BlockSpec((None, R, 1), lambda g, i: (g, 0, 0))
    cwx_spec = pl.BlockSpec((SSD_CONV_W, RP), lambda g, i: (0, g))
    cwn_spec = pl.BlockSpec((SSD_CONV_W, SSD_N), lambda g, i: (0, g))
    cbx_spec = pl.BlockSpec((1, RP), lambda g, i: (0, g))
    cbn_spec = pl.BlockSpec((1, SSD_N), lambda g, i: (0, g))
    e_spec = pl.BlockSpec((R, RP), lambda g, i: (0, 0))
    st_spec = pl.BlockSpec((None, None, SSD_N, RP), lambda g, i: (g, ci(i), 0, 0))
    return dict(z=z_spec, x=x_spec, b=b_spec, c=c_spec, xh=xh_spec, bh=bh_spec, ch=ch_spec, dt=dt_spec, dtT=dtT_spec,
                pr=pr_spec, prT=prT_spec, cwx=cwx_spec, cwn=cwn_spec, cbx=cbx_spec, cbn=cbn_spec, e=e_spec, st=st_spec)


def _ssd_forward_vals(C, R, refs, first, s_in):
    E = refs["E"]
    halo_on = jnp.where(first, 0.0, 1.0)
    xh, bh, ch = refs["xh"] * halo_on, refs["bh"] * halo_on, refs["ch"] * halo_on
    xs, cvx, sgx = _conv_silu(refs["x"], xh, refs["cwx"], refs["cbx"])
    bm, cvb, sgb = _conv_silu(refs["b"], bh, refs["cwb"], refs["cbb"])
    cm, cvc, sgc = _conv_silu(refs["c"], ch, refs["cwc"], refs["cbc"])
    q = _ssd_common(C, R, refs["dt"], refs["dtT"], refs["bias"], refs["biasT"], refs["alog"], refs["alogT"], E)
    xdt = xs * q["dtv_e"]
    cb = _dot(cm, bm, NT)
    yoff_raw = _dot(cm, s_in)
    xdt_b = xdt.astype(BF16)
    low = lax.broadcasted_iota(jnp.int32, (1, 2 * SSD_P), 1) < SSD_P
    pairs = []
    for j in range(R // 2):
        xp = xdt_b[:, 2 * SSD_P * j:2 * SSD_P * (j + 1)]
        y0 = _dot(cb * _head_decay(q, 2 * j, C, R), xp)
        y1 = _dot(cb * _head_decay(q, 2 * j + 1, C, R), xp)
        pairs.append(jnp.where(low, y0, y1))
    ydiag = jnp.concatenate(pairs, axis=1)
    d_e =_dot_sel(refs["dskip"], E, split="a")
    y = ydiag + yoff_raw * q["ea_e"] + d_e * xs
    xd = xdt * q["dte_e"]
    s_out = s_in * q["eal_e"] + _dot(bm, xd, TN)
    z = refs["z"]
    sgz = _sigmoid(z)
    yz = y * (z * sgz)
    rn = lax.rsqrt(jnp.mean(yz * yz, axis=1, keepdims=True) + RMS_EPS)
    return dict(q=q, xh=xh, bh=bh, ch=ch, xs=xs, cvx=cvx, sgx=sgx, bm=bm, cvb=cvb, sgb=sgb, cm=cm, cvc=cvc, sgc=sgc,
                xdt=xdt, cb=cb, yoff_raw=yoff_raw, d_e=d_e, y=y, xd=xd, s_out=s_out, z=z, sgz=sgz, yz=yz, rn=rn)


_SSD_IN_NAMES = ("z", "x", "b", "c", "xh", "bh", "ch", "dt", "dtT", "bias", "biasT", "alog", "alogT", "dskip",
                 "cwx", "cwb", "cwc", "cbx", "cbb", "cbc", "nw", "E")


def _ssd_inputs(pm, dt_g, dtT_g, prm, sp):
    bias, biasT, alog, alogT, dskip, cwx, cwb, cwc, cbx, cbb, cbc, nw, E = prm
    args = [pm, pm, pm, pm, pm, pm, pm, dt_g, dtT_g, bias, biasT, alog, alogT, dskip, cwx, cwb, cwc, cbx, cbb, cbc, nw, E]
    specs = [sp["z"], sp["x"], sp["b"], sp["c"], sp["xh"], sp["bh"], sp["ch"], sp["dt"], sp["dtT"], sp["pr"], sp["prT"],
             sp["pr"], sp["prT"], sp["pr"], sp["cwx"], sp["cwn"], sp["cwn"], sp["cbx"], sp["cbn"], sp["cbn"], sp["cbx"], sp["e"]]
    return args, specs


def _ssd_fwd(pm, dt_g, dtT_g, prm, *, C, R, name):
    T = pm.shape[0]
    NS = T // C
    RP = R * SSD_P
    G = SSD_G
    sp = _ssd_group_specs(C, R, NS, False)
    args, specs = _ssd_inputs(pm, dt_g, dtT_g, prm, sp)
    nin = len(args)

    def body(*refs):
        ins = {n: r[...] for n, r in zip(_SSD_IN_NAMES, refs[:nin])}
        y_ref, st_ref, S = refs[nin:]
        first = pl.program_id(1) == 0

        @pl.when(first)
        def _():
            S[...] = jnp.zeros_like(S)

        s_in = S[...]
        st_ref[...] = s_in
        f = _ssd_forward_vals(C, R, ins, first, s_in)
        y_ref[...] = (f["yz"] * f["rn"] * ins["nw"]).astype(BF16)
        S[...] = f["s_out"]

    return pl.pallas_call(
        body, name=name, grid=(G, NS), in_specs=specs,
        out_specs=[pl.BlockSpec((C, RP), lambda g, i: (i, g)), sp["st"]],
        out_shape=[jax.ShapeDtypeStruct((T, G * RP), BF16), jax.ShapeDtypeStruct((G, NS, SSD_N, RP), F32)],
        scratch_shapes=[pltpu.VMEM((SSD_N, RP), F32)],
        compiler_params=_cparams("parallel", "arbitrary"),
    )(*args)


def _ssd_bwd(pm, dt_g, dtT_g, prm, states, dout, *, C, R, name):
    T = pm.shape[0]
    NS = T // C
    RP = R * SSD_P
    G = SSD_G
    sp = _ssd_group_specs(C, R, NS, True)
    args, specs = _ssd_inputs(pm, dt_g, dtT_g, prm, sp)
    nin = len(args)
    rows_spec = pl.BlockSpec((C, RP), lambda g, i: (NS - 1 - i, g))
    rown_spec = pl.BlockSpec((C, SSD_N), lambda g, i: (NS - 1 - i, g))
    args = args + [states, dout]
    specs = specs + [sp["st"], rows_spec]

    def body(*refs):
        ins = {n: r[...] for n, r in zip(_SSD_IN_NAMES, refs[:nin])}
        st_ref, dout_ref = refs[nin], refs[nin + 1]
        (dpm_ref, ddt_ref, dbias_ref, dalog_ref, dd_ref, dcwx_ref, dcwb_ref, dcwc_ref,
         dcbx_ref, dcbb_ref, dcbc_ref, dnw_ref) = refs[nin + 2:nin + 14]
        dS, cx8, cb8, cc8, stage, sems = refs[nin + 14:]
        acc_refs = (dbias_ref, dalog_ref, dd_ref, dcwx_ref, dcwb_ref, dcwc_ref, dcbx_ref, dcbb_ref, dcbc_ref, dnw_ref)
        step = pl.program_id(1)

        @pl.when(step == 0)
        def _():
            for r_ in acc_refs + (dS, cx8, cb8, cc8):
                r_[...] = jnp.zeros_like(r_)

        first = step == NS - 1
        E = ins["E"]
        s_in = st_ref[...]
        f = _ssd_forward_vals(C, R, ins, first, s_in)
        q = f["q"]
        xs, bm, cm, xdt, cb, y, z, sgz, yz, rn = (f[n] for n in ("xs", "bm", "cm", "xdt", "cb", "y", "z", "sgz", "yz", "rn"))
        nw = ins["nw"]
        dout = dout_ref[...].astype(F32)
        yh = yz * rn
        dnw_ref[...] += _colsum(dout * yh)
        g1 = dout * nw
        dyz = rn * (g1 - yh * jnp.mean(g1 * yh, axis=1, keepdims=True))
        dz = (dyz * y * (sgz * (1.0 + z * (1.0 - sgz)))).astype(BF16)
        dy = dyz * (z * sgz)
        dd_ref[...] += _dot_sel(_colsum(dy * xs), E, NT, split="a")
        dxs = dy * f["d_e"]
        dyo = dy * q["ea_e"]
        dcm = _dot(dyo, s_in, NT)
        ds_acc = _dot(cm, dyo, TN)
        dacum = _dot_sel(dy * f["yoff_raw"], E, NT, split="a", terms=2) * q["ea"]
        dacumT = jnp.zeros((R, C), F32)
        dcb = jnp.zeros((C, C), F32)
        rowR = lax.broadcasted_iota(jnp.int32, (1, R), 1)
        rowRT = lax.broadcasted_iota(jnp.int32, (R, 1), 0)
        dy_b, xdt_b = dy.astype(BF16), xdt.astype(BF16)
        low = lax.broadcasted_iota(jnp.int32, (1, 2 * SSD_P), 1) < SSD_P
        dxdt_pairs = []
        for j in range(R // 2):
            lanes = slice(2 * SSD_P * j, 2 * SSD_P * (j + 1))
            dyp, xp = dy_b[:, lanes], xdt_b[:, lanes]
            halves = []
            for r, mine in ((2 * j, low), (2 * j + 1, jnp.logical_not(low))):
                lr = _head_decay(q, r, C, R)
                w_r = cb * lr
                dw = _dot(jnp.where(mine, dyp, jnp.zeros_like(dyp)), xp, NT)
                halves.append(_dot(w_r, dyp, TN))
                dcb = dcb + dw * lr
                dseg = dw * w_r
                dacum = dacum + jnp.sum(dseg, axis=1, keepdims=True) * jnp.where(rowR == r, 1.0, 0.0)
                dacumT = dacumT - _colsum(dseg) * jnp.where(rowRT == r, 1.0, 0.0)
            dxdt_pairs.append(jnp.where(low, halves[0], halves[1]))
        dxdt = jnp.concatenate(dxdt_pairs, axis=1)
        dsn = dS[...]
        ds_acc = ds_acc + dsn * q["eal_e"]
        d_eal = _dot_sel(_colsum(dsn * s_in), E, NT, split="a")
        dbm = _dot(f["xd"], dsn, NT)
        dxd = _dot(bm, dsn)
        dxdt = dxdt + dxd * q["dte_e"]
        d_dte = _dot_sel(dxd * xdt, E, NT, split="a", terms=2) * q["dte"]
        d_al = _colsum(d_dte) + d_eal * q["eal"]
        dacum = dacum - d_dte
        rowC = lax.broadcasted_iota(jnp.int32, (C, 1), 0)
        dacum = dacum + jnp.where(rowC == C - 1, 1.0, 0.0) * d_al
        dS[...] = ds_acc
        dcm = dcm + _dot(dcb, bm)
        dbm = dbm + _dot(dcb, cm, TN)
        eye = jnp.where(lax.broadcasted_iota(jnp.int32, (C, C), 0) == lax.broadcasted_iota(jnp.int32, (C, C), 1), 1.0, 0.0)
        dacum = dacum + _dot_sel(eye, dacumT, NT, split="b")
        dda = _dot_sel(q["triuf"], dacum, split="b")
        ddtv = dda * q["a"] + _dot_sel(dxdt * xs, E, NT, split="a", terms=2)
        dalog_ref[...] += _colsum(dda * q["dtv"]) * q["a"]
        dxs = dxs + dxdt * q["dtv_e"]
        dp = ddtv * _sigmoid(q["p"])
        ddt_ref[...] = dp
        dbias_ref[...] += _colsum(dp)
        d_raw, d_w, d_b, c8 = _conv_silu_bwd(dxs, f["cvx"], f["sgx"], ins["x"], f["xh"], ins["cwx"], cx8[...])
        dx = d_raw.astype(BF16)
        dcwx_ref[...] += d_w
        dcbx_ref[...] += d_b
        cx8[...] = c8
        d_raw, d_w, d_b, c8 = _conv_silu_bwd(dbm, f["cvb"], f["sgb"], ins["b"], f["bh"], ins["cwb"], cb8[...])
        db = d_raw.astype(BF16)
        dcwb_ref[...] += d_w
        dcbb_ref[...] += d_b
        cb8[...] = c8
        d_raw, d_w, d_b, c8 = _conv_silu_bwd(dcm, f["cvc"], f["sgc"], ins["c"], f["ch"], ins["cwc"], cc8[...])
        dc = d_raw.astype(BF16)
        dcwc_ref[...] += d_w
        dcbc_ref[...] += d_b
        cc8[...] = c8
        g_ = pl.program_id(0)
        pieces = [(0, RP, g_ * RP), (RP, RP, G * RP + g_ * RP), (2 * RP, SSD_N, 2 * G * RP + g_ * SSD_N),
                  (2 * RP + SSD_N, SSD_N, 2 * G * RP + G * SSD_N + g_ * SSD_N)]
        _stage_out(dpm_ref, stage, sems, g_ * NS + step, G * NS, (NS - 1 - step) * C, pieces, [dz, dx, db, dc])

    out_specs = [_ANY, pl.BlockSpec((None, C, R), lambda g, i: (g, NS - 1 - i, 0)),
                 sp["pr"], sp["pr"], sp["pr"], sp["cwx"], sp["cwn"], sp["cwn"], sp["cbx"], sp["cbn"], sp["cbn"], sp["cbx"]]
    out_shape = [jax.ShapeDtypeStruct((T, 2 * G * RP + 2 * G * SSD_N), BF16),
                 jax.ShapeDtypeStruct((G, T, R), F32),
                 jax.ShapeDtypeStruct((G, 1, R), F32), jax.ShapeDtypeStruct((G, 1, R), F32), jax.ShapeDtypeStruct((G, 1, R), F32),
                 jax.ShapeDtypeStruct((SSD_CONV_W, G * RP), F32), jax.ShapeDtypeStruct((SSD_CONV_W, G * SSD_N), F32),
                 jax.ShapeDtypeStruct((SSD_CONV_W, G * SSD_N), F32),
                 jax.ShapeDtypeStruct((1, G * RP), F32), jax.ShapeDtypeStruct((1, G * SSD_N), F32),
                 jax.ShapeDtypeStruct((1, G * SSD_N), F32), jax.ShapeDtypeStruct((1, G * RP), F32)]
    return pl.pallas_call(
        body, name=name, grid=(G, NS), in_specs=specs, out_specs=out_specs, out_shape=out_shape,
        scratch_shapes=[pltpu.VMEM((SSD_N, RP), F32), pltpu.VMEM((8, RP), F32), pltpu.VMEM((8, SSD_N), F32),
                        pltpu.VMEM((8, SSD_N), F32), pltpu.VMEM((2, C, 2 * RP + 2 * SSD_N), BF16),
                        pltpu.SemaphoreType.DMA((2, 4))],
        compiler_params=_cparams("arbitrary", "arbitrary"),
    )(*args)


_ANY = pl.BlockSpec(memory_space=pl.ANY)


def _chip_peer(k):
    x, y, c = lax.axis_index("x"), lax.axis_index("y"), lax.axis_index("c")
    return (x ^ (k >> 1), y ^ (k & 1), c)


def _my_chip():
    return 2 * lax.axis_index("x") + lax.axis_index("y")


def _all_gather_chips(shards, halved, *, name):
    n = len(shards)

    def body(*refs):
        ins, outs = refs[:n], refs[n:2 * n]
        send, recv, fsend, frecv, loc = refs[2 * n:]
        s = _my_chip()
        c = lax.axis_index("c")
        sibling = (lax.axis_index("x"), lax.axis_index("y"), 1 - c)
        copies = []
        for a in range(n):
            cp = pltpu.make_async_copy(ins[a], outs[a].at[s], loc.at[a])
            cp.start()
            copies.append(cp)

        def rows(a, core):
            if not halved[a]:
                return slice(None)
            half = shards[a].shape[0] // 2
            return pl.ds(pl.multiple_of(core * half, 16), half)

        def over_ici(a, k, slot, core):
            return pltpu.make_async_remote_copy(
                src_ref=ins[a].at[rows(a, core)], dst_ref=outs[a].at[slot, rows(a, core)],
                send_sem=send.at[3 * a + k - 1], recv_sem=recv.at[3 * a + k - 1],
                device_id=_chip_peer(k), device_id_type=MESH_ID)

        def over_d2d(a, k, core):
            z = outs[a].at[s ^ k, rows(a, core)]
            return pltpu.make_async_remote_copy(
                src_ref=z, dst_ref=z, send_sem=fsend.at[3 * a + k - 1], recv_sem=frecv.at[3 * a + k - 1],
                device_id=sibling, device_id_type=MESH_ID)

        sent = []
        for a in range(n):
            for k in (1, 2, 3):
                cp = over_ici(a, k, s, c)
                cp.start()
                sent.append(cp)
        passed = []
        for a in range(n):
            for k in (1, 2, 3):
                over_ici(a, k, s ^ k, c).wait_recv()
                if halved[a]:
                    cp = over_d2d(a, k, c)
                    cp.start()
                    passed.append(cp)
        for a in range(n):
            if halved[a]:
                for k in (1, 2, 3):
                    over_d2d(a, k, 1 - c).wait_recv()
        for cp in sent + passed:
            cp.wait_send()
        for cp in copies:
            cp.wait()

    for a, h in zip(shards, halved):
        assert not h or a.shape[0] % 32 == 0, a.shape
    return pl.pallas_call(
        body, name=name, in_specs=[_ANY] * n, out_specs=[_ANY] * n,
        out_shape=[jax.ShapeDtypeStruct((4,) + a.shape, a.dtype) for a in shards],
        scratch_shapes=[pltpu.SemaphoreType.DMA((3 * n,))] * 4 + [pltpu.SemaphoreType.DMA((n,))],
        compiler_params=pltpu.CompilerParams(has_side_effects=True),
    )(*shards)


_HBM = pl.BlockSpec(memory_space=pltpu.HBM)
_SEM = pl.BlockSpec(memory_space=pltpu.SEMAPHORE)
_EFFECT = pltpu.SideEffectType.DATAFLOW_SIDE_EFFECTING


def _split_copies(src, land, send, recv, loc, a, scatter):
    s = _my_chip()
    mine = pltpu.make_async_copy(src.at[s] if scatter else src, land.at[s], loc.at[a])
    pairs = []
    for k in (1, 2, 3):
        sems = dict(send_sem=send.at[3 * a + k - 1], recv_sem=recv.at[3 * a + k - 1],
                    device_id=_chip_peer(k), device_id_type=MESH_ID)
        out = pltpu.make_async_remote_copy(src_ref=src.at[s ^ k] if scatter else src, dst_ref=land.at[s], **sems)
        arriving = pltpu.make_async_remote_copy(src_ref=src.at[s ^ k] if scatter else src, dst_ref=land.at[s ^ k], **sems)
        pairs.append((out, arriving))
    return mine, pairs


def _split_start(arrs, *, scatter, after, name):
    n = len(arrs)
    zones = [lax.empty(a.shape if scatter else (4,) + a.shape, a.dtype) for a in arrs]

    def body(*refs):
        srcs, lands = refs[:n], refs[n:2 * n]
        send, recv, loc = refs[2 * n + 1:2 * n + 4]
        token = refs[-1]
        for a in range(n):
            mine, pairs = _split_copies(srcs[a], lands[a], send, recv, loc, a, scatter)
            mine.start()
            for out, _ in pairs:
                out.start()
        token[...] = jnp.zeros_like(token)

    res = pl.pallas_call(
        body, name=name,
        out_shape=(pltpu.SemaphoreType.DMA((3 * n,)), pltpu.SemaphoreType.DMA((3 * n,)), pltpu.SemaphoreType.DMA((n,)),
                   *[pltpu.HBM(z.shape, z.dtype) for z in zones], jax.ShapeDtypeStruct((8, 128), F32)),
        in_specs=[_ANY] * n + [_HBM] * n + [_ANY],
        out_specs=(_SEM, _SEM, _SEM, *([_HBM] * n), pl.BlockSpec(memory_space=pltpu.VMEM)),
        input_output_aliases={n + i: 3 + i for i in range(n)},
        compiler_params=pltpu.CompilerParams(has_side_effects=_EFFECT),
    )(*arrs, *[pltpu.with_memory_space_constraint(z, pltpu.HBM) for z in zones], after)
    return res[:3], list(res[3:3 + n]), res[-1]


def _split_wait(sems, src, land, a, *, scatter, after, name):
    def body(src_ref, land_ref, send, recv, loc, after_ref, land_out):
        mine, pairs = _split_copies(src_ref, land_ref, send, recv, loc, a, scatter)
        mine.wait()
        for out, arriving in pairs:
            out.wait_send()
            arriving.wait_recv()

    return pl.pallas_call(
        body, name=name, out_shape=pltpu.HBM(land.shape, land.dtype),
        in_specs=[_ANY, _HBM, _SEM, _SEM, _SEM, _ANY], out_specs=_HBM, input_output_aliases={1: 0},
        compiler_params=pltpu.CompilerParams(has_side_effects=_EFFECT),
    )(src, land, *sems, after)


def _sibling_copies(srcs, lands, send, recv):
    sib = (lax.axis_index("x"), lax.axis_index("y"), 1 - lax.axis_index("c"))
    return [pltpu.make_async_remote_copy(src_ref=srcs[a], dst_ref=lands[a], send_sem=send.at[a], recv_sem=recv.at[a],
                                         device_id=sib, device_id_type=MESH_ID) for a in range(len(srcs))]


def _swap_start(arrs, *, after, name):
    n = len(arrs)
    zones = [lax.empty(a.shape, a.dtype) for a in arrs]

    def body(*refs):
        for cp in _sibling_copies(refs[:n], refs[n:2 * n], refs[2 * n + 1], refs[2 * n + 2]):
            cp.start()

    res = pl.pallas_call(
        body, name=name,
        out_shape=(pltpu.SemaphoreType.DMA((n,)), pltpu.SemaphoreType.DMA((n,)), *[pltpu.HBM(z.shape, z.dtype) for z in zones]),
        in_specs=[_ANY] * n + [_HBM] * n + [_ANY], out_specs=(_SEM, _SEM, *([_HBM] * n)),
        input_output_aliases={n + i: 2 + i for i in range(n)},
        compiler_params=pltpu.CompilerParams(has_side_effects=_EFFECT),
    )(*arrs, *[pltpu.with_memory_space_constraint(z, pltpu.HBM) for z in zones], after)
    return res[:2], list(res[2:])


def _swap_wait(sems, srcs, lands, *, after, name):
    n = len(srcs)

    def body(*refs):
        for cp in _sibling_copies(refs[:n], refs[n:2 * n], refs[2 * n], refs[2 * n + 1]):
            cp.wait_send()
            cp.wait_recv()

    res = pl.pallas_call(
        body, name=name, out_shape=tuple(pltpu.HBM(a.shape, a.dtype) for a in lands),
        in_specs=[_ANY] * n + [_HBM] * n + [_SEM, _SEM, _ANY], out_specs=tuple([_HBM] * n),
        input_output_aliases={n + i: i for i in range(n)},
        compiler_params=pltpu.CompilerParams(has_side_effects=_EFFECT),
    )(*srcs, *lands, *sems, after)
    return list(res)


def _all_gather_devices(v, *, name):
    r = v.shape[0]

    def body(v_ref, out_ref, send, recv):
        x, y, c = lax.axis_index("x"), lax.axis_index("y"), lax.axis_index("c")
        me = 4 * x + 2 * y + c
        out_ref[me] = v_ref[...]
        cps = []
        for k in range(1, 8):
            peer = (x ^ (k >> 2), y ^ ((k >> 1) & 1), c ^ (k & 1))
            cp = pltpu.make_async_remote_copy(src_ref=v_ref, dst_ref=out_ref.at[me], send_sem=send.at[k - 1],
                                              recv_sem=recv.at[k - 1], device_id=peer, device_id_type=MESH_ID)
            cp.start()
            cps.append(cp)
        for k, cp in enumerate(cps, start=1):
            cp.wait_send()
            pltpu.make_async_remote_copy(src_ref=v_ref, dst_ref=out_ref.at[me ^ k], send_sem=send.at[k - 1],
                                         recv_sem=recv.at[k - 1], device_id=(x, y, c), device_id_type=MESH_ID).wait_recv()

    vm = pl.BlockSpec(memory_space=pltpu.VMEM)
    return pl.pallas_call(
        body, name=name, in_specs=[vm], out_specs=vm, out_shape=jax.ShapeDtypeStruct((8, r, 128), F32),
        scratch_shapes=[pltpu.SemaphoreType.DMA((7,)), pltpu.SemaphoreType.DMA((7,))],
        compiler_params=pltpu.CompilerParams(has_side_effects=True),
    )(v)


def _row_tile(r, target):
    best = None
    for t in range(16, min(target, r) + 1, 16):
        if r % t == 0:
            best = t
    return best or r


def _sum_slots(buf, *, name, tr=384, layer=None, stack=None):
    S, r, c = buf.shape
    tr = _row_tile(r, tr)

    def body(*refs):
        b_ref, o_ref = refs[0], refs[-1]
        acc = b_ref[0].astype(F32)
        for j in range(1, S):
            acc = acc + b_ref[j].astype(F32)
        o_ref[...] = acc

    in_specs = [pl.BlockSpec((S, tr, c), lambda i: (0, i, 0))]
    args, alias = [buf], {}
    if layer is None:
        out_spec, out_shape = pl.BlockSpec((tr, c), lambda i: (i, 0)), jax.ShapeDtypeStruct((r, c), F32)
    else:
        out_spec = pl.BlockSpec((None, tr, c), lambda i: (layer, i, 0))
        out_shape = jax.ShapeDtypeStruct((2, r, c), F32)
        if stack is not None:
            in_specs.append(_ANY)
            args.append(stack)
            alias = {1: 0}
    return pl.pallas_call(
        body, name=name, grid=(r // tr,), in_specs=in_specs, out_specs=out_spec, out_shape=out_shape,
        input_output_aliases=alias, compiler_params=_cparams("parallel"),
    )(*args)


def _add2(a, b, *, name, tr=384):
    r, c = a.shape
    tr = _row_tile(r, tr)

    def body(a_ref, b_ref, o_ref):
        o_ref[...] = a_ref[...] + b_ref[...]

    spec = pl.BlockSpec((tr, c), lambda i: (i, 0))
    return pl.pallas_call(body, name=name, grid=(r // tr,), in_specs=[spec, spec], out_specs=spec,
                          out_shape=jax.ShapeDtypeStruct((r, c), F32), compiler_params=_cparams("parallel"))(a, b)


ADAMW_BLOCK_ELEMS = 1 << 18


def _adamw(w, gs, m, v, *, name, tr=256):
    r, c = w.shape
    tr = _row_tile(r, min(tr, max(16, ADAMW_BLOCK_ELEMS // c)))
    bc1 = 1.0 - ADAM_B1 ** ADAM_STEP
    bc2 = 1.0 - ADAM_B2 ** ADAM_STEP
    ng = len(gs)

    def body(*refs):
        w_ref, m_ref, v_ref = refs[0], refs[1 + ng], refs[2 + ng]
        g_ref, d_ref, mo_ref, vo_ref = refs[3 + ng:]
        gg = refs[1][...] if ng == 1 else refs[1][...] + refs[2][...]
        mn = ADAM_B1 * m_ref[...] + (1.0 - ADAM_B1) * gg
        vn = ADAM_B2 * v_ref[...] + (1.0 - ADAM_B2) * (gg * gg)
        g_ref[...] = gg
        mo_ref[...] = mn
        vo_ref[...] = vn
        d_ref[...] = -ADAM_LR * ((mn / bc1) / (jnp.sqrt(vn / bc2) + ADAM_EPS) + ADAM_WD * w_ref[...])

    spec = pl.BlockSpec((tr, c), lambda i: (i, 0))
    return pl.pallas_call(body, name=name, grid=(r // tr,), in_specs=[spec] * (3 + ng), out_specs=[spec] * 4,
                          out_shape=[jax.ShapeDtypeStruct((r, c), F32)] * 4,
                          compiler_params=_cparams("parallel"))(w, *gs, m, v)


def _pack(vecs, rows):
    flat = jnp.concatenate([v.reshape(-1).astype(F32) for v in vecs])
    return jnp.pad(flat, (0, rows * 128 - flat.shape[0])).reshape(rows, 128)


def _unpack(packed, shapes):
    flat = packed.reshape(-1)
    out, off = [], 0
    for s in shapes:
        n = math.prod(s)
        out.append(flat[off:off + n].reshape(s))
        off += n
    return out


def _pack_rows(shapes):
    n = sum(math.prod(s) for s in shapes)
    return -(-n // 1024) * 8


def kernel(x, norm_mix_pre, norm_mix_post, norm_ffn_pre, norm_ffn_post, ret_w_in, ret_gn_w, ret_w_out, ssd_w_in, ssd_conv_w, ssd_conv_b, ssd_dt_bias, ssd_a_log, ssd_d, ssd_norm_w, ssd_w_out, mlp_w_up, mlp_w_down, loss_target, m_norm_mix_pre, m_norm_mix_post, m_norm_ffn_pre, m_norm_ffn_post, m_ret_w_in, m_ret_gn_w, m_ret_w_out, m_ssd_w_in, m_ssd_conv_w, m_ssd_conv_b, m_ssd_dt_bias, m_ssd_a_log, m_ssd_d, m_ssd_norm_w, m_ssd_w_out, m_mlp_w_up, m_mlp_w_down, v_norm_mix_pre, v_norm_mix_post, v_norm_ffn_pre, v_norm_ffn_post, v_ret_w_in, v_ret_gn_w, v_ret_w_out, v_ssd_w_in, v_ssd_conv_w, v_ssd_conv_b, v_ssd_dt_bias, v_ssd_a_log, v_ssd_d, v_ssd_norm_w, v_ssd_w_out, v_mlp_w_up, v_mlp_w_down):
    T, D = x.shape[1], x.shape[2]
    H = D // RET_DK
    d_inner = 2 * D
    R = d_inner // SSD_P // SSD_G
    RP = R * SSD_P
    n_heads = SSD_G * R
    conv_dim = d_inner + 2 * SSD_G * SSD_N
    n_main = d_inner + conv_dim
    C = min(256, T)
    chip = _my_chip()
    xs, tgt = x[0], loss_target[0]

    conv_sh = ssd_conv_w.shape[2]
    small_shapes = [(SSD_CONV_W, conv_sh), (conv_sh,), (ssd_norm_w.shape[1],)]
    small_rows = _pack_rows(small_shapes)
    shards = [ret_w_in[0].T.astype(BF16), ret_w_out[0].astype(BF16), ssd_w_in[0].T.astype(BF16), ssd_w_out[0].astype(BF16),
              mlp_w_up[0].T.astype(BF16), mlp_w_up[1].T.astype(BF16), mlp_w_down[0].astype(BF16), mlp_w_down[1].astype(BF16)]
    (ret_in_g, small_g) = _all_gather_chips([shards[0], _pack([ssd_conv_w[0], ssd_conv_b[0], ssd_norm_w[0]], small_rows)],
                                            [True, False], name="gather_first")

    def full(g):
        return g.reshape(4 * g.shape[1], g.shape[2])

    def start_gather(idx, after, name):
        sems, zones, tok = _split_start([shards[i] for i in idx], scatter=False, after=after, name=name)
        return {i: (sems, shards[i], zones[n], n) for n, i in enumerate(idx)}, tok

    def arrived(stage, i, after, name):
        sems, src, zone, n = stage[i]
        return full(_split_wait(sems, src, zone, n, scatter=False, after=after, name=name))

    ret_in_t = full(ret_in_g)
    sm = [_unpack(small_g[j], small_shapes) for j in range(4)]
    conv_w = jnp.concatenate([sm[j][0] for j in range(4)], axis=1)
    conv_b = jnp.concatenate([sm[j][1] for j in range(4)])[None, :]
    norm_w = jnp.concatenate([sm[j][2] for j in range(4)])[None, :]

    gb = SSD_G * SSD_N
    ssd_prm = (ssd_dt_bias.reshape(SSD_G, 1, R), ssd_dt_bias.reshape(SSD_G, R, 1),
               ssd_a_log.reshape(SSD_G, 1, R), ssd_a_log.reshape(SSD_G, R, 1), ssd_d.reshape(SSD_G, 1, R),
               conv_w[:, :d_inner], conv_w[:, d_inner:d_inner + gb], conv_w[:, d_inner + gb:],
               conv_b[:, :d_inner], conv_b[:, d_inner:d_inner + gb], conv_b[:, d_inner + gb:],
               norm_w, jnp.asarray(np.kron(np.eye(R), np.ones((1, SSD_P))), F32))
    ret_consts = _ret_consts(T, C, H)

    u0 = _rms_pre(xs, norm_mix_pre[0:1], name="pre0")
    stage1, tok = start_gather((1, 4), ret_in_g, "gather_start1")
    proj = _matmul(u0, ret_in_t, "nt", out_dtype=F32, name="ret_in", after=tok)
    stage2, tok = start_gather((6, 2), proj, "gather_start2")
    y_ret, st_ret = _ret_fwd(proj, ret_gn_w, ret_consts, C=C, name="ret_fwd")
    ret_out = arrived(stage1, 1, y_ret, "gather_wait_ret_out")
    m0 = _matmul(y_ret, ret_out, "nn", out_dtype=F32, name="ret_out", after=tok)
    h1, u1 = _rms_post_pre(xs, m0, norm_mix_post[0:1], norm_ffn_pre[0:1], name="post_pre1")
    up_t0 = arrived(stage1, 4, u1, "gather_wait_up0")
    a0, hh0 = _matmul(u1, up_t0, "nt", out_dtype=BF16, name="mlp_up0", epi="relu2")
    stage3, tok = start_gather((3, 5, 7), hh0, "gather_start3")
    down0 = arrived(stage2, 6, hh0, "gather_wait_down0")
    f0 = _matmul(hh0, down0, "nn", out_dtype=F32, name="mlp_down0", after=tok)
    h2, u2 = _rms_post_pre(h1, f0, norm_ffn_post[0:1], norm_mix_pre[1:2], name="post_pre2")
    ssd_in_t = arrived(stage2, 2, u2, "gather_wait_ssd_in")
    ssd_main_t, ssd_dt_t = ssd_in_t[:n_main], ssd_in_t[n_main:]
    pm = _matmul(u2, ssd_main_t, "nt", out_dtype=F32, name="ssd_in")
    pdt = _matmul(u2, ssd_dt_t, "nt", out_dtype=F32, name="ssd_in_dt")
    dt_g = pdt.reshape(T, SSD_G, R).transpose(1, 0, 2)
    dtT_g = pdt.reshape(T, SSD_G, R).transpose(1, 2, 0)
    y_ssd, st_ssd = _ssd_fwd(pm, dt_g, dtT_g, ssd_prm, C=C, R=R, name="ssd_fwd")
    ssd_out = arrived(stage3, 3, y_ssd, "gather_wait_ssd_out")
    m1 = _matmul(y_ssd, ssd_out, "nn", out_dtype=F32, name="ssd_out")
    h3, u3 = _rms_post_pre(h2, m1, norm_mix_post[1:2], norm_ffn_pre[1:2], name="post_pre3")
    up_t1 = arrived(stage3, 5, u3, "gather_wait_up1")
    a1, hh1 = _matmul(u3, up_t1, "nt", out_dtype=BF16, name="mlp_up1", epi="relu2")
    down1 = arrived(stage3, 7, hh1, "gather_wait_down1")
    f1 = _matmul(hh1, down1, "nn", out_dtype=F32, name="mlp_down1")
    up_t, down = (up_t0, up_t1), (down0, down1)
    dh4, sq = _rms_post_loss(h3, f1, norm_ffn_post[1:2], tgt, name="post_loss")
    loss = lax.psum(sq[0, 0], MESH_AXES) * (0.5 / D)

    in_flight = []

    def send_grad(g, name):
        part = g if g.ndim == 3 else g.reshape(4, g.shape[0] // 4, g.shape[1])
        sems, zones, tok = _split_start([part], scatter=True, after=part, name=f"scatter_start_{name}")
        in_flight.append((name, sems, part, zones[0]))
        return tok

    def mlp_bwd(i, dh_out, h_in, u, a, hh, f):
        df, d_post = _rms_post_bwd(f, norm_ffn_post[i:i + 1], dh_out, name=f"post_bwd_ffn{i}")
        tok = send_grad(_matmul(hh, df, "tn", out_dtype=BF16, name=f"mlp_down_wg{i}"), f"down{i}")
        da = _matmul(df, down[i], "nt", out_dtype=BF16, name=f"mlp_down_dg{i}", epi="drelu2", extra=a, after=tok)
        tok = send_grad(_matmul(u, da, "tn", out_dtype=BF16, name=f"mlp_up_wg{i}", col_parts=4), f"up{i}")
        du = _matmul(da, up_t[i], "nn", out_dtype=F32, name=f"mlp_up_dg{i}", after=tok)
        dh, d_pre = _rms_pre_bwd(h_in, norm_ffn_pre[i:i + 1], du, dh_out, name=f"pre_bwd_ffn{i}")
        return dh, d_pre, d_post

    dh3, d_nfp1, d_nfpost1 = mlp_bwd(1, dh4, h3, u3, a1, hh1, f1)
    dm1, d_nmpost1 = _rms_post_bwd(m1, norm_mix_post[1:2], dh3, name="post_bwd_mix1")
    tok = send_grad(_matmul(y_ssd, dm1, "tn", out_dtype=BF16, name="ssd_out_wg"), "ssd_out")
    dy_ssd = _matmul(dm1, ssd_out, "nt", out_dtype=F32, name="ssd_out_dg", after=tok)
    (dpm, ddt_g, d_bias, d_alog, d_dskip, dcwx, dcwb, dcwc, dcbx, dcbb, dcbc, d_normw) = _ssd_bwd(
        pm, dt_g, dtT_g, ssd_prm, st_ssd, dy_ssd, C=C, R=R, name="ssd_bwd")
    dpdt = ddt_g.transpose(1, 0, 2).reshape(T, n_heads).astype(BF16)
    tok = send_grad(jnp.concatenate([_matmul(dpm, u2, "tn", out_dtype=BF16, name="ssd_in_wg"),
                                     _matmul(dpdt, u2, "tn", out_dtype=BF16, name="ssd_in_dt_wg")], axis=0), "ssd_in")
    du2 = _matmul(dpm, ssd_main_t, "nn", out_dtype=F32, name="ssd_in_dg", after=tok)
    du2 = _matmul(dpdt, ssd_dt_t, "nn", out_dtype=F32, name="ssd_in_dt_dg", epi="add", extra=du2)
    dh2, d_nmp1 = _rms_pre_bwd(h2, norm_mix_pre[1:2], du2, dh3, name="pre_bwd_mix1")
    dh1, d_nfp0, d_nfpost0 = mlp_bwd(0, dh2, h1, u1, a0, hh0, f0)
    dm0, d_nmpost0 = _rms_post_bwd(m0, norm_mix_post[0:1], dh1, name="post_bwd_mix0")
    tok = send_grad(_matmul(y_ret, dm0, "tn", out_dtype=BF16, name="ret_out_wg"), "ret_out")
    dy_ret = _matmul(dm0, ret_out, "nt", out_dtype=F32, name="ret_out_dg", after=tok)
    dproj, d_gn = _ret_bwd(proj, ret_gn_w, ret_consts, st_ret, dy_ret, C=C, name="ret_bwd")
    tok = send_grad(_matmul(u0, dproj, "tn", out_dtype=BF16, name="ret_in_wg", col_parts=4), "ret_in")
    du0 = _matmul(dproj, ret_in_t, "nn", out_dtype=F32, name="ret_in_dg", after=tok)
    grad_x, d_nmp0 = _rms_pre_bwd(xs, norm_mix_pre[0:1], du0, dh1, name="pre_bwd_mix0")

    landed = {nm: _split_wait(sems, src, zone, 0, scatter=True, after=grad_x, name=f"scatter_wait_{nm}")
              for nm, sems, src, zone in in_flight}
    part_sum = {}

    def two_d(t):
        return t.reshape(-1, t.shape[-1])

    def sum_chips(nm):
        if nm[-1] in "01" and nm[:-1] in ("up", "down"):
            fam, layer = nm[:-1], int(nm[-1])
            part_sum[fam] = _sum_slots(landed[nm], name=f"sum_chips_{nm}", layer=layer, stack=part_sum.get(fam))
        else:
            part_sum[nm] = _sum_slots(landed[nm], name=f"sum_chips_{nm}")

    swaps = []
    for names_, fams in ((("down1", "up1", "ssd_out", "ssd_in", "down0"), ("ssd_out", "ssd_in", "down")),
                         (("up0", "ret_out", "ret_in"), ("up", "ret_out", "ret_in"))):
        for nm in names_:
            sum_chips(nm)
        mine = [two_d(part_sum[f]) for f in fams]
        swaps.append((fams, mine, _swap_start(mine, after=mine[-1], name=f"swap_start{len(swaps)}")))

    def upd(w, gs, m, v, name):
        shp = w.shape
        w2, m2, v2 = (two_d(t) for t in (w, m, v))
        return tuple(t.reshape(shp) for t in _adamw(w2, [g.reshape(w2.shape) for g in gs], m2, v2, name=name))

    def upd_t(w, gs, m, v, name):
        return tuple(t.T[None] for t in _adamw(w[0].T, gs, m[0].T, v[0].T, name=name))

    todo = {"ret_in": (upd, ret_w_in, m_ret_w_in, v_ret_w_in, "ret_w_in"),
            "ret_out": (upd, ret_w_out, m_ret_w_out, v_ret_w_out, "ret_w_out"),
            "ssd_in": (upd_t, ssd_w_in, m_ssd_w_in, v_ssd_w_in, "ssd_w_in"),
            "ssd_out": (upd, ssd_w_out, m_ssd_w_out, v_ssd_w_out, "ssd_w_out"),
            "up": (upd, mlp_w_up, m_mlp_w_up, v_mlp_w_up, "mlp_w_up"),
            "down": (upd, mlp_w_down, m_mlp_w_down, v_mlp_w_down, "mlp_w_down")}
    res = {}
    prev = grad_x
    for n_, (fams, srcs, (sems, zones)) in enumerate(swaps):
        theirs = _swap_wait(sems, srcs, zones, after=prev, name=f"swap_wait{n_}")
        for f, mine, other in zip(fams, srcs, theirs):
            fn, w, m, v, out_name = todo[f]
            res[out_name] = fn(w, [mine, other], m, v, f"adamw_{f}")
            prev = res[out_name][0]

    d_conv_w = jnp.concatenate([dcwx, dcwb, dcwc], axis=1)
    d_conv_b = jnp.concatenate([dcbx, dcbb, dcbc], axis=1)
    small_grads = [jnp.concatenate([d_nmp0, d_nmp1]), jnp.concatenate([d_nmpost0, d_nmpost1]),
                   jnp.concatenate([d_nfp0, d_nfp1]), jnp.concatenate([d_nfpost0, d_nfpost1]),
                   d_gn, d_bias.reshape(1, n_heads), d_alog.reshape(1, n_heads), d_dskip.reshape(1, n_heads),
                   d_conv_w, d_conv_b, d_normw]
    sg_shapes = [g.shape for g in small_grads]
    sg_rows = _pack_rows(sg_shapes)
    everyone = _all_gather_devices(_pack(small_grads, sg_rows), name="gather_small_grads")
    sg = _unpack(_sum_slots(everyone, name="sum_small_grads", tr=sg_rows), sg_shapes)
    (g_nmp, g_nmpost, g_nfp, g_nfpost, g_gn, g_bias, g_alog, g_dskip, g_cw_full, g_cb_full, g_nw_full) = sg
    g_cw = lax.dynamic_slice_in_dim(g_cw_full, chip * conv_sh, conv_sh, axis=1)[None]
    g_cb = lax.dynamic_slice_in_dim(g_cb_full, chip * conv_sh, conv_sh, axis=1)
    nw_sh = ssd_norm_w.shape[1]
    g_nw = lax.dynamic_slice_in_dim(g_nw_full, chip * nw_sh, nw_sh, axis=1)
    small = [("norm_mix_pre", norm_mix_pre, g_nmp, m_norm_mix_pre, v_norm_mix_pre),
             ("norm_mix_post", norm_mix_post, g_nmpost, m_norm_mix_post, v_norm_mix_post),
             ("norm_ffn_pre", norm_ffn_pre, g_nfp, m_norm_ffn_pre, v_norm_ffn_pre),
             ("norm_ffn_post", norm_ffn_post, g_nfpost, m_norm_ffn_post, v_norm_ffn_post),
             ("ret_gn_w", ret_gn_w, g_gn, m_ret_gn_w, v_ret_gn_w),
             ("ssd_conv_w", ssd_conv_w, g_cw, m_ssd_conv_w, v_ssd_conv_w),
             ("ssd_conv_b", ssd_conv_b, g_cb, m_ssd_conv_b, v_ssd_conv_b),
             ("ssd_dt_bias", ssd_dt_bias, g_bias, m_ssd_dt_bias, v_ssd_dt_bias),
             ("ssd_a_log", ssd_a_log, g_alog, m_ssd_a_log, v_ssd_a_log),
             ("ssd_d", ssd_d, g_dskip, m_ssd_d, v_ssd_d),
             ("ssd_norm_w", ssd_norm_w, g_nw, m_ssd_norm_w, v_ssd_norm_w)]
    sw_shapes = [w.shape for _, w, _, _, _ in small]
    sw_rows = _pack_rows(sw_shapes)
    packs = [_pack([t[j] for t in small], sw_rows) for j in (1, 2, 3, 4)]
    _, d_p, m_p, v_p = _adamw(packs[0], [packs[1]], packs[2], packs[3], name="adamw_small", tr=sw_rows)
    d_s, m_s, v_s = _unpack(d_p, sw_shapes), _unpack(m_p, sw_shapes), _unpack(v_p, sw_shapes)
    for j, (nm, w, g, _, _) in enumerate(small):
        res[nm] = (g.reshape(w.shape), d_s[j], m_s[j], v_s[j])

    order = ["norm_mix_pre", "norm_mix_post", "norm_ffn_pre", "norm_ffn_post", "ret_w_in", "ret_gn_w", "ret_w_out",
             "ssd_w_in", "ssd_conv_w", "ssd_conv_b", "ssd_dt_bias", "ssd_a_log", "ssd_d", "ssd_norm_w", "ssd_w_out",
             "mlp_w_up", "mlp_w_down"]
    return (loss, grad_x[None], *[res[n][0] for n in order], *[res[n][1] for n in order],
            *[res[n][2] for n in order], *[res[n][3] for n in order])
```

```python
import math

import numpy as np
import jax
import jax.numpy as jnp
from jax import lax
from jax.experimental import pallas as pl
from jax.experimental.pallas import tpu as pltpu

F32 = jnp.float32
BF16 = jnp.bfloat16
VMEM_LIMIT_BYTES = 56 * 1024 * 1024
MESH_AXES = ("x", "y", "c")
MESH_ID = pl.DeviceIdType.MESH

RMS_EPS = 1e-6
GN_EPS = 1e-5
RET_DK = 256
RET_DV = 512
ROPE_BASE = 10000.0
REF_CHUNK = 64
SSD_P = 64
SSD_N = 128
SSD_G = 8
SSD_CONV_W = 4
ADAM_LR, ADAM_B1, ADAM_B2, ADAM_EPS, ADAM_WD, ADAM_STEP = 0.001, 0.9, 0.999, 1e-08, 0.01, 10

NN = (((1,), (0,)), ((), ()))
NT = (((1,), (1,)), ((), ()))
TN = (((0,), (0,)), ((), ()))


def _cparams(*sem):
    return pltpu.CompilerParams(dimension_semantics=sem, vmem_limit_bytes=VMEM_LIMIT_BYTES)


def _dot(a, b, dims=NN):
    return lax.dot_general(a.astype(BF16), b.astype(BF16), dims, preferred_element_type=F32)


def _split_bf16(x, terms):
    parts, rest = [], x
    for _ in range(terms):
        p = rest.astype(BF16)
        parts.append(p)
        rest = rest - p.astype(F32)
    return parts


def _dot_sel(a, b, dims=NN, *, split, terms=3):
    if split == "a":
        sel = b.astype(BF16)
        return sum(lax.dot_general(p, sel, dims, preferred_element_type=F32) for p in _split_bf16(a, terms))
    sel = a.astype(BF16)
    return sum(lax.dot_general(sel, p, dims, preferred_element_type=F32) for p in _split_bf16(b, terms))


def _sigmoid(x):
    return 1.0 / (1.0 + jnp.exp(-x))


def _colsum(x):
    return jnp.sum(x, axis=0, keepdims=True)


MM_TILE = 1024
MM_FULL_K = 2048


def _mm_tiles(M, N, K):
    if K <= MM_FULL_K:
        return min(M, 2 * MM_TILE), min(N, MM_TILE), K
    return min(M, MM_TILE), min(N, 2 * MM_TILE), MM_TILE


def _matmul(a, b, mode, *, out_dtype, name, epi=None, extra=None, after=None, col_parts=None, tail=None):
    nt_ = tail[0] if tail else 0
    if mode == "nn":
        (M, K), (K2, N) = a.shape, (b.shape[0] - nt_, b.shape[1])
    elif mode == "nt":
        (M, K), (N, K2) = a.shape, (b.shape[0] - nt_, b.shape[1])
    else:
        (K, M), (K2, N) = a.shape, b.shape
    assert K == K2, (a.shape, b.shape, mode)
    tm, tn, tk = _mm_tiles(M, N, K)
    if col_parts:
        tm, tn = min(M, 2 * MM_TILE), min(tn, MM_TILE)
        while (N // col_parts) % tn:
            tn //= 2
    assert M % tm == 0 and N % tn == 0 and K % tk == 0, (M, N, K, tm, tn, tk)
    nk = K // tk
    if mode == "tn":
        a_spec = pl.BlockSpec((tk, tm), lambda i, j, k: (k, i))
    else:
        a_spec = pl.BlockSpec((tm, tk), lambda i, j, k: (i, k))
    if mode == "nt":
        b_spec = pl.BlockSpec((tn, tk), lambda i, j, k: (j, k))
    else:
        b_spec = pl.BlockSpec((tk, tn), lambda i, j, k: (k, j))
    dims = {"nn": NN, "nt": NT, "tn": TN}[mode]
    o_spec = pl.BlockSpec((tm, tn), lambda i, j, k: (i, j))
    out_dims = (M, N)
    if col_parts:
        per = N // col_parts // tn
        o_spec = pl.BlockSpec((None, tm, tn), lambda i, j, k: (j // per, i, j % per))
        out_dims = (col_parts, M, N // col_parts)
    has_extra = epi in ("drelu2", "add")
    n_out = 2 if epi == "relu2" else 1

    in_specs = [a_spec, b_spec] + ([o_spec] if has_extra else [])
    args = [a, b] + ([extra] if has_extra else [])
    if after is not None:
        in_specs.append(pl.BlockSpec(after.shape, lambda i, j, k: (0, 0)))
        args.append(after)
    n_plain = len(args)
    out_specs = [o_spec] * n_out
    out_shape = [jax.ShapeDtypeStruct(out_dims, out_dtype)] * n_out
    if tail and mode == "nt":
        assert nk == 1 and N % nt_ == 0
        in_specs.append(pl.BlockSpec((nt_, tk), lambda i, j, k: (N // nt_, 0)))
        args.append(b)
        out_specs.append(pl.BlockSpec((tm, nt_), lambda i, j, k: (i, 0)))
        out_shape.append(jax.ShapeDtypeStruct((M, nt_), F32))
    elif tail:
        assert mode == "nn" and K % nt_ == 0
        in_specs += [pl.BlockSpec((tm, nt_), lambda i, j, k: (i, 0)), pl.BlockSpec((nt_, tn), lambda i, j, k: (K // nt_, j))]
        args += [tail[1], b]
    n_in = len(args)

    def body(*refs):
        a_ref, b_ref = refs[0], refs[1]
        e_ref = refs[2] if has_extra else None
        outs = refs[n_in:n_in + n_out]

        def finish(r):
            if tail and mode == "nn":
                r = r + _dot(refs[n_plain][...], refs[n_plain + 1][...])
            if epi is None:
                outs[0][...] = r.astype(outs[0].dtype)
            elif epi == "relu2":
                outs[0][...] = r.astype(outs[0].dtype)
                h = jnp.maximum(r, 0.0)
                outs[1][...] = (h * h).astype(outs[1].dtype)
            elif epi == "drelu2":
                av = jnp.maximum(e_ref[...].astype(F32), 0.0)
                outs[0][...] = (r * (2.0 * av)).astype(outs[0].dtype)
            else:
                outs[0][...] = (r + e_ref[...].astype(F32)).astype(outs[0].dtype)

        if tail and mode == "nt":
            @pl.when(pl.program_id(1) == 0)
            def _():
                refs[n_in + n_out][...] = _dot(a_ref[...], refs[n_plain][...], NT)

        if nk == 1:
            finish(_dot(a_ref[...], b_ref[...], dims))
            return
        acc = refs[-1]
        k = pl.program_id(2)

        @pl.when(k == 0)
        def _():
            acc[...] = jnp.zeros_like(acc)

        acc[...] += _dot(a_ref[...], b_ref[...], dims)

        @pl.when(k == nk - 1)
        def _():
            finish(acc[...])

    res = pl.pallas_call(
        body, name=name, grid=(M // tm, N // tn, nk), in_specs=in_specs, out_specs=out_specs,
        out_shape=out_shape, scratch_shapes=[pltpu.VMEM((tm, tn), F32)] if nk > 1 else [],
        compiler_params=_cparams("parallel", "arbitrary" if tail and mode == "nt" else "parallel", "arbitrary"),
    )(*args)
    return res if len(res) > 1 else res[0]


def _rstd(x):
    return lax.rsqrt(jnp.mean(x * x, axis=-1, keepdims=True) + RMS_EPS)


def _row_call(body, ins, outs_shape, *, name, rows, tr, acc_outs=()):
    tr = min(tr, rows)
    assert rows % tr == 0
    in_specs = []
    for arr, blocked in ins:
        if blocked:
            in_specs.append(pl.BlockSpec((tr, arr.shape[1]), lambda i: (i, 0)))
        else:
            in_specs.append(pl.BlockSpec(arr.shape, lambda i: (0, 0)))
    out_specs = []
    for n, s in enumerate(outs_shape):
        if n in acc_outs:
            out_specs.append(pl.BlockSpec(s.shape, lambda i: (0, 0)))
        else:
            out_specs.append(pl.BlockSpec((tr, s.shape[1]), lambda i: (i, 0)))
    return pl.pallas_call(
        body, name=name, grid=(rows // tr,), in_specs=in_specs, out_specs=out_specs, out_shape=outs_shape,
        compiler_params=_cparams("arbitrary" if acc_outs else "parallel"),
    )(*[a for a, _ in ins])


def _rms_pre(h, w, *, name):
    T, D = h.shape

    def body(h_ref, w_ref, u_ref):
        x = h_ref[...]
        u_ref[...] = (x * _rstd(x) * w_ref[...]).astype(BF16)

    return _row_call(body, [(h, True), (w, False)], [jax.ShapeDtypeStruct((T, D), BF16)], name=name, rows=T, tr=256)[0]


def _rms_post_pre(h, m, w_post, w_pre, *, name):
    T, D = h.shape

    def body(h_ref, m_ref, wp_ref, wn_ref, hn_ref, u_ref):
        mm = m_ref[...]
        hn = h_ref[...] + mm * _rstd(mm) * wp_ref[...]
        hn_ref[...] = hn
        u_ref[...] = (hn * _rstd(hn) * wn_ref[...]).astype(BF16)

    return _row_call(body, [(h, True), (m, True), (w_post, False), (w_pre, False)],
                     [jax.ShapeDtypeStruct((T, D), F32), jax.ShapeDtypeStruct((T, D), BF16)], name=name, rows=T, tr=256)


def _rms_post_loss(h, m, w_post, tgt, *, name):
    T, D = h.shape

    def body(h_ref, m_ref, wp_ref, t_ref, dh_ref, loss_ref):
        @pl.when(pl.program_id(0) == 0)
        def _():
            loss_ref[...] = jnp.zeros_like(loss_ref)

        mm = m_ref[...]
        err = h_ref[...] + mm * _rstd(mm) * wp_ref[...] - t_ref[...]
        dh_ref[...] = err * (1.0 / D)
        loss_ref[...] += _colsum(jnp.sum(err * err, axis=1, keepdims=True))

    return _row_call(body, [(h, True), (m, True), (w_post, False), (tgt, True)],
                     [jax.ShapeDtypeStruct((T, D), F32), jax.ShapeDtypeStruct((1, 1), F32)],
                     name=name, rows=T, tr=256, acc_outs=(1,))


def _rms_bwd_vals(x, w, dy):
    r = _rstd(x)
    xh = x * r
    g = dy * w
    dx = r * (g - xh * jnp.mean(g * xh, axis=-1, keepdims=True))
    return dx, _colsum(dy * xh)


def _rms_post_bwd(m, w_post, dh, *, name):
    T, D = m.shape

    def body(m_ref, w_ref, dh_ref, dm_ref, dw_ref):
        @pl.when(pl.program_id(0) == 0)
        def _():
            dw_ref[...] = jnp.zeros_like(dw_ref)

        dx, dw = _rms_bwd_vals(m_ref[...], w_ref[...], dh_ref[...])
        dm_ref[...] = dx.astype(BF16)
        dw_ref[...] += dw

    return _row_call(body, [(m, True), (w_post, False), (dh, True)],
                     [jax.ShapeDtypeStruct((T, D), BF16), jax.ShapeDtypeStruct((1, D), F32)],
                     name=name, rows=T, tr=256, acc_outs=(1,))


def _rms_pre_bwd(h, w_pre, du, dh_out, *, name):
    T, D = h.shape

    def body(h_ref, w_ref, du_ref, dho_ref, dh_ref, dw_ref):
        @pl.when(pl.program_id(0) == 0)
        def _():
            dw_ref[...] = jnp.zeros_like(dw_ref)

        dx, dw = _rms_bwd_vals(h_ref[...], w_ref[...], du_ref[...])
        dh_ref[...] = dho_ref[...] + dx
        dw_ref[...] += dw

    return _row_call(body, [(h, True), (w_pre, False), (du, True), (dh_out, True)],
                     [jax.ShapeDtypeStruct((T, D), F32), jax.ShapeDtypeStruct((1, D), F32)],
                     name=name, rows=T, tr=256, acc_outs=(1,))


def _rms_pre_post_bwd(h, w_pre, du, dh_out, m_prev, w_post_prev, *, name):
    T, D = h.shape

    def body(h_ref, w_ref, du_ref, dho_ref, m_ref, wp_ref, dh_ref, dm_ref, dw_ref, dwp_ref):
        @pl.when(pl.program_id(0) == 0)
        def _():
            dw_ref[...] = jnp.zeros_like(dw_ref)
            dwp_ref[...] = jnp.zeros_like(dwp_ref)

        dx, dw = _rms_bwd_vals(h_ref[...], w_ref[...], du_ref[...])
        dh = dho_ref[...] + dx
        dh_ref[...] = dh
        dw_ref[...] += dw
        dm, dwp = _rms_bwd_vals(m_ref[...], wp_ref[...], dh)
        dm_ref[...] = dm.astype(BF16)
        dwp_ref[...] += dwp

    return _row_call(body, [(h, True), (w_pre, False), (du, True), (dh_out, True), (m_prev, True), (w_post_prev, False)],
                     [jax.ShapeDtypeStruct((T, D), F32), jax.ShapeDtypeStruct((T, D), BF16),
                      jax.ShapeDtypeStruct((1, D), F32), jax.ShapeDtypeStruct((1, D), F32)],
                     name=name, rows=T, tr=256, acc_outs=(2, 3))


def _ret_consts(T, C, H):
    lg = np.log1p(-np.exp2(-5.0 - np.arange(H, dtype=np.float64)))
    idx = np.arange(C, dtype=np.float64)
    dist = np.abs(idx[:, None] - idx[None, :])
    vis = (idx[None, :] // REF_CHUNK) <= (idx[:, None] // REF_CHUNK)
    mask = np.exp(dist[None] * lg[:, None, None]) * vis[None]
    xi = np.exp((idx[None, :] + 1.0) * lg[:, None])[..., None]
    zeta = np.exp((C - 1.0 - idx)[None, :] * lg[:, None])[..., None]
    half = RET_DK // 2
    inv_freq = ROPE_BASE ** (-np.arange(half, dtype=np.float32) / np.float32(half))
    ang = np.arange(T, dtype=np.float32)[:, None] * inv_freq[None, :].astype(np.float32)
    return (jnp.asarray(mask, F32), jnp.asarray(xi, F32), jnp.asarray(zeta, F32),
            jnp.asarray(np.cos(ang), F32), jnp.asarray(np.sin(ang), F32))


def _rot(t, cos, sin):
    half = RET_DK // 2
    t1, t2 = t[:, :half], t[:, half:]
    return jnp.concatenate([t1 * cos - t2 * sin, t1 * sin + t2 * cos], axis=1)


def _unrot(d, cos, sin):
    half = RET_DK // 2
    d1, d2 = d[:, :half], d[:, half:]
    return jnp.concatenate([d1 * cos + d2 * sin, d2 * cos - d1 * sin], axis=1)


def _ret_specs(C, H, rev, NS):
    def ci(i):
        return NS - 1 - i if rev else i

    nq = H
    q_spec = pl.BlockSpec((C, RET_DK), lambda h, i: (ci(i), h))
    k_spec = pl.BlockSpec((C, RET_DK), lambda h, i: (ci(i), nq + h))
    v_spec = pl.BlockSpec((C, RET_DV), lambda h, i: (ci(i), H + h))
    g_spec = pl.BlockSpec((C, RET_DV), lambda h, i: (ci(i), 2 * H + h))
    cs_spec = pl.BlockSpec((C, RET_DK // 2), lambda h, i: (ci(i), 0))
    m_spec = pl.BlockSpec((None, C, C), lambda h, i: (h, 0, 0))
    vec_spec = pl.BlockSpec((None, C, 1), lambda h, i: (h, 0, 0))
    gn_spec = pl.BlockSpec((1, RET_DV), lambda h, i: (0, h))
    st_spec = pl.BlockSpec((None, None, RET_DK, RET_DV), lambda h, i: (h, ci(i), 0, 0))
    return q_spec, k_spec, v_spec, g_spec, cs_spec, m_spec, vec_spec, gn_spec, st_spec


def _ret_fwd_vals(q, k, v, cos, sin, mask, xi, s_in):
    qr = _rot(q, cos, sin)
    kr = _rot(k, cos, sin) * (RET_DK ** -0.5)
    a = _dot(qr, kr, NT) * mask
    o = _dot(a, v) + _dot(qr, s_in) * xi
    mu = jnp.mean(o, axis=1, keepdims=True)
    oc = o - mu
    rstd = lax.rsqrt(jnp.mean(oc * oc, axis=1, keepdims=True) + GN_EPS)
    return qr, kr, a, oc * rstd, rstd


def _ret_fwd(proj, gn_w, consts, *, C, name):
    T = proj.shape[0]
    H = gn_w.shape[1] // RET_DV
    NS = T // C
    mask, xi, zeta, cos, sin = consts
    q_spec, k_spec, v_spec, g_spec, cs_spec, m_spec, vec_spec, gn_spec, st_spec = _ret_specs(C, H, False, NS)
    y_spec = pl.BlockSpec((C, RET_DV), lambda h, i: (i, h))

    def body(q_ref, k_ref, v_ref, g_ref, cos_ref, sin_ref, m_ref, xi_ref, ze_ref, gn_ref, y_ref, st_ref, S):
        @pl.when(pl.program_id(1) == 0)
        def _():
            S[...] = jnp.zeros_like(S)

        s_in = S[...]
        st_ref[...] = s_in
        v = v_ref[...]
        xi_v = xi_ref[...]
        qr, kr, a, on, rstd = _ret_fwd_vals(q_ref[...], k_ref[...], v, cos_ref[...], sin_ref[...], m_ref[...], xi_v, s_in)
        g = g_ref[...]
        y_ref[...] = (g * _sigmoid(g) * on * gn_ref[...]).astype(BF16)
        S[...] = s_in * xi_v[C - 1:C, :] + _dot(kr * ze_ref[...], v, TN)

    return pl.pallas_call(
        body, name=name, grid=(H, NS),
        in_specs=[q_spec, k_spec, v_spec, g_spec, cs_spec, cs_spec, m_spec, vec_spec, vec_spec, gn_spec],
        out_specs=[y_spec, st_spec],
        out_shape=[jax.ShapeDtypeStruct((T, H * RET_DV), BF16), jax.ShapeDtypeStruct((H, NS, RET_DK, RET_DV), F32)],
        scratch_shapes=[pltpu.VMEM((RET_DK, RET_DV), F32)],
        compiler_params=_cparams("parallel", "arbitrary"),
    )(proj, proj, proj, proj, cos, sin, mask, xi, zeta, gn_w)


def _stage_out(out_hbm, stage, sems, step, n_steps, row0, pieces, values):
    C = stage.shape[1]
    slot = step % 2

    def copies(sl):
        return [pltpu.make_async_copy(stage.at[sl, :, pl.ds(c0, w)],
                                      out_hbm.at[pl.ds(pl.multiple_of(row0, 16), C), pl.ds(pl.multiple_of(dc, 128), w)],
                                      sems.at[sl, n]) for n, (c0, w, dc) in enumerate(pieces)]

    @pl.when(step >= 2)
    def _():
        for cp in copies(slot):
            cp.wait()

    for (c0, w, _), v in zip(pieces, values):
        stage[slot, :, c0:c0 + w] = v
    for cp in copies(slot):
        cp.start()

    @pl.when(step == n_steps - 1)
    def _():
        for cp in copies(slot):
            cp.wait()
        if n_steps >= 2:
            for cp in copies(1 - slot):
                cp.wait()


def _ret_bwd(proj, gn_w, consts, states, dy, *, C, name):
    T = proj.shape[0]
    H = gn_w.shape[1] // RET_DV
    NS = T // C
    mask, xi, zeta, cos, sin = consts
    q_spec, k_spec, v_spec, g_spec, cs_spec, m_spec, vec_spec, gn_spec, st_spec = _ret_specs(C, H, True, NS)
    dy_spec = pl.BlockSpec((C, RET_DV), lambda h, i: (NS - 1 - i, h))
    scale = RET_DK ** -0.5
    wq, wv = H * RET_DK, H * RET_DV

    def body(q_ref, k_ref, v_ref, g_ref, cos_ref, sin_ref, m_ref, xi_ref, ze_ref, gn_ref, st_ref, dy_ref,
             dproj_ref, dgn_ref, dS, stage, sems):
        @pl.when(pl.program_id(1) == 0)
        def _():
            dS[...] = jnp.zeros_like(dS)
            dgn_ref[...] = jnp.zeros_like(dgn_ref)

        s_in = st_ref[...]
        v = v_ref[...]
        cos, sin, mask, xi_v, ze = cos_ref[...], sin_ref[...], m_ref[...], xi_ref[...], ze_ref[...]
        qr, kr, a, on, rstd = _ret_fwd_vals(q_ref[...], k_ref[...], v, cos, sin, mask, xi_v, s_in)
        g = g_ref[...]
        sg = _sigmoid(g)
        silu = g * sg
        gnw = gn_ref[...]
        dy = dy_ref[...].astype(F32)
        dg = (dy * on * gnw * (sg * (1.0 + g * (1.0 - sg)))).astype(BF16)
        t = dy * silu
        dgn_ref[...] += _colsum(t * on)
        don = t * gnw
        do = rstd * (don - jnp.mean(don, axis=1, keepdims=True) - on * jnp.mean(don * on, axis=1, keepdims=True))
        dox = do * xi_v
        ds_out = dS[...]
        da = _dot(do, v, NT) * mask
        kz = kr * ze
        dv = (_dot(a, do, TN) + _dot(kz, ds_out)).astype(BF16)
        dqr = _dot(da, kr) + _dot(dox, s_in, NT)
        dkr = _dot(da, qr, TN) + _dot(v, ds_out, NT) * ze
        dS[...] = ds_out * xi_v[C - 1:C, :] + _dot(qr, dox, TN)
        dq = _unrot(dqr, cos, sin).astype(BF16)
        dk = _unrot(dkr * scale, cos, sin).astype(BF16)
        h, i = pl.program_id(0), pl.program_id(1)
        pieces = [(0, RET_DK, h * RET_DK), (RET_DK, RET_DK, wq + h * RET_DK),
                  (2 * RET_DK, RET_DV, 2 * wq + h * RET_DV), (2 * RET_DK + RET_DV, RET_DV, 2 * wq + wv + h * RET_DV)]
        _stage_out(dproj_ref, stage, sems, h * NS + i, H * NS, (NS - 1 - i) * C, pieces, [dq, dk, dv, dg])

    return pl.pallas_call(
        body, name=name, grid=(H, NS),
        in_specs=[q_spec, k_spec, v_spec, g_spec, cs_spec, cs_spec, m_spec, vec_spec, vec_spec, gn_spec, st_spec, dy_spec],
        out_specs=[_ANY, gn_spec],
        out_shape=[jax.ShapeDtypeStruct((T, 2 * wq + 2 * wv), BF16), jax.ShapeDtypeStruct((1, H * RET_DV), F32)],
        scratch_shapes=[pltpu.VMEM((RET_DK, RET_DV), F32), pltpu.VMEM((2, C, 2 * RET_DK + 2 * RET_DV), BF16),
                        pltpu.SemaphoreType.DMA((2, 4))],
        compiler_params=_cparams("arbitrary", "arbitrary"),
    )(proj, proj, proj, proj, cos, sin, mask, xi, zeta, gn_w, states, dy)


def _shift_down(x, prev8, k):
    if k == 0:
        return x
    y = pltpu.roll(x, k, 0)
    row = lax.broadcasted_iota(jnp.int32, prev8.shape, 0)
    top = jnp.where(row < k, pltpu.roll(prev8, k, 0), y[:8])
    return jnp.concatenate([top, y[8:]], axis=0)


def _shift_up(x, next8, k):
    if k == 0:
        return x
    n = x.shape[0]
    y = pltpu.roll(x, n - k, 0)
    row = lax.broadcasted_iota(jnp.int32, next8.shape, 0)
    bot = jnp.where(row >= 8 - k, pltpu.roll(next8, 8 - k, 0), y[n - 8:])
    return jnp.concatenate([y[:n - 8], bot], axis=0)


def _conv_silu(raw, halo, w, b):
    cv = b
    for tap in range(SSD_CONV_W):
        cv = cv + _shift_down(raw, halo, SSD_CONV_W - 1 - tap) * w[tap:tap + 1, :]
    sg = _sigmoid(cv)
    return cv * sg, cv, sg


def _conv_silu_bwd(d_post, cv, sg, raw, halo, w, carry8):
    dcv = d_post * (sg * (1.0 + cv * (1.0 - sg)))
    d_raw = jnp.zeros_like(raw)
    dws = []
    for tap in range(SSD_CONV_W):
        k = SSD_CONV_W - 1 - tap
        d_raw = d_raw + _shift_up(dcv, carry8, k) * w[tap:tap + 1, :]
        dws.append(_colsum(dcv * _shift_down(raw, halo, k)))
    return d_raw, jnp.concatenate(dws, axis=0), _colsum(dcv), dcv[:8]


def _softplus(x):
    return jnp.maximum(x, 0.0) + jnp.log1p(jnp.exp(-jnp.abs(x)))


def _ssd_common(C, R, dt, dtT, bias, biasT, alog, alogT, E):
    p = dt + bias
    dtv = _softplus(p)
    a = -jnp.exp(alog)
    da = dtv * a
    daT = _softplus(dtT + biasT) * (-jnp.exp(alogT))
    row = lax.broadcasted_iota(jnp.int32, (C, C), 0)
    col = lax.broadcasted_iota(jnp.int32, (C, C), 1)
    tril = row >= col
    trilf = jnp.where(tril, 1.0, 0.0).astype(F32)
    triuf = jnp.where(col >= row, 1.0, 0.0).astype(F32)
    acum = _dot_sel(trilf, da, split="b")
    acumT = _dot_sel(daT, trilf, NT, split="a")
    al = acum[C - 1:C, :]
    ea = jnp.exp(acum)
    dte = jnp.exp(al - acum)
    eal = jnp.exp(al)
    return dict(p=p, dtv=dtv, a=a, da=da, tril=tril, triuf=triuf, acum=acum, acumT=acumT, al=al, ea=ea, dte=dte, eal=eal,
                dtv_e=_dot_sel(dtv, E, split="a", terms=2), ea_e=_dot_sel(ea, E, split="a", terms=2),
                dte_e=_dot_sel(dte, E, split="a", terms=2), eal_e=_dot_sel(eal, E, split="a"))


def _head_decay(q, r, C, R):
    seg = jnp.broadcast_to(q["acum"][:, r:r + 1], (C, C)) - q["acumT"][r:r + 1, :]
    return jnp.exp(jnp.where(q["tril"], seg, -1e30))


def _ssd_group_specs(C, R, NS, rev):
    RP = R * SSD_P
    G = SSD_G
    hb = C // 8

    def ci(i):
        return NS - 1 - i if rev else i

    def halo_row(i):
        return jnp.maximum(ci(i) * hb - 1, 0)

    off_b = G * RP // SSD_N
    z_spec = pl.BlockSpec((C, RP), lambda g, i: (ci(i), g))
    x_spec = pl.BlockSpec((C, RP), lambda g, i: (ci(i), G + g))
    b_spec = pl.BlockSpec((C, SSD_N), lambda g, i: (ci(i), 2 * off_b + g))
    c_spec = pl.BlockSpec((C, SSD_N), lambda g, i: (ci(i), 2 * off_b + G + g))
    xh_spec = pl.BlockSpec((8, RP), lambda g, i: (halo_row(i), G + g))
    bh_spec = pl.BlockSpec((8, SSD_N), lambda g, i: (halo_row(i), 2 * off_b + g))
    ch_spec = pl.BlockSpec((8, SSD_N), lambda g, i: (halo_row(i), 2 * off_b + G + g))
    dt_spec = pl.BlockSpec((None, C, R), lambda g, i: (g, ci(i), 0))
    dtT_spec = pl.BlockSpec((None, R, C), lambda g, i: (g, 0, ci(i)))
    pr_spec = pl.BlockSpec((None, 1, R), lambda g, i: (g, 0, 0))
    prT_spec = pl.BlockSpec((None, R, 1), lambda g, i: (g, 0, 0))
    cwx_spec = pl.BlockSpec((SSD_CONV_W, RP), lambda g, i: (0, g))
    cwn_spec = pl.BlockSpec((SSD_CONV_W, SSD_N), lambda g, i: (0, g))
    cbx_spec = pl.BlockSpec((1, RP), lambda g, i: (0, g))
    cbn_spec = pl.BlockSpec((1, SSD_N), lambda g, i: (0, g))
    e_spec = pl.BlockSpec((R, RP), lambda g, i: (0, 0))
    st_spec = pl.BlockSpec((None, None, SSD_N, RP), lambda g, i: (g, ci(i), 0, 0))
    return dict(z=z_spec, x=x_spec, b=b_spec, c=c_spec, xh=xh_spec, bh=bh_spec, ch=ch_spec, dt=dt_spec, dtT=dtT_spec,
                pr=pr_spec, prT=prT_spec, cwx=cwx_spec, cwn=cwn_spec, cbx=cbx_spec, cbn=cbn_spec, e=e_spec, st=st_spec)


def _ssd_forward_vals(C, R, refs, first, s_in):
    E = refs["E"]
    halo_on = jnp.where(first, 0.0, 1.0)
    xh, bh, ch = refs["xh"] * halo_on, refs["bh"] * halo_on, refs["ch"] * halo_on
    xs, cvx, sgx = _conv_silu(refs["x"], xh, refs["cwx"], refs["cbx"])
    bm, cvb, sgb = _conv_silu(refs["b"], bh, refs["cwb"], refs["cbb"])
    cm, cvc, sgc = _conv_silu(refs["c"], ch, refs["cwc"], refs["cbc"])
    q = _ssd_common(C, R, refs["dt"], refs["dtT"], refs["bias"], refs["biasT"], refs["alog"], refs["alogT"], E)
    xdt = xs * q["dtv_e"]
    cb = _dot(cm, bm, NT)
    yoff_raw = _dot(cm, s_in)
    xdt_b = xdt.astype(BF16)
    low = lax.broadcasted_iota(jnp.int32, (1, 2 * SSD_P), 1) < SSD_P
    pairs = []
    for j in range(R // 2):
        xp = xdt_b[:, 2 * SSD_P * j:2 * SSD_P * (j + 1)]
        y0 = _dot(cb * _head_decay(q, 2 * j, C, R), xp)
        y1 = _dot(cb * _head_decay(q, 2 * j + 1, C, R), xp)
        pairs.append(jnp.where(low, y0, y1))
    ydiag = jnp.concatenate(pairs, axis=1)
    d_e =_dot_sel(refs["dskip"], E, split="a")
    y = ydiag + yoff_raw * q["ea_e"] + d_e * xs
    xd = xdt * q["dte_e"]
    s_out = s_in * q["eal_e"] + _dot(bm, xd, TN)
    z = refs["z"]
    sgz = _sigmoid(z)
    yz = y * (z * sgz)
    rn = lax.rsqrt(jnp.mean(yz * yz, axis=1, keepdims=True) + RMS_EPS)
    return dict(q=q, xh=xh, bh=bh, ch=ch, xs=xs, cvx=cvx, sgx=sgx, bm=bm, cvb=cvb, sgb=sgb, cm=cm, cvc=cvc, sgc=sgc,
                xdt=xdt, cb=cb, yoff_raw=yoff_raw, d_e=d_e, y=y, xd=xd, s_out=s_out, z=z, sgz=sgz, yz=yz, rn=rn)


_SSD_IN_NAMES = ("z", "x", "b", "c", "xh", "bh", "ch", "dt", "dtT", "bias", "biasT", "alog", "alogT", "dskip",
                 "cwx", "cwb", "cwc", "cbx", "cbb", "cbc", "nw", "E")


def _ssd_inputs(pm, dt_g, dtT_g, prm, sp):
    bias, biasT, alog, alogT, dskip, cwx, cwb, cwc, cbx, cbb, cbc, nw, E = prm
    args = [pm, pm, pm, pm, pm, pm, pm, dt_g, dtT_g, bias, biasT, alog, alogT, dskip, cwx, cwb, cwc, cbx, cbb, cbc, nw, E]
    specs = [sp["z"], sp["x"], sp["b"], sp["c"], sp["xh"], sp["bh"], sp["ch"], sp["dt"], sp["dtT"], sp["pr"], sp["prT"],
             sp["pr"], sp["prT"], sp["pr"], sp["cwx"], sp["cwn"], sp["cwn"], sp["cbx"], sp["cbn"], sp["cbn"], sp["cbx"], sp["e"]]
    return args, specs


def _ssd_fwd(pm, dt_g, dtT_g, prm, *, C, R, name):
    T = pm.shape[0]
    NS = T // C
    RP = R * SSD_P
    G = SSD_G
    sp = _ssd_group_specs(C, R, NS, False)
    args, specs = _ssd_inputs(pm, dt_g, dtT_g, prm, sp)
    nin = len(args)

    def body(*refs):
        ins = {n: r[...] for n, r in zip(_SSD_IN_NAMES, refs[:nin])}
        y_ref, st_ref, S = refs[nin:]
        first = pl.program_id(1) == 0

        @pl.when(first)
        def _():
            S[...] = jnp.zeros_like(S)

        s_in = S[...]
        st_ref[...] = s_in
        f = _ssd_forward_vals(C, R, ins, first, s_in)
        y_ref[...] = (f["yz"] * f["rn"] * ins["nw"]).astype(BF16)
        S[...] = f["s_out"]

    return pl.pallas_call(
        body, name=name, grid=(G, NS), in_specs=specs,
        out_specs=[pl.BlockSpec((C, RP), lambda g, i: (i, g)), sp["st"]],
        out_shape=[jax.ShapeDtypeStruct((T, G * RP), BF16), jax.ShapeDtypeStruct((G, NS, SSD_N, RP), F32)],
        scratch_shapes=[pltpu.VMEM((SSD_N, RP), F32)],
        compiler_params=_cparams("parallel", "arbitrary"),
    )(*args)


def _ssd_bwd(pm, dt_g, dtT_g, prm, states, dout, *, C, R, name):
    T = pm.shape[0]
    NS = T // C
    RP = R * SSD_P
    G = SSD_G
    sp = _ssd_group_specs(C, R, NS, True)
    args, specs = _ssd_inputs(pm, dt_g, dtT_g, prm, sp)
    nin = len(args)
    rows_spec = pl.BlockSpec((C, RP), lambda g, i: (NS - 1 - i, g))
    args = args + [states, dout]
    specs = specs + [sp["st"], rows_spec]

    def body(*refs):
        ins = {n: r[...] for n, r in zip(_SSD_IN_NAMES, refs[:nin])}
        st_ref, dout_ref = refs[nin], refs[nin + 1]
        (dpm_ref, ddt_ref, dbias_ref, dalog_ref, dd_ref, dcwx_ref, dcwb_ref, dcwc_ref,
         dcbx_ref, dcbb_ref, dcbc_ref, dnw_ref) = refs[nin + 2:nin + 14]
        dS, cx8, cb8, cc8, stage, sems = refs[nin + 14:]
        acc_refs = (dbias_ref, dalog_ref, dd_ref, dcwx_ref, dcwb_ref, dcwc_ref, dcbx_ref, dcbb_ref, dcbc_ref, dnw_ref)
        step = pl.program_id(1)

        @pl.when(step == 0)
        def _():
            for r_ in acc_refs + (dS, cx8, cb8, cc8):
                r_[...] = jnp.zeros_like(r_)

        first = step == NS - 1
        E = ins["E"]
        s_in = st_ref[...]
        f = _ssd_forward_vals(C, R, ins, first, s_in)
        q = f["q"]
        xs, bm, cm, xdt, cb, y, z, sgz, yz, rn = (f[n] for n in ("xs", "bm", "cm", "xdt", "cb", "y", "z", "sgz", "yz", "rn"))
        nw = ins["nw"]
        dout = dout_ref[...].astype(F32)
        yh = yz * rn
        dnw_ref[...] += _colsum(dout * yh)
        g1 = dout * nw
        dyz = rn * (g1 - yh * jnp.mean(g1 * yh, axis=1, keepdims=True))
        dz = (dyz * y * (sgz * (1.0 + z * (1.0 - sgz)))).astype(BF16)
        dy = dyz * (z * sgz)
        dd_ref[...] += _dot_sel(_colsum(dy * xs), E, NT, split="a")
        dxs = dy * f["d_e"]
        dyo = dy * q["ea_e"]
        dcm = _dot(dyo, s_in, NT)
        ds_acc = _dot(cm, dyo, TN)
        dacum = _dot_sel(dy * f["yoff_raw"], E, NT, split="a", terms=2) * q["ea"]
        dacumT = jnp.zeros((R, C), F32)
        dcb = jnp.zeros((C, C), F32)
        rowR = lax.broadcasted_iota(jnp.int32, (1, R), 1)
        rowRT = lax.broadcasted_iota(jnp.int32, (R, 1), 0)
        dy_b, xdt_b = dy.astype(BF16), xdt.astype(BF16)
        low = lax.broadcasted_iota(jnp.int32, (1, 2 * SSD_P), 1) < SSD_P
        dxdt_pairs = []
        for j in range(R // 2):
            lanes = slice(2 * SSD_P * j, 2 * SSD_P * (j + 1))
            dyp, xp = dy_b[:, lanes], xdt_b[:, lanes]
            halves = []
            for r, mine in ((2 * j, low), (2 * j + 1, jnp.logical_not(low))):
                lr = _head_decay(q, r, C, R)
                w_r = cb * lr
                dw = _dot(jnp.where(mine, dyp, jnp.zeros_like(dyp)), xp, NT)
                halves.append(_dot(w_r, dyp, TN))
                dcb = dcb + dw * lr
                dseg = dw * w_r
                dacum = dacum + jnp.sum(dseg, axis=1, keepdims=True) * jnp.where(rowR == r, 1.0, 0.0)
                dacumT = dacumT - _colsum(dseg) * jnp.where(rowRT == r, 1.0, 0.0)
            dxdt_pairs.append(jnp.where(low, halves[0], halves[1]))
        dxdt = jnp.concatenate(dxdt_pairs, axis=1)
        dsn = dS[...]
        ds_acc = ds_acc + dsn * q["eal_e"]
        d_eal = _dot_sel(_colsum(dsn * s_in), E, NT, split="a")
        dbm = _dot(f["xd"], dsn, NT)
        dxd = _dot(bm, dsn)
        dxdt = dxdt + dxd * q["dte_e"]
        d_dte = _dot_sel(dxd * xdt, E, NT, split="a", terms=2) * q["dte"]
        d_al = _colsum(d_dte) + d_eal * q["eal"]
        dacum = dacum - d_dte
        rowC = lax.broadcasted_iota(jnp.int32, (C, 1), 0)
        dacum = dacum + jnp.where(rowC == C - 1, 1.0, 0.0) * d_al
        dS[...] = ds_acc
        dcm = dcm + _dot(dcb, bm)
        dbm = dbm + _dot(dcb, cm, TN)
        eye = jnp.where(lax.broadcasted_iota(jnp.int32, (C, C), 0) == lax.broadcasted_iota(jnp.int32, (C, C), 1), 1.0, 0.0)
        dacum = dacum + _dot_sel(eye, dacumT, NT, split="b")
        dda = _dot_sel(q["triuf"], dacum, split="b")
        ddtv = dda * q["a"] + _dot_sel(dxdt * xs, E, NT, split="a", terms=2)
        dalog_ref[...] += _colsum(dda * q["dtv"]) * q["a"]
        dxs = dxs + dxdt * q["dtv_e"]
        dp = ddtv * _sigmoid(q["p"])
        ddt_ref[...] = dp
        dbias_ref[...] += _colsum(dp)
        d_raw, d_w, d_b, c8 = _conv_silu_bwd(dxs, f["cvx"], f["sgx"], ins["x"], f["xh"], ins["cwx"], cx8[...])
        dx = d_raw.astype(BF16)
        dcwx_ref[...] += d_w
        dcbx_ref[...] += d_b
        cx8[...] = c8
        d_raw, d_w, d_b, c8 = _conv_silu_bwd(dbm, f["cvb"], f["sgb"], ins["b"], f["bh"], ins["cwb"], cb8[...])
        db = d_raw.astype(BF16)
        dcwb_ref[...] += d_w
        dcbb_ref[...] += d_b
        cb8[...] = c8
        d_raw, d_w, d_b, c8 = _conv_silu_bwd(dcm, f["cvc"], f["sgc"], ins["c"], f["ch"], ins["cwc"], cc8[...])
        dc = d_raw.astype(BF16)
        dcwc_ref[...] += d_w
        dcbc_ref[...] += d_b
        cc8[...] = c8
        g_ = pl.program_id(0)
        pieces = [(0, RP, g_ * RP), (RP, RP, G * RP + g_ * RP), (2 * RP, SSD_N, 2 * G * RP + g_ * SSD_N),
                  (2 * RP + SSD_N, SSD_N, 2 * G * RP + G * SSD_N + g_ * SSD_N)]
        _stage_out(dpm_ref, stage, sems, g_ * NS + step, G * NS, (NS - 1 - step) * C, pieces, [dz, dx, db, dc])

    out_specs = [_ANY, pl.BlockSpec((None, C, R), lambda g, i: (g, NS - 1 - i, 0)),
                 sp["pr"], sp["pr"], sp["pr"], sp["cwx"], sp["cwn"], sp["cwn"], sp["cbx"], sp["cbn"], sp["cbn"], sp["cbx"]]
    out_shape = [jax.ShapeDtypeStruct((T, 2 * G * RP + 2 * G * SSD_N), BF16),
                 jax.ShapeDtypeStruct((G, T, R), F32),
                 jax.ShapeDtypeStruct((G, 1, R), F32), jax.ShapeDtypeStruct((G, 1, R), F32), jax.ShapeDtypeStruct((G, 1, R), F32),
                 jax.ShapeDtypeStruct((SSD_CONV_W, G * RP), F32), jax.ShapeDtypeStruct((SSD_CONV_W, G * SSD_N), F32),
                 jax.ShapeDtypeStruct((SSD_CONV_W, G * SSD_N), F32),
                 jax.ShapeDtypeStruct((1, G * RP), F32), jax.ShapeDtypeStruct((1, G * SSD_N), F32),
                 jax.ShapeDtypeStruct((1, G * SSD_N), F32), jax.ShapeDtypeStruct((1, G * RP), F32)]
    return pl.pallas_call(
        body, name=name, grid=(G, NS), in_specs=specs, out_specs=out_specs, out_shape=out_shape,
        scratch_shapes=[pltpu.VMEM((SSD_N, RP), F32), pltpu.VMEM((8, RP), F32), pltpu.VMEM((8, SSD_N), F32),
                        pltpu.VMEM((8, SSD_N), F32), pltpu.VMEM((2, C, 2 * RP + 2 * SSD_N), BF16),
                        pltpu.SemaphoreType.DMA((2, 4))],
        compiler_params=_cparams("arbitrary", "arbitrary"),
    )(*args)


_ANY = pl.BlockSpec(memory_space=pl.ANY)


def _chip_peer(k):
    x, y, c = lax.axis_index("x"), lax.axis_index("y"), lax.axis_index("c")
    return (x ^ (k >> 1), y ^ (k & 1), c)


def _my_chip():
    return 2 * lax.axis_index("x") + lax.axis_index("y")


def _all_gather_chips(shards, halved, *, name):
    n = len(shards)

    def body(*refs):
        ins, outs = refs[:n], refs[n:2 * n]
        send, recv, fsend, frecv, loc = refs[2 * n:]
        s = _my_chip()
        c = lax.axis_index("c")
        sibling = (lax.axis_index("x"), lax.axis_index("y"), 1 - c)
        copies = []
        for a in range(n):
            cp = pltpu.make_async_copy(ins[a], outs[a].at[s], loc.at[a])
            cp.start()
            copies.append(cp)

        def rows(a, core):
            if not halved[a]:
                return slice(None)
            half = shards[a].shape[0] // 2
            return pl.ds(pl.multiple_of(core * half, 16), half)

        def over_ici(a, k, slot, core):
            return pltpu.make_async_remote_copy(
                src_ref=ins[a].at[rows(a, core)], dst_ref=outs[a].at[slot, rows(a, core)],
                send_sem=send.at[3 * a + k - 1], recv_sem=recv.at[3 * a + k - 1],
                device_id=_chip_peer(k), device_id_type=MESH_ID)

        def over_d2d(a, k, core):
            z = outs[a].at[s ^ k, rows(a, core)]
            return pltpu.make_async_remote_copy(
                src_ref=z, dst_ref=z, send_sem=fsend.at[3 * a + k - 1], recv_sem=frecv.at[3 * a + k - 1],
                device_id=sibling, device_id_type=MESH_ID)

        sent = []
        for a in range(n):
            for k in (1, 2, 3):
                cp = over_ici(a, k, s, c)
                cp.start()
                sent.append(cp)
        passed = []
        for a in range(n):
            for k in (1, 2, 3):
                over_ici(a, k, s ^ k, c).wait_recv()
                if halved[a]:
                    cp = over_d2d(a, k, c)
                    cp.start()
                    passed.append(cp)
        for a in range(n):
            if halved[a]:
                for k in (1, 2, 3):
                    over_d2d(a, k, 1 - c).wait_recv()
        for cp in sent + passed:
            cp.wait_send()
        for cp in copies:
            cp.wait()

    for a, h in zip(shards, halved):
        assert not h or a.shape[0] % 32 == 0, a.shape
    return pl.pallas_call(
        body, name=name, in_specs=[_ANY] * n, out_specs=[_ANY] * n,
        out_shape=[jax.ShapeDtypeStruct((4,) + a.shape, a.dtype) for a in shards],
        scratch_shapes=[pltpu.SemaphoreType.DMA((3 * n,))] * 4 + [pltpu.SemaphoreType.DMA((n,))],
        compiler_params=pltpu.CompilerParams(has_side_effects=True),
    )(*shards)


_HBM = pl.BlockSpec(memory_space=pltpu.HBM)
_SEM = pl.BlockSpec(memory_space=pltpu.SEMAPHORE)
_EFFECT = pltpu.SideEffectType.DATAFLOW_SIDE_EFFECTING


def _split_copies(src, land, send, recv, loc, a, scatter):
    s = _my_chip()
    mine = pltpu.make_async_copy(src.at[s] if scatter else src, land.at[s], loc.at[a])
    pairs = []
    for k in (1, 2, 3):
        sems = dict(send_sem=send.at[3 * a + k - 1], recv_sem=recv.at[3 * a + k - 1],
                    device_id=_chip_peer(k), device_id_type=MESH_ID)
        out = pltpu.make_async_remote_copy(src_ref=src.at[s ^ k] if scatter else src, dst_ref=land.at[s], **sems)
        arriving = pltpu.make_async_remote_copy(src_ref=src.at[s ^ k] if scatter else src, dst_ref=land.at[s ^ k], **sems)
        pairs.append((out, arriving))
    return mine, pairs


def _split_start(arrs, *, scatter, after, name):
    n = len(arrs)
    zones = [lax.empty(a.shape if scatter else (4,) + a.shape, a.dtype) for a in arrs]

    def body(*refs):
        srcs, lands = refs[:n], refs[n:2 * n]
        send, recv, loc = refs[2 * n + 1:2 * n + 4]
        token = refs[-1]
        for a in range(n):
            mine, pairs = _split_copies(srcs[a], lands[a], send, recv, loc, a, scatter)
            mine.start()
            for out, _ in pairs:
                out.start()
        token[...] = jnp.zeros_like(token)

    res = pl.pallas_call(
        body, name=name,
        out_shape=(pltpu.SemaphoreType.DMA((3 * n,)), pltpu.SemaphoreType.DMA((3 * n,)), pltpu.SemaphoreType.DMA((n,)),
                   *[pltpu.HBM(z.shape, z.dtype) for z in zones], jax.ShapeDtypeStruct((8, 128), F32)),
        in_specs=[_ANY] * n + [_HBM] * n + [_ANY],
        out_specs=(_SEM, _SEM, _SEM, *([_HBM] * n), pl.BlockSpec(memory_space=pltpu.VMEM)),
        input_output_aliases={n + i: 3 + i for i in range(n)},
        compiler_params=pltpu.CompilerParams(has_side_effects=_EFFECT),
    )(*arrs, *[pltpu.with_memory_space_constraint(z, pltpu.HBM) for z in zones], after)
    return res[:3], list(res[3:3 + n]), res[-1]


def _split_wait(sems, src, land, a, *, scatter, after, name):
    def body(src_ref, land_ref, send, recv, loc, after_ref, land_out):
        mine, pairs = _split_copies(src_ref, land_ref, send, recv, loc, a, scatter)
        mine.wait()
        for out, arriving in pairs:
            out.wait_send()
            arriving.wait_recv()

    return pl.pallas_call(
        body, name=name, out_shape=pltpu.HBM(land.shape, land.dtype),
        in_specs=[_ANY, _HBM, _SEM, _SEM, _SEM, _ANY], out_specs=_HBM, input_output_aliases={1: 0},
        compiler_params=pltpu.CompilerParams(has_side_effects=_EFFECT),
    )(src, land, *sems, after)


def _sibling_copies(srcs, lands, send, recv):
    sib = (lax.axis_index("x"), lax.axis_index("y"), 1 - lax.axis_index("c"))
    return [pltpu.make_async_remote_copy(src_ref=srcs[a], dst_ref=lands[a], send_sem=send.at[a], recv_sem=recv.at[a],
                                         device_id=sib, device_id_type=MESH_ID) for a in range(len(srcs))]


def _swap_start(arrs, *, after, name):
    n = len(arrs)
    zones = [lax.empty(a.shape, a.dtype) for a in arrs]

    def body(*refs):
        for cp in _sibling_copies(refs[:n], refs[n:2 * n], refs[2 * n + 1], refs[2 * n + 2]):
            cp.start()

    res = pl.pallas_call(
        body, name=name,
        out_shape=(pltpu.SemaphoreType.DMA((n,)), pltpu.SemaphoreType.DMA((n,)), *[pltpu.HBM(z.shape, z.dtype) for z in zones]),
        in_specs=[_ANY] * n + [_HBM] * n + [_ANY], out_specs=(_SEM, _SEM, *([_HBM] * n)),
        input_output_aliases={n + i: 2 + i for i in range(n)},
        compiler_params=pltpu.CompilerParams(has_side_effects=_EFFECT),
    )(*arrs, *[pltpu.with_memory_space_constraint(z, pltpu.HBM) for z in zones], after)
    return res[:2], list(res[2:])


def _swap_wait(sems, srcs, lands, *, after, name):
    n = len(srcs)

    def body(*refs):
        for cp in _sibling_copies(refs[:n], refs[n:2 * n], refs[2 * n], refs[2 * n + 1]):
            cp.wait_send()
            cp.wait_recv()

    res = pl.pallas_call(
        body, name=name, out_shape=tuple(pltpu.HBM(a.shape, a.dtype) for a in lands),
        in_specs=[_ANY] * n + [_HBM] * n + [_SEM, _SEM, _ANY], out_specs=tuple([_HBM] * n),
        input_output_aliases={n + i: i for i in range(n)},
        compiler_params=pltpu.CompilerParams(has_side_effects=_EFFECT),
    )(*srcs, *lands, *sems, after)
    return list(res)


def _all_gather_devices(v, *, name):
    r = v.shape[0]

    def body(v_ref, out_ref, send, recv):
        x, y, c = lax.axis_index("x"), lax.axis_index("y"), lax.axis_index("c")
        me = 4 * x + 2 * y + c
        out_ref[me] = v_ref[...]
        cps = []
        for k in range(1, 8):
            peer = (x ^ (k >> 2), y ^ ((k >> 1) & 1), c ^ (k & 1))
            cp = pltpu.make_async_remote_copy(src_ref=v_ref, dst_ref=out_ref.at[me], send_sem=send.at[k - 1],
                                              recv_sem=recv.at[k - 1], device_id=peer, device_id_type=MESH_ID)
            cp.start()
            cps.append(cp)
        for k, cp in enumerate(cps, start=1):
            cp.wait_send()
            pltpu.make_async_remote_copy(src_ref=v_ref, dst_ref=out_ref.at[me ^ k], send_sem=send.at[k - 1],
                                         recv_sem=recv.at[k - 1], device_id=(x, y, c), device_id_type=MESH_ID).wait_recv()

    vm = pl.BlockSpec(memory_space=pltpu.VMEM)
    return pl.pallas_call(
        body, name=name, in_specs=[vm], out_specs=vm, out_shape=jax.ShapeDtypeStruct((8, r, 128), F32),
        scratch_shapes=[pltpu.SemaphoreType.DMA((7,)), pltpu.SemaphoreType.DMA((7,))],
        compiler_params=pltpu.CompilerParams(has_side_effects=True),
    )(v)


def _row_tile(r, target):
    best = None
    for t in range(16, min(target, r) + 1, 16):
        if r % t == 0:
            best = t
    return best or r


def _sum_slots(buf, *, name, tr=384, layer=None, stack=None):
    S, r, c = buf.shape
    tr = _row_tile(r, tr)

    def body(*refs):
        b_ref, o_ref = refs[0], refs[-1]
        acc = b_ref[0].astype(F32)
        for j in range(1, S):
            acc = acc + b_ref[j].astype(F32)
        o_ref[...] = acc

    in_specs = [pl.BlockSpec((S, tr, c), lambda i: (0, i, 0))]
    args, alias = [buf], {}
    if layer is None:
        out_spec, out_shape = pl.BlockSpec((tr, c), lambda i: (i, 0)), jax.ShapeDtypeStruct((r, c), F32)
    else:
        out_spec = pl.BlockSpec((None, tr, c), lambda i: (layer, i, 0))
        out_shape = jax.ShapeDtypeStruct((2, r, c), F32)
        if stack is not None:
            in_specs.append(_ANY)
            args.append(stack)
            alias = {1: 0}
    return pl.pallas_call(
        body, name=name, grid=(r // tr,), in_specs=in_specs, out_specs=out_spec, out_shape=out_shape,
        input_output_aliases=alias, compiler_params=_cparams("parallel"),
    )(*args)


ADAMW_BLOCK_ELEMS = 1 << 18


def _adamw(w, gs, m, v, *, name, tr=256):
    r, c = w.shape
    tr = _row_tile(r, min(tr, max(16, ADAMW_BLOCK_ELEMS // c)))
    bc1 = 1.0 - ADAM_B1 ** ADAM_STEP
    bc2 = 1.0 - ADAM_B2 ** ADAM_STEP
    ng = len(gs)

    def body(*refs):
        w_ref, m_ref, v_ref = refs[0], refs[1 + ng], refs[2 + ng]
        g_ref, d_ref, mo_ref, vo_ref = refs[3 + ng:]
        gg = refs[1][...] if ng == 1 else refs[1][...] + refs[2][...]
        mn = ADAM_B1 * m_ref[...] + (1.0 - ADAM_B1) * gg
        vn = ADAM_B2 * v_ref[...] + (1.0 - ADAM_B2) * (gg * gg)
        g_ref[...] = gg
        mo_ref[...] = mn
        vo_ref[...] = vn
        d_ref[...] = -ADAM_LR * ((mn / bc1) / (jnp.sqrt(vn / bc2) + ADAM_EPS) + ADAM_WD * w_ref[...])

    spec = pl.BlockSpec((tr, c), lambda i: (i, 0))
    return pl.pallas_call(body, name=name, grid=(r // tr,), in_specs=[spec] * (3 + ng), out_specs=[spec] * 4,
                          out_shape=[jax.ShapeDtypeStruct((r, c), F32)] * 4,
                          compiler_params=_cparams("parallel"))(w, *gs, m, v)


def _pack(vecs, rows):
    flat = jnp.concatenate([v.reshape(-1).astype(F32) for v in vecs])
    return jnp.pad(flat, (0, rows * 128 - flat.shape[0])).reshape(rows, 128)


def _unpack(packed, shapes):
    flat = packed.reshape(-1)
    out, off = [], 0
    for s in shapes:
        n = math.prod(s)
        out.append(flat[off:off + n].reshape(s))
        off += n
    return out


def _pack_rows(shapes):
    n = sum(math.prod(s) for s in shapes)
    return -(-n // 1024) * 8


def kernel(x, norm_mix_pre, norm_mix_post, norm_ffn_pre, norm_ffn_post, ret_w_in, ret_gn_w, ret_w_out, ssd_w_in, ssd_conv_w, ssd_conv_b, ssd_dt_bias, ssd_a_log, ssd_d, ssd_norm_w, ssd_w_out, mlp_w_up, mlp_w_down, loss_target, m_norm_mix_pre, m_norm_mix_post, m_norm_ffn_pre, m_norm_ffn_post, m_ret_w_in, m_ret_gn_w, m_ret_w_out, m_ssd_w_in, m_ssd_conv_w, m_ssd_conv_b, m_ssd_dt_bias, m_ssd_a_log, m_ssd_d, m_ssd_norm_w, m_ssd_w_out, m_mlp_w_up, m_mlp_w_down, v_norm_mix_pre, v_norm_mix_post, v_norm_ffn_pre, v_norm_ffn_post, v_ret_w_in, v_ret_gn_w, v_ret_w_out, v_ssd_w_in, v_ssd_conv_w, v_ssd_conv_b, v_ssd_dt_bias, v_ssd_a_log, v_ssd_d, v_ssd_norm_w, v_ssd_w_out, v_mlp_w_up, v_mlp_w_down):
    T, D = x.shape[1], x.shape[2]
    H = D // RET_DK
    d_inner = 2 * D
    R = d_inner // SSD_P // SSD_G
    RP = R * SSD_P
    n_heads = SSD_G * R
    conv_dim = d_inner + 2 * SSD_G * SSD_N
    n_main = d_inner + conv_dim
    C = min(256, T)
    chip = _my_chip()
    xs, tgt = x[0], loss_target[0]

    conv_sh = ssd_conv_w.shape[2]
    small_shapes = [(SSD_CONV_W, conv_sh), (conv_sh,), (ssd_norm_w.shape[1],)]
    small_rows = _pack_rows(small_shapes)
    shards = [ret_w_in[0].T.astype(BF16), ret_w_out[0].astype(BF16), ssd_w_in[0].T.astype(BF16), ssd_w_out[0].astype(BF16),
              mlp_w_up[0].T.astype(BF16), mlp_w_up[1].T.astype(BF16), mlp_w_down[0].astype(BF16), mlp_w_down[1].astype(BF16)]
    (ret_in_g, small_g) = _all_gather_chips([shards[0], _pack([ssd_conv_w[0], ssd_conv_b[0], ssd_norm_w[0]], small_rows)],
                                            [True, False], name="gather_first")

    def full(g):
        return g.reshape(4 * g.shape[1], g.shape[2])

    def start_gather(idx, after, name):
        sems, zones, tok = _split_start([shards[i] for i in idx], scatter=False, after=after, name=name)
        return {i: (sems, shards[i], zones[n], n) for n, i in enumerate(idx)}, tok

    def arrived(stage, i, after, name):
        sems, src, zone, n = stage[i]
        return full(_split_wait(sems, src, zone, n, scatter=False, after=after, name=name))

    ret_in_t = full(ret_in_g)
    sm = [_unpack(small_g[j], small_shapes) for j in range(4)]
    conv_w = jnp.concatenate([sm[j][0] for j in range(4)], axis=1)
    conv_b = jnp.concatenate([sm[j][1] for j in range(4)])[None, :]
    norm_w = jnp.concatenate([sm[j][2] for j in range(4)])[None, :]

    gb = SSD_G * SSD_N
    ssd_prm = (ssd_dt_bias.reshape(SSD_G, 1, R), ssd_dt_bias.reshape(SSD_G, R, 1),
               ssd_a_log.reshape(SSD_G, 1, R), ssd_a_log.reshape(SSD_G, R, 1), ssd_d.reshape(SSD_G, 1, R),
               conv_w[:, :d_inner], conv_w[:, d_inner:d_inner + gb], conv_w[:, d_inner + gb:],
               conv_b[:, :d_inner], conv_b[:, d_inner:d_inner + gb], conv_b[:, d_inner + gb:],
               norm_w, jnp.asarray(np.kron(np.eye(R), np.ones((1, SSD_P))), F32))
    ret_consts = _ret_consts(T, C, H)

    u0 = _rms_pre(xs, norm_mix_pre[0:1], name="pre0")
    stage1, tok = start_gather((1, 4), ret_in_g, "gather_start1")
    proj = _matmul(u0, ret_in_t, "nt", out_dtype=F32, name="ret_in", after=tok)
    stage2, tok = start_gather((6, 2), proj, "gather_start2")
    y_ret, st_ret = _ret_fwd(proj, ret_gn_w, ret_consts, C=C, name="ret_fwd")
    ret_out = arrived(stage1, 1, y_ret, "gather_wait_ret_out")
    m0 = _matmul(y_ret, ret_out, "nn", out_dtype=F32, name="ret_out", after=tok)
    h1, u1 = _rms_post_pre(xs, m0, norm_mix_post[0:1], norm_ffn_pre[0:1], name="post_pre1")
    up_t0 = arrived(stage1, 4, u1, "gather_wait_up0")
    a0, hh0 = _matmul(u1, up_t0, "nt", out_dtype=BF16, name="mlp_up0", epi="relu2")
    stage3, tok = start_gather((3, 5, 7), hh0, "gather_start3")
    down0 = arrived(stage2, 6, hh0, "gather_wait_down0")
    f0 = _matmul(hh0, down0, "nn", out_dtype=F32, name="mlp_down0", after=tok)
    h2, u2 = _rms_post_pre(h1, f0, norm_ffn_post[0:1], norm_mix_pre[1:2], name="post_pre2")
    ssd_in_t = arrived(stage2, 2, u2, "gather_wait_ssd_in")
    pm, pdt = _matmul(u2, ssd_in_t, "nt", out_dtype=F32, name="ssd_in", tail=(n_heads,))
    dt_g = pdt.reshape(T, SSD_G, R).transpose(1, 0, 2)
    dtT_g = pdt.reshape(T, SSD_G, R).transpose(1, 2, 0)
    y_ssd, st_ssd = _ssd_fwd(pm, dt_g, dtT_g, ssd_prm, C=C, R=R, name="ssd_fwd")
    ssd_out = arrived(stage3, 3, y_ssd, "gather_wait_ssd_out")
    m1 = _matmul(y_ssd, ssd_out, "nn", out_dtype=F32, name="ssd_out")
    h3, u3 = _rms_post_pre(h2, m1, norm_mix_post[1:2], norm_ffn_pre[1:2], name="post_pre3")
    up_t1 = arrived(stage3, 5, u3, "gather_wait_up1")
    a1, hh1 = _matmul(u3, up_t1, "nt", out_dtype=BF16, name="mlp_up1", epi="relu2")
    down1 = arrived(stage3, 7, hh1, "gather_wait_down1")
    f1 = _matmul(hh1, down1, "nn", out_dtype=F32, name="mlp_down1")
    up_t, down = (up_t0, up_t1), (down0, down1)
    dh4, sq = _rms_post_loss(h3, f1, norm_ffn_post[1:2], tgt, name="post_loss")
    loss = lax.psum(sq[0, 0], MESH_AXES) * (0.5 / D)

    in_flight = []

    def send_grad(g, name):
        part = g if g.ndim == 3 else g.reshape(4, g.shape[0] // 4, g.shape[1])
        sems, zones, tok = _split_start([part], scatter=True, after=part, name=f"scatter_start_{name}")
        in_flight.append((name, sems, part, zones[0]))
        return tok

    def mlp_bwd(i, df, u, a, hh):
        tok = send_grad(_matmul(hh, df, "tn", out_dtype=BF16, name=f"mlp_down_wg{i}"), f"down{i}")
        da = _matmul(df, down[i], "nt", out_dtype=BF16, name=f"mlp_down_dg{i}", epi="drelu2", extra=a, after=tok)
        tok = send_grad(_matmul(u, da, "tn", out_dtype=BF16, name=f"mlp_up_wg{i}", col_parts=4), f"up{i}")
        return _matmul(da, up_t[i], "nn", out_dtype=F32, name=f"mlp_up_dg{i}", after=tok)

    df1, d_nfpost1 = _rms_post_bwd(f1, norm_ffn_post[1:2], dh4, name="post_bwd_ffn1")
    du3 = mlp_bwd(1, df1, u3, a1, hh1)
    dh3, dm1, d_nfp1, d_nmpost1 = _rms_pre_post_bwd(h3, norm_ffn_pre[1:2], du3, dh4, m1, norm_mix_post[1:2],
                                                    name="pre_bwd_ffn1_post_bwd_mix1")
    tok = send_grad(_matmul(y_ssd, dm1, "tn", out_dtype=BF16, name="ssd_out_wg"), "ssd_out")
    dy_ssd = _matmul(dm1, ssd_out, "nt", out_dtype=F32, name="ssd_out_dg", after=tok)
    (dpm, ddt_g, d_bias, d_alog, d_dskip, dcwx, dcwb, dcwc, dcbx, dcbb, dcbc, d_normw) = _ssd_bwd(
        pm, dt_g, dtT_g, ssd_prm, st_ssd, dy_ssd, C=C, R=R, name="ssd_bwd")
    dpdt = ddt_g.transpose(1, 0, 2).reshape(T, n_heads).astype(BF16)
    tok = send_grad(jnp.concatenate([_matmul(dpm, u2, "tn", out_dtype=BF16, name="ssd_in_wg"),
                                     _matmul(dpdt, u2, "tn", out_dtype=BF16, name="ssd_in_dt_wg")], axis=0), "ssd_in")
    du2 = _matmul(dpm, ssd_in_t, "nn", out_dtype=F32, name="ssd_in_dg", after=tok, tail=(n_heads, dpdt))
    dh2, df0, d_nmp1, d_nfpost0 = _rms_pre_post_bwd(h2, norm_mix_pre[1:2], du2, dh3, f0, norm_ffn_post[0:1],
                                                    name="pre_bwd_mix1_post_bwd_ffn0")
    du1 = mlp_bwd(0, df0, u1, a0, hh0)
    dh1, dm0, d_nfp0, d_nmpost0 = _rms_pre_post_bwd(h1, norm_ffn_pre[0:1], du1, dh2, m0, norm_mix_post[0:1],
                                                    name="pre_bwd_ffn0_post_bwd_mix0")
    tok = send_grad(_matmul(y_ret, dm0, "tn", out_dtype=BF16, name="ret_out_wg"), "ret_out")
    dy_ret = _matmul(dm0, ret_out, "nt", out_dtype=F32, name="ret_out_dg", after=tok)
    dproj, d_gn = _ret_bwd(proj, ret_gn_w, ret_consts, st_ret, dy_ret, C=C, name="ret_bwd")
    tok = send_grad(_matmul(u0, dproj, "tn", out_dtype=BF16, name="ret_in_wg", col_parts=4), "ret_in")
    du0 = _matmul(dproj, ret_in_t, "nn", out_dtype=F32, name="ret_in_dg", after=tok)
    grad_x, d_nmp0 = _rms_pre_bwd(xs, norm_mix_pre[0:1], du0, dh1, name="pre_bwd_mix0")

    landed = {nm: _split_wait(sems, src, zone, 0, scatter=True, after=grad_x, name=f"scatter_wait_{nm}")
              for nm, sems, src, zone in in_flight}
    part_sum = {}

    def two_d(t):
        return t.reshape(-1, t.shape[-1])

    def sum_chips(nm):
        if nm[-1] in "01" and nm[:-1] in ("up", "down"):
            fam, layer = nm[:-1], int(nm[-1])
            part_sum[fam] = _sum_slots(landed[nm], name=f"sum_chips_{nm}", layer=layer, stack=part_sum.get(fam))
        else:
            part_sum[nm] = _sum_slots(landed[nm], name=f"sum_chips_{nm}")

    swaps = []
    for names_, fams in ((("down1", "up1", "ssd_out", "ssd_in", "down0"), ("ssd_out", "ssd_in", "down")),
                         (("up0", "ret_out", "ret_in"), ("up", "ret_out", "ret_in"))):
        for nm in names_:
            sum_chips(nm)
        mine = [two_d(part_sum[f]) for f in fams]
        swaps.append((fams, mine, _swap_start(mine, after=mine[-1], name=f"swap_start{len(swaps)}")))

    def upd(w, gs, m, v, name):
        shp = w.shape
        w2, m2, v2 = (two_d(t) for t in (w, m, v))
        return tuple(t.reshape(shp) for t in _adamw(w2, [g.reshape(w2.shape) for g in gs], m2, v2, name=name))

    def upd_t(w, gs, m, v, name):
        return tuple(t.T[None] for t in _adamw(w[0].T, gs, m[0].T, v[0].T, name=name))

    todo = {"ret_in": (upd, ret_w_in, m_ret_w_in, v_ret_w_in, "ret_w_in"),
            "ret_out": (upd, ret_w_out, m_ret_w_out, v_ret_w_out, "ret_w_out"),
            "ssd_in": (upd_t, ssd_w_in, m_ssd_w_in, v_ssd_w_in, "ssd_w_in"),
            "ssd_out": (upd, ssd_w_out, m_ssd_w_out, v_ssd_w_out, "ssd_w_out"),
            "up": (upd, mlp_w_up, m_mlp_w_up, v_mlp_w_up, "mlp_w_up"),
            "down": (upd, mlp_w_down, m_mlp_w_down, v_mlp_w_down, "mlp_w_down")}
    res = {}
    prev = grad_x
    for n_, (fams, srcs, (sems, zones)) in enumerate(swaps):
        theirs = _swap_wait(sems, srcs, zones, after=prev, name=f"swap_wait{n_}")
        for f, mine, other in zip(fams, srcs, theirs):
            fn, w, m, v, out_name = todo[f]
            res[out_name] = fn(w, [mine, other], m, v, f"adamw_{f}")
            prev = res[out_name][0]

    d_conv_w = jnp.concatenate([dcwx, dcwb, dcwc], axis=1)
    d_conv_b = jnp.concatenate([dcbx, dcbb, dcbc], axis=1)
    small_grads = [jnp.concatenate([d_nmp0, d_nmp1]), jnp.concatenate([d_nmpost0, d_nmpost1]),
                   jnp.concatenate([d_nfp0, d_nfp1]), jnp.concatenate([d_nfpost0, d_nfpost1]),
                   d_gn, d_bias.reshape(1, n_heads), d_alog.reshape(1, n_heads), d_dskip.reshape(1, n_heads),
                   d_conv_w, d_conv_b, d_normw]
    sg_shapes = [g.shape for g in small_grads]
    sg_rows = _pack_rows(sg_shapes)
    everyone = _all_gather_devices(_pack(small_grads, sg_rows), name="gather_small_grads")
    sg = _unpack(_sum_slots(everyone, name="sum_small_grads", tr=sg_rows), sg_shapes)
    (g_nmp, g_nmpost, g_nfp, g_nfpost, g_gn, g_bias, g_alog, g_dskip, g_cw_full, g_cb_full, g_nw_full) = sg
    g_cw = lax.dynamic_slice_in_dim(g_cw_full, chip * conv_sh, conv_sh, axis=1)[None]
    g_cb = lax.dynamic_slice_in_dim(g_cb_full, chip * conv_sh, conv_sh, axis=1)
    nw_sh = ssd_norm_w.shape[1]
    g_nw = lax.dynamic_slice_in_dim(g_nw_full, chip * nw_sh, nw_sh, axis=1)
    small = [("norm_mix_pre", norm_mix_pre, g_nmp, m_norm_mix_pre, v_norm_mix_pre),
             ("norm_mix_post", norm_mix_post, g_nmpost, m_norm_mix_post, v_norm_mix_post),
             ("norm_ffn_pre", norm_ffn_pre, g_nfp, m_norm_ffn_pre, v_norm_ffn_pre),
             ("norm_ffn_post", norm_ffn_post, g_nfpost, m_norm_ffn_post, v_norm_ffn_post),
             ("ret_gn_w", ret_gn_w, g_gn, m_ret_gn_w, v_ret_gn_w),
             ("ssd_conv_w", ssd_conv_w, g_cw, m_ssd_conv_w, v_ssd_conv_w),
             ("ssd_conv_b", ssd_conv_b, g_cb, m_ssd_conv_b, v_ssd_conv_b),
             ("ssd_dt_bias", ssd_dt_bias, g_bias, m_ssd_dt_bias, v_ssd_dt_bias),
             ("ssd_a_log", ssd_a_log, g_alog, m_ssd_a_log, v_ssd_a_log),
             ("ssd_d", ssd_d, g_dskip, m_ssd_d, v_ssd_d),
             ("ssd_norm_w", ssd_norm_w, g_nw, m_ssd_norm_w, v_ssd_norm_w)]
    sw_shapes = [w.shape for _, w, _, _, _ in small]
    sw_rows = _pack_rows(sw_shapes)
    packs = [_pack([t[j] for t in small], sw_rows) for j in (1, 2, 3, 4)]
    _, d_p, m_p, v_p = _adamw(packs[0], [packs[1]], packs[2], packs[3], name="adamw_small", tr=sw_rows)
    d_s, m_s, v_s = _unpack(d_p, sw_shapes), _unpack(m_p, sw_shapes), _unpack(v_p, sw_shapes)
    for j, (nm, w, g, _, _) in enumerate(small):
        res[nm] = (g.reshape(w.shape), d_s[j], m_s[j], v_s[j])

    order = ["norm_mix_pre", "norm_mix_post", "norm_ffn_pre", "norm_ffn_post", "ret_w_in", "ret_gn_w", "ret_w_out",
             "ssd_w_in", "ssd_conv_w", "ssd_conv_b", "ssd_dt_bias", "ssd_a_log", "ssd_d", "ssd_norm_w", "ssd_w_out",
             "mlp_w_up", "mlp_w_down"]
    return (loss, grad_x[None], *[res[n][0] for n in order], *[res[n][1] for n in order],
            *[res[n][2] for n in order], *[res[n][3] for n in order])
```

```python
import math

import numpy as np
import jax
import jax.numpy as jnp
from jax import lax
from jax.experimental import pallas as pl
from jax.experimental.pallas import tpu as pltpu

F32 = jnp.float32
BF16 = jnp.bfloat16
VMEM_LIMIT_BYTES = 56 * 1024 * 1024
MESH_AXES = ("x", "y", "c")
MESH_ID = pl.DeviceIdType.MESH

RMS_EPS = 1e-6
GN_EPS = 1e-5
RET_DK = 256
RET_DV = 512
ROPE_BASE = 10000.0
REF_CHUNK = 64
SSD_P = 64
SSD_N = 128
SSD_G = 8
SSD_CONV_W = 4
ADAM_LR, ADAM_B1, ADAM_B2, ADAM_EPS, ADAM_WD, ADAM_STEP = 0.001, 0.9, 0.999, 1e-08, 0.01, 10

NN = (((1,), (0,)), ((), ()))
NT = (((1,), (1,)), ((), ()))
TN = (((0,), (0,)), ((), ()))


def _cparams(*sem):
    return pltpu.CompilerParams(dimension_semantics=sem, vmem_limit_bytes=VMEM_LIMIT_BYTES)


def _dot(a, b, dims=NN):
    return lax.dot_general(a.astype(BF16), b.astype(BF16), dims, preferred_element_type=F32)


def _split_bf16(x, terms):
    parts, rest = [], x
    for _ in range(terms):
        p = rest.astype(BF16)
        parts.append(p)
        rest = rest - p.astype(F32)
    return parts


def _dot_sel(a, b, dims=NN, *, split, terms=3):
    if split == "a":
        sel = b.astype(BF16)
        return sum(lax.dot_general(p, sel, dims, preferred_element_type=F32) for p in _split_bf16(a, terms))
    sel = a.astype(BF16)
    return sum(lax.dot_general(sel, p, dims, preferred_element_type=F32) for p in _split_bf16(b, terms))


def _sigmoid(x):
    return 1.0 / (1.0 + jnp.exp(-x))


def _colsum(x):
    return jnp.sum(x, axis=0, keepdims=True)


MM_TILE = 1024
MM_FULL_K = 2048


def _mm_tiles(M, N, K):
    if K <= MM_FULL_K:
        return min(M, 2 * MM_TILE), min(N, MM_TILE), K
    return min(M, MM_TILE), min(N, 2 * MM_TILE), MM_TILE


def _matmul(a, b, mode, *, out_dtype, name, epi=None, extra=None, after=None, col_parts=None, tail=None):
    nt_ = tail[0] if tail else 0
    if mode == "nn":
        (M, K), (K2, N) = a.shape, (b.shape[0] - nt_, b.shape[1])
    elif mode == "nt":
        (M, K), (N, K2) = a.shape, (b.shape[0] - nt_, b.shape[1])
    else:
        (K, M), (K2, N) = a.shape, b.shape
    assert K == K2, (a.shape, b.shape, mode)
    tm, tn, tk = _mm_tiles(M, N, K)
    if col_parts:
        tm, tn = min(M, 2 * MM_TILE), min(tn, MM_TILE)
        while (N // col_parts) % tn:
            tn //= 2
    assert M % tm == 0 and N % tn == 0 and K % tk == 0, (M, N, K, tm, tn, tk)
    nk = K // tk
    if mode == "tn":
        a_spec = pl.BlockSpec((tk, tm), lambda i, j, k: (k, i))
    else:
        a_spec = pl.BlockSpec((tm, tk), lambda i, j, k: (i, k))
    if mode == "nt":
        b_spec = pl.BlockSpec((tn, tk), lambda i, j, k: (j, k))
    else:
        b_spec = pl.BlockSpec((tk, tn), lambda i, j, k: (k, j))
    dims = {"nn": NN, "nt": NT, "tn": TN}[mode]
    o_spec = pl.BlockSpec((tm, tn), lambda i, j, k: (i, j))
    out_dims = (M, N)
    if col_parts:
        per = N // col_parts // tn
        o_spec = pl.BlockSpec((None, tm, tn), lambda i, j, k: (j // per, i, j % per))
        out_dims = (col_parts, M, N // col_parts)
    has_extra = epi in ("drelu2", "add")
    n_out = 2 if epi == "relu2" else 1

    in_specs = [a_spec, b_spec] + ([o_spec] if has_extra else [])
    args = [a, b] + ([extra] if has_extra else [])
    if after is not None:
        in_specs.append(pl.BlockSpec(after.shape, lambda i, j, k: (0, 0)))
        args.append(after)
    n_plain = len(args)
    out_specs = [o_spec] * n_out
    out_shape = [jax.ShapeDtypeStruct(out_dims, out_dtype)] * n_out
    if tail and mode == "nt":
        assert nk == 1 and N % nt_ == 0
        in_specs.append(pl.BlockSpec((nt_, tk), lambda i, j, k: (N // nt_, 0)))
        args.append(b)
        out_specs.append(pl.BlockSpec((tm, nt_), lambda i, j, k: (i, 0)))
        out_shape.append(jax.ShapeDtypeStruct((M, nt_), F32))
    elif tail:
        assert mode == "nn" and K % nt_ == 0
        in_specs += [pl.BlockSpec((tm, nt_), lambda i, j, k: (i, 0)), pl.BlockSpec((nt_, tn), lambda i, j, k: (K // nt_, j))]
        args += [tail[1], b]
    n_in = len(args)

    def body(*refs):
        a_ref, b_ref = refs[0], refs[1]
        e_ref = refs[2] if has_extra else None
        outs = refs[n_in:n_in + n_out]

        def finish(r):
            if tail and mode == "nn":
                r = r + _dot(refs[n_plain][...], refs[n_plain + 1][...])
            if epi is None:
                outs[0][...] = r.astype(outs[0].dtype)
            elif epi == "relu2":
                outs[0][...] = r.astype(outs[0].dtype)
                h = jnp.maximum(r, 0.0)
                outs[1][...] = (h * h).astype(outs[1].dtype)
            elif epi == "drelu2":
                av = jnp.maximum(e_ref[...].astype(F32), 0.0)
                outs[0][...] = (r * (2.0 * av)).astype(outs[0].dtype)
            else:
                outs[0][...] = (r + e_ref[...].astype(F32)).astype(outs[0].dtype)

        if tail and mode == "nt":
            @pl.when(pl.program_id(1) == 0)
            def _():
                refs[n_in + n_out][...] = _dot(a_ref[...], refs[n_plain][...], NT)

        if nk == 1:
            finish(_dot(a_ref[...], b_ref[...], dims))
            return
        acc = refs[-1]
        k = pl.program_id(2)

        @pl.when(k == 0)
        def _():
            acc[...] = jnp.zeros_like(acc)

        acc[...] += _dot(a_ref[...], b_ref[...], dims)

        @pl.when(k == nk - 1)
        def _():
            finish(acc[...])

    res = pl.pallas_call(
        body, name=name, grid=(M // tm, N // tn, nk), in_specs=in_specs, out_specs=out_specs,
        out_shape=out_shape, scratch_shapes=[pltpu.VMEM((tm, tn), F32)] if nk > 1 else [],
        compiler_params=_cparams("parallel", "arbitrary" if tail and mode == "nt" else "parallel", "arbitrary"),
    )(*args)
    return res if len(res) > 1 else res[0]


def _rstd(x):
    return lax.rsqrt(jnp.mean(x * x, axis=-1, keepdims=True) + RMS_EPS)


def _row_call(body, ins, outs_shape, *, name, rows, tr, acc_outs=()):
    tr = min(tr, rows)
    assert rows % tr == 0
    in_specs = []
    for arr, blocked in ins:
        if blocked:
            in_specs.append(pl.BlockSpec((tr, arr.shape[1]), lambda i: (i, 0)))
        else:
            in_specs.append(pl.BlockSpec(arr.shape, lambda i: (0, 0)))
    out_specs = []
    for n, s in enumerate(outs_shape):
        if n in acc_outs:
            out_specs.append(pl.BlockSpec(s.shape, lambda i: (0, 0)))
        else:
            out_specs.append(pl.BlockSpec((tr, s.shape[1]), lambda i: (i, 0)))
    return pl.pallas_call(
        body, name=name, grid=(rows // tr,), in_specs=in_specs, out_specs=out_specs, out_shape=outs_shape,
        compiler_params=_cparams("arbitrary" if acc_outs else "parallel"),
    )(*[a for a, _ in ins])


def _rms_pre(h, w, *, name):
    T, D = h.shape

    def body(h_ref, w_ref, u_ref):
        x = h_ref[...]
        u_ref[...] = (x * _rstd(x) * w_ref[...]).astype(BF16)

    return _row_call(body, [(h, True), (w, False)], [jax.ShapeDtypeStruct((T, D), BF16)], name=name, rows=T, tr=256)[0]


def _rms_post_pre(h, m, w_post, w_pre, *, name):
    T, D = h.shape

    def body(h_ref, m_ref, wp_ref, wn_ref, hn_ref, u_ref):
        mm = m_ref[...].astype(F32)
        hn = h_ref[...] + mm * _rstd(mm) * wp_ref[...]
        hn_ref[...] = hn
        u_ref[...] = (hn * _rstd(hn) * wn_ref[...]).astype(BF16)

    return _row_call(body, [(h, True), (m, True), (w_post, False), (w_pre, False)],
                     [jax.ShapeDtypeStruct((T, D), F32), jax.ShapeDtypeStruct((T, D), BF16)], name=name, rows=T, tr=256)


def _rms_post_loss(h, m, w_post, tgt, *, name):
    T, D = h.shape

    def body(h_ref, m_ref, wp_ref, t_ref, dh_ref, loss_ref):
        @pl.when(pl.program_id(0) == 0)
        def _():
            loss_ref[...] = jnp.zeros_like(loss_ref)

        mm = m_ref[...].astype(F32)
        err = h_ref[...] + mm * _rstd(mm) * wp_ref[...] - t_ref[...]
        dh_ref[...] = err * (1.0 / D)
        loss_ref[...] += _colsum(jnp.sum(err * err, axis=1, keepdims=True))

    return _row_call(body, [(h, True), (m, True), (w_post, False), (tgt, True)],
                     [jax.ShapeDtypeStruct((T, D), F32), jax.ShapeDtypeStruct((1, 1), F32)],
                     name=name, rows=T, tr=256, acc_outs=(1,))


def _rms_bwd_vals(x, w, dy):
    r = _rstd(x)
    xh = x * r
    g = dy * w
    dx = r * (g - xh * jnp.mean(g * xh, axis=-1, keepdims=True))
    return dx, _colsum(dy * xh)


def _rms_post_bwd(m, w_post, dh, *, name):
    T, D = m.shape

    def body(m_ref, w_ref, dh_ref, dm_ref, dw_ref):
        @pl.when(pl.program_id(0) == 0)
        def _():
            dw_ref[...] = jnp.zeros_like(dw_ref)

        dx, dw = _rms_bwd_vals(m_ref[...].astype(F32), w_ref[...], dh_ref[...])
        dm_ref[...] = dx.astype(BF16)
        dw_ref[...] += dw

    return _row_call(body, [(m, True), (w_post, False), (dh, True)],
                     [jax.ShapeDtypeStruct((T, D), BF16), jax.ShapeDtypeStruct((1, D), F32)],
                     name=name, rows=T, tr=256, acc_outs=(1,))


def _rms_pre_bwd(h, w_pre, du, dh_out, *, name):
    T, D = h.shape

    def body(h_ref, w_ref, du_ref, dho_ref, dh_ref, dw_ref):
        @pl.when(pl.program_id(0) == 0)
        def _():
            dw_ref[...] = jnp.zeros_like(dw_ref)

        dx, dw = _rms_bwd_vals(h_ref[...], w_ref[...], du_ref[...].astype(F32))
        dh_ref[...] = dho_ref[...] + dx
        dw_ref[...] += dw

    return _row_call(body, [(h, True), (w_pre, False), (du, True), (dh_out, True)],
                     [jax.ShapeDtypeStruct((T, D), F32), jax.ShapeDtypeStruct((1, D), F32)],
                     name=name, rows=T, tr=256, acc_outs=(1,))


def _rms_pre_post_bwd(h, w_pre, du, dh_out, m_prev, w_post_prev, *, name):
    T, D = h.shape

    def body(h_ref, w_ref, du_ref, dho_ref, m_ref, wp_ref, dh_ref, dm_ref, dw_ref, dwp_ref):
        @pl.when(pl.program_id(0) == 0)
        def _():
            dw_ref[...] = jnp.zeros_like(dw_ref)
            dwp_ref[...] = jnp.zeros_like(dwp_ref)

        dx, dw = _rms_bwd_vals(h_ref[...], w_ref[...], du_ref[...].astype(F32))
        dh = dho_ref[...] + dx
        dh_ref[...] = dh
        dw_ref[...] += dw
        dm, dwp = _rms_bwd_vals(m_ref[...].astype(F32), wp_ref[...], dh)
        dm_ref[...] = dm.astype(BF16)
        dwp_ref[...] += dwp

    return _row_call(body, [(h, True), (w_pre, False), (du, True), (dh_out, True), (m_prev, True), (w_post_prev, False)],
                     [jax.ShapeDtypeStruct((T, D), F32), jax.ShapeDtypeStruct((T, D), BF16),
                      jax.ShapeDtypeStruct((1, D), F32), jax.ShapeDtypeStruct((1, D), F32)],
                     name=name, rows=T, tr=256, acc_outs=(2, 3))


def _ret_consts(T, C, H):
    lg = np.log1p(-np.exp2(-5.0 - np.arange(H, dtype=np.float64)))
    idx = np.arange(C, dtype=np.float64)
    dist = np.abs(idx[:, None] - idx[None, :])
    vis = (idx[None, :] // REF_CHUNK) <= (idx[:, None] // REF_CHUNK)
    mask = np.exp(dist[None] * lg[:, None, None]) * vis[None]
    xi = np.exp((idx[None, :] + 1.0) * lg[:, None])[..., None]
    zeta = np.exp((C - 1.0 - idx)[None, :] * lg[:, None])[..., None]
    half = RET_DK // 2
    inv_freq = ROPE_BASE ** (-np.arange(half, dtype=np.float32) / np.float32(half))
    ang = np.arange(T, dtype=np.float32)[:, None] * inv_freq[None, :].astype(np.float32)
    return (jnp.asarray(mask, F32), jnp.asarray(xi, F32), jnp.asarray(zeta, F32),
            jnp.asarray(np.cos(ang), F32), jnp.asarray(np.sin(ang), F32))


def _rot(t, cos, sin):
    half = RET_DK // 2
    t1, t2 = t[:, :half], t[:, half:]
    return jnp.concatenate([t1 * cos - t2 * sin, t1 * sin + t2 * cos], axis=1)


def _unrot(d, cos, sin):
    half = RET_DK // 2
    d1, d2 = d[:, :half], d[:, half:]
    return jnp.concatenate([d1 * cos + d2 * sin, d2 * cos - d1 * sin], axis=1)


def _ret_specs(C, H, rev, NS):
    def ci(i):
        return NS - 1 - i if rev else i

    nq = H
    q_spec = pl.BlockSpec((C, RET_DK), lambda h, i: (ci(i), h))
    k_spec = pl.BlockSpec((C, RET_DK), lambda h, i: (ci(i), nq + h))
    v_spec = pl.BlockSpec((C, RET_DV), lambda h, i: (ci(i), H + h))
    g_spec = pl.BlockSpec((C, RET_DV), lambda h, i: (ci(i), 2 * H + h))
    cs_spec = pl.BlockSpec((C, RET_DK // 2), lambda h, i: (ci(i), 0))
    m_spec = pl.BlockSpec((None, C, C), lambda h, i: (h, 0, 0))
    vec_spec = pl.BlockSpec((None, C, 1), lambda h, i: (h, 0, 0))
    gn_spec = pl.BlockSpec((1, RET_DV), lambda h, i: (0, h))
    st_spec = pl.BlockSpec((None, None, RET_DK, RET_DV), lambda h, i: (h, ci(i), 0, 0))
    return q_spec, k_spec, v_spec, g_spec, cs_spec, m_spec, vec_spec, gn_spec, st_spec


def _ret_fwd_vals(q, k, v, cos, sin, mask, xi, s_in):
    qr = _rot(q, cos, sin)
    kr = _rot(k, cos, sin) * (RET_DK ** -0.5)
    a = _dot(qr, kr, NT) * mask
    o = _dot(a, v) + _dot(qr, s_in) * xi
    mu = jnp.mean(o, axis=1, keepdims=True)
    oc = o - mu
    rstd = lax.rsqrt(jnp.mean(oc * oc, axis=1, keepdims=True) + GN_EPS)
    return qr, kr, a, oc * rstd, rstd


def _ret_fwd(proj, gn_w, consts, *, C, name):
    T = proj.shape[0]
    H = gn_w.shape[1] // RET_DV
    NS = T // C
    mask, xi, zeta, cos, sin = consts
    q_spec, k_spec, v_spec, g_spec, cs_spec, m_spec, vec_spec, gn_spec, st_spec = _ret_specs(C, H, False, NS)
    y_spec = pl.BlockSpec((C, RET_DV), lambda h, i: (i, h))

    def body(q_ref, k_ref, v_ref, g_ref, cos_ref, sin_ref, m_ref, xi_ref, ze_ref, gn_ref, y_ref, st_ref, S):
        @pl.when(pl.program_id(1) == 0)
        def _():
            S[...] = jnp.zeros_like(S)

        s_in = S[...]
        st_ref[...] = s_in
        v = v_ref[...]
        xi_v = xi_ref[...]
        qr, kr, a, on, rstd = _ret_fwd_vals(q_ref[...], k_ref[...], v, cos_ref[...], sin_ref[...], m_ref[...], xi_v, s_in)
        g = g_ref[...]
        y_ref[...] = (g * _sigmoid(g) * on * gn_ref[...]).astype(BF16)
        S[...] = s_in * xi_v[C - 1:C, :] + _dot(kr * ze_ref[...], v, TN)

    return pl.pallas_call(
        body, name=name, grid=(H, NS),
        in_specs=[q_spec, k_spec, v_spec, g_spec, cs_spec, cs_spec, m_spec, vec_spec, vec_spec, gn_spec],
        out_specs=[y_spec, st_spec],
        out_shape=[jax.ShapeDtypeStruct((T, H * RET_DV), BF16), jax.ShapeDtypeStruct((H, NS, RET_DK, RET_DV), F32)],
        scratch_shapes=[pltpu.VMEM((RET_DK, RET_DV), F32)],
        compiler_params=_cparams("parallel", "arbitrary"),
    )(proj, proj, proj, proj, cos, sin, mask, xi, zeta, gn_w)


def _stage_out(out_hbm, stage, sems, step, n_steps, row0, pieces, values):
    C = stage.shape[1]
    slot = step % 2

    def copies(sl):
        return [pltpu.make_async_copy(stage.at[sl, :, pl.ds(c0, w)],
                                      out_hbm.at[pl.ds(pl.multiple_of(row0, 16), C), pl.ds(pl.multiple_of(dc, 128), w)],
                                      sems.at[sl, n]) for n, (c0, w, dc) in enumerate(pieces)]

    @pl.when(step >= 2)
    def _():
        for cp in copies(slot):
            cp.wait()

    for (c0, w, _), v in zip(pieces, values):
        stage[slot, :, c0:c0 + w] = v
    for cp in copies(slot):
        cp.start()

    @pl.when(step == n_steps - 1)
    def _():
        for cp in copies(slot):
            cp.wait()
        if n_steps >= 2:
            for cp in copies(1 - slot):
                cp.wait()


def _ret_bwd(proj, gn_w, consts, states, dy, *, C, name):
    T = proj.shape[0]
    H = gn_w.shape[1] // RET_DV
    NS = T // C
    mask, xi, zeta, cos, sin = consts
    q_spec, k_spec, v_spec, g_spec, cs_spec, m_spec, vec_spec, gn_spec, st_spec = _ret_specs(C, H, True, NS)
    dy_spec = pl.BlockSpec((C, RET_DV), lambda h, i: (NS - 1 - i, h))
    scale = RET_DK ** -0.5
    wq, wv = H * RET_DK, H * RET_DV

    def body(q_ref, k_ref, v_ref, g_ref, cos_ref, sin_ref, m_ref, xi_ref, ze_ref, gn_ref, st_ref, dy_ref,
             dproj_ref, dgn_ref, dS, stage, sems):
        @pl.when(pl.program_id(1) == 0)
        def _():
            dS[...] = jnp.zeros_like(dS)
            dgn_ref[...] = jnp.zeros_like(dgn_ref)

        s_in = st_ref[...]
        v = v_ref[...]
        cos, sin, mask, xi_v, ze = cos_ref[...], sin_ref[...], m_ref[...], xi_ref[...], ze_ref[...]
        qr, kr, a, on, rstd = _ret_fwd_vals(q_ref[...], k_ref[...], v, cos, sin, mask, xi_v, s_in)
        g = g_ref[...]
        sg = _sigmoid(g)
        silu = g * sg
        gnw = gn_ref[...]
        dy = dy_ref[...].astype(F32)
        dg = (dy * on * gnw * (sg * (1.0 + g * (1.0 - sg)))).astype(BF16)
        t = dy * silu
        dgn_ref[...] += _colsum(t * on)
        don = t * gnw
        do = rstd * (don - jnp.mean(don, axis=1, keepdims=True) - on * jnp.mean(don * on, axis=1, keepdims=True))
        dox = do * xi_v
        ds_out = dS[...]
        da = _dot(do, v, NT) * mask
        kz = kr * ze
        dv = (_dot(a, do, TN) + _dot(kz, ds_out)).astype(BF16)
        dqr = _dot(da, kr) + _dot(dox, s_in, NT)
        dkr = _dot(da, qr, TN) + _dot(v, ds_out, NT) * ze
        dS[...] = ds_out * xi_v[C - 1:C, :] + _dot(qr, dox, TN)
        dq = _unrot(dqr, cos, sin).astype(BF16)
        dk = _unrot(dkr * scale, cos, sin).astype(BF16)
        h, i = pl.program_id(0), pl.program_id(1)
        pieces = [(0, RET_DK, h * RET_DK), (RET_DK, RET_DK, wq + h * RET_DK),
                  (2 * RET_DK, RET_DV, 2 * wq + h * RET_DV), (2 * RET_DK + RET_DV, RET_DV, 2 * wq + wv + h * RET_DV)]
        _stage_out(dproj_ref, stage, sems, h * NS + i, H * NS, (NS - 1 - i) * C, pieces, [dq, dk, dv, dg])

    return pl.pallas_call(
        body, name=name, grid=(H, NS),
        in_specs=[q_spec, k_spec, v_spec, g_spec, cs_spec, cs_spec, m_spec, vec_spec, vec_spec, gn_spec, st_spec, dy_spec],
        out_specs=[_ANY, gn_spec],
        out_shape=[jax.ShapeDtypeStruct((T, 2 * wq + 2 * wv), BF16), jax.ShapeDtypeStruct((1, H * RET_DV), F32)],
        scratch_shapes=[pltpu.VMEM((RET_DK, RET_DV), F32), pltpu.VMEM((2, C, 2 * RET_DK + 2 * RET_DV), BF16),
                        pltpu.SemaphoreType.DMA((2, 4))],
        compiler_params=_cparams("arbitrary", "arbitrary"),
    )(proj, proj, proj, proj, cos, sin, mask, xi, zeta, gn_w, states, dy)


def _shift_down(x, prev8, k):
    if k == 0:
        return x
    y = pltpu.roll(x, k, 0)
    row = lax.broadcasted_iota(jnp.int32, prev8.shape, 0)
    top = jnp.where(row < k, pltpu.roll(prev8, k, 0), y[:8])
    return jnp.concatenate([top, y[8:]], axis=0)


def _shift_up(x, next8, k):
    if k == 0:
        return x
    n = x.shape[0]
    y = pltpu.roll(x, n - k, 0)
    row = lax.broadcasted_iota(jnp.int32, next8.shape, 0)
    bot = jnp.where(row >= 8 - k, pltpu.roll(next8, 8 - k, 0), y[n - 8:])
    return jnp.concatenate([y[:n - 8], bot], axis=0)


def _conv_silu(raw, halo, w, b):
    cv = b
    for tap in range(SSD_CONV_W):
        cv = cv + _shift_down(raw, halo, SSD_CONV_W - 1 - tap) * w[tap:tap + 1, :]
    sg = _sigmoid(cv)
    return cv * sg, cv, sg


def _conv_silu_bwd(d_post, cv, sg, raw, halo, w, carry8):
    dcv = d_post * (sg * (1.0 + cv * (1.0 - sg)))
    d_raw = jnp.zeros_like(raw)
    dws = []
    for tap in range(SSD_CONV_W):
        k = SSD_CONV_W - 1 - tap
        d_raw = d_raw + _shift_up(dcv, carry8, k) * w[tap:tap + 1, :]
        dws.append(_colsum(dcv * _shift_down(raw, halo, k)))
    return d_raw, jnp.concatenate(dws, axis=0), _colsum(dcv), dcv[:8]


def _softplus(x):
    return jnp.maximum(x, 0.0) + jnp.log1p(jnp.exp(-jnp.abs(x)))


def _ssd_common(C, R, dt, dtT, bias, biasT, alog, alogT, E):
    p = dt + bias
    dtv = _softplus(p)
    a = -jnp.exp(alog)
    da = dtv * a
    daT = _softplus(dtT + biasT) * (-jnp.exp(alogT))
    row = lax.broadcasted_iota(jnp.int32, (C, C), 0)
    col = lax.broadcasted_iota(jnp.int32, (C, C), 1)
    tril = row >= col
    trilf = jnp.where(tril, 1.0, 0.0).astype(F32)
    triuf = jnp.where(col >= row, 1.0, 0.0).astype(F32)
    acum = _dot_sel(trilf, da, split="b")
    acumT = _dot_sel(daT, trilf, NT, split="a")
    al = acum[C - 1:C, :]
    ea = jnp.exp(acum)
    dte = jnp.exp(al - acum)
    eal = jnp.exp(al)
    return dict(p=p, dtv=dtv, a=a, da=da, tril=tril, triuf=triuf, acum=acum, acumT=acumT, al=al, ea=ea, dte=dte, eal=eal,
                dtv_e=_dot_sel(dtv, E, split="a", terms=2), ea_e=_dot_sel(ea, E, split="a", terms=2),
                dte_e=_dot_sel(dte, E, split="a", terms=2), eal_e=_dot_sel(eal, E, split="a"))


def _head_decay(q, r, C, R):
    seg = jnp.broadcast_to(q["acum"][:, r:r + 1], (C, C)) - q["acumT"][r:r + 1, :]
    return jnp.exp(jnp.where(q["tril"], seg, -1e30))


def _ssd_group_specs(C, R, NS, rev):
    RP = R * SSD_P
    G = SSD_G
    hb = C // 8

    def ci(i):
        return NS - 1 - i if rev else i

    def halo_row(i):
        return jnp.maximum(ci(i) * hb - 1, 0)

    off_b = G * RP // SSD_N
    z_spec = pl.BlockSpec((C, RP), lambda g, i: (ci(i), g))
    x_spec = pl.BlockSpec((C, RP), lambda g, i: (ci(i), G + g))
    b_spec = pl.BlockSpec((C, SSD_N), lambda g, i: (ci(i), 2 * off_b + g))
    c_spec = pl.BlockSpec((C, SSD_N), lambda g, i: (ci(i), 2 * off_b + G + g))
    xh_spec = pl.BlockSpec((8, RP), lambda g, i: (halo_row(i), G + g))
    bh_spec = pl.BlockSpec((8, SSD_N), lambda g, i: (halo_row(i), 2 * off_b + g))
    ch_spec = pl.BlockSpec((8, SSD_N), lambda g, i: (halo_row(i), 2 * off_b + G + g))
    dt_spec = pl.BlockSpec((None, C, R), lambda g, i: (g, ci(i), 0))
    dtT_spec = pl.BlockSpec((None, R, C), lambda g, i: (g, 0, ci(i)))
    pr_spec = pl.BlockSpec((None, 1, R), lambda g, i: (g, 0, 0))
    prT_spec = pl.BlockSpec((None, R, 1), lambda g, i: (g, 0, 0))
    cwx_spec = pl.BlockSpec((SSD_CONV_W, RP), lambda g, i: (0, g))
    cwn_spec = pl.BlockSpec((SSD_CONV_W, SSD_N), lambda g, i: (0, g))
    cbx_spec = pl.BlockSpec((1, RP), lambda g, i: (0, g))
    cbn_spec = pl.BlockSpec((1, SSD_N), lambda g, i: (0, g))
    e_spec = pl.BlockSpec((R, RP), lambda g, i: (0, 0))
    st_spec = pl.BlockSpec((None, None, SSD_N, RP), lambda g, i: (g, ci(i), 0, 0))
    return dict(z=z_spec, x=x_spec, b=b_spec, c=c_spec, xh=xh_spec, bh=bh_spec, ch=ch_spec, dt=dt_spec, dtT=dtT_spec,
                pr=pr_spec, prT=prT_spec, cwx=cwx_spec, cwn=cwn_spec, cbx=cbx_spec, cbn=cbn_spec, e=e_spec, st=st_spec)


def _ssd_forward_vals(C, R, refs, first, s_in):
    E = refs["E"]
    halo_on = jnp.where(first, 0.0, 1.0)
    xh, bh, ch = refs["xh"] * halo_on, refs["bh"] * halo_on, refs["ch"] * halo_on
    xs, cvx, sgx = _conv_silu(refs["x"], xh, refs["cwx"], refs["cbx"])
    bm, cvb, sgb = _conv_silu(refs["b"], bh, refs["cwb"], refs["cbb"])
    cm, cvc, sgc = _conv_silu(refs["c"], ch, refs["cwc"], refs["cbc"])
    q = _ssd_common(C, R, refs["dt"], refs["dtT"], refs["bias"], refs["biasT"], refs["alog"], refs["alogT"], E)
    xdt = xs * q["dtv_e"]
    cb = _dot(cm, bm, NT)
    yoff_raw = _dot(cm, s_in)
    xdt_b = xdt.astype(BF16)
    low = lax.broadcasted_iota(jnp.int32, (1, 2 * SSD_P), 1) < SSD_P
    pairs = []
    for j in range(R // 2):
        xp = xdt_b[:, 2 * SSD_P * j:2 * SSD_P * (j + 1)]
        y0 = _dot(cb * _head_decay(q, 2 * j, C, R), xp)
        y1 = _dot(cb * _head_decay(q, 2 * j + 1, C, R), xp)
        pairs.append(jnp.where(low, y0, y1))
    ydiag = jnp.concatenate(pairs, axis=1)
    d_e =_dot_sel(refs["dskip"], E, split="a")
    y = ydiag + yoff_raw * q["ea_e"] + d_e * xs
    xd = xdt * q["dte_e"]
    s_out = s_in * q["eal_e"] + _dot(bm, xd, TN)
    z = refs["z"]
    sgz = _sigmoid(z)
    yz = y * (z * sgz)
    rn = lax.rsqrt(jnp.mean(yz * yz, axis=1, keepdims=True) + RMS_EPS)
    return dict(q=q, xh=xh, bh=bh, ch=ch, xs=xs, cvx=cvx, sgx=sgx, bm=bm, cvb=cvb, sgb=sgb, cm=cm, cvc=cvc, sgc=sgc,
                xdt=xdt, cb=cb, yoff_raw=yoff_raw, d_e=d_e, y=y, xd=xd, s_out=s_out, z=z, sgz=sgz, yz=yz, rn=rn)


_SSD_IN_NAMES = ("z", "x", "b", "c", "xh", "bh", "ch", "dt", "dtT", "bias", "biasT", "alog", "alogT", "dskip",
                 "cwx", "cwb", "cwc", "cbx", "cbb", "cbc", "nw", "E")


def _ssd_inputs(pm, dt_g, dtT_g, prm, sp):
    bias, biasT, alog, alogT, dskip, cwx, cwb, cwc, cbx, cbb, cbc, nw, E = prm
    args = [pm, pm, pm, pm, pm, pm, pm, dt_g, dtT_g, bias, biasT, alog, alogT, dskip, cwx, cwb, cwc, cbx, cbb, cbc, nw, E]
    specs = [sp["z"], sp["x"], sp["b"], sp["c"], sp["xh"], sp["bh"], sp["ch"], sp["dt"], sp["dtT"], sp["pr"], sp["prT"],
             sp["pr"], sp["prT"], sp["pr"], sp["cwx"], sp["cwn"], sp["cwn"], sp["cbx"], sp["cbn"], sp["cbn"], sp["cbx"], sp["e"]]
    return args, specs


def _ssd_fwd(pm, dt_g, dtT_g, prm, *, C, R, name):
    T = pm.shape[0]
    NS = T // C
    RP = R * SSD_P
    G = SSD_G
    sp = _ssd_group_specs(C, R, NS, False)
    args, specs = _ssd_inputs(pm, dt_g, dtT_g, prm, sp)
    nin = len(args)

    def body(*refs):
        ins = {n: r[...] for n, r in zip(_SSD_IN_NAMES, refs[:nin])}
        y_ref, st_ref, S = refs[nin:]
        first = pl.program_id(1) == 0

        @pl.when(first)
        def _():
            S[...] = jnp.zeros_like(S)

        s_in = S[...]
        st_ref[...] = s_in
        f = _ssd_forward_vals(C, R, ins, first, s_in)
        y_ref[...] = (f["yz"] * f["rn"] * ins["nw"]).astype(BF16)
        S[...] = f["s_out"]

    return pl.pallas_call(
        body, name=name, grid=(G, NS), in_specs=specs,
        out_specs=[pl.BlockSpec((C, RP), lambda g, i: (i, g)), sp["st"]],
        out_shape=[jax.ShapeDtypeStruct((T, G * RP), BF16), jax.ShapeDtypeStruct((G, NS, SSD_N, RP), F32)],
        scratch_shapes=[pltpu.VMEM((SSD_N, RP), F32)],
        compiler_params=_cparams("parallel", "arbitrary"),
    )(*args)


def _ssd_bwd(pm, dt_g, dtT_g, prm, states, dout, *, C, R, name):
    T = pm.shape[0]
    NS = T // C
    RP = R * SSD_P
    G = SSD_G
    sp = _ssd_group_specs(C, R, NS, True)
    args, specs = _ssd_inputs(pm, dt_g, dtT_g, prm, sp)
    nin = len(args)
    rows_spec = pl.BlockSpec((C, RP), lambda g, i: (NS - 1 - i, g))
    args = args + [states, dout]
    specs = specs + [sp["st"], rows_spec]

    def body(*refs):
        ins = {n: r[...] for n, r in zip(_SSD_IN_NAMES, refs[:nin])}
        st_ref, dout_ref = refs[nin], refs[nin + 1]
        (dpm_ref, ddt_ref, dbias_ref, dalog_ref, dd_ref, dcwx_ref, dcwb_ref, dcwc_ref,
         dcbx_ref, dcbb_ref, dcbc_ref, dnw_ref) = refs[nin + 2:nin + 14]
        dS, cx8, cb8, cc8, stage, sems = refs[nin + 14:]
        acc_refs = (dbias_ref, dalog_ref, dd_ref, dcwx_ref, dcwb_ref, dcwc_ref, dcbx_ref, dcbb_ref, dcbc_ref, dnw_ref)
        step = pl.program_id(1)

        @pl.when(step == 0)
        def _():
            for r_ in acc_refs + (dS, cx8, cb8, cc8):
                r_[...] = jnp.zeros_like(r_)

        first = step == NS - 1
        E = ins["E"]
        s_in = st_ref[...]
        f = _ssd_forward_vals(C, R, ins, first, s_in)
        q = f["q"]
        xs, bm, cm, xdt, cb, y, z, sgz, yz, rn = (f[n] for n in ("xs", "bm", "cm", "xdt", "cb", "y", "z", "sgz", "yz", "rn"))
        nw = ins["nw"]
        dout = dout_ref[...].astype(F32)
        yh = yz * rn
        dnw_ref[...] += _colsum(dout * yh)
        g1 = dout * nw
        dyz = rn * (g1 - yh * jnp.mean(g1 * yh, axis=1, keepdims=True))
        dz = (dyz * y * (sgz * (1.0 + z * (1.0 - sgz)))).astype(BF16)
        dy = dyz * (z * sgz)
        dd_ref[...] += _dot_sel(_colsum(dy * xs), E, NT, split="a")
        dxs = dy * f["d_e"]
        dyo = dy * q["ea_e"]
        dcm = _dot(dyo, s_in, NT)
        ds_acc = _dot(cm, dyo, TN)
        dacum = _dot_sel(dy * f["yoff_raw"], E, NT, split="a", terms=1) * q["ea"]
        dacumT = jnp.zeros((R, C), F32)
        dcb = jnp.zeros((C, C), F32)
        rowR = lax.broadcasted_iota(jnp.int32, (1, R), 1)
        rowRT = lax.broadcasted_iota(jnp.int32, (R, 1), 0)
        dy_b, xdt_b = dy.astype(BF16), xdt.astype(BF16)
        low = lax.broadcasted_iota(jnp.int32, (1, 2 * SSD_P), 1) < SSD_P
        dxdt_pairs = []
        for j in range(R // 2):
            lanes = slice(2 * SSD_P * j, 2 * SSD_P * (j + 1))
            dyp, xp = dy_b[:, lanes], xdt_b[:, lanes]
            halves = []
            for r, mine in ((2 * j, low), (2 * j + 1, jnp.logical_not(low))):
                lr = _head_decay(q, r, C, R)
                w_r = cb * lr
                dw = _dot(jnp.where(mine, dyp, jnp.zeros_like(dyp)), xp, NT)
                halves.append(_dot(w_r, dyp, TN))
                dcb = dcb + dw * lr
                dseg = dw * w_r
                dacum = dacum + jnp.sum(dseg, axis=1, keepdims=True) * jnp.where(rowR == r, 1.0, 0.0)
                dacumT = dacumT - _colsum(dseg) * jnp.where(rowRT == r, 1.0, 0.0)
            dxdt_pairs.append(jnp.where(low, halves[0], halves[1]))
        dxdt = jnp.concatenate(dxdt_pairs, axis=1)
        dsn = dS[...]
        ds_acc = ds_acc + dsn * q["eal_e"]
        d_eal = _dot_sel(_colsum(dsn * s_in), E, NT, split="a")
        dbm = _dot(f["xd"], dsn, NT)
        dxd = _dot(bm, dsn)
        dxdt = dxdt + dxd * q["dte_e"]
        d_dte = _dot_sel(dxd * xdt, E, NT, split="a", terms=1) * q["dte"]
        d_al = _colsum(d_dte) + d_eal * q["eal"]
        dacum = dacum - d_dte
        rowC = lax.broadcasted_iota(jnp.int32, (C, 1), 0)
        dacum = dacum + jnp.where(rowC == C - 1, 1.0, 0.0) * d_al
        dS[...] = ds_acc
        dcm = dcm + _dot(dcb, bm)
        dbm = dbm + _dot(dcb, cm, TN)
        eye = jnp.where(lax.broadcasted_iota(jnp.int32, (C, C), 0) == lax.broadcasted_iota(jnp.int32, (C, C), 1), 1.0, 0.0)
        dacum = dacum + _dot_sel(eye, dacumT, NT, split="b")
        dda = _dot_sel(q["triuf"], dacum, split="b")
        ddtv = dda * q["a"] + _dot_sel(dxdt * xs, E, NT, split="a", terms=1)
        dalog_ref[...] += _colsum(dda * q["dtv"]) * q["a"]
        dxs = dxs + dxdt * q["dtv_e"]
        dp = ddtv * _sigmoid(q["p"])
        ddt_ref[...] = dp
        dbias_ref[...] += _colsum(dp)
        d_raw, d_w, d_b, c8 = _conv_silu_bwd(dxs, f["cvx"], f["sgx"], ins["x"], f["xh"], ins["cwx"], cx8[...])
        dx = d_raw.astype(BF16)
        dcwx_ref[...] += d_w
        dcbx_ref[...] += d_b
        cx8[...] = c8
        d_raw, d_w, d_b, c8 = _conv_silu_bwd(dbm, f["cvb"], f["sgb"], ins["b"], f["bh"], ins["cwb"], cb8[...])
        db = d_raw.astype(BF16)
        dcwb_ref[...] += d_w
        dcbb_ref[...] += d_b
        cb8[...] = c8
        d_raw, d_w, d_b, c8 = _conv_silu_bwd(dcm, f["cvc"], f["sgc"], ins["c"], f["ch"], ins["cwc"], cc8[...])
        dc = d_raw.astype(BF16)
        dcwc_ref[...] += d_w
        dcbc_ref[...] += d_b
        cc8[...] = c8
        g_ = pl.program_id(0)
        pieces = [(0, RP, g_ * RP), (RP, RP, G * RP + g_ * RP), (2 * RP, SSD_N, 2 * G * RP + g_ * SSD_N),
                  (2 * RP + SSD_N, SSD_N, 2 * G * RP + G * SSD_N + g_ * SSD_N)]
        _stage_out(dpm_ref, stage, sems, g_ * NS + step, G * NS, (NS - 1 - step) * C, pieces, [dz, dx, db, dc])

    out_specs = [_ANY, pl.BlockSpec((None, C, R), lambda g, i: (g, NS - 1 - i, 0)),
                 sp["pr"], sp["pr"], sp["pr"], sp["cwx"], sp["cwn"], sp["cwn"], sp["cbx"], sp["cbn"], sp["cbn"], sp["cbx"]]
    out_shape = [jax.ShapeDtypeStruct((T, 2 * G * RP + 2 * G * SSD_N), BF16),
                 jax.ShapeDtypeStruct((G, T, R), F32),
                 jax.ShapeDtypeStruct((G, 1, R), F32), jax.ShapeDtypeStruct((G, 1, R), F32), jax.ShapeDtypeStruct((G, 1, R), F32),
                 jax.ShapeDtypeStruct((SSD_CONV_W, G * RP), F32), jax.ShapeDtypeStruct((SSD_CONV_W, G * SSD_N), F32),
                 jax.ShapeDtypeStruct((SSD_CONV_W, G * SSD_N), F32),
                 jax.ShapeDtypeStruct((1, G * RP), F32), jax.ShapeDtypeStruct((1, G * SSD_N), F32),
                 jax.ShapeDtypeStruct((1, G * SSD_N), F32), jax.ShapeDtypeStruct((1, G * RP), F32)]
    return pl.pallas_call(
        body, name=name, grid=(G, NS), in_specs=specs, out_specs=out_specs, out_shape=out_shape,
        scratch_shapes=[pltpu.VMEM((SSD_N, RP), F32), pltpu.VMEM((8, RP), F32), pltpu.VMEM((8, SSD_N), F32),
                        pltpu.VMEM((8, SSD_N), F32), pltpu.VMEM((2, C, 2 * RP + 2 * SSD_N), BF16),
                        pltpu.SemaphoreType.DMA((2, 4))],
        compiler_params=_cparams("arbitrary", "arbitrary"),
    )(*args)


_ANY = pl.BlockSpec(memory_space=pl.ANY)


def _chip_peer(k):
    x, y, c = lax.axis_index("x"), lax.axis_index("y"), lax.axis_index("c")
    return (x ^ (k >> 1), y ^ (k & 1), c)


def _my_chip():
    return 2 * lax.axis_index("x") + lax.axis_index("y")


def _all_gather_chips(shards, halved, *, name):
    n = len(shards)

    def body(*refs):
        ins, outs = refs[:n], refs[n:2 * n]
        send, recv, fsend, frecv, loc = refs[2 * n:]
        s = _my_chip()
        c = lax.axis_index("c")
        sibling = (lax.axis_index("x"), lax.axis_index("y"), 1 - c)
        copies = []
        for a in range(n):
            cp = pltpu.make_async_copy(ins[a], outs[a].at[s], loc.at[a])
            cp.start()
            copies.append(cp)

        def rows(a, core):
            if not halved[a]:
                return slice(None)
            half = shards[a].shape[0] // 2
            return pl.ds(pl.multiple_of(core * half, 16), half)

        def over_ici(a, k, slot, core):
            return pltpu.make_async_remote_copy(
                src_ref=ins[a].at[rows(a, core)], dst_ref=outs[a].at[slot, rows(a, core)],
                send_sem=send.at[3 * a + k - 1], recv_sem=recv.at[3 * a + k - 1],
                device_id=_chip_peer(k), device_id_type=MESH_ID)

        def over_d2d(a, k, core):
            z = outs[a].at[s ^ k, rows(a, core)]
            return pltpu.make_async_remote_copy(
                src_ref=z, dst_ref=z, send_sem=fsend.at[3 * a + k - 1], recv_sem=frecv.at[3 * a + k - 1],
                device_id=sibling, device_id_type=MESH_ID)

        sent = []
        for a in range(n):
            for k in (1, 2, 3):
                cp = over_ici(a, k, s, c)
                cp.start()
                sent.append(cp)
        passed = []
        for a in range(n):
            for k in (1, 2, 3):
                over_ici(a, k, s ^ k, c).wait_recv()
                if halved[a]:
                    cp = over_d2d(a, k, c)
                    cp.start()
                    passed.append(cp)
        for a in range(n):
            if halved[a]:
                for k in (1, 2, 3):
                    over_d2d(a, k, 1 - c).wait_recv()
        for cp in sent + passed:
            cp.wait_send()
        for cp in copies:
            cp.wait()

    for a, h in zip(shards, halved):
        assert not h or a.shape[0] % 32 == 0, a.shape
    return pl.pallas_call(
        body, name=name, in_specs=[_ANY] * n, out_specs=[_ANY] * n,
        out_shape=[jax.ShapeDtypeStruct((4,) + a.shape, a.dtype) for a in shards],
        scratch_shapes=[pltpu.SemaphoreType.DMA((3 * n,))] * 4 + [pltpu.SemaphoreType.DMA((n,))],
        compiler_params=pltpu.CompilerParams(has_side_effects=True),
    )(*shards)


_HBM = pl.BlockSpec(memory_space=pltpu.HBM)
_SEM = pl.BlockSpec(memory_space=pltpu.SEMAPHORE)
_EFFECT = pltpu.SideEffectType.DATAFLOW_SIDE_EFFECTING


def _split_copies(src, land, send, recv, loc, a, scatter):
    s = _my_chip()
    mine = pltpu.make_async_copy(src.at[s] if scatter else src, land.at[s], loc.at[a])
    pairs = []
    for k in (1, 2, 3):
        sems = dict(send_sem=send.at[3 * a + k - 1], recv_sem=recv.at[3 * a + k - 1],
                    device_id=_chip_peer(k), device_id_type=MESH_ID)
        out = pltpu.make_async_remote_copy(src_ref=src.at[s ^ k] if scatter else src, dst_ref=land.at[s], **sems)
        arriving = pltpu.make_async_remote_copy(src_ref=src.at[s ^ k] if scatter else src, dst_ref=land.at[s ^ k], **sems)
        pairs.append((out, arriving))
    return mine, pairs


def _split_start(arrs, *, scatter, after, name):
    n = len(arrs)
    zones = [lax.empty(a.shape if scatter else (4,) + a.shape, a.dtype) for a in arrs]

    def body(*refs):
        srcs, lands = refs[:n], refs[n:2 * n]
        send, recv, loc = refs[2 * n + 1:2 * n + 4]
        token = refs[-1]
        for a in range(n):
            mine, pairs = _split_copies(srcs[a], lands[a], send, recv, loc, a, scatter)
            mine.start()
            for out, _ in pairs:
                out.start()
        token[...] = jnp.zeros_like(token)

    res = pl.pallas_call(
        body, name=name,
        out_shape=(pltpu.SemaphoreType.DMA((3 * n,)), pltpu.SemaphoreType.DMA((3 * n,)), pltpu.SemaphoreType.DMA((n,)),
                   *[pltpu.HBM(z.shape, z.dtype) for z in zones], jax.ShapeDtypeStruct((8, 128), F32)),
        in_specs=[_ANY] * n + [_HBM] * n + [_ANY],
        out_specs=(_SEM, _SEM, _SEM, *([_HBM] * n), pl.BlockSpec(memory_space=pltpu.VMEM)),
        input_output_aliases={n + i: 3 + i for i in range(n)},
        compiler_params=pltpu.CompilerParams(has_side_effects=_EFFECT),
    )(*arrs, *[pltpu.with_memory_space_constraint(z, pltpu.HBM) for z in zones], after)
    return res[:3], list(res[3:3 + n]), res[-1]


def _split_wait(sems, src, land, a, *, scatter, after, name):
    def body(src_ref, land_ref, send, recv, loc, after_ref, land_out):
        mine, pairs = _split_copies(src_ref, land_ref, send, recv, loc, a, scatter)
        mine.wait()
        for out, arriving in pairs:
            out.wait_send()
            arriving.wait_recv()

    return pl.pallas_call(
        body, name=name, out_shape=pltpu.HBM(land.shape, land.dtype),
        in_specs=[_ANY, _HBM, _SEM, _SEM, _SEM, _ANY], out_specs=_HBM, input_output_aliases={1: 0},
        compiler_params=pltpu.CompilerParams(has_side_effects=_EFFECT),
    )(src, land, *sems, after)


def _sibling_copies(srcs, lands, send, recv):
    sib = (lax.axis_index("x"), lax.axis_index("y"), 1 - lax.axis_index("c"))
    return [pltpu.make_async_remote_copy(src_ref=srcs[a], dst_ref=lands[a], send_sem=send.at[a], recv_sem=recv.at[a],
                                         device_id=sib, device_id_type=MESH_ID) for a in range(len(srcs))]


def _swap_start(arrs, *, after, name):
    n = len(arrs)
    zones = [lax.empty(a.shape, a.dtype) for a in arrs]

    def body(*refs):
        for cp in _sibling_copies(refs[:n], refs[n:2 * n], refs[2 * n + 1], refs[2 * n + 2]):
            cp.start()

    res = pl.pallas_call(
        body, name=name,
        out_shape=(pltpu.SemaphoreType.DMA((n,)), pltpu.SemaphoreType.DMA((n,)), *[pltpu.HBM(z.shape, z.dtype) for z in zones]),
        in_specs=[_ANY] * n + [_HBM] * n + [_ANY], out_specs=(_SEM, _SEM, *([_HBM] * n)),
        input_output_aliases={n + i: 2 + i for i in range(n)},
        compiler_params=pltpu.CompilerParams(has_side_effects=_EFFECT),
    )(*arrs, *[pltpu.with_memory_space_constraint(z, pltpu.HBM) for z in zones], after)
    return res[:2], list(res[2:])


def _swap_wait(sems, srcs, lands, *, after, name):
    n = len(srcs)

    def body(*refs):
        for cp in _sibling_copies(refs[:n], refs[n:2 * n], refs[2 * n], refs[2 * n + 1]):
            cp.wait_send()
            cp.wait_recv()

    res = pl.pallas_call(
        body, name=name, out_shape=tuple(pltpu.HBM(a.shape, a.dtype) for a in lands),
        in_specs=[_ANY] * n + [_HBM] * n + [_SEM, _SEM, _ANY], out_specs=tuple([_HBM] * n),
        input_output_aliases={n + i: i for i in range(n)},
        compiler_params=pltpu.CompilerParams(has_side_effects=_EFFECT),
    )(*srcs, *lands, *sems, after)
    return list(res)


def _all_gather_devices(v, *, name):
    r = v.shape[0]

    def body(v_ref, out_ref, send, recv):
        x, y, c = lax.axis_index("x"), lax.axis_index("y"), lax.axis_index("c")
        me = 4 * x + 2 * y + c
        out_ref[me] = v_ref[...]
        cps = []
        for k in range(1, 8):
            peer = (x ^ (k >> 2), y ^ ((k >> 1) & 1), c ^ (k & 1))
            cp = pltpu.make_async_remote_copy(src_ref=v_ref, dst_ref=out_ref.at[me], send_sem=send.at[k - 1],
                                              recv_sem=recv.at[k - 1], device_id=peer, device_id_type=MESH_ID)
            cp.start()
            cps.append(cp)
        for k, cp in enumerate(cps, start=1):
            cp.wait_send()
            pltpu.make_async_remote_copy(src_ref=v_ref, dst_ref=out_ref.at[me ^ k], send_sem=send.at[k - 1],
                                         recv_sem=recv.at[k - 1], device_id=(x, y, c), device_id_type=MESH_ID).wait_recv()

    vm = pl.BlockSpec(memory_space=pltpu.VMEM)
    return pl.pallas_call(
        body, name=name, in_specs=[vm], out_specs=vm, out_shape=jax.ShapeDtypeStruct((8, r, 128), F32),
        scratch_shapes=[pltpu.SemaphoreType.DMA((7,)), pltpu.SemaphoreType.DMA((7,))],
        compiler_params=pltpu.CompilerParams(has_side_effects=True),
    )(v)


def _row_tile(r, target):
    best = None
    for t in range(16, min(target, r) + 1, 16):
        if r % t == 0:
            best = t
    return best or r


def _sum_slots(buf, *, name, tr=384, layer=None, stack=None):
    S, r, c = buf.shape
    tr = _row_tile(r, tr)

    def body(*refs):
        b_ref, o_ref = refs[0], refs[-1]
        acc = b_ref[0].astype(F32)
        for j in range(1, S):
            acc = acc + b_ref[j].astype(F32)
        o_ref[...] = acc

    in_specs = [pl.BlockSpec((S, tr, c), lambda i: (0, i, 0))]
    args, alias = [buf], {}
    if layer is None:
        out_spec, out_shape = pl.BlockSpec((tr, c), lambda i: (i, 0)), jax.ShapeDtypeStruct((r, c), F32)
    else:
        out_spec = pl.BlockSpec((None, tr, c), lambda i: (layer, i, 0))
        out_shape = jax.ShapeDtypeStruct((2, r, c), F32)
        if stack is not None:
            in_specs.append(_ANY)
            args.append(stack)
            alias = {1: 0}
    return pl.pallas_call(
        body, name=name, grid=(r // tr,), in_specs=in_specs, out_specs=out_spec, out_shape=out_shape,
        input_output_aliases=alias, compiler_params=_cparams("parallel"),
    )(*args)


ADAMW_BLOCK_ELEMS = 1 << 18


def _adamw(w, gs, m, v, *, name, tr=256):
    r, c = w.shape
    tr = _row_tile(r, min(tr, max(16, ADAMW_BLOCK_ELEMS // c)))
    bc1 = 1.0 - ADAM_B1 ** ADAM_STEP
    bc2 = 1.0 - ADAM_B2 ** ADAM_STEP
    ng = len(gs)

    def body(*refs):
        w_ref, m_ref, v_ref = refs[0], refs[1 + ng], refs[2 + ng]
        g_ref, d_ref, mo_ref, vo_ref = refs[3 + ng:]
        gg = refs[1][...] if ng == 1 else refs[1][...] + refs[2][...]
        mn = ADAM_B1 * m_ref[...] + (1.0 - ADAM_B1) * gg
        vn = ADAM_B2 * v_ref[...] + (1.0 - ADAM_B2) * (gg * gg)
        g_ref[...] = gg
        mo_ref[...] = mn
        vo_ref[...] = vn
        d_ref[...] = -ADAM_LR * ((mn / bc1) / (jnp.sqrt(vn / bc2) + ADAM_EPS) + ADAM_WD * w_ref[...])

    spec = pl.BlockSpec((tr, c), lambda i: (i, 0))
    return pl.pallas_call(body, name=name, grid=(r // tr,), in_specs=[spec] * (3 + ng), out_specs=[spec] * 4,
                          out_shape=[jax.ShapeDtypeStruct((r, c), F32)] * 4,
                          compiler_params=_cparams("parallel"))(w, *gs, m, v)


def _pack(vecs, rows):
    flat = jnp.concatenate([v.reshape(-1).astype(F32) for v in vecs])
    return jnp.pad(flat, (0, rows * 128 - flat.shape[0])).reshape(rows, 128)


def _unpack(packed, shapes):
    flat = packed.reshape(-1)
    out, off = [], 0
    for s in shapes:
        n = math.prod(s)
        out.append(flat[off:off + n].reshape(s))
        off += n
    return out


def _pack_rows(shapes):
    n = sum(math.prod(s) for s in shapes)
    return -(-n // 1024) * 8


def kernel(x, norm_mix_pre, norm_mix_post, norm_ffn_pre, norm_ffn_post, ret_w_in, ret_gn_w, ret_w_out, ssd_w_in, ssd_conv_w, ssd_conv_b, ssd_dt_bias, ssd_a_log, ssd_d, ssd_norm_w, ssd_w_out, mlp_w_up, mlp_w_down, loss_target, m_norm_mix_pre, m_norm_mix_post, m_norm_ffn_pre, m_norm_ffn_post, m_ret_w_in, m_ret_gn_w, m_ret_w_out, m_ssd_w_in, m_ssd_conv_w, m_ssd_conv_b, m_ssd_dt_bias, m_ssd_a_log, m_ssd_d, m_ssd_norm_w, m_ssd_w_out, m_mlp_w_up, m_mlp_w_down, v_norm_mix_pre, v_norm_mix_post, v_norm_ffn_pre, v_norm_ffn_post, v_ret_w_in, v_ret_gn_w, v_ret_w_out, v_ssd_w_in, v_ssd_conv_w, v_ssd_conv_b, v_ssd_dt_bias, v_ssd_a_log, v_ssd_d, v_ssd_norm_w, v_ssd_w_out, v_mlp_w_up, v_mlp_w_down):
    T, D = x.shape[1], x.shape[2]
    H = D // RET_DK
    d_inner = 2 * D
    R = d_inner // SSD_P // SSD_G
    RP = R * SSD_P
    n_heads = SSD_G * R
    conv_dim = d_inner + 2 * SSD_G * SSD_N
    n_main = d_inner + conv_dim
    C = min(256, T)
    chip = _my_chip()
    xs, tgt = x[0], loss_target[0]

    conv_sh = ssd_conv_w.shape[2]
    small_shapes = [(SSD_CONV_W, conv_sh), (conv_sh,), (ssd_norm_w.shape[1],)]
    small_rows = _pack_rows(small_shapes)
    shards = [ret_w_in[0].T.astype(BF16), ret_w_out[0].astype(BF16), ssd_w_in[0].T.astype(BF16), ssd_w_out[0].astype(BF16),
              mlp_w_up[0].T.astype(BF16), mlp_w_up[1].T.astype(BF16), mlp_w_down[0].astype(BF16), mlp_w_down[1].astype(BF16)]
    (ret_in_g, small_g) = _all_gather_chips([shards[0], _pack([ssd_conv_w[0], ssd_conv_b[0], ssd_norm_w[0]], small_rows)],
                                            [True, False], name="gather_first")

    def full(g):
        return g.reshape(4 * g.shape[1], g.shape[2])

    def start_gather(idx, after, name):
        sems, zones, tok = _split_start([shards[i] for i in idx], scatter=False, after=after, name=name)
        return {i: (sems, shards[i], zones[n], n) for n, i in enumerate(idx)}, tok

    def arrived(stage, i, after, name):
        sems, src, zone, n = stage[i]
        return full(_split_wait(sems, src, zone, n, scatter=False, after=after, name=name))

    ret_in_t = full(ret_in_g)
    sm = [_unpack(small_g[j], small_shapes) for j in range(4)]
    conv_w = jnp.concatenate([sm[j][0] for j in range(4)], axis=1)
    conv_b = jnp.concatenate([sm[j][1] for j in range(4)])[None, :]
    norm_w = jnp.concatenate([sm[j][2] for j in range(4)])[None, :]

    gb = SSD_G * SSD_N
    ssd_prm = (ssd_dt_bias.reshape(SSD_G, 1, R), ssd_dt_bias.reshape(SSD_G, R, 1),
               ssd_a_log.reshape(SSD_G, 1, R), ssd_a_log.reshape(SSD_G, R, 1), ssd_d.reshape(SSD_G, 1, R),
               conv_w[:, :d_inner], conv_w[:, d_inner:d_inner + gb], conv_w[:, d_inner + gb:],
               conv_b[:, :d_inner], conv_b[:, d_inner:d_inner + gb], conv_b[:, d_inner + gb:],
               norm_w, jnp.asarray(np.kron(np.eye(R), np.ones((1, SSD_P))), F32))
    ret_consts = _ret_consts(T, C, H)

    u0 = _rms_pre(xs, norm_mix_pre[0:1], name="pre0")
    stage1, tok = start_gather((1, 4), ret_in_g, "gather_start1")
    proj = _matmul(u0, ret_in_t, "nt", out_dtype=F32, name="ret_in", after=tok)
    stage2, tok = start_gather((6, 2), proj, "gather_start2")
    y_ret, st_ret = _ret_fwd(proj, ret_gn_w, ret_consts, C=C, name="ret_fwd")
    ret_out = arrived(stage1, 1, y_ret, "gather_wait_ret_out")
    m0 = _matmul(y_ret, ret_out, "nn", out_dtype=BF16, name="ret_out", after=tok)
    h1, u1 = _rms_post_pre(xs, m0, norm_mix_post[0:1], norm_ffn_pre[0:1], name="post_pre1")
    up_t0 = arrived(stage1, 4, u1, "gather_wait_up0")
    a0, hh0 = _matmul(u1, up_t0, "nt", out_dtype=BF16, name="mlp_up0", epi="relu2")
    stage3, tok = start_gather((3, 5, 7), hh0, "gather_start3")
    down0 = arrived(stage2, 6, hh0, "gather_wait_down0")
    f0 = _matmul(hh0, down0, "nn", out_dtype=BF16, name="mlp_down0", after=tok)
    h2, u2 = _rms_post_pre(h1, f0, norm_ffn_post[0:1], norm_mix_pre[1:2], name="post_pre2")
    ssd_in_t = arrived(stage2, 2, u2, "gather_wait_ssd_in")
    pm, pdt = _matmul(u2, ssd_in_t, "nt", out_dtype=F32, name="ssd_in", tail=(n_heads,))
    dt_g = pdt.reshape(T, SSD_G, R).transpose(1, 0, 2)
    dtT_g = pdt.reshape(T, SSD_G, R).transpose(1, 2, 0)
    y_ssd, st_ssd = _ssd_fwd(pm, dt_g, dtT_g, ssd_prm, C=C, R=R, name="ssd_fwd")
    ssd_out = arrived(stage3, 3, y_ssd, "gather_wait_ssd_out")
    m1 = _matmul(y_ssd, ssd_out, "nn", out_dtype=BF16, name="ssd_out")
    h3, u3 = _rms_post_pre(h2, m1, norm_mix_post[1:2], norm_ffn_pre[1:2], name="post_pre3")
    up_t1 = arrived(stage3, 5, u3, "gather_wait_up1")
    a1, hh1 = _matmul(u3, up_t1, "nt", out_dtype=BF16, name="mlp_up1", epi="relu2")
    down1 = arrived(stage3, 7, hh1, "gather_wait_down1")
    f1 = _matmul(hh1, down1, "nn", out_dtype=BF16, name="mlp_down1")
    up_t, down = (up_t0, up_t1), (down0, down1)
    dh4, sq = _rms_post_loss(h3, f1, norm_ffn_post[1:2], tgt, name="post_loss")
    loss = lax.psum(sq[0, 0], MESH_AXES) * (0.5 / D)

    in_flight = []

    def send_grad(g, name):
        part = g if g.ndim == 3 else g.reshape(4, g.shape[0] // 4, g.shape[1])
        sems, zones, tok = _split_start([part], scatter=True, after=part, name=f"scatter_start_{name}")
        in_flight.append((name, sems, part, zones[0]))
        return tok

    def mlp_bwd(i, df, u, a, hh):
        tok = send_grad(_matmul(hh, df, "tn", out_dtype=BF16, name=f"mlp_down_wg{i}"), f"down{i}")
        da = _matmul(df, down[i], "nt", out_dtype=BF16, name=f"mlp_down_dg{i}", epi="drelu2", extra=a, after=tok)
        tok = send_grad(_matmul(u, da, "tn", out_dtype=BF16, name=f"mlp_up_wg{i}", col_parts=4), f"up{i}")
        return _matmul(da, up_t[i], "nn", out_dtype=BF16, name=f"mlp_up_dg{i}", after=tok)

    df1, d_nfpost1 = _rms_post_bwd(f1, norm_ffn_post[1:2], dh4, name="post_bwd_ffn1")
    du3 = mlp_bwd(1, df1, u3, a1, hh1)
    dh3, dm1, d_nfp1, d_nmpost1 = _rms_pre_post_bwd(h3, norm_ffn_pre[1:2], du3, dh4, m1, norm_mix_post[1:2],
                                                    name="pre_bwd_ffn1_post_bwd_mix1")
    tok = send_grad(_matmul(y_ssd, dm1, "tn", out_dtype=BF16, name="ssd_out_wg"), "ssd_out")
    dy_ssd = _matmul(dm1, ssd_out, "nt", out_dtype=F32, name="ssd_out_dg", after=tok)
    (dpm, ddt_g, d_bias, d_alog, d_dskip, dcwx, dcwb, dcwc, dcbx, dcbb, dcbc, d_normw) = _ssd_bwd(
        pm, dt_g, dtT_g, ssd_prm, st_ssd, dy_ssd, C=C, R=R, name="ssd_bwd")
    dpdt = ddt_g.transpose(1, 0, 2).reshape(T, n_heads).astype(BF16)
    tok = send_grad(jnp.concatenate([_matmul(dpm, u2, "tn", out_dtype=BF16, name="ssd_in_wg"),
                                     _matmul(dpdt, u2, "tn", out_dtype=BF16, name="ssd_in_dt_wg")], axis=0), "ssd_in")
    du2 = _matmul(dpm, ssd_in_t, "nn", out_dtype=BF16, name="ssd_in_dg", after=tok, tail=(n_heads, dpdt))
    dh2, df0, d_nmp1, d_nfpost0 = _rms_pre_post_bwd(h2, norm_mix_pre[1:2], du2, dh3, f0, norm_ffn_post[0:1],
                                                    name="pre_bwd_mix1_post_bwd_ffn0")
    du1 = mlp_bwd(0, df0, u1, a0, hh0)
    dh1, dm0, d_nfp0, d_nmpost0 = _rms_pre_post_bwd(h1, norm_ffn_pre[0:1], du1, dh2, m0, norm_mix_post[0:1],
                                                    name="pre_bwd_ffn0_post_bwd_mix0")
    tok = send_grad(_matmul(y_ret, dm0, "tn", out_dtype=BF16, name="ret_out_wg"), "ret_out")
    dy_ret = _matmul(dm0, ret_out, "nt", out_dtype=F32, name="ret_out_dg", after=tok)
    dproj, d_gn = _ret_bwd(proj, ret_gn_w, ret_consts, st_ret, dy_ret, C=C, name="ret_bwd")
    tok = send_grad(_matmul(u0, dproj, "tn", out_dtype=BF16, name="ret_in_wg", col_parts=4), "ret_in")
    du0 = _matmul(dproj, ret_in_t, "nn", out_dtype=BF16, name="ret_in_dg", after=tok)
    grad_x, d_nmp0 = _rms_pre_bwd(xs, norm_mix_pre[0:1], du0, dh1, name="pre_bwd_mix0")

    landed = {nm: _split_wait(sems, src, zone, 0, scatter=True, after=grad_x, name=f"scatter_wait_{nm}")
              for nm, sems, src, zone in in_flight}
    part_sum = {}

    def two_d(t):
        return t.reshape(-1, t.shape[-1])

    def sum_chips(nm):
        if nm[-1] in "01" and nm[:-1] in ("up", "down"):
            fam, layer = nm[:-1], int(nm[-1])
            part_sum[fam] = _sum_slots(landed[nm], name=f"sum_chips_{nm}", layer=layer, stack=part_sum.get(fam))
        else:
            part_sum[nm] = _sum_slots(landed[nm], name=f"sum_chips_{nm}")

    swaps = []
    for names_, fams in ((("down1", "up1", "ssd_out", "ssd_in"), ("ssd_out", "ssd_in")),
                         (("down0", "up0"), ("down", "up")), (("ret_out", "ret_in"), ("ret_out", "ret_in"))):
        for nm in names_:
            sum_chips(nm)
        mine = [two_d(part_sum[f]) for f in fams]
        swaps.append((fams, mine, _swap_start(mine, after=mine[-1], name=f"swap_start{len(swaps)}")))

    def upd(w, gs, m, v, name):
        shp = w.shape
        w2, m2, v2 = (two_d(t) for t in (w, m, v))
        return tuple(t.reshape(shp) for t in _adamw(w2, [g.reshape(w2.shape) for g in gs], m2, v2, name=name))

    def upd_t(w, gs, m, v, name):
        return tuple(t.T[None] for t in _adamw(w[0].T, gs, m[0].T, v[0].T, name=name))

    todo = {"ret_in": (upd, ret_w_in, m_ret_w_in, v_ret_w_in, "ret_w_in"),
            "ret_out": (upd, ret_w_out, m_ret_w_out, v_ret_w_out, "ret_w_out"),
            "ssd_in": (upd_t, ssd_w_in, m_ssd_w_in, v_ssd_w_in, "ssd_w_in"),
            "ssd_out": (upd, ssd_w_out, m_ssd_w_out, v_ssd_w_out, "ssd_w_out"),
            "up": (upd, mlp_w_up, m_mlp_w_up, v_mlp_w_up, "mlp_w_up"),
            "down": (upd, mlp_w_down, m_mlp_w_down, v_mlp_w_down, "mlp_w_down")}
    res = {}
    prev = grad_x
    for n_, (fams, srcs, (sems, zones)) in enumerate(swaps):
        theirs = _swap_wait(sems, srcs, zones, after=prev, name=f"swap_wait{n_}")
        for f, mine, other in zip(fams, srcs, theirs):
            fn, w, m, v, out_name = todo[f]
            res[out_name] = fn(w, [mine, other], m, v, f"adamw_{f}")
            prev = res[out_name][0]

    d_conv_w = jnp.concatenate([dcwx, dcwb, dcwc], axis=1)
    d_conv_b = jnp.concatenate([dcbx, dcbb, dcbc], axis=1)
    small_grads = [jnp.concatenate([d_nmp0, d_nmp1]), jnp.concatenate([d_nmpost0, d_nmpost1]),
                   jnp.concatenate([d_nfp0, d_nfp1]), jnp.concatenate([d_nfpost0, d_nfpost1]),
                   d_gn, d_bias.reshape(1, n_heads), d_alog.reshape(1, n_heads), d_dskip.reshape(1, n_heads),
                   d_conv_w, d_conv_b, d_normw]
    sg_shapes = [g.shape for g in small_grads]
    sg_rows = _pack_rows(sg_shapes)
    everyone = _all_gather_devices(_pack(small_grads, sg_rows), name="gather_small_grads")
    sg = _unpack(_sum_slots(everyone, name="sum_small_grads", tr=sg_rows), sg_shapes)
    (g_nmp, g_nmpost, g_nfp, g_nfpost, g_gn, g_bias, g_alog, g_dskip, g_cw_full, g_cb_full, g_nw_full) = sg
    g_cw = lax.dynamic_slice_in_dim(g_cw_full, chip * conv_sh, conv_sh, axis=1)[None]
    g_cb = lax.dynamic_slice_in_dim(g_cb_full, chip * conv_sh, conv_sh, axis=1)
    nw_sh = ssd_norm_w.shape[1]
    g_nw = lax.dynamic_slice_in_dim(g_nw_full, chip * nw_sh, nw_sh, axis=1)
    small = [("norm_mix_pre", norm_mix_pre, g_nmp, m_norm_mix_pre, v_norm_mix_pre),
             ("norm_mix_post", norm_mix_post, g_nmpost, m_norm_mix_post, v_norm_mix_post),
             ("norm_ffn_pre", norm_ffn_pre, g_nfp, m_norm_ffn_pre, v_norm_ffn_pre),
             ("norm_ffn_post", norm_ffn_post, g_nfpost, m_norm_ffn_post, v_norm_ffn_post),
             ("ret_gn_w", ret_gn_w, g_gn, m_ret_gn_w, v_ret_gn_w),
             ("ssd_conv_w", ssd_conv_w, g_cw, m_ssd_conv_w, v_ssd_conv_w),
             ("ssd_conv_b", ssd_conv_b, g_cb, m_ssd_conv_b, v_ssd_conv_b),
             ("ssd_dt_bias", ssd_dt_bias, g_bias, m_ssd_dt_bias, v_ssd_dt_bias),
             ("ssd_a_log", ssd_a_log, g_alog, m_ssd_a_log, v_ssd_a_log),
             ("ssd_d", ssd_d, g_dskip, m_ssd_d, v_ssd_d),
             ("ssd_norm_w", ssd_norm_w, g_nw, m_ssd_norm_w, v_ssd_norm_w)]
    sw_shapes = [w.shape for _, w, _, _, _ in small]
    sw_rows = _pack_rows(sw_shapes)
    packs = [_pack([t[j] for t in small], sw_rows) for j in (1, 2, 3, 4)]
    _, d_p, m_p, v_p = _adamw(packs[0], [packs[1]], packs[2], packs[3], name="adamw_small", tr=sw_rows)
    d_s, m_s, v_s = _unpack(d_p, sw_shapes), _unpack(m_p, sw_shapes), _unpack(v_p, sw_shapes)
    for j, (nm, w, g, _, _) in enumerate(small):
        res[nm] = (g.reshape(w.shape), d_s[j], m_s[j], v_s[j])

    order = ["norm_mix_pre", "norm_mix_post", "norm_ffn_pre", "norm_ffn_post", "ret_w_in", "ret_gn_w", "ret_w_out",
             "ssd_w_in", "ssd_conv_w", "ssd_conv_b", "ssd_dt_bias", "ssd_a_log", "ssd_d", "ssd_norm_w", "ssd_w_out",
             "mlp_w_up", "mlp_w_down"]
    return (loss, grad_x[None], *[res[n][0] for n in order], *[res[n][1] for n in order],
            *[res[n][2] for n in order], *[res[n][3] for n in order])
```

```python
import math

import numpy as np
import jax
import jax.numpy as jnp
from jax import lax
from jax.experimental import pallas as pl
from jax.experimental.pallas import tpu as pltpu

F32 = jnp.float32
BF16 = jnp.bfloat16
VMEM_LIMIT_BYTES = 56 * 1024 * 1024
MESH_AXES = ("x", "y", "c")
MESH_ID = pl.DeviceIdType.MESH

RMS_EPS = 1e-6
GN_EPS = 1e-5
RET_DK = 256
RET_DV = 512
ROPE_BASE = 10000.0
REF_CHUNK = 64
SSD_P = 64
SSD_N = 128
SSD_G = 8
SSD_CONV_W = 4
ADAM_LR, ADAM_B1, ADAM_B2, ADAM_EPS, ADAM_WD, ADAM_STEP = 0.001, 0.9, 0.999, 1e-08, 0.01, 10

NN = (((1,), (0,)), ((), ()))
NT = (((1,), (1,)), ((), ()))
TN = (((0,), (0,)), ((), ()))


def _cparams(*sem):
    return pltpu.CompilerParams(dimension_semantics=sem, vmem_limit_bytes=VMEM_LIMIT_BYTES)


def _dot(a, b, dims=NN):
    return lax.dot_general(a.astype(BF16), b.astype(BF16), dims, preferred_element_type=F32)


def _split_bf16(x, terms):
    parts, rest = [], x
    for _ in range(terms):
        p = rest.astype(BF16)
        parts.append(p)
        rest = rest - p.astype(F32)
    return parts


def _dot_sel(a, b, dims=NN, *, split, terms=3):
    if split == "a":
        sel = b.astype(BF16)
        return sum(lax.dot_general(p, sel, dims, preferred_element_type=F32) for p in _split_bf16(a, terms))
    sel = a.astype(BF16)
    return sum(lax.dot_general(sel, p, dims, preferred_element_type=F32) for p in _split_bf16(b, terms))


def _sigmoid(x):
    return 1.0 / (1.0 + jnp.exp(-x))


def _colsum(x):
    return jnp.sum(x, axis=0, keepdims=True)


MM_TILE = 1024
MM_FULL_K = 2048


def _mm_tiles(M, N, K):
    if K <= MM_FULL_K:
        return min(M, 2 * MM_TILE), min(N, MM_TILE), K
    return min(M, MM_TILE), min(N, 2 * MM_TILE), MM_TILE


def _matmul(a, b, mode, *, out_dtype, name, epi=None, extra=None, after=None, col_parts=None, tail=None):
    nt_ = tail[0] if tail else 0
    if mode == "nn":
        (M, K), (K2, N) = a.shape, (b.shape[0] - nt_, b.shape[1])
    elif mode == "nt":
        (M, K), (N, K2) = a.shape, (b.shape[0] - nt_, b.shape[1])
    else:
        (K, M), (K2, N) = a.shape, b.shape
    assert K == K2, (a.shape, b.shape, mode)
    tm, tn, tk = _mm_tiles(M, N, K)
    if col_parts:
        tm, tn = min(M, 2 * MM_TILE), min(tn, MM_TILE)
        while (N // col_parts) % tn:
            tn //= 2
    assert M % tm == 0 and N % tn == 0 and K % tk == 0, (M, N, K, tm, tn, tk)
    nk = K // tk
    if mode == "tn":
        a_spec = pl.BlockSpec((tk, tm), lambda i, j, k: (k, i))
    else:
        a_spec = pl.BlockSpec((tm, tk), lambda i, j, k: (i, k))
    if mode == "nt":
        b_spec = pl.BlockSpec((tn, tk), lambda i, j, k: (j, k))
    else:
        b_spec = pl.BlockSpec((tk, tn), lambda i, j, k: (k, j))
    dims = {"nn": NN, "nt": NT, "tn": TN}[mode]
    o_spec = pl.BlockSpec((tm, tn), lambda i, j, k: (i, j))
    out_dims = (M, N)
    if col_parts:
        per = N // col_parts // tn
        o_spec = pl.BlockSpec((None, tm, tn), lambda i, j, k: (j // per, i, j % per))
        out_dims = (col_parts, M, N // col_parts)
    has_extra = epi in ("drelu2", "add")
    n_out = 2 if epi == "relu2" else 1

    in_specs = [a_spec, b_spec] + ([o_spec] if has_extra else [])
    args = [a, b] + ([extra] if has_extra else [])
    if after is not None:
        in_specs.append(pl.BlockSpec(after.shape, lambda i, j, k: (0, 0)))
        args.append(after)
    n_plain = len(args)
    out_specs = [o_spec] * n_out
    out_shape = [jax.ShapeDtypeStruct(out_dims, out_dtype)] * n_out
    if tail and mode == "nt":
        assert nk == 1 and N % nt_ == 0
        in_specs.append(pl.BlockSpec((nt_, tk), lambda i, j, k: (N // nt_, 0)))
        args.append(b)
        out_specs.append(pl.BlockSpec((tm, nt_), lambda i, j, k: (i, 0)))
        out_shape.append(jax.ShapeDtypeStruct((M, nt_), F32))
    elif tail:
        assert mode == "nn" and K % nt_ == 0
        in_specs += [pl.BlockSpec((tm, nt_), lambda i, j, k: (i, 0)), pl.BlockSpec((nt_, tn), lambda i, j, k: (K // nt_, j))]
        args += [tail[1], b]
    n_in = len(args)

    def body(*refs):
        a_ref, b_ref = refs[0], refs[1]
        e_ref = refs[2] if has_extra else None
        outs = refs[n_in:n_in + n_out]

        def finish(r):
            if tail and mode == "nn":
                r = r + _dot(refs[n_plain][...], refs[n_plain + 1][...])
            if epi is None:
                outs[0][...] = r.astype(outs[0].dtype)
            elif epi == "relu2":
                outs[0][...] = r.astype(outs[0].dtype)
                h = jnp.maximum(r, 0.0)
                outs[1][...] = (h * h).astype(outs[1].dtype)
            elif epi == "drelu2":
                av = jnp.maximum(e_ref[...].astype(F32), 0.0)
                outs[0][...] = (r * (2.0 * av)).astype(outs[0].dtype)
            else:
                outs[0][...] = (r + e_ref[...].astype(F32)).astype(outs[0].dtype)

        if tail and mode == "nt":
            @pl.when(pl.program_id(1) == 0)
            def _():
                refs[n_in + n_out][...] = _dot(a_ref[...], refs[n_plain][...], NT)

        if nk == 1:
            finish(_dot(a_ref[...], b_ref[...], dims))
            return
        acc = refs[-1]
        k = pl.program_id(2)

        @pl.when(k == 0)
        def _():
            acc[...] = jnp.zeros_like(acc)

        acc[...] += _dot(a_ref[...], b_ref[...], dims)

        @pl.when(k == nk - 1)
        def _():
            finish(acc[...])

    res = pl.pallas_call(
        body, name=name, grid=(M // tm, N // tn, nk), in_specs=in_specs, out_specs=out_specs,
        out_shape=out_shape, scratch_shapes=[pltpu.VMEM((tm, tn), F32)] if nk > 1 else [],
        compiler_params=_cparams("parallel", "arbitrary" if tail and mode == "nt" else "parallel", "arbitrary"),
    )(*args)
    return res if len(res) > 1 else res[0]


def _rstd(x):
    return lax.rsqrt(jnp.mean(x * x, axis=-1, keepdims=True) + RMS_EPS)


def _row_call(body, ins, outs_shape, *, name, rows, tr, acc_outs=()):
    tr = min(tr, rows)
    assert rows % tr == 0
    in_specs = []
    for arr, blocked in ins:
        if blocked:
            in_specs.append(pl.BlockSpec((tr, arr.shape[1]), lambda i: (i, 0)))
        else:
            in_specs.append(pl.BlockSpec(arr.shape, lambda i: (0, 0)))
    out_specs = []
    for n, s in enumerate(outs_shape):
        if n in acc_outs:
            out_specs.append(pl.BlockSpec(s.shape, lambda i: (0, 0)))
        else:
            out_specs.append(pl.BlockSpec((tr, s.shape[1]), lambda i: (i, 0)))
    return pl.pallas_call(
        body, name=name, grid=(rows // tr,), in_specs=in_specs, out_specs=out_specs, out_shape=outs_shape,
        compiler_params=_cparams("arbitrary" if acc_outs else "parallel"),
    )(*[a for a, _ in ins])


def _rms_pre(h, w, *, name):
    T, D = h.shape

    def body(h_ref, w_ref, u_ref):
        x = h_ref[...]
        u_ref[...] = (x * _rstd(x) * w_ref[...]).astype(BF16)

    return _row_call(body, [(h, True), (w, False)], [jax.ShapeDtypeStruct((T, D), BF16)], name=name, rows=T, tr=256)[0]


def _rms_post_pre(h, m, w_post, w_pre, *, name):
    T, D = h.shape

    def body(h_ref, m_ref, wp_ref, wn_ref, hn_ref, u_ref):
        mm = m_ref[...].astype(F32)
        hn = h_ref[...] + mm * _rstd(mm) * wp_ref[...]
        hn_ref[...] = hn
        u_ref[...] = (hn * _rstd(hn) * wn_ref[...]).astype(BF16)

    return _row_call(body, [(h, True), (m, True), (w_post, False), (w_pre, False)],
                     [jax.ShapeDtypeStruct((T, D), F32), jax.ShapeDtypeStruct((T, D), BF16)], name=name, rows=T, tr=256)


def _rms_post_loss(h, m, w_post, tgt, *, name):
    T, D = h.shape

    def body(h_ref, m_ref, wp_ref, t_ref, dh_ref, loss_ref):
        @pl.when(pl.program_id(0) == 0)
        def _():
            loss_ref[...] = jnp.zeros_like(loss_ref)

        mm = m_ref[...].astype(F32)
        err = h_ref[...] + mm * _rstd(mm) * wp_ref[...] - t_ref[...]
        dh_ref[...] = err * (1.0 / D)
        loss_ref[...] += _colsum(jnp.sum(err * err, axis=1, keepdims=True))

    return _row_call(body, [(h, True), (m, True), (w_post, False), (tgt, True)],
                     [jax.ShapeDtypeStruct((T, D), F32), jax.ShapeDtypeStruct((1, 1), F32)],
                     name=name, rows=T, tr=256, acc_outs=(1,))


def _rms_bwd_vals(x, w, dy):
    r = _rstd(x)
    xh = x * r
    g = dy * w
    dx = r * (g - xh * jnp.mean(g * xh, axis=-1, keepdims=True))
    return dx, _colsum(dy * xh)


def _rms_post_bwd(m, w_post, dh, *, name):
    T, D = m.shape

    def body(m_ref, w_ref, dh_ref, dm_ref, dw_ref):
        @pl.when(pl.program_id(0) == 0)
        def _():
            dw_ref[...] = jnp.zeros_like(dw_ref)

        dx, dw = _rms_bwd_vals(m_ref[...].astype(F32), w_ref[...], dh_ref[...])
        dm_ref[...] = dx.astype(BF16)
        dw_ref[...] += dw

    return _row_call(body, [(m, True), (w_post, False), (dh, True)],
                     [jax.ShapeDtypeStruct((T, D), BF16), jax.ShapeDtypeStruct((1, D), F32)],
                     name=name, rows=T, tr=256, acc_outs=(1,))


def _rms_pre_bwd(h, w_pre, du, dh_out, *, name):
    T, D = h.shape

    def body(h_ref, w_ref, du_ref, dho_ref, dh_ref, dw_ref):
        @pl.when(pl.program_id(0) == 0)
        def _():
            dw_ref[...] = jnp.zeros_like(dw_ref)

        dx, dw = _rms_bwd_vals(h_ref[...], w_ref[...], du_ref[...].astype(F32))
        dh_ref[...] = dho_ref[...] + dx
        dw_ref[...] += dw

    return _row_call(body, [(h, True), (w_pre, False), (du, True), (dh_out, True)],
                     [jax.ShapeDtypeStruct((T, D), F32), jax.ShapeDtypeStruct((1, D), F32)],
                     name=name, rows=T, tr=256, acc_outs=(1,))


def _rms_pre_post_bwd(h, w_pre, du, dh_out, m_prev, w_post_prev, *, name):
    T, D = h.shape

    def body(h_ref, w_ref, du_ref, dho_ref, m_ref, wp_ref, dh_ref, dm_ref, dw_ref, dwp_ref):
        @pl.when(pl.program_id(0) == 0)
        def _():
            dw_ref[...] = jnp.zeros_like(dw_ref)
            dwp_ref[...] = jnp.zeros_like(dwp_ref)

        dx, dw = _rms_bwd_vals(h_ref[...], w_ref[...], du_ref[...].astype(F32))
        dh = dho_ref[...] + dx
        dh_ref[...] = dh
        dw_ref[...] += dw
        dm, dwp = _rms_bwd_vals(m_ref[...].astype(F32), wp_ref[...], dh)
        dm_ref[...] = dm.astype(BF16)
        dwp_ref[...] += dwp

    return _row_call(body, [(h, True), (w_pre, False), (du, True), (dh_out, True), (m_prev, True), (w_post_prev, False)],
                     [jax.ShapeDtypeStruct((T, D), F32), jax.ShapeDtypeStruct((T, D), BF16),
                      jax.ShapeDtypeStruct((1, D), F32), jax.ShapeDtypeStruct((1, D), F32)],
                     name=name, rows=T, tr=256, acc_outs=(2, 3))


def _ret_consts(T, C, H):
    lg = np.log1p(-np.exp2(-5.0 - np.arange(H, dtype=np.float64)))
    idx = np.arange(C, dtype=np.float64)
    dist = np.abs(idx[:, None] - idx[None, :])
    vis = (idx[None, :] // REF_CHUNK) <= (idx[:, None] // REF_CHUNK)
    mask = np.exp(dist[None] * lg[:, None, None]) * vis[None]
    xi = np.exp((idx[None, :] + 1.0) * lg[:, None])[..., None]
    zeta = np.exp((C - 1.0 - idx)[None, :] * lg[:, None])[..., None]
    half = RET_DK // 2
    inv_freq = ROPE_BASE ** (-np.arange(half, dtype=np.float32) / np.float32(half))
    ang = np.arange(T, dtype=np.float32)[:, None] * inv_freq[None, :].astype(np.float32)
    return (jnp.asarray(mask, F32), jnp.asarray(xi, F32), jnp.asarray(zeta, F32),
            jnp.asarray(np.cos(ang), F32), jnp.asarray(np.sin(ang), F32))


def _rot(t, cos, sin):
    half = RET_DK // 2
    t1, t2 = t[:, :half], t[:, half:]
    return jnp.concatenate([t1 * cos - t2 * sin, t1 * sin + t2 * cos], axis=1)


def _unrot(d, cos, sin):
    half = RET_DK // 2
    d1, d2 = d[:, :half], d[:, half:]
    return jnp.concatenate([d1 * cos + d2 * sin, d2 * cos - d1 * sin], axis=1)


def _ret_specs(C, H, rev, NS):
    def ci(i):
        return NS - 1 - i if rev else i

    nq = H
    q_spec = pl.BlockSpec((C, RET_DK), lambda h, i: (ci(i), h))
    k_spec = pl.BlockSpec((C, RET_DK), lambda h, i: (ci(i), nq + h))
    v_spec = pl.BlockSpec((C, RET_DV), lambda h, i: (ci(i), H + h))
    g_spec = pl.BlockSpec((C, RET_DV), lambda h, i: (ci(i), 2 * H + h))
    cs_spec = pl.BlockSpec((C, RET_DK // 2), lambda h, i: (ci(i), 0))
    m_spec = pl.BlockSpec((None, C, C), lambda h, i: (h, 0, 0))
    vec_spec = pl.BlockSpec((None, C, 1), lambda h, i: (h, 0, 0))
    gn_spec = pl.BlockSpec((1, RET_DV), lambda h, i: (0, h))
    st_spec = pl.BlockSpec((None, None, RET_DK, RET_DV), lambda h, i: (h, ci(i), 0, 0))
    return q_spec, k_spec, v_spec, g_spec, cs_spec, m_spec, vec_spec, gn_spec, st_spec


def _ret_fwd_vals(q, k, v, cos, sin, mask, xi, s_in):
    qr = _rot(q, cos, sin)
    kr = _rot(k, cos, sin) * (RET_DK ** -0.5)
    a = _dot(qr, kr, NT) * mask
    o = _dot(a, v) + _dot(qr, s_in) * xi
    mu = jnp.mean(o, axis=1, keepdims=True)
    oc = o - mu
    rstd = lax.rsqrt(jnp.mean(oc * oc, axis=1, keepdims=True) + GN_EPS)
    return qr, kr, a, oc * rstd, rstd


def _ret_fwd(proj, gn_w, consts, *, C, name):
    T = proj.shape[0]
    H = gn_w.shape[1] // RET_DV
    NS = T // C
    mask, xi, zeta, cos, sin = consts
    q_spec, k_spec, v_spec, g_spec, cs_spec, m_spec, vec_spec, gn_spec, st_spec = _ret_specs(C, H, False, NS)
    y_spec = pl.BlockSpec((C, RET_DV), lambda h, i: (i, h))

    def body(q_ref, k_ref, v_ref, g_ref, cos_ref, sin_ref, m_ref, xi_ref, ze_ref, gn_ref, y_ref, st_ref, S):
        @pl.when(pl.program_id(1) == 0)
        def _():
            S[...] = jnp.zeros_like(S)

        s_in = S[...]
        st_ref[...] = s_in
        v = v_ref[...]
        xi_v = xi_ref[...]
        qr, kr, a, on, rstd = _ret_fwd_vals(q_ref[...], k_ref[...], v, cos_ref[...], sin_ref[...], m_ref[...], xi_v, s_in)
        g = g_ref[...]
        y_ref[...] = (g * _sigmoid(g) * on * gn_ref[...]).astype(BF16)
        S[...] = s_in * xi_v[C - 1:C, :] + _dot(kr * ze_ref[...], v, TN)

    return pl.pallas_call(
        body, name=name, grid=(H, NS),
        in_specs=[q_spec, k_spec, v_spec, g_spec, cs_spec, cs_spec, m_spec, vec_spec, vec_spec, gn_spec],
        out_specs=[y_spec, st_spec],
        out_shape=[jax.ShapeDtypeStruct((T, H * RET_DV), BF16), jax.ShapeDtypeStruct((H, NS, RET_DK, RET_DV), F32)],
        scratch_shapes=[pltpu.VMEM((RET_DK, RET_DV), F32)],
        compiler_params=_cparams("parallel", "arbitrary"),
    )(proj, proj, proj, proj, cos, sin, mask, xi, zeta, gn_w)


def _stage_out(out_hbm, stage, sems, step, n_steps, row0, pieces, values):
    C = stage.shape[1]
    slot = step % 2

    def copies(sl):
        return [pltpu.make_async_copy(stage.at[sl, :, pl.ds(c0, w)],
                                      out_hbm.at[pl.ds(pl.multiple_of(row0, 16), C), pl.ds(pl.multiple_of(dc, 128), w)],
                                      sems.at[sl, n]) for n, (c0, w, dc) in enumerate(pieces)]

    @pl.when(step >= 2)
    def _():
        for cp in copies(slot):
            cp.wait()

    for (c0, w, _), v in zip(pieces, values):
        stage[slot, :, c0:c0 + w] = v
    for cp in copies(slot):
        cp.start()

    @pl.when(step == n_steps - 1)
    def _():
        for cp in copies(slot):
            cp.wait()
        if n_steps >= 2:
            for cp in copies(1 - slot):
                cp.wait()


def _ret_bwd(proj, gn_w, consts, states, dy, *, C, name):
    T = proj.shape[0]
    H = gn_w.shape[1] // RET_DV
    NS = T // C
    mask, xi, zeta, cos, sin = consts
    q_spec, k_spec, v_spec, g_spec, cs_spec, m_spec, vec_spec, gn_spec, st_spec = _ret_specs(C, H, True, NS)
    dy_spec = pl.BlockSpec((C, RET_DV), lambda h, i: (NS - 1 - i, h))
    scale = RET_DK ** -0.5
    wq, wv = H * RET_DK, H * RET_DV

    def body(q_ref, k_ref, v_ref, g_ref, cos_ref, sin_ref, m_ref, xi_ref, ze_ref, gn_ref, st_ref, dy_ref,
             dproj_ref, dgn_ref, dS, stage, sems):
        @pl.when(pl.program_id(1) == 0)
        def _():
            dS[...] = jnp.zeros_like(dS)
            dgn_ref[...] = jnp.zeros_like(dgn_ref)

        s_in = st_ref[...]
        v = v_ref[...]
        cos, sin, mask, xi_v, ze = cos_ref[...], sin_ref[...], m_ref[...], xi_ref[...], ze_ref[...]
        qr, kr, a, on, rstd = _ret_fwd_vals(q_ref[...], k_ref[...], v, cos, sin, mask, xi_v, s_in)
        g = g_ref[...]
        sg = _sigmoid(g)
        silu = g * sg
        gnw = gn_ref[...]
        dy = dy_ref[...].astype(F32)
        dg = (dy * on * gnw * (sg * (1.0 + g * (1.0 - sg)))).astype(BF16)
        t = dy * silu
        dgn_ref[...] += _colsum(t * on)
        don = t * gnw
        do = rstd * (don - jnp.mean(don, axis=1, keepdims=True) - on * jnp.mean(don * on, axis=1, keepdims=True))
        dox = do * xi_v
        ds_out = dS[...]
        da = _dot(do, v, NT) * mask
        kz = kr * ze
        dv = (_dot(a, do, TN) + _dot(kz, ds_out)).astype(BF16)
        dqr = _dot(da, kr) + _dot(dox, s_in, NT)
        dkr = _dot(da, qr, TN) + _dot(v, ds_out, NT) * ze
        dS[...] = ds_out * xi_v[C - 1:C, :] + _dot(qr, dox, TN)
        dq = _unrot(dqr, cos, sin).astype(BF16)
        dk = _unrot(dkr * scale, cos, sin).astype(BF16)
        h, i = pl.program_id(0), pl.program_id(1)
        pieces = [(0, RET_DK, h * RET_DK), (RET_DK, RET_DK, wq + h * RET_DK),
                  (2 * RET_DK, RET_DV, 2 * wq + h * RET_DV), (2 * RET_DK + RET_DV, RET_DV, 2 * wq + wv + h * RET_DV)]
        _stage_out(dproj_ref, stage, sems, h * NS + i, H * NS, (NS - 1 - i) * C, pieces, [dq, dk, dv, dg])

    return pl.pallas_call(
        body, name=name, grid=(H, NS),
        in_specs=[q_spec, k_spec, v_spec, g_spec, cs_spec, cs_spec, m_spec, vec_spec, vec_spec, gn_spec, st_spec, dy_spec],
        out_specs=[_ANY, gn_spec],
        out_shape=[jax.ShapeDtypeStruct((T, 2 * wq + 2 * wv), BF16), jax.ShapeDtypeStruct((1, H * RET_DV), F32)],
        scratch_shapes=[pltpu.VMEM((RET_DK, RET_DV), F32), pltpu.VMEM((2, C, 2 * RET_DK + 2 * RET_DV), BF16),
                        pltpu.SemaphoreType.DMA((2, 4))],
        compiler_params=_cparams("arbitrary", "arbitrary"),
    )(proj, proj, proj, proj, cos, sin, mask, xi, zeta, gn_w, states, dy)


def _shift_down(x, prev8, k):
    if k == 0:
        return x
    y = pltpu.roll(x, k, 0)
    row = lax.broadcasted_iota(jnp.int32, prev8.shape, 0)
    top = jnp.where(row < k, pltpu.roll(prev8, k, 0), y[:8])
    return jnp.concatenate([top, y[8:]], axis=0)


def _shift_up(x, next8, k):
    if k == 0:
        return x
    n = x.shape[0]
    y = pltpu.roll(x, n - k, 0)
    row = lax.broadcasted_iota(jnp.int32, next8.shape, 0)
    bot = jnp.where(row >= 8 - k, pltpu.roll(next8, 8 - k, 0), y[n - 8:])
    return jnp.concatenate([y[:n - 8], bot], axis=0)


def _conv_silu(raw, halo, w, b):
    cv = b
    for tap in range(SSD_CONV_W):
        cv = cv + _shift_down(raw, halo, SSD_CONV_W - 1 - tap) * w[tap:tap + 1, :]
    sg = _sigmoid(cv)
    return cv * sg, cv, sg


def _conv_silu_bwd(d_post, cv, sg, raw, halo, w, carry8):
    dcv = d_post * (sg * (1.0 + cv * (1.0 - sg)))
    d_raw = jnp.zeros_like(raw)
    dws = []
    for tap in range(SSD_CONV_W):
        k = SSD_CONV_W - 1 - tap
        d_raw = d_raw + _shift_up(dcv, carry8, k) * w[tap:tap + 1, :]
        dws.append(_colsum(dcv * _shift_down(raw, halo, k)))
    return d_raw, jnp.concatenate(dws, axis=0), _colsum(dcv), dcv[:8]


def _softplus(x):
    return jnp.maximum(x, 0.0) + jnp.log1p(jnp.exp(-jnp.abs(x)))


def _ssd_common(C, R, dt, dtT, bias, biasT, alog, alogT, E):
    p = dt + bias
    dtv = _softplus(p)
    a = -jnp.exp(alog)
    da = dtv * a
    daT = _softplus(dtT + biasT) * (-jnp.exp(alogT))
    row = lax.broadcasted_iota(jnp.int32, (C, C), 0)
    col = lax.broadcasted_iota(jnp.int32, (C, C), 1)
    tril = row >= col
    trilf = jnp.where(tril, 1.0, 0.0).astype(F32)
    triuf = jnp.where(col >= row, 1.0, 0.0).astype(F32)
    acum = _dot_sel(trilf, da, split="b")
    acumT = _dot_sel(daT, trilf, NT, split="a")
    al = acum[C - 1:C, :]
    ea = jnp.exp(acum)
    dte = jnp.exp(al - acum)
    eal = jnp.exp(al)
    return dict(p=p, dtv=dtv, a=a, da=da, tril=tril, triuf=triuf, acum=acum, acumT=acumT, al=al, ea=ea, dte=dte, eal=eal,
                dtv_e=_dot_sel(dtv, E, split="a", terms=2), ea_e=_dot_sel(ea, E, split="a", terms=2),
                dte_e=_dot_sel(dte, E, split="a", terms=2), eal_e=_dot_sel(eal, E, split="a"))


def _head_decay(q, r, C, R):
    seg = jnp.broadcast_to(q["acum"][:, r:r + 1], (C, C)) - q["acumT"][r:r + 1, :]
    return jnp.exp(jnp.where(q["tril"], seg, -1e30))


def _ssd_group_specs(C, R, NS, rev):
    RP = R * SSD_P
    G = SSD_G
    hb = C // 8

    def ci(i):
        return NS - 1 - i if rev else i

    def halo_row(i):
        return jnp.maximum(ci(i) * hb - 1, 0)

    off_b = G * RP // SSD_N
    z_spec = pl.BlockSpec((C, RP), lambda g, i: (ci(i), g))
    x_spec = pl.BlockSpec((C, RP), lambda g, i: (ci(i), G + g))
    b_spec = pl.BlockSpec((C, SSD_N), lambda g, i: (ci(i), 2 * off_b + g))
    c_spec = pl.BlockSpec((C, SSD_N), lambda g, i: (ci(i), 2 * off_b + G + g))
    xh_spec = pl.BlockSpec((8, RP), lambda g, i: (halo_row(i), G + g))
    bh_spec = pl.BlockSpec((8, SSD_N), lambda g, i: (halo_row(i), 2 * off_b + g))
    ch_spec = pl.BlockSpec((8, SSD_N), lambda g, i: (halo_row(i), 2 * off_b + G + g))
    dt_spec = pl.BlockSpec((None, C, R), lambda g, i: (g, ci(i), 0))
    dtT_spec = pl.BlockSpec((None, R, C), lambda g, i: (g, 0, ci(i)))
    pr_spec = pl.BlockSpec((None, 1, R), lambda g, i: (g, 0, 0))
    prT_spec = pl.BlockSpec((None, R, 1), lambda g, i: (g, 0, 0))
    cwx_spec = pl.BlockSpec((SSD_CONV_W, RP), lambda g, i: (0, g))
    cwn_spec = pl.BlockSpec((SSD_CONV_W, SSD_N), lambda g, i: (0, g))
    cbx_spec = pl.BlockSpec((1, RP), lambda g, i: (0, g))
    cbn_spec = pl.BlockSpec((1, SSD_N), lambda g, i: (0, g))
    e_spec = pl.BlockSpec((R, RP), lambda g, i: (0, 0))
    st_spec = pl.BlockSpec((None, None, SSD_N, RP), lambda g, i: (g, ci(i), 0, 0))
    return dict(z=z_spec, x=x_spec, b=b_spec, c=c_spec, xh=xh_spec, bh=bh_spec, ch=ch_spec, dt=dt_spec, dtT=dtT_spec,
                pr=pr_spec, prT=prT_spec, cwx=cwx_spec, cwn=cwn_spec, cbx=cbx_spec, cbn=cbn_spec, e=e_spec, st=st_spec)


def _ssd_forward_vals(C, R, refs, first, s_in):
    E = refs["E"]
    halo_on = jnp.where(first, 0.0, 1.0)
    xh, bh, ch = refs["xh"] * halo_on, refs["bh"] * halo_on, refs["ch"] * halo_on
    xs, cvx, sgx = _conv_silu(refs["x"], xh, refs["cwx"], refs["cbx"])
    bm, cvb, sgb = _conv_silu(refs["b"], bh, refs["cwb"], refs["cbb"])
    cm, cvc, sgc = _conv_silu(refs["c"], ch, refs["cwc"], refs["cbc"])
    q = _ssd_common(C, R, refs["dt"], refs["dtT"], refs["bias"], refs["biasT"], refs["alog"], refs["alogT"], E)
    xdt = xs * q["dtv_e"]
    cb = _dot(cm, bm, NT)
    yoff_raw = _dot(cm, s_in)
    xdt_b = xdt.astype(BF16)
    low = lax.broadcasted_iota(jnp.int32, (1, 2 * SSD_P), 1) < SSD_P
    pairs = []
    for j in range(R // 2):
        xp = xdt_b[:, 2 * SSD_P * j:2 * SSD_P * (j + 1)]
        y0 = _dot(cb * _head_decay(q, 2 * j, C, R), xp)
        y1 = _dot(cb * _head_decay(q, 2 * j + 1, C, R), xp)
        pairs.append(jnp.where(low, y0, y1))
    ydiag = jnp.concatenate(pairs, axis=1)
    d_e =_dot_sel(refs["dskip"], E, split="a")
    y = ydiag + yoff_raw * q["ea_e"] + d_e * xs
    xd = xdt * q["dte_e"]
    s_out = s_in * q["eal_e"] + _dot(bm, xd, TN)
    z = refs["z"]
    sgz = _sigmoid(z)
    yz = y * (z * sgz)
    rn = lax.rsqrt(jnp.mean(yz * yz, axis=1, keepdims=True) + RMS_EPS)
    return dict(q=q, xh=xh, bh=bh, ch=ch, xs=xs, cvx=cvx, sgx=sgx, bm=bm, cvb=cvb, sgb=sgb, cm=cm, cvc=cvc, sgc=sgc,
                xdt=xdt, cb=cb, yoff_raw=yoff_raw, d_e=d_e, y=y, xd=xd, s_out=s_out, z=z, sgz=sgz, yz=yz, rn=rn)


_SSD_IN_NAMES = ("z", "x", "b", "c", "xh", "bh", "ch", "dt", "dtT", "bias", "biasT", "alog", "alogT", "dskip",
                 "cwx", "cwb", "cwc", "cbx", "cbb", "cbc", "nw", "E")


def _ssd_inputs(pm, dt_g, dtT_g, prm, sp):
    bias, biasT, alog, alogT, dskip, cwx, cwb, cwc, cbx, cbb, cbc, nw, E = prm
    args = [pm, pm, pm, pm, pm, pm, pm, dt_g, dtT_g, bias, biasT, alog, alogT, dskip, cwx, cwb, cwc, cbx, cbb, cbc, nw, E]
    specs = [sp["z"], sp["x"], sp["b"], sp["c"], sp["xh"], sp["bh"], sp["ch"], sp["dt"], sp["dtT"], sp["pr"], sp["prT"],
             sp["pr"], sp["prT"], sp["pr"], sp["cwx"], sp["cwn"], sp["cwn"], sp["cbx"], sp["cbn"], sp["cbn"], sp["cbx"], sp["e"]]
    return args, specs


def _ssd_fwd(pm, dt_g, dtT_g, prm, *, C, R, name):
    T = pm.shape[0]
    NS = T // C
    RP = R * SSD_P
    G = SSD_G
    sp = _ssd_group_specs(C, R, NS, False)
    args, specs = _ssd_inputs(pm, dt_g, dtT_g, prm, sp)
    nin = len(args)

    def body(*refs):
        ins = {n: r[...] for n, r in zip(_SSD_IN_NAMES, refs[:nin])}
        y_ref, st_ref, S = refs[nin:]
        first = pl.program_id(1) == 0

        @pl.when(first)
        def _():
            S[...] = jnp.zeros_like(S)

        s_in = S[...]
        st_ref[...] = s_in
        f = _ssd_forward_vals(C, R, ins, first, s_in)
        y_ref[...] = (f["yz"] * f["rn"] * ins["nw"]).astype(BF16)
        S[...] = f["s_out"]

    return pl.pallas_call(
        body, name=name, grid=(G, NS), in_specs=specs,
        out_specs=[pl.BlockSpec((C, RP), lambda g, i: (i, g)), sp["st"]],
        out_shape=[jax.ShapeDtypeStruct((T, G * RP), BF16), jax.ShapeDtypeStruct((G, NS, SSD_N, RP), F32)],
        scratch_shapes=[pltpu.VMEM((SSD_N, RP), F32)],
        compiler_params=_cparams("parallel", "arbitrary"),
    )(*args)


def _ssd_bwd(pm, dt_g, dtT_g, prm, states, dout, *, C, R, name):
    T = pm.shape[0]
    NS = T // C
    RP = R * SSD_P
    G = SSD_G
    sp = _ssd_group_specs(C, R, NS, True)
    args, specs = _ssd_inputs(pm, dt_g, dtT_g, prm, sp)
    nin = len(args)
    rows_spec = pl.BlockSpec((C, RP), lambda g, i: (NS - 1 - i, g))
    args = args + [states, dout]
    specs = specs + [sp["st"], rows_spec]

    def body(*refs):
        ins = {n: r[...] for n, r in zip(_SSD_IN_NAMES, refs[:nin])}
        st_ref, dout_ref = refs[nin], refs[nin + 1]
        (dpm_ref, ddt_ref, dbias_ref, dalog_ref, dd_ref, dcwx_ref, dcwb_ref, dcwc_ref,
         dcbx_ref, dcbb_ref, dcbc_ref, dnw_ref) = refs[nin + 2:nin + 14]
        dS, cx8, cb8, cc8, stage, sems = refs[nin + 14:]
        acc_refs = (dbias_ref, dalog_ref, dd_ref, dcwx_ref, dcwb_ref, dcwc_ref, dcbx_ref, dcbb_ref, dcbc_ref, dnw_ref)
        step = pl.program_id(1)

        @pl.when(step == 0)
        def _():
            for r_ in acc_refs + (dS, cx8, cb8, cc8):
                r_[...] = jnp.zeros_like(r_)

        first = step == NS - 1
        E = ins["E"]
        s_in = st_ref[...]
        f = _ssd_forward_vals(C, R, ins, first, s_in)
        q = f["q"]
        xs, bm, cm, xdt, cb, y, z, sgz, yz, rn = (f[n] for n in ("xs", "bm", "cm", "xdt", "cb", "y", "z", "sgz", "yz", "rn"))
        nw = ins["nw"]
        dout = dout_ref[...].astype(F32)
        yh = yz * rn
        dnw_ref[...] += _colsum(dout * yh)
        g1 = dout * nw
        dyz = rn * (g1 - yh * jnp.mean(g1 * yh, axis=1, keepdims=True))
        dz = (dyz * y * (sgz * (1.0 + z * (1.0 - sgz)))).astype(BF16)
        dy = dyz * (z * sgz)
        dd_ref[...] += _dot_sel(_colsum(dy * xs), E, NT, split="a")
        dxs = dy * f["d_e"]
        dyo = dy * q["ea_e"]
        dcm = _dot(dyo, s_in, NT)
        ds_acc = _dot(cm, dyo, TN)
        dacum = _dot_sel(dy * f["yoff_raw"], E, NT, split="a", terms=1) * q["ea"]
        dacumT = jnp.zeros((R, C), F32)
        dcb = jnp.zeros((C, C), F32)
        rowR = lax.broadcasted_iota(jnp.int32, (1, R), 1)
        rowRT = lax.broadcasted_iota(jnp.int32, (R, 1), 0)
        dy_b, xdt_b = dy.astype(BF16), xdt.astype(BF16)
        low = lax.broadcasted_iota(jnp.int32, (1, 2 * SSD_P), 1) < SSD_P
        dxdt_pairs = []
        for j in range(R // 2):
            lanes = slice(2 * SSD_P * j, 2 * SSD_P * (j + 1))
            dyp, xp = dy_b[:, lanes], xdt_b[:, lanes]
            halves = []
            for r, mine in ((2 * j, low), (2 * j + 1, jnp.logical_not(low))):
                lr = _head_decay(q, r, C, R)
                w_r = cb * lr
                dw = _dot(jnp.where(mine, dyp, jnp.zeros_like(dyp)), xp, NT)
                halves.append(_dot(w_r, dyp, TN))
                dcb = dcb + dw * lr
                dseg = dw * w_r
                dacum = dacum + jnp.sum(dseg, axis=1, keepdims=True) * jnp.where(rowR == r, 1.0, 0.0)
                dacumT = dacumT - _colsum(dseg) * jnp.where(rowRT == r, 1.0, 0.0)
            dxdt_pairs.append(jnp.where(low, halves[0], halves[1]))
        dxdt = jnp.concatenate(dxdt_pairs, axis=1)
        dsn = dS[...]
        ds_acc = ds_acc + dsn * q["eal_e"]
        d_eal = _dot_sel(_colsum(dsn * s_in), E, NT, split="a")
        dbm = _dot(f["xd"], dsn, NT)
        dxd = _dot(bm, dsn)
        dxdt = dxdt + dxd * q["dte_e"]
        d_dte = _dot_sel(dxd * xdt, E, NT, split="a", terms=1) * q["dte"]
        d_al = _colsum(d_dte) + d_eal * q["eal"]
        dacum = dacum - d_dte
        rowC = lax.broadcasted_iota(jnp.int32, (C, 1), 0)
        dacum = dacum + jnp.where(rowC == C - 1, 1.0, 0.0) * d_al
        dS[...] = ds_acc
        dcm = dcm + _dot(dcb, bm)
        dbm = dbm + _dot(dcb, cm, TN)
        eye = jnp.where(lax.broadcasted_iota(jnp.int32, (C, C), 0) == lax.broadcasted_iota(jnp.int32, (C, C), 1), 1.0, 0.0)
        dacum = dacum + _dot_sel(eye, dacumT, NT, split="b")
        dda = _dot_sel(q["triuf"], dacum, split="b")
        ddtv = dda * q["a"] + _dot_sel(dxdt * xs, E, NT, split="a", terms=1)
        dalog_ref[...] += _colsum(dda * q["dtv"]) * q["a"]
        dxs = dxs + dxdt * q["dtv_e"]
        dp = ddtv * _sigmoid(q["p"])
        ddt_ref[...] = dp
        dbias_ref[...] += _colsum(dp)
        d_raw, d_w, d_b, c8 = _conv_silu_bwd(dxs, f["cvx"], f["sgx"], ins["x"], f["xh"], ins["cwx"], cx8[...])
        dx = d_raw.astype(BF16)
        dcwx_ref[...] += d_w
        dcbx_ref[...] += d_b
        cx8[...] = c8
        d_raw, d_w, d_b, c8 = _conv_silu_bwd(dbm, f["cvb"], f["sgb"], ins["b"], f["bh"], ins["cwb"], cb8[...])
        db = d_raw.astype(BF16)
        dcwb_ref[...] += d_w
        dcbb_ref[...] += d_b
        cb8[...] = c8
        d_raw, d_w, d_b, c8 = _conv_silu_bwd(dcm, f["cvc"], f["sgc"], ins["c"], f["ch"], ins["cwc"], cc8[...])
        dc = d_raw.astype(BF16)
        dcwc_ref[...] += d_w
        dcbc_ref[...] += d_b
        cc8[...] = c8
        g_ = pl.program_id(0)
        pieces = [(0, RP, g_ * RP), (RP, RP, G * RP + g_ * RP), (2 * RP, SSD_N, 2 * G * RP + g_ * SSD_N),
                  (2 * RP + SSD_N, SSD_N, 2 * G * RP + G * SSD_N + g_ * SSD_N)]
        _stage_out(dpm_ref, stage, sems, g_ * NS + step, G * NS, (NS - 1 - step) * C, pieces, [dz, dx, db, dc])

    out_specs = [_ANY, pl.BlockSpec((None, C, R), lambda g, i: (g, NS - 1 - i, 0)),
                 sp["pr"], sp["pr"], sp["pr"], sp["cwx"], sp["cwn"], sp["cwn"], sp["cbx"], sp["cbn"], sp["cbn"], sp["cbx"]]
    out_shape = [jax.ShapeDtypeStruct((T, 2 * G * RP + 2 * G * SSD_N), BF16),
                 jax.ShapeDtypeStruct((G, T, R), F32),
                 jax.ShapeDtypeStruct((G, 1, R), F32), jax.ShapeDtypeStruct((G, 1, R), F32), jax.ShapeDtypeStruct((G, 1, R), F32),
                 jax.ShapeDtypeStruct((SSD_CONV_W, G * RP), F32), jax.ShapeDtypeStruct((SSD_CONV_W, G * SSD_N), F32),
                 jax.ShapeDtypeStruct((SSD_CONV_W, G * SSD_N), F32),
                 jax.ShapeDtypeStruct((1, G * RP), F32), jax.ShapeDtypeStruct((1, G * SSD_N), F32),
                 jax.ShapeDtypeStruct((1, G * SSD_N), F32), jax.ShapeDtypeStruct((1, G * RP), F32)]
    return pl.pallas_call(
        body, name=name, grid=(G, NS), in_specs=specs, out_specs=out_specs, out_shape=out_shape,
        scratch_shapes=[pltpu.VMEM((SSD_N, RP), F32), pltpu.VMEM((8, RP), F32), pltpu.VMEM((8, SSD_N), F32),
                        pltpu.VMEM((8, SSD_N), F32), pltpu.VMEM((2, C, 2 * RP + 2 * SSD_N), BF16),
                        pltpu.SemaphoreType.DMA((2, 4))],
        compiler_params=_cparams("arbitrary", "arbitrary"),
    )(*args)


_ANY = pl.BlockSpec(memory_space=pl.ANY)


def _chip_peer(k):
    x, y, c = lax.axis_index("x"), lax.axis_index("y"), lax.axis_index("c")
    return (x ^ (k >> 1), y ^ (k & 1), c)


def _my_chip():
    return 2 * lax.axis_index("x") + lax.axis_index("y")


def _all_gather_chips(shards, halved, *, name):
    n = len(shards)

    def body(*refs):
        ins, outs = refs[:n], refs[n:2 * n]
        send, recv, fsend, frecv, loc = refs[2 * n:]
        s = _my_chip()
        c = lax.axis_index("c")
        sibling = (lax.axis_index("x"), lax.axis_index("y"), 1 - c)
        copies = []
        for a in range(n):
            cp = pltpu.make_async_copy(ins[a], outs[a].at[s], loc.at[a])
            cp.start()
            copies.append(cp)

        def rows(a, core):
            if not halved[a]:
                return slice(None)
            half = shards[a].shape[0] // 2
            return pl.ds(pl.multiple_of(core * half, 16), half)

        def over_ici(a, k, slot, core):
            return pltpu.make_async_remote_copy(
                src_ref=ins[a].at[rows(a, core)], dst_ref=outs[a].at[slot, rows(a, core)],
                send_sem=send.at[3 * a + k - 1], recv_sem=recv.at[3 * a + k - 1],
                device_id=_chip_peer(k), device_id_type=MESH_ID)

        def over_d2d(a, k, core):
            z = outs[a].at[s ^ k, rows(a, core)]
            return pltpu.make_async_remote_copy(
                src_ref=z, dst_ref=z, send_sem=fsend.at[3 * a + k - 1], recv_sem=frecv.at[3 * a + k - 1],
                device_id=sibling, device_id_type=MESH_ID)

        sent = []
        for a in range(n):
            for k in (1, 2, 3):
                cp = over_ici(a, k, s, c)
                cp.start()
                sent.append(cp)
        passed = []
        for a in range(n):
            for k in (1, 2, 3):
                over_ici(a, k, s ^ k, c).wait_recv()
                if halved[a]:
                    cp = over_d2d(a, k, c)
                    cp.start()
                    passed.append(cp)
        for a in range(n):
            if halved[a]:
                for k in (1, 2, 3):
                    over_d2d(a, k, 1 - c).wait_recv()
        for cp in sent + passed:
            cp.wait_send()
        for cp in copies:
            cp.wait()

    for a, h in zip(shards, halved):
        assert not h or a.shape[0] % 32 == 0, a.shape
    return pl.pallas_call(
        body, name=name, in_specs=[_ANY] * n, out_specs=[_ANY] * n,
        out_shape=[jax.ShapeDtypeStruct((4,) + a.shape, a.dtype) for a in shards],
        scratch_shapes=[pltpu.SemaphoreType.DMA((3 * n,))] * 4 + [pltpu.SemaphoreType.DMA((n,))],
        compiler_params=pltpu.CompilerParams(has_side_effects=True),
    )(*shards)


_HBM = pl.BlockSpec(memory_space=pltpu.HBM)
_SEM = pl.BlockSpec(memory_space=pltpu.SEMAPHORE)
_EFFECT = pltpu.SideEffectType.DATAFLOW_SIDE_EFFECTING


def _split_copies(src, land, send, recv, loc, a, scatter):
    s = _my_chip()
    mine = pltpu.make_async_copy(src.at[s] if scatter else src, land.at[s], loc.at[a])
    pairs = []
    for k in (1, 2, 3):
        sems = dict(send_sem=send.at[3 * a + k - 1], recv_sem=recv.at[3 * a + k - 1],
                    device_id=_chip_peer(k), device_id_type=MESH_ID)
        out = pltpu.make_async_remote_copy(src_ref=src.at[s ^ k] if scatter else src, dst_ref=land.at[s], **sems)
        arriving = pltpu.make_async_remote_copy(src_ref=src.at[s ^ k] if scatter else src, dst_ref=land.at[s ^ k], **sems)
        pairs.append((out, arriving))
    return mine, pairs


def _split_start(arrs, *, scatter, after, name):
    n = len(arrs)
    zones = [lax.empty(a.shape if scatter else (4,) + a.shape, a.dtype) for a in arrs]

    def body(*refs):
        srcs, lands = refs[:n], refs[n:2 * n]
        send, recv, loc = refs[2 * n + 1:2 * n + 4]
        token = refs[-1]
        for a in range(n):
            mine, pairs = _split_copies(srcs[a], lands[a], send, recv, loc, a, scatter)
            mine.start()
            for out, _ in pairs:
                out.start()
        token[...] = jnp.zeros_like(token)

    res = pl.pallas_call(
        body, name=name,
        out_shape=(pltpu.SemaphoreType.DMA((3 * n,)), pltpu.SemaphoreType.DMA((3 * n,)), pltpu.SemaphoreType.DMA((n,)),
                   *[pltpu.HBM(z.shape, z.dtype) for z in zones], jax.ShapeDtypeStruct((8, 128), F32)),
        in_specs=[_ANY] * n + [_HBM] * n + [_ANY],
        out_specs=(_SEM, _SEM, _SEM, *([_HBM] * n), pl.BlockSpec(memory_space=pltpu.VMEM)),
        input_output_aliases={n + i: 3 + i for i in range(n)},
        compiler_params=pltpu.CompilerParams(has_side_effects=_EFFECT),
    )(*arrs, *[pltpu.with_memory_space_constraint(z, pltpu.HBM) for z in zones], after)
    return res[:3], list(res[3:3 + n]), res[-1]


def _split_wait(sems, src, land, a, *, scatter, after, name):
    def body(src_ref, land_ref, send, recv, loc, after_ref, land_out):
        mine, pairs = _split_copies(src_ref, land_ref, send, recv, loc, a, scatter)
        mine.wait()
        for out, arriving in pairs:
            out.wait_send()
            arriving.wait_recv()

    return pl.pallas_call(
        body, name=name, out_shape=pltpu.HBM(land.shape, land.dtype),
        in_specs=[_ANY, _HBM, _SEM, _SEM, _SEM, _ANY], out_specs=_HBM, input_output_aliases={1: 0},
        compiler_params=pltpu.CompilerParams(has_side_effects=_EFFECT),
    )(src, land, *sems, after)


def _sibling_copies(srcs, lands, send, recv):
    sib = (lax.axis_index("x"), lax.axis_index("y"), 1 - lax.axis_index("c"))
    return [pltpu.make_async_remote_copy(src_ref=srcs[a], dst_ref=lands[a], send_sem=send.at[a], recv_sem=recv.at[a],
                                         device_id=sib, device_id_type=MESH_ID) for a in range(len(srcs))]


def _swap_start(arrs, *, after, name):
    n = len(arrs)
    zones = [lax.empty(a.shape, a.dtype) for a in arrs]

    def body(*refs):
        for cp in _sibling_copies(refs[:n], refs[n:2 * n], refs[2 * n + 1], refs[2 * n + 2]):
            cp.start()

    res = pl.pallas_call(
        body, name=name,
        out_shape=(pltpu.SemaphoreType.DMA((n,)), pltpu.SemaphoreType.DMA((n,)), *[pltpu.HBM(z.shape, z.dtype) for z in zones]),
        in_specs=[_ANY] * n + [_HBM] * n + [_ANY], out_specs=(_SEM, _SEM, *([_HBM] * n)),
        input_output_aliases={n + i: 2 + i for i in range(n)},
        compiler_params=pltpu.CompilerParams(has_side_effects=_EFFECT),
    )(*arrs, *[pltpu.with_memory_space_constraint(z, pltpu.HBM) for z in zones], after)
    return res[:2], list(res[2:])


def _swap_wait(sems, srcs, lands, *, after, name):
    n = len(srcs)

    def body(*refs):
        for cp in _sibling_copies(refs[:n], refs[n:2 * n], refs[2 * n], refs[2 * n + 1]):
            cp.wait_send()
            cp.wait_recv()

    res = pl.pallas_call(
        body, name=name, out_shape=tuple(pltpu.HBM(a.shape, a.dtype) for a in lands),
        in_specs=[_ANY] * n + [_HBM] * n + [_SEM, _SEM, _ANY], out_specs=tuple([_HBM] * n),
        input_output_aliases={n + i: i for i in range(n)},
        compiler_params=pltpu.CompilerParams(has_side_effects=_EFFECT),
    )(*srcs, *lands, *sems, after)
    return list(res)


def _all_gather_devices(v, *, name):
    r = v.shape[0]

    def body(v_ref, out_ref, send, recv):
        x, y, c = lax.axis_index("x"), lax.axis_index("y"), lax.axis_index("c")
        me = 4 * x + 2 * y + c
        out_ref[me] = v_ref[...]
        cps = []
        for k in range(1, 8):
            peer = (x ^ (k >> 2), y ^ ((k >> 1) & 1), c ^ (k & 1))
            cp = pltpu.make_async_remote_copy(src_ref=v_ref, dst_ref=out_ref.at[me], send_sem=send.at[k - 1],
                                              recv_sem=recv.at[k - 1], device_id=peer, device_id_type=MESH_ID)
            cp.start()
            cps.append(cp)
        for k, cp in enumerate(cps, start=1):
            cp.wait_send()
            pltpu.make_async_remote_copy(src_ref=v_ref, dst_ref=out_ref.at[me ^ k], send_sem=send.at[k - 1],
                                         recv_sem=recv.at[k - 1], device_id=(x, y, c), device_id_type=MESH_ID).wait_recv()

    vm = pl.BlockSpec(memory_space=pltpu.VMEM)
    return pl.pallas_call(
        body, name=name, in_specs=[vm], out_specs=vm, out_shape=jax.ShapeDtypeStruct((8, r, 128), F32),
        scratch_shapes=[pltpu.SemaphoreType.DMA((7,)), pltpu.SemaphoreType.DMA((7,))],
        compiler_params=pltpu.CompilerParams(has_side_effects=True),
    )(v)


def _row_tile(r, target):
    best = None
    for t in range(16, min(target, r) + 1, 16):
        if r % t == 0:
            best = t
    return best or r


def _sum_slots(buf, *, name, tr=384):
    S, r, c = buf.shape
    tr = _row_tile(r, tr)

    def body(b_ref, o_ref):
        acc = b_ref[0].astype(F32)
        for j in range(1, S):
            acc = acc + b_ref[j].astype(F32)
        o_ref[...] = acc

    return pl.pallas_call(
        body, name=name, grid=(r // tr,), in_specs=[pl.BlockSpec((S, tr, c), lambda i: (0, i, 0))],
        out_specs=pl.BlockSpec((tr, c), lambda i: (i, 0)), out_shape=jax.ShapeDtypeStruct((r, c), F32),
        compiler_params=_cparams("parallel"),
    )(buf)


ADAMW_BLOCK_ELEMS = 1 << 18


def _adamw(w, gs, m, v, *, name, tr=256, layer=None, stack=None):
    r, c = w.shape[-2:]
    tr = _row_tile(r, min(tr, max(16, ADAMW_BLOCK_ELEMS // c)))
    bc1 = 1.0 - ADAM_B1 ** ADAM_STEP
    bc2 = 1.0 - ADAM_B2 ** ADAM_STEP
    ng = len(gs)

    def body(*refs):
        w_ref, m_ref, v_ref = refs[0], refs[1 + ng], refs[2 + ng]
        g_ref, d_ref, mo_ref, vo_ref = refs[-4:]
        gg = refs[1][...] if ng == 1 else refs[1][...] + refs[2][...]
        mn = ADAM_B1 * m_ref[...] + (1.0 - ADAM_B1) * gg
        vn = ADAM_B2 * v_ref[...] + (1.0 - ADAM_B2) * (gg * gg)
        g_ref[...] = gg
        mo_ref[...] = mn
        vo_ref[...] = vn
        d_ref[...] = -ADAM_LR * ((mn / bc1) / (jnp.sqrt(vn / bc2) + ADAM_EPS) + ADAM_WD * w_ref[...])

    spec = pl.BlockSpec((tr, c), lambda i: (i, 0))
    if layer is None:
        wspec, shape = spec, (r, c)
    else:
        wspec, shape = pl.BlockSpec((None, tr, c), lambda i: (layer, i, 0)), (2, r, c)
    in_specs = [wspec] + [spec] * ng + [wspec, wspec]
    args, alias = [w, *gs, m, v], {}
    if stack is not None:
        in_specs += [_ANY] * 4
        alias = {len(args) + n: n for n in range(4)}
        args += list(stack)
    return pl.pallas_call(body, name=name, grid=(r // tr,), in_specs=in_specs, out_specs=[wspec] * 4,
                          out_shape=[jax.ShapeDtypeStruct(shape, F32)] * 4, input_output_aliases=alias,
                          compiler_params=_cparams("parallel"))(*args)


def _pack(vecs, rows):
    flat = jnp.concatenate([v.reshape(-1).astype(F32) for v in vecs])
    return jnp.pad(flat, (0, rows * 128 - flat.shape[0])).reshape(rows, 128)


def _unpack(packed, shapes):
    flat = packed.reshape(-1)
    out, off = [], 0
    for s in shapes:
        n = math.prod(s)
        out.append(flat[off:off + n].reshape(s))
        off += n
    return out


def _pack_rows(shapes):
    n = sum(math.prod(s) for s in shapes)
    return -(-n // 1024) * 8


def kernel(x, norm_mix_pre, norm_mix_post, norm_ffn_pre, norm_ffn_post, ret_w_in, ret_gn_w, ret_w_out, ssd_w_in, ssd_conv_w, ssd_conv_b, ssd_dt_bias, ssd_a_log, ssd_d, ssd_norm_w, ssd_w_out, mlp_w_up, mlp_w_down, loss_target, m_norm_mix_pre, m_norm_mix_post, m_norm_ffn_pre, m_norm_ffn_post, m_ret_w_in, m_ret_gn_w, m_ret_w_out, m_ssd_w_in, m_ssd_conv_w, m_ssd_conv_b, m_ssd_dt_bias, m_ssd_a_log, m_ssd_d, m_ssd_norm_w, m_ssd_w_out, m_mlp_w_up, m_mlp_w_down, v_norm_mix_pre, v_norm_mix_post, v_norm_ffn_pre, v_norm_ffn_post, v_ret_w_in, v_ret_gn_w, v_ret_w_out, v_ssd_w_in, v_ssd_conv_w, v_ssd_conv_b, v_ssd_dt_bias, v_ssd_a_log, v_ssd_d, v_ssd_norm_w, v_ssd_w_out, v_mlp_w_up, v_mlp_w_down):
    T, D = x.shape[1], x.shape[2]
    H = D // RET_DK
    d_inner = 2 * D
    R = d_inner // SSD_P // SSD_G
    RP = R * SSD_P
    n_heads = SSD_G * R
    conv_dim = d_inner + 2 * SSD_G * SSD_N
    n_main = d_inner + conv_dim
    C = min(256, T)
    chip = _my_chip()
    xs, tgt = x[0], loss_target[0]

    conv_sh = ssd_conv_w.shape[2]
    small_shapes = [(SSD_CONV_W, conv_sh), (conv_sh,), (ssd_norm_w.shape[1],)]
    small_rows = _pack_rows(small_shapes)
    shards = [ret_w_in[0].T.astype(BF16), ret_w_out[0].astype(BF16), ssd_w_in[0].T.astype(BF16), ssd_w_out[0].astype(BF16),
              mlp_w_up[0].T.astype(BF16), mlp_w_up[1].T.astype(BF16), mlp_w_down[0].astype(BF16), mlp_w_down[1].astype(BF16)]
    (ret_in_g, small_g) = _all_gather_chips([shards[0], _pack([ssd_conv_w[0], ssd_conv_b[0], ssd_norm_w[0]], small_rows)],
                                            [True, False], name="gather_first")

    def full(g):
        return g.reshape(4 * g.shape[1], g.shape[2])

    def start_gather(idx, after, name):
        sems, zones, tok = _split_start([shards[i] for i in idx], scatter=False, after=after, name=name)
        return {i: (sems, shards[i], zones[n], n) for n, i in enumerate(idx)}, tok

    def arrived(stage, i, after, name):
        sems, src, zone, n = stage[i]
        return full(_split_wait(sems, src, zone, n, scatter=False, after=after, name=name))

    ret_in_t = full(ret_in_g)
    sm = [_unpack(small_g[j], small_shapes) for j in range(4)]
    conv_w = jnp.concatenate([sm[j][0] for j in range(4)], axis=1)
    conv_b = jnp.concatenate([sm[j][1] for j in range(4)])[None, :]
    norm_w = jnp.concatenate([sm[j][2] for j in range(4)])[None, :]

    gb = SSD_G * SSD_N
    ssd_prm = (ssd_dt_bias.reshape(SSD_G, 1, R), ssd_dt_bias.reshape(SSD_G, R, 1),
               ssd_a_log.reshape(SSD_G, 1, R), ssd_a_log.reshape(SSD_G, R, 1), ssd_d.reshape(SSD_G, 1, R),
               conv_w[:, :d_inner], conv_w[:, d_inner:d_inner + gb], conv_w[:, d_inner + gb:],
               conv_b[:, :d_inner], conv_b[:, d_inner:d_inner + gb], conv_b[:, d_inner + gb:],
               norm_w, jnp.asarray(np.kron(np.eye(R), np.ones((1, SSD_P))), F32))
    ret_consts = _ret_consts(T, C, H)

    u0 = _rms_pre(xs, norm_mix_pre[0:1], name="pre0")
    stage1, tok = start_gather((1, 4), ret_in_g, "gather_start1")
    proj = _matmul(u0, ret_in_t, "nt", out_dtype=F32, name="ret_in", after=tok)
    stage2, tok = start_gather((6, 2), proj, "gather_start2")
    y_ret, st_ret = _ret_fwd(proj, ret_gn_w, ret_consts, C=C, name="ret_fwd")
    ret_out = arrived(stage1, 1, y_ret, "gather_wait_ret_out")
    m0 = _matmul(y_ret, ret_out, "nn", out_dtype=BF16, name="ret_out", after=tok)
    h1, u1 = _rms_post_pre(xs, m0, norm_mix_post[0:1], norm_ffn_pre[0:1], name="post_pre1")
    up_t0 = arrived(stage1, 4, u1, "gather_wait_up0")
    a0, hh0 = _matmul(u1, up_t0, "nt", out_dtype=BF16, name="mlp_up0", epi="relu2")
    stage3, tok = start_gather((3, 5, 7), hh0, "gather_start3")
    down0 = arrived(stage2, 6, hh0, "gather_wait_down0")
    f0 = _matmul(hh0, down0, "nn", out_dtype=BF16, name="mlp_down0", after=tok)
    h2, u2 = _rms_post_pre(h1, f0, norm_ffn_post[0:1], norm_mix_pre[1:2], name="post_pre2")
    ssd_in_t = arrived(stage2, 2, u2, "gather_wait_ssd_in")
    pm, pdt = _matmul(u2, ssd_in_t, "nt", out_dtype=F32, name="ssd_in", tail=(n_heads,))
    dt_g = pdt.reshape(T, SSD_G, R).transpose(1, 0, 2)
    dtT_g = pdt.reshape(T, SSD_G, R).transpose(1, 2, 0)
    y_ssd, st_ssd = _ssd_fwd(pm, dt_g, dtT_g, ssd_prm, C=C, R=R, name="ssd_fwd")
    ssd_out = arrived(stage3, 3, y_ssd, "gather_wait_ssd_out")
    m1 = _matmul(y_ssd, ssd_out, "nn", out_dtype=BF16, name="ssd_out")
    h3, u3 = _rms_post_pre(h2, m1, norm_mix_post[1:2], norm_ffn_pre[1:2], name="post_pre3")
    up_t1 = arrived(stage3, 5, u3, "gather_wait_up1")
    a1, hh1 = _matmul(u3, up_t1, "nt", out_dtype=BF16, name="mlp_up1", epi="relu2")
    down1 = arrived(stage3, 7, hh1, "gather_wait_down1")
    f1 = _matmul(hh1, down1, "nn", out_dtype=BF16, name="mlp_down1")
    up_t, down = (up_t0, up_t1), (down0, down1)
    dh4, sq = _rms_post_loss(h3, f1, norm_ffn_post[1:2], tgt, name="post_loss")
    loss = lax.psum(sq[0, 0], MESH_AXES) * (0.5 / D)

    in_flight = []

    def send_grad(g, name):
        part = g if g.ndim == 3 else g.reshape(4, g.shape[0] // 4, g.shape[1])
        sems, zones, tok = _split_start([part], scatter=True, after=part, name=f"scatter_start_{name}")
        in_flight.append((name, sems, part, zones[0]))
        return tok

    retired = []

    def retire(name, after):
        nm, sems, src, zone = in_flight.pop(0)
        assert nm == name
        part = _sum_slots(_split_wait(sems, src, zone, 0, scatter=True, after=after, name=f"scatter_wait_{nm}"),
                          name=f"sum_chips_{nm}")
        sw_sems, zones = _swap_start([part], after=part, name=f"swap_start_{nm}")
        retired.append((nm, part, sw_sems, zones))

    def mlp_bwd(i, df, u, a, hh):
        tok = send_grad(_matmul(hh, df, "tn", out_dtype=BF16, name=f"mlp_down_wg{i}"), f"down{i}")
        da = _matmul(df, down[i], "nt", out_dtype=BF16, name=f"mlp_down_dg{i}", epi="drelu2", extra=a, after=tok)
        tok = send_grad(_matmul(u, da, "tn", out_dtype=BF16, name=f"mlp_up_wg{i}", col_parts=4), f"up{i}")
        return _matmul(da, up_t[i], "nn", out_dtype=BF16, name=f"mlp_up_dg{i}", after=tok)

    df1, d_nfpost1 = _rms_post_bwd(f1, norm_ffn_post[1:2], dh4, name="post_bwd_ffn1")
    du3 = mlp_bwd(1, df1, u3, a1, hh1)
    dh3, dm1, d_nfp1, d_nmpost1 = _rms_pre_post_bwd(h3, norm_ffn_pre[1:2], du3, dh4, m1, norm_mix_post[1:2],
                                                    name="pre_bwd_ffn1_post_bwd_mix1")
    tok = send_grad(_matmul(y_ssd, dm1, "tn", out_dtype=BF16, name="ssd_out_wg"), "ssd_out")
    dy_ssd = _matmul(dm1, ssd_out, "nt", out_dtype=F32, name="ssd_out_dg", after=tok)
    retire("down1", dy_ssd)
    (dpm, ddt_g, d_bias, d_alog, d_dskip, dcwx, dcwb, dcwc, dcbx, dcbb, dcbc, d_normw) = _ssd_bwd(
        pm, dt_g, dtT_g, ssd_prm, st_ssd, dy_ssd, C=C, R=R, name="ssd_bwd")
    retire("up1", dpm)
    dpdt = ddt_g.transpose(1, 0, 2).reshape(T, n_heads).astype(BF16)
    tok = send_grad(jnp.concatenate([_matmul(dpm, u2, "tn", out_dtype=BF16, name="ssd_in_wg"),
                                     _matmul(dpdt, u2, "tn", out_dtype=BF16, name="ssd_in_dt_wg")], axis=0), "ssd_in")
    du2 = _matmul(dpm, ssd_in_t, "nn", out_dtype=BF16, name="ssd_in_dg", after=tok, tail=(n_heads, dpdt))
    retire("ssd_out", du2)
    dh2, df0, d_nmp1, d_nfpost0 = _rms_pre_post_bwd(h2, norm_mix_pre[1:2], du2, dh3, f0, norm_ffn_post[0:1],
                                                    name="pre_bwd_mix1_post_bwd_ffn0")
    du1 = mlp_bwd(0, df0, u1, a0, hh0)
    retire("ssd_in", du1)
    retire("down0", du1)
    dh1, dm0, d_nfp0, d_nmpost0 = _rms_pre_post_bwd(h1, norm_ffn_pre[0:1], du1, dh2, m0, norm_mix_post[0:1],
                                                    name="pre_bwd_ffn0_post_bwd_mix0")
    tok = send_grad(_matmul(y_ret, dm0, "tn", out_dtype=BF16, name="ret_out_wg"), "ret_out")
    dy_ret = _matmul(dm0, ret_out, "nt", out_dtype=F32, name="ret_out_dg", after=tok)
    retire("up0", dy_ret)
    dproj, d_gn = _ret_bwd(proj, ret_gn_w, ret_consts, st_ret, dy_ret, C=C, name="ret_bwd")
    tok = send_grad(_matmul(u0, dproj, "tn", out_dtype=BF16, name="ret_in_wg", col_parts=4), "ret_in")
    du0 = _matmul(dproj, ret_in_t, "nn", out_dtype=BF16, name="ret_in_dg", after=tok)
    retire("ret_out", du0)
    grad_x, d_nmp0 = _rms_pre_bwd(xs, norm_mix_pre[0:1], du0, dh1, name="pre_bwd_mix0")

    retire("ret_in", grad_x)

    def upd(w, gs, m, v, name):
        shp = w.shape
        w2, m2, v2 = (t.reshape(-1, shp[-1]) for t in (w, m, v))
        return tuple(t.reshape(shp) for t in _adamw(w2, gs, m2, v2, name=name))

    def upd_t(w, gs, m, v, name):
        return tuple(t.T[None] for t in _adamw(w[0].T, gs, m[0].T, v[0].T, name=name))

    def upd_layer(layer):
        def fn(w, gs, m, v, name):
            return tuple(_adamw(w, gs, m, v, name=name, layer=layer, stack=res.get(out_of[name[len("adamw_"):]])))
        return fn

    out_of = {"ret_in": "ret_w_in", "ret_out": "ret_w_out", "ssd_in": "ssd_w_in", "ssd_out": "ssd_w_out",
              "up0": "mlp_w_up", "up1": "mlp_w_up", "down0": "mlp_w_down", "down1": "mlp_w_down"}
    todo = {"ret_in": (upd, ret_w_in, m_ret_w_in, v_ret_w_in), "ret_out": (upd, ret_w_out, m_ret_w_out, v_ret_w_out),
            "ssd_in": (upd_t, ssd_w_in, m_ssd_w_in, v_ssd_w_in), "ssd_out": (upd, ssd_w_out, m_ssd_w_out, v_ssd_w_out),
            "up0": (upd_layer(0), mlp_w_up, m_mlp_w_up, v_mlp_w_up), "up1": (upd_layer(1), mlp_w_up, m_mlp_w_up, v_mlp_w_up),
            "down0": (upd_layer(0), mlp_w_down, m_mlp_w_down, v_mlp_w_down),
            "down1": (upd_layer(1), mlp_w_down, m_mlp_w_down, v_mlp_w_down)}
    res = {}
    prev = grad_x
    for nm, mine, sems, zones in retired:
        other = _swap_wait(sems, [mine], zones, after=prev, name=f"swap_wait_{nm}")[0]
        fn, w, m, v = todo[nm]
        res[out_of[nm]] = fn(w, [mine, other], m, v, f"adamw_{nm}")
        prev = res[out_of[nm]][0]

    d_conv_w = jnp.concatenate([dcwx, dcwb, dcwc], axis=1)
    d_conv_b = jnp.concatenate([dcbx, dcbb, dcbc], axis=1)
    small_grads = [jnp.concatenate([d_nmp0, d_nmp1]), jnp.concatenate([d_nmpost0, d_nmpost1]),
                   jnp.concatenate([d_nfp0, d_nfp1]), jnp.concatenate([d_nfpost0, d_nfpost1]),
                   d_gn, d_bias.reshape(1, n_heads), d_alog.reshape(1, n_heads), d_dskip.reshape(1, n_heads),
                   d_conv_w, d_conv_b, d_normw]
    sg_shapes = [g.shape for g in small_grads]
    sg_rows = _pack_rows(sg_shapes)
    everyone = _all_gather_devices(_pack(small_grads, sg_rows), name="gather_small_grads")
    sg = _unpack(_sum_slots(everyone, name="sum_small_grads", tr=sg_rows), sg_shapes)
    (g_nmp, g_nmpost, g_nfp, g_nfpost, g_gn, g_bias, g_alog, g_dskip, g_cw_full, g_cb_full, g_nw_full) = sg
    g_cw = lax.dynamic_slice_in_dim(g_cw_full, chip * conv_sh, conv_sh, axis=1)[None]
    g_cb = lax.dynamic_slice_in_dim(g_cb_full, chip * conv_sh, conv_sh, axis=1)
    nw_sh = ssd_norm_w.shape[1]
    g_nw = lax.dynamic_slice_in_dim(g_nw_full, chip * nw_sh, nw_sh, axis=1)
    small = [("norm_mix_pre", norm_mix_pre, g_nmp, m_norm_mix_pre, v_norm_mix_pre),
             ("norm_mix_post", norm_mix_post, g_nmpost, m_norm_mix_post, v_norm_mix_post),
             ("norm_ffn_pre", norm_ffn_pre, g_nfp, m_norm_ffn_pre, v_norm_ffn_pre),
             ("norm_ffn_post", norm_ffn_post, g_nfpost, m_norm_ffn_post, v_norm_ffn_post),
             ("ret_gn_w", ret_gn_w, g_gn, m_ret_gn_w, v_ret_gn_w),
             ("ssd_conv_w", ssd_conv_w, g_cw, m_ssd_conv_w, v_ssd_conv_w),
             ("ssd_conv_b", ssd_conv_b, g_cb, m_ssd_conv_b, v_ssd_conv_b),
             ("ssd_dt_bias", ssd_dt_bias, g_bias, m_ssd_dt_bias, v_ssd_dt_bias),
             ("ssd_a_log", ssd_a_log, g_alog, m_ssd_a_log, v_ssd_a_log),
             ("ssd_d", ssd_d, g_dskip, m_ssd_d, v_ssd_d),
             ("ssd_norm_w", ssd_norm_w, g_nw, m_ssd_norm_w, v_ssd_norm_w)]
    sw_shapes = [w.shape for _, w, _, _, _ in small]
    sw_rows = _pack_rows(sw_shapes)
    packs = [_pack([t[j] for t in small], sw_rows) for j in (1, 2, 3, 4)]
    _, d_p, m_p, v_p = _adamw(packs[0], [packs[1]], packs[2], packs[3], name="adamw_small", tr=sw_rows)
    d_s, m_s, v_s = _unpack(d_p, sw_shapes), _unpack(m_p, sw_shapes), _unpack(v_p, sw_shapes)
    for j, (nm, w, g, _, _) in enumerate(small):
        res[nm] = (g.reshape(w.shape), d_s[j], m_s[j], v_s[j])

    order = ["norm_mix_pre", "norm_mix_post", "norm_ffn_pre", "norm_ffn_post", "ret_w_in", "ret_gn_w", "ret_w_out",
             "ssd_w_in", "ssd_conv_w", "ssd_conv_b", "ssd_dt_bias", "ssd_a_log", "ssd_d", "ssd_norm_w", "ssd_w_out",
             "mlp_w_up", "mlp_w_down"]
    return (loss, grad_x[None], *[res[n][0] for n in order], *[res[n][1] for n in order],
            *[res[n][2] for n in order], *[res[n][3] for n in order])
```

```python
import math

import numpy as np
import jax
import jax.numpy as jnp
from jax import lax
from jax.experimental import pallas as pl
from jax.experimental.pallas import tpu as pltpu

F32 = jnp.float32
BF16 = jnp.bfloat16
VMEM_LIMIT_BYTES = 56 * 1024 * 1024
MESH_AXES = ("x", "y", "c")
MESH_ID = pl.DeviceIdType.MESH

RMS_EPS = 1e-6
GN_EPS = 1e-5
RET_DK = 256
RET_DV = 512
ROPE_BASE = 10000.0
REF_CHUNK = 64
SSD_P = 64
SSD_N = 128
SSD_G = 8
SSD_CONV_W = 4
ADAM_LR, ADAM_B1, ADAM_B2, ADAM_EPS, ADAM_WD, ADAM_STEP = 0.001, 0.9, 0.999, 1e-08, 0.01, 10

NN = (((1,), (0,)), ((), ()))
NT = (((1,), (1,)), ((), ()))
TN = (((0,), (0,)), ((), ()))


def _cparams(*sem):
    return pltpu.CompilerParams(dimension_semantics=sem, vmem_limit_bytes=VMEM_LIMIT_BYTES)


def _dot(a, b, dims=NN):
    return lax.dot_general(a.astype(BF16), b.astype(BF16), dims, preferred_element_type=F32)


def _split_bf16(x, terms):
    parts, rest = [], x
    for _ in range(terms):
        p = rest.astype(BF16)
        parts.append(p)
        rest = rest - p.astype(F32)
    return parts


def _dot_sel(a, b, dims=NN, *, split, terms=3):
    if split == "a":
        sel = b.astype(BF16)
        return sum(lax.dot_general(p, sel, dims, preferred_element_type=F32) for p in _split_bf16(a, terms))
    sel = a.astype(BF16)
    return sum(lax.dot_general(sel, p, dims, preferred_element_type=F32) for p in _split_bf16(b, terms))


def _sigmoid(x):
    return 1.0 / (1.0 + jnp.exp(-x))


def _colsum(x):
    return jnp.sum(x, axis=0, keepdims=True)


MM_TILE = 1024
MM_FULL_K = 2048


def _mm_tiles(M, N, K):
    if K <= MM_FULL_K:
        return min(M, 2 * MM_TILE), min(N, MM_TILE), K
    return min(M, MM_TILE), min(N, 2 * MM_TILE), MM_TILE


def _matmul(a, b, mode, *, out_dtype, name, epi=None, extra=None, after=None, col_parts=None, tail=None):
    nt_ = tail[0] if tail else 0
    if mode == "nn":
        (M, K), (K2, N) = a.shape, (b.shape[0] - nt_, b.shape[1])
    elif mode == "nt":
        (M, K), (N, K2) = a.shape, (b.shape[0] - nt_, b.shape[1])
    else:
        (K, M), (K2, N) = a.shape, b.shape
    assert K == K2, (a.shape, b.shape, mode)
    tm, tn, tk = _mm_tiles(M, N, K)
    if col_parts:
        tm, tn = min(M, 2 * MM_TILE), min(tn, MM_TILE)
        while (N // col_parts) % tn:
            tn //= 2
    assert M % tm == 0 and N % tn == 0 and K % tk == 0, (M, N, K, tm, tn, tk)
    nk = K // tk
    if mode == "tn":
        a_spec = pl.BlockSpec((tk, tm), lambda i, j, k: (k, i))
    else:
        a_spec = pl.BlockSpec((tm, tk), lambda i, j, k: (i, k))
    if mode == "nt":
        b_spec = pl.BlockSpec((tn, tk), lambda i, j, k: (j, k))
    else:
        b_spec = pl.BlockSpec((tk, tn), lambda i, j, k: (k, j))
    dims = {"nn": NN, "nt": NT, "tn": TN}[mode]
    o_spec = pl.BlockSpec((tm, tn), lambda i, j, k: (i, j))
    out_dims = (M, N)
    if col_parts:
        per = N // col_parts // tn
        o_spec = pl.BlockSpec((None, tm, tn), lambda i, j, k: (j // per, i, j % per))
        out_dims = (col_parts, M, N // col_parts)
    has_extra = epi in ("drelu2", "add")
    n_out = 2 if epi == "relu2" else 1

    in_specs = [a_spec, b_spec] + ([o_spec] if has_extra else [])
    args = [a, b] + ([extra] if has_extra else [])
    if after is not None:
        in_specs.append(pl.BlockSpec(after.shape, lambda i, j, k: (0, 0)))
        args.append(after)
    n_plain = len(args)
    out_specs = [o_spec] * n_out
    out_shape = [jax.ShapeDtypeStruct(out_dims, out_dtype)] * n_out
    if tail and mode == "nt":
        assert nk == 1 and N % nt_ == 0
        in_specs.append(pl.BlockSpec((nt_, tk), lambda i, j, k: (N // nt_, 0)))
        args.append(b)
        out_specs.append(pl.BlockSpec((tm, nt_), lambda i, j, k: (i, 0)))
        out_shape.append(jax.ShapeDtypeStruct((M, nt_), F32))
    elif tail:
        assert mode == "nn" and K % nt_ == 0
        in_specs += [pl.BlockSpec((tm, nt_), lambda i, j, k: (i, 0)), pl.BlockSpec((nt_, tn), lambda i, j, k: (K // nt_, j))]
        args += [tail[1], b]
    n_in = len(args)

    def body(*refs):
        a_ref, b_ref = refs[0], refs[1]
        e_ref = refs[2] if has_extra else None
        outs = refs[n_in:n_in + n_out]

        def finish(r):
            if tail and mode == "nn":
                r = r + _dot(refs[n_plain][...], refs[n_plain + 1][...])
            if epi is None:
                outs[0][...] = r.astype(outs[0].dtype)
            elif epi == "relu2":
                outs[0][...] = r.astype(outs[0].dtype)
                h = jnp.maximum(r, 0.0)
                outs[1][...] = (h * h).astype(outs[1].dtype)
            elif epi == "drelu2":
                av = jnp.maximum(e_ref[...].astype(F32), 0.0)
                outs[0][...] = (r * (2.0 * av)).astype(outs[0].dtype)
            else:
                outs[0][...] = (r + e_ref[...].astype(F32)).astype(outs[0].dtype)

        if tail and mode == "nt":
            @pl.when(pl.program_id(1) == 0)
            def _():
                refs[n_in + n_out][...] = _dot(a_ref[...], refs[n_plain][...], NT)

        if nk == 1:
            finish(_dot(a_ref[...], b_ref[...], dims))
            return
        acc = refs[-1]
        k = pl.program_id(2)

        @pl.when(k == 0)
        def _():
            acc[...] = jnp.zeros_like(acc)

        acc[...] += _dot(a_ref[...], b_ref[...], dims)

        @pl.when(k == nk - 1)
        def _():
            finish(acc[...])

    res = pl.pallas_call(
        body, name=name, grid=(M // tm, N // tn, nk), in_specs=in_specs, out_specs=out_specs,
        out_shape=out_shape, scratch_shapes=[pltpu.VMEM((tm, tn), F32)] if nk > 1 else [],
        compiler_params=_cparams("parallel", "arbitrary" if tail and mode == "nt" else "parallel", "arbitrary"),
    )(*args)
    return res if len(res) > 1 else res[0]


def _rstd(x):
    return lax.rsqrt(jnp.mean(x * x, axis=-1, keepdims=True) + RMS_EPS)


def _row_call(body, ins, outs_shape, *, name, rows, tr, acc_outs=()):
    tr = min(tr, rows)
    assert rows % tr == 0
    in_specs = []
    for arr, blocked in ins:
        if blocked:
            in_specs.append(pl.BlockSpec((tr, arr.shape[1]), lambda i: (i, 0)))
        else:
            in_specs.append(pl.BlockSpec(arr.shape, lambda i: (0, 0)))
    out_specs = []
    for n, s in enumerate(outs_shape):
        if n in acc_outs:
            out_specs.append(pl.BlockSpec(s.shape, lambda i: (0, 0)))
        else:
            out_specs.append(pl.BlockSpec((tr, s.shape[1]), lambda i: (i, 0)))
    return pl.pallas_call(
        body, name=name, grid=(rows // tr,), in_specs=in_specs, out_specs=out_specs, out_shape=outs_shape,
        compiler_params=_cparams("arbitrary" if acc_outs else "parallel"),
    )(*[a for a, _ in ins])


def _rms_pre(h, w, *, name):
    T, D = h.shape

    def body(h_ref, w_ref, u_ref):
        x = h_ref[...]
        u_ref[...] = (x * _rstd(x) * w_ref[...]).astype(BF16)

    return _row_call(body, [(h, True), (w, False)], [jax.ShapeDtypeStruct((T, D), BF16)], name=name, rows=T, tr=256)[0]


def _rms_post_pre(h, m, w_post, w_pre, *, name):
    T, D = h.shape

    def body(h_ref, m_ref, wp_ref, wn_ref, hn_ref, u_ref):
        mm = m_ref[...].astype(F32)
        hn = h_ref[...] + mm * _rstd(mm) * wp_ref[...]
        hn_ref[...] = hn
        u_ref[...] = (hn * _rstd(hn) * wn_ref[...]).astype(BF16)

    return _row_call(body, [(h, True), (m, True), (w_post, False), (w_pre, False)],
                     [jax.ShapeDtypeStruct((T, D), F32), jax.ShapeDtypeStruct((T, D), BF16)], name=name, rows=T, tr=256)


def _rms_post_loss(h, m, w_post, tgt, *, name):
    T, D = h.shape

    def body(h_ref, m_ref, wp_ref, t_ref, dh_ref, loss_ref):
        @pl.when(pl.program_id(0) == 0)
        def _():
            loss_ref[...] = jnp.zeros_like(loss_ref)

        mm = m_ref[...].astype(F32)
        err = h_ref[...] + mm * _rstd(mm) * wp_ref[...] - t_ref[...]
        dh_ref[...] = err * (1.0 / D)
        loss_ref[...] += _colsum(jnp.sum(err * err, axis=1, keepdims=True))

    return _row_call(body, [(h, True), (m, True), (w_post, False), (tgt, True)],
                     [jax.ShapeDtypeStruct((T, D), F32), jax.ShapeDtypeStruct((1, 1), F32)],
                     name=name, rows=T, tr=256, acc_outs=(1,))


def _rms_bwd_vals(x, w, dy):
    r = _rstd(x)
    xh = x * r
    g = dy * w
    dx = r * (g - xh * jnp.mean(g * xh, axis=-1, keepdims=True))
    return dx, _colsum(dy * xh)


def _rms_post_bwd(m, w_post, dh, *, name):
    T, D = m.shape

    def body(m_ref, w_ref, dh_ref, dm_ref, dw_ref):
        @pl.when(pl.program_id(0) == 0)
        def _():
            dw_ref[...] = jnp.zeros_like(dw_ref)

        dx, dw = _rms_bwd_vals(m_ref[...].astype(F32), w_ref[...], dh_ref[...])
        dm_ref[...] = dx.astype(BF16)
        dw_ref[...] += dw

    return _row_call(body, [(m, True), (w_post, False), (dh, True)],
                     [jax.ShapeDtypeStruct((T, D), BF16), jax.ShapeDtypeStruct((1, D), F32)],
                     name=name, rows=T, tr=256, acc_outs=(1,))


def _rms_pre_bwd(h, w_pre, du, dh_out, *, name):
    T, D = h.shape

    def body(h_ref, w_ref, du_ref, dho_ref, dh_ref, dw_ref):
        @pl.when(pl.program_id(0) == 0)
        def _():
            dw_ref[...] = jnp.zeros_like(dw_ref)

        dx, dw = _rms_bwd_vals(h_ref[...], w_ref[...], du_ref[...].astype(F32))
        dh_ref[...] = dho_ref[...] + dx
        dw_ref[...] += dw

    return _row_call(body, [(h, True), (w_pre, False), (du, True), (dh_out, True)],
                     [jax.ShapeDtypeStruct((T, D), F32), jax.ShapeDtypeStruct((1, D), F32)],
                     name=name, rows=T, tr=256, acc_outs=(1,))


def _rms_pre_post_bwd(h, w_pre, du, dh_out, m_prev, w_post_prev, *, name):
    T, D = h.shape

    def body(h_ref, w_ref, du_ref, dho_ref, m_ref, wp_ref, dh_ref, dm_ref, dw_ref, dwp_ref):
        @pl.when(pl.program_id(0) == 0)
        def _():
            dw_ref[...] = jnp.zeros_like(dw_ref)
            dwp_ref[...] = jnp.zeros_like(dwp_ref)

        dx, dw = _rms_bwd_vals(h_ref[...], w_ref[...], du_ref[...].astype(F32))
        dh = dho_ref[...] + dx
        dh_ref[...] = dh
        dw_ref[...] += dw
        dm, dwp = _rms_bwd_vals(m_ref[...].astype(F32), wp_ref[...], dh)
        dm_ref[...] = dm.astype(BF16)
        dwp_ref[...] += dwp

    return _row_call(body, [(h, True), (w_pre, False), (du, True), (dh_out, True), (m_prev, True), (w_post_prev, False)],
                     [jax.ShapeDtypeStruct((T, D), F32), jax.ShapeDtypeStruct((T, D), BF16),
                      jax.ShapeDtypeStruct((1, D), F32), jax.ShapeDtypeStruct((1, D), F32)],
                     name=name, rows=T, tr=256, acc_outs=(2, 3))


def _ret_consts(T, C, H):
    lg = np.log1p(-np.exp2(-5.0 - np.arange(H, dtype=np.float64)))
    idx = np.arange(C, dtype=np.float64)
    dist = np.abs(idx[:, None] - idx[None, :])
    vis = (idx[None, :] // REF_CHUNK) <= (idx[:, None] // REF_CHUNK)
    mask = np.exp(dist[None] * lg[:, None, None]) * vis[None]
    xi = np.exp((idx[None, :] + 1.0) * lg[:, None])[..., None]
    zeta = np.exp((C - 1.0 - idx)[None, :] * lg[:, None])[..., None]
    half = RET_DK // 2
    inv_freq = ROPE_BASE ** (-np.arange(half, dtype=np.float32) / np.float32(half))
    ang = np.arange(T, dtype=np.float32)[:, None] * inv_freq[None, :].astype(np.float32)
    return (jnp.asarray(mask, F32), jnp.asarray(xi, F32), jnp.asarray(zeta, F32),
            jnp.asarray(np.cos(ang), F32), jnp.asarray(np.sin(ang), F32))


def _rot(t, cos, sin):
    half = RET_DK // 2
    t1, t2 = t[:, :half], t[:, half:]
    return jnp.concatenate([t1 * cos - t2 * sin, t1 * sin + t2 * cos], axis=1)


def _unrot(d, cos, sin):
    half = RET_DK // 2
    d1, d2 = d[:, :half], d[:, half:]
    return jnp.concatenate([d1 * cos + d2 * sin, d2 * cos - d1 * sin], axis=1)


def _ret_specs(C, H, rev, NS):
    def ci(i):
        return NS - 1 - i if rev else i

    nq = H
    q_spec = pl.BlockSpec((C, RET_DK), lambda h, i: (ci(i), h))
    k_spec = pl.BlockSpec((C, RET_DK), lambda h, i: (ci(i), nq + h))
    v_spec = pl.BlockSpec((C, RET_DV), lambda h, i: (ci(i), H + h))
    g_spec = pl.BlockSpec((C, RET_DV), lambda h, i: (ci(i), 2 * H + h))
    cs_spec = pl.BlockSpec((C, RET_DK // 2), lambda h, i: (ci(i), 0))
    m_spec = pl.BlockSpec((None, C, C), lambda h, i: (h, 0, 0))
    vec_spec = pl.BlockSpec((None, C, 1), lambda h, i: (h, 0, 0))
    gn_spec = pl.BlockSpec((1, RET_DV), lambda h, i: (0, h))
    st_spec = pl.BlockSpec((None, None, RET_DK, RET_DV), lambda h, i: (h, ci(i), 0, 0))
    return q_spec, k_spec, v_spec, g_spec, cs_spec, m_spec, vec_spec, gn_spec, st_spec


def _ret_fwd_vals(q, k, v, cos, sin, mask, xi, s_in):
    qr = _rot(q, cos, sin)
    kr = _rot(k, cos, sin) * (RET_DK ** -0.5)
    a = _dot(qr, kr, NT) * mask
    o = _dot(a, v) + _dot(qr, s_in) * xi
    mu = jnp.mean(o, axis=1, keepdims=True)
    oc = o - mu
    rstd = lax.rsqrt(jnp.mean(oc * oc, axis=1, keepdims=True) + GN_EPS)
    return qr, kr, a, oc * rstd, rstd


def _ret_fwd(proj, gn_w, consts, *, C, name):
    T = proj.shape[0]
    H = gn_w.shape[1] // RET_DV
    NS = T // C
    mask, xi, zeta, cos, sin = consts
    q_spec, k_spec, v_spec, g_spec, cs_spec, m_spec, vec_spec, gn_spec, st_spec = _ret_specs(C, H, False, NS)
    y_spec = pl.BlockSpec((C, RET_DV), lambda h, i: (i, h))

    def body(q_ref, k_ref, v_ref, g_ref, cos_ref, sin_ref, m_ref, xi_ref, ze_ref, gn_ref, y_ref, st_ref, S):
        @pl.when(pl.program_id(1) == 0)
        def _():
            S[...] = jnp.zeros_like(S)

        s_in = S[...]
        st_ref[...] = s_in
        v = v_ref[...]
        xi_v = xi_ref[...]
        qr, kr, a, on, rstd = _ret_fwd_vals(q_ref[...], k_ref[...], v, cos_ref[...], sin_ref[...], m_ref[...], xi_v, s_in)
        g = g_ref[...]
        y_ref[...] = (g * _sigmoid(g) * on * gn_ref[...]).astype(BF16)
        S[...] = s_in * xi_v[C - 1:C, :] + _dot(kr * ze_ref[...], v, TN)

    return pl.pallas_call(
        body, name=name, grid=(H, NS),
        in_specs=[q_spec, k_spec, v_spec, g_spec, cs_spec, cs_spec, m_spec, vec_spec, vec_spec, gn_spec],
        out_specs=[y_spec, st_spec],
        out_shape=[jax.ShapeDtypeStruct((T, H * RET_DV), BF16), jax.ShapeDtypeStruct((H, NS, RET_DK, RET_DV), F32)],
        scratch_shapes=[pltpu.VMEM((RET_DK, RET_DV), F32)],
        compiler_params=_cparams("parallel", "arbitrary"),
    )(proj, proj, proj, proj, cos, sin, mask, xi, zeta, gn_w)


def _stage_out(out_hbm, stage, sems, step, n_steps, row0, pieces, values):
    C = stage.shape[1]
    slot = step % 2

    def copies(sl):
        return [pltpu.make_async_copy(stage.at[sl, :, pl.ds(c0, w)],
                                      out_hbm.at[pl.ds(pl.multiple_of(row0, 16), C), pl.ds(pl.multiple_of(dc, 128), w)],
                                      sems.at[sl, n]) for n, (c0, w, dc) in enumerate(pieces)]

    @pl.when(step >= 2)
    def _():
        for cp in copies(slot):
            cp.wait()

    for (c0, w, _), v in zip(pieces, values):
        stage[slot, :, c0:c0 + w] = v
    for cp in copies(slot):
        cp.start()

    @pl.when(step == n_steps - 1)
    def _():
        for cp in copies(slot):
            cp.wait()
        if n_steps >= 2:
            for cp in copies(1 - slot):
                cp.wait()


def _ret_bwd(proj, gn_w, consts, states, dy, *, C, name):
    T = proj.shape[0]
    H = gn_w.shape[1] // RET_DV
    NS = T // C
    mask, xi, zeta, cos, sin = consts
    q_spec, k_spec, v_spec, g_spec, cs_spec, m_spec, vec_spec, gn_spec, st_spec = _ret_specs(C, H, True, NS)
    dy_spec = pl.BlockSpec((C, RET_DV), lambda h, i: (NS - 1 - i, h))
    scale = RET_DK ** -0.5
    wq, wv = H * RET_DK, H * RET_DV

    def body(q_ref, k_ref, v_ref, g_ref, cos_ref, sin_ref, m_ref, xi_ref, ze_ref, gn_ref, st_ref, dy_ref,
             dproj_ref, dgn_ref, dS, stage, sems):
        @pl.when(pl.program_id(1) == 0)
        def _():
            dS[...] = jnp.zeros_like(dS)
            dgn_ref[...] = jnp.zeros_like(dgn_ref)

        s_in = st_ref[...]
        v = v_ref[...]
        cos, sin, mask, xi_v, ze = cos_ref[...], sin_ref[...], m_ref[...], xi_ref[...], ze_ref[...]
        qr, kr, a, on, rstd = _ret_fwd_vals(q_ref[...], k_ref[...], v, cos, sin, mask, xi_v, s_in)
        g = g_ref[...]
        sg = _sigmoid(g)
        silu = g * sg
        gnw = gn_ref[...]
        dy = dy_ref[...].astype(F32)
        dg = (dy * on * gnw * (sg * (1.0 + g * (1.0 - sg)))).astype(BF16)
        t = dy * silu
        dgn_ref[...] += _colsum(t * on)
        don = t * gnw
        do = rstd * (don - jnp.mean(don, axis=1, keepdims=True) - on * jnp.mean(don * on, axis=1, keepdims=True))
        dox = do * xi_v
        ds_out = dS[...]
        da = _dot(do, v, NT) * mask
        kz = kr * ze
        dv = (_dot(a, do, TN) + _dot(kz, ds_out)).astype(BF16)
        dqr = _dot(da, kr) + _dot(dox, s_in, NT)
        dkr = _dot(da, qr, TN) + _dot(v, ds_out, NT) * ze
        dS[...] = ds_out * xi_v[C - 1:C, :] + _dot(qr, dox, TN)
        dq = _unrot(dqr, cos, sin).astype(BF16)
        dk = _unrot(dkr * scale, cos, sin).astype(BF16)
        h, i = pl.program_id(0), pl.program_id(1)
        pieces = [(0, RET_DK, h * RET_DK), (RET_DK, RET_DK, wq + h * RET_DK),
                  (2 * RET_DK, RET_DV, 2 * wq + h * RET_DV), (2 * RET_DK + RET_DV, RET_DV, 2 * wq + wv + h * RET_DV)]
        _stage_out(dproj_ref, stage, sems, h * NS + i, H * NS, (NS - 1 - i) * C, pieces, [dq, dk, dv, dg])

    return pl.pallas_call(
        body, name=name, grid=(H, NS),
        in_specs=[q_spec, k_spec, v_spec, g_spec, cs_spec, cs_spec, m_spec, vec_spec, vec_spec, gn_spec, st_spec, dy_spec],
        out_specs=[_ANY, gn_spec],
        out_shape=[jax.ShapeDtypeStruct((T, 2 * wq + 2 * wv), BF16), jax.ShapeDtypeStruct((1, H * RET_DV), F32)],
        scratch_shapes=[pltpu.VMEM((RET_DK, RET_DV), F32), pltpu.VMEM((2, C, 2 * RET_DK + 2 * RET_DV), BF16),
                        pltpu.SemaphoreType.DMA((2, 4))],
        compiler_params=_cparams("arbitrary", "arbitrary"),
    )(proj, proj, proj, proj, cos, sin, mask, xi, zeta, gn_w, states, dy)


def _shift_down(x, prev8, k):
    if k == 0:
        return x
    y = pltpu.roll(x, k, 0)
    row = lax.broadcasted_iota(jnp.int32, prev8.shape, 0)
    top = jnp.where(row < k, pltpu.roll(prev8, k, 0), y[:8])
    return jnp.concatenate([top, y[8:]], axis=0)


def _shift_up(x, next8, k):
    if k == 0:
        return x
    n = x.shape[0]
    y = pltpu.roll(x, n - k, 0)
    row = lax.broadcasted_iota(jnp.int32, next8.shape, 0)
    bot = jnp.where(row >= 8 - k, pltpu.roll(next8, 8 - k, 0), y[n - 8:])
    return jnp.concatenate([y[:n - 8], bot], axis=0)


def _conv_silu(raw, halo, w, b):
    cv = b
    for tap in range(SSD_CONV_W):
        cv = cv + _shift_down(raw, halo, SSD_CONV_W - 1 - tap) * w[tap:tap + 1, :]
    sg = _sigmoid(cv)
    return cv * sg, cv, sg


def _conv_silu_bwd(d_post, cv, sg, raw, halo, w, carry8):
    dcv = d_post * (sg * (1.0 + cv * (1.0 - sg)))
    d_raw = jnp.zeros_like(raw)
    dws = []
    for tap in range(SSD_CONV_W):
        k = SSD_CONV_W - 1 - tap
        d_raw = d_raw + _shift_up(dcv, carry8, k) * w[tap:tap + 1, :]
        dws.append(_colsum(dcv * _shift_down(raw, halo, k)))
    return d_raw, jnp.concatenate(dws, axis=0), _colsum(dcv), dcv[:8]


def _softplus(x):
    return jnp.maximum(x, 0.0) + jnp.log1p(jnp.exp(-jnp.abs(x)))


def _ssd_common(C, R, dt, dtT, bias, biasT, alog, alogT, E):
    p = dt + bias
    dtv = _softplus(p)
    a = -jnp.exp(alog)
    da = dtv * a
    daT = _softplus(dtT + biasT) * (-jnp.exp(alogT))
    row = lax.broadcasted_iota(jnp.int32, (C, C), 0)
    col = lax.broadcasted_iota(jnp.int32, (C, C), 1)
    tril = row >= col
    trilf = jnp.where(tril, 1.0, 0.0).astype(F32)
    triuf = jnp.where(col >= row, 1.0, 0.0).astype(F32)
    acum = _dot_sel(trilf, da, split="b")
    acumT = _dot_sel(daT, trilf, NT, split="a")
    al = acum[C - 1:C, :]
    ea = jnp.exp(acum)
    dte = jnp.exp(al - acum)
    eal = jnp.exp(al)
    return dict(p=p, dtv=dtv, a=a, da=da, tril=tril, triuf=triuf, acum=acum, acumT=acumT, al=al, ea=ea, dte=dte, eal=eal,
                dtv_e=_dot_sel(dtv, E, split="a", terms=2), ea_e=_dot_sel(ea, E, split="a", terms=2),
                dte_e=_dot_sel(dte, E, split="a", terms=2), eal_e=_dot_sel(eal, E, split="a"))


def _head_decay(q, r, C, R):
    seg = jnp.broadcast_to(q["acum"][:, r:r + 1], (C, C)) - q["acumT"][r:r + 1, :]
    return jnp.exp(jnp.where(q["tril"], seg, -1e30))


def _ssd_group_specs(C, R, NS, rev):
    RP = R * SSD_P
    G = SSD_G
    hb = C // 8

    def ci(i):
        return NS - 1 - i if rev else i

    def halo_row(i):
        return jnp.maximum(ci(i) * hb - 1, 0)

    off_b = G * RP // SSD_N
    z_spec = pl.BlockSpec((C, RP), lambda g, i: (ci(i), g))
    x_spec = pl.BlockSpec((C, RP), lambda g, i: (ci(i), G + g))
    b_spec = pl.BlockSpec((C, SSD_N), lambda g, i: (ci(i), 2 * off_b + g))
    c_spec = pl.BlockSpec((C, SSD_N), lambda g, i: (ci(i), 2 * off_b + G + g))
    xh_spec = pl.BlockSpec((8, RP), lambda g, i: (halo_row(i), G + g))
    bh_spec = pl.BlockSpec((8, SSD_N), lambda g, i: (halo_row(i), 2 * off_b + g))
    ch_spec = pl.BlockSpec((8, SSD_N), lambda g, i: (halo_row(i), 2 * off_b + G + g))
    dt_spec = pl.BlockSpec((None, C, R), lambda g, i: (g, ci(i), 0))
    dtT_spec = pl.BlockSpec((None, R, C), lambda g, i: (g, 0, ci(i)))
    pr_spec = pl.BlockSpec((None, 1, R), lambda g, i: (g, 0, 0))
    prT_spec = pl.BlockSpec((None, R, 1), lambda g, i: (g, 0, 0))
    cwx_spec = pl.BlockSpec((SSD_CONV_W, RP), lambda g, i: (0, g))
    cwn_spec = pl.BlockSpec((SSD_CONV_W, SSD_N), lambda g, i: (0, g))
    cbx_spec = pl.BlockSpec((1, RP), lambda g, i: (0, g))
    cbn_spec = pl.BlockSpec((1, SSD_N), lambda g, i: (0, g))
    e_spec = pl.BlockSpec((R, RP), lambda g, i: (0, 0))
    st_spec = pl.BlockSpec((None, None, SSD_N, RP), lambda g, i: (g, ci(i), 0, 0))
    return dict(z=z_spec, x=x_spec, b=b_spec, c=c_spec, xh=xh_spec, bh=bh_spec, ch=ch_spec, dt=dt_spec, dtT=dtT_spec,
                pr=pr_spec, prT=prT_spec, cwx=cwx_spec, cwn=cwn_spec, cbx=cbx_spec, cbn=cbn_spec, e=e_spec, st=st_spec)


def _ssd_forward_vals(C, R, refs, first, s_in):
    E = refs["E"]
    halo_on = jnp.where(first, 0.0, 1.0)
    xh, bh, ch = refs["xh"] * halo_on, refs["bh"] * halo_on, refs["ch"] * halo_on
    xs, cvx, sgx = _conv_silu(refs["x"], xh, refs["cwx"], refs["cbx"])
    bm, cvb, sgb = _conv_silu(refs["b"], bh, refs["cwb"], refs["cbb"])
    cm, cvc, sgc = _conv_silu(refs["c"], ch, refs["cwc"], refs["cbc"])
    q = _ssd_common(C, R, refs["dt"], refs["dtT"], refs["bias"], refs["biasT"], refs["alog"], refs["alogT"], E)
    xdt = xs * q["dtv_e"]
    cb = _dot(cm, bm, NT)
    yoff_raw = _dot(cm, s_in)
    xdt_b = xdt.astype(BF16)
    low = lax.broadcasted_iota(jnp.int32, (1, 2 * SSD_P), 1) < SSD_P
    pairs = []
    for j in range(R // 2):
        xp = xdt_b[:, 2 * SSD_P * j:2 * SSD_P * (j + 1)]
        y0 = _dot(cb * _head_decay(q, 2 * j, C, R), xp)
        y1 = _dot(cb * _head_decay(q, 2 * j + 1, C, R), xp)
        pairs.append(jnp.where(low, y0, y1))
    ydiag = jnp.concatenate(pairs, axis=1)
    d_e =_dot_sel(refs["dskip"], E, split="a")
    y = ydiag + yoff_raw * q["ea_e"] + d_e * xs
    xd = xdt * q["dte_e"]
    s_out = s_in * q["eal_e"] + _dot(bm, xd, TN)
    z = refs["z"]
    sgz = _sigmoid(z)
    yz = y * (z * sgz)
    rn = lax.rsqrt(jnp.mean(yz * yz, axis=1, keepdims=True) + RMS_EPS)
    return dict(q=q, xh=xh, bh=bh, ch=ch, xs=xs, cvx=cvx, sgx=sgx, bm=bm, cvb=cvb, sgb=sgb, cm=cm, cvc=cvc, sgc=sgc,
                xdt=xdt, cb=cb, yoff_raw=yoff_raw, d_e=d_e, y=y, xd=xd, s_out=s_out, z=z, sgz=sgz, yz=yz, rn=rn)


_SSD_IN_NAMES = ("z", "x", "b", "c", "xh", "bh", "ch", "dt", "dtT", "bias", "biasT", "alog", "alogT", "dskip",
                 "cwx", "cwb", "cwc", "cbx", "cbb", "cbc", "nw", "E")


def _ssd_inputs(pm, dt_g, dtT_g, prm, sp):
    bias, biasT, alog, alogT, dskip, cwx, cwb, cwc, cbx, cbb, cbc, nw, E = prm
    args = [pm, pm, pm, pm, pm, pm, pm, dt_g, dtT_g, bias, biasT, alog, alogT, dskip, cwx, cwb, cwc, cbx, cbb, cbc, nw, E]
    specs = [sp["z"], sp["x"], sp["b"], sp["c"], sp["xh"], sp["bh"], sp["ch"], sp["dt"], sp["dtT"], sp["pr"], sp["prT"],
             sp["pr"], sp["prT"], sp["pr"], sp["cwx"], sp["cwn"], sp["cwn"], sp["cbx"], sp["cbn"], sp["cbn"], sp["cbx"], sp["e"]]
    return args, specs


def _ssd_fwd(pm, dt_g, dtT_g, prm, *, C, R, name):
    T = pm.shape[0]
    NS = T // C
    RP = R * SSD_P
    G = SSD_G
    sp = _ssd_group_specs(C, R, NS, False)
    args, specs = _ssd_inputs(pm, dt_g, dtT_g, prm, sp)
    nin = len(args)

    def body(*refs):
        ins = {n: r[...] for n, r in zip(_SSD_IN_NAMES, refs[:nin])}
        y_ref, st_ref, S = refs[nin:]
        first = pl.program_id(1) == 0

        @pl.when(first)
        def _():
            S[...] = jnp.zeros_like(S)

        s_in = S[...]
        st_ref[...] = s_in
        f = _ssd_forward_vals(C, R, ins, first, s_in)
        y_ref[...] = (f["yz"] * f["rn"] * ins["nw"]).astype(BF16)
        S[...] = f["s_out"]

    return pl.pallas_call(
        body, name=name, grid=(G, NS), in_specs=specs,
        out_specs=[pl.BlockSpec((C, RP), lambda g, i: (i, g)), sp["st"]],
        out_shape=[jax.ShapeDtypeStruct((T, G * RP), BF16), jax.ShapeDtypeStruct((G, NS, SSD_N, RP), F32)],
        scratch_shapes=[pltpu.VMEM((SSD_N, RP), F32)],
        compiler_params=_cparams("parallel", "arbitrary"),
    )(*args)


def _ssd_bwd(pm, dt_g, dtT_g, prm, states, dout, *, C, R, name):
    T = pm.shape[0]
    NS = T // C
    RP = R * SSD_P
    G = SSD_G
    sp = _ssd_group_specs(C, R, NS, True)
    args, specs = _ssd_inputs(pm, dt_g, dtT_g, prm, sp)
    nin = len(args)
    rows_spec = pl.BlockSpec((C, RP), lambda g, i: (NS - 1 - i, g))
    args = args + [states, dout]
    specs = specs + [sp["st"], rows_spec]

    def body(*refs):
        ins = {n: r[...] for n, r in zip(_SSD_IN_NAMES, refs[:nin])}
        st_ref, dout_ref = refs[nin], refs[nin + 1]
        (dpm_ref, ddt_ref, dbias_ref, dalog_ref, dd_ref, dcwx_ref, dcwb_ref, dcwc_ref,
         dcbx_ref, dcbb_ref, dcbc_ref, dnw_ref) = refs[nin + 2:nin + 14]
        dS, cx8, cb8, cc8, stage, sems = refs[nin + 14:]
        acc_refs = (dbias_ref, dalog_ref, dd_ref, dcwx_ref, dcwb_ref, dcwc_ref, dcbx_ref, dcbb_ref, dcbc_ref, dnw_ref)
        step = pl.program_id(1)

        @pl.when(step == 0)
        def _():
            for r_ in acc_refs + (dS, cx8, cb8, cc8):
                r_[...] = jnp.zeros_like(r_)

        first = step == NS - 1
        E = ins["E"]
        s_in = st_ref[...]
        f = _ssd_forward_vals(C, R, ins, first, s_in)
        q = f["q"]
        xs, bm, cm, xdt, cb, y, z, sgz, yz, rn = (f[n] for n in ("xs", "bm", "cm", "xdt", "cb", "y", "z", "sgz", "yz", "rn"))
        nw = ins["nw"]
        dout = dout_ref[...].astype(F32)
        yh = yz * rn
        dnw_ref[...] += _colsum(dout * yh)
        g1 = dout * nw
        dyz = rn * (g1 - yh * jnp.mean(g1 * yh, axis=1, keepdims=True))
        dz = (dyz * y * (sgz * (1.0 + z * (1.0 - sgz)))).astype(BF16)
        dy = dyz * (z * sgz)
        dd_ref[...] += _dot_sel(_colsum(dy * xs), E, NT, split="a")
        dxs = dy * f["d_e"]
        dyo = dy * q["ea_e"]
        dcm = _dot(dyo, s_in, NT)
        ds_acc = _dot(cm, dyo, TN)
        dacum = _dot_sel(dy * f["yoff_raw"], E, NT, split="a", terms=1) * q["ea"]
        dacumT = jnp.zeros((R, C), F32)
        dcb = jnp.zeros((C, C), F32)
        rowR = lax.broadcasted_iota(jnp.int32, (1, R), 1)
        rowRT = lax.broadcasted_iota(jnp.int32, (R, 1), 0)
        dy_b, xdt_b = dy.astype(BF16), xdt.astype(BF16)
        low = lax.broadcasted_iota(jnp.int32, (1, 2 * SSD_P), 1) < SSD_P
        dxdt_pairs = []
        for j in range(R // 2):
            lanes = slice(2 * SSD_P * j, 2 * SSD_P * (j + 1))
            dyp, xp = dy_b[:, lanes], xdt_b[:, lanes]
            halves = []
            for r, mine in ((2 * j, low), (2 * j + 1, jnp.logical_not(low))):
                lr = _head_decay(q, r, C, R)
                w_r = cb * lr
                dw = _dot(jnp.where(mine, dyp, jnp.zeros_like(dyp)), xp, NT)
                halves.append(_dot(w_r, dyp, TN))
                dcb = dcb + dw * lr
                dseg = dw * w_r
                dacum = dacum + jnp.sum(dseg, axis=1, keepdims=True) * jnp.where(rowR == r, 1.0, 0.0)
                dacumT = dacumT - _colsum(dseg) * jnp.where(rowRT == r, 1.0, 0.0)
            dxdt_pairs.append(jnp.where(low, halves[0], halves[1]))
        dxdt = jnp.concatenate(dxdt_pairs, axis=1)
        dsn = dS[...]
        ds_acc = ds_acc + dsn * q["eal_e"]
        d_eal = _dot_sel(_colsum(dsn * s_in), E, NT, split="a")
        dbm = _dot(f["xd"], dsn, NT)
        dxd = _dot(bm, dsn)
        dxdt = dxdt + dxd * q["dte_e"]
        d_dte = _dot_sel(dxd * xdt, E, NT, split="a", terms=1) * q["dte"]
        d_al = _colsum(d_dte) + d_eal * q["eal"]
        dacum = dacum - d_dte
        rowC = lax.broadcasted_iota(jnp.int32, (C, 1), 0)
        dacum = dacum + jnp.where(rowC == C - 1, 1.0, 0.0) * d_al
        dS[...] = ds_acc
        dcm = dcm + _dot(dcb, bm)
        dbm = dbm + _dot(dcb, cm, TN)
        eye = jnp.where(lax.broadcasted_iota(jnp.int32, (C, C), 0) == lax.broadcasted_iota(jnp.int32, (C, C), 1), 1.0, 0.0)
        dacum = dacum + _dot_sel(eye, dacumT, NT, split="b")
        dda = _dot_sel(q["triuf"], dacum, split="b")
        ddtv = dda * q["a"] + _dot_sel(dxdt * xs, E, NT, split="a", terms=1)
        dalog_ref[...] += _colsum(dda * q["dtv"]) * q["a"]
        dxs = dxs + dxdt * q["dtv_e"]
        dp = ddtv * _sigmoid(q["p"])
        ddt_ref[...] = dp
        dbias_ref[...] += _colsum(dp)
        d_raw, d_w, d_b, c8 = _conv_silu_bwd(dxs, f["cvx"], f["sgx"], ins["x"], f["xh"], ins["cwx"], cx8[...])
        dx = d_raw.astype(BF16)
        dcwx_ref[...] += d_w
        dcbx_ref[...] += d_b
        cx8[...] = c8
        d_raw, d_w, d_b, c8 = _conv_silu_bwd(dbm, f["cvb"], f["sgb"], ins["b"], f["bh"], ins["cwb"], cb8[...])
        db = d_raw.astype(BF16)
        dcwb_ref[...] += d_w
        dcbb_ref[...] += d_b
        cb8[...] = c8
        d_raw, d_w, d_b, c8 = _conv_silu_bwd(dcm, f["cvc"], f["sgc"], ins["c"], f["ch"], ins["cwc"], cc8[...])
        dc = d_raw.astype(BF16)
        dcwc_ref[...] += d_w
        dcbc_ref[...] += d_b
        cc8[...] = c8
        g_ = pl.program_id(0)
        pieces = [(0, RP, g_ * RP), (RP, RP, G * RP + g_ * RP), (2 * RP, SSD_N, 2 * G * RP + g_ * SSD_N),
                  (2 * RP + SSD_N, SSD_N, 2 * G * RP + G * SSD_N + g_ * SSD_N)]
        _stage_out(dpm_ref, stage, sems, g_ * NS + step, G * NS, (NS - 1 - step) * C, pieces, [dz, dx, db, dc])

    out_specs = [_ANY, pl.BlockSpec((None, C, R), lambda g, i: (g, NS - 1 - i, 0)),
                 sp["pr"], sp["pr"], sp["pr"], sp["cwx"], sp["cwn"], sp["cwn"], sp["cbx"], sp["cbn"], sp["cbn"], sp["cbx"]]
    out_shape = [jax.ShapeDtypeStruct((T, 2 * G * RP + 2 * G * SSD_N), BF16),
                 jax.ShapeDtypeStruct((G, T, R), F32),
                 jax.ShapeDtypeStruct((G, 1, R), F32), jax.ShapeDtypeStruct((G, 1, R), F32), jax.ShapeDtypeStruct((G, 1, R), F32),
                 jax.ShapeDtypeStruct((SSD_CONV_W, G * RP), F32), jax.ShapeDtypeStruct((SSD_CONV_W, G * SSD_N), F32),
                 jax.ShapeDtypeStruct((SSD_CONV_W, G * SSD_N), F32),
                 jax.ShapeDtypeStruct((1, G * RP), F32), jax.ShapeDtypeStruct((1, G * SSD_N), F32),
                 jax.ShapeDtypeStruct((1, G * SSD_N), F32), jax.ShapeDtypeStruct((1, G * RP), F32)]
    return pl.pallas_call(
        body, name=name, grid=(G, NS), in_specs=specs, out_specs=out_specs, out_shape=out_shape,
        scratch_shapes=[pltpu.VMEM((SSD_N, RP), F32), pltpu.VMEM((8, RP), F32), pltpu.VMEM((8, SSD_N), F32),
                        pltpu.VMEM((8, SSD_N), F32), pltpu.VMEM((2, C, 2 * RP + 2 * SSD_N), BF16),
                        pltpu.SemaphoreType.DMA((2, 4))],
        compiler_params=_cparams("arbitrary", "arbitrary"),
    )(*args)


_ANY = pl.BlockSpec(memory_space=pl.ANY)


def _chip_peer(k):
    x, y, c = lax.axis_index("x"), lax.axis_index("y"), lax.axis_index("c")
    return (x ^ (k >> 1), y ^ (k & 1), c)


def _my_chip():
    return 2 * lax.axis_index("x") + lax.axis_index("y")


def _all_gather_chips(shards, halved, *, name):
    n = len(shards)

    def body(*refs):
        ins, outs = refs[:n], refs[n:2 * n]
        send, recv, fsend, frecv, loc = refs[2 * n:]
        s = _my_chip()
        c = lax.axis_index("c")
        sibling = (lax.axis_index("x"), lax.axis_index("y"), 1 - c)
        copies = []
        for a in range(n):
            cp = pltpu.make_async_copy(ins[a], outs[a].at[s], loc.at[a])
            cp.start()
            copies.append(cp)

        def rows(a, core):
            if not halved[a]:
                return slice(None)
            half = shards[a].shape[0] // 2
            return pl.ds(pl.multiple_of(core * half, 16), half)

        def over_ici(a, k, slot, core):
            return pltpu.make_async_remote_copy(
                src_ref=ins[a].at[rows(a, core)], dst_ref=outs[a].at[slot, rows(a, core)],
                send_sem=send.at[3 * a + k - 1], recv_sem=recv.at[3 * a + k - 1],
                device_id=_chip_peer(k), device_id_type=MESH_ID)

        def over_d2d(a, k, core):
            z = outs[a].at[s ^ k, rows(a, core)]
            return pltpu.make_async_remote_copy(
                src_ref=z, dst_ref=z, send_sem=fsend.at[3 * a + k - 1], recv_sem=frecv.at[3 * a + k - 1],
                device_id=sibling, device_id_type=MESH_ID)

        sent = []
        for a in range(n):
            for k in (1, 2, 3):
                cp = over_ici(a, k, s, c)
                cp.start()
                sent.append(cp)
        passed = []
        for a in range(n):
            for k in (1, 2, 3):
                over_ici(a, k, s ^ k, c).wait_recv()
                if halved[a]:
                    cp = over_d2d(a, k, c)
                    cp.start()
                    passed.append(cp)
        for a in range(n):
            if halved[a]:
                for k in (1, 2, 3):
                    over_d2d(a, k, 1 - c).wait_recv()
        for cp in sent + passed:
            cp.wait_send()
        for cp in copies:
            cp.wait()

    for a, h in zip(shards, halved):
        assert not h or a.shape[0] % 32 == 0, a.shape
    return pl.pallas_call(
        body, name=name, in_specs=[_ANY] * n, out_specs=[_ANY] * n,
        out_shape=[jax.ShapeDtypeStruct((4,) + a.shape, a.dtype) for a in shards],
        scratch_shapes=[pltpu.SemaphoreType.DMA((3 * n,))] * 4 + [pltpu.SemaphoreType.DMA((n,))],
        compiler_params=pltpu.CompilerParams(has_side_effects=True),
    )(*shards)


_HBM = pl.BlockSpec(memory_space=pltpu.HBM)
_SEM = pl.BlockSpec(memory_space=pltpu.SEMAPHORE)
_EFFECT = pltpu.SideEffectType.DATAFLOW_SIDE_EFFECTING


def _split_copies(src, land, send, recv, loc, a, scatter):
    s = _my_chip()
    mine = pltpu.make_async_copy(src.at[s] if scatter else src, land.at[s], loc.at[a])
    pairs = []
    for k in (1, 2, 3):
        sems = dict(send_sem=send.at[3 * a + k - 1], recv_sem=recv.at[3 * a + k - 1],
                    device_id=_chip_peer(k), device_id_type=MESH_ID)
        out = pltpu.make_async_remote_copy(src_ref=src.at[s ^ k] if scatter else src, dst_ref=land.at[s], **sems)
        arriving = pltpu.make_async_remote_copy(src_ref=src.at[s ^ k] if scatter else src, dst_ref=land.at[s ^ k], **sems)
        pairs.append((out, arriving))
    return mine, pairs


def _split_start(arrs, *, scatter, after, name):
    n = len(arrs)
    zones = [lax.empty(a.shape if scatter else (4,) + a.shape, a.dtype) for a in arrs]

    def body(*refs):
        srcs, lands = refs[:n], refs[n:2 * n]
        send, recv, loc = refs[2 * n + 1:2 * n + 4]
        token = refs[-1]
        for a in range(n):
            mine, pairs = _split_copies(srcs[a], lands[a], send, recv, loc, a, scatter)
            mine.start()
            for out, _ in pairs:
                out.start()
        token[...] = jnp.zeros_like(token)

    res = pl.pallas_call(
        body, name=name,
        out_shape=(pltpu.SemaphoreType.DMA((3 * n,)), pltpu.SemaphoreType.DMA((3 * n,)), pltpu.SemaphoreType.DMA((n,)),
                   *[pltpu.HBM(z.shape, z.dtype) for z in zones], jax.ShapeDtypeStruct((8, 128), F32)),
        in_specs=[_ANY] * n + [_HBM] * n + [_ANY],
        out_specs=(_SEM, _SEM, _SEM, *([_HBM] * n), pl.BlockSpec(memory_space=pltpu.VMEM)),
        input_output_aliases={n + i: 3 + i for i in range(n)},
        compiler_params=pltpu.CompilerParams(has_side_effects=_EFFECT),
    )(*arrs, *[pltpu.with_memory_space_constraint(z, pltpu.HBM) for z in zones], after)
    return res[:3], list(res[3:3 + n]), res[-1]


def _split_wait(sems, src, land, a, *, scatter, after, name):
    def body(src_ref, land_ref, send, recv, loc, after_ref, land_out):
        mine, pairs = _split_copies(src_ref, land_ref, send, recv, loc, a, scatter)
        mine.wait()
        for out, arriving in pairs:
            out.wait_send()
            arriving.wait_recv()

    return pl.pallas_call(
        body, name=name, out_shape=pltpu.HBM(land.shape, land.dtype),
        in_specs=[_ANY, _HBM, _SEM, _SEM, _SEM, _ANY], out_specs=_HBM, input_output_aliases={1: 0},
        compiler_params=pltpu.CompilerParams(has_side_effects=_EFFECT),
    )(src, land, *sems, after)


def _sibling_copies(srcs, lands, send, recv):
    sib = (lax.axis_index("x"), lax.axis_index("y"), 1 - lax.axis_index("c"))
    return [pltpu.make_async_remote_copy(src_ref=srcs[a], dst_ref=lands[a], send_sem=send.at[a], recv_sem=recv.at[a],
                                         device_id=sib, device_id_type=MESH_ID) for a in range(len(srcs))]


def _swap_start(arrs, *, after, name):
    n = len(arrs)
    zones = [lax.empty(a.shape, a.dtype) for a in arrs]

    def body(*refs):
        for cp in _sibling_copies(refs[:n], refs[n:2 * n], refs[2 * n + 1], refs[2 * n + 2]):
            cp.start()

    res = pl.pallas_call(
        body, name=name,
        out_shape=(pltpu.SemaphoreType.DMA((n,)), pltpu.SemaphoreType.DMA((n,)), *[pltpu.HBM(z.shape, z.dtype) for z in zones]),
        in_specs=[_ANY] * n + [_HBM] * n + [_ANY], out_specs=(_SEM, _SEM, *([_HBM] * n)),
        input_output_aliases={n + i: 2 + i for i in range(n)},
        compiler_params=pltpu.CompilerParams(has_side_effects=_EFFECT),
    )(*arrs, *[pltpu.with_memory_space_constraint(z, pltpu.HBM) for z in zones], after)
    return res[:2], list(res[2:])


def _swap_wait(sems, srcs, lands, *, after, name):
    n = len(srcs)

    def body(*refs):
        for cp in _sibling_copies(refs[:n], refs[n:2 * n], refs[2 * n], refs[2 * n + 1]):
            cp.wait_send()
            cp.wait_recv()

    res = pl.pallas_call(
        body, name=name, out_shape=tuple(pltpu.HBM(a.shape, a.dtype) for a in lands),
        in_specs=[_ANY] * n + [_HBM] * n + [_SEM, _SEM, _ANY], out_specs=tuple([_HBM] * n),
        input_output_aliases={n + i: i for i in range(n)},
        compiler_params=pltpu.CompilerParams(has_side_effects=_EFFECT),
    )(*srcs, *lands, *sems, after)
    return list(res)


def _all_gather_devices(v, *, name):
    r = v.shape[0]

    def body(v_ref, out_ref, send, recv):
        x, y, c = lax.axis_index("x"), lax.axis_index("y"), lax.axis_index("c")
        me = 4 * x + 2 * y + c
        out_ref[me] = v_ref[...]
        cps = []
        for k in range(1, 8):
            peer = (x ^ (k >> 2), y ^ ((k >> 1) & 1), c ^ (k & 1))
            cp = pltpu.make_async_remote_copy(src_ref=v_ref, dst_ref=out_ref.at[me], send_sem=send.at[k - 1],
                                              recv_sem=recv.at[k - 1], device_id=peer, device_id_type=MESH_ID)
            cp.start()
            cps.append(cp)
        for k, cp in enumerate(cps, start=1):
            cp.wait_send()
            pltpu.make_async_remote_copy(src_ref=v_ref, dst_ref=out_ref.at[me ^ k], send_sem=send.at[k - 1],
                                         recv_sem=recv.at[k - 1], device_id=(x, y, c), device_id_type=MESH_ID).wait_recv()

    vm = pl.BlockSpec(memory_space=pltpu.VMEM)
    return pl.pallas_call(
        body, name=name, in_specs=[vm], out_specs=vm, out_shape=jax.ShapeDtypeStruct((8, r, 128), F32),
        scratch_shapes=[pltpu.SemaphoreType.DMA((7,)), pltpu.SemaphoreType.DMA((7,))],
        compiler_params=pltpu.CompilerParams(has_side_effects=True),
    )(v)


def _row_tile(r, target):
    best = None
    for t in range(16, min(target, r) + 1, 16):
        if r % t == 0:
            best = t
    return best or r


def _sum_slots(buf, *, name, tr=384):
    S, r, c = buf.shape
    tr = _row_tile(r, tr)

    def body(b_ref, o_ref):
        acc = b_ref[0].astype(F32)
        for j in range(1, S):
            acc = acc + b_ref[j].astype(F32)
        o_ref[...] = acc

    return pl.pallas_call(
        body, name=name, grid=(r // tr,), in_specs=[pl.BlockSpec((S, tr, c), lambda i: (0, i, 0))],
        out_specs=pl.BlockSpec((tr, c), lambda i: (i, 0)), out_shape=jax.ShapeDtypeStruct((r, c), F32),
        compiler_params=_cparams("parallel"),
    )(buf)


ADAMW_BLOCK_ELEMS = 1 << 18


def _adamw(w, gs, m, v, *, name, tr=256, layer=None, stack=None):
    r, c = w.shape[-2:]
    tr = _row_tile(r, min(tr, max(16, ADAMW_BLOCK_ELEMS // c)))
    bc1 = 1.0 - ADAM_B1 ** ADAM_STEP
    bc2 = 1.0 - ADAM_B2 ** ADAM_STEP
    ng = len(gs)

    def body(*refs):
        w_ref, m_ref, v_ref = refs[0], refs[1 + ng], refs[2 + ng]
        g_ref, d_ref, mo_ref, vo_ref = refs[-4:]
        gg = refs[1][...] if ng == 1 else refs[1][...] + refs[2][...]
        mn = ADAM_B1 * m_ref[...] + (1.0 - ADAM_B1) * gg
        vn = ADAM_B2 * v_ref[...] + (1.0 - ADAM_B2) * (gg * gg)
        g_ref[...] = gg
        mo_ref[...] = mn
        vo_ref[...] = vn
        d_ref[...] = -ADAM_LR * ((mn / bc1) / (jnp.sqrt(vn / bc2) + ADAM_EPS) + ADAM_WD * w_ref[...])

    spec = pl.BlockSpec((tr, c), lambda i: (i, 0))
    if layer is None:
        wspec, shape = spec, (r, c)
    else:
        wspec, shape = pl.BlockSpec((None, tr, c), lambda i: (layer, i, 0)), (2, r, c)
    in_specs = [wspec] + [spec] * ng + [wspec, wspec]
    args, alias = [w, *gs, m, v], {}
    if stack is not None:
        in_specs += [_ANY] * 4
        alias = {len(args) + n: n for n in range(4)}
        args += list(stack)
    return pl.pallas_call(body, name=name, grid=(r // tr,), in_specs=in_specs, out_specs=[wspec] * 4,
                          out_shape=[jax.ShapeDtypeStruct(shape, F32)] * 4, input_output_aliases=alias,
                          compiler_params=_cparams("parallel"))(*args)


def _pack(vecs, rows):
    flat = jnp.concatenate([v.reshape(-1).astype(F32) for v in vecs])
    return jnp.pad(flat, (0, rows * 128 - flat.shape[0])).reshape(rows, 128)


def _unpack(packed, shapes):
    flat = packed.reshape(-1)
    out, off = [], 0
    for s in shapes:
        n = math.prod(s)
        out.append(flat[off:off + n].reshape(s))
        off += n
    return out


def _pack_rows(shapes):
    n = sum(math.prod(s) for s in shapes)
    return -(-n // 1024) * 8


def kernel(x, norm_mix_pre, norm_mix_post, norm_ffn_pre, norm_ffn_post, ret_w_in, ret_gn_w, ret_w_out, ssd_w_in, ssd_conv_w, ssd_conv_b, ssd_dt_bias, ssd_a_log, ssd_d, ssd_norm_w, ssd_w_out, mlp_w_up, mlp_w_down, loss_target, m_norm_mix_pre, m_norm_mix_post, m_norm_ffn_pre, m_norm_ffn_post, m_ret_w_in, m_ret_gn_w, m_ret_w_out, m_ssd_w_in, m_ssd_conv_w, m_ssd_conv_b, m_ssd_dt_bias, m_ssd_a_log, m_ssd_d, m_ssd_norm_w, m_ssd_w_out, m_mlp_w_up, m_mlp_w_down, v_norm_mix_pre, v_norm_mix_post, v_norm_ffn_pre, v_norm_ffn_post, v_ret_w_in, v_ret_gn_w, v_ret_w_out, v_ssd_w_in, v_ssd_conv_w, v_ssd_conv_b, v_ssd_dt_bias, v_ssd_a_log, v_ssd_d, v_ssd_norm_w, v_ssd_w_out, v_mlp_w_up, v_mlp_w_down):
    T, D = x.shape[1], x.shape[2]
    H = D // RET_DK
    d_inner = 2 * D
    R = d_inner // SSD_P // SSD_G
    RP = R * SSD_P
    n_heads = SSD_G * R
    conv_dim = d_inner + 2 * SSD_G * SSD_N
    n_main = d_inner + conv_dim
    C = min(256, T)
    chip = _my_chip()
    xs, tgt = x[0], loss_target[0]

    conv_sh = ssd_conv_w.shape[2]
    small_shapes = [(SSD_CONV_W, conv_sh), (conv_sh,), (ssd_norm_w.shape[1],)]
    small_rows = _pack_rows(small_shapes)
    shards = [ret_w_in[0].T.astype(BF16), ret_w_out[0].astype(BF16), ssd_w_in[0].T.astype(BF16), ssd_w_out[0].astype(BF16),
              mlp_w_up[0].T.astype(BF16), mlp_w_up[1].T.astype(BF16), mlp_w_down[0].astype(BF16), mlp_w_down[1].astype(BF16)]
    (ret_in_g, small_g) = _all_gather_chips([shards[0], _pack([ssd_conv_w[0], ssd_conv_b[0], ssd_norm_w[0]], small_rows)],
                                            [True, False], name="gather_first")

    def full(g):
        return g.reshape(4 * g.shape[1], g.shape[2])

    def start_gather(idx, after, name):
        sems, zones, tok = _split_start([shards[i] for i in idx], scatter=False, after=after, name=name)
        return {i: (sems, shards[i], zones[n], n) for n, i in enumerate(idx)}, tok

    def arrived(stage, i, after, name):
        sems, src, zone, n = stage[i]
        return full(_split_wait(sems, src, zone, n, scatter=False, after=after, name=name))

    ret_in_t = full(ret_in_g)
    sm = [_unpack(small_g[j], small_shapes) for j in range(4)]
    conv_w = jnp.concatenate([sm[j][0] for j in range(4)], axis=1)
    conv_b = jnp.concatenate([sm[j][1] for j in range(4)])[None, :]
    norm_w = jnp.concatenate([sm[j][2] for j in range(4)])[None, :]

    gb = SSD_G * SSD_N
    ssd_prm = (ssd_dt_bias.reshape(SSD_G, 1, R), ssd_dt_bias.reshape(SSD_G, R, 1),
               ssd_a_log.reshape(SSD_G, 1, R), ssd_a_log.reshape(SSD_G, R, 1), ssd_d.reshape(SSD_G, 1, R),
               conv_w[:, :d_inner], conv_w[:, d_inner:d_inner + gb], conv_w[:, d_inner + gb:],
               conv_b[:, :d_inner], conv_b[:, d_inner:d_inner + gb], conv_b[:, d_inner + gb:],
               norm_w, jnp.asarray(np.kron(np.eye(R), np.ones((1, SSD_P))), F32))
    ret_consts = _ret_consts(T, C, H)

    u0 = _rms_pre(xs, norm_mix_pre[0:1], name="pre0")
    stage1, tok = start_gather((1, 4), ret_in_g, "gather_start1")
    proj = _matmul(u0, ret_in_t, "nt", out_dtype=F32, name="ret_in", after=tok)
    stage2, tok = start_gather((6, 2), proj, "gather_start2")
    y_ret, st_ret = _ret_fwd(proj, ret_gn_w, ret_consts, C=C, name="ret_fwd")
    ret_out = arrived(stage1, 1, y_ret, "gather_wait_ret_out")
    m0 = _matmul(y_ret, ret_out, "nn", out_dtype=BF16, name="ret_out", after=tok)
    h1, u1 = _rms_post_pre(xs, m0, norm_mix_post[0:1], norm_ffn_pre[0:1], name="post_pre1")
    up_t0 = arrived(stage1, 4, u1, "gather_wait_up0")
    a0, hh0 = _matmul(u1, up_t0, "nt", out_dtype=BF16, name="mlp_up0", epi="relu2")
    stage3, tok = start_gather((3, 5, 7), hh0, "gather_start3")
    down0 = arrived(stage2, 6, hh0, "gather_wait_down0")
    f0 = _matmul(hh0, down0, "nn", out_dtype=BF16, name="mlp_down0", after=tok)
    h2, u2 = _rms_post_pre(h1, f0, norm_ffn_post[0:1], norm_mix_pre[1:2], name="post_pre2")
    ssd_in_t = arrived(stage2, 2, u2, "gather_wait_ssd_in")
    pm, pdt = _matmul(u2, ssd_in_t, "nt", out_dtype=F32, name="ssd_in", tail=(n_heads,))
    dt_g = pdt.reshape(T, SSD_G, R).transpose(1, 0, 2)
    dtT_g = pdt.reshape(T, SSD_G, R).transpose(1, 2, 0)
    y_ssd, st_ssd = _ssd_fwd(pm, dt_g, dtT_g, ssd_prm, C=C, R=R, name="ssd_fwd")
    ssd_out = arrived(stage3, 3, y_ssd, "gather_wait_ssd_out")
    m1 = _matmul(y_ssd, ssd_out, "nn", out_dtype=BF16, name="ssd_out")
    h3, u3 = _rms_post_pre(h2, m1, norm_mix_post[1:2], norm_ffn_pre[1:2], name="post_pre3")
    up_t1 = arrived(stage3, 5, u3, "gather_wait_up1")
    a1, hh1 = _matmul(u3, up_t1, "nt", out_dtype=BF16, name="mlp_up1", epi="relu2")
    down1 = arrived(stage3, 7, hh1, "gather_wait_down1")
    f1 = _matmul(hh1, down1, "nn", out_dtype=BF16, name="mlp_down1")
    up_t, down = (up_t0, up_t1), (down0, down1)
    dh4, sq = _rms_post_loss(h3, f1, norm_ffn_post[1:2], tgt, name="post_loss")
    loss = lax.psum(sq[0, 0], MESH_AXES) * (0.5 / D)

    in_flight = []

    def send_grad(g, name):
        part = g if g.ndim == 3 else g.reshape(4, g.shape[0] // 4, g.shape[1])
        sems, zones, tok = _split_start([part], scatter=True, after=part, name=f"scatter_start_{name}")
        in_flight.append((name, sems, part, zones[0]))
        return tok

    retired = []

    def retire(name, after):
        nm, sems, src, zone = in_flight.pop(0)
        assert nm == name
        part = _sum_slots(_split_wait(sems, src, zone, 0, scatter=True, after=after, name=f"scatter_wait_{nm}"),
                          name=f"sum_chips_{nm}")
        sw_sems, zones = _swap_start([part], after=part, name=f"swap_start_{nm}")
        retired.append((nm, part, sw_sems, zones))

    def mlp_bwd(i, df, u, a, hh):
        tok = send_grad(_matmul(hh, df, "tn", out_dtype=BF16, name=f"mlp_down_wg{i}"), f"down{i}")
        da = _matmul(df, down[i], "nt", out_dtype=BF16, name=f"mlp_down_dg{i}", epi="drelu2", extra=a, after=tok)
        tok = send_grad(_matmul(u, da, "tn", out_dtype=BF16, name=f"mlp_up_wg{i}", col_parts=4), f"up{i}")
        return _matmul(da, up_t[i], "nn", out_dtype=BF16, name=f"mlp_up_dg{i}", after=tok)

    df1, d_nfpost1 = _rms_post_bwd(f1, norm_ffn_post[1:2], dh4, name="post_bwd_ffn1")
    du3 = mlp_bwd(1, df1, u3, a1, hh1)
    dh3, dm1, d_nfp1, d_nmpost1 = _rms_pre_post_bwd(h3, norm_ffn_pre[1:2], du3, dh4, m1, norm_mix_post[1:2],
                                                    name="pre_bwd_ffn1_post_bwd_mix1")
    tok = send_grad(_matmul(y_ssd, dm1, "tn", out_dtype=BF16, name="ssd_out_wg"), "ssd_out")
    dy_ssd = _matmul(dm1, ssd_out, "nt", out_dtype=F32, name="ssd_out_dg", after=tok)
    (dpm, ddt_g, d_bias, d_alog, d_dskip, dcwx, dcwb, dcwc, dcbx, dcbb, dcbc, d_normw) = _ssd_bwd(
        pm, dt_g, dtT_g, ssd_prm, st_ssd, dy_ssd, C=C, R=R, name="ssd_bwd")
    dpdt = ddt_g.transpose(1, 0, 2).reshape(T, n_heads).astype(BF16)
    tok = send_grad(jnp.concatenate([_matmul(dpm, u2, "tn", out_dtype=BF16, name="ssd_in_wg"),
                                     _matmul(dpdt, u2, "tn", out_dtype=BF16, name="ssd_in_dt_wg")], axis=0), "ssd_in")
    du2 = _matmul(dpm, ssd_in_t, "nn", out_dtype=BF16, name="ssd_in_dg", after=tok, tail=(n_heads, dpdt))
    dh2, df0, d_nmp1, d_nfpost0 = _rms_pre_post_bwd(h2, norm_mix_pre[1:2], du2, dh3, f0, norm_ffn_post[0:1],
                                                    name="pre_bwd_mix1_post_bwd_ffn0")
    du1 = mlp_bwd(0, df0, u1, a0, hh0)
    dh1, dm0, d_nfp0, d_nmpost0 = _rms_pre_post_bwd(h1, norm_ffn_pre[0:1], du1, dh2, m0, norm_mix_post[0:1],
                                                    name="pre_bwd_ffn0_post_bwd_mix0")
    tok = send_grad(_matmul(y_ret, dm0, "tn", out_dtype=BF16, name="ret_out_wg"), "ret_out")
    dy_ret = _matmul(dm0, ret_out, "nt", out_dtype=F32, name="ret_out_dg", after=tok)
    dproj, d_gn = _ret_bwd(proj, ret_gn_w, ret_consts, st_ret, dy_ret, C=C, name="ret_bwd")
    tok = send_grad(_matmul(u0, dproj, "tn", out_dtype=BF16, name="ret_in_wg", col_parts=4), "ret_in")
    du0 = _matmul(dproj, ret_in_t, "nn", out_dtype=BF16, name="ret_in_dg", after=tok)
    grad_x, d_nmp0 = _rms_pre_bwd(xs, norm_mix_pre[0:1], du0, dh1, name="pre_bwd_mix0")

    for nm in ("down1", "up1", "ssd_out", "ssd_in", "down0", "up0", "ret_out", "ret_in"):
        retire(nm, grad_x)

    def upd(w, gs, m, v, name):
        shp = w.shape
        w2, m2, v2 = (t.reshape(-1, shp[-1]) for t in (w, m, v))
        return tuple(t.reshape(shp) for t in _adamw(w2, gs, m2, v2, name=name))

    def upd_t(w, gs, m, v, name):
        return tuple(t.T[None] for t in _adamw(w[0].T, gs, m[0].T, v[0].T, name=name))

    def upd_layer(layer):
        def fn(w, gs, m, v, name):
            return tuple(_adamw(w, gs, m, v, name=name, layer=layer, stack=res.get(out_of[name[len("adamw_"):]])))
        return fn

    out_of = {"ret_in": "ret_w_in", "ret_out": "ret_w_out", "ssd_in": "ssd_w_in", "ssd_out": "ssd_w_out",
              "up0": "mlp_w_up", "up1": "mlp_w_up", "down0": "mlp_w_down", "down1": "mlp_w_down"}
    todo = {"ret_in": (upd, ret_w_in, m_ret_w_in, v_ret_w_in), "ret_out": (upd, ret_w_out, m_ret_w_out, v_ret_w_out),
            "ssd_in": (upd_t, ssd_w_in, m_ssd_w_in, v_ssd_w_in), "ssd_out": (upd, ssd_w_out, m_ssd_w_out, v_ssd_w_out),
            "up0": (upd_layer(0), mlp_w_up, m_mlp_w_up, v_mlp_w_up), "up1": (upd_layer(1), mlp_w_up, m_mlp_w_up, v_mlp_w_up),
            "down0": (upd_layer(0), mlp_w_down, m_mlp_w_down, v_mlp_w_down),
            "down1": (upd_layer(1), mlp_w_down, m_mlp_w_down, v_mlp_w_down)}
    res = {}
    prev = grad_x
    for nm, mine, sems, zones in retired:
        other = _swap_wait(sems, [mine], zones, after=prev, name=f"swap_wait_{nm}")[0]
        fn, w, m, v = todo[nm]
        res[out_of[nm]] = fn(w, [mine, other], m, v, f"adamw_{nm}")
        prev = res[out_of[nm]][0]

    d_conv_w = jnp.concatenate([dcwx, dcwb, dcwc], axis=1)
    d_conv_b = jnp.concatenate([dcbx, dcbb, dcbc], axis=1)
    small_grads = [jnp.concatenate([d_nmp0, d_nmp1]), jnp.concatenate([d_nmpost0, d_nmpost1]),
                   jnp.concatenate([d_nfp0, d_nfp1]), jnp.concatenate([d_nfpost0, d_nfpost1]),
                   d_gn, d_bias.reshape(1, n_heads), d_alog.reshape(1, n_heads), d_dskip.reshape(1, n_heads),
                   d_conv_w, d_conv_b, d_normw]
    sg_shapes = [g.shape for g in small_grads]
    sg_rows = _pack_rows(sg_shapes)
    everyone = _all_gather_devices(_pack(small_grads, sg_rows), name="gather_small_grads")
    sg = _unpack(_sum_slots(everyone, name="sum_small_grads", tr=sg_rows), sg_shapes)
    (g_nmp, g_nmpost, g_nfp, g_nfpost, g_gn, g_bias, g_alog, g_dskip, g_cw_full, g_cb_full, g_nw_full) = sg
    g_cw = lax.dynamic_slice_in_dim(g_cw_full, chip * conv_sh, conv_sh, axis=1)[None]
    g_cb = lax.dynamic_slice_in_dim(g_cb_full, chip * conv_sh, conv_sh, axis=1)
    nw_sh = ssd_norm_w.shape[1]
    g_nw = lax.dynamic_slice_in_dim(g_nw_full, chip * nw_sh, nw_sh, axis=1)
    small = [("norm_mix_pre", norm_mix_pre, g_nmp, m_norm_mix_pre, v_norm_mix_pre),
             ("norm_mix_post", norm_mix_post, g_nmpost, m_norm_mix_post, v_norm_mix_post),
             ("norm_ffn_pre", norm_ffn_pre, g_nfp, m_norm_ffn_pre, v_norm_ffn_pre),
             ("norm_ffn_post", norm_ffn_post, g_nfpost, m_norm_ffn_post, v_norm_ffn_post),
             ("ret_gn_w", ret_gn_w, g_gn, m_ret_gn_w, v_ret_gn_w),
             ("ssd_conv_w", ssd_conv_w, g_cw, m_ssd_conv_w, v_ssd_conv_w),
             ("ssd_conv_b", ssd_conv_b, g_cb, m_ssd_conv_b, v_ssd_conv_b),
             ("ssd_dt_bias", ssd_dt_bias, g_bias, m_ssd_dt_bias, v_ssd_dt_bias),
             ("ssd_a_log", ssd_a_log, g_alog, m_ssd_a_log, v_ssd_a_log),
             ("ssd_d", ssd_d, g_dskip, m_ssd_d, v_ssd_d),
             ("ssd_norm_w", ssd_norm_w, g_nw, m_ssd_norm_w, v_ssd_norm_w)]
    sw_shapes = [w.shape for _, w, _, _, _ in small]
    sw_rows = _pack_rows(sw_shapes)
    packs = [_pack([t[j] for t in small], sw_rows) for j in (1, 2, 3, 4)]
    _, d_p, m_p, v_p = _adamw(packs[0], [packs[1]], packs[2], packs[3], name="adamw_small", tr=sw_rows)
    d_s, m_s, v_s = _unpack(d_p, sw_shapes), _unpack(m_p, sw_shapes), _unpack(v_p, sw_shapes)
    for j, (nm, w, g, _, _) in enumerate(small):
        res[nm] = (g.reshape(w.shape), d_s[j], m_s[j], v_s[j])

    order = ["norm_mix_pre", "norm_mix_post", "norm_ffn_pre", "norm_ffn_post", "ret_w_in", "ret_gn_w", "ret_w_out",
             "ssd_w_in", "ssd_conv_w", "ssd_conv_b", "ssd_dt_bias", "ssd_a_log", "ssd_d", "ssd_norm_w", "ssd_w_out",
             "mlp_w_up", "mlp_w_down"]
    return (loss, grad_x[None], *[res[n][0] for n in order], *[res[n][1] for n in order],
            *[res[n][2] for n in order], *[res[n][3] for n in order])
```

```python
import math

import numpy as np
import jax
import jax.numpy as jnp
from jax import lax
from jax.experimental import pallas as pl
from jax.experimental.pallas import tpu as pltpu

F32 = jnp.float32
BF16 = jnp.bfloat16
VMEM_LIMIT_BYTES = 56 * 1024 * 1024
MESH_AXES = ("x", "y", "c")
MESH_ID = pl.DeviceIdType.MESH

RMS_EPS = 1e-6
GN_EPS = 1e-5
RET_DK = 256
RET_DV = 512
ROPE_BASE = 10000.0
REF_CHUNK = 64
SSD_P = 64
SSD_N = 128
SSD_G = 8
SSD_CONV_W = 4
ADAM_LR, ADAM_B1, ADAM_B2, ADAM_EPS, ADAM_WD, ADAM_STEP = 0.001, 0.9, 0.999, 1e-08, 0.01, 10

NN = (((1,), (0,)), ((), ()))
NT = (((1,), (1,)), ((), ()))
TN = (((0,), (0,)), ((), ()))


def _cparams(*sem):
    return pltpu.CompilerParams(dimension_semantics=sem, vmem_limit_bytes=VMEM_LIMIT_BYTES)


def _dot(a, b, dims=NN):
    return lax.dot_general(a.astype(BF16), b.astype(BF16), dims, preferred_element_type=F32)


def _split_bf16(x, terms):
    parts, rest = [], x
    for _ in range(terms):
        p = rest.astype(BF16)
        parts.append(p)
        rest = rest - p.astype(F32)
    return parts


def _dot_sel(a, b, dims=NN, *, split, terms=3):
    if split == "a":
        sel = b.astype(BF16)
        return sum(lax.dot_general(p, sel, dims, preferred_element_type=F32) for p in _split_bf16(a, terms))
    sel = a.astype(BF16)
    return sum(lax.dot_general(sel, p, dims, preferred_element_type=F32) for p in _split_bf16(b, terms))


def _sigmoid(x):
    return 1.0 / (1.0 + jnp.exp(-x))


def _colsum(x):
    return jnp.sum(x, axis=0, keepdims=True)


MM_TILE = 1024
MM_FULL_K = 2048


def _mm_tiles(M, N, K):
    if K <= MM_FULL_K:
        return min(M, 2 * MM_TILE), min(N, MM_TILE), K
    return min(M, MM_TILE), min(N, 2 * MM_TILE), MM_TILE


def _matmul(a, b, mode, *, out_dtype, name, epi=None, extra=None, after=None, col_parts=None, tail=None):
    nt_ = tail[0] if tail else 0
    if mode == "nn":
        (M, K), (K2, N) = a.shape, (b.shape[0] - nt_, b.shape[1])
    elif mode == "nt":
        (M, K), (N, K2) = a.shape, (b.shape[0] - nt_, b.shape[1])
    else:
        (K, M), (K2, N) = a.shape, b.shape
    assert K == K2, (a.shape, b.shape, mode)
    tm, tn, tk = _mm_tiles(M, N, K)
    if col_parts:
        tm, tn = min(M, 2 * MM_TILE), min(tn, MM_TILE)
        while (N // col_parts) % tn:
            tn //= 2
    assert M % tm == 0 and N % tn == 0 and K % tk == 0, (M, N, K, tm, tn, tk)
    nk = K // tk
    if mode == "tn":
        a_spec = pl.BlockSpec((tk, tm), lambda i, j, k: (k, i))
    else:
        a_spec = pl.BlockSpec((tm, tk), lambda i, j, k: (i, k))
    if mode == "nt":
        b_spec = pl.BlockSpec((tn, tk), lambda i, j, k: (j, k))
    else:
        b_spec = pl.BlockSpec((tk, tn), lambda i, j, k: (k, j))
    dims = {"nn": NN, "nt": NT, "tn": TN}[mode]
    o_spec = pl.BlockSpec((tm, tn), lambda i, j, k: (i, j))
    out_dims = (M, N)
    if col_parts:
        per = N // col_parts // tn
        o_spec = pl.BlockSpec((None, tm, tn), lambda i, j, k: (j // per, i, j % per))
        out_dims = (col_parts, M, N // col_parts)
    has_extra = epi in ("drelu2", "add")
    n_out = 2 if epi == "relu2" else 1

    in_specs = [a_spec, b_spec] + ([o_spec] if has_extra else [])
    args = [a, b] + ([extra] if has_extra else [])
    if after is not None:
        in_specs.append(pl.BlockSpec(after.shape, lambda i, j, k: (0, 0)))
        args.append(after)
    n_plain = len(args)
    out_specs = [o_spec] * n_out
    out_shape = [jax.ShapeDtypeStruct(out_dims, out_dtype)] * n_out
    if tail and mode == "nt":
        assert nk == 1 and N % nt_ == 0
        in_specs.append(pl.BlockSpec((nt_, tk), lambda i, j, k: (N // nt_, 0)))
        args.append(b)
        out_specs.append(pl.BlockSpec((tm, nt_), lambda i, j, k: (i, 0)))
        out_shape.append(jax.ShapeDtypeStruct((M, nt_), F32))
    elif tail:
        assert mode == "nn" and K % nt_ == 0
        in_specs += [pl.BlockSpec((tm, nt_), lambda i, j, k: (i, 0)), pl.BlockSpec((nt_, tn), lambda i, j, k: (K // nt_, j))]
        args += [tail[1], b]
    n_in = len(args)

    def body(*refs):
        a_ref, b_ref = refs[0], refs[1]
        e_ref = refs[2] if has_extra else None
        outs = refs[n_in:n_in + n_out]

        def finish(r):
            if tail and mode == "nn":
                r = r + _dot(refs[n_plain][...], refs[n_plain + 1][...])
            if epi is None:
                outs[0][...] = r.astype(outs[0].dtype)
            elif epi == "relu2":
                outs[0][...] = r.astype(outs[0].dtype)
                h = jnp.maximum(r, 0.0)
                outs[1][...] = (h * h).astype(outs[1].dtype)
            elif epi == "drelu2":
                av = jnp.maximum(e_ref[...].astype(F32), 0.0)
                outs[0][...] = (r * (2.0 * av)).astype(outs[0].dtype)
            else:
                outs[0][...] = (r + e_ref[...].astype(F32)).astype(outs[0].dtype)

        if tail and mode == "nt":
            @pl.when(pl.program_id(1) == 0)
            def _():
                refs[n_in + n_out][...] = _dot(a_ref[...], refs[n_plain][...], NT)

        if nk == 1:
            finish(_dot(a_ref[...], b_ref[...], dims))
            return
        acc = refs[-1]
        k = pl.program_id(2)

        @pl.when(k == 0)
        def _():
            acc[...] = jnp.zeros_like(acc)

        acc[...] += _dot(a_ref[...], b_ref[...], dims)

        @pl.when(k == nk - 1)
        def _():
            finish(acc[...])

    res = pl.pallas_call(
        body, name=name, grid=(M // tm, N // tn, nk), in_specs=in_specs, out_specs=out_specs,
        out_shape=out_shape, scratch_shapes=[pltpu.VMEM((tm, tn), F32)] if nk > 1 else [],
        compiler_params=_cparams("parallel", "arbitrary" if tail and mode == "nt" else "parallel", "arbitrary"),
    )(*args)
    return res if len(res) > 1 else res[0]


def _rstd(x):
    return lax.rsqrt(jnp.mean(x * x, axis=-1, keepdims=True) + RMS_EPS)


def _row_call(body, ins, outs_shape, *, name, rows, tr, acc_outs=()):
    tr = min(tr, rows)
    assert rows % tr == 0
    in_specs = []
    for arr, blocked in ins:
        if blocked:
            in_specs.append(pl.BlockSpec((tr, arr.shape[1]), lambda i: (i, 0)))
        else:
            in_specs.append(pl.BlockSpec(arr.shape, lambda i: (0, 0)))
    out_specs = []
    for n, s in enumerate(outs_shape):
        if n in acc_outs:
            out_specs.append(pl.BlockSpec(s.shape, lambda i: (0, 0)))
        else:
            out_specs.append(pl.BlockSpec((tr, s.shape[1]), lambda i: (i, 0)))
    return pl.pallas_call(
        body, name=name, grid=(rows // tr,), in_specs=in_specs, out_specs=out_specs, out_shape=outs_shape,
        compiler_params=_cparams("arbitrary" if acc_outs else "parallel"),
    )(*[a for a, _ in ins])


def _rms_pre(h, w, *, name):
    T, D = h.shape

    def body(h_ref, w_ref, u_ref):
        x = h_ref[...]
        u_ref[...] = (x * _rstd(x) * w_ref[...]).astype(BF16)

    return _row_call(body, [(h, True), (w, False)], [jax.ShapeDtypeStruct((T, D), BF16)], name=name, rows=T, tr=256)[0]


def _rms_post_pre(h, m, w_post, w_pre, *, name):
    T, D = h.shape

    def body(h_ref, m_ref, wp_ref, wn_ref, hn_ref, u_ref):
        mm = m_ref[...].astype(F32)
        hn = h_ref[...] + mm * _rstd(mm) * wp_ref[...]
        hn_ref[...] = hn
        u_ref[...] = (hn * _rstd(hn) * wn_ref[...]).astype(BF16)

    return _row_call(body, [(h, True), (m, True), (w_post, False), (w_pre, False)],
                     [jax.ShapeDtypeStruct((T, D), F32), jax.ShapeDtypeStruct((T, D), BF16)], name=name, rows=T, tr=256)


def _rms_post_loss(h, m, w_post, tgt, *, name):
    T, D = h.shape

    def body(h_ref, m_ref, wp_ref, t_ref, dh_ref, loss_ref):
        @pl.when(pl.program_id(0) == 0)
        def _():
            loss_ref[...] = jnp.zeros_like(loss_ref)

        mm = m_ref[...].astype(F32)
        err = h_ref[...] + mm * _rstd(mm) * wp_ref[...] - t_ref[...]
        dh_ref[...] = err * (1.0 / D)
        loss_ref[...] += _colsum(jnp.sum(err * err, axis=1, keepdims=True))

    return _row_call(body, [(h, True), (m, True), (w_post, False), (tgt, True)],
                     [jax.ShapeDtypeStruct((T, D), F32), jax.ShapeDtypeStruct((1, 1), F32)],
                     name=name, rows=T, tr=256, acc_outs=(1,))


def _rms_bwd_vals(x, w, dy):
    r = _rstd(x)
    xh = x * r
    g = dy * w
    dx = r * (g - xh * jnp.mean(g * xh, axis=-1, keepdims=True))
    return dx, _colsum(dy * xh)


def _rms_post_bwd(m, w_post, dh, *, name):
    T, D = m.shape

    def body(m_ref, w_ref, dh_ref, dm_ref, dw_ref):
        @pl.when(pl.program_id(0) == 0)
        def _():
            dw_ref[...] = jnp.zeros_like(dw_ref)

        dx, dw = _rms_bwd_vals(m_ref[...].astype(F32), w_ref[...], dh_ref[...])
        dm_ref[...] = dx.astype(BF16)
        dw_ref[...] += dw

    return _row_call(body, [(m, True), (w_post, False), (dh, True)],
                     [jax.ShapeDtypeStruct((T, D), BF16), jax.ShapeDtypeStruct((1, D), F32)],
                     name=name, rows=T, tr=256, acc_outs=(1,))


def _rms_pre_bwd(h, w_pre, du, dh_out, *, name):
    T, D = h.shape

    def body(h_ref, w_ref, du_ref, dho_ref, dh_ref, dw_ref):
        @pl.when(pl.program_id(0) == 0)
        def _():
            dw_ref[...] = jnp.zeros_like(dw_ref)

        dx, dw = _rms_bwd_vals(h_ref[...], w_ref[...], du_ref[...].astype(F32))
        dh_ref[...] = dho_ref[...] + dx
        dw_ref[...] += dw

    return _row_call(body, [(h, True), (w_pre, False), (du, True), (dh_out, True)],
                     [jax.ShapeDtypeStruct((T, D), F32), jax.ShapeDtypeStruct((1, D), F32)],
                     name=name, rows=T, tr=256, acc_outs=(1,))


def _rms_pre_post_bwd(h, w_pre, du, dh_out, m_prev, w_post_prev, *, name):
    T, D = h.shape

    def body(h_ref, w_ref, du_ref, dho_ref, m_ref, wp_ref, dh_ref, dm_ref, dw_ref, dwp_ref):
        @pl.when(pl.program_id(0) == 0)
        def _():
            dw_ref[...] = jnp.zeros_like(dw_ref)
            dwp_ref[...] = jnp.zeros_like(dwp_ref)

        dx, dw = _rms_bwd_vals(h_ref[...], w_ref[...], du_ref[...].astype(F32))
        dh = dho_ref[...] + dx
        dh_ref[...] = dh
        dw_ref[...] += dw
        dm, dwp = _rms_bwd_vals(m_ref[...].astype(F32), wp_ref[...], dh)
        dm_ref[...] = dm.astype(BF16)
        dwp_ref[...] += dwp

    return _row_call(body, [(h, True), (w_pre, False), (du, True), (dh_out, True), (m_prev, True), (w_post_prev, False)],
                     [jax.ShapeDtypeStruct((T, D), F32), jax.ShapeDtypeStruct((T, D), BF16),
                      jax.ShapeDtypeStruct((1, D), F32), jax.ShapeDtypeStruct((1, D), F32)],
                     name=name, rows=T, tr=256, acc_outs=(2, 3))


def _ret_consts(T, C, H):
    lg = np.log1p(-np.exp2(-5.0 - np.arange(H, dtype=np.float64)))
    idx = np.arange(C, dtype=np.float64)
    dist = np.abs(idx[:, None] - idx[None, :])
    vis = (idx[None, :] // REF_CHUNK) <= (idx[:, None] // REF_CHUNK)
    mask = np.exp(dist[None] * lg[:, None, None]) * vis[None]
    xi = np.exp((idx[None, :] + 1.0) * lg[:, None])[..., None]
    zeta = np.exp((C - 1.0 - idx)[None, :] * lg[:, None])[..., None]
    half = RET_DK // 2
    inv_freq = ROPE_BASE ** (-np.arange(half, dtype=np.float32) / np.float32(half))
    ang = np.arange(T, dtype=np.float32)[:, None] * inv_freq[None, :].astype(np.float32)
    return (jnp.asarray(mask, F32), jnp.asarray(xi, F32), jnp.asarray(zeta, F32),
            jnp.asarray(np.cos(ang), F32), jnp.asarray(np.sin(ang), F32))


def _rot(t, cos, sin):
    half = RET_DK // 2
    t1, t2 = t[:, :half], t[:, half:]
    return jnp.concatenate([t1 * cos - t2 * sin, t1 * sin + t2 * cos], axis=1)


def _unrot(d, cos, sin):
    half = RET_DK // 2
    d1, d2 = d[:, :half], d[:, half:]
    return jnp.concatenate([d1 * cos + d2 * sin, d2 * cos - d1 * sin], axis=1)


def _ret_specs(C, H, rev, NS):
    def ci(i):
        return NS - 1 - i if rev else i

    nq = H
    q_spec = pl.BlockSpec((C, RET_DK), lambda h, i: (ci(i), h))
    k_spec = pl.BlockSpec((C, RET_DK), lambda h, i: (ci(i), nq + h))
    v_spec = pl.BlockSpec((C, RET_DV), lambda h, i: (ci(i), H + h))
    g_spec = pl.BlockSpec((C, RET_DV), lambda h, i: (ci(i), 2 * H + h))
    cs_spec = pl.BlockSpec((C, RET_DK // 2), lambda h, i: (ci(i), 0))
    m_spec = pl.BlockSpec((None, C, C), lambda h, i: (h, 0, 0))
    vec_spec = pl.BlockSpec((None, C, 1), lambda h, i: (h, 0, 0))
    gn_spec = pl.BlockSpec((1, RET_DV), lambda h, i: (0, h))
    st_spec = pl.BlockSpec((None, None, RET_DK, RET_DV), lambda h, i: (h, ci(i), 0, 0))
    return q_spec, k_spec, v_spec, g_spec, cs_spec, m_spec, vec_spec, gn_spec, st_spec


def _ret_fwd_vals(q, k, v, cos, sin, mask, xi, s_in):
    qr = _rot(q, cos, sin)
    kr = _rot(k, cos, sin) * (RET_DK ** -0.5)
    a = _dot(qr, kr, NT) * mask
    o = _dot(a, v) + _dot(qr, s_in) * xi
    mu = jnp.mean(o, axis=1, keepdims=True)
    oc = o - mu
    rstd = lax.rsqrt(jnp.mean(oc * oc, axis=1, keepdims=True) + GN_EPS)
    return qr, kr, a, oc * rstd, rstd


def _ret_fwd(proj, gn_w, consts, *, C, name):
    T = proj.shape[0]
    H = gn_w.shape[1] // RET_DV
    NS = T // C
    mask, xi, zeta, cos, sin = consts
    q_spec, k_spec, v_spec, g_spec, cs_spec, m_spec, vec_spec, gn_spec, st_spec = _ret_specs(C, H, False, NS)
    y_spec = pl.BlockSpec((C, RET_DV), lambda h, i: (i, h))

    def body(q_ref, k_ref, v_ref, g_ref, cos_ref, sin_ref, m_ref, xi_ref, ze_ref, gn_ref, y_ref, st_ref, S):
        @pl.when(pl.program_id(1) == 0)
        def _():
            S[...] = jnp.zeros_like(S)

        s_in = S[...]
        st_ref[...] = s_in
        v = v_ref[...]
        xi_v = xi_ref[...]
        qr, kr, a, on, rstd = _ret_fwd_vals(q_ref[...], k_ref[...], v, cos_ref[...], sin_ref[...], m_ref[...], xi_v, s_in)
        g = g_ref[...]
        y_ref[...] = (g * _sigmoid(g) * on * gn_ref[...]).astype(BF16)
        S[...] = s_in * xi_v[C - 1:C, :] + _dot(kr * ze_ref[...], v, TN)

    return pl.pallas_call(
        body, name=name, grid=(H, NS),
        in_specs=[q_spec, k_spec, v_spec, g_spec, cs_spec, cs_spec, m_spec, vec_spec, vec_spec, gn_spec],
        out_specs=[y_spec, st_spec],
        out_shape=[jax.ShapeDtypeStruct((T, H * RET_DV), BF16), jax.ShapeDtypeStruct((H, NS, RET_DK, RET_DV), F32)],
        scratch_shapes=[pltpu.VMEM((RET_DK, RET_DV), F32)],
        compiler_params=_cparams("parallel", "arbitrary"),
    )(proj, proj, proj, proj, cos, sin, mask, xi, zeta, gn_w)


def _stage_out(out_hbm, stage, sems, step, n_steps, row0, pieces, values):
    C = stage.shape[1]
    slot = step % 2

    def copies(sl):
        return [pltpu.make_async_copy(stage.at[sl, :, pl.ds(c0, w)],
                                      out_hbm.at[pl.ds(pl.multiple_of(row0, 16), C), pl.ds(pl.multiple_of(dc, 128), w)],
                                      sems.at[sl, n]) for n, (c0, w, dc) in enumerate(pieces)]

    @pl.when(step >= 2)
    def _():
        for cp in copies(slot):
            cp.wait()

    for (c0, w, _), v in zip(pieces, values):
        stage[slot, :, c0:c0 + w] = v
    for cp in copies(slot):
        cp.start()

    @pl.when(step == n_steps - 1)
    def _():
        for cp in copies(slot):
            cp.wait()
        if n_steps >= 2:
            for cp in copies(1 - slot):
                cp.wait()


def _ret_bwd(proj, gn_w, consts, states, dy, *, C, name):
    T = proj.shape[0]
    H = gn_w.shape[1] // RET_DV
    NS = T // C
    mask, xi, zeta, cos, sin = consts
    q_spec, k_spec, v_spec, g_spec, cs_spec, m_spec, vec_spec, gn_spec, st_spec = _ret_specs(C, H, True, NS)
    dy_spec = pl.BlockSpec((C, RET_DV), lambda h, i: (NS - 1 - i, h))
    scale = RET_DK ** -0.5
    wq, wv = H * RET_DK, H * RET_DV

    def body(q_ref, k_ref, v_ref, g_ref, cos_ref, sin_ref, m_ref, xi_ref, ze_ref, gn_ref, st_ref, dy_ref,
             dproj_ref, dgn_ref, dS, stage, sems):
        @pl.when(pl.program_id(1) == 0)
        def _():
            dS[...] = jnp.zeros_like(dS)
            dgn_ref[...] = jnp.zeros_like(dgn_ref)

        s_in = st_ref[...]
        v = v_ref[...]
        cos, sin, mask, xi_v, ze = cos_ref[...], sin_ref[...], m_ref[...], xi_ref[...], ze_ref[...]
        qr, kr, a, on, rstd = _ret_fwd_vals(q_ref[...], k_ref[...], v, cos, sin, mask, xi_v, s_in)
        g = g_ref[...]
        sg = _sigmoid(g)
        silu = g * sg
        gnw = gn_ref[...]
        dy = dy_ref[...].astype(F32)
        dg = (dy * on * gnw * (sg * (1.0 + g * (1.0 - sg)))).astype(BF16)
        t = dy * silu
        dgn_ref[...] += _colsum(t * on)
        don = t * gnw
        do = rstd * (don - jnp.mean(don, axis=1, keepdims=True) - on * jnp.mean(don * on, axis=1, keepdims=True))
        dox = do * xi_v
        ds_out = dS[...]
        da = _dot(do, v, NT) * mask
        kz = kr * ze
        dv = (_dot(a, do, TN) + _dot(kz, ds_out)).astype(BF16)
        dqr = _dot(da, kr) + _dot(dox, s_in, NT)
        dkr = _dot(da, qr, TN) + _dot(v, ds_out, NT) * ze
        dS[...] = ds_out * xi_v[C - 1:C, :] + _dot(qr, dox, TN)
        dq = _unrot(dqr, cos, sin).astype(BF16)
        dk = _unrot(dkr * scale, cos, sin).astype(BF16)
        h, i = pl.program_id(0), pl.program_id(1)
        pieces = [(0, RET_DK, h * RET_DK), (RET_DK, RET_DK, wq + h * RET_DK),
                  (2 * RET_DK, RET_DV, 2 * wq + h * RET_DV), (2 * RET_DK + RET_DV, RET_DV, 2 * wq + wv + h * RET_DV)]
        _stage_out(dproj_ref, stage, sems, h * NS + i, H * NS, (NS - 1 - i) * C, pieces, [dq, dk, dv, dg])

    return pl.pallas_call(
        body, name=name, grid=(H, NS),
        in_specs=[q_spec, k_spec, v_spec, g_spec, cs_spec, cs_spec, m_spec, vec_spec, vec_spec, gn_spec, st_spec, dy_spec],
        out_specs=[_ANY, gn_spec],
        out_shape=[jax.ShapeDtypeStruct((T, 2 * wq + 2 * wv), BF16), jax.ShapeDtypeStruct((1, H * RET_DV), F32)],
        scratch_shapes=[pltpu.VMEM((RET_DK, RET_DV), F32), pltpu.VMEM((2, C, 2 * RET_DK + 2 * RET_DV), BF16),
                        pltpu.SemaphoreType.DMA((2, 4))],
        compiler_params=_cparams("arbitrary", "arbitrary"),
    )(proj, proj, proj, proj, cos, sin, mask, xi, zeta, gn_w, states, dy)


def _shift_down(x, prev8, k):
    if k == 0:
        return x
    y = pltpu.roll(x, k, 0)
    row = lax.broadcasted_iota(jnp.int32, prev8.shape, 0)
    top = jnp.where(row < k, pltpu.roll(prev8, k, 0), y[:8])
    return jnp.concatenate([top, y[8:]], axis=0)


def _shift_up(x, next8, k):
    if k == 0:
        return x
    n = x.shape[0]
    y = pltpu.roll(x, n - k, 0)
    row = lax.broadcasted_iota(jnp.int32, next8.shape, 0)
    bot = jnp.where(row >= 8 - k, pltpu.roll(next8, 8 - k, 0), y[n - 8:])
    return jnp.concatenate([y[:n - 8], bot], axis=0)


def _conv_silu(raw, halo, w, b):
    cv = b
    for tap in range(SSD_CONV_W):
        cv = cv + _shift_down(raw, halo, SSD_CONV_W - 1 - tap) * w[tap:tap + 1, :]
    sg = _sigmoid(cv)
    return cv * sg, cv, sg


def _conv_silu_bwd(d_post, cv, sg, raw, halo, w, carry8):
    dcv = d_post * (sg * (1.0 + cv * (1.0 - sg)))
    d_raw = jnp.zeros_like(raw)
    dws = []
    for tap in range(SSD_CONV_W):
        k = SSD_CONV_W - 1 - tap
        d_raw = d_raw + _shift_up(dcv, carry8, k) * w[tap:tap + 1, :]
        dws.append(_colsum(dcv * _shift_down(raw, halo, k)))
    return d_raw, jnp.concatenate(dws, axis=0), _colsum(dcv), dcv[:8]


def _softplus(x):
    return jnp.maximum(x, 0.0) + jnp.log1p(jnp.exp(-jnp.abs(x)))


def _ssd_common(C, R, dt, dtT, bias, biasT, alog, alogT, E):
    p = dt + bias
    dtv = _softplus(p)
    a = -jnp.exp(alog)
    da = dtv * a
    daT = _softplus(dtT + biasT) * (-jnp.exp(alogT))
    row = lax.broadcasted_iota(jnp.int32, (C, C), 0)
    col = lax.broadcasted_iota(jnp.int32, (C, C), 1)
    tril = row >= col
    trilf = jnp.where(tril, 1.0, 0.0).astype(F32)
    triuf = jnp.where(col >= row, 1.0, 0.0).astype(F32)
    acum = _dot_sel(trilf, da, split="b")
    acumT = _dot_sel(daT, trilf, NT, split="a")
    al = acum[C - 1:C, :]
    ea = jnp.exp(acum)
    dte = jnp.exp(al - acum)
    eal = jnp.exp(al)
    return dict(p=p, dtv=dtv, a=a, da=da, tril=tril, triuf=triuf, acum=acum, acumT=acumT, al=al, ea=ea, dte=dte, eal=eal,
                dtv_e=_dot_sel(dtv, E, split="a", terms=2), ea_e=_dot_sel(ea, E, split="a", terms=2),
                dte_e=_dot_sel(dte, E, split="a", terms=2), eal_e=_dot_sel(eal, E, split="a"))


def _head_decay(q, r, C, R):
    seg = jnp.broadcast_to(q["acum"][:, r:r + 1], (C, C)) - q["acumT"][r:r + 1, :]
    return jnp.exp(jnp.where(q["tril"], seg, -1e30))


def _ssd_group_specs(C, R, NS, rev):
    RP = R * SSD_P
    G = SSD_G
    hb = C // 8

    def ci(i):
        return NS - 1 - i if rev else i

    def halo_row(i):
        return jnp.maximum(ci(i) * hb - 1, 0)

    off_b = G * RP // SSD_N
    z_spec = pl.BlockSpec((C, RP), lambda g, i: (ci(i), g))
    x_spec = pl.BlockSpec((C, RP), lambda g, i: (ci(i), G + g))
    b_spec = pl.BlockSpec((C, SSD_N), lambda g, i: (ci(i), 2 * off_b + g))
    c_spec = pl.BlockSpec((C, SSD_N), lambda g, i: (ci(i), 2 * off_b + G + g))
    xh_spec = pl.BlockSpec((8, RP), lambda g, i: (halo_row(i), G + g))
    bh_spec = pl.BlockSpec((8, SSD_N), lambda g, i: (halo_row(i), 2 * off_b + g))
    ch_spec = pl.BlockSpec((8, SSD_N), lambda g, i: (halo_row(i), 2 * off_b + G + g))
    dt_spec = pl.BlockSpec((None, C, R), lambda g, i: (g, ci(i), 0))
    dtT_spec = pl.BlockSpec((None, R, C), lambda g, i: (g, 0, ci(i)))
    pr_spec = pl.BlockSpec((None, 1, R), lambda g, i: (g, 0, 0))
    prT_spec = pl.BlockSpec((None, R, 1), lambda g, i: (g, 0, 0))
    cwx_spec = pl.BlockSpec((SSD_CONV_W, RP), lambda g, i: (0, g))
    cwn_spec = pl.BlockSpec((SSD_CONV_W, SSD_N), lambda g, i: (0, g))
    cbx_spec = pl.BlockSpec((1, RP), lambda g, i: (0, g))
    cbn_spec = pl.BlockSpec((1, SSD_N), lambda g, i: (0, g))
    e_spec = pl.BlockSpec((R, RP), lambda g, i: (0, 0))
    st_spec = pl.BlockSpec((None, None, SSD_N, RP), lambda g, i: (g, ci(i), 0, 0))
    return dict(z=z_spec, x=x_spec, b=b_spec, c=c_spec, xh=xh_spec, bh=bh_spec, ch=ch_spec, dt=dt_spec, dtT=dtT_spec,
                pr=pr_spec, prT=prT_spec, cwx=cwx_spec, cwn=cwn_spec, cbx=cbx_spec, cbn=cbn_spec, e=e_spec, st=st_spec)


def _ssd_forward_vals(C, R, refs, first, s_in):
    E = refs["E"]
    halo_on = jnp.where(first, 0.0, 1.0)
    xh, bh, ch = refs["xh"] * halo_on, refs["bh"] * halo_on, refs["ch"] * halo_on
    xs, cvx, sgx = _conv_silu(refs["x"], xh, refs["cwx"], refs["cbx"])
    bm, cvb, sgb = _conv_silu(refs["b"], bh, refs["cwb"], refs["cbb"])
    cm, cvc, sgc = _conv_silu(refs["c"], ch, refs["cwc"], refs["cbc"])
    q = _ssd_common(C, R, refs["dt"], refs["dtT"], refs["bias"], refs["biasT"], refs["alog"], refs["alogT"], E)
    xdt = xs * q["dtv_e"]
    cb = _dot(cm, bm, NT)
    yoff_raw = _dot(cm, s_in)
    xdt_b = xdt.astype(BF16)
    low = lax.broadcasted_iota(jnp.int32, (1, 2 * SSD_P), 1) < SSD_P
    pairs = []
    for j in range(R // 2):
        xp = xdt_b[:, 2 * SSD_P * j:2 * SSD_P * (j + 1)]
        y0 = _dot(cb * _head_decay(q, 2 * j, C, R), xp)
        y1 = _dot(cb * _head_decay(q, 2 * j + 1, C, R), xp)
        pairs.append(jnp.where(low, y0, y1))
    ydiag = jnp.concatenate(pairs, axis=1)
    d_e =_dot_sel(refs["dskip"], E, split="a")
    y = ydiag + yoff_raw * q["ea_e"] + d_e * xs
    xd = xdt * q["dte_e"]
    s_out = s_in * q["eal_e"] + _dot(bm, xd, TN)
    z = refs["z"]
    sgz = _sigmoid(z)
    yz = y * (z * sgz)
    rn = lax.rsqrt(jnp.mean(yz * yz, axis=1, keepdims=True) + RMS_EPS)
    return dict(q=q, xh=xh, bh=bh, ch=ch, xs=xs, cvx=cvx, sgx=sgx, bm=bm, cvb=cvb, sgb=sgb, cm=cm, cvc=cvc, sgc=sgc,
                xdt=xdt, cb=cb, yoff_raw=yoff_raw, d_e=d_e, y=y, xd=xd, s_out=s_out, z=z, sgz=sgz, yz=yz, rn=rn)


_SSD_IN_NAMES = ("z", "x", "b", "c", "xh", "bh", "ch", "dt", "dtT", "bias", "biasT", "alog", "alogT", "dskip",
                 "cwx", "cwb", "cwc", "cbx", "cbb", "cbc", "nw", "E")


def _ssd_inputs(pm, dt_g, dtT_g, prm, sp):
    bias, biasT, alog, alogT, dskip, cwx, cwb, cwc, cbx, cbb, cbc, nw, E = prm
    args = [pm, pm, pm, pm, pm, pm, pm, dt_g, dtT_g, bias, biasT, alog, alogT, dskip, cwx, cwb, cwc, cbx, cbb, cbc, nw, E]
    specs = [sp["z"], sp["x"], sp["b"], sp["c"], sp["xh"], sp["bh"], sp["ch"], sp["dt"], sp["dtT"], sp["pr"], sp["prT"],
             sp["pr"], sp["prT"], sp["pr"], sp["cwx"], sp["cwn"], sp["cwn"], sp["cbx"], sp["cbn"], sp["cbn"], sp["cbx"], sp["e"]]
    return args, specs


def _ssd_fwd(pm, dt_g, dtT_g, prm, *, C, R, name):
    T = pm.shape[0]
    NS = T // C
    RP = R * SSD_P
    G = SSD_G
    sp = _ssd_group_specs(C, R, NS, False)
    args, specs = _ssd_inputs(pm, dt_g, dtT_g, prm, sp)
    nin = len(args)

    def body(*refs):
        ins = {n: r[...] for n, r in zip(_SSD_IN_NAMES, refs[:nin])}
        y_ref, st_ref, S = refs[nin:]
        first = pl.program_id(1) == 0

        @pl.when(first)
        def _():
            S[...] = jnp.zeros_like(S)

        s_in = S[...]
        st_ref[...] = s_in
        f = _ssd_forward_vals(C, R, ins, first, s_in)
        y_ref[...] = (f["yz"] * f["rn"] * ins["nw"]).astype(BF16)
        S[...] = f["s_out"]

    return pl.pallas_call(
        body, name=name, grid=(G, NS), in_specs=specs,
        out_specs=[pl.BlockSpec((C, RP), lambda g, i: (i, g)), sp["st"]],
        out_shape=[jax.ShapeDtypeStruct((T, G * RP), BF16), jax.ShapeDtypeStruct((G, NS, SSD_N, RP), F32)],
        scratch_shapes=[pltpu.VMEM((SSD_N, RP), F32)],
        compiler_params=_cparams("parallel", "arbitrary"),
    )(*args)


def _ssd_bwd(pm, dt_g, dtT_g, prm, states, dout, *, C, R, name):
    T = pm.shape[0]
    NS = T // C
    RP = R * SSD_P
    G = SSD_G
    sp = _ssd_group_specs(C, R, NS, True)
    args, specs = _ssd_inputs(pm, dt_g, dtT_g, prm, sp)
    nin = len(args)
    rows_spec = pl.BlockSpec((C, RP), lambda g, i: (NS - 1 - i, g))
    args = args + [states, dout]
    specs = specs + [sp["st"], rows_spec]

    def body(*refs):
        ins = {n: r[...] for n, r in zip(_SSD_IN_NAMES, refs[:nin])}
        st_ref, dout_ref = refs[nin], refs[nin + 1]
        (dpm_ref, ddt_ref, dbias_ref, dalog_ref, dd_ref, dcwx_ref, dcwb_ref, dcwc_ref,
         dcbx_ref, dcbb_ref, dcbc_ref, dnw_ref) = refs[nin + 2:nin + 14]
        dS, cx8, cb8, cc8, stage, sems = refs[nin + 14:]
        acc_refs = (dbias_ref, dalog_ref, dd_ref, dcwx_ref, dcwb_ref, dcwc_ref, dcbx_ref, dcbb_ref, dcbc_ref, dnw_ref)
        step = pl.program_id(1)

        @pl.when(step == 0)
        def _():
            for r_ in acc_refs + (dS, cx8, cb8, cc8):
                r_[...] = jnp.zeros_like(r_)

        first = step == NS - 1
        E = ins["E"]
        s_in = st_ref[...]
        f = _ssd_forward_vals(C, R, ins, first, s_in)
        q = f["q"]
        xs, bm, cm, xdt, cb, y, z, sgz, yz, rn = (f[n] for n in ("xs", "bm", "cm", "xdt", "cb", "y", "z", "sgz", "yz", "rn"))
        nw = ins["nw"]
        dout = dout_ref[...].astype(F32)
        yh = yz * rn
        dnw_ref[...] += _colsum(dout * yh)
        g1 = dout * nw
        dyz = rn * (g1 - yh * jnp.mean(g1 * yh, axis=1, keepdims=True))
        dz = (dyz * y * (sgz * (1.0 + z * (1.0 - sgz)))).astype(BF16)
        dy = dyz * (z * sgz)
        dd_ref[...] += _dot_sel(_colsum(dy * xs), E, NT, split="a")
        dxs = dy * f["d_e"]
        dyo = dy * q["ea_e"]
        dcm = _dot(dyo, s_in, NT)
        ds_acc = _dot(cm, dyo, TN)
        dacum = _dot_sel(dy * f["yoff_raw"], E, NT, split="a", terms=1) * q["ea"]
        dacumT = jnp.zeros((R, C), F32)
        dcb = jnp.zeros((C, C), F32)
        rowR = lax.broadcasted_iota(jnp.int32, (1, R), 1)
        rowRT = lax.broadcasted_iota(jnp.int32, (R, 1), 0)
        dy_b, xdt_b = dy.astype(BF16), xdt.astype(BF16)
        low = lax.broadcasted_iota(jnp.int32, (1, 2 * SSD_P), 1) < SSD_P
        dxdt_pairs = []
        for j in range(R // 2):
            lanes = slice(2 * SSD_P * j, 2 * SSD_P * (j + 1))
            dyp, xp = dy_b[:, lanes], xdt_b[:, lanes]
            halves = []
            for r, mine in ((2 * j, low), (2 * j + 1, jnp.logical_not(low))):
                lr = _head_decay(q, r, C, R)
                w_r = cb * lr
                dw = _dot(jnp.where(mine, dyp, jnp.zeros_like(dyp)), xp, NT)
                halves.append(_dot(w_r, dyp, TN))
                dcb = dcb + dw * lr
                dseg = dw * w_r
                dacum = dacum + jnp.sum(dseg, axis=1, keepdims=True) * jnp.where(rowR == r, 1.0, 0.0)
                dacumT = dacumT - _colsum(dseg) * jnp.where(rowRT == r, 1.0, 0.0)
            dxdt_pairs.append(jnp.where(low, halves[0], halves[1]))
        dxdt = jnp.concatenate(dxdt_pairs, axis=1)
        dsn = dS[...]
        ds_acc = ds_acc + dsn * q["eal_e"]
        d_eal = _dot_sel(_colsum(dsn * s_in), E, NT, split="a")
        dbm = _dot(f["xd"], dsn, NT)
        dxd = _dot(bm, dsn)
        dxdt = dxdt + dxd * q["dte_e"]
        d_dte = _dot_sel(dxd * xdt, E, NT, split="a", terms=1) * q["dte"]
        d_al = _colsum(d_dte) + d_eal * q["eal"]
        dacum = dacum - d_dte
        rowC = lax.broadcasted_iota(jnp.int32, (C, 1), 0)
        dacum = dacum + jnp.where(rowC == C - 1, 1.0, 0.0) * d_al
        dS[...] = ds_acc
        dcm = dcm + _dot(dcb, bm)
        dbm = dbm + _dot(dcb, cm, TN)
        eye = jnp.where(lax.broadcasted_iota(jnp.int32, (C, C), 0) == lax.broadcasted_iota(jnp.int32, (C, C), 1), 1.0, 0.0)
        dacum = dacum + _dot_sel(eye, dacumT, NT, split="b")
        dda = _dot_sel(q["triuf"], dacum, split="b")
        ddtv = dda * q["a"] + _dot_sel(dxdt * xs, E, NT, split="a", terms=1)
        dalog_ref[...] += _colsum(dda * q["dtv"]) * q["a"]
        dxs = dxs + dxdt * q["dtv_e"]
        dp = ddtv * _sigmoid(q["p"])
        ddt_ref[...] = dp
        dbias_ref[...] += _colsum(dp)
        d_raw, d_w, d_b, c8 = _conv_silu_bwd(dxs, f["cvx"], f["sgx"], ins["x"], f["xh"], ins["cwx"], cx8[...])
        dx = d_raw.astype(BF16)
        dcwx_ref[...] += d_w
        dcbx_ref[...] += d_b
        cx8[...] = c8
        d_raw, d_w, d_b, c8 = _conv_silu_bwd(dbm, f["cvb"], f["sgb"], ins["b"], f["bh"], ins["cwb"], cb8[...])
        db = d_raw.astype(BF16)
        dcwb_ref[...] += d_w
        dcbb_ref[...] += d_b
        cb8[...] = c8
        d_raw, d_w, d_b, c8 = _conv_silu_bwd(dcm, f["cvc"], f["sgc"], ins["c"], f["ch"], ins["cwc"], cc8[...])
        dc = d_raw.astype(BF16)
        dcwc_ref[...] += d_w
        dcbc_ref[...] += d_b
        cc8[...] = c8
        g_ = pl.program_id(0)
        pieces = [(0, RP, g_ * RP), (RP, RP, G * RP + g_ * RP), (2 * RP, SSD_N, 2 * G * RP + g_ * SSD_N),
                  (2 * RP + SSD_N, SSD_N, 2 * G * RP + G * SSD_N + g_ * SSD_N)]
        _stage_out(dpm_ref, stage, sems, g_ * NS + step, G * NS, (NS - 1 - step) * C, pieces, [dz, dx, db, dc])

    out_specs = [_ANY, pl.BlockSpec((None, C, R), lambda g, i: (g, NS - 1 - i, 0)),
                 sp["pr"], sp["pr"], sp["pr"], sp["cwx"], sp["cwn"], sp["cwn"], sp["cbx"], sp["cbn"], sp["cbn"], sp["cbx"]]
    out_shape = [jax.ShapeDtypeStruct((T, 2 * G * RP + 2 * G * SSD_N), BF16),
                 jax.ShapeDtypeStruct((G, T, R), F32),
                 jax.ShapeDtypeStruct((G, 1, R), F32), jax.ShapeDtypeStruct((G, 1, R), F32), jax.ShapeDtypeStruct((G, 1, R), F32),
                 jax.ShapeDtypeStruct((SSD_CONV_W, G * RP), F32), jax.ShapeDtypeStruct((SSD_CONV_W, G * SSD_N), F32),
                 jax.ShapeDtypeStruct((SSD_CONV_W, G * SSD_N), F32),
                 jax.ShapeDtypeStruct((1, G * RP), F32), jax.ShapeDtypeStruct((1, G * SSD_N), F32),
                 jax.ShapeDtypeStruct((1, G * SSD_N), F32), jax.ShapeDtypeStruct((1, G * RP), F32)]
    return pl.pallas_call(
        body, name=name, grid=(G, NS), in_specs=specs, out_specs=out_specs, out_shape=out_shape,
        scratch_shapes=[pltpu.VMEM((SSD_N, RP), F32), pltpu.VMEM((8, RP), F32), pltpu.VMEM((8, SSD_N), F32),
                        pltpu.VMEM((8, SSD_N), F32), pltpu.VMEM((2, C, 2 * RP + 2 * SSD_N), BF16),
                        pltpu.SemaphoreType.DMA((2, 4))],
        compiler_params=_cparams("arbitrary", "arbitrary"),
    )(*args)


_ANY = pl.BlockSpec(memory_space=pl.ANY)


def _chip_peer(k):
    x, y, c = lax.axis_index("x"), lax.axis_index("y"), lax.axis_index("c")
    return (x ^ (k >> 1), y ^ (k & 1), c)


def _my_chip():
    return 2 * lax.axis_index("x") + lax.axis_index("y")


def _all_gather_chips(shards, halved, *, name):
    n = len(shards)

    def body(*refs):
        ins, outs = refs[:n], refs[n:2 * n]
        send, recv, fsend, frecv, loc = refs[2 * n:]
        s = _my_chip()
        c = lax.axis_index("c")
        sibling = (lax.axis_index("x"), lax.axis_index("y"), 1 - c)
        copies = []
        for a in range(n):
            cp = pltpu.make_async_copy(ins[a], outs[a].at[s], loc.at[a])
            cp.start()
            copies.append(cp)

        def rows(a, core):
            if not halved[a]:
                return slice(None)
            half = shards[a].shape[0] // 2
            return pl.ds(pl.multiple_of(core * half, 16), half)

        def over_ici(a, k, slot, core):
            return pltpu.make_async_remote_copy(
                src_ref=ins[a].at[rows(a, core)], dst_ref=outs[a].at[slot, rows(a, core)],
                send_sem=send.at[3 * a + k - 1], recv_sem=recv.at[3 * a + k - 1],
                device_id=_chip_peer(k), device_id_type=MESH_ID)

        def over_d2d(a, k, core):
            z = outs[a].at[s ^ k, rows(a, core)]
            return pltpu.make_async_remote_copy(
                src_ref=z, dst_ref=z, send_sem=fsend.at[3 * a + k - 1], recv_sem=frecv.at[3 * a + k - 1],
                device_id=sibling, device_id_type=MESH_ID)

        sent = []
        for a in range(n):
            for k in (1, 2, 3):
                cp = over_ici(a, k, s, c)
                cp.start()
                sent.append(cp)
        passed = []
        for a in range(n):
            for k in (1, 2, 3):
                over_ici(a, k, s ^ k, c).wait_recv()
                if halved[a]:
                    cp = over_d2d(a, k, c)
                    cp.start()
                    passed.append(cp)
        for a in range(n):
            if halved[a]:
                for k in (1, 2, 3):
                    over_d2d(a, k, 1 - c).wait_recv()
        for cp in sent + passed:
            cp.wait_send()
        for cp in copies:
            cp.wait()

    for a, h in zip(shards, halved):
        assert not h or a.shape[0] % 32 == 0, a.shape
    return pl.pallas_call(
        body, name=name, in_specs=[_ANY] * n, out_specs=[_ANY] * n,
        out_shape=[jax.ShapeDtypeStruct((4,) + a.shape, a.dtype) for a in shards],
        scratch_shapes=[pltpu.SemaphoreType.DMA((3 * n,))] * 4 + [pltpu.SemaphoreType.DMA((n,))],
        compiler_params=pltpu.CompilerParams(has_side_effects=True),
    )(*shards)


_HBM = pl.BlockSpec(memory_space=pltpu.HBM)
_SEM = pl.BlockSpec(memory_space=pltpu.SEMAPHORE)
_EFFECT = pltpu.SideEffectType.DATAFLOW_SIDE_EFFECTING


def _split_copies(src, land, send, recv, loc, a, scatter):
    s = _my_chip()
    mine = pltpu.make_async_copy(src.at[s] if scatter else src, land.at[s], loc.at[a])
    pairs = []
    for k in (1, 2, 3):
        sems = dict(send_sem=send.at[3 * a + k - 1], recv_sem=recv.at[3 * a + k - 1],
                    device_id=_chip_peer(k), device_id_type=MESH_ID)
        out = pltpu.make_async_remote_copy(src_ref=src.at[s ^ k] if scatter else src, dst_ref=land.at[s], **sems)
        arriving = pltpu.make_async_remote_copy(src_ref=src.at[s ^ k] if scatter else src, dst_ref=land.at[s ^ k], **sems)
        pairs.append((out, arriving))
    return mine, pairs


def _split_start(arrs, *, scatter, after, name):
    n = len(arrs)
    zones = [lax.empty(a.shape if scatter else (4,) + a.shape, a.dtype) for a in arrs]

    def body(*refs):
        srcs, lands = refs[:n], refs[n:2 * n]
        send, recv, loc = refs[2 * n + 1:2 * n + 4]
        token = refs[-1]
        for a in range(n):
            mine, pairs = _split_copies(srcs[a], lands[a], send, recv, loc, a, scatter)
            mine.start()
            for out, _ in pairs:
                out.start()
        token[...] = jnp.zeros_like(token)

    res = pl.pallas_call(
        body, name=name,
        out_shape=(pltpu.SemaphoreType.DMA((3 * n,)), pltpu.SemaphoreType.DMA((3 * n,)), pltpu.SemaphoreType.DMA((n,)),
                   *[pltpu.HBM(z.shape, z.dtype) for z in zones], jax.ShapeDtypeStruct((8, 128), F32)),
        in_specs=[_ANY] * n + [_HBM] * n + [_ANY],
        out_specs=(_SEM, _SEM, _SEM, *([_HBM] * n), pl.BlockSpec(memory_space=pltpu.VMEM)),
        input_output_aliases={n + i: 3 + i for i in range(n)},
        compiler_params=pltpu.CompilerParams(has_side_effects=_EFFECT),
    )(*arrs, *[pltpu.with_memory_space_constraint(z, pltpu.HBM) for z in zones], after)
    return res[:3], list(res[3:3 + n]), res[-1]


def _split_wait(sems, src, land, a, *, scatter, after, name):
    def body(src_ref, land_ref, send, recv, loc, after_ref, land_out):
        mine, pairs = _split_copies(src_ref, land_ref, send, recv, loc, a, scatter)
        mine.wait()
        for out, arriving in pairs:
            out.wait_send()
            arriving.wait_recv()

    return pl.pallas_call(
        body, name=name, out_shape=pltpu.HBM(land.shape, land.dtype),
        in_specs=[_ANY, _HBM, _SEM, _SEM, _SEM, _ANY], out_specs=_HBM, input_output_aliases={1: 0},
        compiler_params=pltpu.CompilerParams(has_side_effects=_EFFECT),
    )(src, land, *sems, after)


def _sibling_copies(srcs, lands, send, recv):
    sib = (lax.axis_index("x"), lax.axis_index("y"), 1 - lax.axis_index("c"))
    return [pltpu.make_async_remote_copy(src_ref=srcs[a], dst_ref=lands[a], send_sem=send.at[a], recv_sem=recv.at[a],
                                         device_id=sib, device_id_type=MESH_ID) for a in range(len(srcs))]


def _swap_start(arrs, *, after, name):
    n = len(arrs)
    zones = [lax.empty(a.shape, a.dtype) for a in arrs]

    def body(*refs):
        for cp in _sibling_copies(refs[:n], refs[n:2 * n], refs[2 * n + 1], refs[2 * n + 2]):
            cp.start()

    res = pl.pallas_call(
        body, name=name,
        out_shape=(pltpu.SemaphoreType.DMA((n,)), pltpu.SemaphoreType.DMA((n,)), *[pltpu.HBM(z.shape, z.dtype) for z in zones]),
        in_specs=[_ANY] * n + [_HBM] * n + [_ANY], out_specs=(_SEM, _SEM, *([_HBM] * n)),
        input_output_aliases={n + i: 2 + i for i in range(n)},
        compiler_params=pltpu.CompilerParams(has_side_effects=_EFFECT),
    )(*arrs, *[pltpu.with_memory_space_constraint(z, pltpu.HBM) for z in zones], after)
    return res[:2], list(res[2:])


def _swap_wait(sems, srcs, lands, *, after, name):
    n = len(srcs)

    def body(*refs):
        for cp in _sibling_copies(refs[:n], refs[n:2 * n], refs[2 * n], refs[2 * n + 1]):
            cp.wait_send()
            cp.wait_recv()

    res = pl.pallas_call(
        body, name=name, out_shape=tuple(pltpu.HBM(a.shape, a.dtype) for a in lands),
        in_specs=[_ANY] * n + [_HBM] * n + [_SEM, _SEM, _ANY], out_specs=tuple([_HBM] * n),
        input_output_aliases={n + i: i for i in range(n)},
        compiler_params=pltpu.CompilerParams(has_side_effects=_EFFECT),
    )(*srcs, *lands, *sems, after)
    return list(res)


def _all_gather_devices(v, *, name):
    r = v.shape[0]

    def body(v_ref, out_ref, send, recv):
        x, y, c = lax.axis_index("x"), lax.axis_index("y"), lax.axis_index("c")
        me = 4 * x + 2 * y + c
        out_ref[me] = v_ref[...]
        cps = []
        for k in range(1, 8):
            peer = (x ^ (k >> 2), y ^ ((k >> 1) & 1), c ^ (k & 1))
            cp = pltpu.make_async_remote_copy(src_ref=v_ref, dst_ref=out_ref.at[me], send_sem=send.at[k - 1],
                                              recv_sem=recv.at[k - 1], device_id=peer, device_id_type=MESH_ID)
            cp.start()
            cps.append(cp)
        for k, cp in enumerate(cps, start=1):
            cp.wait_send()
            pltpu.make_async_remote_copy(src_ref=v_ref, dst_ref=out_ref.at[me ^ k], send_sem=send.at[k - 1],
                                         recv_sem=recv.at[k - 1], device_id=(x, y, c), device_id_type=MESH_ID).wait_recv()

    vm = pl.BlockSpec(memory_space=pltpu.VMEM)
    return pl.pallas_call(
        body, name=name, in_specs=[vm], out_specs=vm, out_shape=jax.ShapeDtypeStruct((8, r, 128), F32),
        scratch_shapes=[pltpu.SemaphoreType.DMA((7,)), pltpu.SemaphoreType.DMA((7,))],
        compiler_params=pltpu.CompilerParams(has_side_effects=True),
    )(v)


def _row_tile(r, target):
    best = None
    for t in range(16, min(target, r) + 1, 16):
        if r % t == 0:
            best = t
    return best or r


def _sum_slots(buf, *, name, tr=384):
    S, r, c = buf.shape
    tr = _row_tile(r, tr)

    def body(b_ref, o_ref):
        acc = b_ref[0].astype(F32)
        for j in range(1, S):
            acc = acc + b_ref[j].astype(F32)
        o_ref[...] = acc

    return pl.pallas_call(
        body, name=name, grid=(r // tr,), in_specs=[pl.BlockSpec((S, tr, c), lambda i: (0, i, 0))],
        out_specs=pl.BlockSpec((tr, c), lambda i: (i, 0)), out_shape=jax.ShapeDtypeStruct((r, c), F32),
        compiler_params=_cparams("parallel"),
    )(buf)


ADAMW_BLOCK_ELEMS = 1 << 18


def _adamw(w, gs, m, v, *, name, tr=256, layer=None, stack=None):
    r, c = w.shape[-2:]
    tr = _row_tile(r, min(tr, max(16, ADAMW_BLOCK_ELEMS // c)))
    bc1 = 1.0 - ADAM_B1 ** ADAM_STEP
    bc2 = 1.0 - ADAM_B2 ** ADAM_STEP
    ng = len(gs)

    def body(*refs):
        w_ref, m_ref, v_ref = refs[0], refs[1 + ng], refs[2 + ng]
        g_ref, d_ref, mo_ref, vo_ref = refs[-4:]
        gg = refs[1][...] if ng == 1 else refs[1][...] + refs[2][...]
        mn = ADAM_B1 * m_ref[...] + (1.0 - ADAM_B1) * gg
        vn = ADAM_B2 * v_ref[...] + (1.0 - ADAM_B2) * (gg * gg)
        g_ref[...] = gg
        mo_ref[...] = mn
        vo_ref[...] = vn
        d_ref[...] = -ADAM_LR * ((mn / bc1) / (jnp.sqrt(vn / bc2) + ADAM_EPS) + ADAM_WD * w_ref[...])

    spec = pl.BlockSpec((tr, c), lambda i: (i, 0))
    if layer is None:
        wspec, shape = spec, (r, c)
    else:
        wspec, shape = pl.BlockSpec((None, tr, c), lambda i: (layer, i, 0)), (2, r, c)
    in_specs = [wspec] + [spec] * ng + [wspec, wspec]
    args, alias = [w, *gs, m, v], {}
    if stack is not None:
        in_specs += [_ANY] * 4
        alias = {len(args) + n: n for n in range(4)}
        args += list(stack)
    return pl.pallas_call(body, name=name, grid=(r // tr,), in_specs=in_specs, out_specs=[wspec] * 4,
                          out_shape=[jax.ShapeDtypeStruct(shape, F32)] * 4, input_output_aliases=alias,
                          compiler_params=_cparams("parallel"))(*args)


def _pack(vecs, rows):
    flat = jnp.concatenate([v.reshape(-1).astype(F32) for v in vecs])
    return jnp.pad(flat, (0, rows * 128 - flat.shape[0])).reshape(rows, 128)


def _unpack(packed, shapes):
    flat = packed.reshape(-1)
    out, off = [], 0
    for s in shapes:
        n = math.prod(s)
        out.append(flat[off:off + n].reshape(s))
        off += n
    return out


def _pack_rows(shapes):
    n = sum(math.prod(s) for s in shapes)
    return -(-n // 1024) * 8


def kernel(x, norm_mix_pre, norm_mix_post, norm_ffn_pre, norm_ffn_post, ret_w_in, ret_gn_w, ret_w_out, ssd_w_in, ssd_conv_w, ssd_conv_b, ssd_dt_bias, ssd_a_log, ssd_d, ssd_norm_w, ssd_w_out, mlp_w_up, mlp_w_down, loss_target, m_norm_mix_pre, m_norm_mix_post, m_norm_ffn_pre, m_norm_ffn_post, m_ret_w_in, m_ret_gn_w, m_ret_w_out, m_ssd_w_in, m_ssd_conv_w, m_ssd_conv_b, m_ssd_dt_bias, m_ssd_a_log, m_ssd_d, m_ssd_norm_w, m_ssd_w_out, m_mlp_w_up, m_mlp_w_down, v_norm_mix_pre, v_norm_mix_post, v_norm_ffn_pre, v_norm_ffn_post, v_ret_w_in, v_ret_gn_w, v_ret_w_out, v_ssd_w_in, v_ssd_conv_w, v_ssd_conv_b, v_ssd_dt_bias, v_ssd_a_log, v_ssd_d, v_ssd_norm_w, v_ssd_w_out, v_mlp_w_up, v_mlp_w_down):
    T, D = x.shape[1], x.shape[2]
    H = D // RET_DK
    d_inner = 2 * D
    R = d_inner // SSD_P // SSD_G
    RP = R * SSD_P
    n_heads = SSD_G * R
    conv_dim = d_inner + 2 * SSD_G * SSD_N
    n_main = d_inner + conv_dim
    C = min(256, T)
    chip = _my_chip()
    xs, tgt = x[0], loss_target[0]

    conv_sh = ssd_conv_w.shape[2]
    small_shapes = [(SSD_CONV_W, conv_sh), (conv_sh,), (ssd_norm_w.shape[1],)]
    small_rows = _pack_rows(small_shapes)
    shards = [ret_w_in[0].T.astype(BF16), ret_w_out[0].astype(BF16), ssd_w_in[0].T.astype(BF16), ssd_w_out[0].astype(BF16),
              mlp_w_up[0].T.astype(BF16), mlp_w_up[1].T.astype(BF16), mlp_w_down[0].astype(BF16), mlp_w_down[1].astype(BF16)]
    (ret_in_g, small_g) = _all_gather_chips([shards[0], _pack([ssd_conv_w[0], ssd_conv_b[0], ssd_norm_w[0]], small_rows)],
                                            [True, False], name="gather_first")

    def full(g):
        return g.reshape(4 * g.shape[1], g.shape[2])

    def start_gather(idx, after, name):
        sems, zones, tok = _split_start([shards[i] for i in idx], scatter=False, after=after, name=name)
        return {i: (sems, shards[i], zones[n], n) for n, i in enumerate(idx)}, tok

    def arrived(stage, i, after, name):
        sems, src, zone, n = stage[i]
        return full(_split_wait(sems, src, zone, n, scatter=False, after=after, name=name))

    ret_in_t = full(ret_in_g)
    sm = [_unpack(small_g[j], small_shapes) for j in range(4)]
    conv_w = jnp.concatenate([sm[j][0] for j in range(4)], axis=1)
    conv_b = jnp.concatenate([sm[j][1] for j in range(4)])[None, :]
    norm_w = jnp.concatenate([sm[j][2] for j in range(4)])[None, :]

    gb = SSD_G * SSD_N
    ssd_prm = (ssd_dt_bias.reshape(SSD_G, 1, R), ssd_dt_bias.reshape(SSD_G, R, 1),
               ssd_a_log.reshape(SSD_G, 1, R), ssd_a_log.reshape(SSD_G, R, 1), ssd_d.reshape(SSD_G, 1, R),
               conv_w[:, :d_inner], conv_w[:, d_inner:d_inner + gb], conv_w[:, d_inner + gb:],
               conv_b[:, :d_inner], conv_b[:, d_inner:d_inner + gb], conv_b[:, d_inner + gb:],
               norm_w, jnp.asarray(np.kron(np.eye(R), np.ones((1, SSD_P))), F32))
    ret_consts = _ret_consts(T, C, H)

    u0 = _rms_pre(xs, norm_mix_pre[0:1], name="pre0")
    stage1, tok = start_gather((1, 4), ret_in_g, "gather_start1")
    proj = _matmul(u0, ret_in_t, "nt", out_dtype=F32, name="ret_in", after=tok)
    stage2, tok = start_gather((6, 2), proj, "gather_start2")
    y_ret, st_ret = _ret_fwd(proj, ret_gn_w, ret_consts, C=C, name="ret_fwd")
    ret_out = arrived(stage1, 1, y_ret, "gather_wait_ret_out")
    m0 = _matmul(y_ret, ret_out, "nn", out_dtype=BF16, name="ret_out", after=tok)
    h1, u1 = _rms_post_pre(xs, m0, norm_mix_post[0:1], norm_ffn_pre[0:1], name="post_pre1")
    up_t0 = arrived(stage1, 4, u1, "gather_wait_up0")
    a0, hh0 = _matmul(u1, up_t0, "nt", out_dtype=BF16, name="mlp_up0", epi="relu2")
    stage3, tok = start_gather((3, 5, 7), hh0, "gather_start3")
    down0 = arrived(stage2, 6, hh0, "gather_wait_down0")
    f0 = _matmul(hh0, down0, "nn", out_dtype=BF16, name="mlp_down0", after=tok)
    h2, u2 = _rms_post_pre(h1, f0, norm_ffn_post[0:1], norm_mix_pre[1:2], name="post_pre2")
    ssd_in_t = arrived(stage2, 2, u2, "gather_wait_ssd_in")
    pm, pdt = _matmul(u2, ssd_in_t, "nt", out_dtype=F32, name="ssd_in", tail=(n_heads,))
    dt_g = pdt.reshape(T, SSD_G, R).transpose(1, 0, 2)
    dtT_g = pdt.reshape(T, SSD_G, R).transpose(1, 2, 0)
    y_ssd, st_ssd = _ssd_fwd(pm, dt_g, dtT_g, ssd_prm, C=C, R=R, name="ssd_fwd")
    ssd_out = arrived(stage3, 3, y_ssd, "gather_wait_ssd_out")
    m1 = _matmul(y_ssd, ssd_out, "nn", out_dtype=BF16, name="ssd_out")
    h3, u3 = _rms_post_pre(h2, m1, norm_mix_post[1:2], norm_ffn_pre[1:2], name="post_pre3")
    up_t1 = arrived(stage3, 5, u3, "gather_wait_up1")
    a1, hh1 = _matmul(u3, up_t1, "nt", out_dtype=BF16, name="mlp_up1", epi="relu2")
    down1 = arrived(stage3, 7, hh1, "gather_wait_down1")
    f1 = _matmul(hh1, down1, "nn", out_dtype=BF16, name="mlp_down1")
    up_t, down = (up_t0, up_t1), (down0, down1)
    dh4, sq = _rms_post_loss(h3, f1, norm_ffn_post[1:2], tgt, name="post_loss")
    loss = lax.psum(sq[0, 0], MESH_AXES) * (0.5 / D)

    in_flight = []

    def send_grad(g, name):
        part = g if g.ndim == 3 else g.reshape(4, g.shape[0] // 4, g.shape[1])
        sems, zones, tok = _split_start([part], scatter=True, after=part, name=f"scatter_start_{name}")
        in_flight.append((name, sems, part, zones[0]))
        return tok

    retired = []

    def retire(name, after):
        nm, sems, src, zone = in_flight.pop(0)
        assert nm == name
        part = _sum_slots(_split_wait(sems, src, zone, 0, scatter=True, after=after, name=f"scatter_wait_{nm}"),
                          name=f"sum_chips_{nm}")
        sw_sems, zones = _swap_start([part], after=part, name=f"swap_start_{nm}")
        retired.append((nm, part, sw_sems, zones))
        return zones[0]

    def mlp_bwd(i, df, u, a, hh):
        tok = send_grad(_matmul(hh, df, "tn", out_dtype=BF16, name=f"mlp_down_wg{i}"), f"down{i}")
        da = _matmul(df, down[i], "nt", out_dtype=BF16, name=f"mlp_down_dg{i}", epi="drelu2", extra=a, after=tok)
        tok = send_grad(_matmul(u, da, "tn", out_dtype=BF16, name=f"mlp_up_wg{i}", col_parts=4), f"up{i}")
        return _matmul(da, up_t[i], "nn", out_dtype=BF16, name=f"mlp_up_dg{i}", after=tok)

    df1, d_nfpost1 = _rms_post_bwd(f1, norm_ffn_post[1:2], dh4, name="post_bwd_ffn1")
    du3 = mlp_bwd(1, df1, u3, a1, hh1)
    dh3, dm1, d_nfp1, d_nmpost1 = _rms_pre_post_bwd(h3, norm_ffn_pre[1:2], du3, dh4, m1, norm_mix_post[1:2],
                                                    name="pre_bwd_ffn1_post_bwd_mix1")
    tok = send_grad(_matmul(y_ssd, dm1, "tn", out_dtype=BF16, name="ssd_out_wg"), "ssd_out")
    dy_ssd = _matmul(dm1, ssd_out, "nt", out_dtype=F32, name="ssd_out_dg", after=tok)
    (dpm, ddt_g, d_bias, d_alog, d_dskip, dcwx, dcwb, dcwc, dcbx, dcbb, dcbc, d_normw) = _ssd_bwd(
        pm, dt_g, dtT_g, ssd_prm, st_ssd, dy_ssd, C=C, R=R, name="ssd_bwd")
    dpdt = ddt_g.transpose(1, 0, 2).reshape(T, n_heads).astype(BF16)
    tok = send_grad(jnp.concatenate([_matmul(dpm, u2, "tn", out_dtype=BF16, name="ssd_in_wg"),
                                     _matmul(dpdt, u2, "tn", out_dtype=BF16, name="ssd_in_dt_wg")], axis=0), "ssd_in")
    du2 = _matmul(dpm, ssd_in_t, "nn", out_dtype=BF16, name="ssd_in_dg", after=tok, tail=(n_heads, dpdt))
    dh2, df0, d_nmp1, d_nfpost0 = _rms_pre_post_bwd(h2, norm_mix_pre[1:2], du2, dh3, f0, norm_ffn_post[0:1],
                                                    name="pre_bwd_mix1_post_bwd_ffn0")
    du1 = mlp_bwd(0, df0, u1, a0, hh0)
    dh1, dm0, d_nfp0, d_nmpost0 = _rms_pre_post_bwd(h1, norm_ffn_pre[0:1], du1, dh2, m0, norm_mix_post[0:1],
                                                    name="pre_bwd_ffn0_post_bwd_mix0")
    tok = send_grad(_matmul(y_ret, dm0, "tn", out_dtype=BF16, name="ret_out_wg"), "ret_out")
    dy_ret = _matmul(dm0, ret_out, "nt", out_dtype=F32, name="ret_out_dg", after=tok)
    dproj, d_gn = _ret_bwd(proj, ret_gn_w, ret_consts, st_ret, dy_ret, C=C, name="ret_bwd")
    tok = send_grad(_matmul(u0, dproj, "tn", out_dtype=BF16, name="ret_in_wg", col_parts=4), "ret_in")
    du0 = _matmul(dproj, ret_in_t, "nn", out_dtype=BF16, name="ret_in_dg", after=tok)
    grad_x, d_nmp0 = _rms_pre_bwd(xs, norm_mix_pre[0:1], du0, dh1, name="pre_bwd_mix0")

    prev = grad_x
    for nm in ("down1", "up1", "ssd_out", "ssd_in", "down0", "up0", "ret_out", "ret_in"):
        prev = retire(nm, prev)

    def upd(w, gs, m, v, name):
        shp = w.shape
        w2, m2, v2 = (t.reshape(-1, shp[-1]) for t in (w, m, v))
        return tuple(t.reshape(shp) for t in _adamw(w2, gs, m2, v2, name=name))

    def upd_t(w, gs, m, v, name):
        return tuple(t.T[None] for t in _adamw(w[0].T, gs, m[0].T, v[0].T, name=name))

    def upd_layer(layer):
        def fn(w, gs, m, v, name):
            return tuple(_adamw(w, gs, m, v, name=name, layer=layer, stack=res.get(out_of[name[len("adamw_"):]])))
        return fn

    out_of = {"ret_in": "ret_w_in", "ret_out": "ret_w_out", "ssd_in": "ssd_w_in", "ssd_out": "ssd_w_out",
              "up0": "mlp_w_up", "up1": "mlp_w_up", "down0": "mlp_w_down", "down1": "mlp_w_down"}
    todo = {"ret_in": (upd, ret_w_in, m_ret_w_in, v_ret_w_in), "ret_out": (upd, ret_w_out, m_ret_w_out, v_ret_w_out),
            "ssd_in": (upd_t, ssd_w_in, m_ssd_w_in, v_ssd_w_in), "ssd_out": (upd, ssd_w_out, m_ssd_w_out, v_ssd_w_out),
            "up0": (upd_layer(0), mlp_w_up, m_mlp_w_up, v_mlp_w_up), "up1": (upd_layer(1), mlp_w_up, m_mlp_w_up, v_mlp_w_up),
            "down0": (upd_layer(0), mlp_w_down, m_mlp_w_down, v_mlp_w_down),
            "down1": (upd_layer(1), mlp_w_down, m_mlp_w_down, v_mlp_w_down)}
    res = {}
    for nm, mine, sems, zones in retired:
        other = _swap_wait(sems, [mine], zones, after=prev, name=f"swap_wait_{nm}")[0]
        fn, w, m, v = todo[nm]
        res[out_of[nm]] = fn(w, [mine, other], m, v, f"adamw_{nm}")
        prev = res[out_of[nm]][0]

    d_conv_w = jnp.concatenate([dcwx, dcwb, dcwc], axis=1)
    d_conv_b = jnp.concatenate([dcbx, dcbb, dcbc], axis=1)
    small_grads = [jnp.concatenate([d_nmp0, d_nmp1]), jnp.concatenate([d_nmpost0, d_nmpost1]),
                   jnp.concatenate([d_nfp0, d_nfp1]), jnp.concatenate([d_nfpost0, d_nfpost1]),
                   d_gn, d_bias.reshape(1, n_heads), d_alog.reshape(1, n_heads), d_dskip.reshape(1, n_heads),
                   d_conv_w, d_conv_b, d_normw]
    sg_shapes = [g.shape for g in small_grads]
    sg_rows = _pack_rows(sg_shapes)
    everyone = _all_gather_devices(_pack(small_grads, sg_rows), name="gather_small_grads")
    sg = _unpack(_sum_slots(everyone, name="sum_small_grads", tr=sg_rows), sg_shapes)
    (g_nmp, g_nmpost, g_nfp, g_nfpost, g_gn, g_bias, g_alog, g_dskip, g_cw_full, g_cb_full, g_nw_full) = sg
    g_cw = lax.dynamic_slice_in_dim(g_cw_full, chip * conv_sh, conv_sh, axis=1)[None]
    g_cb = lax.dynamic_slice_in_dim(g_cb_full, chip * conv_sh, conv_sh, axis=1)
    nw_sh = ssd_norm_w.shape[1]
    g_nw = lax.dynamic_slice_in_dim(g_nw_full, chip * nw_sh, nw_sh, axis=1)
    small = [("norm_mix_pre", norm_mix_pre, g_nmp, m_norm_mix_pre, v_norm_mix_pre),
             ("norm_mix_post", norm_mix_post, g_nmpost, m_norm_mix_post, v_norm_mix_post),
             ("norm_ffn_pre", norm_ffn_pre, g_nfp, m_norm_ffn_pre, v_norm_ffn_pre),
             ("norm_ffn_post", norm_ffn_post, g_nfpost, m_norm_ffn_post, v_norm_ffn_post),
             ("ret_gn_w", ret_gn_w, g_gn, m_ret_gn_w, v_ret_gn_w),
             ("ssd_conv_w", ssd_conv_w, g_cw, m_ssd_conv_w, v_ssd_conv_w),
             ("ssd_conv_b", ssd_conv_b, g_cb, m_ssd_conv_b, v_ssd_conv_b),
             ("ssd_dt_bias", ssd_dt_bias, g_bias, m_ssd_dt_bias, v_ssd_dt_bias),
             ("ssd_a_log", ssd_a_log, g_alog, m_ssd_a_log, v_ssd_a_log),
             ("ssd_d", ssd_d, g_dskip, m_ssd_d, v_ssd_d),
             ("ssd_norm_w", ssd_norm_w, g_nw, m_ssd_norm_w, v_ssd_norm_w)]
    sw_shapes = [w.shape for _, w, _, _, _ in small]
    sw_rows = _pack_rows(sw_shapes)
    packs = [_pack([t[j] for t in small], sw_rows) for j in (1, 2, 3, 4)]
    _, d_p, m_p, v_p = _adamw(packs[0], [packs[1]], packs[2], packs[3], name="adamw_small", tr=sw_rows)
    d_s, m_s, v_s = _unpack(d_p, sw_shapes), _unpack(m_p, sw_shapes), _unpack(v_p, sw_shapes)
    for j, (nm, w, g, _, _) in enumerate(small):
        res[nm] = (g.reshape(w.shape), d_s[j], m_s[j], v_s[j])

    order = ["norm_mix_pre", "norm_mix_post", "norm_ffn_pre", "norm_ffn_post", "ret_w_in", "ret_gn_w", "ret_w_out",
             "ssd_w_in", "ssd_conv_w", "ssd_conv_b", "ssd_dt_bias", "ssd_a_log", "ssd_d", "ssd_norm_w", "ssd_w_out",
             "mlp_w_up", "mlp_w_down"]
    return (loss, grad_x[None], *[res[n][0] for n in order], *[res[n][1] for n in order],
            *[res[n][2] for n in order], *[res[n][3] for n in order])
```

```python
import math

import numpy as np
import jax
import jax.numpy as jnp
from jax import lax
from jax.experimental import pallas as pl
from jax.experimental.pallas import tpu as pltpu

F32 = jnp.float32
BF16 = jnp.bfloat16
VMEM_LIMIT_BYTES = 56 * 1024 * 1024
MESH_AXES = ("x", "y", "c")
MESH_ID = pl.DeviceIdType.MESH

RMS_EPS = 1e-6
GN_EPS = 1e-5
RET_DK = 256
RET_DV = 512
ROPE_BASE = 10000.0
REF_CHUNK = 64
SSD_P = 64
SSD_N = 128
SSD_G = 8
SSD_CONV_W = 4
ADAM_LR, ADAM_B1, ADAM_B2, ADAM_EPS, ADAM_WD, ADAM_STEP = 0.001, 0.9, 0.999, 1e-08, 0.01, 10

NN = (((1,), (0,)), ((), ()))
NT = (((1,), (1,)), ((), ()))
TN = (((0,), (0,)), ((), ()))


def _cparams(*sem):
    return pltpu.CompilerParams(dimension_semantics=sem, vmem_limit_bytes=VMEM_LIMIT_BYTES)


def _dot(a, b, dims=NN):
    return lax.dot_general(a.astype(BF16), b.astype(BF16), dims, preferred_element_type=F32)


def _split_bf16(x, terms):
    parts, rest = [], x
    for _ in range(terms):
        p = rest.astype(BF16)
        parts.append(p)
        rest = rest - p.astype(F32)
    return parts


def _dot_sel(a, b, dims=NN, *, split, terms=3):
    if split == "a":
        sel = b.astype(BF16)
        return sum(lax.dot_general(p, sel, dims, preferred_element_type=F32) for p in _split_bf16(a, terms))
    sel = a.astype(BF16)
    return sum(lax.dot_general(sel, p, dims, preferred_element_type=F32) for p in _split_bf16(b, terms))


def _sigmoid(x):
    return 1.0 / (1.0 + jnp.exp(-x))


def _colsum(x):
    return jnp.sum(x, axis=0, keepdims=True)


MM_TILE = 1024
MM_FULL_K = 2048


def _mm_tiles(M, N, K):
    if K <= MM_FULL_K:
        return min(M, 2 * MM_TILE), min(N, MM_TILE), K
    return min(M, MM_TILE), min(N, 2 * MM_TILE), 2 * MM_TILE if K % (2 * MM_TILE) == 0 else MM_TILE


def _matmul(a, b, mode, *, out_dtype, name, epi=None, extra=None, after=None, col_parts=None, tail=None):
    nt_ = tail[0] if tail else 0
    if mode == "nn":
        (M, K), (K2, N) = a.shape, (b.shape[0] - nt_, b.shape[1])
    elif mode == "nt":
        (M, K), (N, K2) = a.shape, (b.shape[0] - nt_, b.shape[1])
    else:
        (K, M), (K2, N) = a.shape, b.shape
    assert K == K2, (a.shape, b.shape, mode)
    tm, tn, tk = _mm_tiles(M, N, K)
    if col_parts:
        tm, tn, tk = min(M, 2 * MM_TILE), min(tn, MM_TILE), min(tk, MM_TILE)
        while (N // col_parts) % tn:
            tn //= 2
    assert M % tm == 0 and N % tn == 0 and K % tk == 0, (M, N, K, tm, tn, tk)
    nk = K // tk
    if mode == "tn":
        a_spec = pl.BlockSpec((tk, tm), lambda i, j, k: (k, i))
    else:
        a_spec = pl.BlockSpec((tm, tk), lambda i, j, k: (i, k))
    if mode == "nt":
        b_spec = pl.BlockSpec((tn, tk), lambda i, j, k: (j, k))
    else:
        b_spec = pl.BlockSpec((tk, tn), lambda i, j, k: (k, j))
    dims = {"nn": NN, "nt": NT, "tn": TN}[mode]
    o_spec = pl.BlockSpec((tm, tn), lambda i, j, k: (i, j))
    out_dims = (M, N)
    if col_parts:
        per = N // col_parts // tn
        o_spec = pl.BlockSpec((None, tm, tn), lambda i, j, k: (j // per, i, j % per))
        out_dims = (col_parts, M, N // col_parts)
    has_extra = epi in ("drelu2", "add")
    n_out = 2 if epi == "relu2" else 1

    in_specs = [a_spec, b_spec] + ([o_spec] if has_extra else [])
    args = [a, b] + ([extra] if has_extra else [])
    if after is not None:
        in_specs.append(pl.BlockSpec(after.shape, lambda i, j, k: (0, 0)))
        args.append(after)
    n_plain = len(args)
    out_specs = [o_spec] * n_out
    out_shape = [jax.ShapeDtypeStruct(out_dims, out_dtype)] * n_out
    if tail and mode == "nt":
        assert nk == 1 and N % nt_ == 0
        in_specs.append(pl.BlockSpec((nt_, tk), lambda i, j, k: (N // nt_, 0)))
        args.append(b)
        out_specs.append(pl.BlockSpec((tm, nt_), lambda i, j, k: (i, 0)))
        out_shape.append(jax.ShapeDtypeStruct((M, nt_), F32))
    elif tail:
        assert mode == "nn" and K % nt_ == 0
        in_specs += [pl.BlockSpec((tm, nt_), lambda i, j, k: (i, 0)), pl.BlockSpec((nt_, tn), lambda i, j, k: (K // nt_, j))]
        args += [tail[1], b]
    n_in = len(args)

    def body(*refs):
        a_ref, b_ref = refs[0], refs[1]
        e_ref = refs[2] if has_extra else None
        outs = refs[n_in:n_in + n_out]

        def finish(r):
            if tail and mode == "nn":
                r = r + _dot(refs[n_plain][...], refs[n_plain + 1][...])
            if epi is None:
                outs[0][...] = r.astype(outs[0].dtype)
            elif epi == "relu2":
                outs[0][...] = r.astype(outs[0].dtype)
                h = jnp.maximum(r, 0.0)
                outs[1][...] = (h * h).astype(outs[1].dtype)
            elif epi == "drelu2":
                av = jnp.maximum(e_ref[...].astype(F32), 0.0)
                outs[0][...] = (r * (2.0 * av)).astype(outs[0].dtype)
            else:
                outs[0][...] = (r + e_ref[...].astype(F32)).astype(outs[0].dtype)

        if tail and mode == "nt":
            @pl.when(pl.program_id(1) == 0)
            def _():
                refs[n_in + n_out][...] = _dot(a_ref[...], refs[n_plain][...], NT)

        if nk == 1:
            finish(_dot(a_ref[...], b_ref[...], dims))
            return
        acc = refs[-1]
        k = pl.program_id(2)

        @pl.when(k == 0)
        def _():
            acc[...] = jnp.zeros_like(acc)

        acc[...] += _dot(a_ref[...], b_ref[...], dims)

        @pl.when(k == nk - 1)
        def _():
            finish(acc[...])

    res = pl.pallas_call(
        body, name=name, grid=(M // tm, N // tn, nk), in_specs=in_specs, out_specs=out_specs,
        out_shape=out_shape, scratch_shapes=[pltpu.VMEM((tm, tn), F32)] if nk > 1 else [],
        compiler_params=_cparams("parallel", "arbitrary" if tail and mode == "nt" else "parallel", "arbitrary"),
    )(*args)
    return res if len(res) > 1 else res[0]


def _rstd(x):
    return lax.rsqrt(jnp.mean(x * x, axis=-1, keepdims=True) + RMS_EPS)


def _row_call(body, ins, outs_shape, *, name, rows, tr, acc_outs=()):
    tr = min(tr, rows)
    assert rows % tr == 0
    in_specs = []
    for arr, blocked in ins:
        if blocked:
            in_specs.append(pl.BlockSpec((tr, arr.shape[1]), lambda i: (i, 0)))
        else:
            in_specs.append(pl.BlockSpec(arr.shape, lambda i: (0, 0)))
    out_specs = []
    for n, s in enumerate(outs_shape):
        if n in acc_outs:
            out_specs.append(pl.BlockSpec(s.shape, lambda i: (0, 0)))
        else:
            out_specs.append(pl.BlockSpec((tr, s.shape[1]), lambda i: (i, 0)))
    return pl.pallas_call(
        body, name=name, grid=(rows // tr,), in_specs=in_specs, out_specs=out_specs, out_shape=outs_shape,
        compiler_params=_cparams("arbitrary" if acc_outs else "parallel"),
    )(*[a for a, _ in ins])


def _rms_pre(h, w, *, name):
    T, D = h.shape

    def body(h_ref, w_ref, u_ref):
        x = h_ref[...]
        u_ref[...] = (x * _rstd(x) * w_ref[...]).astype(BF16)

    return _row_call(body, [(h, True), (w, False)], [jax.ShapeDtypeStruct((T, D), BF16)], name=name, rows=T, tr=256)[0]


def _rms_post_pre(h, m, w_post, w_pre, *, name):
    T, D = h.shape

    def body(h_ref, m_ref, wp_ref, wn_ref, hn_ref, u_ref):
        mm = m_ref[...].astype(F32)
        hn = h_ref[...] + mm * _rstd(mm) * wp_ref[...]
        hn_ref[...] = hn
        u_ref[...] = (hn * _rstd(hn) * wn_ref[...]).astype(BF16)

    return _row_call(body, [(h, True), (m, True), (w_post, False), (w_pre, False)],
                     [jax.ShapeDtypeStruct((T, D), F32), jax.ShapeDtypeStruct((T, D), BF16)], name=name, rows=T, tr=256)


def _rms_post_loss(h, m, w_post, tgt, *, name):
    T, D = h.shape

    def body(h_ref, m_ref, wp_ref, t_ref, dh_ref, loss_ref):
        @pl.when(pl.program_id(0) == 0)
        def _():
            loss_ref[...] = jnp.zeros_like(loss_ref)

        mm = m_ref[...].astype(F32)
        err = h_ref[...] + mm * _rstd(mm) * wp_ref[...] - t_ref[...]
        dh_ref[...] = err * (1.0 / D)
        loss_ref[...] += _colsum(jnp.sum(err * err, axis=1, keepdims=True))

    return _row_call(body, [(h, True), (m, True), (w_post, False), (tgt, True)],
                     [jax.ShapeDtypeStruct((T, D), F32), jax.ShapeDtypeStruct((1, 1), F32)],
                     name=name, rows=T, tr=256, acc_outs=(1,))


def _rms_bwd_vals(x, w, dy):
    r = _rstd(x)
    xh = x * r
    g = dy * w
    dx = r * (g - xh * jnp.mean(g * xh, axis=-1, keepdims=True))
    return dx, _colsum(dy * xh)


def _rms_post_bwd(m, w_post, dh, *, name):
    T, D = m.shape

    def body(m_ref, w_ref, dh_ref, dm_ref, dw_ref):
        @pl.when(pl.program_id(0) == 0)
        def _():
            dw_ref[...] = jnp.zeros_like(dw_ref)

        dx, dw = _rms_bwd_vals(m_ref[...].astype(F32), w_ref[...], dh_ref[...])
        dm_ref[...] = dx.astype(BF16)
        dw_ref[...] += dw

    return _row_call(body, [(m, True), (w_post, False), (dh, True)],
                     [jax.ShapeDtypeStruct((T, D), BF16), jax.ShapeDtypeStruct((1, D), F32)],
                     name=name, rows=T, tr=256, acc_outs=(1,))


def _rms_pre_bwd(h, w_pre, du, dh_out, *, name):
    T, D = h.shape

    def body(h_ref, w_ref, du_ref, dho_ref, dh_ref, dw_ref):
        @pl.when(pl.program_id(0) == 0)
        def _():
            dw_ref[...] = jnp.zeros_like(dw_ref)

        dx, dw = _rms_bwd_vals(h_ref[...], w_ref[...], du_ref[...].astype(F32))
        dh_ref[...] = dho_ref[...] + dx
        dw_ref[...] += dw

    return _row_call(body, [(h, True), (w_pre, False), (du, True), (dh_out, True)],
                     [jax.ShapeDtypeStruct((T, D), F32), jax.ShapeDtypeStruct((1, D), F32)],
                     name=name, rows=T, tr=256, acc_outs=(1,))


def _rms_pre_post_bwd(h, w_pre, du, dh_out, m_prev, w_post_prev, *, name):
    T, D = h.shape

    def body(h_ref, w_ref, du_ref, dho_ref, m_ref, wp_ref, dh_ref, dm_ref, dw_ref, dwp_ref):
        @pl.when(pl.program_id(0) == 0)
        def _():
            dw_ref[...] = jnp.zeros_like(dw_ref)
            dwp_ref[...] = jnp.zeros_like(dwp_ref)

        dx, dw = _rms_bwd_vals(h_ref[...], w_ref[...], du_ref[...].astype(F32))
        dh = dho_ref[...] + dx
        dh_ref[...] = dh
        dw_ref[...] += dw
        dm, dwp = _rms_bwd_vals(m_ref[...].astype(F32), wp_ref[...], dh)
        dm_ref[...] = dm.astype(BF16)
        dwp_ref[...] += dwp

    return _row_call(body, [(h, True), (w_pre, False), (du, True), (dh_out, True), (m_prev, True), (w_post_prev, False)],
                     [jax.ShapeDtypeStruct((T, D), F32), jax.ShapeDtypeStruct((T, D), BF16),
                      jax.ShapeDtypeStruct((1, D), F32), jax.ShapeDtypeStruct((1, D), F32)],
                     name=name, rows=T, tr=256, acc_outs=(2, 3))


def _ret_consts(T, C, H):
    lg = np.log1p(-np.exp2(-5.0 - np.arange(H, dtype=np.float64)))
    idx = np.arange(C, dtype=np.float64)
    dist = np.abs(idx[:, None] - idx[None, :])
    vis = (idx[None, :] // REF_CHUNK) <= (idx[:, None] // REF_CHUNK)
    mask = np.exp(dist[None] * lg[:, None, None]) * vis[None]
    xi = np.exp((idx[None, :] + 1.0) * lg[:, None])[..., None]
    zeta = np.exp((C - 1.0 - idx)[None, :] * lg[:, None])[..., None]
    half = RET_DK // 2
    inv_freq = ROPE_BASE ** (-np.arange(half, dtype=np.float32) / np.float32(half))
    ang = np.arange(T, dtype=np.float32)[:, None] * inv_freq[None, :].astype(np.float32)
    return (jnp.asarray(mask, F32), jnp.asarray(xi, F32), jnp.asarray(zeta, F32),
            jnp.asarray(np.cos(ang), F32), jnp.asarray(np.sin(ang), F32))


def _rot(t, cos, sin):
    half = RET_DK // 2
    t1, t2 = t[:, :half], t[:, half:]
    return jnp.concatenate([t1 * cos - t2 * sin, t1 * sin + t2 * cos], axis=1)


def _unrot(d, cos, sin):
    half = RET_DK // 2
    d1, d2 = d[:, :half], d[:, half:]
    return jnp.concatenate([d1 * cos + d2 * sin, d2 * cos - d1 * sin], axis=1)


def _ret_specs(C, H, rev, NS):
    def ci(i):
        return NS - 1 - i if rev else i

    nq = H
    q_spec = pl.BlockSpec((C, RET_DK), lambda h, i: (ci(i), h))
    k_spec = pl.BlockSpec((C, RET_DK), lambda h, i: (ci(i), nq + h))
    v_spec = pl.BlockSpec((C, RET_DV), lambda h, i: (ci(i), H + h))
    g_spec = pl.BlockSpec((C, RET_DV), lambda h, i: (ci(i), 2 * H + h))
    cs_spec = pl.BlockSpec((C, RET_DK // 2), lambda h, i: (ci(i), 0))
    m_spec = pl.BlockSpec((None, C, C), lambda h, i: (h, 0, 0))
    vec_spec = pl.BlockSpec((None, C, 1), lambda h, i: (h, 0, 0))
    gn_spec = pl.BlockSpec((1, RET_DV), lambda h, i: (0, h))
    st_spec = pl.BlockSpec((None, None, RET_DK, RET_DV), lambda h, i: (h, ci(i), 0, 0))
    return q_spec, k_spec, v_spec, g_spec, cs_spec, m_spec, vec_spec, gn_spec, st_spec


def _ret_fwd_vals(q, k, v, cos, sin, mask, xi, s_in):
    qr = _rot(q, cos, sin)
    kr = _rot(k, cos, sin) * (RET_DK ** -0.5)
    a = _dot(qr, kr, NT) * mask
    o = _dot(a, v) + _dot(qr, s_in) * xi
    mu = jnp.mean(o, axis=1, keepdims=True)
    oc = o - mu
    rstd = lax.rsqrt(jnp.mean(oc * oc, axis=1, keepdims=True) + GN_EPS)
    return qr, kr, a, oc * rstd, rstd


def _ret_fwd(proj, gn_w, consts, *, C, name):
    T = proj.shape[0]
    H = gn_w.shape[1] // RET_DV
    NS = T // C
    mask, xi, zeta, cos, sin = consts
    q_spec, k_spec, v_spec, g_spec, cs_spec, m_spec, vec_spec, gn_spec, st_spec = _ret_specs(C, H, False, NS)
    y_spec = pl.BlockSpec((C, RET_DV), lambda h, i: (i, h))

    def body(q_ref, k_ref, v_ref, g_ref, cos_ref, sin_ref, m_ref, xi_ref, ze_ref, gn_ref, y_ref, st_ref, S):
        @pl.when(pl.program_id(1) == 0)
        def _():
            S[...] = jnp.zeros_like(S)

        s_in = S[...]
        st_ref[...] = s_in
        v = v_ref[...]
        xi_v = xi_ref[...]
        qr, kr, a, on, rstd = _ret_fwd_vals(q_ref[...], k_ref[...], v, cos_ref[...], sin_ref[...], m_ref[...], xi_v, s_in)
        g = g_ref[...]
        y_ref[...] = (g * _sigmoid(g) * on * gn_ref[...]).astype(BF16)
        S[...] = s_in * xi_v[C - 1:C, :] + _dot(kr * ze_ref[...], v, TN)

    return pl.pallas_call(
        body, name=name, grid=(H, NS),
        in_specs=[q_spec, k_spec, v_spec, g_spec, cs_spec, cs_spec, m_spec, vec_spec, vec_spec, gn_spec],
        out_specs=[y_spec, st_spec],
        out_shape=[jax.ShapeDtypeStruct((T, H * RET_DV), BF16), jax.ShapeDtypeStruct((H, NS, RET_DK, RET_DV), F32)],
        scratch_shapes=[pltpu.VMEM((RET_DK, RET_DV), F32)],
        compiler_params=_cparams("parallel", "arbitrary"),
    )(proj, proj, proj, proj, cos, sin, mask, xi, zeta, gn_w)


def _stage_out(out_hbm, stage, sems, step, n_steps, row0, pieces, values):
    C = stage.shape[1]
    slot = step % 2

    def copies(sl):
        return [pltpu.make_async_copy(stage.at[sl, :, pl.ds(c0, w)],
                                      out_hbm.at[pl.ds(pl.multiple_of(row0, 16), C), pl.ds(pl.multiple_of(dc, 128), w)],
                                      sems.at[sl, n]) for n, (c0, w, dc) in enumerate(pieces)]

    @pl.when(step >= 2)
    def _():
        for cp in copies(slot):
            cp.wait()

    for (c0, w, _), v in zip(pieces, values):
        stage[slot, :, c0:c0 + w] = v
    for cp in copies(slot):
        cp.start()

    @pl.when(step == n_steps - 1)
    def _():
        for cp in copies(slot):
            cp.wait()
        if n_steps >= 2:
            for cp in copies(1 - slot):
                cp.wait()


def _ret_bwd(proj, gn_w, consts, states, dy, *, C, name):
    T = proj.shape[0]
    H = gn_w.shape[1] // RET_DV
    NS = T // C
    mask, xi, zeta, cos, sin = consts
    q_spec, k_spec, v_spec, g_spec, cs_spec, m_spec, vec_spec, gn_spec, st_spec = _ret_specs(C, H, True, NS)
    dy_spec = pl.BlockSpec((C, RET_DV), lambda h, i: (NS - 1 - i, h))
    scale = RET_DK ** -0.5
    wq, wv = H * RET_DK, H * RET_DV

    def body(q_ref, k_ref, v_ref, g_ref, cos_ref, sin_ref, m_ref, xi_ref, ze_ref, gn_ref, st_ref, dy_ref,
             dproj_ref, dgn_ref, dS, stage, sems):
        @pl.when(pl.program_id(1) == 0)
        def _():
            dS[...] = jnp.zeros_like(dS)
            dgn_ref[...] = jnp.zeros_like(dgn_ref)

        s_in = st_ref[...]
        v = v_ref[...]
        cos, sin, mask, xi_v, ze = cos_ref[...], sin_ref[...], m_ref[...], xi_ref[...], ze_ref[...]
        qr, kr, a, on, rstd = _ret_fwd_vals(q_ref[...], k_ref[...], v, cos, sin, mask, xi_v, s_in)
        g = g_ref[...]
        sg = _sigmoid(g)
        silu = g * sg
        gnw = gn_ref[...]
        dy = dy_ref[...].astype(F32)
        dg = (dy * on * gnw * (sg * (1.0 + g * (1.0 - sg)))).astype(BF16)
        t = dy * silu
        dgn_ref[...] += _colsum(t * on)
        don = t * gnw
        do = rstd * (don - jnp.mean(don, axis=1, keepdims=True) - on * jnp.mean(don * on, axis=1, keepdims=True))
        dox = do * xi_v
        ds_out = dS[...]
        da = _dot(do, v, NT) * mask
        kz = kr * ze
        dv = (_dot(a, do, TN) + _dot(kz, ds_out)).astype(BF16)
        dqr = _dot(da, kr) + _dot(dox, s_in, NT)
        dkr = _dot(da, qr, TN) + _dot(v, ds_out, NT) * ze
        dS[...] = ds_out * xi_v[C - 1:C, :] + _dot(qr, dox, TN)
        dq = _unrot(dqr, cos, sin).astype(BF16)
        dk = _unrot(dkr * scale, cos, sin).astype(BF16)
        h, i = pl.program_id(0), pl.program_id(1)
        pieces = [(0, RET_DK, h * RET_DK), (RET_DK, RET_DK, wq + h * RET_DK),
                  (2 * RET_DK, RET_DV, 2 * wq + h * RET_DV), (2 * RET_DK + RET_DV, RET_DV, 2 * wq + wv + h * RET_DV)]
        _stage_out(dproj_ref, stage, sems, h * NS + i, H * NS, (NS - 1 - i) * C, pieces, [dq, dk, dv, dg])

    return pl.pallas_call(
        body, name=name, grid=(H, NS),
        in_specs=[q_spec, k_spec, v_spec, g_spec, cs_spec, cs_spec, m_spec, vec_spec, vec_spec, gn_spec, st_spec, dy_spec],
        out_specs=[_ANY, gn_spec],
        out_shape=[jax.ShapeDtypeStruct((T, 2 * wq + 2 * wv), BF16), jax.ShapeDtypeStruct((1, H * RET_DV), F32)],
        scratch_shapes=[pltpu.VMEM((RET_DK, RET_DV), F32), pltpu.VMEM((2, C, 2 * RET_DK + 2 * RET_DV), BF16),
                        pltpu.SemaphoreType.DMA((2, 4))],
        compiler_params=_cparams("arbitrary", "arbitrary"),
    )(proj, proj, proj, proj, cos, sin, mask, xi, zeta, gn_w, states, dy)


def _shift_down(x, prev8, k):
    if k == 0:
        return x
    y = pltpu.roll(x, k, 0)
    row = lax.broadcasted_iota(jnp.int32, prev8.shape, 0)
    top = jnp.where(row < k, pltpu.roll(prev8, k, 0), y[:8])
    return jnp.concatenate([top, y[8:]], axis=0)


def _shift_up(x, next8, k):
    if k == 0:
        return x
    n = x.shape[0]
    y = pltpu.roll(x, n - k, 0)
    row = lax.broadcasted_iota(jnp.int32, next8.shape, 0)
    bot = jnp.where(row >= 8 - k, pltpu.roll(next8, 8 - k, 0), y[n - 8:])
    return jnp.concatenate([y[:n - 8], bot], axis=0)


def _conv_silu(raw, halo, w, b):
    cv = b
    for tap in range(SSD_CONV_W):
        cv = cv + _shift_down(raw, halo, SSD_CONV_W - 1 - tap) * w[tap:tap + 1, :]
    sg = _sigmoid(cv)
    return cv * sg, cv, sg


def _conv_silu_bwd(d_post, cv, sg, raw, halo, w, carry8):
    dcv = d_post * (sg * (1.0 + cv * (1.0 - sg)))
    d_raw = jnp.zeros_like(raw)
    dws = []
    for tap in range(SSD_CONV_W):
        k = SSD_CONV_W - 1 - tap
        d_raw = d_raw + _shift_up(dcv, carry8, k) * w[tap:tap + 1, :]
        dws.append(_colsum(dcv * _shift_down(raw, halo, k)))
    return d_raw, jnp.concatenate(dws, axis=0), _colsum(dcv), dcv[:8]


def _softplus(x):
    return jnp.maximum(x, 0.0) + jnp.log1p(jnp.exp(-jnp.abs(x)))


def _ssd_common(C, R, dt, dtT, bias, biasT, alog, alogT, E):
    p = dt + bias
    dtv = _softplus(p)
    a = -jnp.exp(alog)
    da = dtv * a
    daT = _softplus(dtT + biasT) * (-jnp.exp(alogT))
    row = lax.broadcasted_iota(jnp.int32, (C, C), 0)
    col = lax.broadcasted_iota(jnp.int32, (C, C), 1)
    tril = row >= col
    trilf = jnp.where(tril, 1.0, 0.0).astype(F32)
    triuf = jnp.where(col >= row, 1.0, 0.0).astype(F32)
    acum = _dot_sel(trilf, da, split="b")
    acumT = _dot_sel(daT, trilf, NT, split="a")
    al = acum[C - 1:C, :]
    ea = jnp.exp(acum)
    dte = jnp.exp(al - acum)
    eal = jnp.exp(al)
    return dict(p=p, dtv=dtv, a=a, da=da, tril=tril, triuf=triuf, acum=acum, acumT=acumT, al=al, ea=ea, dte=dte, eal=eal,
                dtv_e=_dot_sel(dtv, E, split="a", terms=2), ea_e=_dot_sel(ea, E, split="a", terms=2),
                dte_e=_dot_sel(dte, E, split="a", terms=2), eal_e=_dot_sel(eal, E, split="a"))


def _head_decay(q, r, C, R):
    seg = jnp.broadcast_to(q["acum"][:, r:r + 1], (C, C)) - q["acumT"][r:r + 1, :]
    return jnp.exp(jnp.where(q["tril"], seg, -1e30))


def _ssd_group_specs(C, R, NS, rev):
    RP = R * SSD_P
    G = SSD_G
    hb = C // 8

    def ci(i):
        return NS - 1 - i if rev else i

    def halo_row(i):
        return jnp.maximum(ci(i) * hb - 1, 0)

    off_b = G * RP // SSD_N
    z_spec = pl.BlockSpec((C, RP), lambda g, i: (ci(i), g))
    x_spec = pl.BlockSpec((C, RP), lambda g, i: (ci(i), G + g))
    b_spec = pl.BlockSpec((C, SSD_N), lambda g, i: (ci(i), 2 * off_b + g))
    c_spec = pl.BlockSpec((C, SSD_N), lambda g, i: (ci(i), 2 * off_b + G + g))
    xh_spec = pl.BlockSpec((8, RP), lambda g, i: (halo_row(i), G + g))
    bh_spec = pl.BlockSpec((8, SSD_N), lambda g, i: (halo_row(i), 2 * off_b + g))
    ch_spec = pl.BlockSpec((8, SSD_N), lambda g, i: (halo_row(i), 2 * off_b + G + g))
    dt_spec = pl.BlockSpec((None, C, R), lambda g, i: (g, ci(i), 0))
    dtT_spec = pl.BlockSpec((None, R, C), lambda g, i: (g, 0, ci(i)))
    pr_spec = pl.BlockSpec((None, 1, R), lambda g, i: (g, 0, 0))
    prT_spec = pl.BlockSpec((None, R, 1), lambda g, i: (g, 0, 0))
    cwx_spec = pl.BlockSpec((SSD_CONV_W, RP), lambda g, i: (0, g))
    cwn_spec = pl.BlockSpec((SSD_CONV_W, SSD_N), lambda g, i: (0, g))
    cbx_spec = pl.BlockSpec((1, RP), lambda g, i: (0, g))
    cbn_spec = pl.BlockSpec((1, SSD_N), lambda g, i: (0, g))
    e_spec = pl.BlockSpec((R, RP), lambda g, i: (0, 0))
    st_spec = pl.BlockSpec((None, None, SSD_N, RP), lambda g, i: (g, ci(i), 0, 0))
    return dict(z=z_spec, x=x_spec, b=b_spec, c=c_spec, xh=xh_spec, bh=bh_spec, ch=ch_spec, dt=dt_spec, dtT=dtT_spec,
                pr=pr_spec, prT=prT_spec, cwx=cwx_spec, cwn=cwn_spec, cbx=cbx_spec, cbn=cbn_spec, e=e_spec, st=st_spec)


def _ssd_forward_vals(C, R, refs, first, s_in):
    E = refs["E"]
    halo_on = jnp.where(first, 0.0, 1.0)
    xh, bh, ch = refs["xh"] * halo_on, refs["bh"] * halo_on, refs["ch"] * halo_on
    xs, cvx, sgx = _conv_silu(refs["x"], xh, refs["cwx"], refs["cbx"])
    bm, cvb, sgb = _conv_silu(refs["b"], bh, refs["cwb"], refs["cbb"])
    cm, cvc, sgc = _conv_silu(refs["c"], ch, refs["cwc"], refs["cbc"])
    q = _ssd_common(C, R, refs["dt"], refs["dtT"], refs["bias"], refs["biasT"], refs["alog"], refs["alogT"], E)
    xdt = xs * q["dtv_e"]
    cb = _dot(cm, bm, NT)
    yoff_raw = _dot(cm, s_in)
    xdt_b = xdt.astype(BF16)
    low = lax.broadcasted_iota(jnp.int32, (1, 2 * SSD_P), 1) < SSD_P
    pairs = []
    for j in range(R // 2):
        xp = xdt_b[:, 2 * SSD_P * j:2 * SSD_P * (j + 1)]
        y0 = _dot(cb * _head_decay(q, 2 * j, C, R), xp)
        y1 = _dot(cb * _head_decay(q, 2 * j + 1, C, R), xp)
        pairs.append(jnp.where(low, y0, y1))
    ydiag = jnp.concatenate(pairs, axis=1)
    d_e =_dot_sel(refs["dskip"], E, split="a")
    y = ydiag + yoff_raw * q["ea_e"] + d_e * xs
    xd = xdt * q["dte_e"]
    s_out = s_in * q["eal_e"] + _dot(bm, xd, TN)
    z = refs["z"]
    sgz = _sigmoid(z)
    yz = y * (z * sgz)
    rn = lax.rsqrt(jnp.mean(yz * yz, axis=1, keepdims=True) + RMS_EPS)
    return dict(q=q, xh=xh, bh=bh, ch=ch, xs=xs, cvx=cvx, sgx=sgx, bm=bm, cvb=cvb, sgb=sgb, cm=cm, cvc=cvc, sgc=sgc,
                xdt=xdt, cb=cb, yoff_raw=yoff_raw, d_e=d_e, y=y, xd=xd, s_out=s_out, z=z, sgz=sgz, yz=yz, rn=rn)


_SSD_IN_NAMES = ("z", "x", "b", "c", "xh", "bh", "ch", "dt", "dtT", "bias", "biasT", "alog", "alogT", "dskip",
                 "cwx", "cwb", "cwc", "cbx", "cbb", "cbc", "nw", "E")


def _ssd_inputs(pm, dt_g, dtT_g, prm, sp):
    bias, biasT, alog, alogT, dskip, cwx, cwb, cwc, cbx, cbb, cbc, nw, E = prm
    args = [pm, pm, pm, pm, pm, pm, pm, dt_g, dtT_g, bias, biasT, alog, alogT, dskip, cwx, cwb, cwc, cbx, cbb, cbc, nw, E]
    specs = [sp["z"], sp["x"], sp["b"], sp["c"], sp["xh"], sp["bh"], sp["ch"], sp["dt"], sp["dtT"], sp["pr"], sp["prT"],
             sp["pr"], sp["prT"], sp["pr"], sp["cwx"], sp["cwn"], sp["cwn"], sp["cbx"], sp["cbn"], sp["cbn"], sp["cbx"], sp["e"]]
    return args, specs


def _ssd_fwd(pm, dt_g, dtT_g, prm, *, C, R, name):
    T = pm.shape[0]
    NS = T // C
    RP = R * SSD_P
    G = SSD_G
    sp = _ssd_group_specs(C, R, NS, False)
    args, specs = _ssd_inputs(pm, dt_g, dtT_g, prm, sp)
    nin = len(args)

    def body(*refs):
        ins = {n: r[...] for n, r in zip(_SSD_IN_NAMES, refs[:nin])}
        y_ref, st_ref, S = refs[nin:]
        first = pl.program_id(1) == 0

        @pl.when(first)
        def _():
            S[...] = jnp.zeros_like(S)

        s_in = S[...]
        st_ref[...] = s_in
        f = _ssd_forward_vals(C, R, ins, first, s_in)
        y_ref[...] = (f["yz"] * f["rn"] * ins["nw"]).astype(BF16)
        S[...] = f["s_out"]

    return pl.pallas_call(
        body, name=name, grid=(G, NS), in_specs=specs,
        out_specs=[pl.BlockSpec((C, RP), lambda g, i: (i, g)), sp["st"]],
        out_shape=[jax.ShapeDtypeStruct((T, G * RP), BF16), jax.ShapeDtypeStruct((G, NS, SSD_N, RP), F32)],
        scratch_shapes=[pltpu.VMEM((SSD_N, RP), F32)],
        compiler_params=_cparams("parallel", "arbitrary"),
    )(*args)


def _ssd_bwd(pm, dt_g, dtT_g, prm, states, dout, *, C, R, name):
    T = pm.shape[0]
    NS = T // C
    RP = R * SSD_P
    G = SSD_G
    sp = _ssd_group_specs(C, R, NS, True)
    args, specs = _ssd_inputs(pm, dt_g, dtT_g, prm, sp)
    nin = len(args)
    rows_spec = pl.BlockSpec((C, RP), lambda g, i: (NS - 1 - i, g))
    args = args + [states, dout]
    specs = specs + [sp["st"], rows_spec]

    def body(*refs):
        ins = {n: r[...] for n, r in zip(_SSD_IN_NAMES, refs[:nin])}
        st_ref, dout_ref = refs[nin], refs[nin + 1]
        (dpm_ref, ddt_ref, dbias_ref, dalog_ref, dd_ref, dcwx_ref, dcwb_ref, dcwc_ref,
         dcbx_ref, dcbb_ref, dcbc_ref, dnw_ref) = refs[nin + 2:nin + 14]
        dS, cx8, cb8, cc8, stage, sems = refs[nin + 14:]
        acc_refs = (dbias_ref, dalog_ref, dd_ref, dcwx_ref, dcwb_ref, dcwc_ref, dcbx_ref, dcbb_ref, dcbc_ref, dnw_ref)
        step = pl.program_id(1)

        @pl.when(step == 0)
        def _():
            for r_ in acc_refs + (dS, cx8, cb8, cc8):
                r_[...] = jnp.zeros_like(r_)

        first = step == NS - 1
        E = ins["E"]
        s_in = st_ref[...]
        f = _ssd_forward_vals(C, R, ins, first, s_in)
        q = f["q"]
        xs, bm, cm, xdt, cb, y, z, sgz, yz, rn = (f[n] for n in ("xs", "bm", "cm", "xdt", "cb", "y", "z", "sgz", "yz", "rn"))
        nw = ins["nw"]
        dout = dout_ref[...].astype(F32)
        yh = yz * rn
        dnw_ref[...] += _colsum(dout * yh)
        g1 = dout * nw
        dyz = rn * (g1 - yh * jnp.mean(g1 * yh, axis=1, keepdims=True))
        dz = (dyz * y * (sgz * (1.0 + z * (1.0 - sgz)))).astype(BF16)
        dy = dyz * (z * sgz)
        dd_ref[...] += _dot_sel(_colsum(dy * xs), E, NT, split="a")
        dxs = dy * f["d_e"]
        dyo = dy * q["ea_e"]
        dcm = _dot(dyo, s_in, NT)
        ds_acc = _dot(cm, dyo, TN)
        dacum = _dot_sel(dy * f["yoff_raw"], E, NT, split="a", terms=1) * q["ea"]
        dacumT = jnp.zeros((R, C), F32)
        dcb = jnp.zeros((C, C), F32)
        rowR = lax.broadcasted_iota(jnp.int32, (1, R), 1)
        rowRT = lax.broadcasted_iota(jnp.int32, (R, 1), 0)
        dy_b, xdt_b = dy.astype(BF16), xdt.astype(BF16)
        low = lax.broadcasted_iota(jnp.int32, (1, 2 * SSD_P), 1) < SSD_P
        dxdt_pairs = []
        for j in range(R // 2):
            lanes = slice(2 * SSD_P * j, 2 * SSD_P * (j + 1))
            dyp, xp = dy_b[:, lanes], xdt_b[:, lanes]
            halves = []
            for r, mine in ((2 * j, low), (2 * j + 1, jnp.logical_not(low))):
                lr = _head_decay(q, r, C, R)
                w_r = cb * lr
                dw = _dot(jnp.where(mine, dyp, jnp.zeros_like(dyp)), xp, NT)
                halves.append(_dot(w_r, dyp, TN))
                dcb = dcb + dw * lr
                dseg = dw * w_r
                dacum = dacum + jnp.sum(dseg, axis=1, keepdims=True) * jnp.where(rowR == r, 1.0, 0.0)
                dacumT = dacumT - _colsum(dseg) * jnp.where(rowRT == r, 1.0, 0.0)
            dxdt_pairs.append(jnp.where(low, halves[0], halves[1]))
        dxdt = jnp.concatenate(dxdt_pairs, axis=1)
        dsn = dS[...]
        ds_acc = ds_acc + dsn * q["eal_e"]
        d_eal = _dot_sel(_colsum(dsn * s_in), E, NT, split="a")
        dbm = _dot(f["xd"], dsn, NT)
        dxd = _dot(bm, dsn)
        dxdt = dxdt + dxd * q["dte_e"]
        d_dte = _dot_sel(dxd * xdt, E, NT, split="a", terms=1) * q["dte"]
        d_al = _colsum(d_dte) + d_eal * q["eal"]
        dacum = dacum - d_dte
        rowC = lax.broadcasted_iota(jnp.int32, (C, 1), 0)
        dacum = dacum + jnp.where(rowC == C - 1, 1.0, 0.0) * d_al
        dS[...] = ds_acc
        dcm = dcm + _dot(dcb, bm)
        dbm = dbm + _dot(dcb, cm, TN)
        eye = jnp.where(lax.broadcasted_iota(jnp.int32, (C, C), 0) == lax.broadcasted_iota(jnp.int32, (C, C), 1), 1.0, 0.0)
        dacum = dacum + _dot_sel(eye, dacumT, NT, split="b")
        dda = _dot_sel(q["triuf"], dacum, split="b")
        ddtv = dda * q["a"] + _dot_sel(dxdt * xs, E, NT, split="a", terms=1)
        dalog_ref[...] += _colsum(dda * q["dtv"]) * q["a"]
        dxs = dxs + dxdt * q["dtv_e"]
        dp = ddtv * _sigmoid(q["p"])
        ddt_ref[...] = dp
        dbias_ref[...] += _colsum(dp)
        d_raw, d_w, d_b, c8 = _conv_silu_bwd(dxs, f["cvx"], f["sgx"], ins["x"], f["xh"], ins["cwx"], cx8[...])
        dx = d_raw.astype(BF16)
        dcwx_ref[...] += d_w
        dcbx_ref[...] += d_b
        cx8[...] = c8
        d_raw, d_w, d_b, c8 = _conv_silu_bwd(dbm, f["cvb"], f["sgb"], ins["b"], f["bh"], ins["cwb"], cb8[...])
        db = d_raw.astype(BF16)
        dcwb_ref[...] += d_w
        dcbb_ref[...] += d_b
        cb8[...] = c8
        d_raw, d_w, d_b, c8 = _conv_silu_bwd(dcm, f["cvc"], f["sgc"], ins["c"], f["ch"], ins["cwc"], cc8[...])
        dc = d_raw.astype(BF16)
        dcwc_ref[...] += d_w
        dcbc_ref[...] += d_b
        cc8[...] = c8
        g_ = pl.program_id(0)
        pieces = [(0, RP, g_ * RP), (RP, RP, G * RP + g_ * RP), (2 * RP, SSD_N, 2 * G * RP + g_ * SSD_N),
                  (2 * RP + SSD_N, SSD_N, 2 * G * RP + G * SSD_N + g_ * SSD_N)]
        _stage_out(dpm_ref, stage, sems, g_ * NS + step, G * NS, (NS - 1 - step) * C, pieces, [dz, dx, db, dc])

    out_specs = [_ANY, pl.BlockSpec((None, C, R), lambda g, i: (g, NS - 1 - i, 0)),
                 sp["pr"], sp["pr"], sp["pr"], sp["cwx"], sp["cwn"], sp["cwn"], sp["cbx"], sp["cbn"], sp["cbn"], sp["cbx"]]
    out_shape = [jax.ShapeDtypeStruct((T, 2 * G * RP + 2 * G * SSD_N), BF16),
                 jax.ShapeDtypeStruct((G, T, R), F32),
                 jax.ShapeDtypeStruct((G, 1, R), F32), jax.ShapeDtypeStruct((G, 1, R), F32), jax.ShapeDtypeStruct((G, 1, R), F32),
                 jax.ShapeDtypeStruct((SSD_CONV_W, G * RP), F32), jax.ShapeDtypeStruct((SSD_CONV_W, G * SSD_N), F32),
                 jax.ShapeDtypeStruct((SSD_CONV_W, G * SSD_N), F32),
                 jax.ShapeDtypeStruct((1, G * RP), F32), jax.ShapeDtypeStruct((1, G * SSD_N), F32),
                 jax.ShapeDtypeStruct((1, G * SSD_N), F32), jax.ShapeDtypeStruct((1, G * RP), F32)]
    return pl.pallas_call(
        body, name=name, grid=(G, NS), in_specs=specs, out_specs=out_specs, out_shape=out_shape,
        scratch_shapes=[pltpu.VMEM((SSD_N, RP), F32), pltpu.VMEM((8, RP), F32), pltpu.VMEM((8, SSD_N), F32),
                        pltpu.VMEM((8, SSD_N), F32), pltpu.VMEM((2, C, 2 * RP + 2 * SSD_N), BF16),
                        pltpu.SemaphoreType.DMA((2, 4))],
        compiler_params=_cparams("arbitrary", "arbitrary"),
    )(*args)


_ANY = pl.BlockSpec(memory_space=pl.ANY)


def _chip_peer(k):
    x, y, c = lax.axis_index("x"), lax.axis_index("y"), lax.axis_index("c")
    return (x ^ (k >> 1), y ^ (k & 1), c)


def _my_chip():
    return 2 * lax.axis_index("x") + lax.axis_index("y")


def _all_gather_chips(shards, halved, *, name):
    n = len(shards)

    def body(*refs):
        ins, outs = refs[:n], refs[n:2 * n]
        send, recv, fsend, frecv, loc = refs[2 * n:]
        s = _my_chip()
        c = lax.axis_index("c")
        sibling = (lax.axis_index("x"), lax.axis_index("y"), 1 - c)
        copies = []
        for a in range(n):
            cp = pltpu.make_async_copy(ins[a], outs[a].at[s], loc.at[a])
            cp.start()
            copies.append(cp)

        def rows(a, core):
            if not halved[a]:
                return slice(None)
            half = shards[a].shape[0] // 2
            return pl.ds(pl.multiple_of(core * half, 16), half)

        def over_ici(a, k, slot, core):
            return pltpu.make_async_remote_copy(
                src_ref=ins[a].at[rows(a, core)], dst_ref=outs[a].at[slot, rows(a, core)],
                send_sem=send.at[3 * a + k - 1], recv_sem=recv.at[3 * a + k - 1],
                device_id=_chip_peer(k), device_id_type=MESH_ID)

        def over_d2d(a, k, core):
            z = outs[a].at[s ^ k, rows(a, core)]
            return pltpu.make_async_remote_copy(
                src_ref=z, dst_ref=z, send_sem=fsend.at[3 * a + k - 1], recv_sem=frecv.at[3 * a + k - 1],
                device_id=sibling, device_id_type=MESH_ID)

        sent = []
        for a in range(n):
            for k in (1, 2, 3):
                cp = over_ici(a, k, s, c)
                cp.start()
                sent.append(cp)
        passed = []
        for a in range(n):
            for k in (1, 2, 3):
                over_ici(a, k, s ^ k, c).wait_recv()
                if halved[a]:
                    cp = over_d2d(a, k, c)
                    cp.start()
                    passed.append(cp)
        for a in range(n):
            if halved[a]:
                for k in (1, 2, 3):
                    over_d2d(a, k, 1 - c).wait_recv()
        for cp in sent + passed:
            cp.wait_send()
        for cp in copies:
            cp.wait()

    for a, h in zip(shards, halved):
        assert not h or a.shape[0] % 32 == 0, a.shape
    return pl.pallas_call(
        body, name=name, in_specs=[_ANY] * n, out_specs=[_ANY] * n,
        out_shape=[jax.ShapeDtypeStruct((4,) + a.shape, a.dtype) for a in shards],
        scratch_shapes=[pltpu.SemaphoreType.DMA((3 * n,))] * 4 + [pltpu.SemaphoreType.DMA((n,))],
        compiler_params=pltpu.CompilerParams(has_side_effects=True),
    )(*shards)


_HBM = pl.BlockSpec(memory_space=pltpu.HBM)
_SEM = pl.BlockSpec(memory_space=pltpu.SEMAPHORE)
_EFFECT = pltpu.SideEffectType.DATAFLOW_SIDE_EFFECTING


def _split_copies(src, land, send, recv, loc, a, scatter):
    s = _my_chip()
    mine = pltpu.make_async_copy(src.at[s] if scatter else src, land.at[s], loc.at[a])
    pairs = []
    for k in (1, 2, 3):
        sems = dict(send_sem=send.at[3 * a + k - 1], recv_sem=recv.at[3 * a + k - 1],
                    device_id=_chip_peer(k), device_id_type=MESH_ID)
        out = pltpu.make_async_remote_copy(src_ref=src.at[s ^ k] if scatter else src, dst_ref=land.at[s], **sems)
        arriving = pltpu.make_async_remote_copy(src_ref=src.at[s ^ k] if scatter else src, dst_ref=land.at[s ^ k], **sems)
        pairs.append((out, arriving))
    return mine, pairs


def _split_start(arrs, *, scatter, after, name):
    n = len(arrs)
    zones = [lax.empty(a.shape if scatter else (4,) + a.shape, a.dtype) for a in arrs]

    def body(*refs):
        srcs, lands = refs[:n], refs[n:2 * n]
        send, recv, loc = refs[2 * n + 1:2 * n + 4]
        token = refs[-1]
        for a in range(n):
            mine, pairs = _split_copies(srcs[a], lands[a], send, recv, loc, a, scatter)
            mine.start()
            for out, _ in pairs:
                out.start()
        token[...] = jnp.zeros_like(token)

    res = pl.pallas_call(
        body, name=name,
        out_shape=(pltpu.SemaphoreType.DMA((3 * n,)), pltpu.SemaphoreType.DMA((3 * n,)), pltpu.SemaphoreType.DMA((n,)),
                   *[pltpu.HBM(z.shape, z.dtype) for z in zones], jax.ShapeDtypeStruct((8, 128), F32)),
        in_specs=[_ANY] * n + [_HBM] * n + [_ANY],
        out_specs=(_SEM, _SEM, _SEM, *([_HBM] * n), pl.BlockSpec(memory_space=pltpu.VMEM)),
        input_output_aliases={n + i: 3 + i for i in range(n)},
        compiler_params=pltpu.CompilerParams(has_side_effects=_EFFECT),
    )(*arrs, *[pltpu.with_memory_space_constraint(z, pltpu.HBM) for z in zones], after)
    return res[:3], list(res[3:3 + n]), res[-1]


def _split_wait(sems, src, land, a, *, scatter, after, name):
    def body(src_ref, land_ref, send, recv, loc, after_ref, land_out):
        mine, pairs = _split_copies(src_ref, land_ref, send, recv, loc, a, scatter)
        mine.wait()
        for out, arriving in pairs:
            out.wait_send()
            arriving.wait_recv()

    return pl.pallas_call(
        body, name=name, out_shape=pltpu.HBM(land.shape, land.dtype),
        in_specs=[_ANY, _HBM, _SEM, _SEM, _SEM, _ANY], out_specs=_HBM, input_output_aliases={1: 0},
        compiler_params=pltpu.CompilerParams(has_side_effects=_EFFECT),
    )(src, land, *sems, after)


def _sibling_copies(srcs, lands, send, recv):
    sib = (lax.axis_index("x"), lax.axis_index("y"), 1 - lax.axis_index("c"))
    return [pltpu.make_async_remote_copy(src_ref=srcs[a], dst_ref=lands[a], send_sem=send.at[a], recv_sem=recv.at[a],
                                         device_id=sib, device_id_type=MESH_ID) for a in range(len(srcs))]


def _swap_start(arrs, *, after, name):
    n = len(arrs)
    zones = [lax.empty(a.shape, a.dtype) for a in arrs]

    def body(*refs):
        for cp in _sibling_copies(refs[:n], refs[n:2 * n], refs[2 * n + 1], refs[2 * n + 2]):
            cp.start()

    res = pl.pallas_call(
        body, name=name,
        out_shape=(pltpu.SemaphoreType.DMA((n,)), pltpu.SemaphoreType.DMA((n,)), *[pltpu.HBM(z.shape, z.dtype) for z in zones]),
        in_specs=[_ANY] * n + [_HBM] * n + [_ANY], out_specs=(_SEM, _SEM, *([_HBM] * n)),
        input_output_aliases={n + i: 2 + i for i in range(n)},
        compiler_params=pltpu.CompilerParams(has_side_effects=_EFFECT),
    )(*arrs, *[pltpu.with_memory_space_constraint(z, pltpu.HBM) for z in zones], after)
    return res[:2], list(res[2:])


def _swap_wait(sems, srcs, lands, *, after, name):
    n = len(srcs)

    def body(*refs):
        for cp in _sibling_copies(refs[:n], refs[n:2 * n], refs[2 * n], refs[2 * n + 1]):
            cp.wait_send()
            cp.wait_recv()

    res = pl.pallas_call(
        body, name=name, out_shape=tuple(pltpu.HBM(a.shape, a.dtype) for a in lands),
        in_specs=[_ANY] * n + [_HBM] * n + [_SEM, _SEM, _ANY], out_specs=tuple([_HBM] * n),
        input_output_aliases={n + i: i for i in range(n)},
        compiler_params=pltpu.CompilerParams(has_side_effects=_EFFECT),
    )(*srcs, *lands, *sems, after)
    return list(res)


def _all_gather_devices(v, *, name):
    r = v.shape[0]

    def body(v_ref, out_ref, send, recv):
        x, y, c = lax.axis_index("x"), lax.axis_index("y"), lax.axis_index("c")
        me = 4 * x + 2 * y + c
        out_ref[me] = v_ref[...]
        cps = []
        for k in range(1, 8):
            peer = (x ^ (k >> 2), y ^ ((k >> 1) & 1), c ^ (k & 1))
            cp = pltpu.make_async_remote_copy(src_ref=v_ref, dst_ref=out_ref.at[me], send_sem=send.at[k - 1],
                                              recv_sem=recv.at[k - 1], device_id=peer, device_id_type=MESH_ID)
            cp.start()
            cps.append(cp)
        for k, cp in enumerate(cps, start=1):
            cp.wait_send()
            pltpu.make_async_remote_copy(src_ref=v_ref, dst_ref=out_ref.at[me ^ k], send_sem=send.at[k - 1],
                                         recv_sem=recv.at[k - 1], device_id=(x, y, c), device_id_type=MESH_ID).wait_recv()

    vm = pl.BlockSpec(memory_space=pltpu.VMEM)
    return pl.pallas_call(
        body, name=name, in_specs=[vm], out_specs=vm, out_shape=jax.ShapeDtypeStruct((8, r, 128), F32),
        scratch_shapes=[pltpu.SemaphoreType.DMA((7,)), pltpu.SemaphoreType.DMA((7,))],
        compiler_params=pltpu.CompilerParams(has_side_effects=True),
    )(v)


def _row_tile(r, target):
    best = None
    for t in range(16, min(target, r) + 1, 16):
        if r % t == 0:
            best = t
    return best or r


def _sum_slots(buf, *, name, tr=384):
    S, r, c = buf.shape
    tr = _row_tile(r, tr)

    def body(b_ref, o_ref):
        acc = b_ref[0].astype(F32)
        for j in range(1, S):
            acc = acc + b_ref[j].astype(F32)
        o_ref[...] = acc

    return pl.pallas_call(
        body, name=name, grid=(r // tr,), in_specs=[pl.BlockSpec((S, tr, c), lambda i: (0, i, 0))],
        out_specs=pl.BlockSpec((tr, c), lambda i: (i, 0)), out_shape=jax.ShapeDtypeStruct((r, c), F32),
        compiler_params=_cparams("parallel"),
    )(buf)


ADAMW_BLOCK_ELEMS = 1 << 18


def _adamw(w, gs, m, v, *, name, tr=256, layer=None, stack=None):
    r, c = w.shape[-2:]
    tr = _row_tile(r, min(tr, max(16, ADAMW_BLOCK_ELEMS // c)))
    bc1 = 1.0 - ADAM_B1 ** ADAM_STEP
    bc2 = 1.0 - ADAM_B2 ** ADAM_STEP
    ng = len(gs)

    def body(*refs):
        w_ref, m_ref, v_ref = refs[0], refs[1 + ng], refs[2 + ng]
        g_ref, d_ref, mo_ref, vo_ref = refs[-4:]
        gg = refs[1][...] if ng == 1 else refs[1][...] + refs[2][...]
        mn = ADAM_B1 * m_ref[...] + (1.0 - ADAM_B1) * gg
        vn = ADAM_B2 * v_ref[...] + (1.0 - ADAM_B2) * (gg * gg)
        g_ref[...] = gg
        mo_ref[...] = mn
        vo_ref[...] = vn
        d_ref[...] = -ADAM_LR * ((mn / bc1) / (jnp.sqrt(vn / bc2) + ADAM_EPS) + ADAM_WD * w_ref[...])

    spec = pl.BlockSpec((tr, c), lambda i: (i, 0))
    if layer is None:
        wspec, shape = spec, (r, c)
    else:
        wspec, shape = pl.BlockSpec((None, tr, c), lambda i: (layer, i, 0)), (2, r, c)
    in_specs = [wspec] + [spec] * ng + [wspec, wspec]
    args, alias = [w, *gs, m, v], {}
    if stack is not None:
        in_specs += [_ANY] * 4
        alias = {len(args) + n: n for n in range(4)}
        args += list(stack)
    return pl.pallas_call(body, name=name, grid=(r // tr,), in_specs=in_specs, out_specs=[wspec] * 4,
                          out_shape=[jax.ShapeDtypeStruct(shape, F32)] * 4, input_output_aliases=alias,
                          compiler_params=_cparams("parallel"))(*args)


def _pack(vecs, rows):
    flat = jnp.concatenate([v.reshape(-1).astype(F32) for v in vecs])
    return jnp.pad(flat, (0, rows * 128 - flat.shape[0])).reshape(rows, 128)


def _unpack(packed, shapes):
    flat = packed.reshape(-1)
    out, off = [], 0
    for s in shapes:
        n = math.prod(s)
        out.append(flat[off:off + n].reshape(s))
        off += n
    return out


def _pack_rows(shapes):
    n = sum(math.prod(s) for s in shapes)
    return -(-n // 1024) * 8


def kernel(x, norm_mix_pre, norm_mix_post, norm_ffn_pre, norm_ffn_post, ret_w_in, ret_gn_w, ret_w_out, ssd_w_in, ssd_conv_w, ssd_conv_b, ssd_dt_bias, ssd_a_log, ssd_d, ssd_norm_w, ssd_w_out, mlp_w_up, mlp_w_down, loss_target, m_norm_mix_pre, m_norm_mix_post, m_norm_ffn_pre, m_norm_ffn_post, m_ret_w_in, m_ret_gn_w, m_ret_w_out, m_ssd_w_in, m_ssd_conv_w, m_ssd_conv_b, m_ssd_dt_bias, m_ssd_a_log, m_ssd_d, m_ssd_norm_w, m_ssd_w_out, m_mlp_w_up, m_mlp_w_down, v_norm_mix_pre, v_norm_mix_post, v_norm_ffn_pre, v_norm_ffn_post, v_ret_w_in, v_ret_gn_w, v_ret_w_out, v_ssd_w_in, v_ssd_conv_w, v_ssd_conv_b, v_ssd_dt_bias, v_ssd_a_log, v_ssd_d, v_ssd_norm_w, v_ssd_w_out, v_mlp_w_up, v_mlp_w_down):
    T, D = x.shape[1], x.shape[2]
    H = D // RET_DK
    d_inner = 2 * D
    R = d_inner // SSD_P // SSD_G
    RP = R * SSD_P
    n_heads = SSD_G * R
    conv_dim = d_inner + 2 * SSD_G * SSD_N
    n_main = d_inner + conv_dim
    C = min(256, T)
    chip = _my_chip()
    xs, tgt = x[0], loss_target[0]

    conv_sh = ssd_conv_w.shape[2]
    small_shapes = [(SSD_CONV_W, conv_sh), (conv_sh,), (ssd_norm_w.shape[1],)]
    small_rows = _pack_rows(small_shapes)
    shards = [ret_w_in[0].T.astype(BF16), ret_w_out[0].astype(BF16), ssd_w_in[0].T.astype(BF16), ssd_w_out[0].astype(BF16),
              mlp_w_up[0].T.astype(BF16), mlp_w_up[1].T.astype(BF16), mlp_w_down[0].astype(BF16), mlp_w_down[1].astype(BF16)]
    (ret_in_g, small_g) = _all_gather_chips([shards[0], _pack([ssd_conv_w[0], ssd_conv_b[0], ssd_norm_w[0]], small_rows)],
                                            [True, False], name="gather_first")

    def full(g):
        return g.reshape(4 * g.shape[1], g.shape[2])

    def start_gather(idx, after, name):
        sems, zones, tok = _split_start([shards[i] for i in idx], scatter=False, after=after, name=name)
        return {i: (sems, shards[i], zones[n], n) for n, i in enumerate(idx)}, tok

    def arrived(stage, i, after, name):
        sems, src, zone, n = stage[i]
        return full(_split_wait(sems, src, zone, n, scatter=False, after=after, name=name))

    ret_in_t = full(ret_in_g)
    sm = [_unpack(small_g[j], small_shapes) for j in range(4)]
    conv_w = jnp.concatenate([sm[j][0] for j in range(4)], axis=1)
    conv_b = jnp.concatenate([sm[j][1] for j in range(4)])[None, :]
    norm_w = jnp.concatenate([sm[j][2] for j in range(4)])[None, :]

    gb = SSD_G * SSD_N
    ssd_prm = (ssd_dt_bias.reshape(SSD_G, 1, R), ssd_dt_bias.reshape(SSD_G, R, 1),
               ssd_a_log.reshape(SSD_G, 1, R), ssd_a_log.reshape(SSD_G, R, 1), ssd_d.reshape(SSD_G, 1, R),
               conv_w[:, :d_inner], conv_w[:, d_inner:d_inner + gb], conv_w[:, d_inner + gb:],
               conv_b[:, :d_inner], conv_b[:, d_inner:d_inner + gb], conv_b[:, d_inner + gb:],
               norm_w, jnp.asarray(np.kron(np.eye(R), np.ones((1, SSD_P))), F32))
    ret_consts = _ret_consts(T, C, H)

    u0 = _rms_pre(xs, norm_mix_pre[0:1], name="pre0")
    stage1, tok = start_gather((1, 4), ret_in_g, "gather_start1")
    proj = _matmul(u0, ret_in_t, "nt", out_dtype=F32, name="ret_in", after=tok)
    stage2, tok = start_gather((6, 2), proj, "gather_start2")
    y_ret, st_ret = _ret_fwd(proj, ret_gn_w, ret_consts, C=C, name="ret_fwd")
    ret_out = arrived(stage1, 1, y_ret, "gather_wait_ret_out")
    m0 = _matmul(y_ret, ret_out, "nn", out_dtype=BF16, name="ret_out", after=tok)
    h1, u1 = _rms_post_pre(xs, m0, norm_mix_post[0:1], norm_ffn_pre[0:1], name="post_pre1")
    up_t0 = arrived(stage1, 4, u1, "gather_wait_up0")
    a0, hh0 = _matmul(u1, up_t0, "nt", out_dtype=BF16, name="mlp_up0", epi="relu2")
    stage3, tok = start_gather((3, 5, 7), hh0, "gather_start3")
    down0 = arrived(stage2, 6, hh0, "gather_wait_down0")
    f0 = _matmul(hh0, down0, "nn", out_dtype=BF16, name="mlp_down0", after=tok)
    h2, u2 = _rms_post_pre(h1, f0, norm_ffn_post[0:1], norm_mix_pre[1:2], name="post_pre2")
    ssd_in_t = arrived(stage2, 2, u2, "gather_wait_ssd_in")
    pm, pdt = _matmul(u2, ssd_in_t, "nt", out_dtype=F32, name="ssd_in", tail=(n_heads,))
    dt_g = pdt.reshape(T, SSD_G, R).transpose(1, 0, 2)
    dtT_g = pdt.reshape(T, SSD_G, R).transpose(1, 2, 0)
    y_ssd, st_ssd = _ssd_fwd(pm, dt_g, dtT_g, ssd_prm, C=C, R=R, name="ssd_fwd")
    ssd_out = arrived(stage3, 3, y_ssd, "gather_wait_ssd_out")
    m1 = _matmul(y_ssd, ssd_out, "nn", out_dtype=BF16, name="ssd_out")
    h3, u3 = _rms_post_pre(h2, m1, norm_mix_post[1:2], norm_ffn_pre[1:2], name="post_pre3")
    up_t1 = arrived(stage3, 5, u3, "gather_wait_up1")
    a1, hh1 = _matmul(u3, up_t1, "nt", out_dtype=BF16, name="mlp_up1", epi="relu2")
    down1 = arrived(stage3, 7, hh1, "gather_wait_down1")
    f1 = _matmul(hh1, down1, "nn", out_dtype=BF16, name="mlp_down1")
    up_t, down = (up_t0, up_t1), (down0, down1)
    dh4, sq = _rms_post_loss(h3, f1, norm_ffn_post[1:2], tgt, name="post_loss")
    loss = lax.psum(sq[0, 0], MESH_AXES) * (0.5 / D)

    in_flight = []

    def send_grad(g, name):
        part = g if g.ndim == 3 else g.reshape(4, g.shape[0] // 4, g.shape[1])
        sems, zones, tok = _split_start([part], scatter=True, after=part, name=f"scatter_start_{name}")
        in_flight.append((name, sems, part, zones[0]))
        return tok

    retired = []

    def retire(name, after):
        nm, sems, src, zone = in_flight.pop(0)
        assert nm == name
        part = _sum_slots(_split_wait(sems, src, zone, 0, scatter=True, after=after, name=f"scatter_wait_{nm}"),
                          name=f"sum_chips_{nm}")
        sw_sems, zones = _swap_start([part], after=part, name=f"swap_start_{nm}")
        retired.append((nm, part, sw_sems, zones))
        return zones[0]

    def mlp_bwd(i, df, u, a, hh):
        tok = send_grad(_matmul(hh, df, "tn", out_dtype=BF16, name=f"mlp_down_wg{i}"), f"down{i}")
        da = _matmul(df, down[i], "nt", out_dtype=BF16, name=f"mlp_down_dg{i}", epi="drelu2", extra=a, after=tok)
        tok = send_grad(_matmul(u, da, "tn", out_dtype=BF16, name=f"mlp_up_wg{i}", col_parts=4), f"up{i}")
        return _matmul(da, up_t[i], "nn", out_dtype=BF16, name=f"mlp_up_dg{i}", after=tok)

    df1, d_nfpost1 = _rms_post_bwd(f1, norm_ffn_post[1:2], dh4, name="post_bwd_ffn1")
    du3 = mlp_bwd(1, df1, u3, a1, hh1)
    dh3, dm1, d_nfp1, d_nmpost1 = _rms_pre_post_bwd(h3, norm_ffn_pre[1:2], du3, dh4, m1, norm_mix_post[1:2],
                                                    name="pre_bwd_ffn1_post_bwd_mix1")
    tok = send_grad(_matmul(y_ssd, dm1, "tn", out_dtype=BF16, name="ssd_out_wg"), "ssd_out")
    dy_ssd = _matmul(dm1, ssd_out, "nt", out_dtype=F32, name="ssd_out_dg", after=tok)
    (dpm, ddt_g, d_bias, d_alog, d_dskip, dcwx, dcwb, dcwc, dcbx, dcbb, dcbc, d_normw) = _ssd_bwd(
        pm, dt_g, dtT_g, ssd_prm, st_ssd, dy_ssd, C=C, R=R, name="ssd_bwd")
    dpdt = ddt_g.transpose(1, 0, 2).reshape(T, n_heads).astype(BF16)
    tok = send_grad(jnp.concatenate([_matmul(dpm, u2, "tn", out_dtype=BF16, name="ssd_in_wg"),
                                     _matmul(dpdt, u2, "tn", out_dtype=BF16, name="ssd_in_dt_wg")], axis=0), "ssd_in")
    du2 = _matmul(dpm, ssd_in_t, "nn", out_dtype=BF16, name="ssd_in_dg", after=tok, tail=(n_heads, dpdt))
    dh2, df0, d_nmp1, d_nfpost0 = _rms_pre_post_bwd(h2, norm_mix_pre[1:2], du2, dh3, f0, norm_ffn_post[0:1],
                                                    name="pre_bwd_mix1_post_bwd_ffn0")
    du1 = mlp_bwd(0, df0, u1, a0, hh0)
    dh1, dm0, d_nfp0, d_nmpost0 = _rms_pre_post_bwd(h1, norm_ffn_pre[0:1], du1, dh2, m0, norm_mix_post[0:1],
                                                    name="pre_bwd_ffn0_post_bwd_mix0")
    tok = send_grad(_matmul(y_ret, dm0, "tn", out_dtype=BF16, name="ret_out_wg"), "ret_out")
    dy_ret = _matmul(dm0, ret_out, "nt", out_dtype=F32, name="ret_out_dg", after=tok)
    dproj, d_gn = _ret_bwd(proj, ret_gn_w, ret_consts, st_ret, dy_ret, C=C, name="ret_bwd")
    tok = send_grad(_matmul(u0, dproj, "tn", out_dtype=BF16, name="ret_in_wg", col_parts=4), "ret_in")
    du0 = _matmul(dproj, ret_in_t, "nn", out_dtype=BF16, name="ret_in_dg", after=tok)
    grad_x, d_nmp0 = _rms_pre_bwd(xs, norm_mix_pre[0:1], du0, dh1, name="pre_bwd_mix0")

    prev = grad_x
    for nm in ("down1", "up1", "ssd_out", "ssd_in", "down0", "up0", "ret_out", "ret_in"):
        prev = retire(nm, prev)

    def upd(w, gs, m, v, name):
        shp = w.shape
        w2, m2, v2 = (t.reshape(-1, shp[-1]) for t in (w, m, v))
        return tuple(t.reshape(shp) for t in _adamw(w2, gs, m2, v2, name=name))

    def upd_t(w, gs, m, v, name):
        return tuple(t.T[None] for t in _adamw(w[0].T, gs, m[0].T, v[0].T, name=name))

    def upd_layer(layer):
        def fn(w, gs, m, v, name):
            return tuple(_adamw(w, gs, m, v, name=name, layer=layer, stack=res.get(out_of[name[len("adamw_"):]])))
        return fn

    out_of = {"ret_in": "ret_w_in", "ret_out": "ret_w_out", "ssd_in": "ssd_w_in", "ssd_out": "ssd_w_out",
              "up0": "mlp_w_up", "up1": "mlp_w_up", "down0": "mlp_w_down", "down1": "mlp_w_down"}
    todo = {"ret_in": (upd, ret_w_in, m_ret_w_in, v_ret_w_in), "ret_out": (upd, ret_w_out, m_ret_w_out, v_ret_w_out),
            "ssd_in": (upd_t, ssd_w_in, m_ssd_w_in, v_ssd_w_in), "ssd_out": (upd, ssd_w_out, m_ssd_w_out, v_ssd_w_out),
            "up0": (upd_layer(0), mlp_w_up, m_mlp_w_up, v_mlp_w_up), "up1": (upd_layer(1), mlp_w_up, m_mlp_w_up, v_mlp_w_up),
            "down0": (upd_layer(0), mlp_w_down, m_mlp_w_down, v_mlp_w_down),
            "down1": (upd_layer(1), mlp_w_down, m_mlp_w_down, v_mlp_w_down)}
    res = {}
    for nm, mine, sems, zones in retired:
        other = _swap_wait(sems, [mine], zones, after=prev, name=f"swap_wait_{nm}")[0]
        fn, w, m, v = todo[nm]
        res[out_of[nm]] = fn(w, [mine, other], m, v, f"adamw_{nm}")
        prev = res[out_of[nm]][0]

    d_conv_w = jnp.concatenate([dcwx, dcwb, dcwc], axis=1)
    d_conv_b = jnp.concatenate([dcbx, dcbb, dcbc], axis=1)
    small_grads = [jnp.concatenate([d_nmp0, d_nmp1]), jnp.concatenate([d_nmpost0, d_nmpost1]),
                   jnp.concatenate([d_nfp0, d_nfp1]), jnp.concatenate([d_nfpost0, d_nfpost1]),
                   d_gn, d_bias.reshape(1, n_heads), d_alog.reshape(1, n_heads), d_dskip.reshape(1, n_heads),
                   d_conv_w, d_conv_b, d_normw]
    sg_shapes = [g.shape for g in small_grads]
    sg_rows = _pack_rows(sg_shapes)
    everyone = _all_gather_devices(_pack(small_grads, sg_rows), name="gather_small_grads")
    sg = _unpack(_sum_slots(everyone, name="sum_small_grads", tr=sg_rows), sg_shapes)
    (g_nmp, g_nmpost, g_nfp, g_nfpost, g_gn, g_bias, g_alog, g_dskip, g_cw_full, g_cb_full, g_nw_full) = sg
    g_cw = lax.dynamic_slice_in_dim(g_cw_full, chip * conv_sh, conv_sh, axis=1)[None]
    g_cb = lax.dynamic_slice_in_dim(g_cb_full, chip * conv_sh, conv_sh, axis=1)
    nw_sh = ssd_norm_w.shape[1]
    g_nw = lax.dynamic_slice_in_dim(g_nw_full, chip * nw_sh, nw_sh, axis=1)
    small = [("norm_mix_pre", norm_mix_pre, g_nmp, m_norm_mix_pre, v_norm_mix_pre),
             ("norm_mix_post", norm_mix_post, g_nmpost, m_norm_mix_post, v_norm_mix_post),
             ("norm_ffn_pre", norm_ffn_pre, g_nfp, m_norm_ffn_pre, v_norm_ffn_pre),
             ("norm_ffn_post", norm_ffn_post, g_nfpost, m_norm_ffn_post, v_norm_ffn_post),
             ("ret_gn_w", ret_gn_w, g_gn, m_ret_gn_w, v_ret_gn_w),
             ("ssd_conv_w", ssd_conv_w, g_cw, m_ssd_conv_w, v_ssd_conv_w),
             ("ssd_conv_b", ssd_conv_b, g_cb, m_ssd_conv_b, v_ssd_conv_b),
             ("ssd_dt_bias", ssd_dt_bias, g_bias, m_ssd_dt_bias, v_ssd_dt_bias),
             ("ssd_a_log", ssd_a_log, g_alog, m_ssd_a_log, v_ssd_a_log),
             ("ssd_d", ssd_d, g_dskip, m_ssd_d, v_ssd_d),
             ("ssd_norm_w", ssd_norm_w, g_nw, m_ssd_norm_w, v_ssd_norm_w)]
    sw_shapes = [w.shape for _, w, _, _, _ in small]
    sw_rows = _pack_rows(sw_shapes)
    packs = [_pack([t[j] for t in small], sw_rows) for j in (1, 2, 3, 4)]
    _, d_p, m_p, v_p = _adamw(packs[0], [packs[1]], packs[2], packs[3], name="adamw_small", tr=sw_rows)
    d_s, m_s, v_s = _unpack(d_p, sw_shapes), _unpack(m_p, sw_shapes), _unpack(v_p, sw_shapes)
    for j, (nm, w, g, _, _) in enumerate(small):
        res[nm] = (g.reshape(w.shape), d_s[j], m_s[j], v_s[j])

    order = ["norm_mix_pre", "norm_mix_post", "norm_ffn_pre", "norm_ffn_post", "ret_w_in", "ret_gn_w", "ret_w_out",
             "ssd_w_in", "ssd_conv_w", "ssd_conv_b", "ssd_dt_bias", "ssd_a_log", "ssd_d", "ssd_norm_w", "ssd_w_out",
             "mlp_w_up", "mlp_w_down"]
    return (loss, grad_x[None], *[res[n][0] for n in order], *[res[n][1] for n in order],
            *[res[n][2] for n in order], *[res[n][3] for n in order])
```

```python
import math

import numpy as np
import jax
import jax.numpy as jnp
from jax import lax
from jax.experimental import pallas as pl
from jax.experimental.pallas import tpu as pltpu

F32 = jnp.float32
BF16 = jnp.bfloat16
VMEM_LIMIT_BYTES = 56 * 1024 * 1024
MESH_AXES = ("x", "y", "c")
MESH_ID = pl.DeviceIdType.MESH

RMS_EPS = 1e-6
GN_EPS = 1e-5
RET_DK = 256
RET_DV = 512
ROPE_BASE = 10000.0
REF_CHUNK = 64
SSD_P = 64
SSD_N = 128
SSD_G = 8
SSD_CONV_W = 4
ADAM_LR, ADAM_B1, ADAM_B2, ADAM_EPS, ADAM_WD, ADAM_STEP = 0.001, 0.9, 0.999, 1e-08, 0.01, 10

NN = (((1,), (0,)), ((), ()))
NT = (((1,), (1,)), ((), ()))
TN = (((0,), (0,)), ((), ()))


def _cparams(*sem):
    return pltpu.CompilerParams(dimension_semantics=sem, vmem_limit_bytes=VMEM_LIMIT_BYTES)


def _dot(a, b, dims=NN):
    return lax.dot_general(a.astype(BF16), b.astype(BF16), dims, preferred_element_type=F32)


def _split_bf16(x, terms):
    parts, rest = [], x
    for _ in range(terms):
        p = rest.astype(BF16)
        parts.append(p)
        rest = rest - p.astype(F32)
    return parts


def _dot_sel(a, b, dims=NN, *, split, terms=3):
    if split == "a":
        sel = b.astype(BF16)
        return sum(lax.dot_general(p, sel, dims, preferred_element_type=F32) for p in _split_bf16(a, terms))
    sel = a.astype(BF16)
    return sum(lax.dot_general(sel, p, dims, preferred_element_type=F32) for p in _split_bf16(b, terms))


def _sigmoid(x):
    return 1.0 / (1.0 + jnp.exp(-x))


def _colsum(x):
    return jnp.sum(x, axis=0, keepdims=True)


MM_TILE = 1024
MM_FULL_K = 2048


def _mm_tiles(M, N, K):
    if K <= MM_FULL_K:
        return min(M, 2 * MM_TILE), min(N, MM_TILE), K
    return min(M, MM_TILE), min(N, 2 * MM_TILE), 2 * MM_TILE if K % (2 * MM_TILE) == 0 else MM_TILE


def _matmul(a, b, mode, *, out_dtype, name, epi=None, extra=None, after=None, col_parts=None, tail=None):
    nt_ = tail[0] if tail else 0
    if mode == "nn":
        (M, K), (K2, N) = a.shape, (b.shape[0] - nt_, b.shape[1])
    elif mode == "nt":
        (M, K), (N, K2) = a.shape, (b.shape[0] - nt_, b.shape[1])
    else:
        (K, M), (K2, N) = a.shape, b.shape
    assert K == K2, (a.shape, b.shape, mode)
    tm, tn, tk = _mm_tiles(M, N, K)
    if col_parts:
        tm, tn = min(M, 2 * MM_TILE), min(tn, MM_TILE)
        while (N // col_parts) % tn:
            tn //= 2
    assert M % tm == 0 and N % tn == 0 and K % tk == 0, (M, N, K, tm, tn, tk)
    nk = K // tk
    if mode == "tn":
        a_spec = pl.BlockSpec((tk, tm), lambda i, j, k: (k, i))
    else:
        a_spec = pl.BlockSpec((tm, tk), lambda i, j, k: (i, k))
    if mode == "nt":
        b_spec = pl.BlockSpec((tn, tk), lambda i, j, k: (j, k))
    else:
        b_spec = pl.BlockSpec((tk, tn), lambda i, j, k: (k, j))
    dims = {"nn": NN, "nt": NT, "tn": TN}[mode]
    o_spec = pl.BlockSpec((tm, tn), lambda i, j, k: (i, j))
    out_dims = (M, N)
    if col_parts:
        per = N // col_parts // tn
        o_spec = pl.BlockSpec((None, tm, tn), lambda i, j, k: (j // per, i, j % per))
        out_dims = (col_parts, M, N // col_parts)
    has_extra = epi in ("drelu2", "add")
    n_out = 2 if epi == "relu2" else 1

    in_specs = [a_spec, b_spec] + ([o_spec] if has_extra else [])
    args = [a, b] + ([extra] if has_extra else [])
    if after is not None:
        in_specs.append(pl.BlockSpec(after.shape, lambda i, j, k: (0, 0)))
        args.append(after)
    n_plain = len(args)
    out_specs = [o_spec] * n_out
    out_shape = [jax.ShapeDtypeStruct(out_dims, out_dtype)] * n_out
    if tail and mode == "nt":
        assert nk == 1 and N % nt_ == 0
        in_specs.append(pl.BlockSpec((nt_, tk), lambda i, j, k: (N // nt_, 0)))
        args.append(b)
        out_specs.append(pl.BlockSpec((tm, nt_), lambda i, j, k: (i, 0)))
        out_shape.append(jax.ShapeDtypeStruct((M, nt_), F32))
    elif tail:
        assert mode == "nn" and K % nt_ == 0
        in_specs += [pl.BlockSpec((tm, nt_), lambda i, j, k: (i, 0)), pl.BlockSpec((nt_, tn), lambda i, j, k: (K // nt_, j))]
        args += [tail[1], b]
    n_in = len(args)

    def body(*refs):
        a_ref, b_ref = refs[0], refs[1]
        e_ref = refs[2] if has_extra else None
        outs = refs[n_in:n_in + n_out]

        def finish(r):
            if tail and mode == "nn":
                r = r + _dot(refs[n_plain][...], refs[n_plain + 1][...])
            if epi is None:
                outs[0][...] = r.astype(outs[0].dtype)
            elif epi == "relu2":
                outs[0][...] = r.astype(outs[0].dtype)
                h = jnp.maximum(r, 0.0)
                outs[1][...] = (h * h).astype(outs[1].dtype)
            elif epi == "drelu2":
                av = jnp.maximum(e_ref[...].astype(F32), 0.0)
                outs[0][...] = (r * (2.0 * av)).astype(outs[0].dtype)
            else:
                outs[0][...] = (r + e_ref[...].astype(F32)).astype(outs[0].dtype)

        if tail and mode == "nt":
            @pl.when(pl.program_id(1) == 0)
            def _():
                refs[n_in + n_out][...] = _dot(a_ref[...], refs[n_plain][...], NT)

        if nk == 1:
            finish(_dot(a_ref[...], b_ref[...], dims))
            return
        acc = refs[-1]
        k = pl.program_id(2)

        @pl.when(k == 0)
        def _():
            acc[...] = jnp.zeros_like(acc)

        acc[...] += _dot(a_ref[...], b_ref[...], dims)

        @pl.when(k == nk - 1)
        def _():
            finish(acc[...])

    res = pl.pallas_call(
        body, name=name, grid=(M // tm, N // tn, nk), in_specs=in_specs, out_specs=out_specs,
        out_shape=out_shape, scratch_shapes=[pltpu.VMEM((tm, tn), F32)] if nk > 1 else [],
        compiler_params=_cparams("parallel", "arbitrary" if tail and mode == "nt" else "parallel", "arbitrary"),
    )(*args)
    return res if len(res) > 1 else res[0]


def _rstd(x):
    return lax.rsqrt(jnp.mean(x * x, axis=-1, keepdims=True) + RMS_EPS)


def _row_call(body, ins, outs_shape, *, name, rows, tr, acc_outs=()):
    tr = min(tr, rows)
    assert rows % tr == 0
    in_specs = []
    for arr, blocked in ins:
        if blocked:
            in_specs.append(pl.BlockSpec((tr, arr.shape[1]), lambda i: (i, 0)))
        else:
            in_specs.append(pl.BlockSpec(arr.shape, lambda i: (0, 0)))
    out_specs = []
    for n, s in enumerate(outs_shape):
        if n in acc_outs:
            out_specs.append(pl.BlockSpec(s.shape, lambda i: (0, 0)))
        else:
            out_specs.append(pl.BlockSpec((tr, s.shape[1]), lambda i: (i, 0)))
    return pl.pallas_call(
        body, name=name, grid=(rows // tr,), in_specs=in_specs, out_specs=out_specs, out_shape=outs_shape,
        compiler_params=_cparams("arbitrary" if acc_outs else "parallel"),
    )(*[a for a, _ in ins])


def _rms_pre(h, w, *, name):
    T, D = h.shape

    def body(h_ref, w_ref, u_ref):
        x = h_ref[...]
        u_ref[...] = (x * _rstd(x) * w_ref[...]).astype(BF16)

    return _row_call(body, [(h, True), (w, False)], [jax.ShapeDtypeStruct((T, D), BF16)], name=name, rows=T, tr=256)[0]


def _rms_post_pre(h, m, w_post, w_pre, *, name):
    T, D = h.shape

    def body(h_ref, m_ref, wp_ref, wn_ref, hn_ref, u_ref):
        mm = m_ref[...].astype(F32)
        hn = h_ref[...] + mm * _rstd(mm) * wp_ref[...]
        hn_ref[...] = hn
        u_ref[...] = (hn * _rstd(hn) * wn_ref[...]).astype(BF16)

    return _row_call(body, [(h, True), (m, True), (w_post, False), (w_pre, False)],
                     [jax.ShapeDtypeStruct((T, D), F32), jax.ShapeDtypeStruct((T, D), BF16)], name=name, rows=T, tr=256)


def _rms_post_loss(h, m, w_post, tgt, *, name):
    T, D = h.shape

    def body(h_ref, m_ref, wp_ref, t_ref, dh_ref, loss_ref):
        @pl.when(pl.program_id(0) == 0)
        def _():
            loss_ref[...] = jnp.zeros_like(loss_ref)

        mm = m_ref[...].astype(F32)
        err = h_ref[...] + mm * _rstd(mm) * wp_ref[...] - t_ref[...]
        dh_ref[...] = err * (1.0 / D)
        loss_ref[...] += _colsum(jnp.sum(err * err, axis=1, keepdims=True))

    return _row_call(body, [(h, True), (m, True), (w_post, False), (tgt, True)],
                     [jax.ShapeDtypeStruct((T, D), F32), jax.ShapeDtypeStruct((1, 1), F32)],
                     name=name, rows=T, tr=256, acc_outs=(1,))


def _rms_bwd_vals(x, w, dy):
    r = _rstd(x)
    xh = x * r
    g = dy * w
    dx = r * (g - xh * jnp.mean(g * xh, axis=-1, keepdims=True))
    return dx, _colsum(dy * xh)


def _rms_post_bwd(m, w_post, dh, *, name):
    T, D = m.shape

    def body(m_ref, w_ref, dh_ref, dm_ref, dw_ref):
        @pl.when(pl.program_id(0) == 0)
        def _():
            dw_ref[...] = jnp.zeros_like(dw_ref)

        dx, dw = _rms_bwd_vals(m_ref[...].astype(F32), w_ref[...], dh_ref[...])
        dm_ref[...] = dx.astype(BF16)
        dw_ref[...] += dw

    return _row_call(body, [(m, True), (w_post, False), (dh, True)],
                     [jax.ShapeDtypeStruct((T, D), BF16), jax.ShapeDtypeStruct((1, D), F32)],
                     name=name, rows=T, tr=256, acc_outs=(1,))


def _rms_pre_bwd(h, w_pre, du, dh_out, *, name):
    T, D = h.shape

    def body(h_ref, w_ref, du_ref, dho_ref, dh_ref, dw_ref):
        @pl.when(pl.program_id(0) == 0)
        def _():
            dw_ref[...] = jnp.zeros_like(dw_ref)

        dx, dw = _rms_bwd_vals(h_ref[...], w_ref[...], du_ref[...].astype(F32))
        dh_ref[...] = dho_ref[...] + dx
        dw_ref[...] += dw

    return _row_call(body, [(h, True), (w_pre, False), (du, True), (dh_out, True)],
                     [jax.ShapeDtypeStruct((T, D), F32), jax.ShapeDtypeStruct((1, D), F32)],
                     name=name, rows=T, tr=256, acc_outs=(1,))


def _rms_pre_post_bwd(h, w_pre, du, dh_out, m_prev, w_post_prev, *, name):
    T, D = h.shape

    def body(h_ref, w_ref, du_ref, dho_ref, m_ref, wp_ref, dh_ref, dm_ref, dw_ref, dwp_ref):
        @pl.when(pl.program_id(0) == 0)
        def _():
            dw_ref[...] = jnp.zeros_like(dw_ref)
            dwp_ref[...] = jnp.zeros_like(dwp_ref)

        dx, dw = _rms_bwd_vals(h_ref[...], w_ref[...], du_ref[...].astype(F32))
        dh = dho_ref[...] + dx
        dh_ref[...] = dh
        dw_ref[...] += dw
        dm, dwp = _rms_bwd_vals(m_ref[...].astype(F32), wp_ref[...], dh)
        dm_ref[...] = dm.astype(BF16)
        dwp_ref[...] += dwp

    return _row_call(body, [(h, True), (w_pre, False), (du, True), (dh_out, True), (m_prev, True), (w_post_prev, False)],
                     [jax.ShapeDtypeStruct((T, D), F32), jax.ShapeDtypeStruct((T, D), BF16),
                      jax.ShapeDtypeStruct((1, D), F32), jax.ShapeDtypeStruct((1, D), F32)],
                     name=name, rows=T, tr=256, acc_outs=(2, 3))


def _ret_consts(T, C, H):
    lg = np.log1p(-np.exp2(-5.0 - np.arange(H, dtype=np.float64)))
    idx = np.arange(C, dtype=np.float64)
    dist = np.abs(idx[:, None] - idx[None, :])
    vis = (idx[None, :] // REF_CHUNK) <= (idx[:, None] // REF_CHUNK)
    mask = np.exp(dist[None] * lg[:, None, None]) * vis[None]
    xi = np.exp((idx[None, :] + 1.0) * lg[:, None])[..., None]
    zeta = np.exp((C - 1.0 - idx)[None, :] * lg[:, None])[..., None]
    half = RET_DK // 2
    inv_freq = ROPE_BASE ** (-np.arange(half, dtype=np.float32) / np.float32(half))
    ang = np.arange(T, dtype=np.float32)[:, None] * inv_freq[None, :].astype(np.float32)
    return (jnp.asarray(mask, F32), jnp.asarray(xi, F32), jnp.asarray(zeta, F32),
            jnp.asarray(np.cos(ang), F32), jnp.asarray(np.sin(ang), F32))


def _rot(t, cos, sin):
    half = RET_DK // 2
    t1, t2 = t[:, :half], t[:, half:]
    return jnp.concatenate([t1 * cos - t2 * sin, t1 * sin + t2 * cos], axis=1)


def _unrot(d, cos, sin):
    half = RET_DK // 2
    d1, d2 = d[:, :half], d[:, half:]
    return jnp.concatenate([d1 * cos + d2 * sin, d2 * cos - d1 * sin], axis=1)


def _ret_specs(C, H, rev, NS):
    def ci(i):
        return NS - 1 - i if rev else i

    nq = H
    q_spec = pl.BlockSpec((C, RET_DK), lambda h, i: (ci(i), h))
    k_spec = pl.BlockSpec((C, RET_DK), lambda h, i: (ci(i), nq + h))
    v_spec = pl.BlockSpec((C, RET_DV), lambda h, i: (ci(i), H + h))
    g_spec = pl.BlockSpec((C, RET_DV), lambda h, i: (ci(i), 2 * H + h))
    cs_spec = pl.BlockSpec((C, RET_DK // 2), lambda h, i: (ci(i), 0))
    m_spec = pl.BlockSpec((None, C, C), lambda h, i: (h, 0, 0))
    vec_spec = pl.BlockSpec((None, C, 1), lambda h, i: (h, 0, 0))
    gn_spec = pl.BlockSpec((1, RET_DV), lambda h, i: (0, h))
    st_spec = pl.BlockSpec((None, None, RET_DK, RET_DV), lambda h, i: (h, ci(i), 0, 0))
    return q_spec, k_spec, v_spec, g_spec, cs_spec, m_spec, vec_spec, gn_spec, st_spec


def _ret_fwd_vals(q, k, v, cos, sin, mask, xi, s_in):
    qr = _rot(q, cos, sin)
    kr = _rot(k, cos, sin) * (RET_DK ** -0.5)
    a = _dot(qr, kr, NT) * mask
    o = _dot(a, v) + _dot(qr, s_in) * xi
    mu = jnp.mean(o, axis=1, keepdims=True)
    oc = o - mu
    rstd = lax.rsqrt(jnp.mean(oc * oc, axis=1, keepdims=True) + GN_EPS)
    return qr, kr, a, oc * rstd, rstd


def _ret_fwd(proj, gn_w, consts, *, C, name):
    T = proj.shape[0]
    H = gn_w.shape[1] // RET_DV
    NS = T // C
    mask, xi, zeta, cos, sin = consts
    q_spec, k_spec, v_spec, g_spec, cs_spec, m_spec, vec_spec, gn_spec, st_spec = _ret_specs(C, H, False, NS)
    y_spec = pl.BlockSpec((C, RET_DV), lambda h, i: (i, h))

    def body(q_ref, k_ref, v_ref, g_ref, cos_ref, sin_ref, m_ref, xi_ref, ze_ref, gn_ref, y_ref, st_ref, S):
        @pl.when(pl.program_id(1) == 0)
        def _():
            S[...] = jnp.zeros_like(S)

        s_in = S[...]
        st_ref[...] = s_in
        v = v_ref[...]
        xi_v = xi_ref[...]
        qr, kr, a, on, rstd = _ret_fwd_vals(q_ref[...], k_ref[...], v, cos_ref[...], sin_ref[...], m_ref[...], xi_v, s_in)
        g = g_ref[...]
        y_ref[...] = (g * _sigmoid(g) * on * gn_ref[...]).astype(BF16)
        S[...] = s_in * xi_v[C - 1:C, :] + _dot(kr * ze_ref[...], v, TN)

    return pl.pallas_call(
        body, name=name, grid=(H, NS),
        in_specs=[q_spec, k_spec, v_spec, g_spec, cs_spec, cs_spec, m_spec, vec_spec, vec_spec, gn_spec],
        out_specs=[y_spec, st_spec],
        out_shape=[jax.ShapeDtypeStruct((T, H * RET_DV), BF16), jax.ShapeDtypeStruct((H, NS, RET_DK, RET_DV), F32)],
        scratch_shapes=[pltpu.VMEM((RET_DK, RET_DV), F32)],
        compiler_params=_cparams("parallel", "arbitrary"),
    )(proj, proj, proj, proj, cos, sin, mask, xi, zeta, gn_w)


def _stage_out(out_hbm, stage, sems, step, n_steps, row0, pieces, values):
    C = stage.shape[1]
    slot = step % 2

    def copies(sl):
        return [pltpu.make_async_copy(stage.at[sl, :, pl.ds(c0, w)],
                                      out_hbm.at[pl.ds(pl.multiple_of(row0, 16), C), pl.ds(pl.multiple_of(dc, 128), w)],
                                      sems.at[sl, n]) for n, (c0, w, dc) in enumerate(pieces)]

    @pl.when(step >= 2)
    def _():
        for cp in copies(slot):
            cp.wait()

    for (c0, w, _), v in zip(pieces, values):
        stage[slot, :, c0:c0 + w] = v
    for cp in copies(slot):
        cp.start()

    @pl.when(step == n_steps - 1)
    def _():
        for cp in copies(slot):
            cp.wait()
        if n_steps >= 2:
            for cp in copies(1 - slot):
                cp.wait()


def _ret_bwd(proj, gn_w, consts, states, dy, *, C, name):
    T = proj.shape[0]
    H = gn_w.shape[1] // RET_DV
    NS = T // C
    mask, xi, zeta, cos, sin = consts
    q_spec, k_spec, v_spec, g_spec, cs_spec, m_spec, vec_spec, gn_spec, st_spec = _ret_specs(C, H, True, NS)
    dy_spec = pl.BlockSpec((C, RET_DV), lambda h, i: (NS - 1 - i, h))
    scale = RET_DK ** -0.5
    wq, wv = H * RET_DK, H * RET_DV

    def body(q_ref, k_ref, v_ref, g_ref, cos_ref, sin_ref, m_ref, xi_ref, ze_ref, gn_ref, st_ref, dy_ref,
             dproj_ref, dgn_ref, dS, stage, sems):
        @pl.when(pl.program_id(1) == 0)
        def _():
            dS[...] = jnp.zeros_like(dS)
            dgn_ref[...] = jnp.zeros_like(dgn_ref)

        s_in = st_ref[...]
        v = v_ref[...]
        cos, sin, mask, xi_v, ze = cos_ref[...], sin_ref[...], m_ref[...], xi_ref[...], ze_ref[...]
        qr, kr, a, on, rstd = _ret_fwd_vals(q_ref[...], k_ref[...], v, cos, sin, mask, xi_v, s_in)
        g = g_ref[...]
        sg = _sigmoid(g)
        silu = g * sg
        gnw = gn_ref[...]
        dy = dy_ref[...].astype(F32)
        dg = (dy * on * gnw * (sg * (1.0 + g * (1.0 - sg)))).astype(BF16)
        t = dy * silu
        dgn_ref[...] += _colsum(t * on)
        don = t * gnw
        do = rstd * (don - jnp.mean(don, axis=1, keepdims=True) - on * jnp.mean(don * on, axis=1, keepdims=True))
        dox = do * xi_v
        ds_out = dS[...]
        da = _dot(do, v, NT) * mask
        kz = kr * ze
        dv = (_dot(a, do, TN) + _dot(kz, ds_out)).astype(BF16)
        dqr = _dot(da, kr) + _dot(dox, s_in, NT)
        dkr = _dot(da, qr, TN) + _dot(v, ds_out, NT) * ze
        dS[...] = ds_out * xi_v[C - 1:C, :] + _dot(qr, dox, TN)
        dq = _unrot(dqr, cos, sin).astype(BF16)
        dk = _unrot(dkr * scale, cos, sin).astype(BF16)
        h, i = pl.program_id(0), pl.program_id(1)
        pieces = [(0, RET_DK, h * RET_DK), (RET_DK, RET_DK, wq + h * RET_DK),
                  (2 * RET_DK, RET_DV, 2 * wq + h * RET_DV), (2 * RET_DK + RET_DV, RET_DV, 2 * wq + wv + h * RET_DV)]
        _stage_out(dproj_ref, stage, sems, h * NS + i, H * NS, (NS - 1 - i) * C, pieces, [dq, dk, dv, dg])

    return pl.pallas_call(
        body, name=name, grid=(H, NS),
        in_specs=[q_spec, k_spec, v_spec, g_spec, cs_spec, cs_spec, m_spec, vec_spec, vec_spec, gn_spec, st_spec, dy_spec],
        out_specs=[_ANY, gn_spec],
        out_shape=[jax.ShapeDtypeStruct((T, 2 * wq + 2 * wv), BF16), jax.ShapeDtypeStruct((1, H * RET_DV), F32)],
        scratch_shapes=[pltpu.VMEM((RET_DK, RET_DV), F32), pltpu.VMEM((2, C, 2 * RET_DK + 2 * RET_DV), BF16),
                        pltpu.SemaphoreType.DMA((2, 4))],
        compiler_params=_cparams("arbitrary", "arbitrary"),
    )(proj, proj, proj, proj, cos, sin, mask, xi, zeta, gn_w, states, dy)


def _shift_down(x, prev8, k):
    if k == 0:
        return x
    y = pltpu.roll(x, k, 0)
    row = lax.broadcasted_iota(jnp.int32, prev8.shape, 0)
    top = jnp.where(row < k, pltpu.roll(prev8, k, 0), y[:8])
    return jnp.concatenate([top, y[8:]], axis=0)


def _shift_up(x, next8, k):
    if k == 0:
        return x
    n = x.shape[0]
    y = pltpu.roll(x, n - k, 0)
    row = lax.broadcasted_iota(jnp.int32, next8.shape, 0)
    bot = jnp.where(row >= 8 - k, pltpu.roll(next8, 8 - k, 0), y[n - 8:])
    return jnp.concatenate([y[:n - 8], bot], axis=0)


def _conv_silu(raw, halo, w, b):
    cv = b
    for tap in range(SSD_CONV_W):
        cv = cv + _shift_down(raw, halo, SSD_CONV_W - 1 - tap) * w[tap:tap + 1, :]
    sg = _sigmoid(cv)
    return cv * sg, cv, sg


def _conv_silu_bwd(d_post, cv, sg, raw, halo, w, carry8):
    dcv = d_post * (sg * (1.0 + cv * (1.0 - sg)))
    d_raw = jnp.zeros_like(raw)
    dws = []
    for tap in range(SSD_CONV_W):
        k = SSD_CONV_W - 1 - tap
        d_raw = d_raw + _shift_up(dcv, carry8, k) * w[tap:tap + 1, :]
        dws.append(_colsum(dcv * _shift_down(raw, halo, k)))
    return d_raw, jnp.concatenate(dws, axis=0), _colsum(dcv), dcv[:8]


def _softplus(x):
    return jnp.maximum(x, 0.0) + jnp.log1p(jnp.exp(-jnp.abs(x)))


def _ssd_common(C, R, dt, dtT, bias, biasT, alog, alogT, E):
    p = dt + bias
    dtv = _softplus(p)
    a = -jnp.exp(alog)
    da = dtv * a
    daT = _softplus(dtT + biasT) * (-jnp.exp(alogT))
    row = lax.broadcasted_iota(jnp.int32, (C, C), 0)
    col = lax.broadcasted_iota(jnp.int32, (C, C), 1)
    tril = row >= col
    trilf = jnp.where(tril, 1.0, 0.0).astype(F32)
    triuf = jnp.where(col >= row, 1.0, 0.0).astype(F32)
    acum = _dot_sel(trilf, da, split="b")
    acumT = _dot_sel(daT, trilf, NT, split="a")
    al = acum[C - 1:C, :]
    ea = jnp.exp(acum)
    dte = jnp.exp(al - acum)
    eal = jnp.exp(al)
    return dict(p=p, dtv=dtv, a=a, da=da, tril=tril, triuf=triuf, acum=acum, acumT=acumT, al=al, ea=ea, dte=dte, eal=eal,
                dtv_e=_dot_sel(dtv, E, split="a", terms=2), ea_e=_dot_sel(ea, E, split="a", terms=2),
                dte_e=_dot_sel(dte, E, split="a", terms=2), eal_e=_dot_sel(eal, E, split="a"))


def _head_decay(q, r, C, R):
    seg = jnp.broadcast_to(q["acum"][:, r:r + 1], (C, C)) - q["acumT"][r:r + 1, :]
    return jnp.exp(jnp.where(q["tril"], seg, -1e30))


def _ssd_group_specs(C, R, NS, rev):
    RP = R * SSD_P
    G = SSD_G
    hb = C // 8

    def ci(i):
        return NS - 1 - i if rev else i

    def halo_row(i):
        return jnp.maximum(ci(i) * hb - 1, 0)

    off_b = G * RP // SSD_N
    z_spec = pl.BlockSpec((C, RP), lambda g, i: (ci(i), g))
    x_spec = pl.BlockSpec((C, RP), lambda g, i: (ci(i), G + g))
    b_spec = pl.BlockSpec((C, SSD_N), lambda g, i: (ci(i), 2 * off_b + g))
    c_spec = pl.BlockSpec((C, SSD_N), lambda g, i: (ci(i), 2 * off_b + G + g))
    xh_spec = pl.BlockSpec((8, RP), lambda g, i: (halo_row(i), G + g))
    bh_spec = pl.BlockSpec((8, SSD_N), lambda g, i: (halo_row(i), 2 * off_b + g))
    ch_spec = pl.BlockSpec((8, SSD_N), lambda g, i: (halo_row(i), 2 * off_b + G + g))
    dt_spec = pl.BlockSpec((None, C, R), lambda g, i: (g, ci(i), 0))
    dtT_spec = pl.BlockSpec((None, R, C), lambda g, i: (g, 0, ci(i)))
    pr_spec = pl.BlockSpec((None, 1, R), lambda g, i: (g, 0, 0))
    prT_spec = pl.BlockSpec((None, R, 1), lambda g, i: (g, 0, 0))
    cwx_spec = pl.BlockSpec((SSD_CONV_W, RP), lambda g, i: (0, g))
    cwn_spec = pl.BlockSpec((SSD_CONV_W, SSD_N), lambda g, i: (0, g))
    cbx_spec = pl.BlockSpec((1, RP), lambda g, i: (0, g))
    cbn_spec = pl.BlockSpec((1, SSD_N), lambda g, i: (0, g))
    e_spec = pl.BlockSpec((R, RP), lambda g, i: (0, 0))
    st_spec = pl.BlockSpec((None, None, SSD_N, RP), lambda g, i: (g, ci(i), 0, 0))
    return dict(z=z_spec, x=x_spec, b=b_spec, c=c_spec, xh=xh_spec, bh=bh_spec, ch=ch_spec, dt=dt_spec, dtT=dtT_spec,
                pr=pr_spec, prT=prT_spec, cwx=cwx_spec, cwn=cwn_spec, cbx=cbx_spec, cbn=cbn_spec, e=e_spec, st=st_spec)


def _ssd_forward_vals(C, R, refs, first, s_in):
    E = refs["E"]
    halo_on = jnp.where(first, 0.0, 1.0)
    xh, bh, ch = refs["xh"] * halo_on, refs["bh"] * halo_on, refs["ch"] * halo_on
    xs, cvx, sgx = _conv_silu(refs["x"], xh, refs["cwx"], refs["cbx"])
    bm, cvb, sgb = _conv_silu(refs["b"], bh, refs["cwb"], refs["cbb"])
    cm, cvc, sgc = _conv_silu(refs["c"], ch, refs["cwc"], refs["cbc"])
    q = _ssd_common(C, R, refs["dt"], refs["dtT"], refs["bias"], refs["biasT"], refs["alog"], refs["alogT"], E)
    xdt = xs * q["dtv_e"]
    cb = _dot(cm, bm, NT)
    yoff_raw = _dot(cm, s_in)
    xdt_b = xdt.astype(BF16)
    low = lax.broadcasted_iota(jnp.int32, (1, 2 * SSD_P), 1) < SSD_P
    pairs = []
    for j in range(R // 2):
        xp = xdt_b[:, 2 * SSD_P * j:2 * SSD_P * (j + 1)]
        y0 = _dot(cb * _head_decay(q, 2 * j, C, R), xp)
        y1 = _dot(cb * _head_decay(q, 2 * j + 1, C, R), xp)
        pairs.append(jnp.where(low, y0, y1))
    ydiag = jnp.concatenate(pairs, axis=1)
    d_e =_dot_sel(refs["dskip"], E, split="a")
    y = ydiag + yoff_raw * q["ea_e"] + d_e * xs
    xd = xdt * q["dte_e"]
    s_out = s_in * q["eal_e"] + _dot(bm, xd, TN)
    z = refs["z"]
    sgz = _sigmoid(z)
    yz = y * (z * sgz)
    rn = lax.rsqrt(jnp.mean(yz * yz, axis=1, keepdims=True) + RMS_EPS)
    return dict(q=q, xh=xh, bh=bh, ch=ch, xs=xs, cvx=cvx, sgx=sgx, bm=bm, cvb=cvb, sgb=sgb, cm=cm, cvc=cvc, sgc=sgc,
                xdt=xdt, cb=cb, yoff_raw=yoff_raw, d_e=d_e, y=y, xd=xd, s_out=s_out, z=z, sgz=sgz, yz=yz, rn=rn)


_SSD_IN_NAMES = ("z", "x", "b", "c", "xh", "bh", "ch", "dt", "dtT", "bias", "biasT", "alog", "alogT", "dskip",
                 "cwx", "cwb", "cwc", "cbx", "cbb", "cbc", "nw", "E")


def _ssd_inputs(pm, dt_g, dtT_g, prm, sp):
    bias, biasT, alog, alogT, dskip, cwx, cwb, cwc, cbx, cbb, cbc, nw, E = prm
    args = [pm, pm, pm, pm, pm, pm, pm, dt_g, dtT_g, bias, biasT, alog, alogT, dskip, cwx, cwb, cwc, cbx, cbb, cbc, nw, E]
    specs = [sp["z"], sp["x"], sp["b"], sp["c"], sp["xh"], sp["bh"], sp["ch"], sp["dt"], sp["dtT"], sp["pr"], sp["prT"],
             sp["pr"], sp["prT"], sp["pr"], sp["cwx"], sp["cwn"], sp["cwn"], sp["cbx"], sp["cbn"], sp["cbn"], sp["cbx"], sp["e"]]
    return args, specs


def _ssd_fwd(pm, dt_g, dtT_g, prm, *, C, R, name):
    T = pm.shape[0]
    NS = T // C
    RP = R * SSD_P
    G = SSD_G
    sp = _ssd_group_specs(C, R, NS, False)
    args, specs = _ssd_inputs(pm, dt_g, dtT_g, prm, sp)
    nin = len(args)

    def body(*refs):
        ins = {n: r[...] for n, r in zip(_SSD_IN_NAMES, refs[:nin])}
        y_ref, st_ref, S = refs[nin:]
        first = pl.program_id(1) == 0

        @pl.when(first)
        def _():
            S[...] = jnp.zeros_like(S)

        s_in = S[...]
        st_ref[...] = s_in
        f = _ssd_forward_vals(C, R, ins, first, s_in)
        y_ref[...] = (f["yz"] * f["rn"] * ins["nw"]).astype(BF16)
        S[...] = f["s_out"]

    return pl.pallas_call(
        body, name=name, grid=(G, NS), in_specs=specs,
        out_specs=[pl.BlockSpec((C, RP), lambda g, i: (i, g)), sp["st"]],
        out_shape=[jax.ShapeDtypeStruct((T, G * RP), BF16), jax.ShapeDtypeStruct((G, NS, SSD_N, RP), F32)],
        scratch_shapes=[pltpu.VMEM((SSD_N, RP), F32)],
        compiler_params=_cparams("parallel", "arbitrary"),
    )(*args)


def _ssd_bwd(pm, dt_g, dtT_g, prm, states, dout, *, C, R, name):
    T = pm.shape[0]
    NS = T // C
    RP = R * SSD_P
    G = SSD_G
    sp = _ssd_group_specs(C, R, NS, True)
    args, specs = _ssd_inputs(pm, dt_g, dtT_g, prm, sp)
    nin = len(args)
    rows_spec = pl.BlockSpec((C, RP), lambda g, i: (NS - 1 - i, g))
    args = args + [states, dout]
    specs = specs + [sp["st"], rows_spec]

    def body(*refs):
        ins = {n: r[...] for n, r in zip(_SSD_IN_NAMES, refs[:nin])}
        st_ref, dout_ref = refs[nin], refs[nin + 1]
        (dpm_ref, ddt_ref, dbias_ref, dalog_ref, dd_ref, dcwx_ref, dcwb_ref, dcwc_ref,
         dcbx_ref, dcbb_ref, dcbc_ref, dnw_ref) = refs[nin + 2:nin + 14]
        dS, cx8, cb8, cc8, stage, sems = refs[nin + 14:]
        acc_refs = (dbias_ref, dalog_ref, dd_ref, dcwx_ref, dcwb_ref, dcwc_ref, dcbx_ref, dcbb_ref, dcbc_ref, dnw_ref)
        step = pl.program_id(1)

        @pl.when(step == 0)
        def _():
            for r_ in acc_refs + (dS, cx8, cb8, cc8):
                r_[...] = jnp.zeros_like(r_)

        first = step == NS - 1
        E = ins["E"]
        s_in = st_ref[...]
        f = _ssd_forward_vals(C, R, ins, first, s_in)
        q = f["q"]
        xs, bm, cm, xdt, cb, y, z, sgz, yz, rn = (f[n] for n in ("xs", "bm", "cm", "xdt", "cb", "y", "z", "sgz", "yz", "rn"))
        nw = ins["nw"]
        dout = dout_ref[...].astype(F32)
        yh = yz * rn
        dnw_ref[...] += _colsum(dout * yh)
        g1 = dout * nw
        dyz = rn * (g1 - yh * jnp.mean(g1 * yh, axis=1, keepdims=True))
        dz = (dyz * y * (sgz * (1.0 + z * (1.0 - sgz)))).astype(BF16)
        dy = dyz * (z * sgz)
        dd_ref[...] += _dot_sel(_colsum(dy * xs), E, NT, split="a")
        dxs = dy * f["d_e"]
        dyo = dy * q["ea_e"]
        dcm = _dot(dyo, s_in, NT)
        ds_acc = _dot(cm, dyo, TN)
        dacum = _dot_sel(dy * f["yoff_raw"], E, NT, split="a", terms=1) * q["ea"]
        dacumT = jnp.zeros((R, C), F32)
        dcb = jnp.zeros((C, C), F32)
        rowR = lax.broadcasted_iota(jnp.int32, (1, R), 1)
        rowRT = lax.broadcasted_iota(jnp.int32, (R, 1), 0)
        dy_b, xdt_b = dy.astype(BF16), xdt.astype(BF16)
        low = lax.broadcasted_iota(jnp.int32, (1, 2 * SSD_P), 1) < SSD_P
        dxdt_pairs = []
        for j in range(R // 2):
            lanes = slice(2 * SSD_P * j, 2 * SSD_P * (j + 1))
            dyp, xp = dy_b[:, lanes], xdt_b[:, lanes]
            halves = []
            for r, mine in ((2 * j, low), (2 * j + 1, jnp.logical_not(low))):
                lr = _head_decay(q, r, C, R)
                w_r = cb * lr
                dw = _dot(jnp.where(mine, dyp, jnp.zeros_like(dyp)), xp, NT)
                halves.append(_dot(w_r, dyp, TN))
                dcb = dcb + dw * lr
                dseg = dw * w_r
                dacum = dacum + jnp.sum(dseg, axis=1, keepdims=True) * jnp.where(rowR == r, 1.0, 0.0)
                dacumT = dacumT - _colsum(dseg) * jnp.where(rowRT == r, 1.0, 0.0)
            dxdt_pairs.append(jnp.where(low, halves[0], halves[1]))
        dxdt = jnp.concatenate(dxdt_pairs, axis=1)
        dsn = dS[...]
        ds_acc = ds_acc + dsn * q["eal_e"]
        d_eal = _dot_sel(_colsum(dsn * s_in), E, NT, split="a")
        dbm = _dot(f["xd"], dsn, NT)
        dxd = _dot(bm, dsn)
        dxdt = dxdt + dxd * q["dte_e"]
        d_dte = _dot_sel(dxd * xdt, E, NT, split="a", terms=1) * q["dte"]
        d_al = _colsum(d_dte) + d_eal * q["eal"]
        dacum = dacum - d_dte
        rowC = lax.broadcasted_iota(jnp.int32, (C, 1), 0)
        dacum = dacum + jnp.where(rowC == C - 1, 1.0, 0.0) * d_al
        dS[...] = ds_acc
        dcm = dcm + _dot(dcb, bm)
        dbm = dbm + _dot(dcb, cm, TN)
        eye = jnp.where(lax.broadcasted_iota(jnp.int32, (C, C), 0) == lax.broadcasted_iota(jnp.int32, (C, C), 1), 1.0, 0.0)
        dacum = dacum + _dot_sel(eye, dacumT, NT, split="b")
        dda = _dot_sel(q["triuf"], dacum, split="b")
        ddtv = dda * q["a"] + _dot_sel(dxdt * xs, E, NT, split="a", terms=1)
        dalog_ref[...] += _colsum(dda * q["dtv"]) * q["a"]
        dxs = dxs + dxdt * q["dtv_e"]
        dp = ddtv * _sigmoid(q["p"])
        ddt_ref[...] = dp
        dbias_ref[...] += _colsum(dp)
        d_raw, d_w, d_b, c8 = _conv_silu_bwd(dxs, f["cvx"], f["sgx"], ins["x"], f["xh"], ins["cwx"], cx8[...])
        dx = d_raw.astype(BF16)
        dcwx_ref[...] += d_w
        dcbx_ref[...] += d_b
        cx8[...] = c8
        d_raw, d_w, d_b, c8 = _conv_silu_bwd(dbm, f["cvb"], f["sgb"], ins["b"], f["bh"], ins["cwb"], cb8[...])
        db = d_raw.astype(BF16)
        dcwb_ref[...] += d_w
        dcbb_ref[...] += d_b
        cb8[...] = c8
        d_raw, d_w, d_b, c8 = _conv_silu_bwd(dcm, f["cvc"], f["sgc"], ins["c"], f["ch"], ins["cwc"], cc8[...])
        dc = d_raw.astype(BF16)
        dcwc_ref[...] += d_w
        dcbc_ref[...] += d_b
        cc8[...] = c8
        g_ = pl.program_id(0)
        pieces = [(0, RP, g_ * RP), (RP, RP, G * RP + g_ * RP), (2 * RP, SSD_N, 2 * G * RP + g_ * SSD_N),
                  (2 * RP + SSD_N, SSD_N, 2 * G * RP + G * SSD_N + g_ * SSD_N)]
        _stage_out(dpm_ref, stage, sems, g_ * NS + step, G * NS, (NS - 1 - step) * C, pieces, [dz, dx, db, dc])

    out_specs = [_ANY, pl.BlockSpec((None, C, R), lambda g, i: (g, NS - 1 - i, 0)),
                 sp["pr"], sp["pr"], sp["pr"], sp["cwx"], sp["cwn"], sp["cwn"], sp["cbx"], sp["cbn"], sp["cbn"], sp["cbx"]]
    out_shape = [jax.ShapeDtypeStruct((T, 2 * G * RP + 2 * G * SSD_N), BF16),
                 jax.ShapeDtypeStruct((G, T, R), F32),
                 jax.ShapeDtypeStruct((G, 1, R), F32), jax.ShapeDtypeStruct((G, 1, R), F32), jax.ShapeDtypeStruct((G, 1, R), F32),
                 jax.ShapeDtypeStruct((SSD_CONV_W, G * RP), F32), jax.ShapeDtypeStruct((SSD_CONV_W, G * SSD_N), F32),
                 jax.ShapeDtypeStruct((SSD_CONV_W, G * SSD_N), F32),
                 jax.ShapeDtypeStruct((1, G * RP), F32), jax.ShapeDtypeStruct((1, G * SSD_N), F32),
                 jax.ShapeDtypeStruct((1, G * SSD_N), F32), jax.ShapeDtypeStruct((1, G * RP), F32)]
    return pl.pallas_call(
        body, name=name, grid=(G, NS), in_specs=specs, out_specs=out_specs, out_shape=out_shape,
        scratch_shapes=[pltpu.VMEM((SSD_N, RP), F32), pltpu.VMEM((8, RP), F32), pltpu.VMEM((8, SSD_N), F32),
                        pltpu.VMEM((8, SSD_N), F32), pltpu.VMEM((2, C, 2 * RP + 2 * SSD_N), BF16),
                        pltpu.SemaphoreType.DMA((2, 4))],
        compiler_params=_cparams("arbitrary", "arbitrary"),
    )(*args)


_ANY = pl.BlockSpec(memory_space=pl.ANY)


def _chip_peer(k):
    x, y, c = lax.axis_index("x"), lax.axis_index("y"), lax.axis_index("c")
    return (x ^ (k >> 1), y ^ (k & 1), c)


def _my_chip():
    return 2 * lax.axis_index("x") + lax.axis_index("y")


def _all_gather_chips(shards, halved, *, name):
    n = len(shards)

    def body(*refs):
        ins, outs = refs[:n], refs[n:2 * n]
        send, recv, fsend, frecv, loc = refs[2 * n:]
        s = _my_chip()
        c = lax.axis_index("c")
        sibling = (lax.axis_index("x"), lax.axis_index("y"), 1 - c)
        copies = []
        for a in range(n):
            cp = pltpu.make_async_copy(ins[a], outs[a].at[s], loc.at[a])
            cp.start()
            copies.append(cp)

        def rows(a, core):
            if not halved[a]:
                return slice(None)
            half = shards[a].shape[0] // 2
            return pl.ds(pl.multiple_of(core * half, 16), half)

        def over_ici(a, k, slot, core):
            return pltpu.make_async_remote_copy(
                src_ref=ins[a].at[rows(a, core)], dst_ref=outs[a].at[slot, rows(a, core)],
                send_sem=send.at[3 * a + k - 1], recv_sem=recv.at[3 * a + k - 1],
                device_id=_chip_peer(k), device_id_type=MESH_ID)

        def over_d2d(a, k, core):
            z = outs[a].at[s ^ k, rows(a, core)]
            return pltpu.make_async_remote_copy(
                src_ref=z, dst_ref=z, send_sem=fsend.at[3 * a + k - 1], recv_sem=frecv.at[3 * a + k - 1],
                device_id=sibling, device_id_type=MESH_ID)

        sent = []
        for a in range(n):
            for k in (1, 2, 3):
                cp = over_ici(a, k, s, c)
                cp.start()
                sent.append(cp)
        passed = []
        for a in range(n):
            for k in (1, 2, 3):
                over_ici(a, k, s ^ k, c).wait_recv()
                if halved[a]:
                    cp = over_d2d(a, k, c)
                    cp.start()
                    passed.append(cp)
        for a in range(n):
            if halved[a]:
                for k in (1, 2, 3):
                    over_d2d(a, k, 1 - c).wait_recv()
        for cp in sent + passed:
            cp.wait_send()
        for cp in copies:
            cp.wait()

    for a, h in zip(shards, halved):
        assert not h or a.shape[0] % 32 == 0, a.shape
    return pl.pallas_call(
        body, name=name, in_specs=[_ANY] * n, out_specs=[_ANY] * n,
        out_shape=[jax.ShapeDtypeStruct((4,) + a.shape, a.dtype) for a in shards],
        scratch_shapes=[pltpu.SemaphoreType.DMA((3 * n,))] * 4 + [pltpu.SemaphoreType.DMA((n,))],
        compiler_params=pltpu.CompilerParams(has_side_effects=True),
    )(*shards)


_HBM = pl.BlockSpec(memory_space=pltpu.HBM)
_SEM = pl.BlockSpec(memory_space=pltpu.SEMAPHORE)
_EFFECT = pltpu.SideEffectType.DATAFLOW_SIDE_EFFECTING


def _split_copies(src, land, send, recv, loc, a, scatter):
    s = _my_chip()
    mine = pltpu.make_async_copy(src.at[s] if scatter else src, land.at[s], loc.at[a])
    pairs = []
    for k in (1, 2, 3):
        sems = dict(send_sem=send.at[3 * a + k - 1], recv_sem=recv.at[3 * a + k - 1],
                    device_id=_chip_peer(k), device_id_type=MESH_ID)
        out = pltpu.make_async_remote_copy(src_ref=src.at[s ^ k] if scatter else src, dst_ref=land.at[s], **sems)
        arriving = pltpu.make_async_remote_copy(src_ref=src.at[s ^ k] if scatter else src, dst_ref=land.at[s ^ k], **sems)
        pairs.append((out, arriving))
    return mine, pairs


def _split_start(arrs, *, scatter, after, name):
    n = len(arrs)
    zones = [lax.empty(a.shape if scatter else (4,) + a.shape, a.dtype) for a in arrs]

    def body(*refs):
        srcs, lands = refs[:n], refs[n:2 * n]
        send, recv, loc = refs[2 * n + 1:2 * n + 4]
        token = refs[-1]
        for a in range(n):
            mine, pairs = _split_copies(srcs[a], lands[a], send, recv, loc, a, scatter)
            mine.start()
            for out, _ in pairs:
                out.start()
        token[...] = jnp.zeros_like(token)

    res = pl.pallas_call(
        body, name=name,
        out_shape=(pltpu.SemaphoreType.DMA((3 * n,)), pltpu.SemaphoreType.DMA((3 * n,)), pltpu.SemaphoreType.DMA((n,)),
                   *[pltpu.HBM(z.shape, z.dtype) for z in zones], jax.ShapeDtypeStruct((8, 128), F32)),
        in_specs=[_ANY] * n + [_HBM] * n + [_ANY],
        out_specs=(_SEM, _SEM, _SEM, *([_HBM] * n), pl.BlockSpec(memory_space=pltpu.VMEM)),
        input_output_aliases={n + i: 3 + i for i in range(n)},
        compiler_params=pltpu.CompilerParams(has_side_effects=_EFFECT),
    )(*arrs, *[pltpu.with_memory_space_constraint(z, pltpu.HBM) for z in zones], after)
    return res[:3], list(res[3:3 + n]), res[-1]


def _split_wait(sems, src, land, a, *, scatter, after, name):
    def body(src_ref, land_ref, send, recv, loc, after_ref, land_out):
        mine, pairs = _split_copies(src_ref, land_ref, send, recv, loc, a, scatter)
        mine.wait()
        for out, arriving in pairs:
            out.wait_send()
            arriving.wait_recv()

    return pl.pallas_call(
        body, name=name, out_shape=pltpu.HBM(land.shape, land.dtype),
        in_specs=[_ANY, _HBM, _SEM, _SEM, _SEM, _ANY], out_specs=_HBM, input_output_aliases={1: 0},
        compiler_params=pltpu.CompilerParams(has_side_effects=_EFFECT),
    )(src, land, *sems, after)


def _sibling_copies(srcs, lands, send, recv):
    sib = (lax.axis_index("x"), lax.axis_index("y"), 1 - lax.axis_index("c"))
    return [pltpu.make_async_remote_copy(src_ref=srcs[a], dst_ref=lands[a], send_sem=send.at[a], recv_sem=recv.at[a],
                                         device_id=sib, device_id_type=MESH_ID) for a in range(len(srcs))]


def _swap_start(arrs, *, after, name):
    n = len(arrs)
    zones = [lax.empty(a.shape, a.dtype) for a in arrs]

    def body(*refs):
        for cp in _sibling_copies(refs[:n], refs[n:2 * n], refs[2 * n + 1], refs[2 * n + 2]):
            cp.start()

    res = pl.pallas_call(
        body, name=name,
        out_shape=(pltpu.SemaphoreType.DMA((n,)), pltpu.SemaphoreType.DMA((n,)), *[pltpu.HBM(z.shape, z.dtype) for z in zones]),
        in_specs=[_ANY] * n + [_HBM] * n + [_ANY], out_specs=(_SEM, _SEM, *([_HBM] * n)),
        input_output_aliases={n + i: 2 + i for i in range(n)},
        compiler_params=pltpu.CompilerParams(has_side_effects=_EFFECT),
    )(*arrs, *[pltpu.with_memory_space_constraint(z, pltpu.HBM) for z in zones], after)
    return res[:2], list(res[2:])


def _swap_wait(sems, srcs, lands, *, after, name):
    n = len(srcs)

    def body(*refs):
        for cp in _sibling_copies(refs[:n], refs[n:2 * n], refs[2 * n], refs[2 * n + 1]):
            cp.wait_send()
            cp.wait_recv()

    res = pl.pallas_call(
        body, name=name, out_shape=tuple(pltpu.HBM(a.shape, a.dtype) for a in lands),
        in_specs=[_ANY] * n + [_HBM] * n + [_SEM, _SEM, _ANY], out_specs=tuple([_HBM] * n),
        input_output_aliases={n + i: i for i in range(n)},
        compiler_params=pltpu.CompilerParams(has_side_effects=_EFFECT),
    )(*srcs, *lands, *sems, after)
    return list(res)


def _all_gather_devices(v, *, name):
    r = v.shape[0]

    def body(v_ref, out_ref, send, recv):
        x, y, c = lax.axis_index("x"), lax.axis_index("y"), lax.axis_index("c")
        me = 4 * x + 2 * y + c
        out_ref[me] = v_ref[...]
        cps = []
        for k in range(1, 8):
            peer = (x ^ (k >> 2), y ^ ((k >> 1) & 1), c ^ (k & 1))
            cp = pltpu.make_async_remote_copy(src_ref=v_ref, dst_ref=out_ref.at[me], send_sem=send.at[k - 1],
                                              recv_sem=recv.at[k - 1], device_id=peer, device_id_type=MESH_ID)
            cp.start()
            cps.append(cp)
        for k, cp in enumerate(cps, start=1):
            cp.wait_send()
            pltpu.make_async_remote_copy(src_ref=v_ref, dst_ref=out_ref.at[me ^ k], send_sem=send.at[k - 1],
                                         recv_sem=recv.at[k - 1], device_id=(x, y, c), device_id_type=MESH_ID).wait_recv()

    vm = pl.BlockSpec(memory_space=pltpu.VMEM)
    return pl.pallas_call(
        body, name=name, in_specs=[vm], out_specs=vm, out_shape=jax.ShapeDtypeStruct((8, r, 128), F32),
        scratch_shapes=[pltpu.SemaphoreType.DMA((7,)), pltpu.SemaphoreType.DMA((7,))],
        compiler_params=pltpu.CompilerParams(has_side_effects=True),
    )(v)


def _row_tile(r, target):
    best = None
    for t in range(16, min(target, r) + 1, 16):
        if r % t == 0:
            best = t
    return best or r


def _sum_slots(buf, *, name, tr=384):
    S, r, c = buf.shape
    tr = _row_tile(r, tr)

    def body(b_ref, o_ref):
        acc = b_ref[0].astype(F32)
        for j in range(1, S):
            acc = acc + b_ref[j].astype(F32)
        o_ref[...] = acc

    return pl.pallas_call(
        body, name=name, grid=(r // tr,), in_specs=[pl.BlockSpec((S, tr, c), lambda i: (0, i, 0))],
        out_specs=pl.BlockSpec((tr, c), lambda i: (i, 0)), out_shape=jax.ShapeDtypeStruct((r, c), F32),
        compiler_params=_cparams("parallel"),
    )(buf)


ADAMW_BLOCK_ELEMS = 1 << 18


def _adamw(w, gs, m, v, *, name, tr=256, layer=None, stack=None):
    r, c = w.shape[-2:]
    tr = _row_tile(r, min(tr, max(16, ADAMW_BLOCK_ELEMS // c)))
    bc1 = 1.0 - ADAM_B1 ** ADAM_STEP
    bc2 = 1.0 - ADAM_B2 ** ADAM_STEP
    ng = len(gs)

    def body(*refs):
        w_ref, m_ref, v_ref = refs[0], refs[1 + ng], refs[2 + ng]
        g_ref, d_ref, mo_ref, vo_ref = refs[-4:]
        gg = refs[1][...] if ng == 1 else refs[1][...] + refs[2][...]
        mn = ADAM_B1 * m_ref[...] + (1.0 - ADAM_B1) * gg
        vn = ADAM_B2 * v_ref[...] + (1.0 - ADAM_B2) * (gg * gg)
        g_ref[...] = gg
        mo_ref[...] = mn
        vo_ref[...] = vn
        d_ref[...] = -ADAM_LR * ((mn / bc1) / (jnp.sqrt(vn / bc2) + ADAM_EPS) + ADAM_WD * w_ref[...])

    spec = pl.BlockSpec((tr, c), lambda i: (i, 0))
    if layer is None:
        wspec, shape = spec, (r, c)
    else:
        wspec, shape = pl.BlockSpec((None, tr, c), lambda i: (layer, i, 0)), (2, r, c)
    in_specs = [wspec] + [spec] * ng + [wspec, wspec]
    args, alias = [w, *gs, m, v], {}
    if stack is not None:
        in_specs += [_ANY] * 4
        alias = {len(args) + n: n for n in range(4)}
        args += list(stack)
    return pl.pallas_call(body, name=name, grid=(r // tr,), in_specs=in_specs, out_specs=[wspec] * 4,
                          out_shape=[jax.ShapeDtypeStruct(shape, F32)] * 4, input_output_aliases=alias,
                          compiler_params=_cparams("parallel"))(*args)


def _pack(vecs, rows):
    flat = jnp.concatenate([v.reshape(-1).astype(F32) for v in vecs])
    return jnp.pad(flat, (0, rows * 128 - flat.shape[0])).reshape(rows, 128)


def _unpack(packed, shapes):
    flat = packed.reshape(-1)
    out, off = [], 0
    for s in shapes:
        n = math.prod(s)
        out.append(flat[off:off + n].reshape(s))
        off += n
    return out


def _pack_rows(shapes):
    n = sum(math.prod(s) for s in shapes)
    return -(-n // 1024) * 8


def kernel(x, norm_mix_pre, norm_mix_post, norm_ffn_pre, norm_ffn_post, ret_w_in, ret_gn_w, ret_w_out, ssd_w_in, ssd_conv_w, ssd_conv_b, ssd_dt_bias, ssd_a_log, ssd_d, ssd_norm_w, ssd_w_out, mlp_w_up, mlp_w_down, loss_target, m_norm_mix_pre, m_norm_mix_post, m_norm_ffn_pre, m_norm_ffn_post, m_ret_w_in, m_ret_gn_w, m_ret_w_out, m_ssd_w_in, m_ssd_conv_w, m_ssd_conv_b, m_ssd_dt_bias, m_ssd_a_log, m_ssd_d, m_ssd_norm_w, m_ssd_w_out, m_mlp_w_up, m_mlp_w_down, v_norm_mix_pre, v_norm_mix_post, v_norm_ffn_pre, v_norm_ffn_post, v_ret_w_in, v_ret_gn_w, v_ret_w_out, v_ssd_w_in, v_ssd_conv_w, v_ssd_conv_b, v_ssd_dt_bias, v_ssd_a_log, v_ssd_d, v_ssd_norm_w, v_ssd_w_out, v_mlp_w_up, v_mlp_w_down):
    T, D = x.shape[1], x.shape[2]
    H = D // RET_DK
    d_inner = 2 * D
    R = d_inner // SSD_P // SSD_G
    RP = R * SSD_P
    n_heads = SSD_G * R
    conv_dim = d_inner + 2 * SSD_G * SSD_N
    n_main = d_inner + conv_dim
    C = min(256, T)
    chip = _my_chip()
    xs, tgt = x[0], loss_target[0]

    conv_sh = ssd_conv_w.shape[2]
    small_shapes = [(SSD_CONV_W, conv_sh), (conv_sh,), (ssd_norm_w.shape[1],)]
    small_rows = _pack_rows(small_shapes)
    shards = [ret_w_in[0].T.astype(BF16), ret_w_out[0].astype(BF16), ssd_w_in[0].T.astype(BF16), ssd_w_out[0].astype(BF16),
              mlp_w_up[0].T.astype(BF16), mlp_w_up[1].T.astype(BF16), mlp_w_down[0].astype(BF16), mlp_w_down[1].astype(BF16)]
    (ret_in_g, small_g) = _all_gather_chips([shards[0], _pack([ssd_conv_w[0], ssd_conv_b[0], ssd_norm_w[0]], small_rows)],
                                            [True, False], name="gather_first")

    def full(g):
        return g.reshape(4 * g.shape[1], g.shape[2])

    def start_gather(idx, after, name):
        sems, zones, tok = _split_start([shards[i] for i in idx], scatter=False, after=after, name=name)
        return {i: (sems, shards[i], zones[n], n) for n, i in enumerate(idx)}, tok

    def arrived(stage, i, after, name):
        sems, src, zone, n = stage[i]
        return full(_split_wait(sems, src, zone, n, scatter=False, after=after, name=name))

    ret_in_t = full(ret_in_g)
    sm = [_unpack(small_g[j], small_shapes) for j in range(4)]
    conv_w = jnp.concatenate([sm[j][0] for j in range(4)], axis=1)
    conv_b = jnp.concatenate([sm[j][1] for j in range(4)])[None, :]
    norm_w = jnp.concatenate([sm[j][2] for j in range(4)])[None, :]

    gb = SSD_G * SSD_N
    ssd_prm = (ssd_dt_bias.reshape(SSD_G, 1, R), ssd_dt_bias.reshape(SSD_G, R, 1),
               ssd_a_log.reshape(SSD_G, 1, R), ssd_a_log.reshape(SSD_G, R, 1), ssd_d.reshape(SSD_G, 1, R),
               conv_w[:, :d_inner], conv_w[:, d_inner:d_inner + gb], conv_w[:, d_inner + gb:],
               conv_b[:, :d_inner], conv_b[:, d_inner:d_inner + gb], conv_b[:, d_inner + gb:],
               norm_w, jnp.asarray(np.kron(np.eye(R), np.ones((1, SSD_P))), F32))
    ret_consts = _ret_consts(T, C, H)

    u0 = _rms_pre(xs, norm_mix_pre[0:1], name="pre0")
    stage1, tok = start_gather((1, 4), ret_in_g, "gather_start1")
    proj = _matmul(u0, ret_in_t, "nt", out_dtype=F32, name="ret_in", after=tok)
    stage2, tok = start_gather((6, 2), proj, "gather_start2")
    y_ret, st_ret = _ret_fwd(proj, ret_gn_w, ret_consts, C=C, name="ret_fwd")
    ret_out = arrived(stage1, 1, y_ret, "gather_wait_ret_out")
    m0 = _matmul(y_ret, ret_out, "nn", out_dtype=BF16, name="ret_out", after=tok)
    h1, u1 = _rms_post_pre(xs, m0, norm_mix_post[0:1], norm_ffn_pre[0:1], name="post_pre1")
    up_t0 = arrived(stage1, 4, u1, "gather_wait_up0")
    a0, hh0 = _matmul(u1, up_t0, "nt", out_dtype=BF16, name="mlp_up0", epi="relu2")
    stage3, tok = start_gather((3, 5, 7), hh0, "gather_start3")
    down0 = arrived(stage2, 6, hh0, "gather_wait_down0")
    f0 = _matmul(hh0, down0, "nn", out_dtype=BF16, name="mlp_down0", after=tok)
    h2, u2 = _rms_post_pre(h1, f0, norm_ffn_post[0:1], norm_mix_pre[1:2], name="post_pre2")
    ssd_in_t = arrived(stage2, 2, u2, "gather_wait_ssd_in")
    pm, pdt = _matmul(u2, ssd_in_t, "nt", out_dtype=F32, name="ssd_in", tail=(n_heads,))
    dt_g = pdt.reshape(T, SSD_G, R).transpose(1, 0, 2)
    dtT_g = pdt.reshape(T, SSD_G, R).transpose(1, 2, 0)
    y_ssd, st_ssd = _ssd_fwd(pm, dt_g, dtT_g, ssd_prm, C=C, R=R, name="ssd_fwd")
    ssd_out = arrived(stage3, 3, y_ssd, "gather_wait_ssd_out")
    m1 = _matmul(y_ssd, ssd_out, "nn", out_dtype=BF16, name="ssd_out")
    h3, u3 = _rms_post_pre(h2, m1, norm_mix_post[1:2], norm_ffn_pre[1:2], name="post_pre3")
    up_t1 = arrived(stage3, 5, u3, "gather_wait_up1")
    a1, hh1 = _matmul(u3, up_t1, "nt", out_dtype=BF16, name="mlp_up1", epi="relu2")
    down1 = arrived(stage3, 7, hh1, "gather_wait_down1")
    f1 = _matmul(hh1, down1, "nn", out_dtype=BF16, name="mlp_down1")
    up_t, down = (up_t0, up_t1), (down0, down1)
    dh4, sq = _rms_post_loss(h3, f1, norm_ffn_post[1:2], tgt, name="post_loss")
    loss = lax.psum(sq[0, 0], MESH_AXES) * (0.5 / D)

    in_flight = []

    def send_grad(g, name):
        part = g if g.ndim == 3 else g.reshape(4, g.shape[0] // 4, g.shape[1])
        sems, zones, tok = _split_start([part], scatter=True, after=part, name=f"scatter_start_{name}")
        in_flight.append((name, sems, part, zones[0]))
        return tok

    retired = []

    def retire(name, after):
        nm, sems, src, zone = in_flight.pop(0)
        assert nm == name
        part = _sum_slots(_split_wait(sems, src, zone, 0, scatter=True, after=after, name=f"scatter_wait_{nm}"),
                          name=f"sum_chips_{nm}")
        sw_sems, zones = _swap_start([part], after=part, name=f"swap_start_{nm}")
        retired.append((nm, part, sw_sems, zones))
        return zones[0]

    def mlp_bwd(i, df, u, a, hh):
        tok = send_grad(_matmul(hh, df, "tn", out_dtype=BF16, name=f"mlp_down_wg{i}"), f"down{i}")
        da = _matmul(df, down[i], "nt", out_dtype=BF16, name=f"mlp_down_dg{i}", epi="drelu2", extra=a, after=tok)
        tok = send_grad(_matmul(u, da, "tn", out_dtype=BF16, name=f"mlp_up_wg{i}", col_parts=4), f"up{i}")
        return _matmul(da, up_t[i], "nn", out_dtype=BF16, name=f"mlp_up_dg{i}", after=tok)

    df1, d_nfpost1 = _rms_post_bwd(f1, norm_ffn_post[1:2], dh4, name="post_bwd_ffn1")
    du3 = mlp_bwd(1, df1, u3, a1, hh1)
    dh3, dm1, d_nfp1, d_nmpost1 = _rms_pre_post_bwd(h3, norm_ffn_pre[1:2], du3, dh4, m1, norm_mix_post[1:2],
                                                    name="pre_bwd_ffn1_post_bwd_mix1")
    tok = send_grad(_matmul(y_ssd, dm1, "tn", out_dtype=BF16, name="ssd_out_wg"), "ssd_out")
    dy_ssd = _matmul(dm1, ssd_out, "nt", out_dtype=F32, name="ssd_out_dg", after=tok)
    (dpm, ddt_g, d_bias, d_alog, d_dskip, dcwx, dcwb, dcwc, dcbx, dcbb, dcbc, d_normw) = _ssd_bwd(
        pm, dt_g, dtT_g, ssd_prm, st_ssd, dy_ssd, C=C, R=R, name="ssd_bwd")
    dpdt = ddt_g.transpose(1, 0, 2).reshape(T, n_heads).astype(BF16)
    tok = send_grad(jnp.concatenate([_matmul(dpm, u2, "tn", out_dtype=BF16, name="ssd_in_wg"),
                                     _matmul(dpdt, u2, "tn", out_dtype=BF16, name="ssd_in_dt_wg")], axis=0), "ssd_in")
    du2 = _matmul(dpm, ssd_in_t, "nn", out_dtype=BF16, name="ssd_in_dg", after=tok, tail=(n_heads, dpdt))
    dh2, df0, d_nmp1, d_nfpost0 = _rms_pre_post_bwd(h2, norm_mix_pre[1:2], du2, dh3, f0, norm_ffn_post[0:1],
                                                    name="pre_bwd_mix1_post_bwd_ffn0")
    du1 = mlp_bwd(0, df0, u1, a0, hh0)
    dh1, dm0, d_nfp0, d_nmpost0 = _rms_pre_post_bwd(h1, norm_ffn_pre[0:1], du1, dh2, m0, norm_mix_post[0:1],
                                                    name="pre_bwd_ffn0_post_bwd_mix0")
    tok = send_grad(_matmul(y_ret, dm0, "tn", out_dtype=BF16, name="ret_out_wg"), "ret_out")
    dy_ret = _matmul(dm0, ret_out, "nt", out_dtype=F32, name="ret_out_dg", after=tok)
    dproj, d_gn = _ret_bwd(proj, ret_gn_w, ret_consts, st_ret, dy_ret, C=C, name="ret_bwd")
    tok = send_grad(_matmul(u0, dproj, "tn", out_dtype=BF16, name="ret_in_wg", col_parts=4), "ret_in")
    du0 = _matmul(dproj, ret_in_t, "nn", out_dtype=BF16, name="ret_in_dg", after=tok)
    grad_x, d_nmp0 = _rms_pre_bwd(xs, norm_mix_pre[0:1], du0, dh1, name="pre_bwd_mix0")

    prev = grad_x
    for nm in ("down1", "up1", "ssd_out", "ssd_in", "down0", "up0", "ret_out", "ret_in"):
        prev = retire(nm, prev)

    def upd(w, gs, m, v, name):
        shp = w.shape
        w2, m2, v2 = (t.reshape(-1, shp[-1]) for t in (w, m, v))
        return tuple(t.reshape(shp) for t in _adamw(w2, gs, m2, v2, name=name))

    def upd_t(w, gs, m, v, name):
        return tuple(t.T[None] for t in _adamw(w[0].T, gs, m[0].T, v[0].T, name=name))

    def upd_layer(layer):
        def fn(w, gs, m, v, name):
            return tuple(_adamw(w, gs, m, v, name=name, layer=layer, stack=res.get(out_of[name[len("adamw_"):]])))
        return fn

    out_of = {"ret_in": "ret_w_in", "ret_out": "ret_w_out", "ssd_in": "ssd_w_in", "ssd_out": "ssd_w_out",
              "up0": "mlp_w_up", "up1": "mlp_w_up", "down0": "mlp_w_down", "down1": "mlp_w_down"}
    todo = {"ret_in": (upd, ret_w_in, m_ret_w_in, v_ret_w_in), "ret_out": (upd, ret_w_out, m_ret_w_out, v_ret_w_out),
            "ssd_in": (upd_t, ssd_w_in, m_ssd_w_in, v_ssd_w_in), "ssd_out": (upd, ssd_w_out, m_ssd_w_out, v_ssd_w_out),
            "up0": (upd_layer(0), mlp_w_up, m_mlp_w_up, v_mlp_w_up), "up1": (upd_layer(1), mlp_w_up, m_mlp_w_up, v_mlp_w_up),
            "down0": (upd_layer(0), mlp_w_down, m_mlp_w_down, v_mlp_w_down),
            "down1": (upd_layer(1), mlp_w_down, m_mlp_w_down, v_mlp_w_down)}
    res = {}
    for nm, mine, sems, zones in retired:
        other = _swap_wait(sems, [mine], zones, after=prev, name=f"swap_wait_{nm}")[0]
        fn, w, m, v = todo[nm]
        res[out_of[nm]] = fn(w, [mine, other], m, v, f"adamw_{nm}")
        prev = res[out_of[nm]][0]

    d_conv_w = jnp.concatenate([dcwx, dcwb, dcwc], axis=1)
    d_conv_b = jnp.concatenate([dcbx, dcbb, dcbc], axis=1)
    small_grads = [jnp.concatenate([d_nmp0, d_nmp1]), jnp.concatenate([d_nmpost0, d_nmpost1]),
                   jnp.concatenate([d_nfp0, d_nfp1]), jnp.concatenate([d_nfpost0, d_nfpost1]),
                   d_gn, d_bias.reshape(1, n_heads), d_alog.reshape(1, n_heads), d_dskip.reshape(1, n_heads),
                   d_conv_w, d_conv_b, d_normw]
    sg_shapes = [g.shape for g in small_grads]
    sg_rows = _pack_rows(sg_shapes)
    everyone = _all_gather_devices(_pack(small_grads, sg_rows), name="gather_small_grads")
    sg = _unpack(_sum_slots(everyone, name="sum_small_grads", tr=sg_rows), sg_shapes)
    (g_nmp, g_nmpost, g_nfp, g_nfpost, g_gn, g_bias, g_alog, g_dskip, g_cw_full, g_cb_full, g_nw_full) = sg
    g_cw = lax.dynamic_slice_in_dim(g_cw_full, chip * conv_sh, conv_sh, axis=1)[None]
    g_cb = lax.dynamic_slice_in_dim(g_cb_full, chip * conv_sh, conv_sh, axis=1)
    nw_sh = ssd_norm_w.shape[1]
    g_nw = lax.dynamic_slice_in_dim(g_nw_full, chip * nw_sh, nw_sh, axis=1)
    small = [("norm_mix_pre", norm_mix_pre, g_nmp, m_norm_mix_pre, v_norm_mix_pre),
             ("norm_mix_post", norm_mix_post, g_nmpost, m_norm_mix_post, v_norm_mix_post),
             ("norm_ffn_pre", norm_ffn_pre, g_nfp, m_norm_ffn_pre, v_norm_ffn_pre),
             ("norm_ffn_post", norm_ffn_post, g_nfpost, m_norm_ffn_post, v_norm_ffn_post),
             ("ret_gn_w", ret_gn_w, g_gn, m_ret_gn_w, v_ret_gn_w),
             ("ssd_conv_w", ssd_conv_w, g_cw, m_ssd_conv_w, v_ssd_conv_w),
             ("ssd_conv_b", ssd_conv_b, g_cb, m_ssd_conv_b, v_ssd_conv_b),
             ("ssd_dt_bias", ssd_dt_bias, g_bias, m_ssd_dt_bias, v_ssd_dt_bias),
             ("ssd_a_log", ssd_a_log, g_alog, m_ssd_a_log, v_ssd_a_log),
             ("ssd_d", ssd_d, g_dskip, m_ssd_d, v_ssd_d),
             ("ssd_norm_w", ssd_norm_w, g_nw, m_ssd_norm_w, v_ssd_norm_w)]
    sw_shapes = [w.shape for _, w, _, _, _ in small]
    sw_rows = _pack_rows(sw_shapes)
    packs = [_pack([t[j] for t in small], sw_rows) for j in (1, 2, 3, 4)]
    _, d_p, m_p, v_p = _adamw(packs[0], [packs[1]], packs[2], packs[3], name="adamw_small", tr=sw_rows)
    d_s, m_s, v_s = _unpack(d_p, sw_shapes), _unpack(m_p, sw_shapes), _unpack(v_p, sw_shapes)
    for j, (nm, w, g, _, _) in enumerate(small):
        res[nm] = (g.reshape(w.shape), d_s[j], m_s[j], v_s[j])

    order = ["norm_mix_pre", "norm_mix_post", "norm_ffn_pre", "norm_ffn_post", "ret_w_in", "ret_gn_w", "ret_w_out",
             "ssd_w_in", "ssd_conv_w", "ssd_conv_b", "ssd_dt_bias", "ssd_a_log", "ssd_d", "ssd_norm_w", "ssd_w_out",
             "mlp_w_up", "mlp_w_down"]
    return (loss, grad_x[None], *[res[n][0] for n in order], *[res[n][1] for n in order],
            *[res[n][2] for n in order], *[res[n][3] for n in order])
```

```python
import math

import numpy as np
import jax
import jax.numpy as jnp
from jax import lax
from jax.experimental import pallas as pl
from jax.experimental.pallas import tpu as pltpu

F32 = jnp.float32
BF16 = jnp.bfloat16
VMEM_LIMIT_BYTES = 56 * 1024 * 1024
MESH_AXES = ("x", "y", "c")
MESH_ID = pl.DeviceIdType.MESH

RMS_EPS = 1e-6
GN_EPS = 1e-5
RET_DK = 256
RET_DV = 512
ROPE_BASE = 10000.0
REF_CHUNK = 64
SSD_P = 64
SSD_N = 128
SSD_G = 8
SSD_CONV_W = 4
ADAM_LR, ADAM_B1, ADAM_B2, ADAM_EPS, ADAM_WD, ADAM_STEP = 0.001, 0.9, 0.999, 1e-08, 0.01, 10

NN = (((1,), (0,)), ((), ()))
NT = (((1,), (1,)), ((), ()))
TN = (((0,), (0,)), ((), ()))


def _cparams(*sem):
    return pltpu.CompilerParams(dimension_semantics=sem, vmem_limit_bytes=VMEM_LIMIT_BYTES)


def _dot(a, b, dims=NN):
    return lax.dot_general(a.astype(BF16), b.astype(BF16), dims, preferred_element_type=F32)


def _split_bf16(x, terms):
    parts, rest = [], x
    for _ in range(terms):
        p = rest.astype(BF16)
        parts.append(p)
        rest = rest - p.astype(F32)
    return parts


def _dot_sel(a, b, dims=NN, *, split, terms=3):
    if split == "a":
        sel = b.astype(BF16)
        return sum(lax.dot_general(p, sel, dims, preferred_element_type=F32) for p in _split_bf16(a, terms))
    sel = a.astype(BF16)
    return sum(lax.dot_general(sel, p, dims, preferred_element_type=F32) for p in _split_bf16(b, terms))


def _sigmoid(x):
    return 1.0 / (1.0 + jnp.exp(-x))


def _colsum(x):
    return jnp.sum(x, axis=0, keepdims=True)


MM_TILE = 1024
MM_FULL_K = 2048


def _mm_tiles(M, N, K):
    if K <= MM_FULL_K:
        return min(M, 2 * MM_TILE), min(N, MM_TILE), K
    return min(M, MM_TILE), min(N, 2 * MM_TILE), 2 * MM_TILE if K % (2 * MM_TILE) == 0 else MM_TILE


def _matmul(a, b, mode, *, out_dtype, name, epi=None, extra=None, after=None, col_parts=None, tail=None):
    nt_ = tail[0] if tail else 0
    if mode == "nn":
        (M, K), (K2, N) = a.shape, (b.shape[0] - nt_, b.shape[1])
    elif mode == "nt":
        (M, K), (N, K2) = a.shape, (b.shape[0] - nt_, b.shape[1])
    else:
        (K, M), (K2, N) = a.shape, b.shape
    assert K == K2, (a.shape, b.shape, mode)
    tm, tn, tk = _mm_tiles(M, N, K)
    if col_parts:
        tm, tn = min(M, 2 * MM_TILE), min(tn, MM_TILE)
        while (N // col_parts) % tn:
            tn //= 2
    assert M % tm == 0 and N % tn == 0 and K % tk == 0, (M, N, K, tm, tn, tk)
    nk = K // tk
    if mode == "tn":
        a_spec = pl.BlockSpec((tk, tm), lambda i, j, k: (k, i))
    else:
        a_spec = pl.BlockSpec((tm, tk), lambda i, j, k: (i, k))
    if mode == "nt":
        b_spec = pl.BlockSpec((tn, tk), lambda i, j, k: (j, k))
    else:
        b_spec = pl.BlockSpec((tk, tn), lambda i, j, k: (k, j))
    dims = {"nn": NN, "nt": NT, "tn": TN}[mode]
    o_spec = pl.BlockSpec((tm, tn), lambda i, j, k: (i, j))
    out_dims = (M, N)
    if col_parts:
        per = N // col_parts // tn
        o_spec = pl.BlockSpec((None, tm, tn), lambda i, j, k: (j // per, i, j % per))
        out_dims = (col_parts, M, N // col_parts)
    has_extra = epi in ("drelu2", "add")
    n_out = 2 if epi == "relu2" else 1

    in_specs = [a_spec, b_spec] + ([o_spec] if has_extra else [])
    args = [a, b] + ([extra] if has_extra else [])
    if after is not None:
        in_specs.append(pl.BlockSpec(after.shape, lambda i, j, k: (0, 0)))
        args.append(after)
    n_plain = len(args)
    out_specs = [o_spec] * n_out
    out_shape = [jax.ShapeDtypeStruct(out_dims, out_dtype)] * n_out
    if tail and mode == "nt":
        assert nk == 1 and N % nt_ == 0
        in_specs.append(pl.BlockSpec((nt_, tk), lambda i, j, k: (N // nt_, 0)))
        args.append(b)
        out_specs.append(pl.BlockSpec((tm, nt_), lambda i, j, k: (i, 0)))
        out_shape.append(jax.ShapeDtypeStruct((M, nt_), F32))
    elif tail:
        assert mode == "nn" and K % nt_ == 0
        in_specs += [pl.BlockSpec((tm, nt_), lambda i, j, k: (i, 0)), pl.BlockSpec((nt_, tn), lambda i, j, k: (K // nt_, j))]
        args += [tail[1], b]
    n_in = len(args)

    def body(*refs):
        a_ref, b_ref = refs[0], refs[1]
        e_ref = refs[2] if has_extra else None
        outs = refs[n_in:n_in + n_out]

        def finish(r):
            if tail and mode == "nn":
                r = r + _dot(refs[n_plain][...], refs[n_plain + 1][...])
            if epi is None:
                outs[0][...] = r.astype(outs[0].dtype)
            elif epi == "relu2":
                outs[0][...] = r.astype(outs[0].dtype)
                h = jnp.maximum(r, 0.0)
                outs[1][...] = (h * h).astype(outs[1].dtype)
            elif epi == "drelu2":
                av = jnp.maximum(e_ref[...].astype(F32), 0.0)
                outs[0][...] = (r * (2.0 * av)).astype(outs[0].dtype)
            else:
                outs[0][...] = (r + e_ref[...].astype(F32)).astype(outs[0].dtype)

        if tail and mode == "nt":
            @pl.when(pl.program_id(1) == 0)
            def _():
                refs[n_in + n_out][...] = _dot(a_ref[...], refs[n_plain][...], NT)

        if nk == 1:
            finish(_dot(a_ref[...], b_ref[...], dims))
            return
        acc = refs[-1]
        k = pl.program_id(2)

        @pl.when(k == 0)
        def _():
            acc[...] = jnp.zeros_like(acc)

        acc[...] += _dot(a_ref[...], b_ref[...], dims)

        @pl.when(k == nk - 1)
        def _():
            finish(acc[...])

    res = pl.pallas_call(
        body, name=name, grid=(M // tm, N // tn, nk), in_specs=in_specs, out_specs=out_specs,
        out_shape=out_shape, scratch_shapes=[pltpu.VMEM((tm, tn), F32)] if nk > 1 else [],
        compiler_params=_cparams("parallel", "arbitrary" if tail and mode == "nt" else "parallel", "arbitrary"),
    )(*args)
    return res if len(res) > 1 else res[0]


def _rstd(x):
    return lax.rsqrt(jnp.mean(x * x, axis=-1, keepdims=True) + RMS_EPS)


def _row_call(body, ins, outs_shape, *, name, rows, tr, acc_outs=()):
    tr = min(tr, rows)
    assert rows % tr == 0
    in_specs = []
    for arr, blocked in ins:
        if blocked:
            in_specs.append(pl.BlockSpec((tr, arr.shape[1]), lambda i: (i, 0)))
        else:
            in_specs.append(pl.BlockSpec(arr.shape, lambda i: (0, 0)))
    out_specs = []
    for n, s in enumerate(outs_shape):
        if n in acc_outs:
            out_specs.append(pl.BlockSpec(s.shape, lambda i: (0, 0)))
        else:
            out_specs.append(pl.BlockSpec((tr, s.shape[1]), lambda i: (i, 0)))
    return pl.pallas_call(
        body, name=name, grid=(rows // tr,), in_specs=in_specs, out_specs=out_specs, out_shape=outs_shape,
        compiler_params=_cparams("arbitrary" if acc_outs else "parallel"),
    )(*[a for a, _ in ins])


def _rms_post_pre(h, m, w_post, w_pre, *, name):
    T, D = h.shape

    def body(h_ref, m_ref, wp_ref, wn_ref, hn_ref, u_ref):
        mm = m_ref[...].astype(F32)
        hn = h_ref[...] + mm * _rstd(mm) * wp_ref[...]
        hn_ref[...] = hn
        u_ref[...] = (hn * _rstd(hn) * wn_ref[...]).astype(BF16)

    return _row_call(body, [(h, True), (m, True), (w_post, False), (w_pre, False)],
                     [jax.ShapeDtypeStruct((T, D), F32), jax.ShapeDtypeStruct((T, D), BF16)], name=name, rows=T, tr=256)


def _rms_post_loss(h, m, w_post, tgt, *, name):
    T, D = h.shape

    def body(h_ref, m_ref, wp_ref, t_ref, dh_ref, loss_ref):
        @pl.when(pl.program_id(0) == 0)
        def _():
            loss_ref[...] = jnp.zeros_like(loss_ref)

        mm = m_ref[...].astype(F32)
        err = h_ref[...] + mm * _rstd(mm) * wp_ref[...] - t_ref[...]
        dh_ref[...] = err * (1.0 / D)
        loss_ref[...] += _colsum(jnp.sum(err * err, axis=1, keepdims=True))

    return _row_call(body, [(h, True), (m, True), (w_post, False), (tgt, True)],
                     [jax.ShapeDtypeStruct((T, D), F32), jax.ShapeDtypeStruct((1, 1), F32)],
                     name=name, rows=T, tr=256, acc_outs=(1,))


def _rms_bwd_vals(x, w, dy):
    r = _rstd(x)
    xh = x * r
    g = dy * w
    dx = r * (g - xh * jnp.mean(g * xh, axis=-1, keepdims=True))
    return dx, _colsum(dy * xh)


def _rms_post_bwd(m, w_post, dh, *, name):
    T, D = m.shape

    def body(m_ref, w_ref, dh_ref, dm_ref, dw_ref):
        @pl.when(pl.program_id(0) == 0)
        def _():
            dw_ref[...] = jnp.zeros_like(dw_ref)

        dx, dw = _rms_bwd_vals(m_ref[...].astype(F32), w_ref[...], dh_ref[...])
        dm_ref[...] = dx.astype(BF16)
        dw_ref[...] += dw

    return _row_call(body, [(m, True), (w_post, False), (dh, True)],
                     [jax.ShapeDtypeStruct((T, D), BF16), jax.ShapeDtypeStruct((1, D), F32)],
                     name=name, rows=T, tr=256, acc_outs=(1,))


def _rms_pre_bwd(h, w_pre, du, dh_out, *, name):
    T, D = h.shape

    def body(h_ref, w_ref, du_ref, dho_ref, dh_ref, dw_ref):
        @pl.when(pl.program_id(0) == 0)
        def _():
            dw_ref[...] = jnp.zeros_like(dw_ref)

        dx, dw = _rms_bwd_vals(h_ref[...], w_ref[...], du_ref[...].astype(F32))
        dh_ref[...] = dho_ref[...] + dx
        dw_ref[...] += dw

    return _row_call(body, [(h, True), (w_pre, False), (du, True), (dh_out, True)],
                     [jax.ShapeDtypeStruct((T, D), F32), jax.ShapeDtypeStruct((1, D), F32)],
                     name=name, rows=T, tr=256, acc_outs=(1,))


def _rms_pre_post_bwd(h, w_pre, du, dh_out, m_prev, w_post_prev, *, name):
    T, D = h.shape

    def body(h_ref, w_ref, du_ref, dho_ref, m_ref, wp_ref, dh_ref, dm_ref, dw_ref, dwp_ref):
        @pl.when(pl.program_id(0) == 0)
        def _():
            dw_ref[...] = jnp.zeros_like(dw_ref)
            dwp_ref[...] = jnp.zeros_like(dwp_ref)

        dx, dw = _rms_bwd_vals(h_ref[...], w_ref[...], du_ref[...].astype(F32))
        dh = dho_ref[...] + dx
        dh_ref[...] = dh
        dw_ref[...] += dw
        dm, dwp = _rms_bwd_vals(m_ref[...].astype(F32), wp_ref[...], dh)
        dm_ref[...] = dm.astype(BF16)
        dwp_ref[...] += dwp

    return _row_call(body, [(h, True), (w_pre, False), (du, True), (dh_out, True), (m_prev, True), (w_post_prev, False)],
                     [jax.ShapeDtypeStruct((T, D), F32), jax.ShapeDtypeStruct((T, D), BF16),
                      jax.ShapeDtypeStruct((1, D), F32), jax.ShapeDtypeStruct((1, D), F32)],
                     name=name, rows=T, tr=256, acc_outs=(2, 3))


def _ret_consts(T, C, H):
    lg = np.log1p(-np.exp2(-5.0 - np.arange(H, dtype=np.float64)))
    idx = np.arange(C, dtype=np.float64)
    dist = np.abs(idx[:, None] - idx[None, :])
    vis = (idx[None, :] // REF_CHUNK) <= (idx[:, None] // REF_CHUNK)
    mask = np.exp(dist[None] * lg[:, None, None]) * vis[None]
    xi = np.exp((idx[None, :] + 1.0) * lg[:, None])[..., None]
    zeta = np.exp((C - 1.0 - idx)[None, :] * lg[:, None])[..., None]
    half = RET_DK // 2
    inv_freq = ROPE_BASE ** (-np.arange(half, dtype=np.float32) / np.float32(half))
    ang = np.arange(T, dtype=np.float32)[:, None] * inv_freq[None, :].astype(np.float32)
    return (jnp.asarray(mask, F32), jnp.asarray(xi, F32), jnp.asarray(zeta, F32),
            jnp.asarray(np.cos(ang), F32), jnp.asarray(np.sin(ang), F32))


def _rot(t, cos, sin):
    half = RET_DK // 2
    t1, t2 = t[:, :half], t[:, half:]
    return jnp.concatenate([t1 * cos - t2 * sin, t1 * sin + t2 * cos], axis=1)


def _unrot(d, cos, sin):
    half = RET_DK // 2
    d1, d2 = d[:, :half], d[:, half:]
    return jnp.concatenate([d1 * cos + d2 * sin, d2 * cos - d1 * sin], axis=1)


def _ret_specs(C, H, rev, NS):
    def ci(i):
        return NS - 1 - i if rev else i

    nq = H
    q_spec = pl.BlockSpec((C, RET_DK), lambda h, i: (ci(i), h))
    k_spec = pl.BlockSpec((C, RET_DK), lambda h, i: (ci(i), nq + h))
    v_spec = pl.BlockSpec((C, RET_DV), lambda h, i: (ci(i), H + h))
    g_spec = pl.BlockSpec((C, RET_DV), lambda h, i: (ci(i), 2 * H + h))
    cs_spec = pl.BlockSpec((C, RET_DK // 2), lambda h, i: (ci(i), 0))
    m_spec = pl.BlockSpec((None, C, C), lambda h, i: (h, 0, 0))
    vec_spec = pl.BlockSpec((None, C, 1), lambda h, i: (h, 0, 0))
    gn_spec = pl.BlockSpec((1, RET_DV), lambda h, i: (0, h))
    st_spec = pl.BlockSpec((None, None, RET_DK, RET_DV), lambda h, i: (h, ci(i), 0, 0))
    return q_spec, k_spec, v_spec, g_spec, cs_spec, m_spec, vec_spec, gn_spec, st_spec


def _ret_fwd_vals(q, k, v, cos, sin, mask, xi, s_in):
    qr = _rot(q, cos, sin)
    kr = _rot(k, cos, sin) * (RET_DK ** -0.5)
    a = _dot(qr, kr, NT) * mask
    o = _dot(a, v) + _dot(qr, s_in) * xi
    mu = jnp.mean(o, axis=1, keepdims=True)
    oc = o - mu
    rstd = lax.rsqrt(jnp.mean(oc * oc, axis=1, keepdims=True) + GN_EPS)
    return qr, kr, a, oc * rstd, rstd


def _ret_fwd(proj, gn_w, consts, *, C, name):
    T = proj.shape[0]
    H = gn_w.shape[1] // RET_DV
    NS = T // C
    mask, xi, zeta, cos, sin = consts
    q_spec, k_spec, v_spec, g_spec, cs_spec, m_spec, vec_spec, gn_spec, st_spec = _ret_specs(C, H, False, NS)
    y_spec = pl.BlockSpec((C, RET_DV), lambda h, i: (i, h))

    def body(q_ref, k_ref, v_ref, g_ref, cos_ref, sin_ref, m_ref, xi_ref, ze_ref, gn_ref, y_ref, st_ref, S):
        @pl.when(pl.program_id(1) == 0)
        def _():
            S[...] = jnp.zeros_like(S)

        s_in = S[...]
        st_ref[...] = s_in
        v = v_ref[...]
        xi_v = xi_ref[...]
        qr, kr, a, on, rstd = _ret_fwd_vals(q_ref[...], k_ref[...], v, cos_ref[...], sin_ref[...], m_ref[...], xi_v, s_in)
        g = g_ref[...]
        y_ref[...] = (g * _sigmoid(g) * on * gn_ref[...]).astype(BF16)
        S[...] = s_in * xi_v[C - 1:C, :] + _dot(kr * ze_ref[...], v, TN)

    return pl.pallas_call(
        body, name=name, grid=(H, NS),
        in_specs=[q_spec, k_spec, v_spec, g_spec, cs_spec, cs_spec, m_spec, vec_spec, vec_spec, gn_spec],
        out_specs=[y_spec, st_spec],
        out_shape=[jax.ShapeDtypeStruct((T, H * RET_DV), BF16), jax.ShapeDtypeStruct((H, NS, RET_DK, RET_DV), F32)],
        scratch_shapes=[pltpu.VMEM((RET_DK, RET_DV), F32)],
        compiler_params=_cparams("parallel", "arbitrary"),
    )(proj, proj, proj, proj, cos, sin, mask, xi, zeta, gn_w)


def _stage_out(out_hbm, stage, sems, step, n_steps, row0, pieces, values):
    C = stage.shape[1]
    slot = step % 2

    def copies(sl):
        return [pltpu.make_async_copy(stage.at[sl, :, pl.ds(c0, w)],
                                      out_hbm.at[pl.ds(pl.multiple_of(row0, 16), C), pl.ds(pl.multiple_of(dc, 128), w)],
                                      sems.at[sl, n]) for n, (c0, w, dc) in enumerate(pieces)]

    @pl.when(step >= 2)
    def _():
        for cp in copies(slot):
            cp.wait()

    for (c0, w, _), v in zip(pieces, values):
        stage[slot, :, c0:c0 + w] = v
    for cp in copies(slot):
        cp.start()

    @pl.when(step == n_steps - 1)
    def _():
        for cp in copies(slot):
            cp.wait()
        if n_steps >= 2:
            for cp in copies(1 - slot):
                cp.wait()


def _ret_bwd(proj, gn_w, consts, states, dy, *, C, name):
    T = proj.shape[0]
    H = gn_w.shape[1] // RET_DV
    NS = T // C
    mask, xi, zeta, cos, sin = consts
    q_spec, k_spec, v_spec, g_spec, cs_spec, m_spec, vec_spec, gn_spec, st_spec = _ret_specs(C, H, True, NS)
    dy_spec = pl.BlockSpec((C, RET_DV), lambda h, i: (NS - 1 - i, h))
    scale = RET_DK ** -0.5
    wq, wv = H * RET_DK, H * RET_DV

    def body(q_ref, k_ref, v_ref, g_ref, cos_ref, sin_ref, m_ref, xi_ref, ze_ref, gn_ref, st_ref, dy_ref,
             dproj_ref, dgn_ref, dS, stage, sems):
        @pl.when(pl.program_id(1) == 0)
        def _():
            dS[...] = jnp.zeros_like(dS)
            dgn_ref[...] = jnp.zeros_like(dgn_ref)

        s_in = st_ref[...]
        v = v_ref[...]
        cos, sin, mask, xi_v, ze = cos_ref[...], sin_ref[...], m_ref[...], xi_ref[...], ze_ref[...]
        qr, kr, a, on, rstd = _ret_fwd_vals(q_ref[...], k_ref[...], v, cos, sin, mask, xi_v, s_in)
        g = g_ref[...]
        sg = _sigmoid(g)
        silu = g * sg
        gnw = gn_ref[...]
        dy = dy_ref[...].astype(F32)
        dg = (dy * on * gnw * (sg * (1.0 + g * (1.0 - sg)))).astype(BF16)
        t = dy * silu
        dgn_ref[...] += _colsum(t * on)
        don = t * gnw
        do = rstd * (don - jnp.mean(don, axis=1, keepdims=True) - on * jnp.mean(don * on, axis=1, keepdims=True))
        dox = do * xi_v
        ds_out = dS[...]
        da = _dot(do, v, NT) * mask
        kz = kr * ze
        dv = (_dot(a, do, TN) + _dot(kz, ds_out)).astype(BF16)
        dqr = _dot(da, kr) + _dot(dox, s_in, NT)
        dkr = _dot(da, qr, TN) + _dot(v, ds_out, NT) * ze
        dS[...] = ds_out * xi_v[C - 1:C, :] + _dot(qr, dox, TN)
        dq = _unrot(dqr, cos, sin).astype(BF16)
        dk = _unrot(dkr * scale, cos, sin).astype(BF16)
        h, i = pl.program_id(0), pl.program_id(1)
        pieces = [(0, RET_DK, h * RET_DK), (RET_DK, RET_DK, wq + h * RET_DK),
                  (2 * RET_DK, RET_DV, 2 * wq + h * RET_DV), (2 * RET_DK + RET_DV, RET_DV, 2 * wq + wv + h * RET_DV)]
        _stage_out(dproj_ref, stage, sems, h * NS + i, H * NS, (NS - 1 - i) * C, pieces, [dq, dk, dv, dg])

    return pl.pallas_call(
        body, name=name, grid=(H, NS),
        in_specs=[q_spec, k_spec, v_spec, g_spec, cs_spec, cs_spec, m_spec, vec_spec, vec_spec, gn_spec, st_spec, dy_spec],
        out_specs=[_ANY, gn_spec],
        out_shape=[jax.ShapeDtypeStruct((T, 2 * wq + 2 * wv), BF16), jax.ShapeDtypeStruct((1, H * RET_DV), F32)],
        scratch_shapes=[pltpu.VMEM((RET_DK, RET_DV), F32), pltpu.VMEM((2, C, 2 * RET_DK + 2 * RET_DV), BF16),
                        pltpu.SemaphoreType.DMA((2, 4))],
        compiler_params=_cparams("arbitrary", "arbitrary"),
    )(proj, proj, proj, proj, cos, sin, mask, xi, zeta, gn_w, states, dy)


def _shift_down(x, prev8, k):
    if k == 0:
        return x
    y = pltpu.roll(x, k, 0)
    row = lax.broadcasted_iota(jnp.int32, prev8.shape, 0)
    top = jnp.where(row < k, pltpu.roll(prev8, k, 0), y[:8])
    return jnp.concatenate([top, y[8:]], axis=0)


def _shift_up(x, next8, k):
    if k == 0:
        return x
    n = x.shape[0]
    y = pltpu.roll(x, n - k, 0)
    row = lax.broadcasted_iota(jnp.int32, next8.shape, 0)
    bot = jnp.where(row >= 8 - k, pltpu.roll(next8, 8 - k, 0), y[n - 8:])
    return jnp.concatenate([y[:n - 8], bot], axis=0)


def _conv_silu(raw, halo, w, b):
    cv = b
    for tap in range(SSD_CONV_W):
        cv = cv + _shift_down(raw, halo, SSD_CONV_W - 1 - tap) * w[tap:tap + 1, :]
    sg = _sigmoid(cv)
    return cv * sg, cv, sg


def _conv_silu_bwd(d_post, cv, sg, raw, halo, w, carry8):
    dcv = d_post * (sg * (1.0 + cv * (1.0 - sg)))
    d_raw = jnp.zeros_like(raw)
    dws = []
    for tap in range(SSD_CONV_W):
        k = SSD_CONV_W - 1 - tap
        d_raw = d_raw + _shift_up(dcv, carry8, k) * w[tap:tap + 1, :]
        dws.append(_colsum(dcv * _shift_down(raw, halo, k)))
    return d_raw, jnp.concatenate(dws, axis=0), _colsum(dcv), dcv[:8]


def _softplus(x):
    return jnp.maximum(x, 0.0) + jnp.log1p(jnp.exp(-jnp.abs(x)))


def _ssd_common(C, R, dt, dtT, bias, biasT, alog, alogT, E):
    p = dt + bias
    dtv = _softplus(p)
    a = -jnp.exp(alog)
    da = dtv * a
    daT = _softplus(dtT + biasT) * (-jnp.exp(alogT))
    row = lax.broadcasted_iota(jnp.int32, (C, C), 0)
    col = lax.broadcasted_iota(jnp.int32, (C, C), 1)
    tril = row >= col
    trilf = jnp.where(tril, 1.0, 0.0).astype(F32)
    triuf = jnp.where(col >= row, 1.0, 0.0).astype(F32)
    acum = _dot_sel(trilf, da, split="b")
    acumT = _dot_sel(daT, trilf, NT, split="a")
    al = acum[C - 1:C, :]
    ea = jnp.exp(acum)
    dte = jnp.exp(al - acum)
    eal = jnp.exp(al)
    return dict(p=p, dtv=dtv, a=a, da=da, tril=tril, triuf=triuf, acum=acum, acumT=acumT, al=al, ea=ea, dte=dte, eal=eal,
                dtv_e=_dot_sel(dtv, E, split="a", terms=2), ea_e=_dot_sel(ea, E, split="a", terms=2),
                dte_e=_dot_sel(dte, E, split="a", terms=2), eal_e=_dot_sel(eal, E, split="a"))


def _head_decay(q, r, C, R):
    seg = jnp.broadcast_to(q["acum"][:, r:r + 1], (C, C)) - q["acumT"][r:r + 1, :]
    return jnp.exp(jnp.where(q["tril"], seg, -1e30))


def _ssd_group_specs(C, R, NS, rev):
    RP = R * SSD_P
    G = SSD_G
    hb = C // 8

    def ci(i):
        return NS - 1 - i if rev else i

    def halo_row(i):
        return jnp.maximum(ci(i) * hb - 1, 0)

    off_b = G * RP // SSD_N
    z_spec = pl.BlockSpec((C, RP), lambda g, i: (ci(i), g))
    x_spec = pl.BlockSpec((C, RP), lambda g, i: (ci(i), G + g))
    b_spec = pl.BlockSpec((C, SSD_N), lambda g, i: (ci(i), 2 * off_b + g))
    c_spec = pl.BlockSpec((C, SSD_N), lambda g, i: (ci(i), 2 * off_b + G + g))
    xh_spec = pl.BlockSpec((8, RP), lambda g, i: (halo_row(i), G + g))
    bh_spec = pl.BlockSpec((8, SSD_N), lambda g, i: (halo_row(i), 2 * off_b + g))
    ch_spec = pl.BlockSpec((8, SSD_N), lambda g, i: (halo_row(i), 2 * off_b + G + g))
    dt_spec = pl.BlockSpec((None, C, R), lambda g, i: (g, ci(i), 0))
    dtT_spec = pl.BlockSpec((None, R, C), lambda g, i: (g, 0, ci(i)))
    pr_spec = pl.BlockSpec((None, 1, R), lambda g, i: (g, 0, 0))
    prT_spec = pl.BlockSpec((None, R, 1), lambda g, i: (g, 0, 0))
    cwx_spec = pl.BlockSpec((SSD_CONV_W, RP), lambda g, i: (0, g))
    cwn_spec = pl.BlockSpec((SSD_CONV_W, SSD_N), lambda g, i: (0, g))
    cbx_spec = pl.BlockSpec((1, RP), lambda g, i: (0, g))
    cbn_spec = pl.BlockSpec((1, SSD_N), lambda g, i: (0, g))
    e_spec = pl.BlockSpec((R, RP), lambda g, i: (0, 0))
    st_spec = pl.BlockSpec((None, None, SSD_N, RP), lambda g, i: (g, ci(i), 0, 0))
    return dict(z=z_spec, x=x_spec, b=b_spec, c=c_spec, xh=xh_spec, bh=bh_spec, ch=ch_spec, dt=dt_spec, dtT=dtT_spec,
                pr=pr_spec, prT=prT_spec, cwx=cwx_spec, cwn=cwn_spec, cbx=cbx_spec, cbn=cbn_spec, e=e_spec, st=st_spec)


def _ssd_forward_vals(C, R, refs, first, s_in):
    E = refs["E"]
    halo_on = jnp.where(first, 0.0, 1.0)
    xh, bh, ch = refs["xh"] * halo_on, refs["bh"] * halo_on, refs["ch"] * halo_on
    xs, cvx, sgx = _conv_silu(refs["x"], xh, refs["cwx"], refs["cbx"])
    bm, cvb, sgb = _conv_silu(refs["b"], bh, refs["cwb"], refs["cbb"])
    cm, cvc, sgc = _conv_silu(refs["c"], ch, refs["cwc"], refs["cbc"])
    q = _ssd_common(C, R, refs["dt"], refs["dtT"], refs["bias"], refs["biasT"], refs["alog"], refs["alogT"], E)
    xdt = xs * q["dtv_e"]
    cb = _dot(cm, bm, NT)
    yoff_raw = _dot(cm, s_in)
    xdt_b = xdt.astype(BF16)
    low = lax.broadcasted_iota(jnp.int32, (1, 2 * SSD_P), 1) < SSD_P
    pairs = []
    for j in range(R // 2):
        xp = xdt_b[:, 2 * SSD_P * j:2 * SSD_P * (j + 1)]
        y0 = _dot(cb * _head_decay(q, 2 * j, C, R), xp)
        y1 = _dot(cb * _head_decay(q, 2 * j + 1, C, R), xp)
        pairs.append(jnp.where(low, y0, y1))
    ydiag = jnp.concatenate(pairs, axis=1)
    d_e =_dot_sel(refs["dskip"], E, split="a")
    y = ydiag + yoff_raw * q["ea_e"] + d_e * xs
    xd = xdt * q["dte_e"]
    s_out = s_in * q["eal_e"] + _dot(bm, xd, TN)
    z = refs["z"]
    sgz = _sigmoid(z)
    yz = y * (z * sgz)
    rn = lax.rsqrt(jnp.mean(yz * yz, axis=1, keepdims=True) + RMS_EPS)
    return dict(q=q, xh=xh, bh=bh, ch=ch, xs=xs, cvx=cvx, sgx=sgx, bm=bm, cvb=cvb, sgb=sgb, cm=cm, cvc=cvc, sgc=sgc,
                xdt=xdt, cb=cb, yoff_raw=yoff_raw, d_e=d_e, y=y, xd=xd, s_out=s_out, z=z, sgz=sgz, yz=yz, rn=rn)


_SSD_IN_NAMES = ("z", "x", "b", "c", "xh", "bh", "ch", "dt", "dtT", "bias", "biasT", "alog", "alogT", "dskip",
                 "cwx", "cwb", "cwc", "cbx", "cbb", "cbc", "nw", "E")


def _ssd_inputs(pm, dt_g, dtT_g, prm, sp):
    bias, biasT, alog, alogT, dskip, cwx, cwb, cwc, cbx, cbb, cbc, nw, E = prm
    args = [pm, pm, pm, pm, pm, pm, pm, dt_g, dtT_g, bias, biasT, alog, alogT, dskip, cwx, cwb, cwc, cbx, cbb, cbc, nw, E]
    specs = [sp["z"], sp["x"], sp["b"], sp["c"], sp["xh"], sp["bh"], sp["ch"], sp["dt"], sp["dtT"], sp["pr"], sp["prT"],
             sp["pr"], sp["prT"], sp["pr"], sp["cwx"], sp["cwn"], sp["cwn"], sp["cbx"], sp["cbn"], sp["cbn"], sp["cbx"], sp["e"]]
    return args, specs


def _ssd_fwd(pm, dt_g, dtT_g, prm, *, C, R, name):
    T = pm.shape[0]
    NS = T // C
    RP = R * SSD_P
    G = SSD_G
    sp = _ssd_group_specs(C, R, NS, False)
    args, specs = _ssd_inputs(pm, dt_g, dtT_g, prm, sp)
    nin = len(args)

    def body(*refs):
        ins = {n: r[...] for n, r in zip(_SSD_IN_NAMES, refs[:nin])}
        y_ref, st_ref, S = refs[nin:]
        first = pl.program_id(1) == 0

        @pl.when(first)
        def _():
            S[...] = jnp.zeros_like(S)

        s_in = S[...]
        st_ref[...] = s_in
        f = _ssd_forward_vals(C, R, ins, first, s_in)
        y_ref[...] = (f["yz"] * f["rn"] * ins["nw"]).astype(BF16)
        S[...] = f["s_out"]

    return pl.pallas_call(
        body, name=name, grid=(G, NS), in_specs=specs,
        out_specs=[pl.BlockSpec((C, RP), lambda g, i: (i, g)), sp["st"]],
        out_shape=[jax.ShapeDtypeStruct((T, G * RP), BF16), jax.ShapeDtypeStruct((G, NS, SSD_N, RP), F32)],
        scratch_shapes=[pltpu.VMEM((SSD_N, RP), F32)],
        compiler_params=_cparams("parallel", "arbitrary"),
    )(*args)


def _ssd_bwd(pm, dt_g, dtT_g, prm, states, dout, *, C, R, name):
    T = pm.shape[0]
    NS = T // C
    RP = R * SSD_P
    G = SSD_G
    sp = _ssd_group_specs(C, R, NS, True)
    args, specs = _ssd_inputs(pm, dt_g, dtT_g, prm, sp)
    nin = len(args)
    rows_spec = pl.BlockSpec((C, RP), lambda g, i: (NS - 1 - i, g))
    args = args + [states, dout]
    specs = specs + [sp["st"], rows_spec]

    def body(*refs):
        ins = {n: r[...] for n, r in zip(_SSD_IN_NAMES, refs[:nin])}
        st_ref, dout_ref = refs[nin], refs[nin + 1]
        (dpm_ref, ddt_ref, dbias_ref, dalog_ref, dd_ref, dcwx_ref, dcwb_ref, dcwc_ref,
         dcbx_ref, dcbb_ref, dcbc_ref, dnw_ref) = refs[nin + 2:nin + 14]
        dS, cx8, cb8, cc8, stage, sems = refs[nin + 14:]
        acc_refs = (dbias_ref, dalog_ref, dd_ref, dcwx_ref, dcwb_ref, dcwc_ref, dcbx_ref, dcbb_ref, dcbc_ref, dnw_ref)
        step = pl.program_id(1)

        @pl.when(step == 0)
        def _():
            for r_ in acc_refs + (dS, cx8, cb8, cc8):
                r_[...] = jnp.zeros_like(r_)

        first = step == NS - 1
        E = ins["E"]
        s_in = st_ref[...]
        f = _ssd_forward_vals(C, R, ins, first, s_in)
        q = f["q"]
        xs, bm, cm, xdt, cb, y, z, sgz, yz, rn = (f[n] for n in ("xs", "bm", "cm", "xdt", "cb", "y", "z", "sgz", "yz", "rn"))
        nw = ins["nw"]
        dout = dout_ref[...].astype(F32)
        yh = yz * rn
        dnw_ref[...] += _colsum(dout * yh)
        g1 = dout * nw
        dyz = rn * (g1 - yh * jnp.mean(g1 * yh, axis=1, keepdims=True))
        dz = (dyz * y * (sgz * (1.0 + z * (1.0 - sgz)))).astype(BF16)
        dy = dyz * (z * sgz)
        dd_ref[...] += _dot_sel(_colsum(dy * xs), E, NT, split="a")
        dxs = dy * f["d_e"]
        dyo = dy * q["ea_e"]
        dcm = _dot(dyo, s_in, NT)
        ds_acc = _dot(cm, dyo, TN)
        dacum = _dot_sel(dy * f["yoff_raw"], E, NT, split="a", terms=1) * q["ea"]
        dacumT = jnp.zeros((R, C), F32)
        dcb = jnp.zeros((C, C), F32)
        rowR = lax.broadcasted_iota(jnp.int32, (1, R), 1)
        rowRT = lax.broadcasted_iota(jnp.int32, (R, 1), 0)
        dy_b, xdt_b = dy.astype(BF16), xdt.astype(BF16)
        low = lax.broadcasted_iota(jnp.int32, (1, 2 * SSD_P), 1) < SSD_P
        dxdt_pairs = []
        for j in range(R // 2):
            lanes = slice(2 * SSD_P * j, 2 * SSD_P * (j + 1))
            dyp, xp = dy_b[:, lanes], xdt_b[:, lanes]
            halves = []
            for r, mine in ((2 * j, low), (2 * j + 1, jnp.logical_not(low))):
                lr = _head_decay(q, r, C, R)
                w_r = cb * lr
                dw = _dot(jnp.where(mine, dyp, jnp.zeros_like(dyp)), xp, NT)
                halves.append(_dot(w_r, dyp, TN))
                dcb = dcb + dw * lr
                dseg = dw * w_r
                dacum = dacum + jnp.sum(dseg, axis=1, keepdims=True) * jnp.where(rowR == r, 1.0, 0.0)
                dacumT = dacumT - _colsum(dseg) * jnp.where(rowRT == r, 1.0, 0.0)
            dxdt_pairs.append(jnp.where(low, halves[0], halves[1]))
        dxdt = jnp.concatenate(dxdt_pairs, axis=1)
        dsn = dS[...]
        ds_acc = ds_acc + dsn * q["eal_e"]
        d_eal = _dot_sel(_colsum(dsn * s_in), E, NT, split="a")
        dbm = _dot(f["xd"], dsn, NT)
        dxd = _dot(bm, dsn)
        dxdt = dxdt + dxd * q["dte_e"]
        d_dte = _dot_sel(dxd * xdt, E, NT, split="a", terms=1) * q["dte"]
        d_al = _colsum(d_dte) + d_eal * q["eal"]
        dacum = dacum - d_dte
        rowC = lax.broadcasted_iota(jnp.int32, (C, 1), 0)
        dacum = dacum + jnp.where(rowC == C - 1, 1.0, 0.0) * d_al
        dS[...] = ds_acc
        dcm = dcm + _dot(dcb, bm)
        dbm = dbm + _dot(dcb, cm, TN)
        eye = jnp.where(lax.broadcasted_iota(jnp.int32, (C, C), 0) == lax.broadcasted_iota(jnp.int32, (C, C), 1), 1.0, 0.0)
        dacum = dacum + _dot_sel(eye, dacumT, NT, split="b")
        dda = _dot_sel(q["triuf"], dacum, split="b")
        ddtv = dda * q["a"] + _dot_sel(dxdt * xs, E, NT, split="a", terms=1)
        dalog_ref[...] += _colsum(dda * q["dtv"]) * q["a"]
        dxs = dxs + dxdt * q["dtv_e"]
        dp = ddtv * _sigmoid(q["p"])
        ddt_ref[...] = dp
        dbias_ref[...] += _colsum(dp)
        d_raw, d_w, d_b, c8 = _conv_silu_bwd(dxs, f["cvx"], f["sgx"], ins["x"], f["xh"], ins["cwx"], cx8[...])
        dx = d_raw.astype(BF16)
        dcwx_ref[...] += d_w
        dcbx_ref[...] += d_b
        cx8[...] = c8
        d_raw, d_w, d_b, c8 = _conv_silu_bwd(dbm, f["cvb"], f["sgb"], ins["b"], f["bh"], ins["cwb"], cb8[...])
        db = d_raw.astype(BF16)
        dcwb_ref[...] += d_w
        dcbb_ref[...] += d_b
        cb8[...] = c8
        d_raw, d_w, d_b, c8 = _conv_silu_bwd(dcm, f["cvc"], f["sgc"], ins["c"], f["ch"], ins["cwc"], cc8[...])
        dc = d_raw.astype(BF16)
        dcwc_ref[...] += d_w
        dcbc_ref[...] += d_b
        cc8[...] = c8
        g_ = pl.program_id(0)
        pieces = [(0, RP, g_ * RP), (RP, RP, G * RP + g_ * RP), (2 * RP, SSD_N, 2 * G * RP + g_ * SSD_N),
                  (2 * RP + SSD_N, SSD_N, 2 * G * RP + G * SSD_N + g_ * SSD_N)]
        _stage_out(dpm_ref, stage, sems, g_ * NS + step, G * NS, (NS - 1 - step) * C, pieces, [dz, dx, db, dc])

    out_specs = [_ANY, pl.BlockSpec((None, C, R), lambda g, i: (g, NS - 1 - i, 0)),
                 sp["pr"], sp["pr"], sp["pr"], sp["cwx"], sp["cwn"], sp["cwn"], sp["cbx"], sp["cbn"], sp["cbn"], sp["cbx"]]
    out_shape = [jax.ShapeDtypeStruct((T, 2 * G * RP + 2 * G * SSD_N), BF16),
                 jax.ShapeDtypeStruct((G, T, R), F32),
                 jax.ShapeDtypeStruct((G, 1, R), F32), jax.ShapeDtypeStruct((G, 1, R), F32), jax.ShapeDtypeStruct((G, 1, R), F32),
                 jax.ShapeDtypeStruct((SSD_CONV_W, G * RP), F32), jax.ShapeDtypeStruct((SSD_CONV_W, G * SSD_N), F32),
                 jax.ShapeDtypeStruct((SSD_CONV_W, G * SSD_N), F32),
                 jax.ShapeDtypeStruct((1, G * RP), F32), jax.ShapeDtypeStruct((1, G * SSD_N), F32),
                 jax.ShapeDtypeStruct((1, G * SSD_N), F32), jax.ShapeDtypeStruct((1, G * RP), F32)]
    return pl.pallas_call(
        body, name=name, grid=(G, NS), in_specs=specs, out_specs=out_specs, out_shape=out_shape,
        scratch_shapes=[pltpu.VMEM((SSD_N, RP), F32), pltpu.VMEM((8, RP), F32), pltpu.VMEM((8, SSD_N), F32),
                        pltpu.VMEM((8, SSD_N), F32), pltpu.VMEM((2, C, 2 * RP + 2 * SSD_N), BF16),
                        pltpu.SemaphoreType.DMA((2, 4))],
        compiler_params=_cparams("arbitrary", "arbitrary"),
    )(*args)


_ANY = pl.BlockSpec(memory_space=pl.ANY)


def _chip_peer(k):
    x, y, c = lax.axis_index("x"), lax.axis_index("y"), lax.axis_index("c")
    return (x ^ (k >> 1), y ^ (k & 1), c)


def _my_chip():
    return 2 * lax.axis_index("x") + lax.axis_index("y")


NORM_ROWS = 256


def _all_gather_chips(shards, halved, *, name, norm=None):
    n = len(shards)

    nb = 0 if norm is None else norm[0].shape[0] // NORM_ROWS

    def body(*refs):
        ins = refs[:n]
        outs = refs[n + 2 * bool(nb):2 * n + 2 * bool(nb)]
        send, recv, fsend, frecv, loc = refs[-5:]
        s = _my_chip()
        c = lax.axis_index("c")
        sibling = (lax.axis_index("x"), lax.axis_index("y"), 1 - c)
        step = pl.program_id(0) if nb else 0

        def local(a):
            return pltpu.make_async_copy(ins[a], outs[a].at[s], loc.at[a])

        def rows(a, core):
            if not halved[a]:
                return slice(None)
            half = shards[a].shape[0] // 2
            return pl.ds(pl.multiple_of(core * half, 16), half)

        def over_ici(a, k, slot, core):
            return pltpu.make_async_remote_copy(
                src_ref=ins[a].at[rows(a, core)], dst_ref=outs[a].at[slot, rows(a, core)],
                send_sem=send.at[3 * a + k - 1], recv_sem=recv.at[3 * a + k - 1],
                device_id=_chip_peer(k), device_id_type=MESH_ID)

        def over_d2d(a, k, core):
            z = outs[a].at[s ^ k, rows(a, core)]
            return pltpu.make_async_remote_copy(
                src_ref=z, dst_ref=z, send_sem=fsend.at[3 * a + k - 1], recv_sem=frecv.at[3 * a + k - 1],
                device_id=sibling, device_id_type=MESH_ID)

        def send_all():
            for a in range(n):
                local(a).start()
            for a in range(n):
                for k in (1, 2, 3):
                    over_ici(a, k, s, c).start()

        def finish_all():
            passed = []
            for a in range(n):
                for k in (1, 2, 3):
                    over_ici(a, k, s ^ k, c).wait_recv()
                    if halved[a]:
                        cp = over_d2d(a, k, c)
                        cp.start()
                        passed.append(cp)
            for a in range(n):
                if halved[a]:
                    for k in (1, 2, 3):
                        over_d2d(a, k, 1 - c).wait_recv()
            for a in range(n):
                for k in (1, 2, 3):
                    over_ici(a, k, s, c).wait_send()
            for cp in passed:
                cp.wait_send()
            for a in range(n):
                local(a).wait()

        if not nb:
            send_all()
            finish_all()
            return
        pl.when(step == 0)(send_all)
        x_ref, w_ref, u_ref = refs[n], refs[n + 1], refs[2 * n + 2]
        xv = x_ref[...]
        u_ref[...] = (xv * _rstd(xv) * w_ref[...]).astype(BF16)
        pl.when(step == nb - 1)(finish_all)

    for a, h in zip(shards, halved):
        assert not h or a.shape[0] % 32 == 0, a.shape
    in_specs, out_specs = [_ANY] * n, [_ANY] * n
    out_shape = [jax.ShapeDtypeStruct((4,) + a.shape, a.dtype) for a in shards]
    args = list(shards)
    if nb:
        x, w = norm
        row_spec = pl.BlockSpec((NORM_ROWS, x.shape[1]), lambda i: (i, 0))
        in_specs += [row_spec, pl.BlockSpec(w.shape, lambda i: (0, 0))]
        out_specs.append(row_spec)
        out_shape.append(jax.ShapeDtypeStruct(x.shape, BF16))
        args += [x, w]
    return pl.pallas_call(
        body, name=name, grid=(nb,) if nb else (), in_specs=in_specs, out_specs=out_specs, out_shape=out_shape,
        scratch_shapes=[pltpu.SemaphoreType.DMA((3 * n,))] * 4 + [pltpu.SemaphoreType.DMA((n,))],
        compiler_params=pltpu.CompilerParams(has_side_effects=True, vmem_limit_bytes=VMEM_LIMIT_BYTES,
                                             dimension_semantics=("arbitrary",) if nb else None),
    )(*args)


_HBM = pl.BlockSpec(memory_space=pltpu.HBM)
_SEM = pl.BlockSpec(memory_space=pltpu.SEMAPHORE)
_EFFECT = pltpu.SideEffectType.DATAFLOW_SIDE_EFFECTING


def _split_copies(src, land, send, recv, loc, a, scatter):
    s = _my_chip()
    mine = pltpu.make_async_copy(src.at[s] if scatter else src, land.at[s], loc.at[a])
    pairs = []
    for k in (1, 2, 3):
        sems = dict(send_sem=send.at[3 * a + k - 1], recv_sem=recv.at[3 * a + k - 1],
                    device_id=_chip_peer(k), device_id_type=MESH_ID)
        out = pltpu.make_async_remote_copy(src_ref=src.at[s ^ k] if scatter else src, dst_ref=land.at[s], **sems)
        arriving = pltpu.make_async_remote_copy(src_ref=src.at[s ^ k] if scatter else src, dst_ref=land.at[s ^ k], **sems)
        pairs.append((out, arriving))
    return mine, pairs


def _split_start(arrs, *, scatter, after, name):
    n = len(arrs)
    zones = [lax.empty(a.shape if scatter else (4,) + a.shape, a.dtype) for a in arrs]

    def body(*refs):
        srcs, lands = refs[:n], refs[n:2 * n]
        send, recv, loc = refs[2 * n + 1:2 * n + 4]
        token = refs[-1]
        for a in range(n):
            mine, pairs = _split_copies(srcs[a], lands[a], send, recv, loc, a, scatter)
            mine.start()
            for out, _ in pairs:
                out.start()
        token[...] = jnp.zeros_like(token)

    res = pl.pallas_call(
        body, name=name,
        out_shape=(pltpu.SemaphoreType.DMA((3 * n,)), pltpu.SemaphoreType.DMA((3 * n,)), pltpu.SemaphoreType.DMA((n,)),
                   *[pltpu.HBM(z.shape, z.dtype) for z in zones], jax.ShapeDtypeStruct((8, 128), F32)),
        in_specs=[_ANY] * n + [_HBM] * n + [_ANY],
        out_specs=(_SEM, _SEM, _SEM, *([_HBM] * n), pl.BlockSpec(memory_space=pltpu.VMEM)),
        input_output_aliases={n + i: 3 + i for i in range(n)},
        compiler_params=pltpu.CompilerParams(has_side_effects=_EFFECT),
    )(*arrs, *[pltpu.with_memory_space_constraint(z, pltpu.HBM) for z in zones], after)
    return res[:3], list(res[3:3 + n]), res[-1]


def _split_wait(sems, src, land, a, *, scatter, after, name):
    def body(src_ref, land_ref, send, recv, loc, after_ref, land_out):
        mine, pairs = _split_copies(src_ref, land_ref, send, recv, loc, a, scatter)
        mine.wait()
        for out, arriving in pairs:
            out.wait_send()
            arriving.wait_recv()

    return pl.pallas_call(
        body, name=name, out_shape=pltpu.HBM(land.shape, land.dtype),
        in_specs=[_ANY, _HBM, _SEM, _SEM, _SEM, _ANY], out_specs=_HBM, input_output_aliases={1: 0},
        compiler_params=pltpu.CompilerParams(has_side_effects=_EFFECT),
    )(src, land, *sems, after)


def _sibling_copies(srcs, lands, send, recv):
    sib = (lax.axis_index("x"), lax.axis_index("y"), 1 - lax.axis_index("c"))
    return [pltpu.make_async_remote_copy(src_ref=srcs[a], dst_ref=lands[a], send_sem=send.at[a], recv_sem=recv.at[a],
                                         device_id=sib, device_id_type=MESH_ID) for a in range(len(srcs))]


def _swap_start(arrs, *, after, name):
    n = len(arrs)
    zones = [lax.empty(a.shape, a.dtype) for a in arrs]

    def body(*refs):
        for cp in _sibling_copies(refs[:n], refs[n:2 * n], refs[2 * n + 1], refs[2 * n + 2]):
            cp.start()

    res = pl.pallas_call(
        body, name=name,
        out_shape=(pltpu.SemaphoreType.DMA((n,)), pltpu.SemaphoreType.DMA((n,)), *[pltpu.HBM(z.shape, z.dtype) for z in zones]),
        in_specs=[_ANY] * n + [_HBM] * n + [_ANY], out_specs=(_SEM, _SEM, *([_HBM] * n)),
        input_output_aliases={n + i: 2 + i for i in range(n)},
        compiler_params=pltpu.CompilerParams(has_side_effects=_EFFECT),
    )(*arrs, *[pltpu.with_memory_space_constraint(z, pltpu.HBM) for z in zones], after)
    return res[:2], list(res[2:])


def _swap_wait(sems, srcs, lands, *, after, name):
    n = len(srcs)

    def body(*refs):
        for cp in _sibling_copies(refs[:n], refs[n:2 * n], refs[2 * n], refs[2 * n + 1]):
            cp.wait_send()
            cp.wait_recv()

    res = pl.pallas_call(
        body, name=name, out_shape=tuple(pltpu.HBM(a.shape, a.dtype) for a in lands),
        in_specs=[_ANY] * n + [_HBM] * n + [_SEM, _SEM, _ANY], out_specs=tuple([_HBM] * n),
        input_output_aliases={n + i: i for i in range(n)},
        compiler_params=pltpu.CompilerParams(has_side_effects=_EFFECT),
    )(*srcs, *lands, *sems, after)
    return list(res)


def _all_gather_devices(v, *, name):
    r = v.shape[0]

    def body(v_ref, out_ref, send, recv):
        x, y, c = lax.axis_index("x"), lax.axis_index("y"), lax.axis_index("c")
        me = 4 * x + 2 * y + c
        out_ref[me] = v_ref[...]
        cps = []
        for k in range(1, 8):
            peer = (x ^ (k >> 2), y ^ ((k >> 1) & 1), c ^ (k & 1))
            cp = pltpu.make_async_remote_copy(src_ref=v_ref, dst_ref=out_ref.at[me], send_sem=send.at[k - 1],
                                              recv_sem=recv.at[k - 1], device_id=peer, device_id_type=MESH_ID)
            cp.start()
            cps.append(cp)
        for k, cp in enumerate(cps, start=1):
            cp.wait_send()
            pltpu.make_async_remote_copy(src_ref=v_ref, dst_ref=out_ref.at[me ^ k], send_sem=send.at[k - 1],
                                         recv_sem=recv.at[k - 1], device_id=(x, y, c), device_id_type=MESH_ID).wait_recv()

    vm = pl.BlockSpec(memory_space=pltpu.VMEM)
    return pl.pallas_call(
        body, name=name, in_specs=[vm], out_specs=vm, out_shape=jax.ShapeDtypeStruct((8, r, 128), F32),
        scratch_shapes=[pltpu.SemaphoreType.DMA((7,)), pltpu.SemaphoreType.DMA((7,))],
        compiler_params=pltpu.CompilerParams(has_side_effects=True),
    )(v)


def _row_tile(r, target):
    best = None
    for t in range(16, min(target, r) + 1, 16):
        if r % t == 0:
            best = t
    return best or r


def _sum_slots(buf, *, name, tr=384):
    S, r, c = buf.shape
    tr = _row_tile(r, tr)

    def body(b_ref, o_ref):
        acc = b_ref[0].astype(F32)
        for j in range(1, S):
            acc = acc + b_ref[j].astype(F32)
        o_ref[...] = acc

    return pl.pallas_call(
        body, name=name, grid=(r // tr,), in_specs=[pl.BlockSpec((S, tr, c), lambda i: (0, i, 0))],
        out_specs=pl.BlockSpec((tr, c), lambda i: (i, 0)), out_shape=jax.ShapeDtypeStruct((r, c), F32),
        compiler_params=_cparams("parallel"),
    )(buf)


ADAMW_BLOCK_ELEMS = 1 << 18


def _adamw(w, gs, m, v, *, name, tr=256, layer=None, stack=None):
    r, c = w.shape[-2:]
    tr = _row_tile(r, min(tr, max(16, ADAMW_BLOCK_ELEMS // c)))
    bc1 = 1.0 - ADAM_B1 ** ADAM_STEP
    bc2 = 1.0 - ADAM_B2 ** ADAM_STEP
    ng = len(gs)

    def body(*refs):
        w_ref, m_ref, v_ref = refs[0], refs[1 + ng], refs[2 + ng]
        g_ref, d_ref, mo_ref, vo_ref = refs[-4:]
        gg = refs[1][...] if ng == 1 else refs[1][...] + refs[2][...]
        mn = ADAM_B1 * m_ref[...] + (1.0 - ADAM_B1) * gg
        vn = ADAM_B2 * v_ref[...] + (1.0 - ADAM_B2) * (gg * gg)
        g_ref[...] = gg
        mo_ref[...] = mn
        vo_ref[...] = vn
        d_ref[...] = -ADAM_LR * ((mn / bc1) / (jnp.sqrt(vn / bc2) + ADAM_EPS) + ADAM_WD * w_ref[...])

    spec = pl.BlockSpec((tr, c), lambda i: (i, 0))
    if layer is None:
        wspec, shape = spec, (r, c)
    else:
        wspec, shape = pl.BlockSpec((None, tr, c), lambda i: (layer, i, 0)), (2, r, c)
    in_specs = [wspec] + [spec] * ng + [wspec, wspec]
    args, alias = [w, *gs, m, v], {}
    if stack is not None:
        in_specs += [_ANY] * 4
        alias = {len(args) + n: n for n in range(4)}
        args += list(stack)
    return pl.pallas_call(body, name=name, grid=(r // tr,), in_specs=in_specs, out_specs=[wspec] * 4,
                          out_shape=[jax.ShapeDtypeStruct(shape, F32)] * 4, input_output_aliases=alias,
                          compiler_params=_cparams("parallel"))(*args)


def _pack(vecs, rows):
    flat = jnp.concatenate([v.reshape(-1).astype(F32) for v in vecs])
    return jnp.pad(flat, (0, rows * 128 - flat.shape[0])).reshape(rows, 128)


def _unpack(packed, shapes):
    flat = packed.reshape(-1)
    out, off = [], 0
    for s in shapes:
        n = math.prod(s)
        out.append(flat[off:off + n].reshape(s))
        off += n
    return out


def _pack_rows(shapes):
    n = sum(math.prod(s) for s in shapes)
    return -(-n // 1024) * 8


def kernel(x, norm_mix_pre, norm_mix_post, norm_ffn_pre, norm_ffn_post, ret_w_in, ret_gn_w, ret_w_out, ssd_w_in, ssd_conv_w, ssd_conv_b, ssd_dt_bias, ssd_a_log, ssd_d, ssd_norm_w, ssd_w_out, mlp_w_up, mlp_w_down, loss_target, m_norm_mix_pre, m_norm_mix_post, m_norm_ffn_pre, m_norm_ffn_post, m_ret_w_in, m_ret_gn_w, m_ret_w_out, m_ssd_w_in, m_ssd_conv_w, m_ssd_conv_b, m_ssd_dt_bias, m_ssd_a_log, m_ssd_d, m_ssd_norm_w, m_ssd_w_out, m_mlp_w_up, m_mlp_w_down, v_norm_mix_pre, v_norm_mix_post, v_norm_ffn_pre, v_norm_ffn_post, v_ret_w_in, v_ret_gn_w, v_ret_w_out, v_ssd_w_in, v_ssd_conv_w, v_ssd_conv_b, v_ssd_dt_bias, v_ssd_a_log, v_ssd_d, v_ssd_norm_w, v_ssd_w_out, v_mlp_w_up, v_mlp_w_down):
    T, D = x.shape[1], x.shape[2]
    H = D // RET_DK
    d_inner = 2 * D
    R = d_inner // SSD_P // SSD_G
    RP = R * SSD_P
    n_heads = SSD_G * R
    conv_dim = d_inner + 2 * SSD_G * SSD_N
    n_main = d_inner + conv_dim
    C = min(256, T)
    chip = _my_chip()
    xs, tgt = x[0], loss_target[0]

    conv_sh = ssd_conv_w.shape[2]
    small_shapes = [(SSD_CONV_W, conv_sh), (conv_sh,), (ssd_norm_w.shape[1],)]
    small_rows = _pack_rows(small_shapes)
    shards = [ret_w_in[0].T.astype(BF16), ret_w_out[0].astype(BF16), ssd_w_in[0].T.astype(BF16), ssd_w_out[0].astype(BF16),
              mlp_w_up[0].T.astype(BF16), mlp_w_up[1].T.astype(BF16), mlp_w_down[0].astype(BF16), mlp_w_down[1].astype(BF16)]
    (ret_in_g, small_g, u0) = _all_gather_chips(
        [shards[0], _pack([ssd_conv_w[0], ssd_conv_b[0], ssd_norm_w[0]], small_rows)], [True, False], name="gather_first",
        norm=(xs, norm_mix_pre[0:1]))

    def full(g):
        return g.reshape(4 * g.shape[1], g.shape[2])

    def start_gather(idx, after, name):
        sems, zones, tok = _split_start([shards[i] for i in idx], scatter=False, after=after, name=name)
        return {i: (sems, shards[i], zones[n], n) for n, i in enumerate(idx)}, tok

    def arrived(stage, i, after, name):
        sems, src, zone, n = stage[i]
        return full(_split_wait(sems, src, zone, n, scatter=False, after=after, name=name))

    ret_in_t = full(ret_in_g)
    sm = [_unpack(small_g[j], small_shapes) for j in range(4)]
    conv_w = jnp.concatenate([sm[j][0] for j in range(4)], axis=1)
    conv_b = jnp.concatenate([sm[j][1] for j in range(4)])[None, :]
    norm_w = jnp.concatenate([sm[j][2] for j in range(4)])[None, :]

    gb = SSD_G * SSD_N
    ssd_prm = (ssd_dt_bias.reshape(SSD_G, 1, R), ssd_dt_bias.reshape(SSD_G, R, 1),
               ssd_a_log.reshape(SSD_G, 1, R), ssd_a_log.reshape(SSD_G, R, 1), ssd_d.reshape(SSD_G, 1, R),
               conv_w[:, :d_inner], conv_w[:, d_inner:d_inner + gb], conv_w[:, d_inner + gb:],
               conv_b[:, :d_inner], conv_b[:, d_inner:d_inner + gb], conv_b[:, d_inner + gb:],
               norm_w, jnp.asarray(np.kron(np.eye(R), np.ones((1, SSD_P))), F32))
    ret_consts = _ret_consts(T, C, H)

    stage1, tok = start_gather((1, 4), ret_in_g, "gather_start1")
    proj = _matmul(u0, ret_in_t, "nt", out_dtype=F32, name="ret_in", after=tok)
    stage2, tok = start_gather((6, 2), proj, "gather_start2")
    y_ret, st_ret = _ret_fwd(proj, ret_gn_w, ret_consts, C=C, name="ret_fwd")
    ret_out = arrived(stage1, 1, y_ret, "gather_wait_ret_out")
    m0 = _matmul(y_ret, ret_out, "nn", out_dtype=BF16, name="ret_out", after=tok)
    h1, u1 = _rms_post_pre(xs, m0, norm_mix_post[0:1], norm_ffn_pre[0:1], name="post_pre1")
    up_t0 = arrived(stage1, 4, u1, "gather_wait_up0")
    a0, hh0 = _matmul(u1, up_t0, "nt", out_dtype=BF16, name="mlp_up0", epi="relu2")
    stage3, tok = start_gather((3, 5, 7), hh0, "gather_start3")
    down0 = arrived(stage2, 6, hh0, "gather_wait_down0")
    f0 = _matmul(hh0, down0, "nn", out_dtype=BF16, name="mlp_down0", after=tok)
    h2, u2 = _rms_post_pre(h1, f0, norm_ffn_post[0:1], norm_mix_pre[1:2], name="post_pre2")
    ssd_in_t = arrived(stage2, 2, u2, "gather_wait_ssd_in")
    pm, pdt = _matmul(u2, ssd_in_t, "nt", out_dtype=F32, name="ssd_in", tail=(n_heads,))
    dt_g = pdt.reshape(T, SSD_G, R).transpose(1, 0, 2)
    dtT_g = pdt.reshape(T, SSD_G, R).transpose(1, 2, 0)
    y_ssd, st_ssd = _ssd_fwd(pm, dt_g, dtT_g, ssd_prm, C=C, R=R, name="ssd_fwd")
    ssd_out = arrived(stage3, 3, y_ssd, "gather_wait_ssd_out")
    m1 = _matmul(y_ssd, ssd_out, "nn", out_dtype=BF16, name="ssd_out")
    h3, u3 = _rms_post_pre(h2, m1, norm_mix_post[1:2], norm_ffn_pre[1:2], name="post_pre3")
    up_t1 = arrived(stage3, 5, u3, "gather_wait_up1")
    a1, hh1 = _matmul(u3, up_t1, "nt", out_dtype=BF16, name="mlp_up1", epi="relu2")
    down1 = arrived(stage3, 7, hh1, "gather_wait_down1")
    f1 = _matmul(hh1, down1, "nn", out_dtype=BF16, name="mlp_down1")
    up_t, down = (up_t0, up_t1), (down0, down1)
    dh4, sq = _rms_post_loss(h3, f1, norm_ffn_post[1:2], tgt, name="post_loss")
    loss = lax.psum(sq[0, 0], MESH_AXES) * (0.5 / D)

    in_flight = []

    def send_grad(g, name):
        part = g if g.ndim == 3 else g.reshape(4, g.shape[0] // 4, g.shape[1])
        sems, zones, tok = _split_start([part], scatter=True, after=part, name=f"scatter_start_{name}")
        in_flight.append((name, sems, part, zones[0]))
        return tok

    retired = []

    def retire(name, after):
        nm, sems, src, zone = in_flight.pop(0)
        assert nm == name
        part = _sum_slots(_split_wait(sems, src, zone, 0, scatter=True, after=after, name=f"scatter_wait_{nm}"),
                          name=f"sum_chips_{nm}")
        sw_sems, zones = _swap_start([part], after=part, name=f"swap_start_{nm}")
        retired.append((nm, part, sw_sems, zones))
        return zones[0]

    def mlp_bwd(i, df, u, a, hh):
        tok = send_grad(_matmul(hh, df, "tn", out_dtype=BF16, name=f"mlp_down_wg{i}"), f"down{i}")
        da = _matmul(df, down[i], "nt", out_dtype=BF16, name=f"mlp_down_dg{i}", epi="drelu2", extra=a, after=tok)
        tok = send_grad(_matmul(u, da, "tn", out_dtype=BF16, name=f"mlp_up_wg{i}", col_parts=4), f"up{i}")
        return _matmul(da, up_t[i], "nn", out_dtype=BF16, name=f"mlp_up_dg{i}", after=tok)

    df1, d_nfpost1 = _rms_post_bwd(f1, norm_ffn_post[1:2], dh4, name="post_bwd_ffn1")
    du3 = mlp_bwd(1, df1, u3, a1, hh1)
    dh3, dm1, d_nfp1, d_nmpost1 = _rms_pre_post_bwd(h3, norm_ffn_pre[1:2], du3, dh4, m1, norm_mix_post[1:2],
                                                    name="pre_bwd_ffn1_post_bwd_mix1")
    tok = send_grad(_matmul(y_ssd, dm1, "tn", out_dtype=BF16, name="ssd_out_wg"), "ssd_out")
    dy_ssd = _matmul(dm1, ssd_out, "nt", out_dtype=F32, name="ssd_out_dg", after=tok)
    (dpm, ddt_g, d_bias, d_alog, d_dskip, dcwx, dcwb, dcwc, dcbx, dcbb, dcbc, d_normw) = _ssd_bwd(
        pm, dt_g, dtT_g, ssd_prm, st_ssd, dy_ssd, C=C, R=R, name="ssd_bwd")
    dpdt = ddt_g.transpose(1, 0, 2).reshape(T, n_heads).astype(BF16)
    tok = send_grad(jnp.concatenate([_matmul(dpm, u2, "tn", out_dtype=BF16, name="ssd_in_wg"),
                                     _matmul(dpdt, u2, "tn", out_dtype=BF16, name="ssd_in_dt_wg")], axis=0), "ssd_in")
    du2 = _matmul(dpm, ssd_in_t, "nn", out_dtype=BF16, name="ssd_in_dg", after=tok, tail=(n_heads, dpdt))
    dh2, df0, d_nmp1, d_nfpost0 = _rms_pre_post_bwd(h2, norm_mix_pre[1:2], du2, dh3, f0, norm_ffn_post[0:1],
                                                    name="pre_bwd_mix1_post_bwd_ffn0")
    du1 = mlp_bwd(0, df0, u1, a0, hh0)
    dh1, dm0, d_nfp0, d_nmpost0 = _rms_pre_post_bwd(h1, norm_ffn_pre[0:1], du1, dh2, m0, norm_mix_post[0:1],
                                                    name="pre_bwd_ffn0_post_bwd_mix0")
    tok = send_grad(_matmul(y_ret, dm0, "tn", out_dtype=BF16, name="ret_out_wg"), "ret_out")
    dy_ret = _matmul(dm0, ret_out, "nt", out_dtype=F32, name="ret_out_dg", after=tok)
    dproj, d_gn = _ret_bwd(proj, ret_gn_w, ret_consts, st_ret, dy_ret, C=C, name="ret_bwd")
    tok = send_grad(_matmul(u0, dproj, "tn", out_dtype=BF16, name="ret_in_wg", col_parts=4), "ret_in")
    du0 = _matmul(dproj, ret_in_t, "nn", out_dtype=BF16, name="ret_in_dg", after=tok)
    grad_x, d_nmp0 = _rms_pre_bwd(xs, norm_mix_pre[0:1], du0, dh1, name="pre_bwd_mix0")

    prev = grad_x
    for nm in ("down1", "up1", "ssd_out", "ssd_in", "down0", "up0", "ret_out", "ret_in"):
        prev = retire(nm, prev)

    def upd(w, gs, m, v, name):
        shp = w.shape
        w2, m2, v2 = (t.reshape(-1, shp[-1]) for t in (w, m, v))
        return tuple(t.reshape(shp) for t in _adamw(w2, gs, m2, v2, name=name))

    def upd_t(w, gs, m, v, name):
        return tuple(t.T[None] for t in _adamw(w[0].T, gs, m[0].T, v[0].T, name=name))

    def upd_layer(layer):
        def fn(w, gs, m, v, name):
            return tuple(_adamw(w, gs, m, v, name=name, layer=layer, stack=res.get(out_of[name[len("adamw_"):]])))
        return fn

    out_of = {"ret_in": "ret_w_in", "ret_out": "ret_w_out", "ssd_in": "ssd_w_in", "ssd_out": "ssd_w_out",
              "up0": "mlp_w_up", "up1": "mlp_w_up", "down0": "mlp_w_down", "down1": "mlp_w_down"}
    todo = {"ret_in": (upd, ret_w_in, m_ret_w_in, v_ret_w_in), "ret_out": (upd, ret_w_out, m_ret_w_out, v_ret_w_out),
            "ssd_in": (upd_t, ssd_w_in, m_ssd_w_in, v_ssd_w_in), "ssd_out": (upd, ssd_w_out, m_ssd_w_out, v_ssd_w_out),
            "up0": (upd_layer(0), mlp_w_up, m_mlp_w_up, v_mlp_w_up), "up1": (upd_layer(1), mlp_w_up, m_mlp_w_up, v_mlp_w_up),
            "down0": (upd_layer(0), mlp_w_down, m_mlp_w_down, v_mlp_w_down),
            "down1": (upd_layer(1), mlp_w_down, m_mlp_w_down, v_mlp_w_down)}
    res = {}
    for nm, mine, sems, zones in retired:
        other = _swap_wait(sems, [mine], zones, after=prev, name=f"swap_wait_{nm}")[0]
        fn, w, m, v = todo[nm]
        res[out_of[nm]] = fn(w, [mine, other], m, v, f"adamw_{nm}")
        prev = res[out_of[nm]][0]

    d_conv_w = jnp.concatenate([dcwx, dcwb, dcwc], axis=1)
    d_conv_b = jnp.concatenate([dcbx, dcbb, dcbc], axis=1)
    small_grads = [jnp.concatenate([d_nmp0, d_nmp1]), jnp.concatenate([d_nmpost0, d_nmpost1]),
                   jnp.concatenate([d_nfp0, d_nfp1]), jnp.concatenate([d_nfpost0, d_nfpost1]),
                   d_gn, d_bias.reshape(1, n_heads), d_alog.reshape(1, n_heads), d_dskip.reshape(1, n_heads),
                   d_conv_w, d_conv_b, d_normw]
    sg_shapes = [g.shape for g in small_grads]
    sg_rows = _pack_rows(sg_shapes)
    everyone = _all_gather_devices(_pack(small_grads, sg_rows), name="gather_small_grads")
    sg = _unpack(_sum_slots(everyone, name="sum_small_grads", tr=sg_rows), sg_shapes)
    (g_nmp, g_nmpost, g_nfp, g_nfpost, g_gn, g_bias, g_alog, g_dskip, g_cw_full, g_cb_full, g_nw_full) = sg
    g_cw = lax.dynamic_slice_in_dim(g_cw_full, chip * conv_sh, conv_sh, axis=1)[None]
    g_cb = lax.dynamic_slice_in_dim(g_cb_full, chip * conv_sh, conv_sh, axis=1)
    nw_sh = ssd_norm_w.shape[1]
    g_nw = lax.dynamic_slice_in_dim(g_nw_full, chip * nw_sh, nw_sh, axis=1)
    small = [("norm_mix_pre", norm_mix_pre, g_nmp, m_norm_mix_pre, v_norm_mix_pre),
             ("norm_mix_post", norm_mix_post, g_nmpost, m_norm_mix_post, v_norm_mix_post),
             ("norm_ffn_pre", norm_ffn_pre, g_nfp, m_norm_ffn_pre, v_norm_ffn_pre),
             ("norm_ffn_post", norm_ffn_post, g_nfpost, m_norm_ffn_post, v_norm_ffn_post),
             ("ret_gn_w", ret_gn_w, g_gn, m_ret_gn_w, v_ret_gn_w),
             ("ssd_conv_w", ssd_conv_w, g_cw, m_ssd_conv_w, v_ssd_conv_w),
             ("ssd_conv_b", ssd_conv_b, g_cb, m_ssd_conv_b, v_ssd_conv_b),
             ("ssd_dt_bias", ssd_dt_bias, g_bias, m_ssd_dt_bias, v_ssd_dt_bias),
             ("ssd_a_log", ssd_a_log, g_alog, m_ssd_a_log, v_ssd_a_log),
             ("ssd_d", ssd_d, g_dskip, m_ssd_d, v_ssd_d),
             ("ssd_norm_w", ssd_norm_w, g_nw, m_ssd_norm_w, v_ssd_norm_w)]
    sw_shapes = [w.shape for _, w, _, _, _ in small]
    sw_rows = _pack_rows(sw_shapes)
    packs = [_pack([t[j] for t in small], sw_rows) for j in (1, 2, 3, 4)]
    _, d_p, m_p, v_p = _adamw(packs[0], [packs[1]], packs[2], packs[3], name="adamw_small", tr=sw_rows)
    d_s, m_s, v_s = _unpack(d_p, sw_shapes), _unpack(m_p, sw_shapes), _unpack(v_p, sw_shapes)
    for j, (nm, w, g, _, _) in enumerate(small):
        res[nm] = (g.reshape(w.shape), d_s[j], m_s[j], v_s[j])

    order = ["norm_mix_pre", "norm_mix_post", "norm_ffn_pre", "norm_ffn_post", "ret_w_in", "ret_gn_w", "ret_w_out",
             "ssd_w_in", "ssd_conv_w", "ssd_conv_b", "ssd_dt_bias", "ssd_a_log", "ssd_d", "ssd_norm_w", "ssd_w_out",
             "mlp_w_up", "mlp_w_down"]
    return (loss, grad_x[None], *[res[n][0] for n in order], *[res[n][1] for n in order],
            *[res[n][2] for n in order], *[res[n][3] for n in order])
```

```python
import math

import numpy as np
import jax
import jax.numpy as jnp
from jax import lax
from jax.experimental import pallas as pl
from jax.experimental.pallas import tpu as pltpu

F32 = jnp.float32
BF16 = jnp.bfloat16
VMEM_LIMIT_BYTES = 56 * 1024 * 1024
MESH_AXES = ("x", "y", "c")
MESH_ID = pl.DeviceIdType.MESH

RMS_EPS = 1e-6
GN_EPS = 1e-5
RET_DK = 256
RET_DV = 512
ROPE_BASE = 10000.0
REF_CHUNK = 64
SSD_P = 64
SSD_N = 128
SSD_G = 8
SSD_CONV_W = 4
ADAM_LR, ADAM_B1, ADAM_B2, ADAM_EPS, ADAM_WD, ADAM_STEP = 0.001, 0.9, 0.999, 1e-08, 0.01, 10

NN = (((1,), (0,)), ((), ()))
NT = (((1,), (1,)), ((), ()))
TN = (((0,), (0,)), ((), ()))


def _cparams(*sem):
    return pltpu.CompilerParams(dimension_semantics=sem, vmem_limit_bytes=VMEM_LIMIT_BYTES)


def _dot(a, b, dims=NN):
    return lax.dot_general(a.astype(BF16), b.astype(BF16), dims, preferred_element_type=F32)


def _split_bf16(x, terms):
    parts, rest = [], x
    for _ in range(terms):
        p = rest.astype(BF16)
        parts.append(p)
        rest = rest - p.astype(F32)
    return parts


def _dot_sel(a, b, dims=NN, *, split, terms=3):
    if split == "a":
        sel = b.astype(BF16)
        return sum(lax.dot_general(p, sel, dims, preferred_element_type=F32) for p in _split_bf16(a, terms))
    sel = a.astype(BF16)
    return sum(lax.dot_general(sel, p, dims, preferred_element_type=F32) for p in _split_bf16(b, terms))


def _sigmoid(x):
    return 1.0 / (1.0 + jnp.exp(-x))


def _colsum(x):
    return jnp.sum(x, axis=0, keepdims=True)


MM_TILE = 1024
MM_FULL_K = 2048


def _mm_tiles(M, N, K):
    if K <= MM_FULL_K:
        return min(M, 2 * MM_TILE), min(N, MM_TILE), K
    return min(M, MM_TILE), min(N, 2 * MM_TILE), 2 * MM_TILE if K % (2 * MM_TILE) == 0 else MM_TILE


def _matmul(a, b, mode, *, out_dtype, name, epi=None, extra=None, after=None, col_parts=None, tail=None):
    nt_ = tail[0] if tail else 0
    if mode == "nn":
        (M, K), (K2, N) = a.shape, (b.shape[0] - nt_, b.shape[1])
    elif mode == "nt":
        (M, K), (N, K2) = a.shape, (b.shape[0] - nt_, b.shape[1])
    else:
        (K, M), (K2, N) = a.shape, b.shape
    assert K == K2, (a.shape, b.shape, mode)
    tm, tn, tk = _mm_tiles(M, N, K)
    if col_parts:
        tm, tn = min(M, 2 * MM_TILE), min(tn, MM_TILE)
        while (N // col_parts) % tn:
            tn //= 2
    assert M % tm == 0 and N % tn == 0 and K % tk == 0, (M, N, K, tm, tn, tk)
    nk = K // tk
    if mode == "tn":
        a_spec = pl.BlockSpec((tk, tm), lambda i, j, k: (k, i))
    else:
        a_spec = pl.BlockSpec((tm, tk), lambda i, j, k: (i, k))
    if mode == "nt":
        b_spec = pl.BlockSpec((tn, tk), lambda i, j, k: (j, k))
    else:
        b_spec = pl.BlockSpec((tk, tn), lambda i, j, k: (k, j))
    dims = {"nn": NN, "nt": NT, "tn": TN}[mode]
    o_spec = pl.BlockSpec((tm, tn), lambda i, j, k: (i, j))
    out_dims = (M, N)
    if col_parts:
        per = N // col_parts // tn
        o_spec = pl.BlockSpec((None, tm, tn), lambda i, j, k: (j // per, i, j % per))
        out_dims = (col_parts, M, N // col_parts)
    has_extra = epi in ("drelu2", "add")
    n_out = 2 if epi == "relu2" else 1

    in_specs = [a_spec, b_spec] + ([o_spec] if has_extra else [])
    args = [a, b] + ([extra] if has_extra else [])
    if after is not None:
        in_specs.append(pl.BlockSpec(after.shape, lambda i, j, k: (0, 0)))
        args.append(after)
    n_plain = len(args)
    out_specs = [o_spec] * n_out
    out_shape = [jax.ShapeDtypeStruct(out_dims, out_dtype)] * n_out
    if tail and mode == "nt":
        assert nk == 1 and N % nt_ == 0
        in_specs.append(pl.BlockSpec((nt_, tk), lambda i, j, k: (N // nt_, 0)))
        args.append(b)
        out_specs.append(pl.BlockSpec((tm, nt_), lambda i, j, k: (i, 0)))
        out_shape.append(jax.ShapeDtypeStruct((M, nt_), F32))
    elif tail:
        assert mode == "nn" and K % nt_ == 0
        in_specs += [pl.BlockSpec((tm, nt_), lambda i, j, k: (i, 0)), pl.BlockSpec((nt_, tn), lambda i, j, k: (K // nt_, j))]
        args += [tail[1], b]
    n_in = len(args)

    def body(*refs):
        a_ref, b_ref = refs[0], refs[1]
        e_ref = refs[2] if has_extra else None
        outs = refs[n_in:n_in + n_out]

        def finish(r):
            if tail and mode == "nn":
                r = r + _dot(refs[n_plain][...], refs[n_plain + 1][...])
            if epi is None:
                outs[0][...] = r.astype(outs[0].dtype)
            elif epi == "relu2":
                outs[0][...] = r.astype(outs[0].dtype)
                h = jnp.maximum(r, 0.0)
                outs[1][...] = (h * h).astype(outs[1].dtype)
            elif epi == "drelu2":
                av = jnp.maximum(e_ref[...].astype(F32), 0.0)
                outs[0][...] = (r * (2.0 * av)).astype(outs[0].dtype)
            else:
                outs[0][...] = (r + e_ref[...].astype(F32)).astype(outs[0].dtype)

        if tail and mode == "nt":
            @pl.when(pl.program_id(1) == 0)
            def _():
                refs[n_in + n_out][...] = _dot(a_ref[...], refs[n_plain][...], NT)

        if nk == 1:
            finish(_dot(a_ref[...], b_ref[...], dims))
            return
        acc = refs[-1]
        k = pl.program_id(2)

        @pl.when(k == 0)
        def _():
            acc[...] = jnp.zeros_like(acc)

        acc[...] += _dot(a_ref[...], b_ref[...], dims)

        @pl.when(k == nk - 1)
        def _():
            finish(acc[...])

    res = pl.pallas_call(
        body, name=name, grid=(M // tm, N // tn, nk), in_specs=in_specs, out_specs=out_specs,
        out_shape=out_shape, scratch_shapes=[pltpu.VMEM((tm, tn), F32)] if nk > 1 else [],
        compiler_params=_cparams("parallel", "arbitrary" if tail and mode == "nt" else "parallel", "arbitrary"),
    )(*args)
    return res if len(res) > 1 else res[0]


def _rstd(x):
    return lax.rsqrt(jnp.mean(x * x, axis=-1, keepdims=True) + RMS_EPS)


def _row_call(body, ins, outs_shape, *, name, rows, tr, acc_outs=()):
    tr = min(tr, rows)
    assert rows % tr == 0
    in_specs = []
    for arr, blocked in ins:
        if blocked:
            in_specs.append(pl.BlockSpec((tr, arr.shape[1]), lambda i: (i, 0)))
        else:
            in_specs.append(pl.BlockSpec(arr.shape, lambda i: (0, 0)))
    out_specs = []
    for n, s in enumerate(outs_shape):
        if n in acc_outs:
            out_specs.append(pl.BlockSpec(s.shape, lambda i: (0, 0)))
        else:
            out_specs.append(pl.BlockSpec((tr, s.shape[1]), lambda i: (i, 0)))
    return pl.pallas_call(
        body, name=name, grid=(rows // tr,), in_specs=in_specs, out_specs=out_specs, out_shape=outs_shape,
        compiler_params=_cparams("arbitrary" if acc_outs else "parallel"),
    )(*[a for a, _ in ins])


def _rms_post_pre(h, m, w_post, w_pre, *, name):
    T, D = h.shape

    def body(h_ref, m_ref, wp_ref, wn_ref, hn_ref, u_ref):
        mm = m_ref[...].astype(F32)
        hn = h_ref[...] + mm * _rstd(mm) * wp_ref[...]
        hn_ref[...] = hn
        u_ref[...] = (hn * _rstd(hn) * wn_ref[...]).astype(BF16)

    return _row_call(body, [(h, True), (m, True), (w_post, False), (w_pre, False)],
                     [jax.ShapeDtypeStruct((T, D), F32), jax.ShapeDtypeStruct((T, D), BF16)], name=name, rows=T, tr=256)


def _rms_post_loss(h, m, w_post, tgt, *, name):
    T, D = h.shape

    def body(h_ref, m_ref, wp_ref, t_ref, dh_ref, loss_ref):
        @pl.when(pl.program_id(0) == 0)
        def _():
            loss_ref[...] = jnp.zeros_like(loss_ref)

        mm = m_ref[...].astype(F32)
        err = h_ref[...] + mm * _rstd(mm) * wp_ref[...] - t_ref[...]
        dh_ref[...] = err * (1.0 / D)
        loss_ref[...] += _colsum(jnp.sum(err * err, axis=1, keepdims=True))

    return _row_call(body, [(h, True), (m, True), (w_post, False), (tgt, True)],
                     [jax.ShapeDtypeStruct((T, D), F32), jax.ShapeDtypeStruct((1, 1), F32)],
                     name=name, rows=T, tr=256, acc_outs=(1,))


def _rms_bwd_vals(x, w, dy):
    r = _rstd(x)
    xh = x * r
    g = dy * w
    dx = r * (g - xh * jnp.mean(g * xh, axis=-1, keepdims=True))
    return dx, _colsum(dy * xh)


def _rms_post_bwd(m, w_post, dh, *, name):
    T, D = m.shape

    def body(m_ref, w_ref, dh_ref, dm_ref, dw_ref):
        @pl.when(pl.program_id(0) == 0)
        def _():
            dw_ref[...] = jnp.zeros_like(dw_ref)

        dx, dw = _rms_bwd_vals(m_ref[...].astype(F32), w_ref[...], dh_ref[...])
        dm_ref[...] = dx.astype(BF16)
        dw_ref[...] += dw

    return _row_call(body, [(m, True), (w_post, False), (dh, True)],
                     [jax.ShapeDtypeStruct((T, D), BF16), jax.ShapeDtypeStruct((1, D), F32)],
                     name=name, rows=T, tr=256, acc_outs=(1,))


def _rms_pre_bwd(h, w_pre, du, dh_out, *, name):
    T, D = h.shape

    def body(h_ref, w_ref, du_ref, dho_ref, dh_ref, dw_ref):
        @pl.when(pl.program_id(0) == 0)
        def _():
            dw_ref[...] = jnp.zeros_like(dw_ref)

        dx, dw = _rms_bwd_vals(h_ref[...], w_ref[...], du_ref[...].astype(F32))
        dh_ref[...] = dho_ref[...] + dx
        dw_ref[...] += dw

    return _row_call(body, [(h, True), (w_pre, False), (du, True), (dh_out, True)],
                     [jax.ShapeDtypeStruct((T, D), F32), jax.ShapeDtypeStruct((1, D), F32)],
                     name=name, rows=T, tr=256, acc_outs=(1,))


def _rms_pre_post_bwd(h, w_pre, du, dh_out, m_prev, w_post_prev, *, name):
    T, D = h.shape

    def body(h_ref, w_ref, du_ref, dho_ref, m_ref, wp_ref, dh_ref, dm_ref, dw_ref, dwp_ref):
        @pl.when(pl.program_id(0) == 0)
        def _():
            dw_ref[...] = jnp.zeros_like(dw_ref)
            dwp_ref[...] = jnp.zeros_like(dwp_ref)

        dx, dw = _rms_bwd_vals(h_ref[...], w_ref[...], du_ref[...].astype(F32))
        dh = dho_ref[...] + dx
        dh_ref[...] = dh
        dw_ref[...] += dw
        dm, dwp = _rms_bwd_vals(m_ref[...].astype(F32), wp_ref[...], dh)
        dm_ref[...] = dm.astype(BF16)
        dwp_ref[...] += dwp

    return _row_call(body, [(h, True), (w_pre, False), (du, True), (dh_out, True), (m_prev, True), (w_post_prev, False)],
                     [jax.ShapeDtypeStruct((T, D), F32), jax.ShapeDtypeStruct((T, D), BF16),
                      jax.ShapeDtypeStruct((1, D), F32), jax.ShapeDtypeStruct((1, D), F32)],
                     name=name, rows=T, tr=256, acc_outs=(2, 3))


def _ret_consts(T, C, H):
    lg = np.log1p(-np.exp2(-5.0 - np.arange(H, dtype=np.float64)))
    idx = np.arange(C, dtype=np.float64)
    dist = np.abs(idx[:, None] - idx[None, :])
    vis = (idx[None, :] // REF_CHUNK) <= (idx[:, None] // REF_CHUNK)
    mask = np.exp(dist[None] * lg[:, None, None]) * vis[None]
    xi = np.exp((idx[None, :] + 1.0) * lg[:, None])[..., None]
    zeta = np.exp((C - 1.0 - idx)[None, :] * lg[:, None])[..., None]
    half = RET_DK // 2
    inv_freq = ROPE_BASE ** (-np.arange(half, dtype=np.float32) / np.float32(half))
    ang = np.arange(T, dtype=np.float32)[:, None] * inv_freq[None, :].astype(np.float32)
    return (jnp.asarray(mask, F32), jnp.asarray(xi, F32), jnp.asarray(zeta, F32),
            jnp.asarray(np.cos(ang), F32), jnp.asarray(np.sin(ang), F32))


def _rot(t, cos, sin):
    half = RET_DK // 2
    t1, t2 = t[:, :half], t[:, half:]
    return jnp.concatenate([t1 * cos - t2 * sin, t1 * sin + t2 * cos], axis=1)


def _unrot(d, cos, sin):
    half = RET_DK // 2
    d1, d2 = d[:, :half], d[:, half:]
    return jnp.concatenate([d1 * cos + d2 * sin, d2 * cos - d1 * sin], axis=1)


def _ret_specs(C, H, rev, NS):
    def ci(i):
        return NS - 1 - i if rev else i

    nq = H
    q_spec = pl.BlockSpec((C, RET_DK), lambda h, i: (ci(i), h))
    k_spec = pl.BlockSpec((C, RET_DK), lambda h, i: (ci(i), nq + h))
    v_spec = pl.BlockSpec((C, RET_DV), lambda h, i: (ci(i), H + h))
    g_spec = pl.BlockSpec((C, RET_DV), lambda h, i: (ci(i), 2 * H + h))
    cs_spec = pl.BlockSpec((C, RET_DK // 2), lambda h, i: (ci(i), 0))
    m_spec = pl.BlockSpec((None, C, C), lambda h, i: (h, 0, 0))
    vec_spec = pl.BlockSpec((None, C, 1), lambda h, i: (h, 0, 0))
    gn_spec = pl.BlockSpec((1, RET_DV), lambda h, i: (0, h))
    st_spec = pl.BlockSpec((None, None, RET_DK, RET_DV), lambda h, i: (h, ci(i), 0, 0))
    return q_spec, k_spec, v_spec, g_spec, cs_spec, m_spec, vec_spec, gn_spec, st_spec


def _ret_fwd_vals(q, k, v, cos, sin, mask, xi, s_in):
    qr = _rot(q, cos, sin)
    kr = _rot(k, cos, sin) * (RET_DK ** -0.5)
    a = _dot(qr, kr, NT) * mask
    o = _dot(a, v) + _dot(qr, s_in) * xi
    mu = jnp.mean(o, axis=1, keepdims=True)
    oc = o - mu
    rstd = lax.rsqrt(jnp.mean(oc * oc, axis=1, keepdims=True) + GN_EPS)
    return qr, kr, a, oc * rstd, rstd


def _ret_fwd(proj, gn_w, consts, *, C, name):
    T = proj.shape[0]
    H = gn_w.shape[1] // RET_DV
    NS = T // C
    mask, xi, zeta, cos, sin = consts
    q_spec, k_spec, v_spec, g_spec, cs_spec, m_spec, vec_spec, gn_spec, st_spec = _ret_specs(C, H, False, NS)
    y_spec = pl.BlockSpec((C, RET_DV), lambda h, i: (i, h))

    def body(q_ref, k_ref, v_ref, g_ref, cos_ref, sin_ref, m_ref, xi_ref, ze_ref, gn_ref, y_ref, st_ref, S):
        @pl.when(pl.program_id(1) == 0)
        def _():
            S[...] = jnp.zeros_like(S)

        s_in = S[...]
        st_ref[...] = s_in
        v = v_ref[...]
        xi_v = xi_ref[...]
        qr, kr, a, on, rstd = _ret_fwd_vals(q_ref[...], k_ref[...], v, cos_ref[...], sin_ref[...], m_ref[...], xi_v, s_in)
        g = g_ref[...]
        y_ref[...] = (g * _sigmoid(g) * on * gn_ref[...]).astype(BF16)
        S[...] = s_in * xi_v[C - 1:C, :] + _dot(kr * ze_ref[...], v, TN)

    return pl.pallas_call(
        body, name=name, grid=(H, NS),
        in_specs=[q_spec, k_spec, v_spec, g_spec, cs_spec, cs_spec, m_spec, vec_spec, vec_spec, gn_spec],
        out_specs=[y_spec, st_spec],
        out_shape=[jax.ShapeDtypeStruct((T, H * RET_DV), BF16), jax.ShapeDtypeStruct((H, NS, RET_DK, RET_DV), F32)],
        scratch_shapes=[pltpu.VMEM((RET_DK, RET_DV), F32)],
        compiler_params=_cparams("parallel", "arbitrary"),
    )(proj, proj, proj, proj, cos, sin, mask, xi, zeta, gn_w)


def _stage_out(out_hbm, stage, sems, step, n_steps, row0, pieces, values):
    C = stage.shape[1]
    slot = step % 2

    def copies(sl):
        return [pltpu.make_async_copy(stage.at[sl, :, pl.ds(c0, w)],
                                      out_hbm.at[pl.ds(pl.multiple_of(row0, 16), C), pl.ds(pl.multiple_of(dc, 128), w)],
                                      sems.at[sl, n]) for n, (c0, w, dc) in enumerate(pieces)]

    @pl.when(step >= 2)
    def _():
        for cp in copies(slot):
            cp.wait()

    for (c0, w, _), v in zip(pieces, values):
        stage[slot, :, c0:c0 + w] = v
    for cp in copies(slot):
        cp.start()

    @pl.when(step == n_steps - 1)
    def _():
        for cp in copies(slot):
            cp.wait()
        if n_steps >= 2:
            for cp in copies(1 - slot):
                cp.wait()


def _ret_bwd(proj, gn_w, consts, states, dy, *, C, name):
    T = proj.shape[0]
    H = gn_w.shape[1] // RET_DV
    NS = T // C
    mask, xi, zeta, cos, sin = consts
    q_spec, k_spec, v_spec, g_spec, cs_spec, m_spec, vec_spec, gn_spec, st_spec = _ret_specs(C, H, True, NS)
    dy_spec = pl.BlockSpec((C, RET_DV), lambda h, i: (NS - 1 - i, h))
    scale = RET_DK ** -0.5
    wq, wv = H * RET_DK, H * RET_DV

    def body(q_ref, k_ref, v_ref, g_ref, cos_ref, sin_ref, m_ref, xi_ref, ze_ref, gn_ref, st_ref, dy_ref,
             dproj_ref, dgn_ref, dS, stage, sems):
        @pl.when(pl.program_id(1) == 0)
        def _():
            dS[...] = jnp.zeros_like(dS)
            dgn_ref[...] = jnp.zeros_like(dgn_ref)

        s_in = st_ref[...]
        v = v_ref[...]
        cos, sin, mask, xi_v, ze = cos_ref[...], sin_ref[...], m_ref[...], xi_ref[...], ze_ref[...]
        qr, kr, a, on, rstd = _ret_fwd_vals(q_ref[...], k_ref[...], v, cos, sin, mask, xi_v, s_in)
        g = g_ref[...]
        sg = _sigmoid(g)
        silu = g * sg
        gnw = gn_ref[...]
        dy = dy_ref[...].astype(F32)
        dg = (dy * on * gnw * (sg * (1.0 + g * (1.0 - sg)))).astype(BF16)
        t = dy * silu
        dgn_ref[...] += _colsum(t * on)
        don = t * gnw
        do = rstd * (don - jnp.mean(don, axis=1, keepdims=True) - on * jnp.mean(don * on, axis=1, keepdims=True))
        dox = do * xi_v
        ds_out = dS[...]
        da = _dot(do, v, NT) * mask
        kz = kr * ze
        dv = (_dot(a, do, TN) + _dot(kz, ds_out)).astype(BF16)
        dqr = _dot(da, kr) + _dot(dox, s_in, NT)
        dkr = _dot(da, qr, TN) + _dot(v, ds_out, NT) * ze
        dS[...] = ds_out * xi_v[C - 1:C, :] + _dot(qr, dox, TN)
        dq = _unrot(dqr, cos, sin).astype(BF16)
        dk = _unrot(dkr * scale, cos, sin).astype(BF16)
        h, i = pl.program_id(0), pl.program_id(1)
        pieces = [(0, RET_DK, h * RET_DK), (RET_DK, RET_DK, wq + h * RET_DK),
                  (2 * RET_DK, RET_DV, 2 * wq + h * RET_DV), (2 * RET_DK + RET_DV, RET_DV, 2 * wq + wv + h * RET_DV)]
        _stage_out(dproj_ref, stage, sems, h * NS + i, H * NS, (NS - 1 - i) * C, pieces, [dq, dk, dv, dg])

    return pl.pallas_call(
        body, name=name, grid=(H, NS),
        in_specs=[q_spec, k_spec, v_spec, g_spec, cs_spec, cs_spec, m_spec, vec_spec, vec_spec, gn_spec, st_spec, dy_spec],
        out_specs=[_ANY, gn_spec],
        out_shape=[jax.ShapeDtypeStruct((T, 2 * wq + 2 * wv), BF16), jax.ShapeDtypeStruct((1, H * RET_DV), F32)],
        scratch_shapes=[pltpu.VMEM((RET_DK, RET_DV), F32), pltpu.VMEM((2, C, 2 * RET_DK + 2 * RET_DV), BF16),
                        pltpu.SemaphoreType.DMA((2, 4))],
        compiler_params=_cparams("arbitrary", "arbitrary"),
    )(proj, proj, proj, proj, cos, sin, mask, xi, zeta, gn_w, states, dy)


def _shift_down(x, prev8, k):
    if k == 0:
        return x
    y = pltpu.roll(x, k, 0)
    row = lax.broadcasted_iota(jnp.int32, prev8.shape, 0)
    top = jnp.where(row < k, pltpu.roll(prev8, k, 0), y[:8])
    return jnp.concatenate([top, y[8:]], axis=0)


def _shift_up(x, next8, k):
    if k == 0:
        return x
    n = x.shape[0]
    y = pltpu.roll(x, n - k, 0)
    row = lax.broadcasted_iota(jnp.int32, next8.shape, 0)
    bot = jnp.where(row >= 8 - k, pltpu.roll(next8, 8 - k, 0), y[n - 8:])
    return jnp.concatenate([y[:n - 8], bot], axis=0)


def _conv_silu(raw, halo, w, b):
    cv = b
    for tap in range(SSD_CONV_W):
        cv = cv + _shift_down(raw, halo, SSD_CONV_W - 1 - tap) * w[tap:tap + 1, :]
    sg = _sigmoid(cv)
    return cv * sg, cv, sg


def _conv_silu_bwd(d_post, cv, sg, raw, halo, w, carry8):
    dcv = d_post * (sg * (1.0 + cv * (1.0 - sg)))
    d_raw = jnp.zeros_like(raw)
    dws = []
    for tap in range(SSD_CONV_W):
        k = SSD_CONV_W - 1 - tap
        d_raw = d_raw + _shift_up(dcv, carry8, k) * w[tap:tap + 1, :]
        dws.append(_colsum(dcv * _shift_down(raw, halo, k)))
    return d_raw, jnp.concatenate(dws, axis=0), _colsum(dcv), dcv[:8]


def _softplus(x):
    return jnp.maximum(x, 0.0) + jnp.log1p(jnp.exp(-jnp.abs(x)))


def _ssd_common(C, R, dt, dtT, bias, biasT, alog, alogT, E):
    p = dt + bias
    dtv = _softplus(p)
    a = -jnp.exp(alog)
    da = dtv * a
    daT = _softplus(dtT + biasT) * (-jnp.exp(alogT))
    row = lax.broadcasted_iota(jnp.int32, (C, C), 0)
    col = lax.broadcasted_iota(jnp.int32, (C, C), 1)
    tril = row >= col
    trilf = jnp.where(tril, 1.0, 0.0).astype(F32)
    triuf = jnp.where(col >= row, 1.0, 0.0).astype(F32)
    acum = _dot_sel(trilf, da, split="b")
    acumT = _dot_sel(daT, trilf, NT, split="a")
    al = acum[C - 1:C, :]
    ea = jnp.exp(acum)
    dte = jnp.exp(al - acum)
    eal = jnp.exp(al)
    return dict(p=p, dtv=dtv, a=a, da=da, tril=tril, triuf=triuf, acum=acum, acumT=acumT, al=al, ea=ea, dte=dte, eal=eal,
                dtv_e=_dot_sel(dtv, E, split="a", terms=2), ea_e=_dot_sel(ea, E, split="a", terms=2),
                dte_e=_dot_sel(dte, E, split="a", terms=2), eal_e=_dot_sel(eal, E, split="a"))


def _head_decay(q, r, C, R):
    seg = jnp.broadcast_to(q["acum"][:, r:r + 1], (C, C)) - q["acumT"][r:r + 1, :]
    return jnp.exp(jnp.where(q["tril"], seg, -1e30))


def _ssd_group_specs(C, R, NS, rev):
    RP = R * SSD_P
    G = SSD_G
    hb = C // 8

    def ci(i):
        return NS - 1 - i if rev else i

    def halo_row(i):
        return jnp.maximum(ci(i) * hb - 1, 0)

    off_b = G * RP // SSD_N
    z_spec = pl.BlockSpec((C, RP), lambda g, i: (ci(i), g))
    x_spec = pl.BlockSpec((C, RP), lambda g, i: (ci(i), G + g))
    b_spec = pl.BlockSpec((C, SSD_N), lambda g, i: (ci(i), 2 * off_b + g))
    c_spec = pl.BlockSpec((C, SSD_N), lambda g, i: (ci(i), 2 * off_b + G + g))
    xh_spec = pl.BlockSpec((8, RP), lambda g, i: (halo_row(i), G + g))
    bh_spec = pl.BlockSpec((8, SSD_N), lambda g, i: (halo_row(i), 2 * off_b + g))
    ch_spec = pl.BlockSpec((8, SSD_N), lambda g, i: (halo_row(i), 2 * off_b + G + g))
    dt_spec = pl.BlockSpec((None, C, R), lambda g, i: (g, ci(i), 0))
    dtT_spec = pl.BlockSpec((None, R, C), lambda g, i: (g, 0, ci(i)))
    pr_spec = pl.BlockSpec((None, 1, R), lambda g, i: (g, 0, 0))
    prT_spec = pl.BlockSpec((None, R, 1), lambda g, i: (g, 0, 0))
    cwx_spec = pl.BlockSpec((SSD_CONV_W, RP), lambda g, i: (0, g))
    cwn_spec = pl.BlockSpec((SSD_CONV_W, SSD_N), lambda g, i: (0, g))
    cbx_spec = pl.BlockSpec((1, RP), lambda g, i: (0, g))
    cbn_spec = pl.BlockSpec((1, SSD_N), lambda g, i: (0, g))
    e_spec = pl.BlockSpec((R, RP), lambda g, i: (0, 0))
    st_spec = pl.BlockSpec((None, None, SSD_N, RP), lambda g, i: (g, ci(i), 0, 0))
    return dict(z=z_spec, x=x_spec, b=b_spec, c=c_spec, xh=xh_spec, bh=bh_spec, ch=ch_spec, dt=dt_spec, dtT=dtT_spec,
                pr=pr_spec, prT=prT_spec, cwx=cwx_spec, cwn=cwn_spec, cbx=cbx_spec, cbn=cbn_spec, e=e_spec, st=st_spec)


def _ssd_forward_vals(C, R, refs, first, s_in):
    E = refs["E"]
    halo_on = jnp.where(first, 0.0, 1.0)
    xh, bh, ch = refs["xh"] * halo_on, refs["bh"] * halo_on, refs["ch"] * halo_on
    xs, cvx, sgx = _conv_silu(refs["x"], xh, refs["cwx"], refs["cbx"])
    bm, cvb, sgb = _conv_silu(refs["b"], bh, refs["cwb"], refs["cbb"])
    cm, cvc, sgc = _conv_silu(refs["c"], ch, refs["cwc"], refs["cbc"])
    q = _ssd_common(C, R, refs["dt"], refs["dtT"], refs["bias"], refs["biasT"], refs["alog"], refs["alogT"], E)
    xdt = xs * q["dtv_e"]
    cb = _dot(cm, bm, NT)
    yoff_raw = _dot(cm, s_in)
    xdt_b = xdt.astype(BF16)
    low = lax.broadcasted_iota(jnp.int32, (1, 2 * SSD_P), 1) < SSD_P
    pairs = []
    for j in range(R // 2):
        xp = xdt_b[:, 2 * SSD_P * j:2 * SSD_P * (j + 1)]
        y0 = _dot(cb * _head_decay(q, 2 * j, C, R), xp)
        y1 = _dot(cb * _head_decay(q, 2 * j + 1, C, R), xp)
        pairs.append(jnp.where(low, y0, y1))
    ydiag = jnp.concatenate(pairs, axis=1)
    d_e =_dot_sel(refs["dskip"], E, split="a")
    y = ydiag + yoff_raw * q["ea_e"] + d_e * xs
    xd = xdt * q["dte_e"]
    s_out = s_in * q["eal_e"] + _dot(bm, xd, TN)
    z = refs["z"]
    sgz = _sigmoid(z)
    yz = y * (z * sgz)
    rn = lax.rsqrt(jnp.mean(yz * yz, axis=1, keepdims=True) + RMS_EPS)
    return dict(q=q, xh=xh, bh=bh, ch=ch, xs=xs, cvx=cvx, sgx=sgx, bm=bm, cvb=cvb, sgb=sgb, cm=cm, cvc=cvc, sgc=sgc,
                xdt=xdt, cb=cb, yoff_raw=yoff_raw, d_e=d_e, y=y, xd=xd, s_out=s_out, z=z, sgz=sgz, yz=yz, rn=rn)


_SSD_IN_NAMES = ("z", "x", "b", "c", "xh", "bh", "ch", "dt", "dtT", "bias", "biasT", "alog", "alogT", "dskip",
                 "cwx", "cwb", "cwc", "cbx", "cbb", "cbc", "nw", "E")


def _ssd_inputs(pm, dt_g, dtT_g, prm, sp):
    bias, biasT, alog, alogT, dskip, cwx, cwb, cwc, cbx, cbb, cbc, nw, E = prm
    args = [pm, pm, pm, pm, pm, pm, pm, dt_g, dtT_g, bias, biasT, alog, alogT, dskip, cwx, cwb, cwc, cbx, cbb, cbc, nw, E]
    specs = [sp["z"], sp["x"], sp["b"], sp["c"], sp["xh"], sp["bh"], sp["ch"], sp["dt"], sp["dtT"], sp["pr"], sp["prT"],
             sp["pr"], sp["prT"], sp["pr"], sp["cwx"], sp["cwn"], sp["cwn"], sp["cbx"], sp["cbn"], sp["cbn"], sp["cbx"], sp["e"]]
    return args, specs


def _ssd_fwd(pm, dt_g, dtT_g, prm, *, C, R, name):
    T = pm.shape[0]
    NS = T // C
    RP = R * SSD_P
    G = SSD_G
    sp = _ssd_group_specs(C, R, NS, False)
    args, specs = _ssd_inputs(pm, dt_g, dtT_g, prm, sp)
    nin = len(args)

    def body(*refs):
        ins = {n: r[...] for n, r in zip(_SSD_IN_NAMES, refs[:nin])}
        y_ref, st_ref, S = refs[nin:]
        first = pl.program_id(1) == 0

        @pl.when(first)
        def _():
            S[...] = jnp.zeros_like(S)

        s_in = S[...]
        st_ref[...] = s_in
        f = _ssd_forward_vals(C, R, ins, first, s_in)
        y_ref[...] = (f["yz"] * f["rn"] * ins["nw"]).astype(BF16)
        S[...] = f["s_out"]

    return pl.pallas_call(
        body, name=name, grid=(G, NS), in_specs=specs,
        out_specs=[pl.BlockSpec((C, RP), lambda g, i: (i, g)), sp["st"]],
        out_shape=[jax.ShapeDtypeStruct((T, G * RP), BF16), jax.ShapeDtypeStruct((G, NS, SSD_N, RP), F32)],
        scratch_shapes=[pltpu.VMEM((SSD_N, RP), F32)],
        compiler_params=_cparams("parallel", "arbitrary"),
    )(*args)


def _ssd_bwd(pm, dt_g, dtT_g, prm, states, dout, *, C, R, name):
    T = pm.shape[0]
    NS = T // C
    RP = R * SSD_P
    G = SSD_G
    sp = _ssd_group_specs(C, R, NS, True)
    args, specs = _ssd_inputs(pm, dt_g, dtT_g, prm, sp)
    nin = len(args)
    rows_spec = pl.BlockSpec((C, RP), lambda g, i: (NS - 1 - i, g))
    args = args + [states, dout]
    specs = specs + [sp["st"], rows_spec]

    def body(*refs):
        ins = {n: r[...] for n, r in zip(_SSD_IN_NAMES, refs[:nin])}
        st_ref, dout_ref = refs[nin], refs[nin + 1]
        (dpm_ref, ddt_ref, dbias_ref, dalog_ref, dd_ref, dcwx_ref, dcwb_ref, dcwc_ref,
         dcbx_ref, dcbb_ref, dcbc_ref, dnw_ref) = refs[nin + 2:nin + 14]
        dS, cx8, cb8, cc8, stage, sems = refs[nin + 14:]
        acc_refs = (dbias_ref, dalog_ref, dd_ref, dcwx_ref, dcwb_ref, dcwc_ref, dcbx_ref, dcbb_ref, dcbc_ref, dnw_ref)
        step = pl.program_id(1)

        @pl.when(step == 0)
        def _():
            for r_ in acc_refs + (dS, cx8, cb8, cc8):
                r_[...] = jnp.zeros_like(r_)

        first = step == NS - 1
        E = ins["E"]
        s_in = st_ref[...]
        f = _ssd_forward_vals(C, R, ins, first, s_in)
        q = f["q"]
        xs, bm, cm, xdt, cb, y, z, sgz, yz, rn = (f[n] for n in ("xs", "bm", "cm", "xdt", "cb", "y", "z", "sgz", "yz", "rn"))
        nw = ins["nw"]
        dout = dout_ref[...].astype(F32)
        yh = yz * rn
        dnw_ref[...] += _colsum(dout * yh)
        g1 = dout * nw
        dyz = rn * (g1 - yh * jnp.mean(g1 * yh, axis=1, keepdims=True))
        dz = (dyz * y * (sgz * (1.0 + z * (1.0 - sgz)))).astype(BF16)
        dy = dyz * (z * sgz)
        dd_ref[...] += _dot_sel(_colsum(dy * xs), E, NT, split="a")
        dxs = dy * f["d_e"]
        dyo = dy * q["ea_e"]
        dcm = _dot(dyo, s_in, NT)
        ds_acc = _dot(cm, dyo, TN)
        dacum = _dot_sel(dy * f["yoff_raw"], E, NT, split="a", terms=1) * q["ea"]
        dacumT = jnp.zeros((R, C), F32)
        dcb = jnp.zeros((C, C), F32)
        rowR = lax.broadcasted_iota(jnp.int32, (1, R), 1)
        rowRT = lax.broadcasted_iota(jnp.int32, (R, 1), 0)
        dy_b, xdt_b = dy.astype(BF16), xdt.astype(BF16)
        low = lax.broadcasted_iota(jnp.int32, (1, 2 * SSD_P), 1) < SSD_P
        dxdt_pairs = []
        for j in range(R // 2):
            lanes = slice(2 * SSD_P * j, 2 * SSD_P * (j + 1))
            dyp, xp = dy_b[:, lanes], xdt_b[:, lanes]
            halves = []
            for r, mine in ((2 * j, low), (2 * j + 1, jnp.logical_not(low))):
                lr = _head_decay(q, r, C, R)
                w_r = cb * lr
                dw = _dot(jnp.where(mine, dyp, jnp.zeros_like(dyp)), xp, NT)
                halves.append(_dot(w_r, dyp, TN))
                dcb = dcb + dw * lr
                dseg = dw * w_r
                dacum = dacum + jnp.sum(dseg, axis=1, keepdims=True) * jnp.where(rowR == r, 1.0, 0.0)
                dacumT = dacumT - _colsum(dseg) * jnp.where(rowRT == r, 1.0, 0.0)
            dxdt_pairs.append(jnp.where(low, halves[0], halves[1]))
        dxdt = jnp.concatenate(dxdt_pairs, axis=1)
        dsn = dS[...]
        ds_acc = ds_acc + dsn * q["eal_e"]
        d_eal = _dot_sel(_colsum(dsn * s_in), E, NT, split="a")
        dbm = _dot(f["xd"], dsn, NT)
        dxd = _dot(bm, dsn)
        dxdt = dxdt + dxd * q["dte_e"]
        d_dte = _dot_sel(dxd * xdt, E, NT, split="a", terms=1) * q["dte"]
        d_al = _colsum(d_dte) + d_eal * q["eal"]
        dacum = dacum - d_dte
        rowC = lax.broadcasted_iota(jnp.int32, (C, 1), 0)
        dacum = dacum + jnp.where(rowC == C - 1, 1.0, 0.0) * d_al
        dS[...] = ds_acc
        dcm = dcm + _dot(dcb, bm)
        dbm = dbm + _dot(dcb, cm, TN)
        eye = jnp.where(lax.broadcasted_iota(jnp.int32, (C, C), 0) == lax.broadcasted_iota(jnp.int32, (C, C), 1), 1.0, 0.0)
        dacum = dacum + _dot_sel(eye, dacumT, NT, split="b")
        dda = _dot_sel(q["triuf"], dacum, split="b")
        ddtv = dda * q["a"] + _dot_sel(dxdt * xs, E, NT, split="a", terms=1)
        dalog_ref[...] += _colsum(dda * q["dtv"]) * q["a"]
        dxs = dxs + dxdt * q["dtv_e"]
        dp = ddtv * _sigmoid(q["p"])
        ddt_ref[...] = dp
        dbias_ref[...] += _colsum(dp)
        d_raw, d_w, d_b, c8 = _conv_silu_bwd(dxs, f["cvx"], f["sgx"], ins["x"], f["xh"], ins["cwx"], cx8[...])
        dx = d_raw.astype(BF16)
        dcwx_ref[...] += d_w
        dcbx_ref[...] += d_b
        cx8[...] = c8
        d_raw, d_w, d_b, c8 = _conv_silu_bwd(dbm, f["cvb"], f["sgb"], ins["b"], f["bh"], ins["cwb"], cb8[...])
        db = d_raw.astype(BF16)
        dcwb_ref[...] += d_w
        dcbb_ref[...] += d_b
        cb8[...] = c8
        d_raw, d_w, d_b, c8 = _conv_silu_bwd(dcm, f["cvc"], f["sgc"], ins["c"], f["ch"], ins["cwc"], cc8[...])
        dc = d_raw.astype(BF16)
        dcwc_ref[...] += d_w
        dcbc_ref[...] += d_b
        cc8[...] = c8
        g_ = pl.program_id(0)
        pieces = [(0, RP, g_ * RP), (RP, RP, G * RP + g_ * RP), (2 * RP, SSD_N, 2 * G * RP + g_ * SSD_N),
                  (2 * RP + SSD_N, SSD_N, 2 * G * RP + G * SSD_N + g_ * SSD_N)]
        _stage_out(dpm_ref, stage, sems, g_ * NS + step, G * NS, (NS - 1 - step) * C, pieces, [dz, dx, db, dc])

    out_specs = [_ANY, pl.BlockSpec((None, C, R), lambda g, i: (g, NS - 1 - i, 0)),
                 sp["pr"], sp["pr"], sp["pr"], sp["cwx"], sp["cwn"], sp["cwn"], sp["cbx"], sp["cbn"], sp["cbn"], sp["cbx"]]
    out_shape = [jax.ShapeDtypeStruct((T, 2 * G * RP + 2 * G * SSD_N), BF16),
                 jax.ShapeDtypeStruct((G, T, R), F32),
                 jax.ShapeDtypeStruct((G, 1, R), F32), jax.ShapeDtypeStruct((G, 1, R), F32), jax.ShapeDtypeStruct((G, 1, R), F32),
                 jax.ShapeDtypeStruct((SSD_CONV_W, G * RP), F32), jax.ShapeDtypeStruct((SSD_CONV_W, G * SSD_N), F32),
                 jax.ShapeDtypeStruct((SSD_CONV_W, G * SSD_N), F32),
                 jax.ShapeDtypeStruct((1, G * RP), F32), jax.ShapeDtypeStruct((1, G * SSD_N), F32),
                 jax.ShapeDtypeStruct((1, G * SSD_N), F32), jax.ShapeDtypeStruct((1, G * RP), F32)]
    return pl.pallas_call(
        body, name=name, grid=(G, NS), in_specs=specs, out_specs=out_specs, out_shape=out_shape,
        scratch_shapes=[pltpu.VMEM((SSD_N, RP), F32), pltpu.VMEM((8, RP), F32), pltpu.VMEM((8, SSD_N), F32),
                        pltpu.VMEM((8, SSD_N), F32), pltpu.VMEM((2, C, 2 * RP + 2 * SSD_N), BF16),
                        pltpu.SemaphoreType.DMA((2, 4))],
        compiler_params=_cparams("arbitrary", "arbitrary"),
    )(*args)


_ANY = pl.BlockSpec(memory_space=pl.ANY)


def _chip_peer(k):
    x, y, c = lax.axis_index("x"), lax.axis_index("y"), lax.axis_index("c")
    return (x ^ (k >> 1), y ^ (k & 1), c)


def _my_chip():
    return 2 * lax.axis_index("x") + lax.axis_index("y")


NORM_ROWS = 256


def _all_gather_chips(shards, halved, *, name, norm=None):
    n = len(shards)

    nb = 0 if norm is None else norm[0].shape[0] // NORM_ROWS

    def body(*refs):
        ins = refs[:n]
        outs = refs[n + 2 * bool(nb):2 * n + 2 * bool(nb)]
        send, recv, fsend, frecv, loc = refs[-5:]
        s = _my_chip()
        c = lax.axis_index("c")
        sibling = (lax.axis_index("x"), lax.axis_index("y"), 1 - c)
        step = pl.program_id(0) if nb else 0

        def local(a):
            return pltpu.make_async_copy(ins[a], outs[a].at[s], loc.at[a])

        def rows(a, core):
            if not halved[a]:
                return slice(None)
            half = shards[a].shape[0] // 2
            return pl.ds(pl.multiple_of(core * half, 16), half)

        def over_ici(a, k, slot, core):
            return pltpu.make_async_remote_copy(
                src_ref=ins[a].at[rows(a, core)], dst_ref=outs[a].at[slot, rows(a, core)],
                send_sem=send.at[3 * a + k - 1], recv_sem=recv.at[3 * a + k - 1],
                device_id=_chip_peer(k), device_id_type=MESH_ID)

        def over_d2d(a, k, core):
            z = outs[a].at[s ^ k, rows(a, core)]
            return pltpu.make_async_remote_copy(
                src_ref=z, dst_ref=z, send_sem=fsend.at[3 * a + k - 1], recv_sem=frecv.at[3 * a + k - 1],
                device_id=sibling, device_id_type=MESH_ID)

        def send_all():
            for a in range(n):
                local(a).start()
            for a in range(n):
                for k in (1, 2, 3):
                    over_ici(a, k, s, c).start()

        def finish_all():
            passed = []
            for a in range(n):
                for k in (1, 2, 3):
                    over_ici(a, k, s ^ k, c).wait_recv()
                    if halved[a]:
                        cp = over_d2d(a, k, c)
                        cp.start()
                        passed.append(cp)
            for a in range(n):
                if halved[a]:
                    for k in (1, 2, 3):
                        over_d2d(a, k, 1 - c).wait_recv()
            for a in range(n):
                for k in (1, 2, 3):
                    over_ici(a, k, s, c).wait_send()
            for cp in passed:
                cp.wait_send()
            for a in range(n):
                local(a).wait()

        if not nb:
            send_all()
            finish_all()
            return
        pl.when(step == 0)(send_all)
        x_ref, w_ref, u_ref = refs[n], refs[n + 1], refs[2 * n + 2]
        xv = x_ref[...]
        u_ref[...] = (xv * _rstd(xv) * w_ref[...]).astype(BF16)
        pl.when(step == nb - 1)(finish_all)

    for a, h in zip(shards, halved):
        assert not h or a.shape[0] % 32 == 0, a.shape
    in_specs, out_specs = [_ANY] * n, [_ANY] * n
    out_shape = [jax.ShapeDtypeStruct((4,) + a.shape, a.dtype) for a in shards]
    args = list(shards)
    if nb:
        x, w = norm
        row_spec = pl.BlockSpec((NORM_ROWS, x.shape[1]), lambda i: (i, 0))
        in_specs += [row_spec, pl.BlockSpec(w.shape, lambda i: (0, 0))]
        out_specs.append(row_spec)
        out_shape.append(jax.ShapeDtypeStruct(x.shape, BF16))
        args += [x, w]
    return pl.pallas_call(
        body, name=name, grid=(nb,) if nb else (), in_specs=in_specs, out_specs=out_specs, out_shape=out_shape,
        scratch_shapes=[pltpu.SemaphoreType.DMA((3 * n,))] * 4 + [pltpu.SemaphoreType.DMA((n,))],
        compiler_params=pltpu.CompilerParams(has_side_effects=True, vmem_limit_bytes=VMEM_LIMIT_BYTES,
                                             dimension_semantics=("arbitrary",) if nb else None),
    )(*args)


_HBM = pl.BlockSpec(memory_space=pltpu.HBM)
_SEM = pl.BlockSpec(memory_space=pltpu.SEMAPHORE)
_EFFECT = pltpu.SideEffectType.DATAFLOW_SIDE_EFFECTING


def _split_copies(src, land, send, recv, loc, a, scatter):
    s = _my_chip()
    mine = pltpu.make_async_copy(src.at[s] if scatter else src, land.at[s], loc.at[a])
    pairs = []
    for k in (1, 2, 3):
        sems = dict(send_sem=send.at[3 * a + k - 1], recv_sem=recv.at[3 * a + k - 1],
                    device_id=_chip_peer(k), device_id_type=MESH_ID)
        out = pltpu.make_async_remote_copy(src_ref=src.at[s ^ k] if scatter else src, dst_ref=land.at[s], **sems)
        arriving = pltpu.make_async_remote_copy(src_ref=src.at[s ^ k] if scatter else src, dst_ref=land.at[s ^ k], **sems)
        pairs.append((out, arriving))
    return mine, pairs


def _split_start(arrs, *, scatter, after, name):
    n = len(arrs)
    zones = [lax.empty(a.shape if scatter else (4,) + a.shape, a.dtype) for a in arrs]

    def body(*refs):
        srcs, lands = refs[:n], refs[n:2 * n]
        send, recv, loc = refs[2 * n + 1:2 * n + 4]
        token = refs[-1]
        for a in range(n):
            mine, pairs = _split_copies(srcs[a], lands[a], send, recv, loc, a, scatter)
            mine.start()
            for out, _ in pairs:
                out.start()
        token[...] = jnp.zeros_like(token)

    res = pl.pallas_call(
        body, name=name,
        out_shape=(pltpu.SemaphoreType.DMA((3 * n,)), pltpu.SemaphoreType.DMA((3 * n,)), pltpu.SemaphoreType.DMA((n,)),
                   *[pltpu.HBM(z.shape, z.dtype) for z in zones], jax.ShapeDtypeStruct((8, 128), F32)),
        in_specs=[_ANY] * n + [_HBM] * n + [_ANY],
        out_specs=(_SEM, _SEM, _SEM, *([_HBM] * n), pl.BlockSpec(memory_space=pltpu.VMEM)),
        input_output_aliases={n + i: 3 + i for i in range(n)},
        compiler_params=pltpu.CompilerParams(has_side_effects=_EFFECT),
    )(*arrs, *[pltpu.with_memory_space_constraint(z, pltpu.HBM) for z in zones], after)
    return res[:3], list(res[3:3 + n]), res[-1]


def _split_wait(sems, src, land, a, *, scatter, after, name):
    def body(src_ref, land_ref, send, recv, loc, after_ref, land_out):
        mine, pairs = _split_copies(src_ref, land_ref, send, recv, loc, a, scatter)
        mine.wait()
        for out, arriving in pairs:
            out.wait_send()
            arriving.wait_recv()

    return pl.pallas_call(
        body, name=name, out_shape=pltpu.HBM(land.shape, land.dtype),
        in_specs=[_ANY, _HBM, _SEM, _SEM, _SEM, _ANY], out_specs=_HBM, input_output_aliases={1: 0},
        compiler_params=pltpu.CompilerParams(has_side_effects=_EFFECT),
    )(src, land, *sems, after)


def _sibling_copies(srcs, lands, send, recv):
    sib = (lax.axis_index("x"), lax.axis_index("y"), 1 - lax.axis_index("c"))
    return [pltpu.make_async_remote_copy(src_ref=srcs[a], dst_ref=lands[a], send_sem=send.at[a], recv_sem=recv.at[a],
                                         device_id=sib, device_id_type=MESH_ID) for a in range(len(srcs))]


def _swap_start(arrs, *, after, name):
    n = len(arrs)
    zones = [lax.empty(a.shape, a.dtype) for a in arrs]

    def body(*refs):
        for cp in _sibling_copies(refs[:n], refs[n:2 * n], refs[2 * n + 1], refs[2 * n + 2]):
            cp.start()

    res = pl.pallas_call(
        body, name=name,
        out_shape=(pltpu.SemaphoreType.DMA((n,)), pltpu.SemaphoreType.DMA((n,)), *[pltpu.HBM(z.shape, z.dtype) for z in zones]),
        in_specs=[_ANY] * n + [_HBM] * n + [_ANY], out_specs=(_SEM, _SEM, *([_HBM] * n)),
        input_output_aliases={n + i: 2 + i for i in range(n)},
        compiler_params=pltpu.CompilerParams(has_side_effects=_EFFECT),
    )(*arrs, *[pltpu.with_memory_space_constraint(z, pltpu.HBM) for z in zones], after)
    return res[:2], list(res[2:])


def _swap_wait(sems, srcs, lands, *, after, name):
    n = len(srcs)

    def body(*refs):
        for cp in _sibling_copies(refs[:n], refs[n:2 * n], refs[2 * n], refs[2 * n + 1]):
            cp.wait_send()
            cp.wait_recv()

    res = pl.pallas_call(
        body, name=name, out_shape=tuple(pltpu.HBM(a.shape, a.dtype) for a in lands),
        in_specs=[_ANY] * n + [_HBM] * n + [_SEM, _SEM, _ANY], out_specs=tuple([_HBM] * n),
        input_output_aliases={n + i: i for i in range(n)},
        compiler_params=pltpu.CompilerParams(has_side_effects=_EFFECT),
    )(*srcs, *lands, *sems, after)
    return list(res)


def _all_gather_devices(v, *, name):
    r = v.shape[0]

    def body(v_ref, out_ref, send, recv):
        x, y, c = lax.axis_index("x"), lax.axis_index("y"), lax.axis_index("c")
        me = 4 * x + 2 * y + c
        out_ref[me] = v_ref[...]
        cps = []
        for k in range(1, 8):
            peer = (x ^ (k >> 2), y ^ ((k >> 1) & 1), c ^ (k & 1))
            cp = pltpu.make_async_remote_copy(src_ref=v_ref, dst_ref=out_ref.at[me], send_sem=send.at[k - 1],
                                              recv_sem=recv.at[k - 1], device_id=peer, device_id_type=MESH_ID)
            cp.start()
            cps.append(cp)
        for k, cp in enumerate(cps, start=1):
            cp.wait_send()
            pltpu.make_async_remote_copy(src_ref=v_ref, dst_ref=out_ref.at[me ^ k], send_sem=send.at[k - 1],
                                         recv_sem=recv.at[k - 1], device_id=(x, y, c), device_id_type=MESH_ID).wait_recv()

    vm = pl.BlockSpec(memory_space=pltpu.VMEM)
    return pl.pallas_call(
        body, name=name, in_specs=[vm], out_specs=vm, out_shape=jax.ShapeDtypeStruct((8, r, 128), F32),
        scratch_shapes=[pltpu.SemaphoreType.DMA((7,)), pltpu.SemaphoreType.DMA((7,))],
        compiler_params=pltpu.CompilerParams(has_side_effects=True),
    )(v)


def _row_tile(r, target):
    best = None
    for t in range(16, min(target, r) + 1, 16):
        if r % t == 0:
            best = t
    return best or r


def _sum_slots(buf, *, name, tr=384):
    S, r, c = buf.shape
    tr = _row_tile(r, tr)

    def body(b_ref, o_ref):
        acc = b_ref[0].astype(F32)
        for j in range(1, S):
            acc = acc + b_ref[j].astype(F32)
        o_ref[...] = acc

    return pl.pallas_call(
        body, name=name, grid=(r // tr,), in_specs=[pl.BlockSpec((S, tr, c), lambda i: (0, i, 0))],
        out_specs=pl.BlockSpec((tr, c), lambda i: (i, 0)), out_shape=jax.ShapeDtypeStruct((r, c), F32),
        compiler_params=_cparams("parallel"),
    )(buf)


ADAMW_BLOCK_ELEMS = 1 << 18


def _adamw(w, gs, m, v, *, name, tr=256, layer=None, stack=None):
    r, c = w.shape[-2:]
    tr = _row_tile(r, min(tr, max(16, ADAMW_BLOCK_ELEMS // c)))
    bc1 = 1.0 - ADAM_B1 ** ADAM_STEP
    bc2 = 1.0 - ADAM_B2 ** ADAM_STEP
    ng = len(gs)

    def body(*refs):
        w_ref, m_ref, v_ref = refs[0], refs[1 + ng], refs[2 + ng]
        g_ref, d_ref, mo_ref, vo_ref = refs[-4:]
        gg = refs[1][...] if ng == 1 else refs[1][...] + refs[2][...]
        mn = ADAM_B1 * m_ref[...] + (1.0 - ADAM_B1) * gg
        vn = ADAM_B2 * v_ref[...] + (1.0 - ADAM_B2) * (gg * gg)
        g_ref[...] = gg
        mo_ref[...] = mn
        vo_ref[...] = vn
        d_ref[...] = -ADAM_LR * ((mn / bc1) / (jnp.sqrt(vn / bc2) + ADAM_EPS) + ADAM_WD * w_ref[...])

    spec = pl.BlockSpec((tr, c), lambda i: (i, 0))
    if layer is None:
        wspec, shape = spec, (r, c)
    else:
        wspec, shape = pl.BlockSpec((None, tr, c), lambda i: (layer, i, 0)), (2, r, c)
    in_specs = [wspec] + [spec] * ng + [wspec, wspec]
    args, alias = [w, *gs, m, v], {}
    if stack is not None:
        in_specs += [_ANY] * 4
        alias = {len(args) + n: n for n in range(4)}
        args += list(stack)
    return pl.pallas_call(body, name=name, grid=(r // tr,), in_specs=in_specs, out_specs=[wspec] * 4,
                          out_shape=[jax.ShapeDtypeStruct(shape, F32)] * 4, input_output_aliases=alias,
                          compiler_params=_cparams("parallel"))(*args)


def _pack(vecs, rows):
    flat = jnp.concatenate([v.reshape(-1).astype(F32) for v in vecs])
    return jnp.pad(flat, (0, rows * 128 - flat.shape[0])).reshape(rows, 128)


def _unpack(packed, shapes):
    flat = packed.reshape(-1)
    out, off = [], 0
    for s in shapes:
        n = math.prod(s)
        out.append(flat[off:off + n].reshape(s))
        off += n
    return out


def _pack_rows(shapes):
    n = sum(math.prod(s) for s in shapes)
    return -(-n // 1024) * 8


def kernel(x, norm_mix_pre, norm_mix_post, norm_ffn_pre, norm_ffn_post, ret_w_in, ret_gn_w, ret_w_out, ssd_w_in, ssd_conv_w, ssd_conv_b, ssd_dt_bias, ssd_a_log, ssd_d, ssd_norm_w, ssd_w_out, mlp_w_up, mlp_w_down, loss_target, m_norm_mix_pre, m_norm_mix_post, m_norm_ffn_pre, m_norm_ffn_post, m_ret_w_in, m_ret_gn_w, m_ret_w_out, m_ssd_w_in, m_ssd_conv_w, m_ssd_conv_b, m_ssd_dt_bias, m_ssd_a_log, m_ssd_d, m_ssd_norm_w, m_ssd_w_out, m_mlp_w_up, m_mlp_w_down, v_norm_mix_pre, v_norm_mix_post, v_norm_ffn_pre, v_norm_ffn_post, v_ret_w_in, v_ret_gn_w, v_ret_w_out, v_ssd_w_in, v_ssd_conv_w, v_ssd_conv_b, v_ssd_dt_bias, v_ssd_a_log, v_ssd_d, v_ssd_norm_w, v_ssd_w_out, v_mlp_w_up, v_mlp_w_down):
    T, D = x.shape[1], x.shape[2]
    H = D // RET_DK
    d_inner = 2 * D
    R = d_inner // SSD_P // SSD_G
    RP = R * SSD_P
    n_heads = SSD_G * R
    conv_dim = d_inner + 2 * SSD_G * SSD_N
    n_main = d_inner + conv_dim
    C = min(256, T)
    C_ret = min(512, T)
    chip = _my_chip()
    xs, tgt = x[0], loss_target[0]

    conv_sh = ssd_conv_w.shape[2]
    small_shapes = [(SSD_CONV_W, conv_sh), (conv_sh,), (ssd_norm_w.shape[1],)]
    small_rows = _pack_rows(small_shapes)
    shards = [ret_w_in[0].T.astype(BF16), ret_w_out[0].astype(BF16), ssd_w_in[0].T.astype(BF16), ssd_w_out[0].astype(BF16),
              mlp_w_up[0].T.astype(BF16), mlp_w_up[1].T.astype(BF16), mlp_w_down[0].astype(BF16), mlp_w_down[1].astype(BF16)]
    (ret_in_g, small_g, u0) = _all_gather_chips(
        [shards[0], _pack([ssd_conv_w[0], ssd_conv_b[0], ssd_norm_w[0]], small_rows)], [True, False], name="gather_first",
        norm=(xs, norm_mix_pre[0:1]))

    def full(g):
        return g.reshape(4 * g.shape[1], g.shape[2])

    def start_gather(idx, after, name):
        sems, zones, tok = _split_start([shards[i] for i in idx], scatter=False, after=after, name=name)
        return {i: (sems, shards[i], zones[n], n) for n, i in enumerate(idx)}, tok

    def arrived(stage, i, after, name):
        sems, src, zone, n = stage[i]
        return full(_split_wait(sems, src, zone, n, scatter=False, after=after, name=name))

    ret_in_t = full(ret_in_g)
    sm = [_unpack(small_g[j], small_shapes) for j in range(4)]
    conv_w = jnp.concatenate([sm[j][0] for j in range(4)], axis=1)
    conv_b = jnp.concatenate([sm[j][1] for j in range(4)])[None, :]
    norm_w = jnp.concatenate([sm[j][2] for j in range(4)])[None, :]

    gb = SSD_G * SSD_N
    ssd_prm = (ssd_dt_bias.reshape(SSD_G, 1, R), ssd_dt_bias.reshape(SSD_G, R, 1),
               ssd_a_log.reshape(SSD_G, 1, R), ssd_a_log.reshape(SSD_G, R, 1), ssd_d.reshape(SSD_G, 1, R),
               conv_w[:, :d_inner], conv_w[:, d_inner:d_inner + gb], conv_w[:, d_inner + gb:],
               conv_b[:, :d_inner], conv_b[:, d_inner:d_inner + gb], conv_b[:, d_inner + gb:],
               norm_w, jnp.asarray(np.kron(np.eye(R), np.ones((1, SSD_P))), F32))
    ret_consts = _ret_consts(T, C_ret, H)

    stage1, tok = start_gather((1, 4), ret_in_g, "gather_start1")
    proj = _matmul(u0, ret_in_t, "nt", out_dtype=F32, name="ret_in", after=tok)
    stage2, tok = start_gather((6, 2), proj, "gather_start2")
    y_ret, st_ret = _ret_fwd(proj, ret_gn_w, ret_consts, C=C_ret, name="ret_fwd")
    ret_out = arrived(stage1, 1, y_ret, "gather_wait_ret_out")
    m0 = _matmul(y_ret, ret_out, "nn", out_dtype=BF16, name="ret_out", after=tok)
    h1, u1 = _rms_post_pre(xs, m0, norm_mix_post[0:1], norm_ffn_pre[0:1], name="post_pre1")
    up_t0 = arrived(stage1, 4, u1, "gather_wait_up0")
    a0, hh0 = _matmul(u1, up_t0, "nt", out_dtype=BF16, name="mlp_up0", epi="relu2")
    stage3, tok = start_gather((3, 5, 7), hh0, "gather_start3")
    down0 = arrived(stage2, 6, hh0, "gather_wait_down0")
    f0 = _matmul(hh0, down0, "nn", out_dtype=BF16, name="mlp_down0", after=tok)
    h2, u2 = _rms_post_pre(h1, f0, norm_ffn_post[0:1], norm_mix_pre[1:2], name="post_pre2")
    ssd_in_t = arrived(stage2, 2, u2, "gather_wait_ssd_in")
    pm, pdt = _matmul(u2, ssd_in_t, "nt", out_dtype=F32, name="ssd_in", tail=(n_heads,))
    dt_g = pdt.reshape(T, SSD_G, R).transpose(1, 0, 2)
    dtT_g = pdt.reshape(T, SSD_G, R).transpose(1, 2, 0)
    y_ssd, st_ssd = _ssd_fwd(pm, dt_g, dtT_g, ssd_prm, C=C, R=R, name="ssd_fwd")
    ssd_out = arrived(stage3, 3, y_ssd, "gather_wait_ssd_out")
    m1 = _matmul(y_ssd, ssd_out, "nn", out_dtype=BF16, name="ssd_out")
    h3, u3 = _rms_post_pre(h2, m1, norm_mix_post[1:2], norm_ffn_pre[1:2], name="post_pre3")
    up_t1 = arrived(stage3, 5, u3, "gather_wait_up1")
    a1, hh1 = _matmul(u3, up_t1, "nt", out_dtype=BF16, name="mlp_up1", epi="relu2")
    down1 = arrived(stage3, 7, hh1, "gather_wait_down1")
    f1 = _matmul(hh1, down1, "nn", out_dtype=BF16, name="mlp_down1")
    up_t, down = (up_t0, up_t1), (down0, down1)
    dh4, sq = _rms_post_loss(h3, f1, norm_ffn_post[1:2], tgt, name="post_loss")
    loss = lax.psum(sq[0, 0], MESH_AXES) * (0.5 / D)

    in_flight = []

    def send_grad(g, name):
        part = g if g.ndim == 3 else g.reshape(4, g.shape[0] // 4, g.shape[1])
        sems, zones, tok = _split_start([part], scatter=True, after=part, name=f"scatter_start_{name}")
        in_flight.append((name, sems, part, zones[0]))
        return tok

    retired = []

    def retire(name, after):
        nm, sems, src, zone = in_flight.pop(0)
        assert nm == name
        part = _sum_slots(_split_wait(sems, src, zone, 0, scatter=True, after=after, name=f"scatter_wait_{nm}"),
                          name=f"sum_chips_{nm}")
        sw_sems, zones = _swap_start([part], after=part, name=f"swap_start_{nm}")
        retired.append((nm, part, sw_sems, zones))
        return zones[0]

    def mlp_bwd(i, df, u, a, hh):
        tok = send_grad(_matmul(hh, df, "tn", out_dtype=BF16, name=f"mlp_down_wg{i}"), f"down{i}")
        da = _matmul(df, down[i], "nt", out_dtype=BF16, name=f"mlp_down_dg{i}", epi="drelu2", extra=a, after=tok)
        tok = send_grad(_matmul(u, da, "tn", out_dtype=BF16, name=f"mlp_up_wg{i}", col_parts=4), f"up{i}")
        return _matmul(da, up_t[i], "nn", out_dtype=BF16, name=f"mlp_up_dg{i}", after=tok)

    df1, d_nfpost1 = _rms_post_bwd(f1, norm_ffn_post[1:2], dh4, name="post_bwd_ffn1")
    du3 = mlp_bwd(1, df1, u3, a1, hh1)
    dh3, dm1, d_nfp1, d_nmpost1 = _rms_pre_post_bwd(h3, norm_ffn_pre[1:2], du3, dh4, m1, norm_mix_post[1:2],
                                                    name="pre_bwd_ffn1_post_bwd_mix1")
    tok = send_grad(_matmul(y_ssd, dm1, "tn", out_dtype=BF16, name="ssd_out_wg"), "ssd_out")
    dy_ssd = _matmul(dm1, ssd_out, "nt", out_dtype=F32, name="ssd_out_dg", after=tok)
    (dpm, ddt_g, d_bias, d_alog, d_dskip, dcwx, dcwb, dcwc, dcbx, dcbb, dcbc, d_normw) = _ssd_bwd(
        pm, dt_g, dtT_g, ssd_prm, st_ssd, dy_ssd, C=C, R=R, name="ssd_bwd")
    dpdt = ddt_g.transpose(1, 0, 2).reshape(T, n_heads).astype(BF16)
    tok = send_grad(jnp.concatenate([_matmul(dpm, u2, "tn", out_dtype=BF16, name="ssd_in_wg"),
                                     _matmul(dpdt, u2, "tn", out_dtype=BF16, name="ssd_in_dt_wg")], axis=0), "ssd_in")
    du2 = _matmul(dpm, ssd_in_t, "nn", out_dtype=BF16, name="ssd_in_dg", after=tok, tail=(n_heads, dpdt))
    dh2, df0, d_nmp1, d_nfpost0 = _rms_pre_post_bwd(h2, norm_mix_pre[1:2], du2, dh3, f0, norm_ffn_post[0:1],
                                                    name="pre_bwd_mix1_post_bwd_ffn0")
    du1 = mlp_bwd(0, df0, u1, a0, hh0)
    dh1, dm0, d_nfp0, d_nmpost0 = _rms_pre_post_bwd(h1, norm_ffn_pre[0:1], du1, dh2, m0, norm_mix_post[0:1],
                                                    name="pre_bwd_ffn0_post_bwd_mix0")
    tok = send_grad(_matmul(y_ret, dm0, "tn", out_dtype=BF16, name="ret_out_wg"), "ret_out")
    dy_ret = _matmul(dm0, ret_out, "nt", out_dtype=F32, name="ret_out_dg", after=tok)
    dproj, d_gn = _ret_bwd(proj, ret_gn_w, ret_consts, st_ret, dy_ret, C=C_ret, name="ret_bwd")
    tok = send_grad(_matmul(u0, dproj, "tn", out_dtype=BF16, name="ret_in_wg", col_parts=4), "ret_in")
    du0 = _matmul(dproj, ret_in_t, "nn", out_dtype=BF16, name="ret_in_dg", after=tok)
    grad_x, d_nmp0 = _rms_pre_bwd(xs, norm_mix_pre[0:1], du0, dh1, name="pre_bwd_mix0")

    prev = grad_x
    for nm in ("down1", "up1", "ssd_out", "ssd_in", "down0", "up0", "ret_out", "ret_in"):
        prev = retire(nm, prev)

    def upd(w, gs, m, v, name):
        shp = w.shape
        w2, m2, v2 = (t.reshape(-1, shp[-1]) for t in (w, m, v))
        return tuple(t.reshape(shp) for t in _adamw(w2, gs, m2, v2, name=name))

    def upd_t(w, gs, m, v, name):
        return tuple(t.T[None] for t in _adamw(w[0].T, gs, m[0].T, v[0].T, name=name))

    def upd_layer(layer):
        def fn(w, gs, m, v, name):
            return tuple(_adamw(w, gs, m, v, name=name, layer=layer, stack=res.get(out_of[name[len("adamw_"):]])))
        return fn

    out_of = {"ret_in": "ret_w_in", "ret_out": "ret_w_out", "ssd_in": "ssd_w_in", "ssd_out": "ssd_w_out",
              "up0": "mlp_w_up", "up1": "mlp_w_up", "down0": "mlp_w_down", "down1": "mlp_w_down"}
    todo = {"ret_in": (upd, ret_w_in, m_ret_w_in, v_ret_w_in), "ret_out": (upd, ret_w_out, m_ret_w_out, v_ret_w_out),
            "ssd_in": (upd_t, ssd_w_in, m_ssd_w_in, v_ssd_w_in), "ssd_out": (upd, ssd_w_out, m_ssd_w_out, v_ssd_w_out),
            "up0": (upd_layer(0), mlp_w_up, m_mlp_w_up, v_mlp_w_up), "up1": (upd_layer(1), mlp_w_up, m_mlp_w_up, v_mlp_w_up),
            "down0": (upd_layer(0), mlp_w_down, m_mlp_w_down, v_mlp_w_down),
            "down1": (upd_layer(1), mlp_w_down, m_mlp_w_down, v_mlp_w_down)}
    res = {}
    for nm, mine, sems, zones in retired:
        other = _swap_wait(sems, [mine], zones, after=prev, name=f"swap_wait_{nm}")[0]
        fn, w, m, v = todo[nm]
        res[out_of[nm]] = fn(w, [mine, other], m, v, f"adamw_{nm}")
        prev = res[out_of[nm]][0]

    d_conv_w = jnp.concatenate([dcwx, dcwb, dcwc], axis=1)
    d_conv_b = jnp.concatenate([dcbx, dcbb, dcbc], axis=1)
    small_grads = [jnp.concatenate([d_nmp0, d_nmp1]), jnp.concatenate([d_nmpost0, d_nmpost1]),
                   jnp.concatenate([d_nfp0, d_nfp1]), jnp.concatenate([d_nfpost0, d_nfpost1]),
                   d_gn, d_bias.reshape(1, n_heads), d_alog.reshape(1, n_heads), d_dskip.reshape(1, n_heads),
                   d_conv_w, d_conv_b, d_normw]
    sg_shapes = [g.shape for g in small_grads]
    sg_rows = _pack_rows(sg_shapes)
    everyone = _all_gather_devices(_pack(small_grads, sg_rows), name="gather_small_grads")
    sg = _unpack(_sum_slots(everyone, name="sum_small_grads", tr=sg_rows), sg_shapes)
    (g_nmp, g_nmpost, g_nfp, g_nfpost, g_gn, g_bias, g_alog, g_dskip, g_cw_full, g_cb_full, g_nw_full) = sg
    g_cw = lax.dynamic_slice_in_dim(g_cw_full, chip * conv_sh, conv_sh, axis=1)[None]
    g_cb = lax.dynamic_slice_in_dim(g_cb_full, chip * conv_sh, conv_sh, axis=1)
    nw_sh = ssd_norm_w.shape[1]
    g_nw = lax.dynamic_slice_in_dim(g_nw_full, chip * nw_sh, nw_sh, axis=1)
    small = [("norm_mix_pre", norm_mix_pre, g_nmp, m_norm_mix_pre, v_norm_mix_pre),
             ("norm_mix_post", norm_mix_post, g_nmpost, m_norm_mix_post, v_norm_mix_post),
             ("norm_ffn_pre", norm_ffn_pre, g_nfp, m_norm_ffn_pre, v_norm_ffn_pre),
             ("norm_ffn_post", norm_ffn_post, g_nfpost, m_norm_ffn_post, v_norm_ffn_post),
             ("ret_gn_w", ret_gn_w, g_gn, m_ret_gn_w, v_ret_gn_w),
             ("ssd_conv_w", ssd_conv_w, g_cw, m_ssd_conv_w, v_ssd_conv_w),
             ("ssd_conv_b", ssd_conv_b, g_cb, m_ssd_conv_b, v_ssd_conv_b),
             ("ssd_dt_bias", ssd_dt_bias, g_bias, m_ssd_dt_bias, v_ssd_dt_bias),
             ("ssd_a_log", ssd_a_log, g_alog, m_ssd_a_log, v_ssd_a_log),
             ("ssd_d", ssd_d, g_dskip, m_ssd_d, v_ssd_d),
             ("ssd_norm_w", ssd_norm_w, g_nw, m_ssd_norm_w, v_ssd_norm_w)]
    sw_shapes = [w.shape for _, w, _, _, _ in small]
    sw_rows = _pack_rows(sw_shapes)
    packs = [_pack([t[j] for t in small], sw_rows) for j in (1, 2, 3, 4)]
    _, d_p, m_p, v_p = _adamw(packs[0], [packs[1]], packs[2], packs[3], name="adamw_small", tr=sw_rows)
    d_s, m_s, v_s = _unpack(d_p, sw_shapes), _unpack(m_p, sw_shapes), _unpack(v_p, sw_shapes)
    for j, (nm, w, g, _, _) in enumerate(small):
        res[nm] = (g.reshape(w.shape), d_s[j], m_s[j], v_s[j])

    order = ["norm_mix_pre", "norm_mix_post", "norm_ffn_pre", "norm_ffn_post", "ret_w_in", "ret_gn_w", "ret_w_out",
             "ssd_w_in", "ssd_conv_w", "ssd_conv_b", "ssd_dt_bias", "ssd_a_log", "ssd_d", "ssd_norm_w", "ssd_w_out",
             "mlp_w_up", "mlp_w_down"]
    return (loss, grad_x[None], *[res[n][0] for n in order], *[res[n][1] for n in order],
            *[res[n][2] for n in order], *[res[n][3] for n in order])
```
